```python
import math
import jax, jax.numpy as jnp
from jax import lax
import numpy as np

D_MODEL = 2048
BATCH = 8
SEQ = 4096
DEPTH = 1

S5_WIDTH = D_MODEL // 2
S5_GROUP = 16
S5_GROUPS = S5_WIDTH // S5_GROUP
S5_STATE = 64
DT_MIN = 1e-3
DT_MAX = 1e-1
S5_MAX_RE = -1e-4
HGRN_WIDTH = D_MODEL // 2
HGRN_EXPAND = 128
HGRN_HEADS = HGRN_WIDTH // HGRN_EXPAND
HGRN_HEAD_DIM = HGRN_EXPAND
HGRN_CHUNK = 64
D_FF = 5632
CONV_WIDTH = 3
RMS_EPS = 1e-6
N_IN = S5_WIDTH + 4 * HGRN_WIDTH + 2 * D_MODEL

kernel_name = 'hybrid_s5_hgrn2_gated_merge_block'


def _rmsnorm(x, g):
    xf = x.astype(jnp.float32)
    y = xf * lax.rsqrt(jnp.mean(xf * xf, axis=-1, keepdims=True) + RMS_EPS)
    return (y * g.astype(jnp.float32)).astype(x.dtype)


def _in_splits():
    sizes = [S5_WIDTH, HGRN_WIDTH, HGRN_WIDTH, HGRN_WIDTH, HGRN_WIDTH, D_MODEL, D_MODEL]
    offs, acc = [], 0
    for s in sizes[:-1]:
        acc += s
        offs.append(acc)
    return offs


def _complex_affine_combine(first, second):
    a1r, a1i, b1r, b1i = first
    a2r, a2i, b2r, b2i = second
    return (a2r * a1r - a2i * a1i,
            a2r * a1i + a2i * a1r,
            a2r * b1r - a2i * b1i + b2r,
            a2r * b1i + a2i * b1r + b2i)


def _s5_branch(u, a_re, a_im, log_dt, b_re, b_im, c_re, c_im, d, w_glu, b_glu):
    bsz, L, _ = u.shape
    f32 = jnp.float32
    uf = u.astype(f32).reshape(bsz, L, S5_GROUPS, S5_GROUP)
    lam_re = jnp.minimum(a_re.astype(f32), S5_MAX_RE)
    lam_im = a_im.astype(f32)
    dt = jnp.exp(log_dt.astype(f32))[:, None]
    mag = jnp.exp(lam_re * dt)
    abar_re = mag * jnp.cos(lam_im * dt)
    abar_im = mag * jnp.sin(lam_im * dt)
    den = lam_re * lam_re + lam_im * lam_im
    nr = abar_re - 1.0
    ni = abar_im
    coef_re = (nr * lam_re + ni * lam_im) / den
    coef_im = (ni * lam_re - nr * lam_im) / den
    bu_re = jnp.einsum('blgc,gpc->blgp', uf, b_re.astype(f32))
    bu_im = jnp.einsum('blgc,gpc->blgp', uf, b_im.astype(f32))
    bb_re = coef_re * bu_re - coef_im * bu_im
    bb_im = coef_re * bu_im + coef_im * bu_re
    a_seq_re = jnp.broadcast_to(abar_re, (1, L, S5_GROUPS, S5_STATE))
    a_seq_im = jnp.broadcast_to(abar_im, (1, L, S5_GROUPS, S5_STATE))
    _, _, s_re, s_im = lax.associative_scan(
        _complex_affine_combine, (a_seq_re, a_seq_im, bb_re, bb_im), axis=1)
    y = (jnp.einsum('blgp,gcp->blgc', s_re, c_re.astype(f32))
         - jnp.einsum('blgp,gcp->blgc', s_im, c_im.astype(f32))
         + d.astype(f32) * uf)
    y = y.reshape(bsz, L, S5_WIDTH)
    z = jax.nn.gelu(y)
    z = z * jax.nn.sigmoid(z @ w_glu.astype(f32) + b_glu.astype(f32))
    return z.astype(u.dtype)


def _hgrn2_branch(q_in, f_in, i_in, g_in, lb, norm_g):
    bsz, L, _ = q_in.shape
    f32 = jnp.float32
    n_chunks = L // HGRN_CHUNK

    def heads(t):
        return t.reshape(bsz, n_chunks, HGRN_CHUNK, HGRN_HEADS, HGRN_HEAD_DIM).transpose(1, 0, 3, 2, 4)

    q = jax.nn.silu(q_in.astype(f32))
    zf = f_in.astype(f32)
    lbf = lb.astype(f32)
    log_f = jnp.logaddexp(jnp.log(lbf), jnp.log1p(-lbf) + jax.nn.log_sigmoid(zf))
    k = (1.0 - lbf) * jax.nn.sigmoid(-zf)
    v = i_in.astype(f32)
    causal = jnp.tril(jnp.ones((HGRN_CHUNK, HGRN_CHUNK), dtype=bool))

    def step(S, blk):
        qc, lfc, kc, vc = blk
        b = jnp.cumsum(lfc, axis=2)
        b_last = b[:, :, -1:, :]
        inter = jnp.einsum('bhtk,bhkv->bhtv', qc * jnp.exp(b), S)
        diff = b[:, :, :, None, :] - b[:, :, None, :, :]
        decay = jnp.exp(jnp.where(causal[None, None, :, :, None], diff, -jnp.inf))
        scores = jnp.einsum('bhtk,bhsk,bhtsk->bhts', qc, kc, decay)
        intra = jnp.einsum('bhts,bhsv->bhtv', scores, vc)
        S_new = (jnp.exp(b_last[:, :, 0, :])[..., None] * S
                 + jnp.einsum('bhsk,bhsv->bhkv', kc * jnp.exp(b_last - b), vc))
        return S_new, inter + intra

    S0 = jnp.zeros((bsz, HGRN_HEADS, HGRN_HEAD_DIM, HGRN_HEAD_DIM), f32)
    _, o = lax.scan(step, S0, (heads(q), heads(log_f), heads(k), heads(v)))
    o = o.transpose(1, 0, 3, 2, 4).reshape(bsz, L, HGRN_HEADS, HGRN_HEAD_DIM)
    o = o * lax.rsqrt(jnp.mean(o * o, axis=-1, keepdims=True) + RMS_EPS)
    o = o * norm_g.astype(f32).reshape(HGRN_HEADS, HGRN_HEAD_DIM)
    o = o.reshape(bsz, L, HGRN_WIDTH) * jax.nn.silu(g_in.astype(f32))
    return o.astype(q_in.dtype)


def _conv_glu_ffn(h, w_up, conv_w, conv_b, w_down):
    L = h.shape[1]
    up = h @ w_up
    up_pad = jnp.pad(up, ((0, 0), (CONV_WIDTH - 1, 0), (0, 0)))
    conv = conv_b
    for j in range(CONV_WIDTH):
        conv = conv + conv_w[j] * up_pad[:, j:j + L, :]
    gate, val = jnp.split(conv, 2, axis=-1)
    return (jax.nn.silu(gate) * val) @ w_down


def _fwd_setup_inputs(seed: int = 0) -> dict:
    key = jax.random.key(seed)
    ks = jax.random.split(key, 24)
    f32 = jnp.float32

    def nrm(k, shape, scale):
        return jax.random.normal(k, shape, f32) * scale

    G, P = S5_GROUPS, S5_STATE
    return {
        'x': nrm(ks[0], (BATCH, SEQ, D_MODEL), 1.0),
        'ln_mix_g': 1.0 + nrm(ks[1], (DEPTH, D_MODEL), 0.01),
        'w_in': nrm(ks[2], (DEPTH, D_MODEL, N_IN), D_MODEL ** -0.5),
        's5_a_re': -0.5 + nrm(ks[3], (DEPTH, G, P), 0.01),
        's5_a_im': math.pi * jnp.arange(P, dtype=f32) + nrm(ks[4], (DEPTH, G, P), 0.01),
        's5_log_dt': jax.random.uniform(ks[5], (DEPTH, G), f32, math.log(DT_MIN), math.log(DT_MAX)),
        's5_b_re': nrm(ks[6], (DEPTH, G, P, S5_GROUP), (2 * S5_GROUP) ** -0.5),
        's5_b_im': nrm(ks[7], (DEPTH, G, P, S5_GROUP), (2 * S5_GROUP) ** -0.5),
        's5_c_re': nrm(ks[8], (DEPTH, G, S5_GROUP, P), S5_STATE ** -0.5),
        's5_c_im': nrm(ks[9], (DEPTH, G, S5_GROUP, P), S5_STATE ** -0.5),
        's5_d': nrm(ks[10], (DEPTH, G, S5_GROUP), 1.0),
        's5_w_glu': nrm(ks[11], (DEPTH, S5_WIDTH, S5_WIDTH), S5_WIDTH ** -0.5),
        's5_b_glu': nrm(ks[12], (DEPTH, S5_WIDTH), 0.01),
        'w_proj_s5': nrm(ks[13], (DEPTH, S5_WIDTH, D_MODEL), S5_WIDTH ** -0.5),
        'hgrn_lb_logits': nrm(ks[14], (DEPTH + 1, HGRN_WIDTH), 0.1),
        'hgrn_norm_g': 1.0 + nrm(ks[15], (DEPTH, HGRN_WIDTH), 0.01),
        'w_proj_hgrn': nrm(ks[16], (DEPTH, HGRN_WIDTH, D_MODEL), HGRN_WIDTH ** -0.5),
        'w_out': nrm(ks[17], (DEPTH, D_MODEL, D_MODEL), D_MODEL ** -0.5),
        'ln_ffn_g': 1.0 + nrm(ks[18], (DEPTH, D_MODEL), 0.01),
        'w_up': nrm(ks[19], (DEPTH, D_MODEL, 2 * D_FF), D_MODEL ** -0.5),
        'conv_w': nrm(ks[20], (DEPTH, CONV_WIDTH, 2 * D_FF), CONV_WIDTH ** -0.5),
        'conv_b': nrm(ks[21], (DEPTH, 2 * D_FF), 0.01),
        'w_down': nrm(ks[22], (DEPTH, D_FF, D_MODEL), D_FF ** -0.5),
        'ln_final_g': 1.0 + nrm(ks[23], (D_MODEL,), 0.01),
    }


def _fwd_reference(x, ln_mix_g, w_in, s5_a_re, s5_a_im, s5_log_dt, s5_b_re, s5_b_im, s5_c_re, s5_c_im,
              s5_d, s5_w_glu, s5_b_glu, w_proj_s5, hgrn_lb_logits, hgrn_norm_g, w_proj_hgrn,
              w_out, ln_ffn_g, w_up, conv_w, conv_b, w_down, ln_final_g):
    lb_all = jnp.cumsum(jax.nn.softmax(hgrn_lb_logits.astype(jnp.float32), axis=0), axis=0)
    for l in range(DEPTH):
        h = _rmsnorm(x, ln_mix_g[l])
        proj = h @ w_in[l]
        u_s5, q_h, f_h, i_h, g_h, gate_s5, gate_hgrn = jnp.split(proj, _in_splits(), axis=-1)
        y_s5 = _s5_branch(u_s5, s5_a_re[l], s5_a_im[l], s5_log_dt[l], s5_b_re[l], s5_b_im[l],
                          s5_c_re[l], s5_c_im[l], s5_d[l], s5_w_glu[l], s5_b_glu[l]) @ w_proj_s5[l]
        y_hgrn = _hgrn2_branch(q_h, f_h, i_h, g_h, lb_all[l], hgrn_norm_g[l]) @ w_proj_hgrn[l]
        merged = jax.nn.sigmoid(gate_s5) * y_s5 + jax.nn.sigmoid(gate_hgrn) * y_hgrn
        x = x + merged @ w_out[l]
        x = x + _conv_glu_ffn(_rmsnorm(x, ln_ffn_g[l]), w_up[l], conv_w[l], conv_b[l], w_down[l])
    return _rmsnorm(x, ln_final_g)


import jax as _jax
import jax.numpy as _jnp

TWIN_FORMAT = 'train_step'
FWD_PARAMS = ['x', 'ln_mix_g', 'w_in', 's5_a_re', 's5_a_im', 's5_log_dt', 's5_b_re', 's5_b_im', 's5_c_re', 's5_c_im', 's5_d', 's5_w_glu', 's5_b_glu', 'w_proj_s5', 'hgrn_lb_logits', 'hgrn_norm_g', 'w_proj_hgrn', 'w_out', 'ln_ffn_g', 'w_up', 'conv_w', 'conv_b', 'w_down', 'ln_final_g']
TWIN_WEIGHTS = ['ln_mix_g', 'w_in', 's5_a_re', 's5_a_im', 's5_log_dt', 's5_b_re', 's5_b_im', 's5_c_re', 's5_c_im', 's5_d', 's5_w_glu', 's5_b_glu', 'w_proj_s5', 'hgrn_lb_logits', 'hgrn_norm_g', 'w_proj_hgrn', 'w_out', 'ln_ffn_g', 'w_up', 'conv_w', 'conv_b', 'w_down', 'ln_final_g']
TWIN_DIFF_INPUT = 'x'
TWIN_INPUTS = ['x', 'ln_mix_g', 'w_in', 's5_a_re', 's5_a_im', 's5_log_dt', 's5_b_re', 's5_b_im', 's5_c_re', 's5_c_im', 's5_d', 's5_w_glu', 's5_b_glu', 'w_proj_s5', 'hgrn_lb_logits', 'hgrn_norm_g', 'w_proj_hgrn', 'w_out', 'ln_ffn_g', 'w_up', 'conv_w', 'conv_b', 'w_down', 'ln_final_g', 'loss_target', 'm_ln_mix_g', 'm_w_in', 'm_s5_a_re', 'm_s5_a_im', 'm_s5_log_dt', 'm_s5_b_re', 'm_s5_b_im', 'm_s5_c_re', 'm_s5_c_im', 'm_s5_d', 'm_s5_w_glu', 'm_s5_b_glu', 'm_w_proj_s5', 'm_hgrn_lb_logits', 'm_hgrn_norm_g', 'm_w_proj_hgrn', 'm_w_out', 'm_ln_ffn_g', 'm_w_up', 'm_conv_w', 'm_conv_b', 'm_w_down', 'm_ln_final_g', 'v_ln_mix_g', 'v_w_in', 'v_s5_a_re', 'v_s5_a_im', 'v_s5_log_dt', 'v_s5_b_re', 'v_s5_b_im', 'v_s5_c_re', 'v_s5_c_im', 'v_s5_d', 'v_s5_w_glu', 'v_s5_b_glu', 'v_w_proj_s5', 'v_hgrn_lb_logits', 'v_hgrn_norm_g', 'v_w_proj_hgrn', 'v_w_out', 'v_ln_ffn_g', 'v_w_up', 'v_conv_w', 'v_conv_b', 'v_w_down', 'v_ln_final_g']
TWIN_OUTPUTS = ['loss', 'grad_x', 'grad_ln_mix_g', 'grad_w_in', 'grad_s5_a_re', 'grad_s5_a_im', 'grad_s5_log_dt', 'grad_s5_b_re', 'grad_s5_b_im', 'grad_s5_c_re', 'grad_s5_c_im', 'grad_s5_d', 'grad_s5_w_glu', 'grad_s5_b_glu', 'grad_w_proj_s5', 'grad_hgrn_lb_logits', 'grad_hgrn_norm_g', 'grad_w_proj_hgrn', 'grad_w_out', 'grad_ln_ffn_g', 'grad_w_up', 'grad_conv_w', 'grad_conv_b', 'grad_w_down', 'grad_ln_final_g', 'delta_ln_mix_g', 'delta_w_in', 'delta_s5_a_re', 'delta_s5_a_im', 'delta_s5_log_dt', 'delta_s5_b_re', 'delta_s5_b_im', 'delta_s5_c_re', 'delta_s5_c_im', 'delta_s5_d', 'delta_s5_w_glu', 'delta_s5_b_glu', 'delta_w_proj_s5', 'delta_hgrn_lb_logits', 'delta_hgrn_norm_g', 'delta_w_proj_hgrn', 'delta_w_out', 'delta_ln_ffn_g', 'delta_w_up', 'delta_conv_w', 'delta_conv_b', 'delta_w_down', 'delta_ln_final_g', 'new_m_ln_mix_g', 'new_m_w_in', 'new_m_s5_a_re', 'new_m_s5_a_im', 'new_m_s5_log_dt', 'new_m_s5_b_re', 'new_m_s5_b_im', 'new_m_s5_c_re', 'new_m_s5_c_im', 'new_m_s5_d', 'new_m_s5_w_glu', 'new_m_s5_b_glu', 'new_m_w_proj_s5', 'new_m_hgrn_lb_logits', 'new_m_hgrn_norm_g', 'new_m_w_proj_hgrn', 'new_m_w_out', 'new_m_ln_ffn_g', 'new_m_w_up', 'new_m_conv_w', 'new_m_conv_b', 'new_m_w_down', 'new_m_ln_final_g', 'new_v_ln_mix_g', 'new_v_w_in', 'new_v_s5_a_re', 'new_v_s5_a_im', 'new_v_s5_log_dt', 'new_v_s5_b_re', 'new_v_s5_b_im', 'new_v_s5_c_re', 'new_v_s5_c_im', 'new_v_s5_d', 'new_v_s5_w_glu', 'new_v_s5_b_glu', 'new_v_w_proj_s5', 'new_v_hgrn_lb_logits', 'new_v_hgrn_norm_g', 'new_v_w_proj_hgrn', 'new_v_w_out', 'new_v_ln_ffn_g', 'new_v_w_up', 'new_v_conv_w', 'new_v_conv_b', 'new_v_w_down', 'new_v_ln_final_g']
TWIN_LEAF_KINDS = {'loss': 'loss', 'grad_x': 'grad_x', 'grad_ln_mix_g': 'grad_w', 'grad_w_in': 'grad_w', 'grad_s5_a_re': 'grad_w', 'grad_s5_a_im': 'grad_w', 'grad_s5_log_dt': 'grad_w', 'grad_s5_b_re': 'grad_w', 'grad_s5_b_im': 'grad_w', 'grad_s5_c_re': 'grad_w', 'grad_s5_c_im': 'grad_w', 'grad_s5_d': 'grad_w', 'grad_s5_w_glu': 'grad_w', 'grad_s5_b_glu': 'grad_w', 'grad_w_proj_s5': 'grad_w', 'grad_hgrn_lb_logits': 'grad_w', 'grad_hgrn_norm_g': 'grad_w', 'grad_w_proj_hgrn': 'grad_w', 'grad_w_out': 'grad_w', 'grad_ln_ffn_g': 'grad_w', 'grad_w_up': 'grad_w', 'grad_conv_w': 'grad_w', 'grad_conv_b': 'grad_w', 'grad_w_down': 'grad_w', 'grad_ln_final_g': 'grad_w', 'delta_ln_mix_g': 'delta_w', 'delta_w_in': 'delta_w', 'delta_s5_a_re': 'delta_w', 'delta_s5_a_im': 'delta_w', 'delta_s5_log_dt': 'delta_w', 'delta_s5_b_re': 'delta_w', 'delta_s5_b_im': 'delta_w', 'delta_s5_c_re': 'delta_w', 'delta_s5_c_im': 'delta_w', 'delta_s5_d': 'delta_w', 'delta_s5_w_glu': 'delta_w', 'delta_s5_b_glu': 'delta_w', 'delta_w_proj_s5': 'delta_w', 'delta_hgrn_lb_logits': 'delta_w', 'delta_hgrn_norm_g': 'delta_w', 'delta_w_proj_hgrn': 'delta_w', 'delta_w_out': 'delta_w', 'delta_ln_ffn_g': 'delta_w', 'delta_w_up': 'delta_w', 'delta_conv_w': 'delta_w', 'delta_conv_b': 'delta_w', 'delta_w_down': 'delta_w', 'delta_ln_final_g': 'delta_w', 'new_m_ln_mix_g': 'new_m', 'new_m_w_in': 'new_m', 'new_m_s5_a_re': 'new_m', 'new_m_s5_a_im': 'new_m', 'new_m_s5_log_dt': 'new_m', 'new_m_s5_b_re': 'new_m', 'new_m_s5_b_im': 'new_m', 'new_m_s5_c_re': 'new_m', 'new_m_s5_c_im': 'new_m', 'new_m_s5_d': 'new_m', 'new_m_s5_w_glu': 'new_m', 'new_m_s5_b_glu': 'new_m', 'new_m_w_proj_s5': 'new_m', 'new_m_hgrn_lb_logits': 'new_m', 'new_m_hgrn_norm_g': 'new_m', 'new_m_w_proj_hgrn': 'new_m', 'new_m_w_out': 'new_m', 'new_m_ln_ffn_g': 'new_m', 'new_m_w_up': 'new_m', 'new_m_conv_w': 'new_m', 'new_m_conv_b': 'new_m', 'new_m_w_down': 'new_m', 'new_m_ln_final_g': 'new_m', 'new_v_ln_mix_g': 'new_v', 'new_v_w_in': 'new_v', 'new_v_s5_a_re': 'new_v', 'new_v_s5_a_im': 'new_v', 'new_v_s5_log_dt': 'new_v', 'new_v_s5_b_re': 'new_v', 'new_v_s5_b_im': 'new_v', 'new_v_s5_c_re': 'new_v', 'new_v_s5_c_im': 'new_v', 'new_v_s5_d': 'new_v', 'new_v_s5_w_glu': 'new_v', 'new_v_s5_b_glu': 'new_v', 'new_v_w_proj_s5': 'new_v', 'new_v_hgrn_lb_logits': 'new_v', 'new_v_hgrn_norm_g': 'new_v', 'new_v_w_proj_hgrn': 'new_v', 'new_v_w_out': 'new_v', 'new_v_ln_ffn_g': 'new_v', 'new_v_w_up': 'new_v', 'new_v_conv_w': 'new_v', 'new_v_conv_b': 'new_v', 'new_v_w_down': 'new_v', 'new_v_ln_final_g': 'new_v'}


def _forward(args):
    return _fwd_reference(*[args[k] for k in FWD_PARAMS])


def _output_shape():
    def fwd():
        inp = _fwd_setup_inputs(0)
        return _fwd_reference(*[inp[k] for k in FWD_PARAMS])
    out = _jax.eval_shape(fwd)
    return out.shape, out.dtype

N_MICROBATCH = 1
ADAM_LR = 0.001
ADAM_B1 = 0.9
ADAM_B2 = 0.999
ADAM_EPS = 1e-08
ADAM_WD = 0.01
ADAM_STEP = 10
PER_EXAMPLE_BATCH_AXIS = {'x': 0, 'loss_target': 0}
SHARED_INPUTS = []
_WEIGHT_DTYPES = {'ln_mix_g': _jnp.float32, 'w_in': _jnp.float32, 's5_a_re': _jnp.float32, 's5_a_im': _jnp.float32, 's5_log_dt': _jnp.float32, 's5_b_re': _jnp.float32, 's5_b_im': _jnp.float32, 's5_c_re': _jnp.float32, 's5_c_im': _jnp.float32, 's5_d': _jnp.float32, 's5_w_glu': _jnp.float32, 's5_b_glu': _jnp.float32, 'w_proj_s5': _jnp.float32, 'hgrn_lb_logits': _jnp.float32, 'hgrn_norm_g': _jnp.float32, 'w_proj_hgrn': _jnp.float32, 'w_out': _jnp.float32, 'ln_ffn_g': _jnp.float32, 'w_up': _jnp.float32, 'conv_w': _jnp.float32, 'conv_b': _jnp.float32, 'w_down': _jnp.float32, 'ln_final_g': _jnp.float32}
MOMENT_SCALE = {'ln_mix_g': 4.963729e-02, 'w_in': 2.339816e-02, 's5_a_re': 1.923701e-03, 's5_a_im': 1.992774e-03, 's5_log_dt': 1.297431e+00, 's5_b_re': 1.265154e-03, 's5_b_im': 1.266823e-03, 's5_c_re': 1.774025e-03, 's5_c_im': 1.792189e-03, 's5_d': 2.912608e-02, 's5_w_glu': 7.538013e-03, 's5_b_glu': 1.152464e-02, 'w_proj_s5': 1.763865e-02, 'hgrn_lb_logits': 3.915960e-03, 'hgrn_norm_g': 4.276771e-02, 'w_proj_hgrn': 3.007825e-02, 'w_out': 3.493326e-02, 'ln_ffn_g': 6.409234e-02, 'w_up': 2.709341e-02, 'conv_w': 2.737385e-02, 'conv_b': 2.610307e-02, 'w_down': 4.413927e-02, 'ln_final_g': 1.598668e+01}


def _to_microbatches(a, axis):
    t = _jnp.moveaxis(a, axis, 0)
    t = t.reshape((N_MICROBATCH, t.shape[0] // N_MICROBATCH) + t.shape[1:])
    return _jnp.moveaxis(t, 1, axis + 1)


def setup_inputs(seed: int = 0) -> dict:
    inp = _fwd_setup_inputs(seed)
    key = _jax.random.fold_in(_jax.random.key(seed), 7919)
    shape, _ = _output_shape()
    out = dict(inp)
    out["loss_target"] = _jax.random.normal(_jax.random.fold_in(key, 0), shape, _jnp.float32)
    for i, name in enumerate(TWIN_WEIGHTS):
        w = inp[name].astype(_jnp.float32)
        if MOMENT_SCALE is None:
            s = _jnp.sqrt(_jnp.mean(_jnp.square(w)) + 1e-30)
        else:
            s = MOMENT_SCALE[name]
        km, kv = _jax.random.split(_jax.random.fold_in(key, i + 1))
        out[name] = w
        out["m_" + name] = s * _jax.random.normal(km, w.shape, _jnp.float32)
        out["v_" + name] = (s * s) * _jax.random.uniform(kv, w.shape, _jnp.float32, 0.5, 1.5)
    if N_MICROBATCH > 1:
        for name, axis in PER_EXAMPLE_BATCH_AXIS.items():
            out[name] = _to_microbatches(out[name], axis)
    return {'x': out['x'], 'ln_mix_g': out['ln_mix_g'], 'w_in': out['w_in'], 's5_a_re': out['s5_a_re'], 's5_a_im': out['s5_a_im'], 's5_log_dt': out['s5_log_dt'], 's5_b_re': out['s5_b_re'], 's5_b_im': out['s5_b_im'], 's5_c_re': out['s5_c_re'], 's5_c_im': out['s5_c_im'], 's5_d': out['s5_d'], 's5_w_glu': out['s5_w_glu'], 's5_b_glu': out['s5_b_glu'], 'w_proj_s5': out['w_proj_s5'], 'hgrn_lb_logits': out['hgrn_lb_logits'], 'hgrn_norm_g': out['hgrn_norm_g'], 'w_proj_hgrn': out['w_proj_hgrn'], 'w_out': out['w_out'], 'ln_ffn_g': out['ln_ffn_g'], 'w_up': out['w_up'], 'conv_w': out['conv_w'], 'conv_b': out['conv_b'], 'w_down': out['w_down'], 'ln_final_g': out['ln_final_g'], 'loss_target': out['loss_target'], 'm_ln_mix_g': out['m_ln_mix_g'], 'm_w_in': out['m_w_in'], 'm_s5_a_re': out['m_s5_a_re'], 'm_s5_a_im': out['m_s5_a_im'], 'm_s5_log_dt': out['m_s5_log_dt'], 'm_s5_b_re': out['m_s5_b_re'], 'm_s5_b_im': out['m_s5_b_im'], 'm_s5_c_re': out['m_s5_c_re'], 'm_s5_c_im': out['m_s5_c_im'], 'm_s5_d': out['m_s5_d'], 'm_s5_w_glu': out['m_s5_w_glu'], 'm_s5_b_glu': out['m_s5_b_glu'], 'm_w_proj_s5': out['m_w_proj_s5'], 'm_hgrn_lb_logits': out['m_hgrn_lb_logits'], 'm_hgrn_norm_g': out['m_hgrn_norm_g'], 'm_w_proj_hgrn': out['m_w_proj_hgrn'], 'm_w_out': out['m_w_out'], 'm_ln_ffn_g': out['m_ln_ffn_g'], 'm_w_up': out['m_w_up'], 'm_conv_w': out['m_conv_w'], 'm_conv_b': out['m_conv_b'], 'm_w_down': out['m_w_down'], 'm_ln_final_g': out['m_ln_final_g'], 'v_ln_mix_g': out['v_ln_mix_g'], 'v_w_in': out['v_w_in'], 'v_s5_a_re': out['v_s5_a_re'], 'v_s5_a_im': out['v_s5_a_im'], 'v_s5_log_dt': out['v_s5_log_dt'], 'v_s5_b_re': out['v_s5_b_re'], 'v_s5_b_im': out['v_s5_b_im'], 'v_s5_c_re': out['v_s5_c_re'], 'v_s5_c_im': out['v_s5_c_im'], 'v_s5_d': out['v_s5_d'], 'v_s5_w_glu': out['v_s5_w_glu'], 'v_s5_b_glu': out['v_s5_b_glu'], 'v_w_proj_s5': out['v_w_proj_s5'], 'v_hgrn_lb_logits': out['v_hgrn_lb_logits'], 'v_hgrn_norm_g': out['v_hgrn_norm_g'], 'v_w_proj_hgrn': out['v_w_proj_hgrn'], 'v_w_out': out['v_w_out'], 'v_ln_ffn_g': out['v_ln_ffn_g'], 'v_w_up': out['v_w_up'], 'v_conv_w': out['v_conv_w'], 'v_conv_b': out['v_conv_b'], 'v_w_down': out['v_w_down'], 'v_ln_final_g': out['v_ln_final_g']}


def _loss(weights, diff, rest, loss_target):
    with _jax.named_scope("forward"):
        args = {**rest, TWIN_DIFF_INPUT: diff, **{k: w.astype(_WEIGHT_DTYPES[k]) for k, w in weights.items()}}
        y = _forward(args)
    with _jax.named_scope("loss_head"):
        err = _jnp.square(y.astype(_jnp.float32) - loss_target)
        return 0.5 * _jnp.sum(_jnp.mean(err, axis=-1)) if err.ndim else 0.5 * err


def _adamw(w, g, m, v):
    m = ADAM_B1 * m + (1.0 - ADAM_B1) * g
    v = ADAM_B2 * v + (1.0 - ADAM_B2) * _jnp.square(g)
    m_hat = m / (1.0 - ADAM_B1 ** ADAM_STEP)
    v_hat = v / (1.0 - ADAM_B2 ** ADAM_STEP)
    delta = -ADAM_LR * (m_hat / (_jnp.sqrt(v_hat) + ADAM_EPS) + ADAM_WD * w)
    return delta, m, v


def reference(x, ln_mix_g, w_in, s5_a_re, s5_a_im, s5_log_dt, s5_b_re, s5_b_im, s5_c_re, s5_c_im, s5_d, s5_w_glu, s5_b_glu, w_proj_s5, hgrn_lb_logits, hgrn_norm_g, w_proj_hgrn, w_out, ln_ffn_g, w_up, conv_w, conv_b, w_down, ln_final_g, loss_target, m_ln_mix_g, m_w_in, m_s5_a_re, m_s5_a_im, m_s5_log_dt, m_s5_b_re, m_s5_b_im, m_s5_c_re, m_s5_c_im, m_s5_d, m_s5_w_glu, m_s5_b_glu, m_w_proj_s5, m_hgrn_lb_logits, m_hgrn_norm_g, m_w_proj_hgrn, m_w_out, m_ln_ffn_g, m_w_up, m_conv_w, m_conv_b, m_w_down, m_ln_final_g, v_ln_mix_g, v_w_in, v_s5_a_re, v_s5_a_im, v_s5_log_dt, v_s5_b_re, v_s5_b_im, v_s5_c_re, v_s5_c_im, v_s5_d, v_s5_w_glu, v_s5_b_glu, v_w_proj_s5, v_hgrn_lb_logits, v_hgrn_norm_g, v_w_proj_hgrn, v_w_out, v_ln_ffn_g, v_w_up, v_conv_w, v_conv_b, v_w_down, v_ln_final_g):
    given = dict(x=x, ln_mix_g=ln_mix_g, w_in=w_in, s5_a_re=s5_a_re, s5_a_im=s5_a_im, s5_log_dt=s5_log_dt, s5_b_re=s5_b_re, s5_b_im=s5_b_im, s5_c_re=s5_c_re, s5_c_im=s5_c_im, s5_d=s5_d, s5_w_glu=s5_w_glu, s5_b_glu=s5_b_glu, w_proj_s5=w_proj_s5, hgrn_lb_logits=hgrn_lb_logits, hgrn_norm_g=hgrn_norm_g, w_proj_hgrn=w_proj_hgrn, w_out=w_out, ln_ffn_g=ln_ffn_g, w_up=w_up, conv_w=conv_w, conv_b=conv_b, w_down=w_down, ln_final_g=ln_final_g, loss_target=loss_target, m_ln_mix_g=m_ln_mix_g, m_w_in=m_w_in, m_s5_a_re=m_s5_a_re, m_s5_a_im=m_s5_a_im, m_s5_log_dt=m_s5_log_dt, m_s5_b_re=m_s5_b_re, m_s5_b_im=m_s5_b_im, m_s5_c_re=m_s5_c_re, m_s5_c_im=m_s5_c_im, m_s5_d=m_s5_d, m_s5_w_glu=m_s5_w_glu, m_s5_b_glu=m_s5_b_glu, m_w_proj_s5=m_w_proj_s5, m_hgrn_lb_logits=m_hgrn_lb_logits, m_hgrn_norm_g=m_hgrn_norm_g, m_w_proj_hgrn=m_w_proj_hgrn, m_w_out=m_w_out, m_ln_ffn_g=m_ln_ffn_g, m_w_up=m_w_up, m_conv_w=m_conv_w, m_conv_b=m_conv_b, m_w_down=m_w_down, m_ln_final_g=m_ln_final_g, v_ln_mix_g=v_ln_mix_g, v_w_in=v_w_in, v_s5_a_re=v_s5_a_re, v_s5_a_im=v_s5_a_im, v_s5_log_dt=v_s5_log_dt, v_s5_b_re=v_s5_b_re, v_s5_b_im=v_s5_b_im, v_s5_c_re=v_s5_c_re, v_s5_c_im=v_s5_c_im, v_s5_d=v_s5_d, v_s5_w_glu=v_s5_w_glu, v_s5_b_glu=v_s5_b_glu, v_w_proj_s5=v_w_proj_s5, v_hgrn_lb_logits=v_hgrn_lb_logits, v_hgrn_norm_g=v_hgrn_norm_g, v_w_proj_hgrn=v_w_proj_hgrn, v_w_out=v_w_out, v_ln_ffn_g=v_ln_ffn_g, v_w_up=v_w_up, v_conv_w=v_conv_w, v_conv_b=v_conv_b, v_w_down=v_w_down, v_ln_final_g=v_ln_final_g)
    weights = {n: given[n] for n in TWIN_WEIGHTS}
    shared = {n: given[n] for n in SHARED_INPUTS}
    per_example = {n: given[n] for n in ['x']}
    grad_fn = _jax.value_and_grad(_loss, argnums=(0, 1))

    def one_microbatch(ex, loss_target):
        ex = dict(ex)
        diff = ex.pop(TWIN_DIFF_INPUT)
        return grad_fn(weights, diff, {**shared, **ex}, loss_target)

    if N_MICROBATCH == 1:
        loss, (grad_w, grad_x) = one_microbatch(per_example, given["loss_target"])
    else:
        def body(carry, xs):
            loss_sum, grad_sum = carry
            l_k, (gw_k, gx_k) = one_microbatch(xs[0], xs[1])
            with _jax.named_scope("update"):
                return (loss_sum + l_k, _jax.tree.map(_jnp.add, grad_sum, gw_k)), gx_k

        init = (_jnp.zeros((), _jnp.float32), _jax.tree.map(_jnp.zeros_like, weights))
        (loss, grad_w), grad_x = _jax.lax.scan(body, init, (per_example, given["loss_target"]))
    with _jax.named_scope("update"):
        delta_w, new_m, new_v = {}, {}, {}
        for n in TWIN_WEIGHTS:
            delta_w[n], new_m[n], new_v[n] = _adamw(weights[n], grad_w[n], given["m_" + n], given["v_" + n])
    return (loss, grad_x, *[grad_w[n] for n in TWIN_WEIGHTS], *[delta_w[n] for n in TWIN_WEIGHTS],
            *[new_m[n] for n in TWIN_WEIGHTS], *[new_v[n] for n in TWIN_WEIGHTS])
```

```python
import math

import jax
import jax.numpy as jnp
from jax import lax
from jax.experimental import pallas as pl
from jax.experimental.pallas import tpu as pltpu

F32 = jnp.float32
BF16 = jnp.bfloat16

N_DEV = 8
D_MODEL = 2048
S5_WIDTH = 1024
S5_GROUP = 16
S5_GROUPS = 64
S5_STATE = 64
S5_MAX_RE = -1e-4
S5_SUPER = 8
S5_LANES = S5_SUPER * S5_STATE
HGRN_WIDTH = 1024
HGRN_HEADS = 8
HGRN_DH = 128
HGRN_CHUNK = 64
D_FF = 5632
RMS_EPS = 1e-6
ADAM_LR = 0.001
ADAM_B1 = 0.9
ADAM_B2 = 0.999
ADAM_EPS = 1e-08
ADAM_WD = 0.01
ADAM_STEP = 10

LANE = 128
SUBLANE = 8
VMEM_LIMIT = 48 * 1024 * 1024
MESH = pl.DeviceIdType.MESH
GELU_C = math.sqrt(2.0 / math.pi)
GELU_A = 0.044715


def _params(sem=None):
    return pltpu.CompilerParams(dimension_semantics=sem, vmem_limit_bytes=VMEM_LIMIT)


def _pick(n, cap, unit=LANE):
    best = None
    for t in range(unit, min(n, cap) + 1, unit):
        if n % t == 0:
            best = t
    return best if best is not None else n


def _sigmoid(x):
    return 1.0 / (1.0 + jnp.exp(-x))


def _silu_and_grad(x):
    s = _sigmoid(x)
    return x * s, s * (1.0 + x * (1.0 - s))


def _gelu_and_grad(y):
    inner = GELU_C * (y + GELU_A * y * y * y)
    th = jnp.tanh(inner)
    val = 0.5 * y * (1.0 + th)
    grad = 0.5 * (1.0 + th) + 0.5 * y * (1.0 - th * th) * GELU_C * (1.0 + 3.0 * GELU_A * y * y)
    return val, grad


def _dot(a, b):
    return jnp.dot(a, b, preferred_element_type=F32)


def _dot_nt(a, b):
    return lax.dot_general(a, b, (((1,), (1,)), ((), ())), preferred_element_type=F32)


def _dot_tn(a, b):
    return lax.dot_general(a, b, (((0,), (0,)), ((), ())), preferred_element_type=F32)


def _mm_nn(a, w, name, res=None, out_dtype=F32):
    m, kdim = a.shape
    nb, _, ns = w.shape
    tm, tk, tn = _pick(m, 512), _pick(kdim, 512), _pick(ns, 1536)
    npb, nk = ns // tn, kdim // tk

    def body(*refs):
        if res is None:
            a_ref, w_ref, o_ref, acc = refs
        else:
            a_ref, w_ref, r_ref, o_ref, acc = refs
        k = pl.program_id(2)

        @pl.when(k == 0)
        def _():
            acc[...] = jnp.zeros_like(acc)

        acc[...] += _dot(a_ref[...], w_ref[...])

        @pl.when(k == nk - 1)
        def _():
            r = acc[...]
            if res is not None:
                r = r + r_ref[...]
            o_ref[...] = r.astype(out_dtype)

    in_specs = [pl.BlockSpec((tm, tk), lambda i, j, k: (i, k)),
                pl.BlockSpec((None, tk, tn), lambda i, j, k: (j // npb, k, j % npb))]
    args = [a, w]
    if res is not None:
        in_specs.append(pl.BlockSpec((tm, tn), lambda i, j, k: (i, j)))
        args.append(res)
    return pl.pallas_call(
        body, name=name, grid=(m // tm, nb * npb, nk),
        in_specs=in_specs, out_specs=pl.BlockSpec((tm, tn), lambda i, j, k: (i, j)),
        out_shape=jax.ShapeDtypeStruct((m, nb * ns), out_dtype),
        scratch_shapes=[pltpu.VMEM((tm, tn), F32)],
        compiler_params=_params(("parallel", "parallel", "arbitrary")),
    )(*args)


def _mm_nt(a, w, name, out_dtype=F32):
    m, _ = a.shape
    nb, kdim, ns = w.shape
    tm, tko, tn = _pick(m, 512), _pick(kdim, 512), _pick(ns, 1536)
    npb = ns // tn
    nred = nb * npb

    def body(a_ref, w_ref, o_ref, acc):
        n = pl.program_id(2)

        @pl.when(n == 0)
        def _():
            acc[...] = jnp.zeros_like(acc)

        acc[...] += _dot_nt(a_ref[...], w_ref[...])

        @pl.when(n == nred - 1)
        def _():
            o_ref[...] = acc[...].astype(out_dtype)

    return pl.pallas_call(
        body, name=name, grid=(m // tm, kdim // tko, nred),
        in_specs=[pl.BlockSpec((tm, tn), lambda i, j, n: (i, n)),
                  pl.BlockSpec((None, tko, tn), lambda i, j, n: (n // npb, j, n % npb))],
        out_specs=pl.BlockSpec((tm, tko), lambda i, j, n: (i, j)),
        out_shape=jax.ShapeDtypeStruct((m, kdim), out_dtype),
        scratch_shapes=[pltpu.VMEM((tm, tko), F32)],
        compiler_params=_params(("parallel", "parallel", "arbitrary")),
    )(a, w)


def _mm_tn(a, d, nb, name, out_dtype=BF16):
    m, kdim = a.shape
    ns = d.shape[1] // nb
    tm, tko, tn = _pick(m, 512), _pick(kdim, 512), _pick(ns, 1536)
    npb, nm = ns // tn, m // tm

    def body(a_ref, d_ref, o_ref, acc):
        r = pl.program_id(2)

        @pl.when(r == 0)
        def _():
            acc[...] = jnp.zeros_like(acc)

        acc[...] += _dot_tn(a_ref[...], d_ref[...])

        @pl.when(r == nm - 1)
        def _():
            o_ref[...] = acc[...].astype(out_dtype)

    return pl.pallas_call(
        body, name=name, grid=(kdim // tko, nb * npb, nm),
        in_specs=[pl.BlockSpec((tm, tko), lambda i, j, r: (r, i)),
                  pl.BlockSpec((tm, tn), lambda i, j, r: (r, j))],
        out_specs=pl.BlockSpec((None, tko, tn), lambda i, j, r: (j // npb, i, j % npb)),
        out_shape=jax.ShapeDtypeStruct((nb, kdim, ns), out_dtype),
        scratch_shapes=[pltpu.VMEM((tko, tn), F32)],
        compiler_params=_params(("parallel", "parallel", "arbitrary")),
    )(a, d)


def _rms_fwd(x, g, name):
    t, d = x.shape
    tr = _pick(t, 256, SUBLANE)

    def body(x_ref, g_ref, h_ref):
        xv = x_ref[...]
        r = lax.rsqrt(jnp.mean(xv * xv, axis=-1, keepdims=True) + RMS_EPS)
        h_ref[...] = (xv * r * g_ref[...]).astype(BF16)

    return pl.pallas_call(
        body, name=name, grid=(t // tr,),
        in_specs=[pl.BlockSpec((tr, d), lambda i: (i, 0)), pl.BlockSpec((1, d), lambda i: (0, 0))],
        out_specs=pl.BlockSpec((tr, d), lambda i: (i, 0)),
        out_shape=jax.ShapeDtypeStruct((t, d), BF16),
        compiler_params=_params(("parallel",)),
    )(x, g)


def _rms_bwd(x, g, dh, add, name, want_bf16):
    t, d = x.shape
    tr = _pick(t, 256, SUBLANE)

    def body(x_ref, g_ref, dh_ref, add_ref, *outs):
        if want_bf16:
            dx_ref, dxb_ref, dg_ref = outs
        else:
            dx_ref, dg_ref = outs
        i = pl.program_id(0)

        @pl.when(i == 0)
        def _():
            dg_ref[...] = jnp.zeros_like(dg_ref)

        xv, dhv = x_ref[...], dh_ref[...]
        r = lax.rsqrt(jnp.mean(xv * xv, axis=-1, keepdims=True) + RMS_EPS)
        xh = xv * r
        dg_ref[...] += jnp.sum(dhv * xh, axis=0, keepdims=True)
        dxh = dhv * g_ref[...]
        dx = add_ref[...] + r * (dxh - xh * jnp.mean(dxh * xh, axis=-1, keepdims=True))
        dx_ref[...] = dx
        if want_bf16:
            dxb_ref[...] = dx.astype(BF16)

    row = pl.BlockSpec((tr, d), lambda i: (i, 0))
    vec = pl.BlockSpec((1, d), lambda i: (0, 0))
    out_specs = [row] + ([row] if want_bf16 else []) + [vec]
    out_shape = ([jax.ShapeDtypeStruct((t, d), F32)] + ([jax.ShapeDtypeStruct((t, d), BF16)] if want_bf16 else [])
                 + [jax.ShapeDtypeStruct((1, d), F32)])
    return pl.pallas_call(
        body, name=name, grid=(t // tr,),
        in_specs=[row, vec, row, row], out_specs=out_specs, out_shape=out_shape,
        compiler_params=_params(("arbitrary",)),
    )(x, g, dh, add)


def _loss_head(x2, g, target, name="loss_head"):
    t, d = x2.shape
    tr = _pick(t, 256, SUBLANE)

    def body(x_ref, g_ref, t_ref, dx_ref, dxb_ref, dg_ref, loss_ref):
        i = pl.program_id(0)

        @pl.when(i == 0)
        def _():
            dg_ref[...] = jnp.zeros_like(dg_ref)
            loss_ref[...] = jnp.zeros_like(loss_ref)

        xv = x_ref[...]
        gv = g_ref[...]
        r = lax.rsqrt(jnp.mean(xv * xv, axis=-1, keepdims=True) + RMS_EPS)
        xh = xv * r
        err = xh * gv - t_ref[...]
        part = 0.5 * jnp.sum(jnp.mean(err * err, axis=-1, keepdims=True), axis=0, keepdims=True)
        loss_ref[...] += jnp.broadcast_to(part, loss_ref.shape)
        dy = err * (1.0 / d)
        dg_ref[...] += jnp.sum(dy * xh, axis=0, keepdims=True)
        dxh = dy * gv
        dx = r * (dxh - xh * jnp.mean(dxh * xh, axis=-1, keepdims=True))
        dx_ref[...] = dx
        dxb_ref[...] = dx.astype(BF16)

    row = pl.BlockSpec((tr, d), lambda i: (i, 0))
    vec = pl.BlockSpec((1, d), lambda i: (0, 0))
    return pl.pallas_call(
        body, name=name, grid=(t // tr,),
        in_specs=[row, vec, row],
        out_specs=[row, row, vec, pl.BlockSpec((1, LANE), lambda i: (0, 0))],
        out_shape=[jax.ShapeDtypeStruct((t, d), F32), jax.ShapeDtypeStruct((t, d), BF16),
                   jax.ShapeDtypeStruct((1, d), F32), jax.ShapeDtypeStruct((1, LANE), F32)],
        compiler_params=_params(("arbitrary",)),
    )(x2, g, target)


def _s5_discretize(a_re, a_im, ldt):
    lam_re = jnp.minimum(a_re, S5_MAX_RE)
    lam_im = a_im
    dt = jnp.exp(ldt)
    mag = jnp.exp(lam_re * dt)
    abar_re = mag * jnp.cos(lam_im * dt)
    abar_im = mag * jnp.sin(lam_im * dt)
    den = lam_re * lam_re + lam_im * lam_im
    nr = abar_re - 1.0
    ni = abar_im
    coef_re = (nr * lam_re + ni * lam_im) / den
    coef_im = (ni * lam_re - nr * lam_im) / den
    return abar_re, abar_im, coef_re, coef_im


def _s5_param_fwd(a_re, a_im, ldt):
    def body(ar_ref, ai_ref, l_ref, o0, o1, o2, o3):
        outs = _s5_discretize(ar_ref[...], ai_ref[...], l_ref[...])
        for o, v in zip((o0, o1, o2, o3), outs):
            o[...] = v

    sh = jax.ShapeDtypeStruct(a_re.shape, F32)
    return pl.pallas_call(body, name="s5_param_fwd", out_shape=[sh, sh, sh, sh], compiler_params=_params())(a_re, a_im, ldt)


def _s5_param_bwd(a_re, a_im, ldt, cts):
    def body(ar_ref, ai_ref, l_ref, c0, c1, c2, c3, g0, g1, g2):
        _, vjp = jax.vjp(_s5_discretize, ar_ref[...], ai_ref[...], l_ref[...])
        ga, gb, gl = vjp((c0[...], c1[...], c2[...], c3[...]))
        g0[...] = ga
        g1[...] = gb
        g2[...] = gl

    sh = jax.ShapeDtypeStruct(a_re.shape, F32)
    return pl.pallas_call(body, name="s5_param_bwd", out_shape=[sh, sh, jax.ShapeDtypeStruct(ldt.shape, F32)],
                          compiler_params=_params())(a_re, a_im, ldt, *cts)


def _cmul(ar, ai, br, bi):
    return ar * br - ai * bi, ar * bi + ai * br


def _s5_tables(ar, ai, reverse):
    a1 = (ar, ai)
    a2 = _cmul(*a1, *a1)
    a4 = _cmul(*a2, *a2)
    pows = [a1]
    for _ in range(SUBLANE - 1):
        pows.append(_cmul(*pows[-1], *a1))
    row = lax.broadcasted_iota(jnp.int32, (SUBLANE, ar.shape[1]), 0)
    tr = jnp.zeros((SUBLANE, ar.shape[1]), F32)
    ti = jnp.zeros((SUBLANE, ar.shape[1]), F32)
    for r in range(SUBLANE):
        p = pows[SUBLANE - 1 - r] if reverse else pows[r]
        tr = jnp.where(row == r, p[0], tr)
        ti = jnp.where(row == r, p[1], ti)
    return (a1, a2, a4), (tr, ti), row


def _s5_block_scan(xr, xi, pw, table, row, kr, ki):
    for k, (pr, pi) in zip((1, 2, 4), pw):
        sr = jnp.where(row >= k, pltpu.roll(xr, k, 0), 0.0)
        si = jnp.where(row >= k, pltpu.roll(xi, k, 0), 0.0)
        xr, xi = xr + pr * sr - pi * si, xi + pr * si + pi * sr
    tr, ti = table
    return xr + tr * kr - ti * ki, xi + tr * ki + ti * kr


def _s5_block_scan_rev(xr, xi, pw, table, row, kr, ki):
    for k, (pr, pi) in zip((1, 2, 4), pw):
        sr = jnp.where(row < SUBLANE - k, pltpu.roll(xr, SUBLANE - k, 0), 0.0)
        si = jnp.where(row < SUBLANE - k, pltpu.roll(xi, SUBLANE - k, 0), 0.0)
        xr, xi = xr + pr * sr + pi * si, xi + pr * si - pi * sr
    tr, ti = table
    return xr + tr * kr + ti * ki, xi + tr * ki - ti * kr


def _s5_fwd(proj, bsg, ccat, dvec, abar_re, abar_im, coef_re, coef_im):
    t = proj.shape[0]
    tc = _pick(t, 512, SUBLANE)
    n_chunks, nblk, n = t // tc, tc // SUBLANE, S5_LANES

    def body(u_ref, b_ref, c_ref, d_ref, ar_ref, ai_ref, cr_ref, ci_ref, y_ref, sb_ref, bu, st, car):
        @pl.when(pl.program_id(1) == 0)
        def _():
            car[...] = jnp.zeros_like(car)

        sb_ref[...] = car[...]
        u = u_ref[...]
        bu[...] = _dot(u.astype(BF16), b_ref[...])
        cr, ci = cr_ref[...], ci_ref[...]
        pw, table, row = _s5_tables(ar_ref[...], ai_ref[...], reverse=False)

        def blk(i, carry):
            r0 = pl.multiple_of(i * SUBLANE, SUBLANE)
            br, bi = bu[pl.ds(r0, SUBLANE), 0:n], bu[pl.ds(r0, SUBLANE), n:2 * n]
            xr, xi = _cmul(cr, ci, br, bi)
            xr, xi = _s5_block_scan(xr, xi, pw, table, row, *carry)
            st[pl.ds(r0, SUBLANE), 0:n] = xr
            st[pl.ds(r0, SUBLANE), n:2 * n] = xi
            return xr[SUBLANE - 1:SUBLANE, :], xi[SUBLANE - 1:SUBLANE, :]

        kr, ki = lax.fori_loop(0, nblk, blk, (car[:, 0:n], car[:, n:2 * n]))
        car[:, 0:n] = kr
        car[:, n:2 * n] = ki
        y_ref[...] = _dot(st[...].astype(BF16), c_ref[...]) + d_ref[...] * u

    vec = pl.BlockSpec((None, 1, n), lambda s, c: (s, 0, 0))
    return pl.pallas_call(
        body, name="s5_fwd", grid=(S5_SUPER, n_chunks),
        in_specs=[pl.BlockSpec((tc, LANE), lambda s, c: (c, s)),
                  pl.BlockSpec((None, LANE, 2 * n), lambda s, c: (s, 0, 0)),
                  pl.BlockSpec((None, 2 * n, LANE), lambda s, c: (s, 0, 0)),
                  pl.BlockSpec((None, 1, LANE), lambda s, c: (s, 0, 0)),
                  vec, vec, vec, vec],
        out_specs=[pl.BlockSpec((tc, LANE), lambda s, c: (c, s)),
                   pl.BlockSpec((None, None, 1, 2 * n), lambda s, c: (s, c, 0, 0))],
        out_shape=[jax.ShapeDtypeStruct((t, S5_WIDTH), F32),
                   jax.ShapeDtypeStruct((S5_SUPER, n_chunks, 1, 2 * n), F32)],
        scratch_shapes=[pltpu.VMEM((tc, 2 * n), F32), pltpu.VMEM((tc, 2 * n), F32), pltpu.VMEM((1, 2 * n), F32)],
        compiler_params=_params(("parallel", "arbitrary")),
    )(proj, bsg, ccat, dvec, abar_re, abar_im, coef_re, coef_im)


def _s5_bwd(proj, dy, sb, bsg, ccat, dvec, abar_re, abar_im, coef_re, coef_im):
    t = proj.shape[0]
    tc = _pick(t, 512, SUBLANE)
    n_chunks, nblk, n = t // tc, tc // SUBLANE, S5_LANES

    def body(u_ref, dy_ref, sb_ref, b_ref, c_ref, d_ref, ar_ref, ai_ref, cr_ref, ci_ref,
             du_ref, gb_ref, gc_ref, gd_ref, gar_ref, gai_ref, gcr_ref, gci_ref,
             bu, st, sp, gbu, gcar, acc):
        @pl.when(pl.program_id(1) == 0)
        def _():
            gcar[...] = jnp.zeros_like(gcar)
            acc[...] = jnp.zeros_like(acc)
            gb_ref[...] = jnp.zeros_like(gb_ref)
            gc_ref[...] = jnp.zeros_like(gc_ref)
            gd_ref[...] = jnp.zeros_like(gd_ref)

        u = u_ref[...]
        dyv = dy_ref[...]
        u16, dy16 = u.astype(BF16), dyv.astype(BF16)
        bu[...] = _dot(u16, b_ref[...])
        ar, ai = ar_ref[...], ai_ref[...]
        cr, ci = cr_ref[...], ci_ref[...]
        pw, table, row = _s5_tables(ar, ai, reverse=False)

        def fblk(i, carry):
            kr, ki = carry
            r0 = pl.multiple_of(i * SUBLANE, SUBLANE)
            br, bi = bu[pl.ds(r0, SUBLANE), 0:n], bu[pl.ds(r0, SUBLANE), n:2 * n]
            xr, xi = _cmul(cr, ci, br, bi)
            xr, xi = _s5_block_scan(xr, xi, pw, table, row, kr, ki)
            st[pl.ds(r0, SUBLANE), 0:n] = xr
            st[pl.ds(r0, SUBLANE), n:2 * n] = xi
            sp[pl.ds(r0, SUBLANE), 0:n] = jnp.where(row == 0, kr, pltpu.roll(xr, 1, 0))
            sp[pl.ds(r0, SUBLANE), n:2 * n] = jnp.where(row == 0, ki, pltpu.roll(xi, 1, 0))
            return xr[SUBLANE - 1:SUBLANE, :], xi[SUBLANE - 1:SUBLANE, :]

        lax.fori_loop(0, nblk, fblk, (sb_ref[:, 0:n], sb_ref[:, n:2 * n]))

        gbu[...] = _dot_nt(dy16, c_ref[...])
        _, rtable, _ = _s5_tables(ar, ai, reverse=True)

        def rblk(j, carry):
            kr, ki, a0, a1, a2, a3 = carry
            r0 = pl.multiple_of((nblk - 1 - j) * SUBLANE, SUBLANE)
            xr, xi = gbu[pl.ds(r0, SUBLANE), 0:n], gbu[pl.ds(r0, SUBLANE), n:2 * n]
            xr, xi = _s5_block_scan_rev(xr, xi, pw, rtable, row, kr, ki)
            pr, pi = sp[pl.ds(r0, SUBLANE), 0:n], sp[pl.ds(r0, SUBLANE), n:2 * n]
            br, bi = bu[pl.ds(r0, SUBLANE), 0:n], bu[pl.ds(r0, SUBLANE), n:2 * n]
            a0 = a0 + pr * xr + pi * xi
            a1 = a1 + pr * xi - pi * xr
            a2 = a2 + br * xr + bi * xi
            a3 = a3 + br * xi - bi * xr
            gbu[pl.ds(r0, SUBLANE), 0:n] = cr * xr + ci * xi
            gbu[pl.ds(r0, SUBLANE), n:2 * n] = cr * xi - ci * xr
            return xr[0:1, :], xi[0:1, :], a0, a1, a2, a3

        init = (gcar[:, 0:n], gcar[:, n:2 * n], acc[0], acc[1], acc[2], acc[3])
        kr, ki, a0, a1, a2, a3 = lax.fori_loop(0, nblk, rblk, init)
        gcar[:, 0:n] = kr
        gcar[:, n:2 * n] = ki
        for idx, (a, o) in enumerate(zip((a0, a1, a2, a3), (gar_ref, gai_ref, gcr_ref, gci_ref))):
            acc[idx] = a
            o[...] = jnp.sum(a, axis=0, keepdims=True)

        g16 = gbu[...].astype(BF16)
        gb_ref[...] += _dot_tn(u16, g16)
        gc_ref[...] += _dot_tn(st[...].astype(BF16), dy16)
        gd_ref[...] += jnp.sum(dyv * u, axis=0, keepdims=True)
        du_ref[...] = (_dot_nt(g16, b_ref[...]) + d_ref[...] * dyv).astype(BF16)

    last = n_chunks - 1
    vec = pl.BlockSpec((None, 1, n), lambda s, c: (s, 0, 0))
    vsh = jax.ShapeDtypeStruct((S5_SUPER, 1, n), F32)
    return pl.pallas_call(
        body, name="s5_bwd", grid=(S5_SUPER, n_chunks),
        in_specs=[pl.BlockSpec((tc, LANE), lambda s, c: (last - c, s)),
                  pl.BlockSpec((tc, LANE), lambda s, c: (last - c, s)),
                  pl.BlockSpec((None, None, 1, 2 * n), lambda s, c: (s, last - c, 0, 0)),
                  pl.BlockSpec((None, LANE, 2 * n), lambda s, c: (s, 0, 0)),
                  pl.BlockSpec((None, 2 * n, LANE), lambda s, c: (s, 0, 0)),
                  pl.BlockSpec((None, 1, LANE), lambda s, c: (s, 0, 0)),
                  vec, vec, vec, vec],
        out_specs=[pl.BlockSpec((tc, LANE), lambda s, c: (last - c, s)),
                   pl.BlockSpec((None, LANE, 2 * n), lambda s, c: (s, 0, 0)),
                   pl.BlockSpec((None, 2 * n, LANE), lambda s, c: (s, 0, 0)),
                   pl.BlockSpec((None, 1, LANE), lambda s, c: (s, 0, 0)),
                   vec, vec, vec, vec],
        out_shape=[jax.ShapeDtypeStruct((t, S5_WIDTH), BF16),
                   jax.ShapeDtypeStruct((S5_SUPER, LANE, 2 * n), F32),
                   jax.ShapeDtypeStruct((S5_SUPER, 2 * n, LANE), F32),
                   jax.ShapeDtypeStruct((S5_SUPER, 1, LANE), F32),
                   vsh, vsh, vsh, vsh],
        scratch_shapes=[pltpu.VMEM((tc, 2 * n), F32), pltpu.VMEM((tc, 2 * n), F32), pltpu.VMEM((tc, 2 * n), F32),
                        pltpu.VMEM((tc, 2 * n), F32), pltpu.VMEM((1, 2 * n), F32), pltpu.VMEM((4, SUBLANE, n), F32)],
        compiler_params=_params(("parallel", "arbitrary")),
    )(proj, dy, sb, bsg, ccat, dvec, abar_re, abar_im, coef_re, coef_im)


def _gelu_fwd(y, name="s5_gelu"):
    t, w = y.shape
    tr = _pick(t, 512, SUBLANE)

    def body(y_ref, z_ref):
        z_ref[...] = _gelu_and_grad(y_ref[...])[0].astype(BF16)

    row = pl.BlockSpec((tr, w), lambda i: (i, 0))
    return pl.pallas_call(body, name=name, grid=(t // tr,), in_specs=[row], out_specs=row,
                          out_shape=jax.ShapeDtypeStruct((t, w), BF16), compiler_params=_params(("parallel",)))(y)


def _glu_fwd(y, gl, b, name="s5_glu"):
    t, w = y.shape
    tr = _pick(t, 512, SUBLANE)

    def body(y_ref, gl_ref, b_ref, z2_ref):
        z = _gelu_and_grad(y_ref[...])[0]
        z2_ref[...] = (z * _sigmoid(gl_ref[...] + b_ref[...])).astype(BF16)

    row = pl.BlockSpec((tr, w), lambda i: (i, 0))
    return pl.pallas_call(body, name=name, grid=(t // tr,),
                          in_specs=[row, row, pl.BlockSpec((1, w), lambda i: (0, 0))], out_specs=row,
                          out_shape=jax.ShapeDtypeStruct((t, w), BF16), compiler_params=_params(("parallel",)))(y, gl, b)


def _glu_bwd(y, gl, b, dz2, name="s5_glu_bwd"):
    t, w = y.shape
    tr = _pick(t, 512, SUBLANE)

    def body(y_ref, gl_ref, b_ref, dz2_ref, dgl_ref, dza_ref, db_ref):
        @pl.when(pl.program_id(0) == 0)
        def _():
            db_ref[...] = jnp.zeros_like(db_ref)

        z = _gelu_and_grad(y_ref[...])[0]
        s = _sigmoid(gl_ref[...] + b_ref[...])
        dz2v = dz2_ref[...]
        dgl = dz2v * z * s * (1.0 - s)
        dgl_ref[...] = dgl.astype(BF16)
        dza_ref[...] = dz2v * s
        db_ref[...] += jnp.sum(dgl, axis=0, keepdims=True)

    row = pl.BlockSpec((tr, w), lambda i: (i, 0))
    vec = pl.BlockSpec((1, w), lambda i: (0, 0))
    return pl.pallas_call(body, name=name, grid=(t // tr,), in_specs=[row, row, vec, row], out_specs=[row, row, vec],
                          out_shape=[jax.ShapeDtypeStruct((t, w), BF16), jax.ShapeDtypeStruct((t, w), F32),
                                     jax.ShapeDtypeStruct((1, w), F32)],
                          compiler_params=_params(("arbitrary",)))(y, gl, b, dz2)


def _gelu_bwd(y, dza, dzb, name="s5_gelu_bwd"):
    t, w = y.shape
    tr = _pick(t, 512, SUBLANE)

    def body(y_ref, a_ref, b_ref, dy_ref):
        dy_ref[...] = (a_ref[...] + b_ref[...]) * _gelu_and_grad(y_ref[...])[1]

    row = pl.BlockSpec((tr, w), lambda i: (i, 0))
    return pl.pallas_call(body, name=name, grid=(t // tr,), in_specs=[row, row, row], out_specs=row,
                          out_shape=jax.ShapeDtypeStruct((t, w), F32), compiler_params=_params(("parallel",)))(y, dza, dzb)


def _tri_dot(tri16, x):
    hi = x.astype(BF16)
    r1 = x - hi.astype(F32)
    mid = r1.astype(BF16)
    lo = (r1 - mid.astype(F32)).astype(BF16)
    return _dot(tri16, hi) + _dot(tri16, mid) + _dot(tri16, lo)


def _hgrn_pre(q_in, z, lg):
    lb = _sigmoid(lg[0:1, :] - lg[1:2, :])
    qs, dqs = _silu_and_grad(q_in)
    sz = _sigmoid(z)
    f = lb + (1.0 - lb) * sz
    k = (1.0 - lb) * (1.0 - sz)
    c = HGRN_CHUNK
    r = lax.broadcasted_iota(jnp.int32, (c, c), 0)
    s = lax.broadcasted_iota(jnp.int32, (c, c), 1)
    causal = r >= s
    b = _tri_dot(jnp.where(causal, 1.0, 0.0).astype(BF16), jnp.log(f))
    b_end = b[c - 1:c, :]
    b_mid = b[c // 2 - 1:c // 2, :]
    e_q, e_k, e_0, e_c = jnp.exp(b - b_mid), jnp.exp(b_mid - b), jnp.exp(b), jnp.exp(b_end - b)
    return dict(lb=lb, qs=qs, dqs=dqs, sz=sz, f=f, k=k, causal=causal, b_end=b_end,
                e_q=e_q, e_k=e_k, e_0=e_0, e_c=e_c,
                qt=qs * e_q, kt=k * e_k, q0=qs * e_0, kc=k * e_c)


def _hgrn_fwd(proj, logits, ng):
    t = proj.shape[0]
    c, dh = HGRN_CHUNK, HGRN_DH
    n_chunks = t // c

    def body(q_ref, z_ref, v_ref, g_ref, lg_ref, ng_ref, o_ref, oh_ref, s0_ref, st):
        @pl.when(pl.program_id(1) == 0)
        def _():
            st[...] = jnp.zeros_like(st)

        s0 = st[...]
        s0_ref[...] = s0
        p = _hgrn_pre(q_ref[...], z_ref[...], lg_ref[...])
        v16 = v_ref[...].astype(BF16)
        a = jnp.where(p["causal"], _dot_nt(p["qt"].astype(BF16), p["kt"].astype(BF16)), 0.0)
        o = _dot_nt(p["q0"].astype(BF16), s0.astype(BF16)) + _dot(a.astype(BF16), v16)
        st[...] = jnp.exp(p["b_end"]) * s0 + _dot_tn(v16, p["kc"].astype(BF16))
        o_ref[...] = o
        rn = lax.rsqrt(jnp.mean(o * o, axis=-1, keepdims=True) + RMS_EPS)
        oh_ref[...] = (o * rn * ng_ref[...] * _silu_and_grad(g_ref[...])[0]).astype(BF16)

    def col(off):
        return pl.BlockSpec((c, dh), lambda h, i: (i, off + h))

    return pl.pallas_call(
        body, name="hgrn_fwd", grid=(HGRN_HEADS, n_chunks),
        in_specs=[col(8), col(16), col(24), col(32),
                  pl.BlockSpec((2, dh), lambda h, i: (0, h)), pl.BlockSpec((1, dh), lambda h, i: (0, h))],
        out_specs=[col(0), col(0), pl.BlockSpec((None, None, dh, dh), lambda h, i: (h, i, 0, 0))],
        out_shape=[jax.ShapeDtypeStruct((t, HGRN_WIDTH), F32), jax.ShapeDtypeStruct((t, HGRN_WIDTH), BF16),
                   jax.ShapeDtypeStruct((HGRN_HEADS, n_chunks, dh, dh), F32)],
        scratch_shapes=[pltpu.VMEM((dh, dh), F32)],
        compiler_params=_params(("parallel", "arbitrary")),
    )(proj, proj, proj, proj, logits, ng)


def _hgrn_bwd(proj, o_raw, s0s, doh, logits, ng):
    t = proj.shape[0]
    c, dh = HGRN_CHUNK, HGRN_DH
    n_chunks = t // c
    last = n_chunks - 1

    def body(q_ref, z_ref, v_ref, g_ref, o_ref, s0_ref, doh_ref, lg_ref, ng_ref,
             dq_ref, dz_ref, dv_ref, dg_ref, dng_ref, dlb_ref, dst):
        @pl.when(pl.program_id(1) == 0)
        def _():
            dst[...] = jnp.zeros_like(dst)
            dng_ref[...] = jnp.zeros_like(dng_ref)
            dlb_ref[...] = jnp.zeros_like(dlb_ref)

        p = _hgrn_pre(q_ref[...], z_ref[...], lg_ref[...])
        v = v_ref[...]
        v16 = v.astype(BF16)
        s0 = s0_ref[...]
        ds_end = dst[...]
        ds16 = ds_end.astype(BF16)
        ngv = ng_ref[...]

        o = o_ref[...]
        dohv = doh_ref[...]
        sg, dsg = _silu_and_grad(g_ref[...])
        rn = lax.rsqrt(jnp.mean(o * o, axis=-1, keepdims=True) + RMS_EPS)
        oh = o * rn
        dg_ref[...] = (dohv * oh * ngv * dsg).astype(BF16)
        don = dohv * sg
        dng_ref[...] += jnp.sum(don * oh, axis=0, keepdims=True)
        doh_n = don * ngv
        do = rn * (doh_n - oh * jnp.mean(doh_n * oh, axis=-1, keepdims=True))
        do16 = do.astype(BF16)

        qt16, kt16, q016, kc16 = (p[n].astype(BF16) for n in ("qt", "kt", "q0", "kc"))
        a = jnp.where(p["causal"], _dot_nt(qt16, kt16), 0.0)
        da = jnp.where(p["causal"], _dot_nt(do16, v16), 0.0)
        da16 = da.astype(BF16)
        dqt = _dot(da16, kt16)
        dq0 = _dot(do16, s0.astype(BF16))
        dkt = _dot_tn(da16, qt16)
        dkc = _dot(v16, ds16)
        dv_ref[...] = (_dot_tn(a.astype(BF16), do16) + _dot_nt(kc16, ds16)).astype(BF16)
        lam_end = jnp.exp(p["b_end"])
        dst[...] = lam_end * ds_end + _dot_tn(do16, q016)

        qt, kt, q0, kc = (a.astype(F32) for a in (qt16, kt16, q016, kc16))
        db = dqt * qt + dq0 * q0 - dkt * kt - dkc * kc
        db_end = (jnp.sum(dkc * kc, axis=0, keepdims=True)
                  + jnp.sum(ds_end * s0, axis=0, keepdims=True) * lam_end)
        rowi = lax.broadcasted_iota(jnp.int32, (c, dh), 0)
        db = db + jnp.where(rowi == c - 1, db_end, 0.0)
        r = lax.broadcasted_iota(jnp.int32, (c, c), 0)
        s = lax.broadcasted_iota(jnp.int32, (c, c), 1)
        dlf = _tri_dot(jnp.where(s >= r, 1.0, 0.0).astype(BF16), db)

        dqs = dqt * p["e_q"] + dq0 * p["e_0"]
        dq_ref[...] = (dqs * p["dqs"]).astype(BF16)
        dk = dkt * p["e_k"] + dkc * p["e_c"]
        sz, lb = p["sz"], p["lb"]
        common = dlf / p["f"] - dk
        dz_ref[...] = ((1.0 - lb) * sz * (1.0 - sz) * common).astype(BF16)
        dlb_ref[...] += jnp.sum((1.0 - sz) * common, axis=0, keepdims=True)

    def col(off):
        return pl.BlockSpec((c, dh), lambda h, i: (last - i, off + h))

    vec = pl.BlockSpec((1, dh), lambda h, i: (0, h))
    act = jax.ShapeDtypeStruct((t, HGRN_WIDTH), BF16)
    vsh = jax.ShapeDtypeStruct((1, HGRN_WIDTH), F32)
    return pl.pallas_call(
        body, name="hgrn_bwd", grid=(HGRN_HEADS, n_chunks),
        in_specs=[col(8), col(16), col(24), col(32), col(0),
                  pl.BlockSpec((None, None, dh, dh), lambda h, i: (h, last - i, 0, 0)),
                  col(0), pl.BlockSpec((2, dh), lambda h, i: (0, h)), vec],
        out_specs=[col(0), col(0), col(0), col(0), vec, vec],
        out_shape=[act, act, act, act, vsh, vsh],
        scratch_shapes=[pltpu.VMEM((dh, dh), F32)],
        compiler_params=_params(("parallel", "arbitrary")),
    )(proj, proj, proj, proj, o_raw, s0s, doh, logits, ng)


def _lb_bwd(logits, dlb):
    def body(lg_ref, d_ref, o_ref):
        lg = lg_ref[...]
        lb = _sigmoid(lg[0:1, :] - lg[1:2, :])
        g = d_ref[...] * lb * (1.0 - lb)
        o_ref[0:1, :] = g
        o_ref[1:2, :] = -g

    return pl.pallas_call(body, name="hgrn_lb_bwd", out_shape=jax.ShapeDtypeStruct(logits.shape, F32),
                          compiler_params=_params())(logits, dlb)


MERGE_TC = 1024
GS_BLOCK = (S5_WIDTH + 4 * HGRN_WIDTH) // MERGE_TC
GH_BLOCK = GS_BLOCK + D_MODEL // MERGE_TC


def _merge_fwd(proj, ys, yh):
    t = proj.shape[0]
    tr = _pick(t, 256, SUBLANE)

    def body(gs_ref, gh_ref, ys_ref, yh_ref, m_ref):
        m_ref[...] = (_sigmoid(gs_ref[...]) * ys_ref[...] + _sigmoid(gh_ref[...]) * yh_ref[...]).astype(BF16)

    blk = pl.BlockSpec((tr, MERGE_TC), lambda i, j: (i, j))
    return pl.pallas_call(
        body, name="merge_fwd", grid=(t // tr, D_MODEL // MERGE_TC),
        in_specs=[pl.BlockSpec((tr, MERGE_TC), lambda i, j: (i, GS_BLOCK + j)),
                  pl.BlockSpec((tr, MERGE_TC), lambda i, j: (i, GH_BLOCK + j)), blk, blk],
        out_specs=blk, out_shape=jax.ShapeDtypeStruct((t, D_MODEL), BF16),
        compiler_params=_params(("parallel", "parallel")),
    )(proj, proj, ys, yh)


def _merge_bwd(proj, ys, yh, dm):
    t = proj.shape[0]
    tr = _pick(t, 256, SUBLANE)

    def body(gs_ref, gh_ref, ys_ref, yh_ref, dm_ref, dys_ref, dyh_ref, dgs_ref, dgh_ref):
        dmv = dm_ref[...]
        ss, sh = _sigmoid(gs_ref[...]), _sigmoid(gh_ref[...])
        dys_ref[...] = (dmv * ss).astype(BF16)
        dyh_ref[...] = (dmv * sh).astype(BF16)
        dgs_ref[...] = (dmv * ys_ref[...] * ss * (1.0 - ss)).astype(BF16)
        dgh_ref[...] = (dmv * yh_ref[...] * sh * (1.0 - sh)).astype(BF16)

    blk = pl.BlockSpec((tr, MERGE_TC), lambda i, j: (i, j))
    sh16 = jax.ShapeDtypeStruct((t, D_MODEL), BF16)
    return pl.pallas_call(
        body, name="merge_bwd", grid=(t // tr, D_MODEL // MERGE_TC),
        in_specs=[pl.BlockSpec((tr, MERGE_TC), lambda i, j: (i, GS_BLOCK + j)),
                  pl.BlockSpec((tr, MERGE_TC), lambda i, j: (i, GH_BLOCK + j)), blk, blk, blk],
        out_specs=[blk, blk, blk, blk], out_shape=[sh16, sh16, sh16, sh16],
        compiler_params=_params(("parallel", "parallel")),
    )(proj, proj, ys, yh, dm)


FFN_TC = 128
FFN_ROWS = 512
HALO = SUBLANE


def _rows_with_halo(ref, r0, nrows, t, before, after):
    lo = r0 - before if r0 - before >= 0 else r0
    hi = r0 + nrows + after if r0 + nrows + after <= t else r0 + nrows
    parts = []
    if lo == r0 and before:
        parts.append(jnp.zeros((before, ref.shape[1]), F32))
    parts.append(ref[lo:hi, :])
    if hi == r0 + nrows and after:
        parts.append(jnp.zeros((after, ref.shape[1]), F32))
    return parts[0] if len(parts) == 1 else jnp.concatenate(parts, axis=0)


def _conv3(ext, w, b, nrows, off):
    n = ext.shape[0]
    x0 = ext[off:off + nrows, :]
    x1 = pltpu.roll(ext, 1, 0)[off:off + nrows, :]
    x2 = pltpu.roll(ext, 2, 0)[off:off + nrows, :]
    return b + w[0:1, :] * x2 + w[1:2, :] * x1 + w[2:3, :] * x0, (x0, x1, x2)


def _ffn_act_fwd(up, cw, cb):
    t = up.shape[0]
    rows = _pick(t, FFN_ROWS, SUBLANE)
    nvb = D_FF // FFN_TC

    def body(ug_ref, uv_ref, wg_ref, wv_ref, bg_ref, bv_ref, act_ref):
        wg, wv, bg, bv = wg_ref[...], wv_ref[...], bg_ref[...], bv_ref[...]
        for r0 in range(0, t, rows):
            cg, _ = _conv3(_rows_with_halo(ug_ref, r0, rows, t, HALO, 0), wg, bg, rows, HALO)
            cv, _ = _conv3(_rows_with_halo(uv_ref, r0, rows, t, HALO, 0), wv, bv, rows, HALO)
            act_ref[r0:r0 + rows, :] = (_silu_and_grad(cg)[0] * cv).astype(BF16)

    def colblk(nrow, off):
        return pl.BlockSpec((nrow, FFN_TC), lambda j: (0, off + j))

    return pl.pallas_call(
        body, name="ffn_act_fwd", grid=(nvb,),
        in_specs=[colblk(t, 0), colblk(t, nvb), colblk(3, 0), colblk(3, nvb), colblk(1, 0), colblk(1, nvb)],
        out_specs=colblk(t, 0), out_shape=jax.ShapeDtypeStruct((t, D_FF), BF16),
        compiler_params=_params(("parallel",)),
    )(up, up, cw, cw, cb, cb)


def _ffn_act_bwd(up, dact, cw, cb):
    t = up.shape[0]
    rows = _pick(t, FFN_ROWS, SUBLANE)
    nvb = D_FF // FFN_TC

    def body(ug_ref, uv_ref, da_ref, wg_ref, wv_ref, bg_ref, bv_ref,
             dug_ref, duv_ref, dwg_ref, dwv_ref, dbg_ref, dbv_ref):
        wg, wv, bg, bv = wg_ref[...], wv_ref[...], bg_ref[...], bv_ref[...]
        ext = rows + HALO
        acc_g = [jnp.zeros((1, FFN_TC), F32) for _ in range(4)]
        acc_v = [jnp.zeros((1, FFN_TC), F32) for _ in range(4)]
        for r0 in range(0, t, rows):
            cg, xg = _conv3(_rows_with_halo(ug_ref, r0, rows, t, HALO, HALO), wg, bg, ext, HALO)
            cv, xv = _conv3(_rows_with_halo(uv_ref, r0, rows, t, HALO, HALO), wv, bv, ext, HALO)
            dav = _rows_with_halo(da_ref, r0, rows, t, 0, HALO)
            sg, dsg = _silu_and_grad(cg)
            for dconv, xs, w, acc, out in ((dav * cv * dsg, xg, wg, acc_g, dug_ref), (dav * sg, xv, wv, acc_v, duv_ref)):
                d0 = dconv[0:rows, :]
                d1 = pltpu.roll(dconv, ext - 1, 0)[0:rows, :]
                d2 = pltpu.roll(dconv, ext - 2, 0)[0:rows, :]
                out[r0:r0 + rows, :] = (w[2:3, :] * d0 + w[1:2, :] * d1 + w[0:1, :] * d2).astype(BF16)
                x0, x1, x2 = xs
                acc[0] = acc[0] + jnp.sum(d0 * x2[0:rows, :], axis=0, keepdims=True)
                acc[1] = acc[1] + jnp.sum(d0 * x1[0:rows, :], axis=0, keepdims=True)
                acc[2] = acc[2] + jnp.sum(d0 * x0[0:rows, :], axis=0, keepdims=True)
                acc[3] = acc[3] + jnp.sum(d0, axis=0, keepdims=True)
        for acc, dw_ref, db_ref in ((acc_g, dwg_ref, dbg_ref), (acc_v, dwv_ref, dbv_ref)):
            dw_ref[0:1, :] = acc[0]
            dw_ref[1:2, :] = acc[1]
            dw_ref[2:3, :] = acc[2]
            db_ref[...] = acc[3]

    def colblk(nrow, off):
        return pl.BlockSpec((nrow, FFN_TC), lambda j: (0, off + j))

    return pl.pallas_call(
        body, name="ffn_act_bwd", grid=(nvb,),
        in_specs=[colblk(t, 0), colblk(t, nvb), colblk(t, 0), colblk(3, 0), colblk(3, nvb), colblk(1, 0), colblk(1, nvb)],
        out_specs=[colblk(t, 0), colblk(t, 0), colblk(3, 0), colblk(3, 0), colblk(1, 0), colblk(1, 0)],
        out_shape=[jax.ShapeDtypeStruct((t, D_FF), BF16), jax.ShapeDtypeStruct((t, D_FF), BF16),
                   jax.ShapeDtypeStruct((3, D_FF), F32), jax.ShapeDtypeStruct((3, D_FF), F32),
                   jax.ShapeDtypeStruct((1, D_FF), F32), jax.ShapeDtypeStruct((1, D_FF), F32)],
        compiler_params=_params(("parallel",)),
    )(up, up, dact, cw, cw, cb, cb)


def _all_gather(shards, name):
    nw = len(shards)

    def body(*refs):
        x_refs, out_refs = refs[:nw], refs[nw:2 * nw]
        send_sems, recv_sems, local_sems = refs[2 * nw:]
        x, y, c = lax.axis_index("x"), lax.axis_index("y"), lax.axis_index("c")
        me, sibling = (x, y, c), (x, y, 1 - c)
        chips = [(1 - x, y), (x, 1 - y), (1 - x, 1 - y)]

        def copy(w, k, block, to, src=None):
            slot = out_refs[w].at[4 * block[0] + 2 * block[1] + block[2]]
            return pltpu.make_async_remote_copy(
                src_ref=slot if src is None else src, dst_ref=slot,
                send_sem=send_sems.at[w, k], recv_sem=recv_sems.at[w, k],
                device_id=to, device_id_type=MESH)

        mine, first, passed = [], [], []
        for w in range(nw):
            cp = pltpu.make_async_copy(x_refs[w], out_refs[w].at[4 * x + 2 * y + c], local_sems.at[w])
            cp.start()
            mine.append(cp)
            first.append(copy(w, 0, me, sibling, src=x_refs[w]))
            first += [copy(w, 1 + j, me, (*chip, c), src=x_refs[w]) for j, chip in enumerate(chips)]
        for cp in first:
            cp.start()
        for w in range(nw):
            for j, chip in enumerate(chips):
                copy(w, 1 + j, (*chip, c), me).wait_recv()
                fwd = copy(w, 4 + j, (*chip, c), sibling)
                fwd.start()
                passed.append(fwd)
        for w in range(nw):
            copy(w, 0, sibling, me).wait_recv()
            for j, chip in enumerate(chips):
                copy(w, 4 + j, (*chip, 1 - c), me).wait_recv()
        for cp in first + passed:
            cp.wait_send()
        for cp in mine:
            cp.wait()

    anyspec = pl.BlockSpec(memory_space=pl.ANY)
    return pl.pallas_call(
        body, name=name,
        in_specs=[anyspec] * nw, out_specs=[anyspec] * nw,
        out_shape=[jax.ShapeDtypeStruct((N_DEV,) + s.shape, s.dtype) for s in shards],
        scratch_shapes=[pltpu.SemaphoreType.DMA((nw, 7)), pltpu.SemaphoreType.DMA((nw, 7)),
                        pltpu.SemaphoreType.DMA((nw,))],
    )(*shards)


def _scatter_blocks(parts, name):
    nw = len(parts)

    def body(*refs):
        p_refs, out_refs = refs[:nw], refs[nw:2 * nw]
        send_sems, recv_sems, local_sems = refs[2 * nw:]
        x, y, c = lax.axis_index("x"), lax.axis_index("y"), lax.axis_index("c")
        me = 4 * x + 2 * y + c
        peers = []
        for k in range(1, N_DEV):
            px = 1 - x if k & 4 else x
            py = 1 - y if k & 2 else y
            pc = 1 - c if k & 1 else c
            peers.append((k, (px, py, pc), 4 * px + 2 * py + pc))

        def copy(w, k, peer, peer_lin):
            return pltpu.make_async_remote_copy(
                src_ref=p_refs[w].at[peer_lin], dst_ref=out_refs[w].at[me],
                send_sem=send_sems.at[w, k - 1], recv_sem=recv_sems.at[w, k - 1],
                device_id=peer, device_id_type=MESH)

        def arrival(w, k, peer, peer_lin):
            return pltpu.make_async_remote_copy(
                src_ref=p_refs[w].at[peer_lin], dst_ref=out_refs[w].at[peer_lin],
                send_sem=send_sems.at[w, k - 1], recv_sem=recv_sems.at[w, k - 1],
                device_id=peer, device_id_type=MESH)

        mine, sends = [], []
        for w in range(nw):
            cp = pltpu.make_async_copy(p_refs[w].at[me], out_refs[w].at[me], local_sems.at[w])
            cp.start()
            mine.append(cp)
            for k, peer, peer_lin in peers:
                cp = copy(w, k, peer, peer_lin)
                cp.start()
                sends.append(cp)
        for w in range(nw):
            for k, peer, peer_lin in peers:
                arrival(w, k, peer, peer_lin).wait_recv()
        for cp in sends:
            cp.wait_send()
        for cp in mine:
            cp.wait()

    anyspec = pl.BlockSpec(memory_space=pl.ANY)
    return pl.pallas_call(
        body, name=name,
        in_specs=[anyspec] * nw, out_specs=[anyspec] * nw,
        out_shape=[jax.ShapeDtypeStruct(p.shape, p.dtype) for p in parts],
        scratch_shapes=[pltpu.SemaphoreType.DMA((nw, 7)), pltpu.SemaphoreType.DMA((nw, 7)),
                        pltpu.SemaphoreType.DMA((nw,))],
    )(*parts)


def _adamw(w, g, m, v):
    m = ADAM_B1 * m + (1.0 - ADAM_B1) * g
    v = ADAM_B2 * v + (1.0 - ADAM_B2) * (g * g)
    m_hat = m / (1.0 - ADAM_B1 ** ADAM_STEP)
    v_hat = v / (1.0 - ADAM_B2 ** ADAM_STEP)
    delta = -ADAM_LR * (m_hat / (jnp.sqrt(v_hat) + ADAM_EPS) + ADAM_WD * w)
    return delta, m, v


def _sum_adam(parts, w, m, v, name):
    _, r, c = parts.shape
    tr = _pick(r, 128, 16)

    def body(p_ref, w_ref, m_ref, v_ref, g_ref, d_ref, mo_ref, vo_ref):
        g = p_ref[0].astype(F32)
        for s in range(1, N_DEV):
            g = g + p_ref[s].astype(F32)
        g_ref[...] = g
        d_ref[...], mo_ref[...], vo_ref[...] = _adamw(w_ref[...], g, m_ref[...], v_ref[...])

    row = pl.BlockSpec((tr, c), lambda i: (i, 0))
    sh = jax.ShapeDtypeStruct((r, c), F32)
    return pl.pallas_call(
        body, name=name, grid=(r // tr,),
        in_specs=[pl.BlockSpec((N_DEV, tr, c), lambda i: (0, i, 0)), row, row, row],
        out_specs=[row, row, row, row], out_shape=[sh, sh, sh, sh],
        compiler_params=_params(("parallel",)),
    )(parts, w, m, v)


def _sum_slots(parts, name):
    _, r, c = parts.shape
    tr = _pick(r, 512, SUBLANE)

    def body(p_ref, o_ref):
        g = p_ref[0]
        for s in range(1, N_DEV):
            g = g + p_ref[s]
        o_ref[...] = g

    return pl.pallas_call(
        body, name=name, grid=(r // tr,),
        in_specs=[pl.BlockSpec((N_DEV, tr, c), lambda i: (0, i, 0))],
        out_specs=pl.BlockSpec((tr, c), lambda i: (i, 0)), out_shape=jax.ShapeDtypeStruct((r, c), F32),
        compiler_params=_params(("parallel",)),
    )(parts)


def _adam_rows(g, w, m, v, name):
    r, c = g.shape
    tr = _pick(r, 512, SUBLANE)

    def body(g_ref, w_ref, m_ref, v_ref, d_ref, mo_ref, vo_ref):
        d_ref[...], mo_ref[...], vo_ref[...] = _adamw(w_ref[...], g_ref[...], m_ref[...], v_ref[...])

    row = pl.BlockSpec((tr, c), lambda i: (i, 0))
    sh = jax.ShapeDtypeStruct((r, c), F32)
    return pl.pallas_call(body, name=name, grid=(r // tr,), in_specs=[row] * 4, out_specs=[row] * 3,
                          out_shape=[sh, sh, sh], compiler_params=_params(("parallel",)))(g, w, m, v)


def _pack(arrays):
    flat = jnp.concatenate([a.reshape(-1).astype(F32) for a in arrays])
    pad = (-flat.shape[0]) % (SUBLANE * LANE)
    return jnp.pad(flat, (0, pad)).reshape(-1, LANE)


def _unpack(packed, shapes):
    flat = packed.reshape(-1)
    out, off = [], 0
    for s in shapes:
        n = math.prod(s)
        out.append(flat[off:off + n].reshape(s))
        off += n
    return out


def _block_diag(t):
    eye = jnp.eye(S5_SUPER, dtype=bool)
    bd = jnp.where(eye[None, :, None, :, None], t[:, :, :, None, :], 0.0)
    return bd.reshape(S5_SUPER, S5_SUPER * t.shape[2], S5_SUPER * t.shape[3])


def _diag_blocks(dense, a, b):
    x = dense.reshape(S5_SUPER, S5_SUPER, a, S5_SUPER, b)
    return jnp.moveaxis(jnp.diagonal(x, axis1=1, axis2=3), -1, 1)


def _s5_layouts(b_re, b_im, c_re, c_im, d):
    g2 = (S5_GROUPS // S5_SUPER, S5_SUPER)
    bt = lambda b: _block_diag(b.reshape(*g2, S5_STATE, S5_GROUP).transpose(0, 1, 3, 2))
    ct = lambda c: _block_diag(c.reshape(*g2, S5_GROUP, S5_STATE).transpose(0, 1, 3, 2))
    bsg = jnp.concatenate([bt(b_re), bt(b_im)], axis=2).astype(BF16)
    ccat = jnp.concatenate([ct(c_re), -ct(c_im)], axis=1).astype(BF16)
    return bsg, ccat, d.reshape(S5_GROUPS // S5_SUPER, 1, LANE)


def _s5_param_grads(gb, gc):
    n = S5_LANES
    gb_re = _diag_blocks(gb[:, :, 0:n], S5_GROUP, S5_STATE).transpose(0, 1, 3, 2).reshape(S5_GROUPS, S5_STATE, S5_GROUP)
    gb_im = _diag_blocks(gb[:, :, n:2 * n], S5_GROUP, S5_STATE).transpose(0, 1, 3, 2).reshape(S5_GROUPS, S5_STATE, S5_GROUP)
    gc_re = _diag_blocks(gc[:, 0:n, :], S5_STATE, S5_GROUP).transpose(0, 1, 3, 2).reshape(S5_GROUPS, S5_GROUP, S5_STATE)
    gc_im = -_diag_blocks(gc[:, n:2 * n, :], S5_STATE, S5_GROUP).transpose(0, 1, 3, 2).reshape(S5_GROUPS, S5_GROUP, S5_STATE)
    return gb_re, gb_im, gc_re, gc_im


def _local_step(x, target, wts, small):
    w_in, w_glu, w_ps, w_ph, w_out, w_up, w_down = wts
    sp = small
    a_re, a_im = sp["s5_a_re"], sp["s5_a_im"]
    ldt = sp["s5_log_dt"].reshape(S5_GROUPS, 1)
    lanes = lambda a: a.reshape(S5_GROUPS // S5_SUPER, 1, S5_LANES)

    h1 = _rms_fwd(x, sp["ln_mix_g"], "rms_mix")
    proj = _mm_nn(h1, w_in, "mm_in")
    disc = _s5_param_fwd(a_re, a_im, ldt)
    bsg, ccat, dvec = _s5_layouts(sp["s5_b_re"], sp["s5_b_im"], sp["s5_c_re"], sp["s5_c_im"], sp["s5_d"])
    disc_l = [lanes(a) for a in disc]
    y, sb = _s5_fwd(proj, bsg, ccat, dvec, *disc_l)
    z16 = _gelu_fwd(y)
    gl = _mm_nn(z16, w_glu, "mm_glu")
    z2 = _glu_fwd(y, gl, sp["s5_b_glu"])
    ys = _mm_nn(z2, w_ps, "mm_proj_s5")
    o_raw, oh, s0s = _hgrn_fwd(proj, sp["hgrn_lb_logits"], sp["hgrn_norm_g"])
    yh = _mm_nn(oh, w_ph, "mm_proj_hgrn")
    merged = _merge_fwd(proj, ys, yh)
    x1 = _mm_nn(merged, w_out, "mm_out", res=x)
    h2 = _rms_fwd(x1, sp["ln_ffn_g"], "rms_ffn")
    up = _mm_nn(h2, w_up, "mm_up")
    act = _ffn_act_fwd(up, sp["conv_w"], sp["conv_b"])
    x2 = _mm_nn(act, w_down, "mm_down", res=x1)
    dx2, dx2_16, g_ln_final, loss = _loss_head(x2, sp["ln_final_g"], target)

    dact = _mm_nt(dx2_16, w_down, "mm_down_dx")
    g_w_down = _mm_tn(act, dx2_16, 1, "mm_down_dw")
    dup_g, dup_v, dcw_g, dcw_v, dcb_g, dcb_v = _ffn_act_bwd(up, dact, sp["conv_w"], sp["conv_b"])
    dup = jnp.concatenate([dup_g, dup_v], axis=1)
    g_conv_w = jnp.concatenate([dcw_g, dcw_v], axis=1)
    g_conv_b = jnp.concatenate([dcb_g, dcb_v], axis=1)
    dh2 = _mm_nt(dup, w_up, "mm_up_dx")
    g_w_up = _mm_tn(h2, dup, N_DEV, "mm_up_dw")
    dx1, dx1_16, g_ln_ffn = _rms_bwd(x1, sp["ln_ffn_g"], dh2, dx2, "rms_ffn_bwd", True)

    dmerged = _mm_nt(dx1_16, w_out, "mm_out_dx")
    g_w_out = _mm_tn(merged, dx1_16, 1, "mm_out_dw")
    dys, dyh, dgs, dgh = _merge_bwd(proj, ys, yh, dmerged)
    doh = _mm_nt(dyh, w_ph, "mm_proj_hgrn_dx")
    g_w_ph = _mm_tn(oh, dyh, N_DEV, "mm_proj_hgrn_dw")
    dz2 = _mm_nt(dys, w_ps, "mm_proj_s5_dx")
    g_w_ps = _mm_tn(z2, dys, N_DEV, "mm_proj_s5_dw")
    dgl, dza, g_b_glu = _glu_bwd(y, gl, sp["s5_b_glu"], dz2)
    dzb = _mm_nt(dgl, w_glu, "mm_glu_dx")
    g_w_glu = _mm_tn(z16, dgl, 1, "mm_glu_dw")
    dy = _gelu_bwd(y, dza, dzb)
    du, gb, gc, gd, gar, gai, gcr, gci = _s5_bwd(proj, dy, sb, bsg, ccat, dvec, *disc_l)
    flat = lambda a: a.reshape(S5_GROUPS, S5_STATE)
    g_a_re, g_a_im, g_ldt = _s5_param_bwd(a_re, a_im, ldt, [flat(a) for a in (gar, gai, gcr, gci)])
    g_b_re, g_b_im, g_c_re, g_c_im = _s5_param_grads(gb, gc)
    dq, dz, dv, dg, g_norm, dlb = _hgrn_bwd(proj, o_raw, s0s, doh, sp["hgrn_lb_logits"], sp["hgrn_norm_g"])
    g_logits = _lb_bwd(sp["hgrn_lb_logits"], dlb)

    dproj = jnp.concatenate([du, dq, dz, dv, dg, dgs, dgh], axis=1)
    dh1 = _mm_nt(dproj, w_in, "mm_in_dx")
    g_w_in = _mm_tn(h1, dproj, N_DEV, "mm_in_dw")
    grad_x, g_ln_mix = _rms_bwd(x, sp["ln_mix_g"], dh1, dx1, "rms_mix_bwd", False)

    big = dict(w_in=g_w_in, s5_w_glu=g_w_glu, w_proj_s5=g_w_ps, w_proj_hgrn=g_w_ph, w_out=g_w_out,
               w_up=g_w_up, w_down=g_w_down)
    small_g = dict(ln_mix_g=g_ln_mix, s5_a_re=g_a_re, s5_a_im=g_a_im, s5_log_dt=g_ldt.reshape(1, S5_GROUPS),
                   s5_b_re=g_b_re, s5_b_im=g_b_im, s5_c_re=g_c_re, s5_c_im=g_c_im,
                   s5_d=gd.reshape(S5_GROUPS, S5_GROUP), s5_b_glu=g_b_glu, hgrn_lb_logits=g_logits,
                   hgrn_norm_g=g_norm, ln_ffn_g=g_ln_ffn, conv_w=g_conv_w, conv_b=g_conv_b, ln_final_g=g_ln_final)
    return loss, grad_x, big, small_g


BIG = ("w_in", "s5_w_glu", "w_proj_s5", "w_proj_hgrn", "w_out", "w_up", "w_down")
COL_SHARDED = ("w_in", "w_proj_s5", "w_proj_hgrn", "w_up")
SMALL = ("ln_mix_g", "s5_a_re", "s5_a_im", "s5_log_dt", "s5_b_re", "s5_b_im", "s5_c_re", "s5_c_im", "s5_d",
         "s5_b_glu", "hgrn_lb_logits", "hgrn_norm_g", "ln_ffn_g", "conv_b", "ln_final_g")
WEIGHTS = ("ln_mix_g", "w_in", "s5_a_re", "s5_a_im", "s5_log_dt", "s5_b_re", "s5_b_im", "s5_c_re", "s5_c_im", "s5_d",
           "s5_w_glu", "s5_b_glu", "w_proj_s5", "hgrn_lb_logits", "hgrn_norm_g", "w_proj_hgrn", "w_out", "ln_ffn_g",
           "w_up", "conv_w", "conv_b", "w_down", "ln_final_g")


def kernel(x, ln_mix_g, w_in, s5_a_re, s5_a_im, s5_log_dt, s5_b_re, s5_b_im, s5_c_re, s5_c_im, s5_d, s5_w_glu, s5_b_glu, w_proj_s5, hgrn_lb_logits, hgrn_norm_g, w_proj_hgrn, w_out, ln_ffn_g, w_up, conv_w, conv_b, w_down, ln_final_g, loss_target, m_ln_mix_g, m_w_in, m_s5_a_re, m_s5_a_im, m_s5_log_dt, m_s5_b_re, m_s5_b_im, m_s5_c_re, m_s5_c_im, m_s5_d, m_s5_w_glu, m_s5_b_glu, m_w_proj_s5, m_hgrn_lb_logits, m_hgrn_norm_g, m_w_proj_hgrn, m_w_out, m_ln_ffn_g, m_w_up, m_conv_w, m_conv_b, m_w_down, m_ln_final_g, v_ln_mix_g, v_w_in, v_s5_a_re, v_s5_a_im, v_s5_log_dt, v_s5_b_re, v_s5_b_im, v_s5_c_re, v_s5_c_im, v_s5_d, v_s5_w_glu, v_s5_b_glu, v_w_proj_s5, v_hgrn_lb_logits, v_hgrn_norm_g, v_w_proj_hgrn, v_w_out, v_ln_ffn_g, v_w_up, v_conv_w, v_conv_b, v_w_down, v_ln_final_g):
    given = dict(locals())
    w = {n: given[n] for n in WEIGHTS}
    mom = {n: given["m_" + n] for n in WEIGHTS}
    var = {n: given["v_" + n] for n in WEIGHTS}

    shards = [w[n][0].astype(BF16) for n in BIG]
    gathered = _all_gather(shards, "gather_weights")
    wts = [g if n in COL_SHARDED else g.reshape(1, N_DEV * g.shape[1], g.shape[2]) for n, g in zip(BIG, gathered)]

    small = dict(ln_mix_g=ln_mix_g, s5_a_re=s5_a_re[0], s5_a_im=s5_a_im[0], s5_log_dt=s5_log_dt,
                 s5_b_re=s5_b_re[0], s5_b_im=s5_b_im[0], s5_c_re=s5_c_re[0], s5_c_im=s5_c_im[0], s5_d=s5_d[0],
                 s5_b_glu=s5_b_glu, hgrn_lb_logits=hgrn_lb_logits, hgrn_norm_g=hgrn_norm_g, ln_ffn_g=ln_ffn_g,
                 conv_w=None, conv_b=conv_b, ln_final_g=ln_final_g.reshape(1, D_MODEL))
    conv_w_full = _all_gather([conv_w[0]], "gather_conv_w")[0]
    small["conv_w"] = conv_w_full.transpose(1, 0, 2).reshape(3, 2 * D_FF)
    loss, grad_x, big_g, small_g = _local_step(x[0], loss_target[0], wts, small)

    parts = [big_g[n] if n in COL_SHARDED else big_g[n].reshape(N_DEV, -1, big_g[n].shape[2]) for n in BIG]
    received = _scatter_blocks(parts, "scatter_grads")
    grads, delta, new_m, new_v = {}, {}, {}, {}
    for n, r in zip(BIG, received):
        g, d, m2, v2 = _sum_adam(r, w[n][0], mom[n][0], var[n][0], "adam_" + n)
        grads[n], delta[n], new_m[n], new_v[n] = g[None], d[None], m2[None], v2[None]

    names = SMALL + ("conv_w",)
    shapes = [small_g[n].shape for n in names] + [(1,)]
    packed = _pack([small_g[n] for n in names] + [loss[0, 0:1]])
    total = _sum_slots(_all_gather([packed], "gather_small")[0], "sum_small")
    summed = dict(zip(names + ("loss",), _unpack(total, shapes)))

    pw = _pack([w[n] for n in SMALL])
    d_s, m_s, v_s = _adam_rows(_pack([summed[n] for n in SMALL]), pw, _pack([mom[n] for n in SMALL]),
                               _pack([var[n] for n in SMALL]), "adam_small")
    wshapes = [w[n].shape for n in SMALL]
    for n, g, d, m2, v2 in zip(SMALL, [summed[n] for n in SMALL], _unpack(d_s, wshapes), _unpack(m_s, wshapes),
                               _unpack(v_s, wshapes)):
        grads[n], delta[n], new_m[n], new_v[n] = g.reshape(w[n].shape), d, m2, v2
    me = 4 * lax.axis_index("x") + 2 * lax.axis_index("y") + lax.axis_index("c")
    ncol = conv_w.shape[2]
    g_cw = lax.dynamic_slice_in_dim(summed["conv_w"], me * ncol, ncol, axis=1)
    d_cw, m_cw, v_cw = _adam_rows(g_cw, conv_w[0], m_conv_w[0], v_conv_w[0], "adam_conv_w")
    grads["conv_w"], delta["conv_w"], new_m["conv_w"], new_v["conv_w"] = g_cw[None], d_cw[None], m_cw[None], v_cw[None]

    return (summed["loss"].reshape(()), grad_x[None], *[grads[n] for n in WEIGHTS], *[delta[n] for n in WEIGHTS],
            *[new_m[n] for n in WEIGHTS], *[new_v[n] for n in WEIGHTS])
```

```python
import math

import jax
import jax.numpy as jnp
from jax import lax
from jax.experimental import pallas as pl
from jax.experimental.pallas import tpu as pltpu

F32 = jnp.float32
BF16 = jnp.bfloat16

N_DEV = 8
D_MODEL = 2048
S5_WIDTH = 1024
S5_GROUP = 16
S5_GROUPS = 64
S5_STATE = 64
S5_MAX_RE = -1e-4
S5_SUPER = 8
S5_LANES = S5_SUPER * S5_STATE
HGRN_WIDTH = 1024
HGRN_HEADS = 8
HGRN_DH = 128
HGRN_CHUNK = 64
D_FF = 5632
RMS_EPS = 1e-6
ADAM_LR = 0.001
ADAM_B1 = 0.9
ADAM_B2 = 0.999
ADAM_EPS = 1e-08
ADAM_WD = 0.01
ADAM_STEP = 10

LANE = 128
SUBLANE = 8
VMEM_LIMIT = 48 * 1024 * 1024
MESH = pl.DeviceIdType.MESH
GELU_C = math.sqrt(2.0 / math.pi)
GELU_A = 0.044715


def _params(sem=None):
    return pltpu.CompilerParams(dimension_semantics=sem, vmem_limit_bytes=VMEM_LIMIT)


def _pick(n, cap, unit=LANE):
    best = None
    for t in range(unit, min(n, cap) + 1, unit):
        if n % t == 0:
            best = t
    return best if best is not None else n


def _ordered(body, in_specs, args, after):
    if after is None:
        return body, list(in_specs), list(args)
    n_in = len(args)

    def ordered_body(*refs):
        return body(*refs[:n_in], *refs[n_in + 1:])

    return ordered_body, [*in_specs, pl.BlockSpec(memory_space=pl.ANY)], [*args, after]


def _sigmoid(x):
    return 1.0 / (1.0 + jnp.exp(-x))


def _silu_and_grad(x):
    s = _sigmoid(x)
    return x * s, s * (1.0 + x * (1.0 - s))


def _gelu_and_grad(y):
    inner = GELU_C * (y + GELU_A * y * y * y)
    th = jnp.tanh(inner)
    val = 0.5 * y * (1.0 + th)
    grad = 0.5 * (1.0 + th) + 0.5 * y * (1.0 - th * th) * GELU_C * (1.0 + 3.0 * GELU_A * y * y)
    return val, grad


def _dot(a, b):
    return jnp.dot(a, b, preferred_element_type=F32)


def _dot_nt(a, b):
    return lax.dot_general(a, b, (((1,), (1,)), ((), ())), preferred_element_type=F32)


def _dot_tn(a, b):
    return lax.dot_general(a, b, (((0,), (0,)), ((), ())), preferred_element_type=F32)


def _mm_nn(a, w, name, res=None, out_dtype=F32):
    m, kdim = a.shape
    nb, _, ns = w.shape
    tm, tk, tn = _pick(m, 512), _pick(kdim, 2048), _pick(ns, 1536)
    npb, nk = ns // tn, kdim // tk

    def body(*refs):
        a_ref, w_ref = refs[0], refs[1]
        r_ref = refs[2] if res is not None else None
        o_ref = refs[3] if res is not None else refs[2]

        def finish(r):
            if res is not None:
                r = r + r_ref[...]
            o_ref[...] = r.astype(out_dtype)

        if nk == 1:
            finish(_dot(a_ref[...], w_ref[...]))
            return
        acc = refs[-1]
        k = pl.program_id(2)

        @pl.when(k == 0)
        def _():
            acc[...] = jnp.zeros_like(acc)

        acc[...] += _dot(a_ref[...], w_ref[...])

        @pl.when(k == nk - 1)
        def _():
            finish(acc[...])

    in_specs = [pl.BlockSpec((tm, tk), lambda j, i, k: (i, k)),
                pl.BlockSpec((None, tk, tn), lambda j, i, k: (j // npb, k, j % npb))]
    args = [a, w]
    if res is not None:
        in_specs.append(pl.BlockSpec((tm, tn), lambda j, i, k: (i, j)))
        args.append(res)
    return pl.pallas_call(
        body, name=name, grid=(nb * npb, m // tm, nk),
        in_specs=in_specs, out_specs=pl.BlockSpec((tm, tn), lambda j, i, k: (i, j)),
        out_shape=jax.ShapeDtypeStruct((m, nb * ns), out_dtype),
        scratch_shapes=[pltpu.VMEM((tm, tn), F32)] if nk > 1 else [],
        compiler_params=_params(("parallel", "parallel", "arbitrary")),
    )(*args)


def _mm_nt(a, w, name, out_dtype=F32, after=None):
    m, _ = a.shape
    nb, kdim, ns = w.shape
    tm, tko, tn = _pick(m, 512), _pick(kdim, 1024), _pick(ns, 1536)
    npb = ns // tn
    nred = nb * npb

    def body(a_ref, w_ref, o_ref, *scratch):
        if nred == 1:
            o_ref[...] = _dot_nt(a_ref[...], w_ref[...]).astype(out_dtype)
            return
        acc = scratch[0]
        n = pl.program_id(2)

        @pl.when(n == 0)
        def _():
            acc[...] = jnp.zeros_like(acc)

        acc[...] += _dot_nt(a_ref[...], w_ref[...])

        @pl.when(n == nred - 1)
        def _():
            o_ref[...] = acc[...].astype(out_dtype)

    in_specs = [pl.BlockSpec((tm, tn), lambda i, j, n: (i, n)),
                pl.BlockSpec((None, tko, tn), lambda i, j, n: (n // npb, j, n % npb))]
    body, in_specs, args = _ordered(body, in_specs, [a, w], after)
    return pl.pallas_call(
        body, name=name, grid=(m // tm, kdim // tko, nred),
        in_specs=in_specs,
        out_specs=pl.BlockSpec((tm, tko), lambda i, j, n: (i, j)),
        out_shape=jax.ShapeDtypeStruct((m, kdim), out_dtype),
        scratch_shapes=[pltpu.VMEM((tm, tko), F32)] if nred > 1 else [],
        compiler_params=_params(("parallel", "parallel", "arbitrary")),
    )(*args)


def _mm_tn(a, d, nb, name, out_dtype=BF16):
    m, kdim = a.shape
    ns = d.shape[1] // nb
    tm, tko, tn = _pick(m, 4096), _pick(kdim, 512), _pick(ns, 1536)
    npb, nm = ns // tn, m // tm

    def body(a_ref, d_ref, o_ref, *scratch):
        if nm == 1:
            o_ref[...] = _dot_tn(a_ref[...], d_ref[...]).astype(out_dtype)
            return
        acc = scratch[0]
        r = pl.program_id(2)

        @pl.when(r == 0)
        def _():
            acc[...] = jnp.zeros_like(acc)

        acc[...] += _dot_tn(a_ref[...], d_ref[...])

        @pl.when(r == nm - 1)
        def _():
            o_ref[...] = acc[...].astype(out_dtype)

    return pl.pallas_call(
        body, name=name, grid=(nb * npb, kdim // tko, nm),
        in_specs=[pl.BlockSpec((tm, tko), lambda j, i, r: (r, i)),
                  pl.BlockSpec((tm, tn), lambda j, i, r: (r, j))],
        out_specs=pl.BlockSpec((None, tko, tn), lambda j, i, r: (j // npb, i, j % npb)),
        out_shape=jax.ShapeDtypeStruct((nb, kdim, ns), out_dtype),
        scratch_shapes=[pltpu.VMEM((tko, tn), F32)] if nm > 1 else [],
        compiler_params=_params(("parallel", "parallel", "arbitrary")),
    )(a, d)


def _rms_fwd(x, g, name, after=None):
    t, d = x.shape
    tr = _pick(t, 256, SUBLANE)

    def body(x_ref, g_ref, h_ref):
        xv = x_ref[...]
        r = lax.rsqrt(jnp.mean(xv * xv, axis=-1, keepdims=True) + RMS_EPS)
        h_ref[...] = (xv * r * g_ref[...]).astype(BF16)

    in_specs = [pl.BlockSpec((tr, d), lambda i: (i, 0)), pl.BlockSpec((1, d), lambda i: (0, 0))]
    body, in_specs, args = _ordered(body, in_specs, [x, g], after)
    return pl.pallas_call(
        body, name=name, grid=(t // tr,),
        in_specs=in_specs,
        out_specs=pl.BlockSpec((tr, d), lambda i: (i, 0)),
        out_shape=jax.ShapeDtypeStruct((t, d), BF16),
        compiler_params=_params(("parallel",)),
    )(*args)


def _rms_bwd(x, g, dh, add, name, want_bf16, after=None):
    t, d = x.shape
    tr = _pick(t, 256, SUBLANE)

    def body(x_ref, g_ref, dh_ref, add_ref, *outs):
        if want_bf16:
            dx_ref, dxb_ref, dg_ref = outs
        else:
            dx_ref, dg_ref = outs
        i = pl.program_id(0)

        @pl.when(i == 0)
        def _():
            dg_ref[...] = jnp.zeros_like(dg_ref)

        xv, dhv = x_ref[...], dh_ref[...]
        r = lax.rsqrt(jnp.mean(xv * xv, axis=-1, keepdims=True) + RMS_EPS)
        xh = xv * r
        dg_ref[...] += jnp.sum(dhv * xh, axis=0, keepdims=True)
        dxh = dhv * g_ref[...]
        dx = add_ref[...] + r * (dxh - xh * jnp.mean(dxh * xh, axis=-1, keepdims=True))
        dx_ref[...] = dx
        if want_bf16:
            dxb_ref[...] = dx.astype(BF16)

    row = pl.BlockSpec((tr, d), lambda i: (i, 0))
    vec = pl.BlockSpec((1, d), lambda i: (0, 0))
    out_specs = [row] + ([row] if want_bf16 else []) + [vec]
    out_shape = ([jax.ShapeDtypeStruct((t, d), F32)] + ([jax.ShapeDtypeStruct((t, d), BF16)] if want_bf16 else [])
                 + [jax.ShapeDtypeStruct((1, d), F32)])
    body, in_specs, args = _ordered(body, [row, vec, row, row], [x, g, dh, add], after)
    return pl.pallas_call(
        body, name=name, grid=(t // tr,),
        in_specs=in_specs, out_specs=out_specs, out_shape=out_shape,
        compiler_params=_params(("arbitrary",)),
    )(*args)


def _loss_head(x2, g, target, name="loss_head"):
    t, d = x2.shape
    tr = _pick(t, 256, SUBLANE)

    def body(x_ref, g_ref, t_ref, dx_ref, dxb_ref, dg_ref, loss_ref):
        i = pl.program_id(0)

        @pl.when(i == 0)
        def _():
            dg_ref[...] = jnp.zeros_like(dg_ref)
            loss_ref[...] = jnp.zeros_like(loss_ref)

        xv = x_ref[...]
        gv = g_ref[...]
        r = lax.rsqrt(jnp.mean(xv * xv, axis=-1, keepdims=True) + RMS_EPS)
        xh = xv * r
        err = xh * gv - t_ref[...]
        part = 0.5 * jnp.sum(jnp.mean(err * err, axis=-1, keepdims=True), axis=0, keepdims=True)
        loss_ref[...] += jnp.broadcast_to(part, loss_ref.shape)
        dy = err * (1.0 / d)
        dg_ref[...] += jnp.sum(dy * xh, axis=0, keepdims=True)
        dxh = dy * gv
        dx = r * (dxh - xh * jnp.mean(dxh * xh, axis=-1, keepdims=True))
        dx_ref[...] = dx
        dxb_ref[...] = dx.astype(BF16)

    row = pl.BlockSpec((tr, d), lambda i: (i, 0))
    vec = pl.BlockSpec((1, d), lambda i: (0, 0))
    return pl.pallas_call(
        body, name=name, grid=(t // tr,),
        in_specs=[row, vec, row],
        out_specs=[row, row, vec, pl.BlockSpec((1, LANE), lambda i: (0, 0))],
        out_shape=[jax.ShapeDtypeStruct((t, d), F32), jax.ShapeDtypeStruct((t, d), BF16),
                   jax.ShapeDtypeStruct((1, d), F32), jax.ShapeDtypeStruct((1, LANE), F32)],
        compiler_params=_params(("arbitrary",)),
    )(x2, g, target)


def _s5_discretize(a_re, a_im, ldt):
    lam_re = jnp.minimum(a_re, S5_MAX_RE)
    lam_im = a_im
    dt = jnp.exp(ldt)
    mag = jnp.exp(lam_re * dt)
    abar_re = mag * jnp.cos(lam_im * dt)
    abar_im = mag * jnp.sin(lam_im * dt)
    den = lam_re * lam_re + lam_im * lam_im
    nr = abar_re - 1.0
    ni = abar_im
    coef_re = (nr * lam_re + ni * lam_im) / den
    coef_im = (ni * lam_re - nr * lam_im) / den
    return abar_re, abar_im, coef_re, coef_im


def _s5_param_fwd(a_re, a_im, ldt):
    def body(ar_ref, ai_ref, l_ref, o0, o1, o2, o3):
        outs = _s5_discretize(ar_ref[...], ai_ref[...], l_ref[...])
        for o, v in zip((o0, o1, o2, o3), outs):
            o[...] = v

    sh = jax.ShapeDtypeStruct(a_re.shape, F32)
    return pl.pallas_call(body, name="s5_param_fwd", out_shape=[sh, sh, sh, sh], compiler_params=_params())(a_re, a_im, ldt)


def _s5_param_bwd(a_re, a_im, ldt, cts):
    def body(ar_ref, ai_ref, l_ref, c0, c1, c2, c3, g0, g1, g2):
        _, vjp = jax.vjp(_s5_discretize, ar_ref[...], ai_ref[...], l_ref[...])
        ga, gb, gl = vjp((c0[...], c1[...], c2[...], c3[...]))
        g0[...] = ga
        g1[...] = gb
        g2[...] = gl

    sh = jax.ShapeDtypeStruct(a_re.shape, F32)
    return pl.pallas_call(body, name="s5_param_bwd", out_shape=[sh, sh, jax.ShapeDtypeStruct(ldt.shape, F32)],
                          compiler_params=_params())(a_re, a_im, ldt, *cts)


def _cmul(ar, ai, br, bi):
    return ar * br - ai * bi, ar * bi + ai * br


def _s5_tables(ar, ai, reverse):
    a1 = (ar, ai)
    a2 = _cmul(*a1, *a1)
    a4 = _cmul(*a2, *a2)
    pows = [a1]
    for _ in range(SUBLANE - 1):
        pows.append(_cmul(*pows[-1], *a1))
    row = lax.broadcasted_iota(jnp.int32, (SUBLANE, ar.shape[1]), 0)
    tr = jnp.zeros((SUBLANE, ar.shape[1]), F32)
    ti = jnp.zeros((SUBLANE, ar.shape[1]), F32)
    for r in range(SUBLANE):
        p = pows[SUBLANE - 1 - r] if reverse else pows[r]
        tr = jnp.where(row == r, p[0], tr)
        ti = jnp.where(row == r, p[1], ti)
    return (a1, a2, a4), (tr, ti), row


def _s5_block_scan(xr, xi, pw, table, row, kr, ki):
    for k, (pr, pi) in zip((1, 2, 4), pw):
        sr = jnp.where(row >= k, pltpu.roll(xr, k, 0), 0.0)
        si = jnp.where(row >= k, pltpu.roll(xi, k, 0), 0.0)
        xr, xi = xr + pr * sr - pi * si, xi + pr * si + pi * sr
    tr, ti = table
    return xr + tr * kr - ti * ki, xi + tr * ki + ti * kr


def _s5_block_scan_rev(xr, xi, pw, table, row, kr, ki):
    for k, (pr, pi) in zip((1, 2, 4), pw):
        sr = jnp.where(row < SUBLANE - k, pltpu.roll(xr, SUBLANE - k, 0), 0.0)
        si = jnp.where(row < SUBLANE - k, pltpu.roll(xi, SUBLANE - k, 0), 0.0)
        xr, xi = xr + pr * sr + pi * si, xi + pr * si - pi * sr
    tr, ti = table
    return xr + tr * kr + ti * ki, xi + tr * ki - ti * kr


def _s5_fwd(proj, bsg, ccat, dvec, abar_re, abar_im, coef_re, coef_im):
    t = proj.shape[0]
    tc = _pick(t, 512, SUBLANE)
    n_chunks, nblk, n = t // tc, tc // SUBLANE, S5_LANES

    def body(u_ref, b_ref, c_ref, d_ref, ar_ref, ai_ref, cr_ref, ci_ref, y_ref, sb_ref, bu, st, car):
        @pl.when(pl.program_id(1) == 0)
        def _():
            car[...] = jnp.zeros_like(car)

        sb_ref[...] = car[...]
        u = u_ref[...]
        bu[...] = _dot(u.astype(BF16), b_ref[...])
        cr, ci = cr_ref[...], ci_ref[...]
        pw, table, row = _s5_tables(ar_ref[...], ai_ref[...], reverse=False)

        def blk(i, carry):
            r0 = pl.multiple_of(i * SUBLANE, SUBLANE)
            br, bi = bu[pl.ds(r0, SUBLANE), 0:n], bu[pl.ds(r0, SUBLANE), n:2 * n]
            xr, xi = _cmul(cr, ci, br, bi)
            xr, xi = _s5_block_scan(xr, xi, pw, table, row, *carry)
            st[pl.ds(r0, SUBLANE), 0:n] = xr
            st[pl.ds(r0, SUBLANE), n:2 * n] = xi
            return xr[SUBLANE - 1:SUBLANE, :], xi[SUBLANE - 1:SUBLANE, :]

        kr, ki = lax.fori_loop(0, nblk, blk, (car[:, 0:n], car[:, n:2 * n]))
        car[:, 0:n] = kr
        car[:, n:2 * n] = ki
        y_ref[...] = _dot(st[...].astype(BF16), c_ref[...]) + d_ref[...] * u

    vec = pl.BlockSpec((None, 1, n), lambda s, c: (s, 0, 0))
    return pl.pallas_call(
        body, name="s5_fwd", grid=(S5_SUPER, n_chunks),
        in_specs=[pl.BlockSpec((tc, LANE), lambda s, c: (c, s)),
                  pl.BlockSpec((None, LANE, 2 * n), lambda s, c: (s, 0, 0)),
                  pl.BlockSpec((None, 2 * n, LANE), lambda s, c: (s, 0, 0)),
                  pl.BlockSpec((None, 1, LANE), lambda s, c: (s, 0, 0)),
                  vec, vec, vec, vec],
        out_specs=[pl.BlockSpec((tc, LANE), lambda s, c: (c, s)),
                   pl.BlockSpec((None, None, 1, 2 * n), lambda s, c: (s, c, 0, 0))],
        out_shape=[jax.ShapeDtypeStruct((t, S5_WIDTH), F32),
                   jax.ShapeDtypeStruct((S5_SUPER, n_chunks, 1, 2 * n), F32)],
        scratch_shapes=[pltpu.VMEM((tc, 2 * n), F32), pltpu.VMEM((tc, 2 * n), F32), pltpu.VMEM((1, 2 * n), F32)],
        compiler_params=_params(("parallel", "arbitrary")),
    )(proj, bsg, ccat, dvec, abar_re, abar_im, coef_re, coef_im)


def _s5_bwd(proj, dy, sb, bsg, ccat, dvec, abar_re, abar_im, coef_re, coef_im):
    t = proj.shape[0]
    tc = _pick(t, 512, SUBLANE)
    n_chunks, nblk, n = t // tc, tc // SUBLANE, S5_LANES

    def body(u_ref, dy_ref, sb_ref, b_ref, c_ref, d_ref, ar_ref, ai_ref, cr_ref, ci_ref,
             du_ref, gb_ref, gc_ref, gd_ref, gar_ref, gai_ref, gcr_ref, gci_ref,
             bu, st, sp, gbu, gcar, acc):
        @pl.when(pl.program_id(1) == 0)
        def _():
            gcar[...] = jnp.zeros_like(gcar)
            acc[...] = jnp.zeros_like(acc)
            gb_ref[...] = jnp.zeros_like(gb_ref)
            gc_ref[...] = jnp.zeros_like(gc_ref)
            gd_ref[...] = jnp.zeros_like(gd_ref)

        u = u_ref[...]
        dyv = dy_ref[...]
        u16, dy16 = u.astype(BF16), dyv.astype(BF16)
        bu[...] = _dot(u16, b_ref[...])
        ar, ai = ar_ref[...], ai_ref[...]
        cr, ci = cr_ref[...], ci_ref[...]
        pw, table, row = _s5_tables(ar, ai, reverse=False)

        def fblk(i, carry):
            kr, ki = carry
            r0 = pl.multiple_of(i * SUBLANE, SUBLANE)
            br, bi = bu[pl.ds(r0, SUBLANE), 0:n], bu[pl.ds(r0, SUBLANE), n:2 * n]
            xr, xi = _cmul(cr, ci, br, bi)
            xr, xi = _s5_block_scan(xr, xi, pw, table, row, kr, ki)
            st[pl.ds(r0, SUBLANE), 0:n] = xr
            st[pl.ds(r0, SUBLANE), n:2 * n] = xi
            sp[pl.ds(r0, SUBLANE), 0:n] = jnp.where(row == 0, kr, pltpu.roll(xr, 1, 0))
            sp[pl.ds(r0, SUBLANE), n:2 * n] = jnp.where(row == 0, ki, pltpu.roll(xi, 1, 0))
            return xr[SUBLANE - 1:SUBLANE, :], xi[SUBLANE - 1:SUBLANE, :]

        lax.fori_loop(0, nblk, fblk, (sb_ref[:, 0:n], sb_ref[:, n:2 * n]))

        gbu[...] = _dot_nt(dy16, c_ref[...])
        _, rtable, _ = _s5_tables(ar, ai, reverse=True)

        def rblk(j, carry):
            kr, ki, a0, a1, a2, a3 = carry
            r0 = pl.multiple_of((nblk - 1 - j) * SUBLANE, SUBLANE)
            xr, xi = gbu[pl.ds(r0, SUBLANE), 0:n], gbu[pl.ds(r0, SUBLANE), n:2 * n]
            xr, xi = _s5_block_scan_rev(xr, xi, pw, rtable, row, kr, ki)
            pr, pi = sp[pl.ds(r0, SUBLANE), 0:n], sp[pl.ds(r0, SUBLANE), n:2 * n]
            br, bi = bu[pl.ds(r0, SUBLANE), 0:n], bu[pl.ds(r0, SUBLANE), n:2 * n]
            a0 = a0 + pr * xr + pi * xi
            a1 = a1 + pr * xi - pi * xr
            a2 = a2 + br * xr + bi * xi
            a3 = a3 + br * xi - bi * xr
            gbu[pl.ds(r0, SUBLANE), 0:n] = cr * xr + ci * xi
            gbu[pl.ds(r0, SUBLANE), n:2 * n] = cr * xi - ci * xr
            return xr[0:1, :], xi[0:1, :], a0, a1, a2, a3

        init = (gcar[:, 0:n], gcar[:, n:2 * n], acc[0], acc[1], acc[2], acc[3])
        kr, ki, a0, a1, a2, a3 = lax.fori_loop(0, nblk, rblk, init)
        gcar[:, 0:n] = kr
        gcar[:, n:2 * n] = ki
        for idx, (a, o) in enumerate(zip((a0, a1, a2, a3), (gar_ref, gai_ref, gcr_ref, gci_ref))):
            acc[idx] = a
            o[...] = jnp.sum(a, axis=0, keepdims=True)

        g16 = gbu[...].astype(BF16)
        gb_ref[...] += _dot_tn(u16, g16)
        gc_ref[...] += _dot_tn(st[...].astype(BF16), dy16)
        gd_ref[...] += jnp.sum(dyv * u, axis=0, keepdims=True)
        du_ref[...] = (_dot_nt(g16, b_ref[...]) + d_ref[...] * dyv).astype(BF16)

    last = n_chunks - 1
    vec = pl.BlockSpec((None, 1, n), lambda s, c: (s, 0, 0))
    vsh = jax.ShapeDtypeStruct((S5_SUPER, 1, n), F32)
    return pl.pallas_call(
        body, name="s5_bwd", grid=(S5_SUPER, n_chunks),
        in_specs=[pl.BlockSpec((tc, LANE), lambda s, c: (last - c, s)),
                  pl.BlockSpec((tc, LANE), lambda s, c: (last - c, s)),
                  pl.BlockSpec((None, None, 1, 2 * n), lambda s, c: (s, last - c, 0, 0)),
                  pl.BlockSpec((None, LANE, 2 * n), lambda s, c: (s, 0, 0)),
                  pl.BlockSpec((None, 2 * n, LANE), lambda s, c: (s, 0, 0)),
                  pl.BlockSpec((None, 1, LANE), lambda s, c: (s, 0, 0)),
                  vec, vec, vec, vec],
        out_specs=[pl.BlockSpec((tc, LANE), lambda s, c: (last - c, s)),
                   pl.BlockSpec((None, LANE, 2 * n), lambda s, c: (s, 0, 0)),
                   pl.BlockSpec((None, 2 * n, LANE), lambda s, c: (s, 0, 0)),
                   pl.BlockSpec((None, 1, LANE), lambda s, c: (s, 0, 0)),
                   vec, vec, vec, vec],
        out_shape=[jax.ShapeDtypeStruct((t, S5_WIDTH), BF16),
                   jax.ShapeDtypeStruct((S5_SUPER, LANE, 2 * n), F32),
                   jax.ShapeDtypeStruct((S5_SUPER, 2 * n, LANE), F32),
                   jax.ShapeDtypeStruct((S5_SUPER, 1, LANE), F32),
                   vsh, vsh, vsh, vsh],
        scratch_shapes=[pltpu.VMEM((tc, 2 * n), F32), pltpu.VMEM((tc, 2 * n), F32), pltpu.VMEM((tc, 2 * n), F32),
                        pltpu.VMEM((tc, 2 * n), F32), pltpu.VMEM((1, 2 * n), F32), pltpu.VMEM((4, SUBLANE, n), F32)],
        compiler_params=_params(("parallel", "arbitrary")),
    )(proj, dy, sb, bsg, ccat, dvec, abar_re, abar_im, coef_re, coef_im)


def _gelu_fwd(y, name="s5_gelu"):
    t, w = y.shape
    tr = _pick(t, 512, SUBLANE)

    def body(y_ref, z_ref):
        z_ref[...] = _gelu_and_grad(y_ref[...])[0].astype(BF16)

    row = pl.BlockSpec((tr, w), lambda i: (i, 0))
    return pl.pallas_call(body, name=name, grid=(t // tr,), in_specs=[row], out_specs=row,
                          out_shape=jax.ShapeDtypeStruct((t, w), BF16), compiler_params=_params(("parallel",)))(y)


def _glu_fwd(y, gl, b, name="s5_glu"):
    t, w = y.shape
    tr = _pick(t, 512, SUBLANE)

    def body(y_ref, gl_ref, b_ref, z2_ref):
        z = _gelu_and_grad(y_ref[...])[0]
        z2_ref[...] = (z * _sigmoid(gl_ref[...] + b_ref[...])).astype(BF16)

    row = pl.BlockSpec((tr, w), lambda i: (i, 0))
    return pl.pallas_call(body, name=name, grid=(t // tr,),
                          in_specs=[row, row, pl.BlockSpec((1, w), lambda i: (0, 0))], out_specs=row,
                          out_shape=jax.ShapeDtypeStruct((t, w), BF16), compiler_params=_params(("parallel",)))(y, gl, b)


def _glu_bwd(y, gl, b, dz2, name="s5_glu_bwd", after=None):
    t, w = y.shape
    tr = _pick(t, 512, SUBLANE)

    def body(y_ref, gl_ref, b_ref, dz2_ref, dgl_ref, dza_ref, db_ref):
        @pl.when(pl.program_id(0) == 0)
        def _():
            db_ref[...] = jnp.zeros_like(db_ref)

        z = _gelu_and_grad(y_ref[...])[0]
        s = _sigmoid(gl_ref[...] + b_ref[...])
        dz2v = dz2_ref[...]
        dgl = dz2v * z * s * (1.0 - s)
        dgl_ref[...] = dgl.astype(BF16)
        dza_ref[...] = dz2v * s
        db_ref[...] += jnp.sum(dgl, axis=0, keepdims=True)

    row = pl.BlockSpec((tr, w), lambda i: (i, 0))
    vec = pl.BlockSpec((1, w), lambda i: (0, 0))
    body, in_specs, args = _ordered(body, [row, row, vec, row], [y, gl, b, dz2], after)
    return pl.pallas_call(body, name=name, grid=(t // tr,), in_specs=in_specs, out_specs=[row, row, vec],
                          out_shape=[jax.ShapeDtypeStruct((t, w), BF16), jax.ShapeDtypeStruct((t, w), F32),
                                     jax.ShapeDtypeStruct((1, w), F32)],
                          compiler_params=_params(("arbitrary",)))(*args)


def _gelu_bwd(y, dza, dzb, name="s5_gelu_bwd", after=None):
    t, w = y.shape
    tr = _pick(t, 512, SUBLANE)

    def body(y_ref, a_ref, b_ref, dy_ref):
        dy_ref[...] = (a_ref[...] + b_ref[...]) * _gelu_and_grad(y_ref[...])[1]

    row = pl.BlockSpec((tr, w), lambda i: (i, 0))
    body, in_specs, args = _ordered(body, [row, row, row], [y, dza, dzb], after)
    return pl.pallas_call(body, name=name, grid=(t // tr,), in_specs=in_specs, out_specs=row,
                          out_shape=jax.ShapeDtypeStruct((t, w), F32), compiler_params=_params(("parallel",)))(*args)


def _tri_dot(tri16, x):
    hi = x.astype(BF16)
    r1 = x - hi.astype(F32)
    mid = r1.astype(BF16)
    lo = (r1 - mid.astype(F32)).astype(BF16)
    return _dot(tri16, hi) + _dot(tri16, mid) + _dot(tri16, lo)


def _hgrn_pre(q_in, z, lg):
    lb = _sigmoid(lg[0:1, :] - lg[1:2, :])
    qs, dqs = _silu_and_grad(q_in)
    sz = _sigmoid(z)
    f = lb + (1.0 - lb) * sz
    k = (1.0 - lb) * (1.0 - sz)
    c = HGRN_CHUNK
    r = lax.broadcasted_iota(jnp.int32, (c, c), 0)
    s = lax.broadcasted_iota(jnp.int32, (c, c), 1)
    causal = r >= s
    b = _tri_dot(jnp.where(causal, 1.0, 0.0).astype(BF16), jnp.log(f))
    b_end = b[c - 1:c, :]
    b_mid = b[c // 2 - 1:c // 2, :]
    e_q, e_k, e_0, e_c = jnp.exp(b - b_mid), jnp.exp(b_mid - b), jnp.exp(b), jnp.exp(b_end - b)
    return dict(lb=lb, qs=qs, dqs=dqs, sz=sz, f=f, k=k, causal=causal, b_end=b_end,
                e_q=e_q, e_k=e_k, e_0=e_0, e_c=e_c,
                qt=qs * e_q, kt=k * e_k, q0=qs * e_0, kc=k * e_c)


def _hgrn_fwd(proj, logits, ng):
    t = proj.shape[0]
    c, dh = HGRN_CHUNK, HGRN_DH
    n_chunks = t // c

    def body(q_ref, z_ref, v_ref, g_ref, lg_ref, ng_ref, o_ref, oh_ref, s0_ref, st):
        @pl.when(pl.program_id(1) == 0)
        def _():
            st[...] = jnp.zeros_like(st)

        s0 = st[...]
        s0_ref[...] = s0
        p = _hgrn_pre(q_ref[...], z_ref[...], lg_ref[...])
        v16 = v_ref[...].astype(BF16)
        a = jnp.where(p["causal"], _dot_nt(p["qt"].astype(BF16), p["kt"].astype(BF16)), 0.0)
        o = _dot_nt(p["q0"].astype(BF16), s0.astype(BF16)) + _dot(a.astype(BF16), v16)
        st[...] = jnp.exp(p["b_end"]) * s0 + _dot_tn(v16, p["kc"].astype(BF16))
        o_ref[...] = o
        rn = lax.rsqrt(jnp.mean(o * o, axis=-1, keepdims=True) + RMS_EPS)
        oh_ref[...] = (o * rn * ng_ref[...] * _silu_and_grad(g_ref[...])[0]).astype(BF16)

    def col(off):
        return pl.BlockSpec((c, dh), lambda h, i: (i, off + h))

    return pl.pallas_call(
        body, name="hgrn_fwd", grid=(HGRN_HEADS, n_chunks),
        in_specs=[col(8), col(16), col(24), col(32),
                  pl.BlockSpec((2, dh), lambda h, i: (0, h)), pl.BlockSpec((1, dh), lambda h, i: (0, h))],
        out_specs=[col(0), col(0), pl.BlockSpec((None, None, dh, dh), lambda h, i: (h, i, 0, 0))],
        out_shape=[jax.ShapeDtypeStruct((t, HGRN_WIDTH), F32), jax.ShapeDtypeStruct((t, HGRN_WIDTH), BF16),
                   jax.ShapeDtypeStruct((HGRN_HEADS, n_chunks, dh, dh), F32)],
        scratch_shapes=[pltpu.VMEM((dh, dh), F32)],
        compiler_params=_params(("parallel", "arbitrary")),
    )(proj, proj, proj, proj, logits, ng)


def _hgrn_bwd(proj, o_raw, s0s, doh, logits, ng):
    t = proj.shape[0]
    c, dh = HGRN_CHUNK, HGRN_DH
    n_chunks = t // c
    last = n_chunks - 1

    def body(q_ref, z_ref, v_ref, g_ref, o_ref, s0_ref, doh_ref, lg_ref, ng_ref,
             dq_ref, dz_ref, dv_ref, dg_ref, dng_ref, dlb_ref, dst):
        @pl.when(pl.program_id(1) == 0)
        def _():
            dst[...] = jnp.zeros_like(dst)
            dng_ref[...] = jnp.zeros_like(dng_ref)
            dlb_ref[...] = jnp.zeros_like(dlb_ref)

        p = _hgrn_pre(q_ref[...], z_ref[...], lg_ref[...])
        v = v_ref[...]
        v16 = v.astype(BF16)
        s0 = s0_ref[...]
        ds_end = dst[...]
        ds16 = ds_end.astype(BF16)
        ngv = ng_ref[...]

        o = o_ref[...]
        dohv = doh_ref[...]
        sg, dsg = _silu_and_grad(g_ref[...])
        rn = lax.rsqrt(jnp.mean(o * o, axis=-1, keepdims=True) + RMS_EPS)
        oh = o * rn
        dg_ref[...] = (dohv * oh * ngv * dsg).astype(BF16)
        don = dohv * sg
        dng_ref[...] += jnp.sum(don * oh, axis=0, keepdims=True)
        doh_n = don * ngv
        do = rn * (doh_n - oh * jnp.mean(doh_n * oh, axis=-1, keepdims=True))
        do16 = do.astype(BF16)

        qt16, kt16, q016, kc16 = (p[n].astype(BF16) for n in ("qt", "kt", "q0", "kc"))
        a = jnp.where(p["causal"], _dot_nt(qt16, kt16), 0.0)
        da = jnp.where(p["causal"], _dot_nt(do16, v16), 0.0)
        da16 = da.astype(BF16)
        dqt = _dot(da16, kt16)
        dq0 = _dot(do16, s0.astype(BF16))
        dkt = _dot_tn(da16, qt16)
        dkc = _dot(v16, ds16)
        dv_ref[...] = (_dot_tn(a.astype(BF16), do16) + _dot_nt(kc16, ds16)).astype(BF16)
        lam_end = jnp.exp(p["b_end"])
        dst[...] = lam_end * ds_end + _dot_tn(do16, q016)

        qt, kt, q0, kc = (a.astype(F32) for a in (qt16, kt16, q016, kc16))
        db = dqt * qt + dq0 * q0 - dkt * kt - dkc * kc
        db_end = (jnp.sum(dkc * kc, axis=0, keepdims=True)
                  + jnp.sum(ds_end * s0, axis=0, keepdims=True) * lam_end)
        rowi = lax.broadcasted_iota(jnp.int32, (c, dh), 0)
        db = db + jnp.where(rowi == c - 1, db_end, 0.0)
        r = lax.broadcasted_iota(jnp.int32, (c, c), 0)
        s = lax.broadcasted_iota(jnp.int32, (c, c), 1)
        dlf = _tri_dot(jnp.where(s >= r, 1.0, 0.0).astype(BF16), db)

        dqs = dqt * p["e_q"] + dq0 * p["e_0"]
        dq_ref[...] = (dqs * p["dqs"]).astype(BF16)
        dk = dkt * p["e_k"] + dkc * p["e_c"]
        sz, lb = p["sz"], p["lb"]
        common = dlf / p["f"] - dk
        dz_ref[...] = ((1.0 - lb) * sz * (1.0 - sz) * common).astype(BF16)
        dlb_ref[...] += jnp.sum((1.0 - sz) * common, axis=0, keepdims=True)

    def col(off):
        return pl.BlockSpec((c, dh), lambda h, i: (last - i, off + h))

    vec = pl.BlockSpec((1, dh), lambda h, i: (0, h))
    act = jax.ShapeDtypeStruct((t, HGRN_WIDTH), BF16)
    vsh = jax.ShapeDtypeStruct((1, HGRN_WIDTH), F32)
    return pl.pallas_call(
        body, name="hgrn_bwd", grid=(HGRN_HEADS, n_chunks),
        in_specs=[col(8), col(16), col(24), col(32), col(0),
                  pl.BlockSpec((None, None, dh, dh), lambda h, i: (h, last - i, 0, 0)),
                  col(0), pl.BlockSpec((2, dh), lambda h, i: (0, h)), vec],
        out_specs=[col(0), col(0), col(0), col(0), vec, vec],
        out_shape=[act, act, act, act, vsh, vsh],
        scratch_shapes=[pltpu.VMEM((dh, dh), F32)],
        compiler_params=_params(("parallel", "arbitrary")),
    )(proj, proj, proj, proj, o_raw, s0s, doh, logits, ng)


def _lb_bwd(logits, dlb):
    def body(lg_ref, d_ref, o_ref):
        lg = lg_ref[...]
        lb = _sigmoid(lg[0:1, :] - lg[1:2, :])
        g = d_ref[...] * lb * (1.0 - lb)
        o_ref[0:1, :] = g
        o_ref[1:2, :] = -g

    return pl.pallas_call(body, name="hgrn_lb_bwd", out_shape=jax.ShapeDtypeStruct(logits.shape, F32),
                          compiler_params=_params())(logits, dlb)


MERGE_TC = 1024
GS_BLOCK = (S5_WIDTH + 4 * HGRN_WIDTH) // MERGE_TC
GH_BLOCK = GS_BLOCK + D_MODEL // MERGE_TC


def _merge_fwd(proj, ys, yh):
    t = proj.shape[0]
    tr = _pick(t, 256, SUBLANE)

    def body(gs_ref, gh_ref, ys_ref, yh_ref, m_ref):
        m_ref[...] = (_sigmoid(gs_ref[...]) * ys_ref[...] + _sigmoid(gh_ref[...]) * yh_ref[...]).astype(BF16)

    blk = pl.BlockSpec((tr, MERGE_TC), lambda i, j: (i, j))
    return pl.pallas_call(
        body, name="merge_fwd", grid=(t // tr, D_MODEL // MERGE_TC),
        in_specs=[pl.BlockSpec((tr, MERGE_TC), lambda i, j: (i, GS_BLOCK + j)),
                  pl.BlockSpec((tr, MERGE_TC), lambda i, j: (i, GH_BLOCK + j)), blk, blk],
        out_specs=blk, out_shape=jax.ShapeDtypeStruct((t, D_MODEL), BF16),
        compiler_params=_params(("parallel", "parallel")),
    )(proj, proj, ys, yh)


def _merge_bwd(proj, ys, yh, dm, after=None):
    t = proj.shape[0]
    tr = _pick(t, 256, SUBLANE)

    def body(gs_ref, gh_ref, ys_ref, yh_ref, dm_ref, dys_ref, dyh_ref, dgs_ref, dgh_ref):
        dmv = dm_ref[...]
        ss, sh = _sigmoid(gs_ref[...]), _sigmoid(gh_ref[...])
        dys_ref[...] = (dmv * ss).astype(BF16)
        dyh_ref[...] = (dmv * sh).astype(BF16)
        dgs_ref[...] = (dmv * ys_ref[...] * ss * (1.0 - ss)).astype(BF16)
        dgh_ref[...] = (dmv * yh_ref[...] * sh * (1.0 - sh)).astype(BF16)

    blk = pl.BlockSpec((tr, MERGE_TC), lambda i, j: (i, j))
    sh16 = jax.ShapeDtypeStruct((t, D_MODEL), BF16)
    in_specs = [pl.BlockSpec((tr, MERGE_TC), lambda i, j: (i, GS_BLOCK + j)),
                pl.BlockSpec((tr, MERGE_TC), lambda i, j: (i, GH_BLOCK + j)), blk, blk, blk]
    body, in_specs, args = _ordered(body, in_specs, [proj, proj, ys, yh, dm], after)
    return pl.pallas_call(
        body, name="merge_bwd", grid=(t // tr, D_MODEL // MERGE_TC),
        in_specs=in_specs,
        out_specs=[blk, blk, blk, blk], out_shape=[sh16, sh16, sh16, sh16],
        compiler_params=_params(("parallel", "parallel")),
    )(*args)


FFN_TC = 128
FFN_ROWS = 512
HALO = SUBLANE


def _rows_with_halo(ref, r0, nrows, t, before, after):
    lo = r0 - before if r0 - before >= 0 else r0
    hi = r0 + nrows + after if r0 + nrows + after <= t else r0 + nrows
    parts = []
    if lo == r0 and before:
        parts.append(jnp.zeros((before, ref.shape[1]), F32))
    parts.append(ref[lo:hi, :])
    if hi == r0 + nrows and after:
        parts.append(jnp.zeros((after, ref.shape[1]), F32))
    return parts[0] if len(parts) == 1 else jnp.concatenate(parts, axis=0)


def _conv3(ext, w, b, nrows, off):
    n = ext.shape[0]
    x0 = ext[off:off + nrows, :]
    x1 = pltpu.roll(ext, 1, 0)[off:off + nrows, :]
    x2 = pltpu.roll(ext, 2, 0)[off:off + nrows, :]
    return b + w[0:1, :] * x2 + w[1:2, :] * x1 + w[2:3, :] * x0, (x0, x1, x2)


def _ffn_act_fwd(up, cw, cb):
    t = up.shape[0]
    rows = _pick(t, FFN_ROWS, SUBLANE)
    nvb = D_FF // FFN_TC

    def body(ug_ref, uv_ref, wg_ref, wv_ref, bg_ref, bv_ref, act_ref):
        wg, wv, bg, bv = wg_ref[...], wv_ref[...], bg_ref[...], bv_ref[...]
        for r0 in range(0, t, rows):
            cg, _ = _conv3(_rows_with_halo(ug_ref, r0, rows, t, HALO, 0), wg, bg, rows, HALO)
            cv, _ = _conv3(_rows_with_halo(uv_ref, r0, rows, t, HALO, 0), wv, bv, rows, HALO)
            act_ref[r0:r0 + rows, :] = (_silu_and_grad(cg)[0] * cv).astype(BF16)

    def colblk(nrow, off):
        return pl.BlockSpec((nrow, FFN_TC), lambda j: (0, off + j))

    return pl.pallas_call(
        body, name="ffn_act_fwd", grid=(nvb,),
        in_specs=[colblk(t, 0), colblk(t, nvb), colblk(3, 0), colblk(3, nvb), colblk(1, 0), colblk(1, nvb)],
        out_specs=colblk(t, 0), out_shape=jax.ShapeDtypeStruct((t, D_FF), BF16),
        compiler_params=_params(("parallel",)),
    )(up, up, cw, cw, cb, cb)


def _ffn_act_bwd(up, dact, cw, cb, after=None):
    t = up.shape[0]
    rows = _pick(t, FFN_ROWS, SUBLANE)
    nvb = D_FF // FFN_TC

    def body(ug_ref, uv_ref, da_ref, wg_ref, wv_ref, bg_ref, bv_ref,
             dug_ref, duv_ref, dwg_ref, dwv_ref, dbg_ref, dbv_ref):
        wg, wv, bg, bv = wg_ref[...], wv_ref[...], bg_ref[...], bv_ref[...]
        ext = rows + HALO
        acc_g = [jnp.zeros((1, FFN_TC), F32) for _ in range(4)]
        acc_v = [jnp.zeros((1, FFN_TC), F32) for _ in range(4)]
        for r0 in range(0, t, rows):
            cg, xg = _conv3(_rows_with_halo(ug_ref, r0, rows, t, HALO, HALO), wg, bg, ext, HALO)
            cv, xv = _conv3(_rows_with_halo(uv_ref, r0, rows, t, HALO, HALO), wv, bv, ext, HALO)
            dav = _rows_with_halo(da_ref, r0, rows, t, 0, HALO)
            sg, dsg = _silu_and_grad(cg)
            for dconv, xs, w, acc, out in ((dav * cv * dsg, xg, wg, acc_g, dug_ref), (dav * sg, xv, wv, acc_v, duv_ref)):
                d0 = dconv[0:rows, :]
                d1 = pltpu.roll(dconv, ext - 1, 0)[0:rows, :]
                d2 = pltpu.roll(dconv, ext - 2, 0)[0:rows, :]
                out[r0:r0 + rows, :] = (w[2:3, :] * d0 + w[1:2, :] * d1 + w[0:1, :] * d2).astype(BF16)
                x0, x1, x2 = xs
                acc[0] = acc[0] + jnp.sum(d0 * x2[0:rows, :], axis=0, keepdims=True)
                acc[1] = acc[1] + jnp.sum(d0 * x1[0:rows, :], axis=0, keepdims=True)
                acc[2] = acc[2] + jnp.sum(d0 * x0[0:rows, :], axis=0, keepdims=True)
                acc[3] = acc[3] + jnp.sum(d0, axis=0, keepdims=True)
        for acc, dw_ref, db_ref in ((acc_g, dwg_ref, dbg_ref), (acc_v, dwv_ref, dbv_ref)):
            dw_ref[0:1, :] = acc[0]
            dw_ref[1:2, :] = acc[1]
            dw_ref[2:3, :] = acc[2]
            db_ref[...] = acc[3]

    def colblk(nrow, off):
        return pl.BlockSpec((nrow, FFN_TC), lambda j: (0, off + j))

    in_specs = [colblk(t, 0), colblk(t, nvb), colblk(t, 0), colblk(3, 0), colblk(3, nvb), colblk(1, 0), colblk(1, nvb)]
    body, in_specs, args = _ordered(body, in_specs, [up, up, dact, cw, cw, cb, cb], after)
    return pl.pallas_call(
        body, name="ffn_act_bwd", grid=(nvb,),
        in_specs=in_specs,
        out_specs=[colblk(t, 0), colblk(t, 0), colblk(3, 0), colblk(3, 0), colblk(1, 0), colblk(1, 0)],
        out_shape=[jax.ShapeDtypeStruct((t, D_FF), BF16), jax.ShapeDtypeStruct((t, D_FF), BF16),
                   jax.ShapeDtypeStruct((3, D_FF), F32), jax.ShapeDtypeStruct((3, D_FF), F32),
                   jax.ShapeDtypeStruct((1, D_FF), F32), jax.ShapeDtypeStruct((1, D_FF), F32)],
        compiler_params=_params(("parallel",)),
    )(*args)


def _all_gather(shards, name):
    nw = len(shards)

    def body(*refs):
        x_refs, out_refs = refs[:nw], refs[nw:2 * nw]
        send_sems, recv_sems, local_sems = refs[2 * nw:]
        x, y, c = lax.axis_index("x"), lax.axis_index("y"), lax.axis_index("c")
        me, sibling = (x, y, c), (x, y, 1 - c)
        chips = [(1 - x, y), (x, 1 - y), (1 - x, 1 - y)]

        def copy(w, k, block, to, src=None):
            slot = out_refs[w].at[4 * block[0] + 2 * block[1] + block[2]]
            return pltpu.make_async_remote_copy(
                src_ref=slot if src is None else src, dst_ref=slot,
                send_sem=send_sems.at[w, k], recv_sem=recv_sems.at[w, k],
                device_id=to, device_id_type=MESH)

        mine, first, passed = [], [], []
        for w in range(nw):
            cp = pltpu.make_async_copy(x_refs[w], out_refs[w].at[4 * x + 2 * y + c], local_sems.at[w])
            cp.start()
            mine.append(cp)
            first.append(copy(w, 0, me, sibling, src=x_refs[w]))
            first += [copy(w, 1 + j, me, (*chip, c), src=x_refs[w]) for j, chip in enumerate(chips)]
        for cp in first:
            cp.start()
        for w in range(nw):
            for j, chip in enumerate(chips):
                copy(w, 1 + j, (*chip, c), me).wait_recv()
                fwd = copy(w, 4 + j, (*chip, c), sibling)
                fwd.start()
                passed.append(fwd)
        for w in range(nw):
            copy(w, 0, sibling, me).wait_recv()
            for j, chip in enumerate(chips):
                copy(w, 4 + j, (*chip, 1 - c), me).wait_recv()
        for cp in first + passed:
            cp.wait_send()
        for cp in mine:
            cp.wait()

    anyspec = pl.BlockSpec(memory_space=pl.ANY)
    return pl.pallas_call(
        body, name=name,
        in_specs=[anyspec] * nw, out_specs=[anyspec] * nw,
        out_shape=[jax.ShapeDtypeStruct((N_DEV,) + s.shape, s.dtype) for s in shards],
        scratch_shapes=[pltpu.SemaphoreType.DMA((nw, 7)), pltpu.SemaphoreType.DMA((nw, 7)),
                        pltpu.SemaphoreType.DMA((nw,))],
    )(*shards)


HBM_SPEC = pl.BlockSpec(memory_space=pltpu.HBM)
SEM_SPEC = pl.BlockSpec(memory_space=pltpu.SEMAPHORE)
ANY_SPEC = pl.BlockSpec(memory_space=pl.ANY)
DATAFLOW = pltpu.SideEffectType.DATAFLOW_SIDE_EFFECTING


def _my_index():
    return 4 * lax.axis_index("x") + 2 * lax.axis_index("y") + lax.axis_index("c")


def _peers():
    x, y, c = lax.axis_index("x"), lax.axis_index("y"), lax.axis_index("c")
    peers = []
    for k in range(1, N_DEV):
        px = 1 - x if k & 4 else x
        py = 1 - y if k & 2 else y
        pc = 1 - c if k & 1 else c
        peers.append((k, (px, py, pc), 4 * px + 2 * py + pc))
    return peers


def _split_copy(src_ref, land_ref, send_sems, recv_sems, w, k, peer, slot, scatter, outgoing):
    return pltpu.make_async_remote_copy(
        src_ref=src_ref.at[slot] if scatter else src_ref,
        dst_ref=land_ref.at[_my_index() if outgoing else slot],
        send_sem=send_sems.at[w * (N_DEV - 1) + k - 1], recv_sem=recv_sems.at[w * (N_DEV - 1) + k - 1],
        device_id=peer, device_id_type=MESH)


def _exchange_start(srcs, scatter, after, name):
    nw = len(srcs)
    me = _my_index()
    lands = []
    for s in srcs:
        own = lax.dynamic_index_in_dim(s, me, 0, keepdims=True) if scatter else s[None]
        shape = s.shape if scatter else (N_DEV,) + s.shape
        lands.append(lax.dynamic_update_slice_in_dim(lax.empty(shape, s.dtype), own, me, 0))

    afters = [] if after is None else [after]

    def body(*refs):
        s_refs, l_refs = refs[:nw], refs[nw:2 * nw]
        send_sems, recv_sems = refs[2 * nw + len(afters)], refs[2 * nw + len(afters) + 1]
        token = refs[-1]
        for w in range(nw):
            for k, peer, slot in _peers():
                _split_copy(s_refs[w], l_refs[w], send_sems, recv_sems, w, k, peer, slot, scatter, True).start()
        token[...] = jnp.zeros_like(token)

    sems = pltpu.SemaphoreType.DMA((nw * (N_DEV - 1),))
    outs = pl.pallas_call(
        body, name=name,
        out_shape=(sems, sems, *[pltpu.HBM(a.shape, a.dtype) for a in (*srcs, *lands)],
                   jax.ShapeDtypeStruct((SUBLANE, LANE), F32)),
        in_specs=[HBM_SPEC] * (2 * nw) + [ANY_SPEC] * len(afters),
        out_specs=(SEM_SPEC, SEM_SPEC, *[HBM_SPEC] * (2 * nw), pl.BlockSpec(memory_space=pltpu.VMEM)),
        input_output_aliases={i: 2 + i for i in range(2 * nw)},
        compiler_params=pltpu.CompilerParams(has_side_effects=DATAFLOW),
    )(*[pltpu.with_memory_space_constraint(a, pltpu.HBM) for a in (*srcs, *lands)], *afters)
    return dict(sems=outs[:2], srcs=outs[2:2 + nw], lands=outs[2 + nw:2 + 2 * nw], token=outs[-1], scatter=scatter)


def _exchange_wait(handle, afters, name):
    srcs, lands, scatter = handle["srcs"], handle["lands"], handle["scatter"]
    nw = len(srcs)

    def body(*refs):
        s_refs, l_refs = refs[:nw], refs[nw:2 * nw]
        send_sems, recv_sems = refs[2 * nw], refs[2 * nw + 1]
        for w in range(nw):
            for k, peer, slot in _peers():
                cp = _split_copy(s_refs[w], l_refs[w], send_sems, recv_sems, w, k, peer, slot, scatter, False)
                cp.wait_send()
                cp.wait_recv()

    outs = pl.pallas_call(
        body, name=name,
        out_shape=tuple(pltpu.HBM(a.shape, a.dtype) for a in (*srcs, *lands)),
        in_specs=[HBM_SPEC] * (2 * nw) + [SEM_SPEC, SEM_SPEC] + [ANY_SPEC] * len(afters),
        out_specs=tuple([HBM_SPEC] * (2 * nw)),
        input_output_aliases={i: i for i in range(2 * nw)},
        compiler_params=pltpu.CompilerParams(has_side_effects=DATAFLOW),
    )(*srcs, *lands, *handle["sems"], *afters)
    return list(outs[nw:])


def _adamw(w, g, m, v):
    m = ADAM_B1 * m + (1.0 - ADAM_B1) * g
    v = ADAM_B2 * v + (1.0 - ADAM_B2) * (g * g)
    m_hat = m / (1.0 - ADAM_B1 ** ADAM_STEP)
    v_hat = v / (1.0 - ADAM_B2 ** ADAM_STEP)
    delta = -ADAM_LR * (m_hat / (jnp.sqrt(v_hat) + ADAM_EPS) + ADAM_WD * w)
    return delta, m, v


def _sum_adam(parts, w, m, v, name):
    _, r, c = parts.shape
    tr = _pick(r, 128, 16)

    def body(p_ref, w_ref, m_ref, v_ref, g_ref, d_ref, mo_ref, vo_ref):
        g = p_ref[0].astype(F32)
        for s in range(1, N_DEV):
            g = g + p_ref[s].astype(F32)
        g_ref[...] = g
        d_ref[...], mo_ref[...], vo_ref[...] = _adamw(w_ref[...], g, m_ref[...], v_ref[...])

    row = pl.BlockSpec((tr, c), lambda i: (i, 0))
    sh = jax.ShapeDtypeStruct((r, c), F32)
    return pl.pallas_call(
        body, name=name, grid=(r // tr,),
        in_specs=[pl.BlockSpec((N_DEV, tr, c), lambda i: (0, i, 0)), row, row, row],
        out_specs=[row, row, row, row], out_shape=[sh, sh, sh, sh],
        compiler_params=_params(("parallel",)),
    )(parts, w, m, v)


def _sum_slots(parts, name):
    _, r, c = parts.shape
    tr = _pick(r, 512, SUBLANE)

    def body(p_ref, o_ref):
        g = p_ref[0]
        for s in range(1, N_DEV):
            g = g + p_ref[s]
        o_ref[...] = g

    return pl.pallas_call(
        body, name=name, grid=(r // tr,),
        in_specs=[pl.BlockSpec((N_DEV, tr, c), lambda i: (0, i, 0))],
        out_specs=pl.BlockSpec((tr, c), lambda i: (i, 0)), out_shape=jax.ShapeDtypeStruct((r, c), F32),
        compiler_params=_params(("parallel",)),
    )(parts)


def _adam_rows(g, w, m, v, name):
    r, c = g.shape
    tr = _pick(r, 512, SUBLANE)

    def body(g_ref, w_ref, m_ref, v_ref, d_ref, mo_ref, vo_ref):
        d_ref[...], mo_ref[...], vo_ref[...] = _adamw(w_ref[...], g_ref[...], m_ref[...], v_ref[...])

    row = pl.BlockSpec((tr, c), lambda i: (i, 0))
    sh = jax.ShapeDtypeStruct((r, c), F32)
    return pl.pallas_call(body, name=name, grid=(r // tr,), in_specs=[row] * 4, out_specs=[row] * 3,
                          out_shape=[sh, sh, sh], compiler_params=_params(("parallel",)))(g, w, m, v)


def _pack(arrays):
    flat = jnp.concatenate([a.reshape(-1).astype(F32) for a in arrays])
    pad = (-flat.shape[0]) % (SUBLANE * LANE)
    return jnp.pad(flat, (0, pad)).reshape(-1, LANE)


def _unpack(packed, shapes):
    flat = packed.reshape(-1)
    out, off = [], 0
    for s in shapes:
        n = math.prod(s)
        out.append(flat[off:off + n].reshape(s))
        off += n
    return out


def _block_diag(t):
    eye = jnp.eye(S5_SUPER, dtype=bool)
    bd = jnp.where(eye[None, :, None, :, None], t[:, :, :, None, :], 0.0)
    return bd.reshape(S5_SUPER, S5_SUPER * t.shape[2], S5_SUPER * t.shape[3])


def _diag_blocks(dense, a, b):
    x = dense.reshape(S5_SUPER, S5_SUPER, a, S5_SUPER, b)
    return jnp.moveaxis(jnp.diagonal(x, axis1=1, axis2=3), -1, 1)


def _s5_layouts(b_re, b_im, c_re, c_im, d):
    g2 = (S5_GROUPS // S5_SUPER, S5_SUPER)
    bt = lambda b: _block_diag(b.reshape(*g2, S5_STATE, S5_GROUP).transpose(0, 1, 3, 2))
    ct = lambda c: _block_diag(c.reshape(*g2, S5_GROUP, S5_STATE).transpose(0, 1, 3, 2))
    bsg = jnp.concatenate([bt(b_re), bt(b_im)], axis=2).astype(BF16)
    ccat = jnp.concatenate([ct(c_re), -ct(c_im)], axis=1).astype(BF16)
    return bsg, ccat, d.reshape(S5_GROUPS // S5_SUPER, 1, LANE)


def _s5_param_grads(gb, gc):
    n = S5_LANES
    gb_re = _diag_blocks(gb[:, :, 0:n], S5_GROUP, S5_STATE).transpose(0, 1, 3, 2).reshape(S5_GROUPS, S5_STATE, S5_GROUP)
    gb_im = _diag_blocks(gb[:, :, n:2 * n], S5_GROUP, S5_STATE).transpose(0, 1, 3, 2).reshape(S5_GROUPS, S5_STATE, S5_GROUP)
    gc_re = _diag_blocks(gc[:, 0:n, :], S5_STATE, S5_GROUP).transpose(0, 1, 3, 2).reshape(S5_GROUPS, S5_GROUP, S5_STATE)
    gc_im = -_diag_blocks(gc[:, n:2 * n, :], S5_STATE, S5_GROUP).transpose(0, 1, 3, 2).reshape(S5_GROUPS, S5_GROUP, S5_STATE)
    return gb_re, gb_im, gc_re, gc_im


def _local_step(x, target, weight, emit, small, after=None):
    sp = small
    a_re, a_im = sp["s5_a_re"], sp["s5_a_im"]
    ldt = sp["s5_log_dt"].reshape(S5_GROUPS, 1)
    lanes = lambda a: a.reshape(S5_GROUPS // S5_SUPER, 1, S5_LANES)

    h1 = _rms_fwd(x, sp["ln_mix_g"], "rms_mix", after=after)
    w_in = weight("w_in", h1)
    proj = _mm_nn(h1, w_in, "mm_in")
    disc = _s5_param_fwd(a_re, a_im, ldt)
    bsg, ccat, dvec = _s5_layouts(sp["s5_b_re"], sp["s5_b_im"], sp["s5_c_re"], sp["s5_c_im"], sp["s5_d"])
    disc_l = [lanes(a) for a in disc]
    y, sb = _s5_fwd(proj, bsg, ccat, dvec, *disc_l)
    z16 = _gelu_fwd(y)
    w_glu = weight("s5_w_glu", z16)
    gl = _mm_nn(z16, w_glu, "mm_glu")
    z2 = _glu_fwd(y, gl, sp["s5_b_glu"])
    w_ps = weight("w_proj_s5", z2)
    ys = _mm_nn(z2, w_ps, "mm_proj_s5")
    o_raw, oh, s0s = _hgrn_fwd(proj, sp["hgrn_lb_logits"], sp["hgrn_norm_g"])
    w_ph = weight("w_proj_hgrn", oh)
    yh = _mm_nn(oh, w_ph, "mm_proj_hgrn")
    merged = _merge_fwd(proj, ys, yh)
    w_out = weight("w_out", merged)
    x1 = _mm_nn(merged, w_out, "mm_out", res=x)
    h2 = _rms_fwd(x1, sp["ln_ffn_g"], "rms_ffn")
    w_up = weight("w_up", h2)
    up = _mm_nn(h2, w_up, "mm_up")
    act = _ffn_act_fwd(up, sp["conv_w"], sp["conv_b"])
    w_down = weight("w_down", act)
    x2 = _mm_nn(act, w_down, "mm_down", res=x1)
    dx2, dx2_16, g_ln_final, loss = _loss_head(x2, sp["ln_final_g"], target)

    dact = _mm_nt(dx2_16, w_down, "mm_down_dx")
    tok = emit("w_down", _mm_tn(act, dx2_16, 1, "mm_down_dw"))
    dup_g, dup_v, dcw_g, dcw_v, dcb_g, dcb_v = _ffn_act_bwd(up, dact, sp["conv_w"], sp["conv_b"], after=tok)
    dup = jnp.concatenate([dup_g, dup_v], axis=1)
    g_conv_w = jnp.concatenate([dcw_g, dcw_v], axis=1)
    g_conv_b = jnp.concatenate([dcb_g, dcb_v], axis=1)
    dh2 = _mm_nt(dup, w_up, "mm_up_dx")
    tok = emit("w_up", _mm_tn(h2, dup, N_DEV, "mm_up_dw"))
    dx1, dx1_16, g_ln_ffn = _rms_bwd(x1, sp["ln_ffn_g"], dh2, dx2, "rms_ffn_bwd", True, after=tok)

    dmerged = _mm_nt(dx1_16, w_out, "mm_out_dx")
    tok = emit("w_out", _mm_tn(merged, dx1_16, 1, "mm_out_dw"))
    dys, dyh, dgs, dgh = _merge_bwd(proj, ys, yh, dmerged, after=tok)
    doh = _mm_nt(dyh, w_ph, "mm_proj_hgrn_dx")
    tok = emit("w_proj_hgrn", _mm_tn(oh, dyh, N_DEV, "mm_proj_hgrn_dw"))
    dz2 = _mm_nt(dys, w_ps, "mm_proj_s5_dx", after=tok)
    tok = emit("w_proj_s5", _mm_tn(z2, dys, N_DEV, "mm_proj_s5_dw"))
    dgl, dza, g_b_glu = _glu_bwd(y, gl, sp["s5_b_glu"], dz2, after=tok)
    dzb = _mm_nt(dgl, w_glu, "mm_glu_dx")
    tok = emit("s5_w_glu", _mm_tn(z16, dgl, 1, "mm_glu_dw"))
    dy = _gelu_bwd(y, dza, dzb, after=tok)
    du, gb, gc, gd, gar, gai, gcr, gci = _s5_bwd(proj, dy, sb, bsg, ccat, dvec, *disc_l)
    flat = lambda a: a.reshape(S5_GROUPS, S5_STATE)
    g_a_re, g_a_im, g_ldt = _s5_param_bwd(a_re, a_im, ldt, [flat(a) for a in (gar, gai, gcr, gci)])
    g_b_re, g_b_im, g_c_re, g_c_im = _s5_param_grads(gb, gc)
    dq, dz, dv, dg, g_norm, dlb = _hgrn_bwd(proj, o_raw, s0s, doh, sp["hgrn_lb_logits"], sp["hgrn_norm_g"])
    g_logits = _lb_bwd(sp["hgrn_lb_logits"], dlb)

    dproj = jnp.concatenate([du, dq, dz, dv, dg, dgs, dgh], axis=1)
    dh1 = _mm_nt(dproj, w_in, "mm_in_dx")
    tok = emit("w_in", _mm_tn(h1, dproj, N_DEV, "mm_in_dw"))
    grad_x, g_ln_mix = _rms_bwd(x, sp["ln_mix_g"], dh1, dx1, "rms_mix_bwd", False, after=tok)

    small_g = dict(ln_mix_g=g_ln_mix, s5_a_re=g_a_re, s5_a_im=g_a_im, s5_log_dt=g_ldt.reshape(1, S5_GROUPS),
                   s5_b_re=g_b_re, s5_b_im=g_b_im, s5_c_re=g_c_re, s5_c_im=g_c_im,
                   s5_d=gd.reshape(S5_GROUPS, S5_GROUP), s5_b_glu=g_b_glu, hgrn_lb_logits=g_logits,
                   hgrn_norm_g=g_norm, ln_ffn_g=g_ln_ffn, conv_w=g_conv_w, conv_b=g_conv_b, ln_final_g=g_ln_final)
    return loss, grad_x, small_g


BIG = ("w_in", "s5_w_glu", "w_proj_s5", "w_proj_hgrn", "w_out", "w_up", "w_down")
COL_SHARDED = ("w_in", "w_proj_s5", "w_proj_hgrn", "w_up")
SMALL = ("ln_mix_g", "s5_a_re", "s5_a_im", "s5_log_dt", "s5_b_re", "s5_b_im", "s5_c_re", "s5_c_im", "s5_d",
         "s5_b_glu", "hgrn_lb_logits", "hgrn_norm_g", "ln_ffn_g", "conv_b", "ln_final_g")
WEIGHTS = ("ln_mix_g", "w_in", "s5_a_re", "s5_a_im", "s5_log_dt", "s5_b_re", "s5_b_im", "s5_c_re", "s5_c_im", "s5_d",
           "s5_w_glu", "s5_b_glu", "w_proj_s5", "hgrn_lb_logits", "hgrn_norm_g", "w_proj_hgrn", "w_out", "ln_ffn_g",
           "w_up", "conv_w", "conv_b", "w_down", "ln_final_g")


def kernel(x, ln_mix_g, w_in, s5_a_re, s5_a_im, s5_log_dt, s5_b_re, s5_b_im, s5_c_re, s5_c_im, s5_d, s5_w_glu, s5_b_glu, w_proj_s5, hgrn_lb_logits, hgrn_norm_g, w_proj_hgrn, w_out, ln_ffn_g, w_up, conv_w, conv_b, w_down, ln_final_g, loss_target, m_ln_mix_g, m_w_in, m_s5_a_re, m_s5_a_im, m_s5_log_dt, m_s5_b_re, m_s5_b_im, m_s5_c_re, m_s5_c_im, m_s5_d, m_s5_w_glu, m_s5_b_glu, m_w_proj_s5, m_hgrn_lb_logits, m_hgrn_norm_g, m_w_proj_hgrn, m_w_out, m_ln_ffn_g, m_w_up, m_conv_w, m_conv_b, m_w_down, m_ln_final_g, v_ln_mix_g, v_w_in, v_s5_a_re, v_s5_a_im, v_s5_log_dt, v_s5_b_re, v_s5_b_im, v_s5_c_re, v_s5_c_im, v_s5_d, v_s5_w_glu, v_s5_b_glu, v_w_proj_s5, v_hgrn_lb_logits, v_hgrn_norm_g, v_w_proj_hgrn, v_w_out, v_ln_ffn_g, v_w_up, v_conv_w, v_conv_b, v_w_down, v_ln_final_g):
    given = dict(locals())
    w = {n: given[n] for n in WEIGHTS}
    mom = {n: given["m_" + n] for n in WEIGHTS}
    var = {n: given["v_" + n] for n in WEIGHTS}

    shard16 = {n: w[n][0].astype(BF16) for n in BIG}
    w_in_all, conv_w_all = _all_gather([shard16["w_in"], conv_w[0]], "gather_first")
    gather_groups = (("s5_w_glu", "w_proj_s5", "w_proj_hgrn", "w_out"), ("w_up",), ("w_down",))
    pending, token = {}, w_in_all
    for i, group in enumerate(gather_groups):
        handle = _exchange_start([shard16[n] for n in group], False, token, f"gather_start_{i}")
        token = handle["token"]
        for n in group:
            pending[n] = (group, handle, f"gather_wait_{i}")
    ready = {"w_in": w_in_all}

    def weight(name, after):
        if name not in ready:
            group, handle, wait_name = pending[name]
            for n, g in zip(group, _exchange_wait(handle, [after], wait_name)):
                ready[n] = g
        g = ready[name]
        return g if name in COL_SHARDED else g.reshape(1, N_DEV * g.shape[1], g.shape[2])

    scatter_groups = (("w_down",), ("w_up",), ("w_out", "w_proj_hgrn", "w_proj_s5", "s5_w_glu"), ("w_in",))
    emitted, scatters = {}, []

    def emit(name, grad):
        emitted[name] = grad if name in COL_SHARDED else grad.reshape(N_DEV, -1, grad.shape[2])
        group = scatter_groups[len(scatters)]
        if not all(n in emitted for n in group):
            return None
        handle = _exchange_start([emitted[n] for n in group], True, None, f"scatter_start_{len(scatters)}")
        scatters.append((group, handle))
        return handle["token"]

    small = dict(ln_mix_g=ln_mix_g, s5_a_re=s5_a_re[0], s5_a_im=s5_a_im[0], s5_log_dt=s5_log_dt,
                 s5_b_re=s5_b_re[0], s5_b_im=s5_b_im[0], s5_c_re=s5_c_re[0], s5_c_im=s5_c_im[0], s5_d=s5_d[0],
                 s5_b_glu=s5_b_glu, hgrn_lb_logits=hgrn_lb_logits, hgrn_norm_g=hgrn_norm_g, ln_ffn_g=ln_ffn_g,
                 conv_w=conv_w_all.transpose(1, 0, 2).reshape(3, 2 * D_FF), conv_b=conv_b,
                 ln_final_g=ln_final_g.reshape(1, D_MODEL))
    loss, grad_x, small_g = _local_step(x[0], loss_target[0], weight, emit, small, after=token)

    names = SMALL + ("conv_w",)
    shapes = [small_g[n].shape for n in names] + [(1,)]
    packed = _pack([small_g[n] for n in names] + [loss[0, 0:1]])
    total = _sum_slots(_all_gather([packed], "gather_small")[0], "sum_small")
    summed = dict(zip(names + ("loss",), _unpack(total, shapes)))

    grads, delta, new_m, new_v = {}, {}, {}, {}
    afters = [grad_x, total]
    for i, (group, handle) in enumerate(scatters):
        for n, r in zip(group, _exchange_wait(handle, afters, f"scatter_wait_{i}")):
            g, d, m2, v2 = _sum_adam(r, w[n][0], mom[n][0], var[n][0], "adam_" + n)
            grads[n], delta[n], new_m[n], new_v[n] = g[None], d[None], m2[None], v2[None]
        if i == len(scatters) - 2:
            afters = [delta[n] for g2, _ in scatters[:-1] for n in g2]

    pw = _pack([w[n] for n in SMALL])
    d_s, m_s, v_s = _adam_rows(_pack([summed[n] for n in SMALL]), pw, _pack([mom[n] for n in SMALL]),
                               _pack([var[n] for n in SMALL]), "adam_small")
    wshapes = [w[n].shape for n in SMALL]
    for n, g, d, m2, v2 in zip(SMALL, [summed[n] for n in SMALL], _unpack(d_s, wshapes), _unpack(m_s, wshapes),
                               _unpack(v_s, wshapes)):
        grads[n], delta[n], new_m[n], new_v[n] = g.reshape(w[n].shape), d, m2, v2
    me = 4 * lax.axis_index("x") + 2 * lax.axis_index("y") + lax.axis_index("c")
    ncol = conv_w.shape[2]
    g_cw = lax.dynamic_slice_in_dim(summed["conv_w"], me * ncol, ncol, axis=1)
    d_cw, m_cw, v_cw = _adam_rows(g_cw, conv_w[0], m_conv_w[0], v_conv_w[0], "adam_conv_w")
    grads["conv_w"], delta["conv_w"], new_m["conv_w"], new_v["conv_w"] = g_cw[None], d_cw[None], m_cw[None], v_cw[None]

    return (summed["loss"].reshape(()), grad_x[None], *[grads[n] for n in WEIGHTS], *[delta[n] for n in WEIGHTS],
            *[new_m[n] for n in WEIGHTS], *[new_v[n] for n in WEIGHTS])
```

```python
import math

import jax
import jax.numpy as jnp
from jax import lax
from jax.experimental import pallas as pl
from jax.experimental.pallas import tpu as pltpu

F32 = jnp.float32
BF16 = jnp.bfloat16

N_DEV = 8
D_MODEL = 2048
S5_WIDTH = 1024
S5_GROUP = 16
S5_GROUPS = 64
S5_STATE = 64
S5_MAX_RE = -1e-4
S5_SUPER = 8
S5_LANES = S5_SUPER * S5_STATE
HGRN_WIDTH = 1024
HGRN_HEADS = 8
HGRN_DH = 128
HGRN_CHUNK = 64
D_FF = 5632
RMS_EPS = 1e-6
ADAM_LR = 0.001
ADAM_B1 = 0.9
ADAM_B2 = 0.999
ADAM_EPS = 1e-08
ADAM_WD = 0.01
ADAM_STEP = 10

LANE = 128
SUBLANE = 8
VMEM_LIMIT = 48 * 1024 * 1024
MESH = pl.DeviceIdType.MESH
GELU_C = math.sqrt(2.0 / math.pi)
GELU_A = 0.044715


def _params(sem=None):
    return pltpu.CompilerParams(dimension_semantics=sem, vmem_limit_bytes=VMEM_LIMIT)


def _pick(n, cap, unit=LANE):
    best = None
    for t in range(unit, min(n, cap) + 1, unit):
        if n % t == 0:
            best = t
    return best if best is not None else n


def _ordered(body, in_specs, args, after):
    if after is None:
        return body, list(in_specs), list(args)
    n_in = len(args)

    def ordered_body(*refs):
        return body(*refs[:n_in], *refs[n_in + 1:])

    return ordered_body, [*in_specs, pl.BlockSpec(memory_space=pl.ANY)], [*args, after]


def _sigmoid(x):
    return 1.0 / (1.0 + jnp.exp(-x))


def _silu_and_grad(x):
    s = _sigmoid(x)
    return x * s, s * (1.0 + x * (1.0 - s))


def _gelu_and_grad(y):
    inner = GELU_C * (y + GELU_A * y * y * y)
    th = jnp.tanh(inner)
    val = 0.5 * y * (1.0 + th)
    grad = 0.5 * (1.0 + th) + 0.5 * y * (1.0 - th * th) * GELU_C * (1.0 + 3.0 * GELU_A * y * y)
    return val, grad


def _dot(a, b):
    return jnp.dot(a, b, preferred_element_type=F32)


def _dot_nt(a, b):
    return lax.dot_general(a, b, (((1,), (1,)), ((), ())), preferred_element_type=F32)


def _dot_tn(a, b):
    return lax.dot_general(a, b, (((0,), (0,)), ((), ())), preferred_element_type=F32)


def _blocks_per_step(nb, ns, tn, cap=2048):
    if tn != ns:
        return 1
    best = 1
    for b in range(1, nb + 1):
        if nb % b == 0 and b * ns <= cap:
            best = b
    return best


def _mm_nn(a, w, name, res=None, out_dtype=F32):
    m, kdim = a.shape
    nb, _, ns = w.shape
    tm, tk, tn = _pick(m, 512), _pick(kdim, 2048), _pick(ns, 1536)
    npb, nk = ns // tn, kdim // tk
    bps = _blocks_per_step(nb, ns, tn)
    assert bps == 1 or nk == 1

    def body(*refs):
        a_ref, w_ref = refs[0], refs[1]
        r_ref = refs[2] if res is not None else None
        o_ref = refs[3] if res is not None else refs[2]

        def finish(r, cols):
            if res is not None:
                r = r + r_ref[:, cols]
            o_ref[:, cols] = r.astype(out_dtype)

        if nk == 1:
            for b in range(bps):
                finish(_dot(a_ref[...], w_ref[b]), slice(b * tn, (b + 1) * tn))
            return
        acc = refs[-1]
        k = pl.program_id(2)

        @pl.when(k == 0)
        def _():
            acc[...] = jnp.zeros_like(acc)

        acc[...] += _dot(a_ref[...], w_ref[0])

        @pl.when(k == nk - 1)
        def _():
            finish(acc[...], slice(0, tn))

    in_specs = [pl.BlockSpec((tm, tk), lambda j, i, k: (i, k)),
                pl.BlockSpec((bps, tk, tn), lambda j, i, k: (j // npb, k, j % npb))]
    args = [a, w]
    if res is not None:
        in_specs.append(pl.BlockSpec((tm, bps * tn), lambda j, i, k: (i, j)))
        args.append(res)
    return pl.pallas_call(
        body, name=name, grid=(nb * npb // bps, m // tm, nk),
        in_specs=in_specs, out_specs=pl.BlockSpec((tm, bps * tn), lambda j, i, k: (i, j)),
        out_shape=jax.ShapeDtypeStruct((m, nb * ns), out_dtype),
        scratch_shapes=[pltpu.VMEM((tm, tn), F32)] if nk > 1 else [],
        compiler_params=_params(("parallel", "parallel", "arbitrary")),
    )(*args)


def _mm_nt(a, w, name, out_dtype=F32, after=None):
    m, _ = a.shape
    nb, kdim, ns = w.shape
    tm, tko, tn = _pick(m, 512), _pick(kdim, 1024), _pick(ns, 2048)
    npb = ns // tn
    bps = _blocks_per_step(nb, ns, tn)
    nred = nb * npb // bps

    def body(a_ref, w_ref, o_ref, *scratch):
        total = _dot_nt(a_ref[:, 0:tn], w_ref[0])
        for b in range(1, bps):
            total = total + _dot_nt(a_ref[:, b * tn:(b + 1) * tn], w_ref[b])
        if nred == 1:
            o_ref[...] = total.astype(out_dtype)
            return
        acc = scratch[0]
        n = pl.program_id(2)

        @pl.when(n == 0)
        def _():
            acc[...] = jnp.zeros_like(acc)

        acc[...] += total

        @pl.when(n == nred - 1)
        def _():
            o_ref[...] = acc[...].astype(out_dtype)

    in_specs = [pl.BlockSpec((tm, bps * tn), lambda i, j, n: (i, n)),
                pl.BlockSpec((bps, tko, tn), lambda i, j, n: (n // npb, j, n % npb))]
    body, in_specs, args = _ordered(body, in_specs, [a, w], after)
    return pl.pallas_call(
        body, name=name, grid=(m // tm, kdim // tko, nred),
        in_specs=in_specs,
        out_specs=pl.BlockSpec((tm, tko), lambda i, j, n: (i, j)),
        out_shape=jax.ShapeDtypeStruct((m, kdim), out_dtype),
        scratch_shapes=[pltpu.VMEM((tm, tko), F32)] if nred > 1 else [],
        compiler_params=_params(("parallel", "parallel", "arbitrary")),
    )(*args)


def _mm_tn(a, d, nb, name, out_dtype=BF16):
    m, kdim = a.shape
    ns = d.shape[1] // nb
    tm, tko, tn = _pick(m, 4096), _pick(kdim, 512), _pick(ns, 1536)
    npb, nm = ns // tn, m // tm

    def body(a_ref, d_ref, o_ref, *scratch):
        if nm == 1:
            o_ref[...] = _dot_tn(a_ref[...], d_ref[...]).astype(out_dtype)
            return
        acc = scratch[0]
        r = pl.program_id(2)

        @pl.when(r == 0)
        def _():
            acc[...] = jnp.zeros_like(acc)

        acc[...] += _dot_tn(a_ref[...], d_ref[...])

        @pl.when(r == nm - 1)
        def _():
            o_ref[...] = acc[...].astype(out_dtype)

    return pl.pallas_call(
        body, name=name, grid=(nb * npb, kdim // tko, nm),
        in_specs=[pl.BlockSpec((tm, tko), lambda j, i, r: (r, i)),
                  pl.BlockSpec((tm, tn), lambda j, i, r: (r, j))],
        out_specs=pl.BlockSpec((None, tko, tn), lambda j, i, r: (j // npb, i, j % npb)),
        out_shape=jax.ShapeDtypeStruct((nb, kdim, ns), out_dtype),
        scratch_shapes=[pltpu.VMEM((tko, tn), F32)] if nm > 1 else [],
        compiler_params=_params(("parallel", "parallel", "arbitrary")),
    )(a, d)


def _rms_fwd(x, g, name, after=None):
    t, d = x.shape
    tr = _pick(t, 256, SUBLANE)

    def body(x_ref, g_ref, h_ref):
        xv = x_ref[...]
        r = lax.rsqrt(jnp.mean(xv * xv, axis=-1, keepdims=True) + RMS_EPS)
        h_ref[...] = (xv * r * g_ref[...]).astype(BF16)

    in_specs = [pl.BlockSpec((tr, d), lambda i: (i, 0)), pl.BlockSpec((1, d), lambda i: (0, 0))]
    body, in_specs, args = _ordered(body, in_specs, [x, g], after)
    return pl.pallas_call(
        body, name=name, grid=(t // tr,),
        in_specs=in_specs,
        out_specs=pl.BlockSpec((tr, d), lambda i: (i, 0)),
        out_shape=jax.ShapeDtypeStruct((t, d), BF16),
        compiler_params=_params(("parallel",)),
    )(*args)


def _rms_bwd(x, g, dh, add, name, want_bf16, after=None):
    t, d = x.shape
    tr = _pick(t, 256, SUBLANE)

    def body(x_ref, g_ref, dh_ref, add_ref, *outs):
        if want_bf16:
            dx_ref, dxb_ref, dg_ref = outs
        else:
            dx_ref, dg_ref = outs
        i = pl.program_id(0)

        @pl.when(i == 0)
        def _():
            dg_ref[...] = jnp.zeros_like(dg_ref)

        xv, dhv = x_ref[...], dh_ref[...]
        r = lax.rsqrt(jnp.mean(xv * xv, axis=-1, keepdims=True) + RMS_EPS)
        xh = xv * r
        dg_ref[...] += jnp.sum(dhv * xh, axis=0, keepdims=True)
        dxh = dhv * g_ref[...]
        dx = add_ref[...] + r * (dxh - xh * jnp.mean(dxh * xh, axis=-1, keepdims=True))
        dx_ref[...] = dx
        if want_bf16:
            dxb_ref[...] = dx.astype(BF16)

    row = pl.BlockSpec((tr, d), lambda i: (i, 0))
    vec = pl.BlockSpec((1, d), lambda i: (0, 0))
    out_specs = [row] + ([row] if want_bf16 else []) + [vec]
    out_shape = ([jax.ShapeDtypeStruct((t, d), F32)] + ([jax.ShapeDtypeStruct((t, d), BF16)] if want_bf16 else [])
                 + [jax.ShapeDtypeStruct((1, d), F32)])
    body, in_specs, args = _ordered(body, [row, vec, row, row], [x, g, dh, add], after)
    return pl.pallas_call(
        body, name=name, grid=(t // tr,),
        in_specs=in_specs, out_specs=out_specs, out_shape=out_shape,
        compiler_params=_params(("arbitrary",)),
    )(*args)


def _loss_head(x2, g, target, name="loss_head"):
    t, d = x2.shape
    tr = _pick(t, 256, SUBLANE)

    def body(x_ref, g_ref, t_ref, dx_ref, dxb_ref, dg_ref, loss_ref):
        i = pl.program_id(0)

        @pl.when(i == 0)
        def _():
            dg_ref[...] = jnp.zeros_like(dg_ref)
            loss_ref[...] = jnp.zeros_like(loss_ref)

        xv = x_ref[...]
        gv = g_ref[...]
        r = lax.rsqrt(jnp.mean(xv * xv, axis=-1, keepdims=True) + RMS_EPS)
        xh = xv * r
        err = xh * gv - t_ref[...]
        part = 0.5 * jnp.sum(jnp.mean(err * err, axis=-1, keepdims=True), axis=0, keepdims=True)
        loss_ref[...] += jnp.broadcast_to(part, loss_ref.shape)
        dy = err * (1.0 / d)
        dg_ref[...] += jnp.sum(dy * xh, axis=0, keepdims=True)
        dxh = dy * gv
        dx = r * (dxh - xh * jnp.mean(dxh * xh, axis=-1, keepdims=True))
        dx_ref[...] = dx
        dxb_ref[...] = dx.astype(BF16)

    row = pl.BlockSpec((tr, d), lambda i: (i, 0))
    vec = pl.BlockSpec((1, d), lambda i: (0, 0))
    return pl.pallas_call(
        body, name=name, grid=(t // tr,),
        in_specs=[row, vec, row],
        out_specs=[row, row, vec, pl.BlockSpec((1, LANE), lambda i: (0, 0))],
        out_shape=[jax.ShapeDtypeStruct((t, d), F32), jax.ShapeDtypeStruct((t, d), BF16),
                   jax.ShapeDtypeStruct((1, d), F32), jax.ShapeDtypeStruct((1, LANE), F32)],
        compiler_params=_params(("arbitrary",)),
    )(x2, g, target)


def _s5_discretize(a_re, a_im, ldt):
    lam_re = jnp.minimum(a_re, S5_MAX_RE)
    lam_im = a_im
    dt = jnp.exp(ldt)
    mag = jnp.exp(lam_re * dt)
    abar_re = mag * jnp.cos(lam_im * dt)
    abar_im = mag * jnp.sin(lam_im * dt)
    den = lam_re * lam_re + lam_im * lam_im
    nr = abar_re - 1.0
    ni = abar_im
    coef_re = (nr * lam_re + ni * lam_im) / den
    coef_im = (ni * lam_re - nr * lam_im) / den
    return abar_re, abar_im, coef_re, coef_im


def _s5_param_fwd(a_re, a_im, ldt):
    def body(ar_ref, ai_ref, l_ref, o0, o1, o2, o3):
        outs = _s5_discretize(ar_ref[...], ai_ref[...], l_ref[...])
        for o, v in zip((o0, o1, o2, o3), outs):
            o[...] = v

    sh = jax.ShapeDtypeStruct(a_re.shape, F32)
    return pl.pallas_call(body, name="s5_param_fwd", out_shape=[sh, sh, sh, sh], compiler_params=_params())(a_re, a_im, ldt)


def _s5_param_bwd(a_re, a_im, ldt, cts):
    def body(ar_ref, ai_ref, l_ref, c0, c1, c2, c3, g0, g1, g2):
        _, vjp = jax.vjp(_s5_discretize, ar_ref[...], ai_ref[...], l_ref[...])
        ga, gb, gl = vjp((c0[...], c1[...], c2[...], c3[...]))
        g0[...] = ga
        g1[...] = gb
        g2[...] = gl

    sh = jax.ShapeDtypeStruct(a_re.shape, F32)
    return pl.pallas_call(body, name="s5_param_bwd", out_shape=[sh, sh, jax.ShapeDtypeStruct(ldt.shape, F32)],
                          compiler_params=_params())(a_re, a_im, ldt, *cts)


def _cmul(ar, ai, br, bi):
    return ar * br - ai * bi, ar * bi + ai * br


def _s5_tables(ar, ai, reverse):
    a1 = (ar, ai)
    a2 = _cmul(*a1, *a1)
    a4 = _cmul(*a2, *a2)
    pows = [a1]
    for _ in range(SUBLANE - 1):
        pows.append(_cmul(*pows[-1], *a1))
    row = lax.broadcasted_iota(jnp.int32, (SUBLANE, ar.shape[1]), 0)
    tr = jnp.zeros((SUBLANE, ar.shape[1]), F32)
    ti = jnp.zeros((SUBLANE, ar.shape[1]), F32)
    for r in range(SUBLANE):
        p = pows[SUBLANE - 1 - r] if reverse else pows[r]
        tr = jnp.where(row == r, p[0], tr)
        ti = jnp.where(row == r, p[1], ti)
    return (a1, a2, a4), (tr, ti), row


def _s5_block_scan(xr, xi, pw, table, row, kr, ki):
    for k, (pr, pi) in zip((1, 2, 4), pw):
        sr = jnp.where(row >= k, pltpu.roll(xr, k, 0), 0.0)
        si = jnp.where(row >= k, pltpu.roll(xi, k, 0), 0.0)
        xr, xi = xr + pr * sr - pi * si, xi + pr * si + pi * sr
    tr, ti = table
    return xr + tr * kr - ti * ki, xi + tr * ki + ti * kr


def _s5_block_scan_rev(xr, xi, pw, table, row, kr, ki):
    for k, (pr, pi) in zip((1, 2, 4), pw):
        sr = jnp.where(row < SUBLANE - k, pltpu.roll(xr, SUBLANE - k, 0), 0.0)
        si = jnp.where(row < SUBLANE - k, pltpu.roll(xi, SUBLANE - k, 0), 0.0)
        xr, xi = xr + pr * sr + pi * si, xi + pr * si - pi * sr
    tr, ti = table
    return xr + tr * kr + ti * ki, xi + tr * ki - ti * kr


def _s5_fwd(proj, bsg, ccat, dvec, abar_re, abar_im, coef_re, coef_im):
    t = proj.shape[0]
    tc = _pick(t, 512, SUBLANE)
    n_chunks, nblk, n = t // tc, tc // SUBLANE, S5_LANES

    def body(u_ref, b_ref, c_ref, d_ref, ar_ref, ai_ref, cr_ref, ci_ref, y_ref, sb_ref, bu, st, car):
        @pl.when(pl.program_id(1) == 0)
        def _():
            car[...] = jnp.zeros_like(car)

        sb_ref[...] = car[...]
        u = u_ref[...]
        bu[...] = _dot(u.astype(BF16), b_ref[...])
        cr, ci = cr_ref[...], ci_ref[...]
        pw, table, row = _s5_tables(ar_ref[...], ai_ref[...], reverse=False)

        def blk(i, carry):
            r0 = pl.multiple_of(i * SUBLANE, SUBLANE)
            br, bi = bu[pl.ds(r0, SUBLANE), 0:n], bu[pl.ds(r0, SUBLANE), n:2 * n]
            xr, xi = _cmul(cr, ci, br, bi)
            xr, xi = _s5_block_scan(xr, xi, pw, table, row, *carry)
            st[pl.ds(r0, SUBLANE), 0:n] = xr
            st[pl.ds(r0, SUBLANE), n:2 * n] = xi
            return xr[SUBLANE - 1:SUBLANE, :], xi[SUBLANE - 1:SUBLANE, :]

        kr, ki = lax.fori_loop(0, nblk, blk, (car[:, 0:n], car[:, n:2 * n]))
        car[:, 0:n] = kr
        car[:, n:2 * n] = ki
        y_ref[...] = _dot(st[...].astype(BF16), c_ref[...]) + d_ref[...] * u

    vec = pl.BlockSpec((None, 1, n), lambda s, c: (s, 0, 0))
    return pl.pallas_call(
        body, name="s5_fwd", grid=(S5_SUPER, n_chunks),
        in_specs=[pl.BlockSpec((tc, LANE), lambda s, c: (c, s)),
                  pl.BlockSpec((None, LANE, 2 * n), lambda s, c: (s, 0, 0)),
                  pl.BlockSpec((None, 2 * n, LANE), lambda s, c: (s, 0, 0)),
                  pl.BlockSpec((None, 1, LANE), lambda s, c: (s, 0, 0)),
                  vec, vec, vec, vec],
        out_specs=[pl.BlockSpec((tc, LANE), lambda s, c: (c, s)),
                   pl.BlockSpec((None, None, 1, 2 * n), lambda s, c: (s, c, 0, 0))],
        out_shape=[jax.ShapeDtypeStruct((t, S5_WIDTH), F32),
                   jax.ShapeDtypeStruct((S5_SUPER, n_chunks, 1, 2 * n), F32)],
        scratch_shapes=[pltpu.VMEM((tc, 2 * n), F32), pltpu.VMEM((tc, 2 * n), F32), pltpu.VMEM((1, 2 * n), F32)],
        compiler_params=_params(("parallel", "arbitrary")),
    )(proj, bsg, ccat, dvec, abar_re, abar_im, coef_re, coef_im)


def _s5_bwd(proj, dy, sb, bsg, ccat, dvec, abar_re, abar_im, coef_re, coef_im):
    t = proj.shape[0]
    tc = _pick(t, 512, SUBLANE)
    n_chunks, nblk, n = t // tc, tc // SUBLANE, S5_LANES

    def body(u_ref, dy_ref, sb_ref, b_ref, c_ref, d_ref, ar_ref, ai_ref, cr_ref, ci_ref,
             du_ref, gb_ref, gc_ref, gd_ref, gar_ref, gai_ref, gcr_ref, gci_ref,
             bu, st, sp, gbu, gcar, acc):
        @pl.when(pl.program_id(1) == 0)
        def _():
            gcar[...] = jnp.zeros_like(gcar)
            acc[...] = jnp.zeros_like(acc)
            gb_ref[...] = jnp.zeros_like(gb_ref)
            gc_ref[...] = jnp.zeros_like(gc_ref)
            gd_ref[...] = jnp.zeros_like(gd_ref)

        u = u_ref[...]
        dyv = dy_ref[...]
        u16, dy16 = u.astype(BF16), dyv.astype(BF16)
        bu[...] = _dot(u16, b_ref[...])
        ar, ai = ar_ref[...], ai_ref[...]
        cr, ci = cr_ref[...], ci_ref[...]
        pw, table, row = _s5_tables(ar, ai, reverse=False)

        def fblk(i, carry):
            kr, ki = carry
            r0 = pl.multiple_of(i * SUBLANE, SUBLANE)
            br, bi = bu[pl.ds(r0, SUBLANE), 0:n], bu[pl.ds(r0, SUBLANE), n:2 * n]
            xr, xi = _cmul(cr, ci, br, bi)
            xr, xi = _s5_block_scan(xr, xi, pw, table, row, kr, ki)
            st[pl.ds(r0, SUBLANE), 0:n] = xr
            st[pl.ds(r0, SUBLANE), n:2 * n] = xi
            sp[pl.ds(r0, SUBLANE), 0:n] = jnp.where(row == 0, kr, pltpu.roll(xr, 1, 0))
            sp[pl.ds(r0, SUBLANE), n:2 * n] = jnp.where(row == 0, ki, pltpu.roll(xi, 1, 0))
            return xr[SUBLANE - 1:SUBLANE, :], xi[SUBLANE - 1:SUBLANE, :]

        lax.fori_loop(0, nblk, fblk, (sb_ref[:, 0:n], sb_ref[:, n:2 * n]))

        gbu[...] = _dot_nt(dy16, c_ref[...])
        _, rtable, _ = _s5_tables(ar, ai, reverse=True)

        def rblk(j, carry):
            kr, ki, a0, a1, a2, a3 = carry
            r0 = pl.multiple_of((nblk - 1 - j) * SUBLANE, SUBLANE)
            xr, xi = gbu[pl.ds(r0, SUBLANE), 0:n], gbu[pl.ds(r0, SUBLANE), n:2 * n]
            xr, xi = _s5_block_scan_rev(xr, xi, pw, rtable, row, kr, ki)
            pr, pi = sp[pl.ds(r0, SUBLANE), 0:n], sp[pl.ds(r0, SUBLANE), n:2 * n]
            br, bi = bu[pl.ds(r0, SUBLANE), 0:n], bu[pl.ds(r0, SUBLANE), n:2 * n]
            a0 = a0 + pr * xr + pi * xi
            a1 = a1 + pr * xi - pi * xr
            a2 = a2 + br * xr + bi * xi
            a3 = a3 + br * xi - bi * xr
            gbu[pl.ds(r0, SUBLANE), 0:n] = cr * xr + ci * xi
            gbu[pl.ds(r0, SUBLANE), n:2 * n] = cr * xi - ci * xr
            return xr[0:1, :], xi[0:1, :], a0, a1, a2, a3

        init = (gcar[:, 0:n], gcar[:, n:2 * n], acc[0], acc[1], acc[2], acc[3])
        kr, ki, a0, a1, a2, a3 = lax.fori_loop(0, nblk, rblk, init)
        gcar[:, 0:n] = kr
        gcar[:, n:2 * n] = ki
        for idx, (a, o) in enumerate(zip((a0, a1, a2, a3), (gar_ref, gai_ref, gcr_ref, gci_ref))):
            acc[idx] = a
            o[...] = jnp.sum(a, axis=0, keepdims=True)

        g16 = gbu[...].astype(BF16)
        gb_ref[...] += _dot_tn(u16, g16)
        gc_ref[...] += _dot_tn(st[...].astype(BF16), dy16)
        gd_ref[...] += jnp.sum(dyv * u, axis=0, keepdims=True)
        du_ref[...] = (_dot_nt(g16, b_ref[...]) + d_ref[...] * dyv).astype(BF16)

    last = n_chunks - 1
    vec = pl.BlockSpec((None, 1, n), lambda s, c: (s, 0, 0))
    vsh = jax.ShapeDtypeStruct((S5_SUPER, 1, n), F32)
    return pl.pallas_call(
        body, name="s5_bwd", grid=(S5_SUPER, n_chunks),
        in_specs=[pl.BlockSpec((tc, LANE), lambda s, c: (last - c, s)),
                  pl.BlockSpec((tc, LANE), lambda s, c: (last - c, s)),
                  pl.BlockSpec((None, None, 1, 2 * n), lambda s, c: (s, last - c, 0, 0)),
                  pl.BlockSpec((None, LANE, 2 * n), lambda s, c: (s, 0, 0)),
                  pl.BlockSpec((None, 2 * n, LANE), lambda s, c: (s, 0, 0)),
                  pl.BlockSpec((None, 1, LANE), lambda s, c: (s, 0, 0)),
                  vec, vec, vec, vec],
        out_specs=[pl.BlockSpec((tc, LANE), lambda s, c: (last - c, s)),
                   pl.BlockSpec((None, LANE, 2 * n), lambda s, c: (s, 0, 0)),
                   pl.BlockSpec((None, 2 * n, LANE), lambda s, c: (s, 0, 0)),
                   pl.BlockSpec((None, 1, LANE), lambda s, c: (s, 0, 0)),
                   vec, vec, vec, vec],
        out_shape=[jax.ShapeDtypeStruct((t, S5_WIDTH), BF16),
                   jax.ShapeDtypeStruct((S5_SUPER, LANE, 2 * n), F32),
                   jax.ShapeDtypeStruct((S5_SUPER, 2 * n, LANE), F32),
                   jax.ShapeDtypeStruct((S5_SUPER, 1, LANE), F32),
                   vsh, vsh, vsh, vsh],
        scratch_shapes=[pltpu.VMEM((tc, 2 * n), F32), pltpu.VMEM((tc, 2 * n), F32), pltpu.VMEM((tc, 2 * n), F32),
                        pltpu.VMEM((tc, 2 * n), F32), pltpu.VMEM((1, 2 * n), F32), pltpu.VMEM((4, SUBLANE, n), F32)],
        compiler_params=_params(("parallel", "arbitrary")),
    )(proj, dy, sb, bsg, ccat, dvec, abar_re, abar_im, coef_re, coef_im)


def _gelu_fwd(y, name="s5_gelu"):
    t, w = y.shape
    tr = _pick(t, 512, SUBLANE)

    def body(y_ref, z_ref):
        z_ref[...] = _gelu_and_grad(y_ref[...])[0].astype(BF16)

    row = pl.BlockSpec((tr, w), lambda i: (i, 0))
    return pl.pallas_call(body, name=name, grid=(t // tr,), in_specs=[row], out_specs=row,
                          out_shape=jax.ShapeDtypeStruct((t, w), BF16), compiler_params=_params(("parallel",)))(y)


def _glu_fwd(y, gl, b, name="s5_glu"):
    t, w = y.shape
    tr = _pick(t, 512, SUBLANE)

    def body(y_ref, gl_ref, b_ref, z2_ref):
        z = _gelu_and_grad(y_ref[...])[0]
        z2_ref[...] = (z * _sigmoid(gl_ref[...] + b_ref[...])).astype(BF16)

    row = pl.BlockSpec((tr, w), lambda i: (i, 0))
    return pl.pallas_call(body, name=name, grid=(t // tr,),
                          in_specs=[row, row, pl.BlockSpec((1, w), lambda i: (0, 0))], out_specs=row,
                          out_shape=jax.ShapeDtypeStruct((t, w), BF16), compiler_params=_params(("parallel",)))(y, gl, b)


def _glu_bwd(y, gl, b, dz2, name="s5_glu_bwd", after=None):
    t, w = y.shape
    tr = _pick(t, 512, SUBLANE)

    def body(y_ref, gl_ref, b_ref, dz2_ref, dgl_ref, dza_ref, db_ref):
        @pl.when(pl.program_id(0) == 0)
        def _():
            db_ref[...] = jnp.zeros_like(db_ref)

        z = _gelu_and_grad(y_ref[...])[0]
        s = _sigmoid(gl_ref[...] + b_ref[...])
        dz2v = dz2_ref[...]
        dgl = dz2v * z * s * (1.0 - s)
        dgl_ref[...] = dgl.astype(BF16)
        dza_ref[...] = dz2v * s
        db_ref[...] += jnp.sum(dgl, axis=0, keepdims=True)

    row = pl.BlockSpec((tr, w), lambda i: (i, 0))
    vec = pl.BlockSpec((1, w), lambda i: (0, 0))
    body, in_specs, args = _ordered(body, [row, row, vec, row], [y, gl, b, dz2], after)
    return pl.pallas_call(body, name=name, grid=(t // tr,), in_specs=in_specs, out_specs=[row, row, vec],
                          out_shape=[jax.ShapeDtypeStruct((t, w), BF16), jax.ShapeDtypeStruct((t, w), F32),
                                     jax.ShapeDtypeStruct((1, w), F32)],
                          compiler_params=_params(("arbitrary",)))(*args)


def _gelu_bwd(y, dza, dzb, name="s5_gelu_bwd", after=None):
    t, w = y.shape
    tr = _pick(t, 512, SUBLANE)

    def body(y_ref, a_ref, b_ref, dy_ref):
        dy_ref[...] = (a_ref[...] + b_ref[...]) * _gelu_and_grad(y_ref[...])[1]

    row = pl.BlockSpec((tr, w), lambda i: (i, 0))
    body, in_specs, args = _ordered(body, [row, row, row], [y, dza, dzb], after)
    return pl.pallas_call(body, name=name, grid=(t // tr,), in_specs=in_specs, out_specs=row,
                          out_shape=jax.ShapeDtypeStruct((t, w), F32), compiler_params=_params(("parallel",)))(*args)


def _tri_dot(tri16, x):
    hi = x.astype(BF16)
    r1 = x - hi.astype(F32)
    mid = r1.astype(BF16)
    lo = (r1 - mid.astype(F32)).astype(BF16)
    return _dot(tri16, hi) + _dot(tri16, mid) + _dot(tri16, lo)


def _hgrn_pre(q_in, z, lg):
    lb = _sigmoid(lg[0:1, :] - lg[1:2, :])
    qs, dqs = _silu_and_grad(q_in)
    sz = _sigmoid(z)
    f = lb + (1.0 - lb) * sz
    k = (1.0 - lb) * (1.0 - sz)
    c = HGRN_CHUNK
    r = lax.broadcasted_iota(jnp.int32, (c, c), 0)
    s = lax.broadcasted_iota(jnp.int32, (c, c), 1)
    causal = r >= s
    b = _tri_dot(jnp.where(causal, 1.0, 0.0).astype(BF16), jnp.log(f))
    b_end = b[c - 1:c, :]
    b_mid = b[c // 2 - 1:c // 2, :]
    e_q, e_k, e_0, e_c = jnp.exp(b - b_mid), jnp.exp(b_mid - b), jnp.exp(b), jnp.exp(b_end - b)
    return dict(lb=lb, qs=qs, dqs=dqs, sz=sz, f=f, k=k, causal=causal, b_end=b_end,
                e_q=e_q, e_k=e_k, e_0=e_0, e_c=e_c,
                qt=qs * e_q, kt=k * e_k, q0=qs * e_0, kc=k * e_c)


def _hgrn_fwd(proj, logits, ng):
    t = proj.shape[0]
    c, dh = HGRN_CHUNK, HGRN_DH
    n_chunks = t // c

    def head(h, q_ref, z_ref, v_ref, g_ref, lg_ref, ng_ref, o_ref, oh_ref, s0_ref, st):
        sl = slice(h * dh, (h + 1) * dh)
        s0 = st[h]
        s0_ref[h] = s0
        p = _hgrn_pre(q_ref[:, sl], z_ref[:, sl], lg_ref[:, sl])
        v16 = v_ref[:, sl].astype(BF16)
        a = jnp.where(p["causal"], _dot_nt(p["qt"].astype(BF16), p["kt"].astype(BF16)), 0.0)
        o = _dot_nt(p["q0"].astype(BF16), s0.astype(BF16)) + _dot(a.astype(BF16), v16)
        st[h] = jnp.exp(p["b_end"]) * s0 + _dot_tn(v16, p["kc"].astype(BF16))
        o_ref[:, sl] = o
        rn = lax.rsqrt(jnp.mean(o * o, axis=-1, keepdims=True) + RMS_EPS)
        oh_ref[:, sl] = (o * rn * ng_ref[:, sl] * _silu_and_grad(g_ref[:, sl])[0]).astype(BF16)

    def body(*refs):
        st = refs[-1]

        @pl.when(pl.program_id(0) == 0)
        def _():
            st[...] = jnp.zeros_like(st)

        for h in range(HGRN_HEADS):
            head(h, *refs)

    def wide(off):
        return pl.BlockSpec((c, HGRN_WIDTH), lambda i: (i, off))

    return pl.pallas_call(
        body, name="hgrn_fwd", grid=(n_chunks,),
        in_specs=[wide(1), wide(2), wide(3), wide(4),
                  pl.BlockSpec((2, HGRN_WIDTH), lambda i: (0, 0)), pl.BlockSpec((1, HGRN_WIDTH), lambda i: (0, 0))],
        out_specs=[wide(0), wide(0), pl.BlockSpec((HGRN_HEADS, None, dh, dh), lambda i: (0, i, 0, 0))],
        out_shape=[jax.ShapeDtypeStruct((t, HGRN_WIDTH), F32), jax.ShapeDtypeStruct((t, HGRN_WIDTH), BF16),
                   jax.ShapeDtypeStruct((HGRN_HEADS, n_chunks, dh, dh), F32)],
        scratch_shapes=[pltpu.VMEM((HGRN_HEADS, dh, dh), F32)],
        compiler_params=_params(("arbitrary",)),
    )(proj, proj, proj, proj, logits, ng)


def _hgrn_bwd(proj, o_raw, s0s, doh, logits, ng):
    t = proj.shape[0]
    c, dh = HGRN_CHUNK, HGRN_DH
    n_chunks = t // c
    last = n_chunks - 1

    def head(h, q_ref, z_ref, v_ref, g_ref, o_ref, s0_ref, doh_ref, lg_ref, ng_ref,
             dq_ref, dz_ref, dv_ref, dg_ref, dng_ref, dlb_ref, dst):
        sl = slice(h * dh, (h + 1) * dh)
        p = _hgrn_pre(q_ref[:, sl], z_ref[:, sl], lg_ref[:, sl])
        v = v_ref[:, sl]
        v16 = v.astype(BF16)
        s0 = s0_ref[h]
        ds_end = dst[h]
        ds16 = ds_end.astype(BF16)
        ngv = ng_ref[:, sl]

        o = o_ref[:, sl]
        dohv = doh_ref[:, sl]
        sg, dsg = _silu_and_grad(g_ref[:, sl])
        rn = lax.rsqrt(jnp.mean(o * o, axis=-1, keepdims=True) + RMS_EPS)
        oh = o * rn
        dg_ref[:, sl] = (dohv * oh * ngv * dsg).astype(BF16)
        don = dohv * sg
        dng_ref[:, sl] += jnp.sum(don * oh, axis=0, keepdims=True)
        doh_n = don * ngv
        do = rn * (doh_n - oh * jnp.mean(doh_n * oh, axis=-1, keepdims=True))
        do16 = do.astype(BF16)

        qt16, kt16, q016, kc16 = (p[n].astype(BF16) for n in ("qt", "kt", "q0", "kc"))
        a = jnp.where(p["causal"], _dot_nt(qt16, kt16), 0.0)
        da = jnp.where(p["causal"], _dot_nt(do16, v16), 0.0)
        da16 = da.astype(BF16)
        dqt = _dot(da16, kt16)
        dq0 = _dot(do16, s0.astype(BF16))
        dkt = _dot_tn(da16, qt16)
        dkc = _dot(v16, ds16)
        dv_ref[:, sl] = (_dot_tn(a.astype(BF16), do16) + _dot_nt(kc16, ds16)).astype(BF16)
        lam_end = jnp.exp(p["b_end"])
        dst[h] = lam_end * ds_end + _dot_tn(do16, q016)

        qt, kt, q0, kc = (a.astype(F32) for a in (qt16, kt16, q016, kc16))
        db = dqt * qt + dq0 * q0 - dkt * kt - dkc * kc
        db_end = (jnp.sum(dkc * kc, axis=0, keepdims=True)
                  + jnp.sum(ds_end * s0, axis=0, keepdims=True) * lam_end)
        rowi = lax.broadcasted_iota(jnp.int32, (c, dh), 0)
        db = db + jnp.where(rowi == c - 1, db_end, 0.0)
        r = lax.broadcasted_iota(jnp.int32, (c, c), 0)
        s = lax.broadcasted_iota(jnp.int32, (c, c), 1)
        dlf = _tri_dot(jnp.where(s >= r, 1.0, 0.0).astype(BF16), db)

        dqs = dqt * p["e_q"] + dq0 * p["e_0"]
        dq_ref[:, sl] = (dqs * p["dqs"]).astype(BF16)
        dk = dkt * p["e_k"] + dkc * p["e_c"]
        sz, lb = p["sz"], p["lb"]
        common = dlf / p["f"] - dk
        dz_ref[:, sl] = ((1.0 - lb) * sz * (1.0 - sz) * common).astype(BF16)
        dlb_ref[:, sl] += jnp.sum((1.0 - sz) * common, axis=0, keepdims=True)

    def body(*refs):
        dng_ref, dlb_ref, dst = refs[-3:]

        @pl.when(pl.program_id(0) == 0)
        def _():
            dst[...] = jnp.zeros_like(dst)
            dng_ref[...] = jnp.zeros_like(dng_ref)
            dlb_ref[...] = jnp.zeros_like(dlb_ref)

        for h in range(HGRN_HEADS):
            head(h, *refs)

    def wide(off):
        return pl.BlockSpec((c, HGRN_WIDTH), lambda i: (last - i, off))

    vec = pl.BlockSpec((1, HGRN_WIDTH), lambda i: (0, 0))
    act = jax.ShapeDtypeStruct((t, HGRN_WIDTH), BF16)
    vsh = jax.ShapeDtypeStruct((1, HGRN_WIDTH), F32)
    return pl.pallas_call(
        body, name="hgrn_bwd", grid=(n_chunks,),
        in_specs=[wide(1), wide(2), wide(3), wide(4), wide(0),
                  pl.BlockSpec((HGRN_HEADS, None, dh, dh), lambda i: (0, last - i, 0, 0)),
                  wide(0), pl.BlockSpec((2, HGRN_WIDTH), lambda i: (0, 0)), vec],
        out_specs=[wide(0), wide(0), wide(0), wide(0), vec, vec],
        out_shape=[act, act, act, act, vsh, vsh],
        scratch_shapes=[pltpu.VMEM((HGRN_HEADS, dh, dh), F32)],
        compiler_params=_params(("arbitrary",)),
    )(proj, proj, proj, proj, o_raw, s0s, doh, logits, ng)


def _lb_bwd(logits, dlb):
    def body(lg_ref, d_ref, o_ref):
        lg = lg_ref[...]
        lb = _sigmoid(lg[0:1, :] - lg[1:2, :])
        g = d_ref[...] * lb * (1.0 - lb)
        o_ref[0:1, :] = g
        o_ref[1:2, :] = -g

    return pl.pallas_call(body, name="hgrn_lb_bwd", out_shape=jax.ShapeDtypeStruct(logits.shape, F32),
                          compiler_params=_params())(logits, dlb)


MERGE_TC = 1024
GS_BLOCK = (S5_WIDTH + 4 * HGRN_WIDTH) // MERGE_TC
GH_BLOCK = GS_BLOCK + D_MODEL // MERGE_TC


def _merge_fwd(proj, ys, yh):
    t = proj.shape[0]
    tr = _pick(t, 256, SUBLANE)

    def body(gs_ref, gh_ref, ys_ref, yh_ref, m_ref):
        m_ref[...] = (_sigmoid(gs_ref[...]) * ys_ref[...] + _sigmoid(gh_ref[...]) * yh_ref[...]).astype(BF16)

    blk = pl.BlockSpec((tr, MERGE_TC), lambda i, j: (i, j))
    return pl.pallas_call(
        body, name="merge_fwd", grid=(t // tr, D_MODEL // MERGE_TC),
        in_specs=[pl.BlockSpec((tr, MERGE_TC), lambda i, j: (i, GS_BLOCK + j)),
                  pl.BlockSpec((tr, MERGE_TC), lambda i, j: (i, GH_BLOCK + j)), blk, blk],
        out_specs=blk, out_shape=jax.ShapeDtypeStruct((t, D_MODEL), BF16),
        compiler_params=_params(("parallel", "parallel")),
    )(proj, proj, ys, yh)


def _merge_bwd(proj, ys, yh, dm, after=None):
    t = proj.shape[0]
    tr = _pick(t, 256, SUBLANE)

    def body(gs_ref, gh_ref, ys_ref, yh_ref, dm_ref, dys_ref, dyh_ref, dgs_ref, dgh_ref):
        dmv = dm_ref[...]
        ss, sh = _sigmoid(gs_ref[...]), _sigmoid(gh_ref[...])
        dys_ref[...] = (dmv * ss).astype(BF16)
        dyh_ref[...] = (dmv * sh).astype(BF16)
        dgs_ref[...] = (dmv * ys_ref[...] * ss * (1.0 - ss)).astype(BF16)
        dgh_ref[...] = (dmv * yh_ref[...] * sh * (1.0 - sh)).astype(BF16)

    blk = pl.BlockSpec((tr, MERGE_TC), lambda i, j: (i, j))
    sh16 = jax.ShapeDtypeStruct((t, D_MODEL), BF16)
    in_specs = [pl.BlockSpec((tr, MERGE_TC), lambda i, j: (i, GS_BLOCK + j)),
                pl.BlockSpec((tr, MERGE_TC), lambda i, j: (i, GH_BLOCK + j)), blk, blk, blk]
    body, in_specs, args = _ordered(body, in_specs, [proj, proj, ys, yh, dm], after)
    return pl.pallas_call(
        body, name="merge_bwd", grid=(t // tr, D_MODEL // MERGE_TC),
        in_specs=in_specs,
        out_specs=[blk, blk, blk, blk], out_shape=[sh16, sh16, sh16, sh16],
        compiler_params=_params(("parallel", "parallel")),
    )(*args)


FFN_TC = 128
FFN_ROWS = 512
HALO = SUBLANE


def _rows_with_halo(ref, r0, nrows, t, before, after):
    lo = r0 - before if r0 - before >= 0 else r0
    hi = r0 + nrows + after if r0 + nrows + after <= t else r0 + nrows
    parts = []
    if lo == r0 and before:
        parts.append(jnp.zeros((before, ref.shape[1]), F32))
    parts.append(ref[lo:hi, :])
    if hi == r0 + nrows and after:
        parts.append(jnp.zeros((after, ref.shape[1]), F32))
    return parts[0] if len(parts) == 1 else jnp.concatenate(parts, axis=0)


def _conv3(ext, w, b, nrows, off):
    n = ext.shape[0]
    x0 = ext[off:off + nrows, :]
    x1 = pltpu.roll(ext, 1, 0)[off:off + nrows, :]
    x2 = pltpu.roll(ext, 2, 0)[off:off + nrows, :]
    return b + w[0:1, :] * x2 + w[1:2, :] * x1 + w[2:3, :] * x0, (x0, x1, x2)


def _ffn_act_fwd(up, cw, cb):
    t = up.shape[0]
    rows = _pick(t, FFN_ROWS, SUBLANE)
    nvb = D_FF // FFN_TC

    def body(ug_ref, uv_ref, wg_ref, wv_ref, bg_ref, bv_ref, act_ref):
        wg, wv, bg, bv = wg_ref[...], wv_ref[...], bg_ref[...], bv_ref[...]
        for r0 in range(0, t, rows):
            cg, _ = _conv3(_rows_with_halo(ug_ref, r0, rows, t, HALO, 0), wg, bg, rows, HALO)
            cv, _ = _conv3(_rows_with_halo(uv_ref, r0, rows, t, HALO, 0), wv, bv, rows, HALO)
            act_ref[r0:r0 + rows, :] = (_silu_and_grad(cg)[0] * cv).astype(BF16)

    def colblk(nrow, off):
        return pl.BlockSpec((nrow, FFN_TC), lambda j: (0, off + j))

    return pl.pallas_call(
        body, name="ffn_act_fwd", grid=(nvb,),
        in_specs=[colblk(t, 0), colblk(t, nvb), colblk(3, 0), colblk(3, nvb), colblk(1, 0), colblk(1, nvb)],
        out_specs=colblk(t, 0), out_shape=jax.ShapeDtypeStruct((t, D_FF), BF16),
        compiler_params=_params(("parallel",)),
    )(up, up, cw, cw, cb, cb)


def _ffn_act_bwd(up, dact, cw, cb, after=None):
    t = up.shape[0]
    rows = _pick(t, FFN_ROWS, SUBLANE)
    nvb = D_FF // FFN_TC

    def body(ug_ref, uv_ref, da_ref, wg_ref, wv_ref, bg_ref, bv_ref,
             dug_ref, duv_ref, dwg_ref, dwv_ref, dbg_ref, dbv_ref):
        wg, wv, bg, bv = wg_ref[...], wv_ref[...], bg_ref[...], bv_ref[...]
        ext = rows + HALO
        acc_g = [jnp.zeros((1, FFN_TC), F32) for _ in range(4)]
        acc_v = [jnp.zeros((1, FFN_TC), F32) for _ in range(4)]
        for r0 in range(0, t, rows):
            cg, xg = _conv3(_rows_with_halo(ug_ref, r0, rows, t, HALO, HALO), wg, bg, ext, HALO)
            cv, xv = _conv3(_rows_with_halo(uv_ref, r0, rows, t, HALO, HALO), wv, bv, ext, HALO)
            dav = _rows_with_halo(da_ref, r0, rows, t, 0, HALO)
            sg, dsg = _silu_and_grad(cg)
            for dconv, xs, w, acc, out in ((dav * cv * dsg, xg, wg, acc_g, dug_ref), (dav * sg, xv, wv, acc_v, duv_ref)):
                d0 = dconv[0:rows, :]
                d1 = pltpu.roll(dconv, ext - 1, 0)[0:rows, :]
                d2 = pltpu.roll(dconv, ext - 2, 0)[0:rows, :]
                out[r0:r0 + rows, :] = (w[2:3, :] * d0 + w[1:2, :] * d1 + w[0:1, :] * d2).astype(BF16)
                x0, x1, x2 = xs
                acc[0] = acc[0] + jnp.sum(d0 * x2[0:rows, :], axis=0, keepdims=True)
                acc[1] = acc[1] + jnp.sum(d0 * x1[0:rows, :], axis=0, keepdims=True)
                acc[2] = acc[2] + jnp.sum(d0 * x0[0:rows, :], axis=0, keepdims=True)
                acc[3] = acc[3] + jnp.sum(d0, axis=0, keepdims=True)
        for acc, dw_ref, db_ref in ((acc_g, dwg_ref, dbg_ref), (acc_v, dwv_ref, dbv_ref)):
            dw_ref[0:1, :] = acc[0]
            dw_ref[1:2, :] = acc[1]
            dw_ref[2:3, :] = acc[2]
            db_ref[...] = acc[3]

    def colblk(nrow, off):
        return pl.BlockSpec((nrow, FFN_TC), lambda j: (0, off + j))

    in_specs = [colblk(t, 0), colblk(t, nvb), colblk(t, 0), colblk(3, 0), colblk(3, nvb), colblk(1, 0), colblk(1, nvb)]
    body, in_specs, args = _ordered(body, in_specs, [up, up, dact, cw, cw, cb, cb], after)
    return pl.pallas_call(
        body, name="ffn_act_bwd", grid=(nvb,),
        in_specs=in_specs,
        out_specs=[colblk(t, 0), colblk(t, 0), colblk(3, 0), colblk(3, 0), colblk(1, 0), colblk(1, 0)],
        out_shape=[jax.ShapeDtypeStruct((t, D_FF), BF16), jax.ShapeDtypeStruct((t, D_FF), BF16),
                   jax.ShapeDtypeStruct((3, D_FF), F32), jax.ShapeDtypeStruct((3, D_FF), F32),
                   jax.ShapeDtypeStruct((1, D_FF), F32), jax.ShapeDtypeStruct((1, D_FF), F32)],
        compiler_params=_params(("parallel",)),
    )(*args)


def _all_gather(shards, name):
    nw = len(shards)

    def body(*refs):
        x_refs, out_refs = refs[:nw], refs[nw:2 * nw]
        send_sems, recv_sems, local_sems = refs[2 * nw:]
        x, y, c = lax.axis_index("x"), lax.axis_index("y"), lax.axis_index("c")
        me, sibling = (x, y, c), (x, y, 1 - c)
        chips = [(1 - x, y), (x, 1 - y), (1 - x, 1 - y)]

        def copy(w, k, block, to, src=None):
            slot = out_refs[w].at[4 * block[0] + 2 * block[1] + block[2]]
            return pltpu.make_async_remote_copy(
                src_ref=slot if src is None else src, dst_ref=slot,
                send_sem=send_sems.at[w, k], recv_sem=recv_sems.at[w, k],
                device_id=to, device_id_type=MESH)

        mine, first, passed = [], [], []
        for w in range(nw):
            cp = pltpu.make_async_copy(x_refs[w], out_refs[w].at[4 * x + 2 * y + c], local_sems.at[w])
            cp.start()
            mine.append(cp)
            first.append(copy(w, 0, me, sibling, src=x_refs[w]))
            first += [copy(w, 1 + j, me, (*chip, c), src=x_refs[w]) for j, chip in enumerate(chips)]
        for cp in first:
            cp.start()
        for w in range(nw):
            for j, chip in enumerate(chips):
                copy(w, 1 + j, (*chip, c), me).wait_recv()
                fwd = copy(w, 4 + j, (*chip, c), sibling)
                fwd.start()
                passed.append(fwd)
        for w in range(nw):
            copy(w, 0, sibling, me).wait_recv()
            for j, chip in enumerate(chips):
                copy(w, 4 + j, (*chip, 1 - c), me).wait_recv()
        for cp in first + passed:
            cp.wait_send()
        for cp in mine:
            cp.wait()

    anyspec = pl.BlockSpec(memory_space=pl.ANY)
    return pl.pallas_call(
        body, name=name,
        in_specs=[anyspec] * nw, out_specs=[anyspec] * nw,
        out_shape=[jax.ShapeDtypeStruct((N_DEV,) + s.shape, s.dtype) for s in shards],
        scratch_shapes=[pltpu.SemaphoreType.DMA((nw, 7)), pltpu.SemaphoreType.DMA((nw, 7)),
                        pltpu.SemaphoreType.DMA((nw,))],
    )(*shards)


HBM_SPEC = pl.BlockSpec(memory_space=pltpu.HBM)
SEM_SPEC = pl.BlockSpec(memory_space=pltpu.SEMAPHORE)
ANY_SPEC = pl.BlockSpec(memory_space=pl.ANY)
DATAFLOW = pltpu.SideEffectType.DATAFLOW_SIDE_EFFECTING


def _my_index():
    return 4 * lax.axis_index("x") + 2 * lax.axis_index("y") + lax.axis_index("c")


def _peers():
    x, y, c = lax.axis_index("x"), lax.axis_index("y"), lax.axis_index("c")
    peers = []
    for k in range(1, N_DEV):
        px = 1 - x if k & 4 else x
        py = 1 - y if k & 2 else y
        pc = 1 - c if k & 1 else c
        peers.append((k, (px, py, pc), 4 * px + 2 * py + pc))
    return peers


def _split_copy(src_ref, land_ref, send_sems, recv_sems, w, k, peer, slot, scatter, outgoing):
    return pltpu.make_async_remote_copy(
        src_ref=src_ref.at[slot] if scatter else src_ref,
        dst_ref=land_ref.at[_my_index() if outgoing else slot],
        send_sem=send_sems.at[w * (N_DEV - 1) + k - 1], recv_sem=recv_sems.at[w * (N_DEV - 1) + k - 1],
        device_id=peer, device_id_type=MESH)


def _exchange_start(srcs, scatter, after, name):
    nw = len(srcs)
    me = _my_index()
    lands = []
    for s in srcs:
        own = lax.dynamic_index_in_dim(s, me, 0, keepdims=True) if scatter else s[None]
        shape = s.shape if scatter else (N_DEV,) + s.shape
        lands.append(lax.dynamic_update_slice_in_dim(lax.empty(shape, s.dtype), own, me, 0))

    afters = [] if after is None else [after]

    def body(*refs):
        s_refs, l_refs = refs[:nw], refs[nw:2 * nw]
        send_sems, recv_sems = refs[2 * nw + len(afters)], refs[2 * nw + len(afters) + 1]
        token = refs[-1]
        for w in range(nw):
            for k, peer, slot in _peers():
                _split_copy(s_refs[w], l_refs[w], send_sems, recv_sems, w, k, peer, slot, scatter, True).start()
        token[...] = jnp.zeros_like(token)

    sems = pltpu.SemaphoreType.DMA((nw * (N_DEV - 1),))
    outs = pl.pallas_call(
        body, name=name,
        out_shape=(sems, sems, *[pltpu.HBM(a.shape, a.dtype) for a in (*srcs, *lands)],
                   jax.ShapeDtypeStruct((SUBLANE, LANE), F32)),
        in_specs=[HBM_SPEC] * (2 * nw) + [ANY_SPEC] * len(afters),
        out_specs=(SEM_SPEC, SEM_SPEC, *[HBM_SPEC] * (2 * nw), pl.BlockSpec(memory_space=pltpu.VMEM)),
        input_output_aliases={i: 2 + i for i in range(2 * nw)},
        compiler_params=pltpu.CompilerParams(has_side_effects=DATAFLOW),
    )(*[pltpu.with_memory_space_constraint(a, pltpu.HBM) for a in (*srcs, *lands)], *afters)
    return dict(sems=outs[:2], srcs=outs[2:2 + nw], lands=outs[2 + nw:2 + 2 * nw], token=outs[-1], scatter=scatter)


def _exchange_wait(handle, afters, name):
    srcs, lands, scatter = handle["srcs"], handle["lands"], handle["scatter"]
    nw = len(srcs)

    def body(*refs):
        s_refs, l_refs = refs[:nw], refs[nw:2 * nw]
        send_sems, recv_sems = refs[2 * nw], refs[2 * nw + 1]
        for w in range(nw):
            for k, peer, slot in _peers():
                cp = _split_copy(s_refs[w], l_refs[w], send_sems, recv_sems, w, k, peer, slot, scatter, False)
                cp.wait_send()
                cp.wait_recv()

    outs = pl.pallas_call(
        body, name=name,
        out_shape=tuple(pltpu.HBM(a.shape, a.dtype) for a in (*srcs, *lands)),
        in_specs=[HBM_SPEC] * (2 * nw) + [SEM_SPEC, SEM_SPEC] + [ANY_SPEC] * len(afters),
        out_specs=tuple([HBM_SPEC] * (2 * nw)),
        input_output_aliases={i: i for i in range(2 * nw)},
        compiler_params=pltpu.CompilerParams(has_side_effects=DATAFLOW),
    )(*srcs, *lands, *handle["sems"], *afters)
    return list(outs[nw:])


def _adamw(w, g, m, v):
    m = ADAM_B1 * m + (1.0 - ADAM_B1) * g
    v = ADAM_B2 * v + (1.0 - ADAM_B2) * (g * g)
    m_hat = m / (1.0 - ADAM_B1 ** ADAM_STEP)
    v_hat = v / (1.0 - ADAM_B2 ** ADAM_STEP)
    delta = -ADAM_LR * (m_hat / (jnp.sqrt(v_hat) + ADAM_EPS) + ADAM_WD * w)
    return delta, m, v


def _sum_adam(parts, w, m, v, name):
    _, r, c = parts.shape
    tr = _pick(r, 128, 16)

    def body(p_ref, w_ref, m_ref, v_ref, g_ref, d_ref, mo_ref, vo_ref):
        g = p_ref[0].astype(F32)
        for s in range(1, N_DEV):
            g = g + p_ref[s].astype(F32)
        g_ref[...] = g
        d_ref[...], mo_ref[...], vo_ref[...] = _adamw(w_ref[...], g, m_ref[...], v_ref[...])

    row = pl.BlockSpec((tr, c), lambda i: (i, 0))
    sh = jax.ShapeDtypeStruct((r, c), F32)
    return pl.pallas_call(
        body, name=name, grid=(r // tr,),
        in_specs=[pl.BlockSpec((N_DEV, tr, c), lambda i: (0, i, 0)), row, row, row],
        out_specs=[row, row, row, row], out_shape=[sh, sh, sh, sh],
        compiler_params=_params(("parallel",)),
    )(parts, w, m, v)


def _sum_slots(parts, name):
    _, r, c = parts.shape
    tr = _pick(r, 512, SUBLANE)

    def body(p_ref, o_ref):
        g = p_ref[0]
        for s in range(1, N_DEV):
            g = g + p_ref[s]
        o_ref[...] = g

    return pl.pallas_call(
        body, name=name, grid=(r // tr,),
        in_specs=[pl.BlockSpec((N_DEV, tr, c), lambda i: (0, i, 0))],
        out_specs=pl.BlockSpec((tr, c), lambda i: (i, 0)), out_shape=jax.ShapeDtypeStruct((r, c), F32),
        compiler_params=_params(("parallel",)),
    )(parts)


def _adam_rows(g, w, m, v, name):
    r, c = g.shape
    tr = _pick(r, 512, SUBLANE)

    def body(g_ref, w_ref, m_ref, v_ref, d_ref, mo_ref, vo_ref):
        d_ref[...], mo_ref[...], vo_ref[...] = _adamw(w_ref[...], g_ref[...], m_ref[...], v_ref[...])

    row = pl.BlockSpec((tr, c), lambda i: (i, 0))
    sh = jax.ShapeDtypeStruct((r, c), F32)
    return pl.pallas_call(body, name=name, grid=(r // tr,), in_specs=[row] * 4, out_specs=[row] * 3,
                          out_shape=[sh, sh, sh], compiler_params=_params(("parallel",)))(g, w, m, v)


def _pack(arrays):
    flat = jnp.concatenate([a.reshape(-1).astype(F32) for a in arrays])
    pad = (-flat.shape[0]) % (SUBLANE * LANE)
    return jnp.pad(flat, (0, pad)).reshape(-1, LANE)


def _unpack(packed, shapes):
    flat = packed.reshape(-1)
    out, off = [], 0
    for s in shapes:
        n = math.prod(s)
        out.append(flat[off:off + n].reshape(s))
        off += n
    return out


def _block_diag(t):
    eye = jnp.eye(S5_SUPER, dtype=bool)
    bd = jnp.where(eye[None, :, None, :, None], t[:, :, :, None, :], 0.0)
    return bd.reshape(S5_SUPER, S5_SUPER * t.shape[2], S5_SUPER * t.shape[3])


def _diag_blocks(dense, a, b):
    x = dense.reshape(S5_SUPER, S5_SUPER, a, S5_SUPER, b)
    return jnp.moveaxis(jnp.diagonal(x, axis1=1, axis2=3), -1, 1)


def _s5_layouts(b_re, b_im, c_re, c_im, d):
    g2 = (S5_GROUPS // S5_SUPER, S5_SUPER)
    bt = lambda b: _block_diag(b.reshape(*g2, S5_STATE, S5_GROUP).transpose(0, 1, 3, 2))
    ct = lambda c: _block_diag(c.reshape(*g2, S5_GROUP, S5_STATE).transpose(0, 1, 3, 2))
    bsg = jnp.concatenate([bt(b_re), bt(b_im)], axis=2).astype(BF16)
    ccat = jnp.concatenate([ct(c_re), -ct(c_im)], axis=1).astype(BF16)
    return bsg, ccat, d.reshape(S5_GROUPS // S5_SUPER, 1, LANE)


def _s5_param_grads(gb, gc):
    n = S5_LANES
    gb_re = _diag_blocks(gb[:, :, 0:n], S5_GROUP, S5_STATE).transpose(0, 1, 3, 2).reshape(S5_GROUPS, S5_STATE, S5_GROUP)
    gb_im = _diag_blocks(gb[:, :, n:2 * n], S5_GROUP, S5_STATE).transpose(0, 1, 3, 2).reshape(S5_GROUPS, S5_STATE, S5_GROUP)
    gc_re = _diag_blocks(gc[:, 0:n, :], S5_STATE, S5_GROUP).transpose(0, 1, 3, 2).reshape(S5_GROUPS, S5_GROUP, S5_STATE)
    gc_im = -_diag_blocks(gc[:, n:2 * n, :], S5_STATE, S5_GROUP).transpose(0, 1, 3, 2).reshape(S5_GROUPS, S5_GROUP, S5_STATE)
    return gb_re, gb_im, gc_re, gc_im


def _local_step(x, target, weight, emit, small, after=None):
    sp = small
    a_re, a_im = sp["s5_a_re"], sp["s5_a_im"]
    ldt = sp["s5_log_dt"].reshape(S5_GROUPS, 1)
    lanes = lambda a: a.reshape(S5_GROUPS // S5_SUPER, 1, S5_LANES)

    h1 = _rms_fwd(x, sp["ln_mix_g"], "rms_mix", after=after)
    w_in = weight("w_in", h1)
    proj = _mm_nn(h1, w_in, "mm_in")
    disc = _s5_param_fwd(a_re, a_im, ldt)
    bsg, ccat, dvec = _s5_layouts(sp["s5_b_re"], sp["s5_b_im"], sp["s5_c_re"], sp["s5_c_im"], sp["s5_d"])
    disc_l = [lanes(a) for a in disc]
    y, sb = _s5_fwd(proj, bsg, ccat, dvec, *disc_l)
    z16 = _gelu_fwd(y)
    w_glu = weight("s5_w_glu", z16)
    gl = _mm_nn(z16, w_glu, "mm_glu")
    z2 = _glu_fwd(y, gl, sp["s5_b_glu"])
    w_ps = weight("w_proj_s5", z2)
    ys = _mm_nn(z2, w_ps, "mm_proj_s5")
    o_raw, oh, s0s = _hgrn_fwd(proj, sp["hgrn_lb_logits"], sp["hgrn_norm_g"])
    w_ph = weight("w_proj_hgrn", oh)
    yh = _mm_nn(oh, w_ph, "mm_proj_hgrn")
    merged = _merge_fwd(proj, ys, yh)
    w_out = weight("w_out", merged)
    x1 = _mm_nn(merged, w_out, "mm_out", res=x)
    h2 = _rms_fwd(x1, sp["ln_ffn_g"], "rms_ffn")
    w_up = weight("w_up", h2)
    up = _mm_nn(h2, w_up, "mm_up")
    act = _ffn_act_fwd(up, sp["conv_w"], sp["conv_b"])
    w_down = weight("w_down", act)
    x2 = _mm_nn(act, w_down, "mm_down", res=x1)
    dx2, dx2_16, g_ln_final, loss = _loss_head(x2, sp["ln_final_g"], target)

    dact = _mm_nt(dx2_16, w_down, "mm_down_dx")
    tok = emit("w_down", _mm_tn(act, dx2_16, 1, "mm_down_dw"))
    dup_g, dup_v, dcw_g, dcw_v, dcb_g, dcb_v = _ffn_act_bwd(up, dact, sp["conv_w"], sp["conv_b"], after=tok)
    dup = jnp.concatenate([dup_g, dup_v], axis=1)
    g_conv_w = jnp.concatenate([dcw_g, dcw_v], axis=1)
    g_conv_b = jnp.concatenate([dcb_g, dcb_v], axis=1)
    dh2 = _mm_nt(dup, w_up, "mm_up_dx")
    tok = emit("w_up", _mm_tn(h2, dup, N_DEV, "mm_up_dw"))
    dx1, dx1_16, g_ln_ffn = _rms_bwd(x1, sp["ln_ffn_g"], dh2, dx2, "rms_ffn_bwd", True, after=tok)

    dmerged = _mm_nt(dx1_16, w_out, "mm_out_dx")
    tok = emit("w_out", _mm_tn(merged, dx1_16, 1, "mm_out_dw"))
    dys, dyh, dgs, dgh = _merge_bwd(proj, ys, yh, dmerged, after=tok)
    doh = _mm_nt(dyh, w_ph, "mm_proj_hgrn_dx")
    tok = emit("w_proj_hgrn", _mm_tn(oh, dyh, N_DEV, "mm_proj_hgrn_dw"))
    dz2 = _mm_nt(dys, w_ps, "mm_proj_s5_dx", after=tok)
    tok = emit("w_proj_s5", _mm_tn(z2, dys, N_DEV, "mm_proj_s5_dw"))
    dgl, dza, g_b_glu = _glu_bwd(y, gl, sp["s5_b_glu"], dz2, after=tok)
    dzb = _mm_nt(dgl, w_glu, "mm_glu_dx")
    tok = emit("s5_w_glu", _mm_tn(z16, dgl, 1, "mm_glu_dw"))
    dy = _gelu_bwd(y, dza, dzb, after=tok)
    du, gb, gc, gd, gar, gai, gcr, gci = _s5_bwd(proj, dy, sb, bsg, ccat, dvec, *disc_l)
    flat = lambda a: a.reshape(S5_GROUPS, S5_STATE)
    g_a_re, g_a_im, g_ldt = _s5_param_bwd(a_re, a_im, ldt, [flat(a) for a in (gar, gai, gcr, gci)])
    g_b_re, g_b_im, g_c_re, g_c_im = _s5_param_grads(gb, gc)
    dq, dz, dv, dg, g_norm, dlb = _hgrn_bwd(proj, o_raw, s0s, doh, sp["hgrn_lb_logits"], sp["hgrn_norm_g"])
    g_logits = _lb_bwd(sp["hgrn_lb_logits"], dlb)

    dproj = jnp.concatenate([du, dq, dz, dv, dg, dgs, dgh], axis=1)
    dh1 = _mm_nt(dproj, w_in, "mm_in_dx")
    tok = emit("w_in", _mm_tn(h1, dproj, N_DEV, "mm_in_dw"))
    grad_x, g_ln_mix = _rms_bwd(x, sp["ln_mix_g"], dh1, dx1, "rms_mix_bwd", False, after=tok)

    small_g = dict(ln_mix_g=g_ln_mix, s5_a_re=g_a_re, s5_a_im=g_a_im, s5_log_dt=g_ldt.reshape(1, S5_GROUPS),
                   s5_b_re=g_b_re, s5_b_im=g_b_im, s5_c_re=g_c_re, s5_c_im=g_c_im,
                   s5_d=gd.reshape(S5_GROUPS, S5_GROUP), s5_b_glu=g_b_glu, hgrn_lb_logits=g_logits,
                   hgrn_norm_g=g_norm, ln_ffn_g=g_ln_ffn, conv_w=g_conv_w, conv_b=g_conv_b, ln_final_g=g_ln_final)
    return loss, grad_x, small_g


BIG = ("w_in", "s5_w_glu", "w_proj_s5", "w_proj_hgrn", "w_out", "w_up", "w_down")
COL_SHARDED = ("w_in", "w_proj_s5", "w_proj_hgrn", "w_up")
SMALL = ("ln_mix_g", "s5_a_re", "s5_a_im", "s5_log_dt", "s5_b_re", "s5_b_im", "s5_c_re", "s5_c_im", "s5_d",
         "s5_b_glu", "hgrn_lb_logits", "hgrn_norm_g", "ln_ffn_g", "conv_b", "ln_final_g")
WEIGHTS = ("ln_mix_g", "w_in", "s5_a_re", "s5_a_im", "s5_log_dt", "s5_b_re", "s5_b_im", "s5_c_re", "s5_c_im", "s5_d",
           "s5_w_glu", "s5_b_glu", "w_proj_s5", "hgrn_lb_logits", "hgrn_norm_g", "w_proj_hgrn", "w_out", "ln_ffn_g",
           "w_up", "conv_w", "conv_b", "w_down", "ln_final_g")


def kernel(x, ln_mix_g, w_in, s5_a_re, s5_a_im, s5_log_dt, s5_b_re, s5_b_im, s5_c_re, s5_c_im, s5_d, s5_w_glu, s5_b_glu, w_proj_s5, hgrn_lb_logits, hgrn_norm_g, w_proj_hgrn, w_out, ln_ffn_g, w_up, conv_w, conv_b, w_down, ln_final_g, loss_target, m_ln_mix_g, m_w_in, m_s5_a_re, m_s5_a_im, m_s5_log_dt, m_s5_b_re, m_s5_b_im, m_s5_c_re, m_s5_c_im, m_s5_d, m_s5_w_glu, m_s5_b_glu, m_w_proj_s5, m_hgrn_lb_logits, m_hgrn_norm_g, m_w_proj_hgrn, m_w_out, m_ln_ffn_g, m_w_up, m_conv_w, m_conv_b, m_w_down, m_ln_final_g, v_ln_mix_g, v_w_in, v_s5_a_re, v_s5_a_im, v_s5_log_dt, v_s5_b_re, v_s5_b_im, v_s5_c_re, v_s5_c_im, v_s5_d, v_s5_w_glu, v_s5_b_glu, v_w_proj_s5, v_hgrn_lb_logits, v_hgrn_norm_g, v_w_proj_hgrn, v_w_out, v_ln_ffn_g, v_w_up, v_conv_w, v_conv_b, v_w_down, v_ln_final_g):
    given = dict(locals())
    w = {n: given[n] for n in WEIGHTS}
    mom = {n: given["m_" + n] for n in WEIGHTS}
    var = {n: given["v_" + n] for n in WEIGHTS}

    shard16 = {n: w[n][0].astype(BF16) for n in BIG}
    w_in_all, conv_w_all = _all_gather([shard16["w_in"], conv_w[0]], "gather_first")
    gather_groups = (("s5_w_glu", "w_proj_s5", "w_proj_hgrn", "w_out"), ("w_up",), ("w_down",))
    pending, token = {}, w_in_all
    for i, group in enumerate(gather_groups):
        handle = _exchange_start([shard16[n] for n in group], False, token, f"gather_start_{i}")
        token = handle["token"]
        for n in group:
            pending[n] = (group, handle, f"gather_wait_{i}")
    ready = {"w_in": w_in_all}

    def weight(name, after):
        if name not in ready:
            group, handle, wait_name = pending[name]
            for n, g in zip(group, _exchange_wait(handle, [after], wait_name)):
                ready[n] = g
        g = ready[name]
        return g if name in COL_SHARDED else g.reshape(1, N_DEV * g.shape[1], g.shape[2])

    scatter_groups = (("w_down",), ("w_up",), ("w_out", "w_proj_hgrn", "w_proj_s5", "s5_w_glu"), ("w_in",))
    emitted, scatters = {}, []

    def emit(name, grad):
        emitted[name] = grad if name in COL_SHARDED else grad.reshape(N_DEV, -1, grad.shape[2])
        group = scatter_groups[len(scatters)]
        if not all(n in emitted for n in group):
            return None
        handle = _exchange_start([emitted[n] for n in group], True, None, f"scatter_start_{len(scatters)}")
        scatters.append((group, handle))
        return handle["token"]

    small = dict(ln_mix_g=ln_mix_g, s5_a_re=s5_a_re[0], s5_a_im=s5_a_im[0], s5_log_dt=s5_log_dt,
                 s5_b_re=s5_b_re[0], s5_b_im=s5_b_im[0], s5_c_re=s5_c_re[0], s5_c_im=s5_c_im[0], s5_d=s5_d[0],
                 s5_b_glu=s5_b_glu, hgrn_lb_logits=hgrn_lb_logits, hgrn_norm_g=hgrn_norm_g, ln_ffn_g=ln_ffn_g,
                 conv_w=conv_w_all.transpose(1, 0, 2).reshape(3, 2 * D_FF), conv_b=conv_b,
                 ln_final_g=ln_final_g.reshape(1, D_MODEL))
    loss, grad_x, small_g = _local_step(x[0], loss_target[0], weight, emit, small, after=token)

    names = SMALL + ("conv_w",)
    shapes = [small_g[n].shape for n in names] + [(1,)]
    packed = _pack([small_g[n] for n in names] + [loss[0, 0:1]])
    total = _sum_slots(_all_gather([packed], "gather_small")[0], "sum_small")
    summed = dict(zip(names + ("loss",), _unpack(total, shapes)))

    grads, delta, new_m, new_v = {}, {}, {}, {}
    afters = [grad_x, total]
    for i, (group, handle) in enumerate(scatters):
        for n, r in zip(group, _exchange_wait(handle, afters, f"scatter_wait_{i}")):
            g, d, m2, v2 = _sum_adam(r, w[n][0], mom[n][0], var[n][0], "adam_" + n)
            grads[n], delta[n], new_m[n], new_v[n] = g[None], d[None], m2[None], v2[None]
        if i == len(scatters) - 2:
            afters = [delta[n] for g2, _ in scatters[:-1] for n in g2]

    pw = _pack([w[n] for n in SMALL])
    d_s, m_s, v_s = _adam_rows(_pack([summed[n] for n in SMALL]), pw, _pack([mom[n] for n in SMALL]),
                               _pack([var[n] for n in SMALL]), "adam_small")
    wshapes = [w[n].shape for n in SMALL]
    for n, g, d, m2, v2 in zip(SMALL, [summed[n] for n in SMALL], _unpack(d_s, wshapes), _unpack(m_s, wshapes),
                               _unpack(v_s, wshapes)):
        grads[n], delta[n], new_m[n], new_v[n] = g.reshape(w[n].shape), d, m2, v2
    me = 4 * lax.axis_index("x") + 2 * lax.axis_index("y") + lax.axis_index("c")
    ncol = conv_w.shape[2]
    g_cw = lax.dynamic_slice_in_dim(summed["conv_w"], me * ncol, ncol, axis=1)
    d_cw, m_cw, v_cw = _adam_rows(g_cw, conv_w[0], m_conv_w[0], v_conv_w[0], "adam_conv_w")
    grads["conv_w"], delta["conv_w"], new_m["conv_w"], new_v["conv_w"] = g_cw[None], d_cw[None], m_cw[None], v_cw[None]

    return (summed["loss"].reshape(()), grad_x[None], *[grads[n] for n in WEIGHTS], *[delta[n] for n in WEIGHTS],
            *[new_m[n] for n in WEIGHTS], *[new_v[n] for n in WEIGHTS])
```

```python
import math

import jax
import jax.numpy as jnp
from jax import lax
from jax.experimental import pallas as pl
from jax.experimental.pallas import tpu as pltpu

F32 = jnp.float32
BF16 = jnp.bfloat16

N_DEV = 8
D_MODEL = 2048
S5_WIDTH = 1024
S5_GROUP = 16
S5_GROUPS = 64
S5_STATE = 64
S5_MAX_RE = -1e-4
S5_SUPER = 8
S5_LANES = S5_SUPER * S5_STATE
HGRN_WIDTH = 1024
HGRN_HEADS = 8
HGRN_DH = 128
HGRN_CHUNK = 64
D_FF = 5632
RMS_EPS = 1e-6
ADAM_LR = 0.001
ADAM_B1 = 0.9
ADAM_B2 = 0.999
ADAM_EPS = 1e-08
ADAM_WD = 0.01
ADAM_STEP = 10

LANE = 128
SUBLANE = 8
VMEM_LIMIT = 48 * 1024 * 1024
MESH = pl.DeviceIdType.MESH
GELU_C = math.sqrt(2.0 / math.pi)
GELU_A = 0.044715


def _params(sem=None):
    return pltpu.CompilerParams(dimension_semantics=sem, vmem_limit_bytes=VMEM_LIMIT)


def _pick(n, cap, unit=LANE):
    best = None
    for t in range(unit, min(n, cap) + 1, unit):
        if n % t == 0:
            best = t
    return best if best is not None else n


def _ordered(body, in_specs, args, after):
    if after is None:
        return body, list(in_specs), list(args)
    n_in = len(args)

    def ordered_body(*refs):
        return body(*refs[:n_in], *refs[n_in + 1:])

    return ordered_body, [*in_specs, pl.BlockSpec(memory_space=pl.ANY)], [*args, after]


def _sigmoid(x):
    return 1.0 / (1.0 + jnp.exp(-x))


def _silu_and_grad(x):
    s = _sigmoid(x)
    return x * s, s * (1.0 + x * (1.0 - s))


def _gelu_and_grad(y):
    inner = GELU_C * (y + GELU_A * y * y * y)
    th = jnp.tanh(inner)
    val = 0.5 * y * (1.0 + th)
    grad = 0.5 * (1.0 + th) + 0.5 * y * (1.0 - th * th) * GELU_C * (1.0 + 3.0 * GELU_A * y * y)
    return val, grad


def _dot(a, b):
    return jnp.dot(a, b, preferred_element_type=F32)


def _dot_nt(a, b):
    return lax.dot_general(a, b, (((1,), (1,)), ((), ())), preferred_element_type=F32)


def _dot_tn(a, b):
    return lax.dot_general(a, b, (((0,), (0,)), ((), ())), preferred_element_type=F32)


def _blocks_per_step(nb, ns, tn, cap=2048):
    if tn != ns:
        return 1
    best = 1
    for b in range(1, nb + 1):
        if nb % b == 0 and b * ns <= cap:
            best = b
    return best


def _mm_nn(a, w, name, res=None, out_dtype=F32):
    m, kdim = a.shape
    nb, _, ns = w.shape
    tm, tk, tn = _pick(m, 512), _pick(kdim, 2048), _pick(ns, 1536)
    npb, nk = ns // tn, kdim // tk
    bps = _blocks_per_step(nb, ns, tn)
    assert bps == 1 or nk == 1

    def body(*refs):
        a_ref, w_ref = refs[0], refs[1]
        r_ref = refs[2] if res is not None else None
        o_ref = refs[3] if res is not None else refs[2]

        def finish(r, cols):
            if res is not None:
                r = r + r_ref[:, cols]
            o_ref[:, cols] = r.astype(out_dtype)

        if nk == 1:
            for b in range(bps):
                finish(_dot(a_ref[...], w_ref[b]), slice(b * tn, (b + 1) * tn))
            return
        acc = refs[-1]
        k = pl.program_id(2)

        @pl.when(k == 0)
        def _():
            acc[...] = jnp.zeros_like(acc)

        acc[...] += _dot(a_ref[...], w_ref[0])

        @pl.when(k == nk - 1)
        def _():
            finish(acc[...], slice(0, tn))

    in_specs = [pl.BlockSpec((tm, tk), lambda j, i, k: (i, k)),
                pl.BlockSpec((bps, tk, tn), lambda j, i, k: (j // npb, k, j % npb))]
    args = [a, w]
    if res is not None:
        in_specs.append(pl.BlockSpec((tm, bps * tn), lambda j, i, k: (i, j)))
        args.append(res)
    return pl.pallas_call(
        body, name=name, grid=(nb * npb // bps, m // tm, nk),
        in_specs=in_specs, out_specs=pl.BlockSpec((tm, bps * tn), lambda j, i, k: (i, j)),
        out_shape=jax.ShapeDtypeStruct((m, nb * ns), out_dtype),
        scratch_shapes=[pltpu.VMEM((tm, tn), F32)] if nk > 1 else [],
        compiler_params=_params(("parallel", "parallel", "arbitrary")),
    )(*args)


def _mm_nt(a, w, name, out_dtype=F32, after=None):
    m, _ = a.shape
    nb, kdim, ns = w.shape
    tm, tko, tn = _pick(m, 1024), _pick(kdim, 1024), _pick(ns, 2048)
    npb = ns // tn
    bps = _blocks_per_step(nb, ns, tn)
    nred = nb * npb // bps

    def body(a_ref, w_ref, o_ref, *scratch):
        total = _dot_nt(a_ref[:, 0:tn], w_ref[0])
        for b in range(1, bps):
            total = total + _dot_nt(a_ref[:, b * tn:(b + 1) * tn], w_ref[b])
        if nred == 1:
            o_ref[...] = total.astype(out_dtype)
            return
        acc = scratch[0]
        n = pl.program_id(2)

        @pl.when(n == 0)
        def _():
            acc[...] = jnp.zeros_like(acc)

        acc[...] += total

        @pl.when(n == nred - 1)
        def _():
            o_ref[...] = acc[...].astype(out_dtype)

    in_specs = [pl.BlockSpec((tm, bps * tn), lambda i, j, n: (i, n)),
                pl.BlockSpec((bps, tko, tn), lambda i, j, n: (n // npb, j, n % npb))]
    body, in_specs, args = _ordered(body, in_specs, [a, w], after)
    return pl.pallas_call(
        body, name=name, grid=(m // tm, kdim // tko, nred),
        in_specs=in_specs,
        out_specs=pl.BlockSpec((tm, tko), lambda i, j, n: (i, j)),
        out_shape=jax.ShapeDtypeStruct((m, kdim), out_dtype),
        scratch_shapes=[pltpu.VMEM((tm, tko), F32)] if nred > 1 else [],
        compiler_params=_params(("parallel", "parallel", "arbitrary")),
    )(*args)


def _mm_tn(a, d, nb, name, out_dtype=BF16):
    m, kdim = a.shape
    ns = d.shape[1] // nb
    tm, tko, tn = _pick(m, 4096), _pick(kdim, 512), _pick(ns, 1536)
    npb, nm = ns // tn, m // tm

    def body(a_ref, d_ref, o_ref, *scratch):
        if nm == 1:
            o_ref[...] = _dot_tn(a_ref[...], d_ref[...]).astype(out_dtype)
            return
        acc = scratch[0]
        r = pl.program_id(2)

        @pl.when(r == 0)
        def _():
            acc[...] = jnp.zeros_like(acc)

        acc[...] += _dot_tn(a_ref[...], d_ref[...])

        @pl.when(r == nm - 1)
        def _():
            o_ref[...] = acc[...].astype(out_dtype)

    return pl.pallas_call(
        body, name=name, grid=(nb * npb, kdim // tko, nm),
        in_specs=[pl.BlockSpec((tm, tko), lambda j, i, r: (r, i)),
                  pl.BlockSpec((tm, tn), lambda j, i, r: (r, j))],
        out_specs=pl.BlockSpec((None, tko, tn), lambda j, i, r: (j // npb, i, j % npb)),
        out_shape=jax.ShapeDtypeStruct((nb, kdim, ns), out_dtype),
        scratch_shapes=[pltpu.VMEM((tko, tn), F32)] if nm > 1 else [],
        compiler_params=_params(("parallel", "parallel", "arbitrary")),
    )(a, d)


def _rms_fwd(x, g, name, after=None):
    t, d = x.shape
    tr = _pick(t, 256, SUBLANE)

    def body(x_ref, g_ref, h_ref):
        xv = x_ref[...]
        r = lax.rsqrt(jnp.mean(xv * xv, axis=-1, keepdims=True) + RMS_EPS)
        h_ref[...] = (xv * r * g_ref[...]).astype(BF16)

    in_specs = [pl.BlockSpec((tr, d), lambda i: (i, 0)), pl.BlockSpec((1, d), lambda i: (0, 0))]
    body, in_specs, args = _ordered(body, in_specs, [x, g], after)
    return pl.pallas_call(
        body, name=name, grid=(t // tr,),
        in_specs=in_specs,
        out_specs=pl.BlockSpec((tr, d), lambda i: (i, 0)),
        out_shape=jax.ShapeDtypeStruct((t, d), BF16),
        compiler_params=_params(("parallel",)),
    )(*args)


def _rms_bwd(x, g, dh, add, name, want_bf16, after=None):
    t, d = x.shape
    tr = _pick(t, 256, SUBLANE)

    def body(x_ref, g_ref, dh_ref, add_ref, *outs):
        if want_bf16:
            dx_ref, dxb_ref, dg_ref = outs
        else:
            dx_ref, dg_ref = outs
        i = pl.program_id(0)

        @pl.when(i == 0)
        def _():
            dg_ref[...] = jnp.zeros_like(dg_ref)

        xv, dhv = x_ref[...], dh_ref[...]
        r = lax.rsqrt(jnp.mean(xv * xv, axis=-1, keepdims=True) + RMS_EPS)
        xh = xv * r
        dg_ref[...] += jnp.sum(dhv * xh, axis=0, keepdims=True)
        dxh = dhv * g_ref[...]
        dx = add_ref[...] + r * (dxh - xh * jnp.mean(dxh * xh, axis=-1, keepdims=True))
        dx_ref[...] = dx
        if want_bf16:
            dxb_ref[...] = dx.astype(BF16)

    row = pl.BlockSpec((tr, d), lambda i: (i, 0))
    vec = pl.BlockSpec((1, d), lambda i: (0, 0))
    out_specs = [row] + ([row] if want_bf16 else []) + [vec]
    out_shape = ([jax.ShapeDtypeStruct((t, d), F32)] + ([jax.ShapeDtypeStruct((t, d), BF16)] if want_bf16 else [])
                 + [jax.ShapeDtypeStruct((1, d), F32)])
    body, in_specs, args = _ordered(body, [row, vec, row, row], [x, g, dh, add], after)
    return pl.pallas_call(
        body, name=name, grid=(t // tr,),
        in_specs=in_specs, out_specs=out_specs, out_shape=out_shape,
        compiler_params=_params(("arbitrary",)),
    )(*args)


def _loss_head(x2, g, target, name="loss_head"):
    t, d = x2.shape
    tr = _pick(t, 256, SUBLANE)

    def body(x_ref, g_ref, t_ref, dx_ref, dxb_ref, dg_ref, loss_ref):
        i = pl.program_id(0)

        @pl.when(i == 0)
        def _():
            dg_ref[...] = jnp.zeros_like(dg_ref)
            loss_ref[...] = jnp.zeros_like(loss_ref)

        xv = x_ref[...]
        gv = g_ref[...]
        r = lax.rsqrt(jnp.mean(xv * xv, axis=-1, keepdims=True) + RMS_EPS)
        xh = xv * r
        err = xh * gv - t_ref[...]
        part = 0.5 * jnp.sum(jnp.mean(err * err, axis=-1, keepdims=True), axis=0, keepdims=True)
        loss_ref[...] += jnp.broadcast_to(part, loss_ref.shape)
        dy = err * (1.0 / d)
        dg_ref[...] += jnp.sum(dy * xh, axis=0, keepdims=True)
        dxh = dy * gv
        dx = r * (dxh - xh * jnp.mean(dxh * xh, axis=-1, keepdims=True))
        dx_ref[...] = dx
        dxb_ref[...] = dx.astype(BF16)

    row = pl.BlockSpec((tr, d), lambda i: (i, 0))
    vec = pl.BlockSpec((1, d), lambda i: (0, 0))
    return pl.pallas_call(
        body, name=name, grid=(t // tr,),
        in_specs=[row, vec, row],
        out_specs=[row, row, vec, pl.BlockSpec((1, LANE), lambda i: (0, 0))],
        out_shape=[jax.ShapeDtypeStruct((t, d), F32), jax.ShapeDtypeStruct((t, d), BF16),
                   jax.ShapeDtypeStruct((1, d), F32), jax.ShapeDtypeStruct((1, LANE), F32)],
        compiler_params=_params(("arbitrary",)),
    )(x2, g, target)


def _s5_discretize(a_re, a_im, ldt):
    lam_re = jnp.minimum(a_re, S5_MAX_RE)
    lam_im = a_im
    dt = jnp.exp(ldt)
    mag = jnp.exp(lam_re * dt)
    abar_re = mag * jnp.cos(lam_im * dt)
    abar_im = mag * jnp.sin(lam_im * dt)
    den = lam_re * lam_re + lam_im * lam_im
    nr = abar_re - 1.0
    ni = abar_im
    coef_re = (nr * lam_re + ni * lam_im) / den
    coef_im = (ni * lam_re - nr * lam_im) / den
    return abar_re, abar_im, coef_re, coef_im


def _s5_param_fwd(a_re, a_im, ldt):
    def body(ar_ref, ai_ref, l_ref, o0, o1, o2, o3):
        outs = _s5_discretize(ar_ref[...], ai_ref[...], l_ref[...])
        for o, v in zip((o0, o1, o2, o3), outs):
            o[...] = v

    sh = jax.ShapeDtypeStruct(a_re.shape, F32)
    return pl.pallas_call(body, name="s5_param_fwd", out_shape=[sh, sh, sh, sh], compiler_params=_params())(a_re, a_im, ldt)


def _s5_param_bwd(a_re, a_im, ldt, cts):
    def body(ar_ref, ai_ref, l_ref, c0, c1, c2, c3, g0, g1, g2):
        _, vjp = jax.vjp(_s5_discretize, ar_ref[...], ai_ref[...], l_ref[...])
        ga, gb, gl = vjp((c0[...], c1[...], c2[...], c3[...]))
        g0[...] = ga
        g1[...] = gb
        g2[...] = gl

    sh = jax.ShapeDtypeStruct(a_re.shape, F32)
    return pl.pallas_call(body, name="s5_param_bwd", out_shape=[sh, sh, jax.ShapeDtypeStruct(ldt.shape, F32)],
                          compiler_params=_params())(a_re, a_im, ldt, *cts)


def _cmul(ar, ai, br, bi):
    return ar * br - ai * bi, ar * bi + ai * br


S5_TC = 128
S5_TILE = S5_SUPER * SUBLANE
S5_HALF = S5_TILE // 2


def _s5_to_tile(re, im):
    f = lambda a: a.reshape(S5_SUPER, S5_LANES // LANE, LANE).transpose(1, 0, 2).reshape(S5_HALF, LANE)
    return jnp.concatenate([f(re), f(im)], axis=0)


def _s5_from_tile(tile):
    f = lambda a: a.reshape(S5_LANES // LANE, S5_SUPER, LANE).transpose(1, 0, 2).reshape(S5_GROUPS, S5_STATE)
    return f(tile[0:S5_HALF]), f(tile[S5_HALF:])


RE = slice(0, S5_HALF)
IM = slice(S5_HALF, S5_TILE)


def _s5_scatter_rows(buf, rows, first_tile=0):
    tc = rows[0].shape[0]
    for j in range(SUBLANE):
        stacked = jnp.stack([r[:, j * LANE:(j + 1) * LANE] for r in rows], axis=0)
        buf[first_tile:first_tile + tc, j * SUBLANE:(j + 1) * SUBLANE, :] = jnp.swapaxes(stacked, 0, 1)


def _s5_gather_rows(buf, tc, first_tile=0):
    per_j = [jnp.swapaxes(buf[first_tile:first_tile + tc, j * SUBLANE:(j + 1) * SUBLANE, :], 0, 1)
             for j in range(SUBLANE)]
    return [jnp.concatenate([per_j[j][k] for j in range(SUBLANE)], axis=1) for k in range(S5_SUPER)]


def _s5_fwd(proj, bsg, ccat, d_row, abar_t, coef_t):
    t = proj.shape[0]
    tc = min(t, S5_TC)
    n_chunks = t // tc

    def body(u_ref, b_ref, c_ref, d_ref, a_ref, cf_ref, y_ref, sb_ref, x, car):
        @pl.when(pl.program_id(0) == 0)
        def _():
            car[...] = jnp.zeros_like(car)

        sb_ref[...] = car[...]
        u = u_ref[...]
        _s5_scatter_rows(x, [_dot(u[:, k * LANE:(k + 1) * LANE].astype(BF16), b_ref[k]) for k in range(S5_SUPER)])
        ar, ai = a_ref[RE, :], a_ref[IM, :]
        cr, ci = cf_ref[RE, :], cf_ref[IM, :]

        def step(i, carry):
            sr, si = carry
            xr, xi = _cmul(cr, ci, x[i, RE, :], x[i, IM, :])
            sr, si = ar * sr - ai * si + xr, ar * si + ai * sr + xi
            x[i, RE, :] = sr
            x[i, IM, :] = si
            return sr, si

        sr, si = lax.fori_loop(0, tc, step, (car[RE, :], car[IM, :]), unroll=4)
        car[RE, :] = sr
        car[IM, :] = si
        for k, s_k in enumerate(_s5_gather_rows(x, tc)):
            cols = slice(k * LANE, (k + 1) * LANE)
            y_ref[:, cols] = _dot(s_k.astype(BF16), c_ref[k]) + d_ref[:, cols] * u[:, cols]

    full = lambda shape: pl.BlockSpec(shape, lambda c: (0,) * len(shape))
    return pl.pallas_call(
        body, name="s5_fwd", grid=(n_chunks,),
        in_specs=[pl.BlockSpec((tc, S5_WIDTH), lambda c: (c, 0)), full(bsg.shape), full(ccat.shape), full(d_row.shape),
                  full(abar_t.shape), full(coef_t.shape)],
        out_specs=[pl.BlockSpec((tc, S5_WIDTH), lambda c: (c, 0)), pl.BlockSpec((None, S5_TILE, LANE), lambda c: (c, 0, 0))],
        out_shape=[jax.ShapeDtypeStruct((t, S5_WIDTH), F32), jax.ShapeDtypeStruct((n_chunks, S5_TILE, LANE), F32)],
        scratch_shapes=[pltpu.VMEM((tc, S5_TILE, LANE), F32), pltpu.VMEM((S5_TILE, LANE), F32)],
        compiler_params=_params(("arbitrary",)),
    )(proj, bsg, ccat, d_row, abar_t, coef_t)


def _s5_bwd(proj, dy, sb, bsg, ccat, d_row, abar_t, coef_t):
    t = proj.shape[0]
    tc = min(t, S5_TC)
    n_chunks = t // tc
    last = n_chunks - 1

    def body(u_ref, dy_ref, sb_ref, b_ref, c_ref, d_ref, a_ref, cf_ref,
             du_ref, gb_ref, gc_ref, gd_ref, ga_ref, gcf_ref, xb, xs, xg, gcar, acc):
        @pl.when(pl.program_id(0) == 0)
        def _():
            gcar[...] = jnp.zeros_like(gcar)
            acc[...] = jnp.zeros_like(acc)
            gb_ref[...] = jnp.zeros_like(gb_ref)
            gc_ref[...] = jnp.zeros_like(gc_ref)
            gd_ref[...] = jnp.zeros_like(gd_ref)

        u = u_ref[...]
        dyv = dy_ref[...]
        u16, dy16 = u.astype(BF16), dyv.astype(BF16)
        subs = [slice(k * LANE, (k + 1) * LANE) for k in range(S5_SUPER)]
        _s5_scatter_rows(xb, [_dot(u16[:, c], b_ref[k]) for k, c in enumerate(subs)])
        _s5_scatter_rows(xg, [_dot_nt(dy16[:, c], c_ref[k]) for k, c in enumerate(subs)])
        ar, ai = a_ref[RE, :], a_ref[IM, :]
        cr, ci = cf_ref[RE, :], cf_ref[IM, :]

        xs[0] = sb_ref[...]

        def fstep(i, carry):
            sr, si = carry
            xr, xi = _cmul(cr, ci, xb[i, RE, :], xb[i, IM, :])
            sr, si = ar * sr - ai * si + xr, ar * si + ai * sr + xi
            xs[i + 1, RE, :] = sr
            xs[i + 1, IM, :] = si
            return sr, si

        lax.fori_loop(0, tc, fstep, (sb_ref[RE, :], sb_ref[IM, :]), unroll=4)

        def rstep(n, carry):
            gr, gi, a0, a1, a2, a3 = carry
            i = tc - 1 - n
            xr = xg[i, RE, :] + ar * gr + ai * gi
            xi = xg[i, IM, :] + ar * gi - ai * gr
            pr, pi = xs[i, RE, :], xs[i, IM, :]
            br, bi = xb[i, RE, :], xb[i, IM, :]
            a0 = a0 + pr * xr + pi * xi
            a1 = a1 + pr * xi - pi * xr
            a2 = a2 + br * xr + bi * xi
            a3 = a3 + br * xi - bi * xr
            xg[i, RE, :] = cr * xr + ci * xi
            xg[i, IM, :] = cr * xi - ci * xr
            return xr, xi, a0, a1, a2, a3

        init = (gcar[RE, :], gcar[IM, :], acc[0], acc[1], acc[2], acc[3])
        gr, gi, a0, a1, a2, a3 = lax.fori_loop(0, tc, rstep, init, unroll=2)
        gcar[RE, :] = gr
        gcar[IM, :] = gi
        for idx, a in enumerate((a0, a1, a2, a3)):
            acc[idx] = a
        ga_ref[RE, :] = a0
        ga_ref[IM, :] = a1
        gcf_ref[RE, :] = a2
        gcf_ref[IM, :] = a3

        g_rows = _s5_gather_rows(xg, tc)
        s_rows = _s5_gather_rows(xs, tc, first_tile=1)
        for k in range(S5_SUPER):
            cols = subs[k]
            g16 = g_rows[k].astype(BF16)
            s16 = s_rows[k].astype(BF16)
            gb_ref[k] += _dot_tn(u16[:, cols], g16)
            gc_ref[k] += _dot_tn(s16, dy16[:, cols])
            du_ref[:, cols] = (_dot_nt(g16, b_ref[k]) + d_ref[:, cols] * dyv[:, cols]).astype(BF16)
        gd_ref[...] += jnp.sum(dyv * u, axis=0, keepdims=True)

    full = lambda shape: pl.BlockSpec(shape, lambda c: (0,) * len(shape))
    rows = pl.BlockSpec((tc, S5_WIDTH), lambda c: (last - c, 0))
    tile = (S5_TILE, LANE)
    return pl.pallas_call(
        body, name="s5_bwd", grid=(n_chunks,),
        in_specs=[rows, rows, pl.BlockSpec((None, S5_TILE, LANE), lambda c: (last - c, 0, 0)),
                  full(bsg.shape), full(ccat.shape), full(d_row.shape), full(abar_t.shape), full(coef_t.shape)],
        out_specs=[rows, full(bsg.shape), full(ccat.shape), full(d_row.shape), full(tile), full(tile)],
        out_shape=[jax.ShapeDtypeStruct((t, S5_WIDTH), BF16), jax.ShapeDtypeStruct(bsg.shape, F32),
                   jax.ShapeDtypeStruct(ccat.shape, F32), jax.ShapeDtypeStruct(d_row.shape, F32),
                   jax.ShapeDtypeStruct(tile, F32), jax.ShapeDtypeStruct(tile, F32)],
        scratch_shapes=[pltpu.VMEM((tc, S5_TILE, LANE), F32), pltpu.VMEM((tc + 1, S5_TILE, LANE), F32),
                        pltpu.VMEM((tc, S5_TILE, LANE), F32), pltpu.VMEM(tile, F32),
                        pltpu.VMEM((4, S5_HALF, LANE), F32)],
        compiler_params=_params(("arbitrary",)),
    )(proj, dy, sb, bsg, ccat, d_row, abar_t, coef_t)


def _gelu_fwd(y, name="s5_gelu"):
    t, w = y.shape
    tr = _pick(t, 512, SUBLANE)

    def body(y_ref, z_ref):
        z_ref[...] = _gelu_and_grad(y_ref[...])[0].astype(BF16)

    row = pl.BlockSpec((tr, w), lambda i: (i, 0))
    return pl.pallas_call(body, name=name, grid=(t // tr,), in_specs=[row], out_specs=row,
                          out_shape=jax.ShapeDtypeStruct((t, w), BF16), compiler_params=_params(("parallel",)))(y)


def _glu_fwd(y, gl, b, name="s5_glu"):
    t, w = y.shape
    tr = _pick(t, 512, SUBLANE)

    def body(y_ref, gl_ref, b_ref, z2_ref):
        z = _gelu_and_grad(y_ref[...])[0]
        z2_ref[...] = (z * _sigmoid(gl_ref[...] + b_ref[...])).astype(BF16)

    row = pl.BlockSpec((tr, w), lambda i: (i, 0))
    return pl.pallas_call(body, name=name, grid=(t // tr,),
                          in_specs=[row, row, pl.BlockSpec((1, w), lambda i: (0, 0))], out_specs=row,
                          out_shape=jax.ShapeDtypeStruct((t, w), BF16), compiler_params=_params(("parallel",)))(y, gl, b)


def _glu_bwd(y, gl, b, dz2, name="s5_glu_bwd", after=None):
    t, w = y.shape
    tr = _pick(t, 512, SUBLANE)

    def body(y_ref, gl_ref, b_ref, dz2_ref, dgl_ref, dza_ref, db_ref):
        @pl.when(pl.program_id(0) == 0)
        def _():
            db_ref[...] = jnp.zeros_like(db_ref)

        z = _gelu_and_grad(y_ref[...])[0]
        s = _sigmoid(gl_ref[...] + b_ref[...])
        dz2v = dz2_ref[...]
        dgl = dz2v * z * s * (1.0 - s)
        dgl_ref[...] = dgl.astype(BF16)
        dza_ref[...] = dz2v * s
        db_ref[...] += jnp.sum(dgl, axis=0, keepdims=True)

    row = pl.BlockSpec((tr, w), lambda i: (i, 0))
    vec = pl.BlockSpec((1, w), lambda i: (0, 0))
    body, in_specs, args = _ordered(body, [row, row, vec, row], [y, gl, b, dz2], after)
    return pl.pallas_call(body, name=name, grid=(t // tr,), in_specs=in_specs, out_specs=[row, row, vec],
                          out_shape=[jax.ShapeDtypeStruct((t, w), BF16), jax.ShapeDtypeStruct((t, w), F32),
                                     jax.ShapeDtypeStruct((1, w), F32)],
                          compiler_params=_params(("arbitrary",)))(*args)


def _gelu_bwd(y, dza, dzb, name="s5_gelu_bwd", after=None):
    t, w = y.shape
    tr = _pick(t, 512, SUBLANE)

    def body(y_ref, a_ref, b_ref, dy_ref):
        dy_ref[...] = (a_ref[...] + b_ref[...]) * _gelu_and_grad(y_ref[...])[1]

    row = pl.BlockSpec((tr, w), lambda i: (i, 0))
    body, in_specs, args = _ordered(body, [row, row, row], [y, dza, dzb], after)
    return pl.pallas_call(body, name=name, grid=(t // tr,), in_specs=in_specs, out_specs=row,
                          out_shape=jax.ShapeDtypeStruct((t, w), F32), compiler_params=_params(("parallel",)))(*args)


def _tri_dot(tri16, x):
    hi = x.astype(BF16)
    r1 = x - hi.astype(F32)
    mid = r1.astype(BF16)
    lo = (r1 - mid.astype(F32)).astype(BF16)
    return _dot(tri16, hi) + _dot(tri16, mid) + _dot(tri16, lo)


def _hgrn_pre(q_in, z, lg):
    lb = _sigmoid(lg[0:1, :] - lg[1:2, :])
    qs, dqs = _silu_and_grad(q_in)
    sz = _sigmoid(z)
    f = lb + (1.0 - lb) * sz
    k = (1.0 - lb) * (1.0 - sz)
    c = HGRN_CHUNK
    r = lax.broadcasted_iota(jnp.int32, (c, c), 0)
    s = lax.broadcasted_iota(jnp.int32, (c, c), 1)
    causal = r >= s
    b = _tri_dot(jnp.where(causal, 1.0, 0.0).astype(BF16), jnp.log(f))
    b_end = b[c - 1:c, :]
    b_mid = b[c // 2 - 1:c // 2, :]
    e_q, e_k, e_0, e_c = jnp.exp(b - b_mid), jnp.exp(b_mid - b), jnp.exp(b), jnp.exp(b_end - b)
    return dict(lb=lb, qs=qs, dqs=dqs, sz=sz, f=f, k=k, causal=causal, b_end=b_end,
                e_q=e_q, e_k=e_k, e_0=e_0, e_c=e_c,
                qt=qs * e_q, kt=k * e_k, q0=qs * e_0, kc=k * e_c)


def _hgrn_fwd(proj, logits, ng):
    t = proj.shape[0]
    c, dh = HGRN_CHUNK, HGRN_DH
    n_chunks = t // c

    def head(h, q_ref, z_ref, v_ref, g_ref, lg_ref, ng_ref, o_ref, oh_ref, s0_ref, st):
        sl = slice(h * dh, (h + 1) * dh)
        s0 = st[h]
        s0_ref[h] = s0
        p = _hgrn_pre(q_ref[:, sl], z_ref[:, sl], lg_ref[:, sl])
        v16 = v_ref[:, sl].astype(BF16)
        a = jnp.where(p["causal"], _dot_nt(p["qt"].astype(BF16), p["kt"].astype(BF16)), 0.0)
        o = _dot_nt(p["q0"].astype(BF16), s0.astype(BF16)) + _dot(a.astype(BF16), v16)
        st[h] = jnp.exp(p["b_end"]) * s0 + _dot_tn(v16, p["kc"].astype(BF16))
        o_ref[:, sl] = o
        rn = lax.rsqrt(jnp.mean(o * o, axis=-1, keepdims=True) + RMS_EPS)
        oh_ref[:, sl] = (o * rn * ng_ref[:, sl] * _silu_and_grad(g_ref[:, sl])[0]).astype(BF16)

    def body(*refs):
        st = refs[-1]

        @pl.when(pl.program_id(0) == 0)
        def _():
            st[...] = jnp.zeros_like(st)

        for h in range(HGRN_HEADS):
            head(h, *refs)

    def wide(off):
        return pl.BlockSpec((c, HGRN_WIDTH), lambda i: (i, off))

    return pl.pallas_call(
        body, name="hgrn_fwd", grid=(n_chunks,),
        in_specs=[wide(1), wide(2), wide(3), wide(4),
                  pl.BlockSpec((2, HGRN_WIDTH), lambda i: (0, 0)), pl.BlockSpec((1, HGRN_WIDTH), lambda i: (0, 0))],
        out_specs=[wide(0), wide(0), pl.BlockSpec((HGRN_HEADS, None, dh, dh), lambda i: (0, i, 0, 0))],
        out_shape=[jax.ShapeDtypeStruct((t, HGRN_WIDTH), F32), jax.ShapeDtypeStruct((t, HGRN_WIDTH), BF16),
                   jax.ShapeDtypeStruct((HGRN_HEADS, n_chunks, dh, dh), F32)],
        scratch_shapes=[pltpu.VMEM((HGRN_HEADS, dh, dh), F32)],
        compiler_params=_params(("arbitrary",)),
    )(proj, proj, proj, proj, logits, ng)


def _hgrn_bwd(proj, o_raw, s0s, doh, logits, ng):
    t = proj.shape[0]
    c, dh = HGRN_CHUNK, HGRN_DH
    n_chunks = t // c
    last = n_chunks - 1

    def head(h, q_ref, z_ref, v_ref, g_ref, o_ref, s0_ref, doh_ref, lg_ref, ng_ref,
             dq_ref, dz_ref, dv_ref, dg_ref, dng_ref, dlb_ref, dst):
        sl = slice(h * dh, (h + 1) * dh)
        p = _hgrn_pre(q_ref[:, sl], z_ref[:, sl], lg_ref[:, sl])
        v = v_ref[:, sl]
        v16 = v.astype(BF16)
        s0 = s0_ref[h]
        ds_end = dst[h]
        ds16 = ds_end.astype(BF16)
        ngv = ng_ref[:, sl]

        o = o_ref[:, sl]
        dohv = doh_ref[:, sl]
        sg, dsg = _silu_and_grad(g_ref[:, sl])
        rn = lax.rsqrt(jnp.mean(o * o, axis=-1, keepdims=True) + RMS_EPS)
        oh = o * rn
        dg_ref[:, sl] = (dohv * oh * ngv * dsg).astype(BF16)
        don = dohv * sg
        dng_ref[:, sl] += jnp.sum(don * oh, axis=0, keepdims=True)
        doh_n = don * ngv
        do = rn * (doh_n - oh * jnp.mean(doh_n * oh, axis=-1, keepdims=True))
        do16 = do.astype(BF16)

        qt16, kt16, q016, kc16 = (p[n].astype(BF16) for n in ("qt", "kt", "q0", "kc"))
        a = jnp.where(p["causal"], _dot_nt(qt16, kt16), 0.0)
        da = jnp.where(p["causal"], _dot_nt(do16, v16), 0.0)
        da16 = da.astype(BF16)
        dqt = _dot(da16, kt16)
        dq0 = _dot(do16, s0.astype(BF16))
        dkt = _dot_tn(da16, qt16)
        dkc = _dot(v16, ds16)
        dv_ref[:, sl] = (_dot_tn(a.astype(BF16), do16) + _dot_nt(kc16, ds16)).astype(BF16)
        lam_end = jnp.exp(p["b_end"])
        dst[h] = lam_end * ds_end + _dot_tn(do16, q016)

        qt, kt, q0, kc = (a.astype(F32) for a in (qt16, kt16, q016, kc16))
        db = dqt * qt + dq0 * q0 - dkt * kt - dkc * kc
        db_end = (jnp.sum(dkc * kc, axis=0, keepdims=True)
                  + jnp.sum(ds_end * s0, axis=0, keepdims=True) * lam_end)
        rowi = lax.broadcasted_iota(jnp.int32, (c, dh), 0)
        db = db + jnp.where(rowi == c - 1, db_end, 0.0)
        r = lax.broadcasted_iota(jnp.int32, (c, c), 0)
        s = lax.broadcasted_iota(jnp.int32, (c, c), 1)
        dlf = _tri_dot(jnp.where(s >= r, 1.0, 0.0).astype(BF16), db)

        dqs = dqt * p["e_q"] + dq0 * p["e_0"]
        dq_ref[:, sl] = (dqs * p["dqs"]).astype(BF16)
        dk = dkt * p["e_k"] + dkc * p["e_c"]
        sz, lb = p["sz"], p["lb"]
        common = dlf / p["f"] - dk
        dz_ref[:, sl] = ((1.0 - lb) * sz * (1.0 - sz) * common).astype(BF16)
        dlb_ref[:, sl] += jnp.sum((1.0 - sz) * common, axis=0, keepdims=True)

    def body(*refs):
        dng_ref, dlb_ref, dst = refs[-3:]

        @pl.when(pl.program_id(0) == 0)
        def _():
            dst[...] = jnp.zeros_like(dst)
            dng_ref[...] = jnp.zeros_like(dng_ref)
            dlb_ref[...] = jnp.zeros_like(dlb_ref)

        for h in range(HGRN_HEADS):
            head(h, *refs)

    def wide(off):
        return pl.BlockSpec((c, HGRN_WIDTH), lambda i: (last - i, off))

    vec = pl.BlockSpec((1, HGRN_WIDTH), lambda i: (0, 0))
    act = jax.ShapeDtypeStruct((t, HGRN_WIDTH), BF16)
    vsh = jax.ShapeDtypeStruct((1, HGRN_WIDTH), F32)
    return pl.pallas_call(
        body, name="hgrn_bwd", grid=(n_chunks,),
        in_specs=[wide(1), wide(2), wide(3), wide(4), wide(0),
                  pl.BlockSpec((HGRN_HEADS, None, dh, dh), lambda i: (0, last - i, 0, 0)),
                  wide(0), pl.BlockSpec((2, HGRN_WIDTH), lambda i: (0, 0)), vec],
        out_specs=[wide(0), wide(0), wide(0), wide(0), vec, vec],
        out_shape=[act, act, act, act, vsh, vsh],
        scratch_shapes=[pltpu.VMEM((HGRN_HEADS, dh, dh), F32)],
        compiler_params=_params(("arbitrary",)),
    )(proj, proj, proj, proj, o_raw, s0s, doh, logits, ng)


def _lb_bwd(logits, dlb):
    def body(lg_ref, d_ref, o_ref):
        lg = lg_ref[...]
        lb = _sigmoid(lg[0:1, :] - lg[1:2, :])
        g = d_ref[...] * lb * (1.0 - lb)
        o_ref[0:1, :] = g
        o_ref[1:2, :] = -g

    return pl.pallas_call(body, name="hgrn_lb_bwd", out_shape=jax.ShapeDtypeStruct(logits.shape, F32),
                          compiler_params=_params())(logits, dlb)


MERGE_TC = 1024
GS_BLOCK = (S5_WIDTH + 4 * HGRN_WIDTH) // MERGE_TC
GH_BLOCK = GS_BLOCK + D_MODEL // MERGE_TC


def _merge_fwd(proj, ys, yh):
    t = proj.shape[0]
    tr = _pick(t, 256, SUBLANE)

    def body(gs_ref, gh_ref, ys_ref, yh_ref, m_ref):
        m_ref[...] = (_sigmoid(gs_ref[...]) * ys_ref[...] + _sigmoid(gh_ref[...]) * yh_ref[...]).astype(BF16)

    blk = pl.BlockSpec((tr, MERGE_TC), lambda i, j: (i, j))
    return pl.pallas_call(
        body, name="merge_fwd", grid=(t // tr, D_MODEL // MERGE_TC),
        in_specs=[pl.BlockSpec((tr, MERGE_TC), lambda i, j: (i, GS_BLOCK + j)),
                  pl.BlockSpec((tr, MERGE_TC), lambda i, j: (i, GH_BLOCK + j)), blk, blk],
        out_specs=blk, out_shape=jax.ShapeDtypeStruct((t, D_MODEL), BF16),
        compiler_params=_params(("parallel", "parallel")),
    )(proj, proj, ys, yh)


def _merge_bwd(proj, ys, yh, dm, after=None):
    t = proj.shape[0]
    tr = _pick(t, 256, SUBLANE)

    def body(gs_ref, gh_ref, ys_ref, yh_ref, dm_ref, dys_ref, dyh_ref, dgs_ref, dgh_ref):
        dmv = dm_ref[...]
        ss, sh = _sigmoid(gs_ref[...]), _sigmoid(gh_ref[...])
        dys_ref[...] = (dmv * ss).astype(BF16)
        dyh_ref[...] = (dmv * sh).astype(BF16)
        dgs_ref[...] = (dmv * ys_ref[...] * ss * (1.0 - ss)).astype(BF16)
        dgh_ref[...] = (dmv * yh_ref[...] * sh * (1.0 - sh)).astype(BF16)

    blk = pl.BlockSpec((tr, MERGE_TC), lambda i, j: (i, j))
    sh16 = jax.ShapeDtypeStruct((t, D_MODEL), BF16)
    in_specs = [pl.BlockSpec((tr, MERGE_TC), lambda i, j: (i, GS_BLOCK + j)),
                pl.BlockSpec((tr, MERGE_TC), lambda i, j: (i, GH_BLOCK + j)), blk, blk, blk]
    body, in_specs, args = _ordered(body, in_specs, [proj, proj, ys, yh, dm], after)
    return pl.pallas_call(
        body, name="merge_bwd", grid=(t // tr, D_MODEL // MERGE_TC),
        in_specs=in_specs,
        out_specs=[blk, blk, blk, blk], out_shape=[sh16, sh16, sh16, sh16],
        compiler_params=_params(("parallel", "parallel")),
    )(*args)


FFN_TC = 128
FFN_ROWS = 512
HALO = SUBLANE


def _rows_with_halo(ref, r0, nrows, t, before, after):
    lo = r0 - before if r0 - before >= 0 else r0
    hi = r0 + nrows + after if r0 + nrows + after <= t else r0 + nrows
    parts = []
    if lo == r0 and before:
        parts.append(jnp.zeros((before, ref.shape[1]), F32))
    parts.append(ref[lo:hi, :])
    if hi == r0 + nrows and after:
        parts.append(jnp.zeros((after, ref.shape[1]), F32))
    return parts[0] if len(parts) == 1 else jnp.concatenate(parts, axis=0)


def _conv3(ext, w, b, nrows, off):
    n = ext.shape[0]
    x0 = ext[off:off + nrows, :]
    x1 = pltpu.roll(ext, 1, 0)[off:off + nrows, :]
    x2 = pltpu.roll(ext, 2, 0)[off:off + nrows, :]
    return b + w[0:1, :] * x2 + w[1:2, :] * x1 + w[2:3, :] * x0, (x0, x1, x2)


def _ffn_act_fwd(up, cw, cb):
    t = up.shape[0]
    rows = _pick(t, FFN_ROWS, SUBLANE)
    nvb = D_FF // FFN_TC

    def body(ug_ref, uv_ref, wg_ref, wv_ref, bg_ref, bv_ref, act_ref):
        wg, wv, bg, bv = wg_ref[...], wv_ref[...], bg_ref[...], bv_ref[...]
        for r0 in range(0, t, rows):
            cg, _ = _conv3(_rows_with_halo(ug_ref, r0, rows, t, HALO, 0), wg, bg, rows, HALO)
            cv, _ = _conv3(_rows_with_halo(uv_ref, r0, rows, t, HALO, 0), wv, bv, rows, HALO)
            act_ref[r0:r0 + rows, :] = (_silu_and_grad(cg)[0] * cv).astype(BF16)

    def colblk(nrow, off):
        return pl.BlockSpec((nrow, FFN_TC), lambda j: (0, off + j))

    return pl.pallas_call(
        body, name="ffn_act_fwd", grid=(nvb,),
        in_specs=[colblk(t, 0), colblk(t, nvb), colblk(3, 0), colblk(3, nvb), colblk(1, 0), colblk(1, nvb)],
        out_specs=colblk(t, 0), out_shape=jax.ShapeDtypeStruct((t, D_FF), BF16),
        compiler_params=_params(("parallel",)),
    )(up, up, cw, cw, cb, cb)


def _ffn_act_bwd(up, dact, cw, cb, after=None):
    t = up.shape[0]
    rows = _pick(t, FFN_ROWS, SUBLANE)
    nvb = D_FF // FFN_TC

    def body(ug_ref, uv_ref, da_ref, wg_ref, wv_ref, bg_ref, bv_ref,
             dug_ref, duv_ref, dwg_ref, dwv_ref, dbg_ref, dbv_ref):
        wg, wv, bg, bv = wg_ref[...], wv_ref[...], bg_ref[...], bv_ref[...]
        ext = rows + HALO
        acc_g = [jnp.zeros((1, FFN_TC), F32) for _ in range(4)]
        acc_v = [jnp.zeros((1, FFN_TC), F32) for _ in range(4)]
        for r0 in range(0, t, rows):
            cg, xg = _conv3(_rows_with_halo(ug_ref, r0, rows, t, HALO, HALO), wg, bg, ext, HALO)
            cv, xv = _conv3(_rows_with_halo(uv_ref, r0, rows, t, HALO, HALO), wv, bv, ext, HALO)
            dav = _rows_with_halo(da_ref, r0, rows, t, 0, HALO)
            sg, dsg = _silu_and_grad(cg)
            for dconv, xs, w, acc, out in ((dav * cv * dsg, xg, wg, acc_g, dug_ref), (dav * sg, xv, wv, acc_v, duv_ref)):
                d0 = dconv[0:rows, :]
                d1 = pltpu.roll(dconv, ext - 1, 0)[0:rows, :]
                d2 = pltpu.roll(dconv, ext - 2, 0)[0:rows, :]
                out[r0:r0 + rows, :] = (w[2:3, :] * d0 + w[1:2, :] * d1 + w[0:1, :] * d2).astype(BF16)
                x0, x1, x2 = xs
                acc[0] = acc[0] + jnp.sum(d0 * x2[0:rows, :], axis=0, keepdims=True)
                acc[1] = acc[1] + jnp.sum(d0 * x1[0:rows, :], axis=0, keepdims=True)
                acc[2] = acc[2] + jnp.sum(d0 * x0[0:rows, :], axis=0, keepdims=True)
                acc[3] = acc[3] + jnp.sum(d0, axis=0, keepdims=True)
        for acc, dw_ref, db_ref in ((acc_g, dwg_ref, dbg_ref), (acc_v, dwv_ref, dbv_ref)):
            dw_ref[0:1, :] = acc[0]
            dw_ref[1:2, :] = acc[1]
            dw_ref[2:3, :] = acc[2]
            db_ref[...] = acc[3]

    def colblk(nrow, off):
        return pl.BlockSpec((nrow, FFN_TC), lambda j: (0, off + j))

    in_specs = [colblk(t, 0), colblk(t, nvb), colblk(t, 0), colblk(3, 0), colblk(3, nvb), colblk(1, 0), colblk(1, nvb)]
    body, in_specs, args = _ordered(body, in_specs, [up, up, dact, cw, cw, cb, cb], after)
    return pl.pallas_call(
        body, name="ffn_act_bwd", grid=(nvb,),
        in_specs=in_specs,
        out_specs=[colblk(t, 0), colblk(t, 0), colblk(3, 0), colblk(3, 0), colblk(1, 0), colblk(1, 0)],
        out_shape=[jax.ShapeDtypeStruct((t, D_FF), BF16), jax.ShapeDtypeStruct((t, D_FF), BF16),
                   jax.ShapeDtypeStruct((3, D_FF), F32), jax.ShapeDtypeStruct((3, D_FF), F32),
                   jax.ShapeDtypeStruct((1, D_FF), F32), jax.ShapeDtypeStruct((1, D_FF), F32)],
        compiler_params=_params(("parallel",)),
    )(*args)


def _all_gather(shards, name):
    nw = len(shards)

    def body(*refs):
        x_refs, out_refs = refs[:nw], refs[nw:2 * nw]
        send_sems, recv_sems, local_sems = refs[2 * nw:]
        x, y, c = lax.axis_index("x"), lax.axis_index("y"), lax.axis_index("c")
        me, sibling = (x, y, c), (x, y, 1 - c)
        chips = [(1 - x, y), (x, 1 - y), (1 - x, 1 - y)]

        def copy(w, k, block, to, src=None):
            slot = out_refs[w].at[4 * block[0] + 2 * block[1] + block[2]]
            return pltpu.make_async_remote_copy(
                src_ref=slot if src is None else src, dst_ref=slot,
                send_sem=send_sems.at[w, k], recv_sem=recv_sems.at[w, k],
                device_id=to, device_id_type=MESH)

        mine, first, passed = [], [], []
        for w in range(nw):
            cp = pltpu.make_async_copy(x_refs[w], out_refs[w].at[4 * x + 2 * y + c], local_sems.at[w])
            cp.start()
            mine.append(cp)
            first.append(copy(w, 0, me, sibling, src=x_refs[w]))
            first += [copy(w, 1 + j, me, (*chip, c), src=x_refs[w]) for j, chip in enumerate(chips)]
        for cp in first:
            cp.start()
        for w in range(nw):
            for j, chip in enumerate(chips):
                copy(w, 1 + j, (*chip, c), me).wait_recv()
                fwd = copy(w, 4 + j, (*chip, c), sibling)
                fwd.start()
                passed.append(fwd)
        for w in range(nw):
            copy(w, 0, sibling, me).wait_recv()
            for j, chip in enumerate(chips):
                copy(w, 4 + j, (*chip, 1 - c), me).wait_recv()
        for cp in first + passed:
            cp.wait_send()
        for cp in mine:
            cp.wait()

    anyspec = pl.BlockSpec(memory_space=pl.ANY)
    return pl.pallas_call(
        body, name=name,
        in_specs=[anyspec] * nw, out_specs=[anyspec] * nw,
        out_shape=[jax.ShapeDtypeStruct((N_DEV,) + s.shape, s.dtype) for s in shards],
        scratch_shapes=[pltpu.SemaphoreType.DMA((nw, 7)), pltpu.SemaphoreType.DMA((nw, 7)),
                        pltpu.SemaphoreType.DMA((nw,))],
    )(*shards)


HBM_SPEC = pl.BlockSpec(memory_space=pltpu.HBM)
SEM_SPEC = pl.BlockSpec(memory_space=pltpu.SEMAPHORE)
ANY_SPEC = pl.BlockSpec(memory_space=pl.ANY)
DATAFLOW = pltpu.SideEffectType.DATAFLOW_SIDE_EFFECTING


def _my_index():
    return 4 * lax.axis_index("x") + 2 * lax.axis_index("y") + lax.axis_index("c")


def _peers():
    x, y, c = lax.axis_index("x"), lax.axis_index("y"), lax.axis_index("c")
    peers = []
    for k in range(1, N_DEV):
        px = 1 - x if k & 4 else x
        py = 1 - y if k & 2 else y
        pc = 1 - c if k & 1 else c
        peers.append((k, (px, py, pc), 4 * px + 2 * py + pc))
    return peers


def _split_copy(src_ref, land_ref, send_sems, recv_sems, w, k, peer, slot, scatter, outgoing):
    return pltpu.make_async_remote_copy(
        src_ref=src_ref.at[slot] if scatter else src_ref,
        dst_ref=land_ref.at[_my_index() if outgoing else slot],
        send_sem=send_sems.at[w * (N_DEV - 1) + k - 1], recv_sem=recv_sems.at[w * (N_DEV - 1) + k - 1],
        device_id=peer, device_id_type=MESH)


def _exchange_start(srcs, scatter, after, name):
    nw = len(srcs)
    me = _my_index()
    lands = []
    for s in srcs:
        own = lax.dynamic_index_in_dim(s, me, 0, keepdims=True) if scatter else s[None]
        shape = s.shape if scatter else (N_DEV,) + s.shape
        lands.append(lax.dynamic_update_slice_in_dim(lax.empty(shape, s.dtype), own, me, 0))

    afters = [] if after is None else [after]

    def body(*refs):
        s_refs, l_refs = refs[:nw], refs[nw:2 * nw]
        send_sems, recv_sems = refs[2 * nw + len(afters)], refs[2 * nw + len(afters) + 1]
        token = refs[-1]
        for w in range(nw):
            for k, peer, slot in _peers():
                _split_copy(s_refs[w], l_refs[w], send_sems, recv_sems, w, k, peer, slot, scatter, True).start()
        token[...] = jnp.zeros_like(token)

    sems = pltpu.SemaphoreType.DMA((nw * (N_DEV - 1),))
    outs = pl.pallas_call(
        body, name=name,
        out_shape=(sems, sems, *[pltpu.HBM(a.shape, a.dtype) for a in (*srcs, *lands)],
                   jax.ShapeDtypeStruct((SUBLANE, LANE), F32)),
        in_specs=[HBM_SPEC] * (2 * nw) + [ANY_SPEC] * len(afters),
        out_specs=(SEM_SPEC, SEM_SPEC, *[HBM_SPEC] * (2 * nw), pl.BlockSpec(memory_space=pltpu.VMEM)),
        input_output_aliases={i: 2 + i for i in range(2 * nw)},
        compiler_params=pltpu.CompilerParams(has_side_effects=DATAFLOW),
    )(*[pltpu.with_memory_space_constraint(a, pltpu.HBM) for a in (*srcs, *lands)], *afters)
    return dict(sems=outs[:2], srcs=outs[2:2 + nw], lands=outs[2 + nw:2 + 2 * nw], token=outs[-1], scatter=scatter)


def _exchange_wait(handle, afters, name):
    srcs, lands, scatter = handle["srcs"], handle["lands"], handle["scatter"]
    nw = len(srcs)

    def body(*refs):
        s_refs, l_refs = refs[:nw], refs[nw:2 * nw]
        send_sems, recv_sems = refs[2 * nw], refs[2 * nw + 1]
        for w in range(nw):
            for k, peer, slot in _peers():
                cp = _split_copy(s_refs[w], l_refs[w], send_sems, recv_sems, w, k, peer, slot, scatter, False)
                cp.wait_send()
                cp.wait_recv()

    outs = pl.pallas_call(
        body, name=name,
        out_shape=tuple(pltpu.HBM(a.shape, a.dtype) for a in (*srcs, *lands)),
        in_specs=[HBM_SPEC] * (2 * nw) + [SEM_SPEC, SEM_SPEC] + [ANY_SPEC] * len(afters),
        out_specs=tuple([HBM_SPEC] * (2 * nw)),
        input_output_aliases={i: i for i in range(2 * nw)},
        compiler_params=pltpu.CompilerParams(has_side_effects=DATAFLOW),
    )(*srcs, *lands, *handle["sems"], *afters)
    return list(outs[nw:])


def _adamw(w, g, m, v):
    m = ADAM_B1 * m + (1.0 - ADAM_B1) * g
    v = ADAM_B2 * v + (1.0 - ADAM_B2) * (g * g)
    m_hat = m / (1.0 - ADAM_B1 ** ADAM_STEP)
    v_hat = v / (1.0 - ADAM_B2 ** ADAM_STEP)
    delta = -ADAM_LR * (m_hat / (jnp.sqrt(v_hat) + ADAM_EPS) + ADAM_WD * w)
    return delta, m, v


def _sum_adam(parts, w, m, v, name):
    _, r, c = parts.shape
    tr = _pick(r, 128, 16)

    def body(p_ref, w_ref, m_ref, v_ref, g_ref, d_ref, mo_ref, vo_ref):
        g = p_ref[0].astype(F32)
        for s in range(1, N_DEV):
            g = g + p_ref[s].astype(F32)
        g_ref[...] = g
        d_ref[...], mo_ref[...], vo_ref[...] = _adamw(w_ref[...], g, m_ref[...], v_ref[...])

    row = pl.BlockSpec((tr, c), lambda i: (i, 0))
    sh = jax.ShapeDtypeStruct((r, c), F32)
    return pl.pallas_call(
        body, name=name, grid=(r // tr,),
        in_specs=[pl.BlockSpec((N_DEV, tr, c), lambda i: (0, i, 0)), row, row, row],
        out_specs=[row, row, row, row], out_shape=[sh, sh, sh, sh],
        compiler_params=_params(("parallel",)),
    )(parts, w, m, v)


def _sum_slots(parts, name):
    _, r, c = parts.shape
    tr = _pick(r, 512, SUBLANE)

    def body(p_ref, o_ref):
        g = p_ref[0]
        for s in range(1, N_DEV):
            g = g + p_ref[s]
        o_ref[...] = g

    return pl.pallas_call(
        body, name=name, grid=(r // tr,),
        in_specs=[pl.BlockSpec((N_DEV, tr, c), lambda i: (0, i, 0))],
        out_specs=pl.BlockSpec((tr, c), lambda i: (i, 0)), out_shape=jax.ShapeDtypeStruct((r, c), F32),
        compiler_params=_params(("parallel",)),
    )(parts)


def _adam_rows(g, w, m, v, name):
    r, c = g.shape
    tr = _pick(r, 512, SUBLANE)

    def body(g_ref, w_ref, m_ref, v_ref, d_ref, mo_ref, vo_ref):
        d_ref[...], mo_ref[...], vo_ref[...] = _adamw(w_ref[...], g_ref[...], m_ref[...], v_ref[...])

    row = pl.BlockSpec((tr, c), lambda i: (i, 0))
    sh = jax.ShapeDtypeStruct((r, c), F32)
    return pl.pallas_call(body, name=name, grid=(r // tr,), in_specs=[row] * 4, out_specs=[row] * 3,
                          out_shape=[sh, sh, sh], compiler_params=_params(("parallel",)))(g, w, m, v)


def _pack(arrays):
    flat = jnp.concatenate([a.reshape(-1).astype(F32) for a in arrays])
    pad = (-flat.shape[0]) % (SUBLANE * LANE)
    return jnp.pad(flat, (0, pad)).reshape(-1, LANE)


def _unpack(packed, shapes):
    flat = packed.reshape(-1)
    out, off = [], 0
    for s in shapes:
        n = math.prod(s)
        out.append(flat[off:off + n].reshape(s))
        off += n
    return out


def _block_diag(t):
    eye = jnp.eye(S5_SUPER, dtype=bool)
    bd = jnp.where(eye[None, :, None, :, None], t[:, :, :, None, :], 0.0)
    return bd.reshape(S5_SUPER, S5_SUPER * t.shape[2], S5_SUPER * t.shape[3])


def _diag_blocks(dense, a, b):
    x = dense.reshape(S5_SUPER, S5_SUPER, a, S5_SUPER, b)
    return jnp.moveaxis(jnp.diagonal(x, axis1=1, axis2=3), -1, 1)


def _s5_layouts(b_re, b_im, c_re, c_im, d):
    g2 = (S5_GROUPS // S5_SUPER, S5_SUPER)
    bt = lambda b: _block_diag(b.reshape(*g2, S5_STATE, S5_GROUP).transpose(0, 1, 3, 2))
    ct = lambda c: _block_diag(c.reshape(*g2, S5_GROUP, S5_STATE).transpose(0, 1, 3, 2))
    bsg = jnp.concatenate([bt(b_re), bt(b_im)], axis=2).astype(BF16)
    ccat = jnp.concatenate([ct(c_re), -ct(c_im)], axis=1).astype(BF16)
    return bsg, ccat, d.reshape(1, S5_WIDTH)


def _s5_param_grads(gb, gc):
    n = S5_LANES
    gb_re = _diag_blocks(gb[:, :, 0:n], S5_GROUP, S5_STATE).transpose(0, 1, 3, 2).reshape(S5_GROUPS, S5_STATE, S5_GROUP)
    gb_im = _diag_blocks(gb[:, :, n:2 * n], S5_GROUP, S5_STATE).transpose(0, 1, 3, 2).reshape(S5_GROUPS, S5_STATE, S5_GROUP)
    gc_re = _diag_blocks(gc[:, 0:n, :], S5_STATE, S5_GROUP).transpose(0, 1, 3, 2).reshape(S5_GROUPS, S5_GROUP, S5_STATE)
    gc_im = -_diag_blocks(gc[:, n:2 * n, :], S5_STATE, S5_GROUP).transpose(0, 1, 3, 2).reshape(S5_GROUPS, S5_GROUP, S5_STATE)
    return gb_re, gb_im, gc_re, gc_im


def _local_step(x, target, weight, emit, small, after=None):
    sp = small
    a_re, a_im = sp["s5_a_re"], sp["s5_a_im"]
    ldt = sp["s5_log_dt"].reshape(S5_GROUPS, 1)

    h1 = _rms_fwd(x, sp["ln_mix_g"], "rms_mix", after=after)
    w_in = weight("w_in", h1)
    proj = _mm_nn(h1, w_in, "mm_in")
    disc = _s5_param_fwd(a_re, a_im, ldt)
    bsg, ccat, d_row = _s5_layouts(sp["s5_b_re"], sp["s5_b_im"], sp["s5_c_re"], sp["s5_c_im"], sp["s5_d"])
    abar_t, coef_t = _s5_to_tile(disc[0], disc[1]), _s5_to_tile(disc[2], disc[3])
    y, sb = _s5_fwd(proj, bsg, ccat, d_row, abar_t, coef_t)
    z16 = _gelu_fwd(y)
    w_glu = weight("s5_w_glu", z16)
    gl = _mm_nn(z16, w_glu, "mm_glu")
    z2 = _glu_fwd(y, gl, sp["s5_b_glu"])
    w_ps = weight("w_proj_s5", z2)
    ys = _mm_nn(z2, w_ps, "mm_proj_s5")
    o_raw, oh, s0s = _hgrn_fwd(proj, sp["hgrn_lb_logits"], sp["hgrn_norm_g"])
    w_ph = weight("w_proj_hgrn", oh)
    yh = _mm_nn(oh, w_ph, "mm_proj_hgrn")
    merged = _merge_fwd(proj, ys, yh)
    w_out = weight("w_out", merged)
    x1 = _mm_nn(merged, w_out, "mm_out", res=x)
    h2 = _rms_fwd(x1, sp["ln_ffn_g"], "rms_ffn")
    w_up = weight("w_up", h2)
    up = _mm_nn(h2, w_up, "mm_up")
    act = _ffn_act_fwd(up, sp["conv_w"], sp["conv_b"])
    w_down = weight("w_down", act)
    x2 = _mm_nn(act, w_down, "mm_down", res=x1)
    dx2, dx2_16, g_ln_final, loss = _loss_head(x2, sp["ln_final_g"], target)

    dact = _mm_nt(dx2_16, w_down, "mm_down_dx")
    tok = emit("w_down", _mm_tn(act, dx2_16, 1, "mm_down_dw"))
    dup_g, dup_v, dcw_g, dcw_v, dcb_g, dcb_v = _ffn_act_bwd(up, dact, sp["conv_w"], sp["conv_b"], after=tok)
    dup = jnp.concatenate([dup_g, dup_v], axis=1)
    g_conv_w = jnp.concatenate([dcw_g, dcw_v], axis=1)
    g_conv_b = jnp.concatenate([dcb_g, dcb_v], axis=1)
    dh2 = _mm_nt(dup, w_up, "mm_up_dx")
    tok = emit("w_up", _mm_tn(h2, dup, N_DEV, "mm_up_dw"))
    dx1, dx1_16, g_ln_ffn = _rms_bwd(x1, sp["ln_ffn_g"], dh2, dx2, "rms_ffn_bwd", True, after=tok)

    dmerged = _mm_nt(dx1_16, w_out, "mm_out_dx")
    tok = emit("w_out", _mm_tn(merged, dx1_16, 1, "mm_out_dw"))
    dys, dyh, dgs, dgh = _merge_bwd(proj, ys, yh, dmerged, after=tok)
    doh = _mm_nt(dyh, w_ph, "mm_proj_hgrn_dx")
    tok = emit("w_proj_hgrn", _mm_tn(oh, dyh, N_DEV, "mm_proj_hgrn_dw"))
    dz2 = _mm_nt(dys, w_ps, "mm_proj_s5_dx", after=tok)
    tok = emit("w_proj_s5", _mm_tn(z2, dys, N_DEV, "mm_proj_s5_dw"))
    dgl, dza, g_b_glu = _glu_bwd(y, gl, sp["s5_b_glu"], dz2, after=tok)
    dzb = _mm_nt(dgl, w_glu, "mm_glu_dx")
    tok = emit("s5_w_glu", _mm_tn(z16, dgl, 1, "mm_glu_dw"))
    dy = _gelu_bwd(y, dza, dzb, after=tok)
    du, gb, gc, gd, g_abar_t, g_coef_t = _s5_bwd(proj, dy, sb, bsg, ccat, d_row, abar_t, coef_t)
    g_a_re, g_a_im, g_ldt = _s5_param_bwd(a_re, a_im, ldt, [*_s5_from_tile(g_abar_t), *_s5_from_tile(g_coef_t)])
    g_b_re, g_b_im, g_c_re, g_c_im = _s5_param_grads(gb, gc)
    dq, dz, dv, dg, g_norm, dlb = _hgrn_bwd(proj, o_raw, s0s, doh, sp["hgrn_lb_logits"], sp["hgrn_norm_g"])
    g_logits = _lb_bwd(sp["hgrn_lb_logits"], dlb)

    dproj = jnp.concatenate([du, dq, dz, dv, dg, dgs, dgh], axis=1)
    dh1 = _mm_nt(dproj, w_in, "mm_in_dx")
    tok = emit("w_in", _mm_tn(h1, dproj, N_DEV, "mm_in_dw"))
    grad_x, g_ln_mix = _rms_bwd(x, sp["ln_mix_g"], dh1, dx1, "rms_mix_bwd", False, after=tok)

    small_g = dict(ln_mix_g=g_ln_mix, s5_a_re=g_a_re, s5_a_im=g_a_im, s5_log_dt=g_ldt.reshape(1, S5_GROUPS),
                   s5_b_re=g_b_re, s5_b_im=g_b_im, s5_c_re=g_c_re, s5_c_im=g_c_im,
                   s5_d=gd.reshape(S5_GROUPS, S5_GROUP), s5_b_glu=g_b_glu, hgrn_lb_logits=g_logits,
                   hgrn_norm_g=g_norm, ln_ffn_g=g_ln_ffn, conv_w=g_conv_w, conv_b=g_conv_b, ln_final_g=g_ln_final)
    return loss, grad_x, small_g


BIG = ("w_in", "s5_w_glu", "w_proj_s5", "w_proj_hgrn", "w_out", "w_up", "w_down")
COL_SHARDED = ("w_in", "w_proj_s5", "w_proj_hgrn", "w_up")
SMALL = ("ln_mix_g", "s5_a_re", "s5_a_im", "s5_log_dt", "s5_b_re", "s5_b_im", "s5_c_re", "s5_c_im", "s5_d",
         "s5_b_glu", "hgrn_lb_logits", "hgrn_norm_g", "ln_ffn_g", "conv_b", "ln_final_g")
WEIGHTS = ("ln_mix_g", "w_in", "s5_a_re", "s5_a_im", "s5_log_dt", "s5_b_re", "s5_b_im", "s5_c_re", "s5_c_im", "s5_d",
           "s5_w_glu", "s5_b_glu", "w_proj_s5", "hgrn_lb_logits", "hgrn_norm_g", "w_proj_hgrn", "w_out", "ln_ffn_g",
           "w_up", "conv_w", "conv_b", "w_down", "ln_final_g")


def kernel(x, ln_mix_g, w_in, s5_a_re, s5_a_im, s5_log_dt, s5_b_re, s5_b_im, s5_c_re, s5_c_im, s5_d, s5_w_glu, s5_b_glu, w_proj_s5, hgrn_lb_logits, hgrn_norm_g, w_proj_hgrn, w_out, ln_ffn_g, w_up, conv_w, conv_b, w_down, ln_final_g, loss_target, m_ln_mix_g, m_w_in, m_s5_a_re, m_s5_a_im, m_s5_log_dt, m_s5_b_re, m_s5_b_im, m_s5_c_re, m_s5_c_im, m_s5_d, m_s5_w_glu, m_s5_b_glu, m_w_proj_s5, m_hgrn_lb_logits, m_hgrn_norm_g, m_w_proj_hgrn, m_w_out, m_ln_ffn_g, m_w_up, m_conv_w, m_conv_b, m_w_down, m_ln_final_g, v_ln_mix_g, v_w_in, v_s5_a_re, v_s5_a_im, v_s5_log_dt, v_s5_b_re, v_s5_b_im, v_s5_c_re, v_s5_c_im, v_s5_d, v_s5_w_glu, v_s5_b_glu, v_w_proj_s5, v_hgrn_lb_logits, v_hgrn_norm_g, v_w_proj_hgrn, v_w_out, v_ln_ffn_g, v_w_up, v_conv_w, v_conv_b, v_w_down, v_ln_final_g):
    given = dict(locals())
    w = {n: given[n] for n in WEIGHTS}
    mom = {n: given["m_" + n] for n in WEIGHTS}
    var = {n: given["v_" + n] for n in WEIGHTS}

    shard16 = {n: w[n][0].astype(BF16) for n in BIG}
    w_in_all, conv_w_all = _all_gather([shard16["w_in"], conv_w[0]], "gather_first")
    gather_groups = (("s5_w_glu", "w_proj_s5", "w_proj_hgrn", "w_out"), ("w_up",), ("w_down",))
    pending, token = {}, w_in_all
    for i, group in enumerate(gather_groups):
        handle = _exchange_start([shard16[n] for n in group], False, token, f"gather_start_{i}")
        token = handle["token"]
        for n in group:
            pending[n] = (group, handle, f"gather_wait_{i}")
    ready = {"w_in": w_in_all}

    def weight(name, after):
        if name not in ready:
            group, handle, wait_name = pending[name]
            for n, g in zip(group, _exchange_wait(handle, [after], wait_name)):
                ready[n] = g
        g = ready[name]
        return g if name in COL_SHARDED else g.reshape(1, N_DEV * g.shape[1], g.shape[2])

    scatter_groups = (("w_down",), ("w_up",), ("w_out", "w_proj_hgrn", "w_proj_s5", "s5_w_glu"), ("w_in",))
    emitted, scatters = {}, []

    def emit(name, grad):
        emitted[name] = grad if name in COL_SHARDED else grad.reshape(N_DEV, -1, grad.shape[2])
        group = scatter_groups[len(scatters)]
        if not all(n in emitted for n in group):
            return None
        handle = _exchange_start([emitted[n] for n in group], True, None, f"scatter_start_{len(scatters)}")
        scatters.append((group, handle))
        return handle["token"]

    small = dict(ln_mix_g=ln_mix_g, s5_a_re=s5_a_re[0], s5_a_im=s5_a_im[0], s5_log_dt=s5_log_dt,
                 s5_b_re=s5_b_re[0], s5_b_im=s5_b_im[0], s5_c_re=s5_c_re[0], s5_c_im=s5_c_im[0], s5_d=s5_d[0],
                 s5_b_glu=s5_b_glu, hgrn_lb_logits=hgrn_lb_logits, hgrn_norm_g=hgrn_norm_g, ln_ffn_g=ln_ffn_g,
                 conv_w=conv_w_all.transpose(1, 0, 2).reshape(3, 2 * D_FF), conv_b=conv_b,
                 ln_final_g=ln_final_g.reshape(1, D_MODEL))
    loss, grad_x, small_g = _local_step(x[0], loss_target[0], weight, emit, small, after=token)

    names = SMALL + ("conv_w",)
    shapes = [small_g[n].shape for n in names] + [(1,)]
    packed = _pack([small_g[n] for n in names] + [loss[0, 0:1]])
    total = _sum_slots(_all_gather([packed], "gather_small")[0], "sum_small")
    summed = dict(zip(names + ("loss",), _unpack(total, shapes)))

    grads, delta, new_m, new_v = {}, {}, {}, {}
    afters = [grad_x, total]
    for i, (group, handle) in enumerate(scatters):
        for n, r in zip(group, _exchange_wait(handle, afters, f"scatter_wait_{i}")):
            g, d, m2, v2 = _sum_adam(r, w[n][0], mom[n][0], var[n][0], "adam_" + n)
            grads[n], delta[n], new_m[n], new_v[n] = g[None], d[None], m2[None], v2[None]
        if i == len(scatters) - 2:
            afters = [delta[n] for g2, _ in scatters[:-1] for n in g2]

    pw = _pack([w[n] for n in SMALL])
    d_s, m_s, v_s = _adam_rows(_pack([summed[n] for n in SMALL]), pw, _pack([mom[n] for n in SMALL]),
                               _pack([var[n] for n in SMALL]), "adam_small")
    wshapes = [w[n].shape for n in SMALL]
    for n, g, d, m2, v2 in zip(SMALL, [summed[n] for n in SMALL], _unpack(d_s, wshapes), _unpack(m_s, wshapes),
                               _unpack(v_s, wshapes)):
        grads[n], delta[n], new_m[n], new_v[n] = g.reshape(w[n].shape), d, m2, v2
    me = 4 * lax.axis_index("x") + 2 * lax.axis_index("y") + lax.axis_index("c")
    ncol = conv_w.shape[2]
    g_cw = lax.dynamic_slice_in_dim(summed["conv_w"], me * ncol, ncol, axis=1)
    d_cw, m_cw, v_cw = _adam_rows(g_cw, conv_w[0], m_conv_w[0], v_conv_w[0], "adam_conv_w")
    grads["conv_w"], delta["conv_w"], new_m["conv_w"], new_v["conv_w"] = g_cw[None], d_cw[None], m_cw[None], v_cw[None]

    return (summed["loss"].reshape(()), grad_x[None], *[grads[n] for n in WEIGHTS], *[delta[n] for n in WEIGHTS],
            *[new_m[n] for n in WEIGHTS], *[new_v[n] for n in WEIGHTS])
```

```python
import math

import jax
import jax.numpy as jnp
from jax import lax
from jax.experimental import pallas as pl
from jax.experimental.pallas import tpu as pltpu

F32 = jnp.float32
BF16 = jnp.bfloat16

N_DEV = 8
D_MODEL = 2048
S5_WIDTH = 1024
S5_GROUP = 16
S5_GROUPS = 64
S5_STATE = 64
S5_MAX_RE = -1e-4
S5_SUPER = 8
S5_LANES = S5_SUPER * S5_STATE
HGRN_WIDTH = 1024
HGRN_HEADS = 8
HGRN_DH = 128
HGRN_CHUNK = 64
D_FF = 5632
RMS_EPS = 1e-6
ADAM_LR = 0.001
ADAM_B1 = 0.9
ADAM_B2 = 0.999
ADAM_EPS = 1e-08
ADAM_WD = 0.01
ADAM_STEP = 10

LANE = 128
SUBLANE = 8
VMEM_LIMIT = 48 * 1024 * 1024
MESH = pl.DeviceIdType.MESH
GELU_C = math.sqrt(2.0 / math.pi)
GELU_A = 0.044715


def _params(sem=None):
    return pltpu.CompilerParams(dimension_semantics=sem, vmem_limit_bytes=VMEM_LIMIT)


def _pick(n, cap, unit=LANE):
    best = None
    for t in range(unit, min(n, cap) + 1, unit):
        if n % t == 0:
            best = t
    return best if best is not None else n


def _ordered(body, in_specs, args, after):
    if after is None:
        return body, list(in_specs), list(args)
    n_in = len(args)

    def ordered_body(*refs):
        return body(*refs[:n_in], *refs[n_in + 1:])

    return ordered_body, [*in_specs, pl.BlockSpec(memory_space=pl.ANY)], [*args, after]


def _sigmoid(x):
    return 0.5 * jnp.tanh(0.5 * x) + 0.5


def _silu_and_grad(x):
    s = _sigmoid(x)
    return x * s, s * (1.0 + x * (1.0 - s))


def _gelu_and_grad(y):
    inner = GELU_C * (y + GELU_A * y * y * y)
    th = jnp.tanh(inner)
    val = 0.5 * y * (1.0 + th)
    grad = 0.5 * (1.0 + th) + 0.5 * y * (1.0 - th * th) * GELU_C * (1.0 + 3.0 * GELU_A * y * y)
    return val, grad


def _dot(a, b):
    return jnp.dot(a, b, preferred_element_type=F32)


def _dot_nt(a, b):
    return lax.dot_general(a, b, (((1,), (1,)), ((), ())), preferred_element_type=F32)


def _dot_tn(a, b):
    return lax.dot_general(a, b, (((0,), (0,)), ((), ())), preferred_element_type=F32)


def _blocks_per_step(nb, ns, tn, cap=2048):
    if tn != ns:
        return 1
    best = 1
    for b in range(1, nb + 1):
        if nb % b == 0 and b * ns <= cap:
            best = b
    return best


def _mm_nn(a, w, name, res=None, out_dtype=F32):
    m, kdim = a.shape
    nb, _, ns = w.shape
    tm, tk, tn = _pick(m, 512), _pick(kdim, 2048), _pick(ns, 1536)
    npb, nk = ns // tn, kdim // tk
    bps = _blocks_per_step(nb, ns, tn)
    assert bps == 1 or nk == 1

    def body(*refs):
        a_ref, w_ref = refs[0], refs[1]
        r_ref = refs[2] if res is not None else None
        o_ref = refs[3] if res is not None else refs[2]

        def finish(r, cols):
            if res is not None:
                r = r + r_ref[:, cols]
            o_ref[:, cols] = r.astype(out_dtype)

        if nk == 1:
            for b in range(bps):
                finish(_dot(a_ref[...], w_ref[b]), slice(b * tn, (b + 1) * tn))
            return
        acc = refs[-1]
        k = pl.program_id(2)

        @pl.when(k == 0)
        def _():
            acc[...] = jnp.zeros_like(acc)

        acc[...] += _dot(a_ref[...], w_ref[0])

        @pl.when(k == nk - 1)
        def _():
            finish(acc[...], slice(0, tn))

    in_specs = [pl.BlockSpec((tm, tk), lambda j, i, k: (i, k)),
                pl.BlockSpec((bps, tk, tn), lambda j, i, k: (j // npb, k, j % npb))]
    args = [a, w]
    if res is not None:
        in_specs.append(pl.BlockSpec((tm, bps * tn), lambda j, i, k: (i, j)))
        args.append(res)
    return pl.pallas_call(
        body, name=name, grid=(nb * npb // bps, m // tm, nk),
        in_specs=in_specs, out_specs=pl.BlockSpec((tm, bps * tn), lambda j, i, k: (i, j)),
        out_shape=jax.ShapeDtypeStruct((m, nb * ns), out_dtype),
        scratch_shapes=[pltpu.VMEM((tm, tn), F32)] if nk > 1 else [],
        compiler_params=_params(("parallel", "parallel", "arbitrary")),
    )(*args)


def _mm_nt(a, w, name, out_dtype=F32, after=None):
    m, _ = a.shape
    nb, kdim, ns = w.shape
    tm, tko, tn = _pick(m, 1024), _pick(kdim, 1024), _pick(ns, 2048)
    npb = ns // tn
    bps = _blocks_per_step(nb, ns, tn)
    nred = nb * npb // bps

    def body(a_ref, w_ref, o_ref, *scratch):
        total = _dot_nt(a_ref[:, 0:tn], w_ref[0])
        for b in range(1, bps):
            total = total + _dot_nt(a_ref[:, b * tn:(b + 1) * tn], w_ref[b])
        if nred == 1:
            o_ref[...] = total.astype(out_dtype)
            return
        acc = scratch[0]
        n = pl.program_id(2)

        @pl.when(n == 0)
        def _():
            acc[...] = jnp.zeros_like(acc)

        acc[...] += total

        @pl.when(n == nred - 1)
        def _():
            o_ref[...] = acc[...].astype(out_dtype)

    in_specs = [pl.BlockSpec((tm, bps * tn), lambda i, j, n: (i, n)),
                pl.BlockSpec((bps, tko, tn), lambda i, j, n: (n // npb, j, n % npb))]
    body, in_specs, args = _ordered(body, in_specs, [a, w], after)
    return pl.pallas_call(
        body, name=name, grid=(m // tm, kdim // tko, nred),
        in_specs=in_specs,
        out_specs=pl.BlockSpec((tm, tko), lambda i, j, n: (i, j)),
        out_shape=jax.ShapeDtypeStruct((m, kdim), out_dtype),
        scratch_shapes=[pltpu.VMEM((tm, tko), F32)] if nred > 1 else [],
        compiler_params=_params(("parallel", "parallel", "arbitrary")),
    )(*args)


def _mm_tn(a, d, nb, name, out_dtype=BF16):
    m, kdim = a.shape
    ns = d.shape[1] // nb
    tm, tko, tn = _pick(m, 4096), _pick(kdim, 512), _pick(ns, 1536)
    npb, nm = ns // tn, m // tm

    def body(a_ref, d_ref, o_ref, *scratch):
        if nm == 1:
            o_ref[...] = _dot_tn(a_ref[...], d_ref[...]).astype(out_dtype)
            return
        acc = scratch[0]
        r = pl.program_id(2)

        @pl.when(r == 0)
        def _():
            acc[...] = jnp.zeros_like(acc)

        acc[...] += _dot_tn(a_ref[...], d_ref[...])

        @pl.when(r == nm - 1)
        def _():
            o_ref[...] = acc[...].astype(out_dtype)

    return pl.pallas_call(
        body, name=name, grid=(nb * npb, kdim // tko, nm),
        in_specs=[pl.BlockSpec((tm, tko), lambda j, i, r: (r, i)),
                  pl.BlockSpec((tm, tn), lambda j, i, r: (r, j))],
        out_specs=pl.BlockSpec((None, tko, tn), lambda j, i, r: (j // npb, i, j % npb)),
        out_shape=jax.ShapeDtypeStruct((nb, kdim, ns), out_dtype),
        scratch_shapes=[pltpu.VMEM((tko, tn), F32)] if nm > 1 else [],
        compiler_params=_params(("parallel", "parallel", "arbitrary")),
    )(a, d)


def _rms_fwd(x, g, name, after=None):
    t, d = x.shape
    tr = _pick(t, 256, SUBLANE)

    def body(x_ref, g_ref, h_ref):
        xv = x_ref[...]
        r = lax.rsqrt(jnp.mean(xv * xv, axis=-1, keepdims=True) + RMS_EPS)
        h_ref[...] = (xv * r * g_ref[...]).astype(BF16)

    in_specs = [pl.BlockSpec((tr, d), lambda i: (i, 0)), pl.BlockSpec((1, d), lambda i: (0, 0))]
    body, in_specs, args = _ordered(body, in_specs, [x, g], after)
    return pl.pallas_call(
        body, name=name, grid=(t // tr,),
        in_specs=in_specs,
        out_specs=pl.BlockSpec((tr, d), lambda i: (i, 0)),
        out_shape=jax.ShapeDtypeStruct((t, d), BF16),
        compiler_params=_params(("parallel",)),
    )(*args)


def _rms_bwd(x, g, dh, add, name, want_bf16, after=None):
    t, d = x.shape
    tr = _pick(t, 256, SUBLANE)

    def body(x_ref, g_ref, dh_ref, add_ref, *outs):
        if want_bf16:
            dx_ref, dxb_ref, dg_ref = outs
        else:
            dx_ref, dg_ref = outs
        i = pl.program_id(0)

        @pl.when(i == 0)
        def _():
            dg_ref[...] = jnp.zeros_like(dg_ref)

        xv, dhv = x_ref[...], dh_ref[...]
        r = lax.rsqrt(jnp.mean(xv * xv, axis=-1, keepdims=True) + RMS_EPS)
        xh = xv * r
        dg_ref[...] += jnp.sum(dhv * xh, axis=0, keepdims=True)
        dxh = dhv * g_ref[...]
        dx = add_ref[...] + r * (dxh - xh * jnp.mean(dxh * xh, axis=-1, keepdims=True))
        dx_ref[...] = dx
        if want_bf16:
            dxb_ref[...] = dx.astype(BF16)

    row = pl.BlockSpec((tr, d), lambda i: (i, 0))
    vec = pl.BlockSpec((1, d), lambda i: (0, 0))
    out_specs = [row] + ([row] if want_bf16 else []) + [vec]
    out_shape = ([jax.ShapeDtypeStruct((t, d), F32)] + ([jax.ShapeDtypeStruct((t, d), BF16)] if want_bf16 else [])
                 + [jax.ShapeDtypeStruct((1, d), F32)])
    body, in_specs, args = _ordered(body, [row, vec, row, row], [x, g, dh, add], after)
    return pl.pallas_call(
        body, name=name, grid=(t // tr,),
        in_specs=in_specs, out_specs=out_specs, out_shape=out_shape,
        compiler_params=_params(("arbitrary",)),
    )(*args)


def _loss_head(x2, g, target, name="loss_head"):
    t, d = x2.shape
    tr = _pick(t, 256, SUBLANE)

    def body(x_ref, g_ref, t_ref, dx_ref, dxb_ref, dg_ref, loss_ref):
        i = pl.program_id(0)

        @pl.when(i == 0)
        def _():
            dg_ref[...] = jnp.zeros_like(dg_ref)
            loss_ref[...] = jnp.zeros_like(loss_ref)

        xv = x_ref[...]
        gv = g_ref[...]
        r = lax.rsqrt(jnp.mean(xv * xv, axis=-1, keepdims=True) + RMS_EPS)
        xh = xv * r
        err = xh * gv - t_ref[...]
        part = 0.5 * jnp.sum(jnp.mean(err * err, axis=-1, keepdims=True), axis=0, keepdims=True)
        loss_ref[...] += jnp.broadcast_to(part, loss_ref.shape)
        dy = err * (1.0 / d)
        dg_ref[...] += jnp.sum(dy * xh, axis=0, keepdims=True)
        dxh = dy * gv
        dx = r * (dxh - xh * jnp.mean(dxh * xh, axis=-1, keepdims=True))
        dx_ref[...] = dx
        dxb_ref[...] = dx.astype(BF16)

    row = pl.BlockSpec((tr, d), lambda i: (i, 0))
    vec = pl.BlockSpec((1, d), lambda i: (0, 0))
    return pl.pallas_call(
        body, name=name, grid=(t // tr,),
        in_specs=[row, vec, row],
        out_specs=[row, row, vec, pl.BlockSpec((1, LANE), lambda i: (0, 0))],
        out_shape=[jax.ShapeDtypeStruct((t, d), F32), jax.ShapeDtypeStruct((t, d), BF16),
                   jax.ShapeDtypeStruct((1, d), F32), jax.ShapeDtypeStruct((1, LANE), F32)],
        compiler_params=_params(("arbitrary",)),
    )(x2, g, target)


def _s5_discretize(a_re, a_im, ldt):
    lam_re = jnp.minimum(a_re, S5_MAX_RE)
    lam_im = a_im
    dt = jnp.exp(ldt)
    mag = jnp.exp(lam_re * dt)
    abar_re = mag * jnp.cos(lam_im * dt)
    abar_im = mag * jnp.sin(lam_im * dt)
    den = lam_re * lam_re + lam_im * lam_im
    nr = abar_re - 1.0
    ni = abar_im
    coef_re = (nr * lam_re + ni * lam_im) / den
    coef_im = (ni * lam_re - nr * lam_im) / den
    return abar_re, abar_im, coef_re, coef_im


def _s5_param_fwd(a_re, a_im, ldt):
    def body(ar_ref, ai_ref, l_ref, o0, o1, o2, o3):
        outs = _s5_discretize(ar_ref[...], ai_ref[...], l_ref[...])
        for o, v in zip((o0, o1, o2, o3), outs):
            o[...] = v

    sh = jax.ShapeDtypeStruct(a_re.shape, F32)
    return pl.pallas_call(body, name="s5_param_fwd", out_shape=[sh, sh, sh, sh], compiler_params=_params())(a_re, a_im, ldt)


def _s5_param_bwd(a_re, a_im, ldt, cts):
    def body(ar_ref, ai_ref, l_ref, c0, c1, c2, c3, g0, g1, g2):
        _, vjp = jax.vjp(_s5_discretize, ar_ref[...], ai_ref[...], l_ref[...])
        ga, gb, gl = vjp((c0[...], c1[...], c2[...], c3[...]))
        g0[...] = ga
        g1[...] = gb
        g2[...] = gl

    sh = jax.ShapeDtypeStruct(a_re.shape, F32)
    return pl.pallas_call(body, name="s5_param_bwd", out_shape=[sh, sh, jax.ShapeDtypeStruct(ldt.shape, F32)],
                          compiler_params=_params())(a_re, a_im, ldt, *cts)


def _cmul(ar, ai, br, bi):
    return ar * br - ai * bi, ar * bi + ai * br


S5_TC = 128
S5_TILE = S5_SUPER * SUBLANE
S5_HALF = S5_TILE // 2


def _s5_to_tile(re, im):
    f = lambda a: a.reshape(S5_SUPER, S5_LANES // LANE, LANE).transpose(1, 0, 2).reshape(S5_HALF, LANE)
    return jnp.concatenate([f(re), f(im)], axis=0)


def _s5_from_tile(tile):
    f = lambda a: a.reshape(S5_LANES // LANE, S5_SUPER, LANE).transpose(1, 0, 2).reshape(S5_GROUPS, S5_STATE)
    return f(tile[0:S5_HALF]), f(tile[S5_HALF:])


RE = slice(0, S5_HALF)
IM = slice(S5_HALF, S5_TILE)


def _s5_scatter_rows(buf, rows, first_tile=0):
    tc = rows[0].shape[0]
    for j in range(SUBLANE):
        stacked = jnp.stack([r[:, j * LANE:(j + 1) * LANE] for r in rows], axis=0)
        buf[first_tile:first_tile + tc, j * SUBLANE:(j + 1) * SUBLANE, :] = jnp.swapaxes(stacked, 0, 1)


def _s5_gather_rows(buf, tc, first_tile=0):
    per_j = [jnp.swapaxes(buf[first_tile:first_tile + tc, j * SUBLANE:(j + 1) * SUBLANE, :], 0, 1)
             for j in range(SUBLANE)]
    return [jnp.concatenate([per_j[j][k] for j in range(SUBLANE)], axis=1) for k in range(S5_SUPER)]


def _s5_fwd(proj, bsg, ccat, d_row, abar_t, coef_t):
    t = proj.shape[0]
    tc = min(t, S5_TC)
    n_chunks = t // tc

    def body(u_ref, b_ref, c_ref, d_ref, a_ref, cf_ref, y_ref, sb_ref, x, car):
        @pl.when(pl.program_id(0) == 0)
        def _():
            car[...] = jnp.zeros_like(car)

        sb_ref[...] = car[...]
        u = u_ref[...]
        _s5_scatter_rows(x, [_dot(u[:, k * LANE:(k + 1) * LANE].astype(BF16), b_ref[k]) for k in range(S5_SUPER)])
        ar, ai = a_ref[RE, :], a_ref[IM, :]
        cr, ci = cf_ref[RE, :], cf_ref[IM, :]

        def step(i, carry):
            sr, si = carry
            xr, xi = _cmul(cr, ci, x[i, RE, :], x[i, IM, :])
            sr, si = ar * sr - ai * si + xr, ar * si + ai * sr + xi
            x[i, RE, :] = sr
            x[i, IM, :] = si
            return sr, si

        sr, si = lax.fori_loop(0, tc, step, (car[RE, :], car[IM, :]), unroll=4)
        car[RE, :] = sr
        car[IM, :] = si
        for k, s_k in enumerate(_s5_gather_rows(x, tc)):
            cols = slice(k * LANE, (k + 1) * LANE)
            y_ref[:, cols] = _dot(s_k.astype(BF16), c_ref[k]) + d_ref[:, cols] * u[:, cols]

    full = lambda shape: pl.BlockSpec(shape, lambda c: (0,) * len(shape))
    return pl.pallas_call(
        body, name="s5_fwd", grid=(n_chunks,),
        in_specs=[pl.BlockSpec((tc, S5_WIDTH), lambda c: (c, 0)), full(bsg.shape), full(ccat.shape), full(d_row.shape),
                  full(abar_t.shape), full(coef_t.shape)],
        out_specs=[pl.BlockSpec((tc, S5_WIDTH), lambda c: (c, 0)), pl.BlockSpec((None, S5_TILE, LANE), lambda c: (c, 0, 0))],
        out_shape=[jax.ShapeDtypeStruct((t, S5_WIDTH), F32), jax.ShapeDtypeStruct((n_chunks, S5_TILE, LANE), F32)],
        scratch_shapes=[pltpu.VMEM((tc, S5_TILE, LANE), F32), pltpu.VMEM((S5_TILE, LANE), F32)],
        compiler_params=_params(("arbitrary",)),
    )(proj, bsg, ccat, d_row, abar_t, coef_t)


def _s5_bwd(proj, dy, sb, bsg, ccat, d_row, abar_t, coef_t):
    t = proj.shape[0]
    tc = min(t, S5_TC)
    n_chunks = t // tc
    last = n_chunks - 1

    def body(u_ref, dy_ref, sb_ref, b_ref, c_ref, d_ref, a_ref, cf_ref,
             du_ref, gb_ref, gc_ref, gd_ref, ga_ref, gcf_ref, xb, xs, xg, gcar, acc):
        @pl.when(pl.program_id(0) == 0)
        def _():
            gcar[...] = jnp.zeros_like(gcar)
            acc[...] = jnp.zeros_like(acc)
            gb_ref[...] = jnp.zeros_like(gb_ref)
            gc_ref[...] = jnp.zeros_like(gc_ref)
            gd_ref[...] = jnp.zeros_like(gd_ref)

        u = u_ref[...]
        dyv = dy_ref[...]
        u16, dy16 = u.astype(BF16), dyv.astype(BF16)
        subs = [slice(k * LANE, (k + 1) * LANE) for k in range(S5_SUPER)]
        _s5_scatter_rows(xb, [_dot(u16[:, c], b_ref[k]) for k, c in enumerate(subs)])
        _s5_scatter_rows(xg, [_dot_nt(dy16[:, c], c_ref[k]) for k, c in enumerate(subs)])
        ar, ai = a_ref[RE, :], a_ref[IM, :]
        cr, ci = cf_ref[RE, :], cf_ref[IM, :]

        xs[0] = sb_ref[...]

        def fstep(i, carry):
            sr, si = carry
            xr, xi = _cmul(cr, ci, xb[i, RE, :], xb[i, IM, :])
            sr, si = ar * sr - ai * si + xr, ar * si + ai * sr + xi
            xs[i + 1, RE, :] = sr
            xs[i + 1, IM, :] = si
            return sr, si

        lax.fori_loop(0, tc, fstep, (sb_ref[RE, :], sb_ref[IM, :]), unroll=4)

        def rstep(n, carry):
            gr, gi, a0, a1, a2, a3 = carry
            i = tc - 1 - n
            xr = xg[i, RE, :] + ar * gr + ai * gi
            xi = xg[i, IM, :] + ar * gi - ai * gr
            pr, pi = xs[i, RE, :], xs[i, IM, :]
            br, bi = xb[i, RE, :], xb[i, IM, :]
            a0 = a0 + pr * xr + pi * xi
            a1 = a1 + pr * xi - pi * xr
            a2 = a2 + br * xr + bi * xi
            a3 = a3 + br * xi - bi * xr
            xg[i, RE, :] = cr * xr + ci * xi
            xg[i, IM, :] = cr * xi - ci * xr
            return xr, xi, a0, a1, a2, a3

        init = (gcar[RE, :], gcar[IM, :], acc[0], acc[1], acc[2], acc[3])
        gr, gi, a0, a1, a2, a3 = lax.fori_loop(0, tc, rstep, init, unroll=2)
        gcar[RE, :] = gr
        gcar[IM, :] = gi
        for idx, a in enumerate((a0, a1, a2, a3)):
            acc[idx] = a
        ga_ref[RE, :] = a0
        ga_ref[IM, :] = a1
        gcf_ref[RE, :] = a2
        gcf_ref[IM, :] = a3

        g_rows = _s5_gather_rows(xg, tc)
        s_rows = _s5_gather_rows(xs, tc, first_tile=1)
        for k in range(S5_SUPER):
            cols = subs[k]
            g16 = g_rows[k].astype(BF16)
            s16 = s_rows[k].astype(BF16)
            gb_ref[k] += _dot_tn(u16[:, cols], g16)
            gc_ref[k] += _dot_tn(s16, dy16[:, cols])
            du_ref[:, cols] = (_dot_nt(g16, b_ref[k]) + d_ref[:, cols] * dyv[:, cols]).astype(BF16)
        gd_ref[...] += jnp.sum(dyv * u, axis=0, keepdims=True)

    full = lambda shape: pl.BlockSpec(shape, lambda c: (0,) * len(shape))
    rows = pl.BlockSpec((tc, S5_WIDTH), lambda c: (last - c, 0))
    tile = (S5_TILE, LANE)
    return pl.pallas_call(
        body, name="s5_bwd", grid=(n_chunks,),
        in_specs=[rows, rows, pl.BlockSpec((None, S5_TILE, LANE), lambda c: (last - c, 0, 0)),
                  full(bsg.shape), full(ccat.shape), full(d_row.shape), full(abar_t.shape), full(coef_t.shape)],
        out_specs=[rows, full(bsg.shape), full(ccat.shape), full(d_row.shape), full(tile), full(tile)],
        out_shape=[jax.ShapeDtypeStruct((t, S5_WIDTH), BF16), jax.ShapeDtypeStruct(bsg.shape, F32),
                   jax.ShapeDtypeStruct(ccat.shape, F32), jax.ShapeDtypeStruct(d_row.shape, F32),
                   jax.ShapeDtypeStruct(tile, F32), jax.ShapeDtypeStruct(tile, F32)],
        scratch_shapes=[pltpu.VMEM((tc, S5_TILE, LANE), F32), pltpu.VMEM((tc + 1, S5_TILE, LANE), F32),
                        pltpu.VMEM((tc, S5_TILE, LANE), F32), pltpu.VMEM(tile, F32),
                        pltpu.VMEM((4, S5_HALF, LANE), F32)],
        compiler_params=_params(("arbitrary",)),
    )(proj, dy, sb, bsg, ccat, d_row, abar_t, coef_t)


def _gelu_fwd(y, name="s5_gelu"):
    t, w = y.shape
    tr = _pick(t, 512, SUBLANE)

    def body(y_ref, z_ref):
        z_ref[...] = _gelu_and_grad(y_ref[...])[0].astype(BF16)

    row = pl.BlockSpec((tr, w), lambda i: (i, 0))
    return pl.pallas_call(body, name=name, grid=(t // tr,), in_specs=[row], out_specs=row,
                          out_shape=jax.ShapeDtypeStruct((t, w), BF16), compiler_params=_params(("parallel",)))(y)


def _glu_fwd(y, gl, b, name="s5_glu"):
    t, w = y.shape
    tr = _pick(t, 512, SUBLANE)

    def body(y_ref, gl_ref, b_ref, z2_ref):
        z = _gelu_and_grad(y_ref[...])[0]
        z2_ref[...] = (z * _sigmoid(gl_ref[...] + b_ref[...])).astype(BF16)

    row = pl.BlockSpec((tr, w), lambda i: (i, 0))
    return pl.pallas_call(body, name=name, grid=(t // tr,),
                          in_specs=[row, row, pl.BlockSpec((1, w), lambda i: (0, 0))], out_specs=row,
                          out_shape=jax.ShapeDtypeStruct((t, w), BF16), compiler_params=_params(("parallel",)))(y, gl, b)


def _glu_bwd(y, gl, b, dz2, name="s5_glu_bwd", after=None):
    t, w = y.shape
    tr = _pick(t, 512, SUBLANE)

    def body(y_ref, gl_ref, b_ref, dz2_ref, dgl_ref, dza_ref, db_ref):
        @pl.when(pl.program_id(0) == 0)
        def _():
            db_ref[...] = jnp.zeros_like(db_ref)

        z = _gelu_and_grad(y_ref[...])[0]
        s = _sigmoid(gl_ref[...] + b_ref[...])
        dz2v = dz2_ref[...]
        dgl = dz2v * z * s * (1.0 - s)
        dgl_ref[...] = dgl.astype(BF16)
        dza_ref[...] = dz2v * s
        db_ref[...] += jnp.sum(dgl, axis=0, keepdims=True)

    row = pl.BlockSpec((tr, w), lambda i: (i, 0))
    vec = pl.BlockSpec((1, w), lambda i: (0, 0))
    body, in_specs, args = _ordered(body, [row, row, vec, row], [y, gl, b, dz2], after)
    return pl.pallas_call(body, name=name, grid=(t // tr,), in_specs=in_specs, out_specs=[row, row, vec],
                          out_shape=[jax.ShapeDtypeStruct((t, w), BF16), jax.ShapeDtypeStruct((t, w), F32),
                                     jax.ShapeDtypeStruct((1, w), F32)],
                          compiler_params=_params(("arbitrary",)))(*args)


def _gelu_bwd(y, dza, dzb, name="s5_gelu_bwd", after=None):
    t, w = y.shape
    tr = _pick(t, 512, SUBLANE)

    def body(y_ref, a_ref, b_ref, dy_ref):
        dy_ref[...] = (a_ref[...] + b_ref[...]) * _gelu_and_grad(y_ref[...])[1]

    row = pl.BlockSpec((tr, w), lambda i: (i, 0))
    body, in_specs, args = _ordered(body, [row, row, row], [y, dza, dzb], after)
    return pl.pallas_call(body, name=name, grid=(t // tr,), in_specs=in_specs, out_specs=row,
                          out_shape=jax.ShapeDtypeStruct((t, w), F32), compiler_params=_params(("parallel",)))(*args)


def _tri_dot(tri16, x):
    hi = x.astype(BF16)
    r1 = x - hi.astype(F32)
    mid = r1.astype(BF16)
    lo = (r1 - mid.astype(F32)).astype(BF16)
    return _dot(tri16, hi) + _dot(tri16, mid) + _dot(tri16, lo)


def _hgrn_pre(q_in, z, lg):
    lb = _sigmoid(lg[0:1, :] - lg[1:2, :])
    qs, dqs = _silu_and_grad(q_in)
    sz = _sigmoid(z)
    f = lb + (1.0 - lb) * sz
    k = (1.0 - lb) * (1.0 - sz)
    c = HGRN_CHUNK
    r = lax.broadcasted_iota(jnp.int32, (c, c), 0)
    s = lax.broadcasted_iota(jnp.int32, (c, c), 1)
    causal = r >= s
    b = _tri_dot(jnp.where(causal, 1.0, 0.0).astype(BF16), jnp.log(f))
    b_end = b[c - 1:c, :]
    b_mid = b[c // 2 - 1:c // 2, :]
    e_q, e_k, e_0, e_c = jnp.exp(b - b_mid), jnp.exp(b_mid - b), jnp.exp(b), jnp.exp(b_end - b)
    return dict(lb=lb, qs=qs, dqs=dqs, sz=sz, f=f, k=k, causal=causal, b_end=b_end,
                e_q=e_q, e_k=e_k, e_0=e_0, e_c=e_c,
                qt=qs * e_q, kt=k * e_k, q0=qs * e_0, kc=k * e_c)


def _hgrn_fwd(proj, logits, ng):
    t = proj.shape[0]
    c, dh = HGRN_CHUNK, HGRN_DH
    n_chunks = t // c

    def head(h, q_ref, z_ref, v_ref, g_ref, lg_ref, ng_ref, o_ref, oh_ref, s0_ref, st):
        sl = slice(h * dh, (h + 1) * dh)
        s0 = st[h]
        s0_ref[h] = s0
        p = _hgrn_pre(q_ref[:, sl], z_ref[:, sl], lg_ref[:, sl])
        v16 = v_ref[:, sl].astype(BF16)
        a = jnp.where(p["causal"], _dot_nt(p["qt"].astype(BF16), p["kt"].astype(BF16)), 0.0)
        o = _dot_nt(p["q0"].astype(BF16), s0.astype(BF16)) + _dot(a.astype(BF16), v16)
        st[h] = jnp.exp(p["b_end"]) * s0 + _dot_tn(v16, p["kc"].astype(BF16))
        o_ref[:, sl] = o
        rn = lax.rsqrt(jnp.mean(o * o, axis=-1, keepdims=True) + RMS_EPS)
        oh_ref[:, sl] = (o * rn * ng_ref[:, sl] * _silu_and_grad(g_ref[:, sl])[0]).astype(BF16)

    def body(*refs):
        st = refs[-1]

        @pl.when(pl.program_id(0) == 0)
        def _():
            st[...] = jnp.zeros_like(st)

        for h in range(HGRN_HEADS):
            head(h, *refs)

    def wide(off):
        return pl.BlockSpec((c, HGRN_WIDTH), lambda i: (i, off))

    return pl.pallas_call(
        body, name="hgrn_fwd", grid=(n_chunks,),
        in_specs=[wide(1), wide(2), wide(3), wide(4),
                  pl.BlockSpec((2, HGRN_WIDTH), lambda i: (0, 0)), pl.BlockSpec((1, HGRN_WIDTH), lambda i: (0, 0))],
        out_specs=[wide(0), wide(0), pl.BlockSpec((HGRN_HEADS, None, dh, dh), lambda i: (0, i, 0, 0))],
        out_shape=[jax.ShapeDtypeStruct((t, HGRN_WIDTH), F32), jax.ShapeDtypeStruct((t, HGRN_WIDTH), BF16),
                   jax.ShapeDtypeStruct((HGRN_HEADS, n_chunks, dh, dh), F32)],
        scratch_shapes=[pltpu.VMEM((HGRN_HEADS, dh, dh), F32)],
        compiler_params=_params(("arbitrary",)),
    )(proj, proj, proj, proj, logits, ng)


def _hgrn_bwd(proj, o_raw, s0s, doh, logits, ng):
    t = proj.shape[0]
    c, dh = HGRN_CHUNK, HGRN_DH
    n_chunks = t // c
    last = n_chunks - 1

    def head(h, q_ref, z_ref, v_ref, g_ref, o_ref, s0_ref, doh_ref, lg_ref, ng_ref,
             dq_ref, dz_ref, dv_ref, dg_ref, dng_ref, dlb_ref, dst):
        sl = slice(h * dh, (h + 1) * dh)
        p = _hgrn_pre(q_ref[:, sl], z_ref[:, sl], lg_ref[:, sl])
        v = v_ref[:, sl]
        v16 = v.astype(BF16)
        s0 = s0_ref[h]
        ds_end = dst[h]
        ds16 = ds_end.astype(BF16)
        ngv = ng_ref[:, sl]

        o = o_ref[:, sl]
        dohv = doh_ref[:, sl]
        sg, dsg = _silu_and_grad(g_ref[:, sl])
        rn = lax.rsqrt(jnp.mean(o * o, axis=-1, keepdims=True) + RMS_EPS)
        oh = o * rn
        dg_ref[:, sl] = (dohv * oh * ngv * dsg).astype(BF16)
        don = dohv * sg
        dng_ref[:, sl] += jnp.sum(don * oh, axis=0, keepdims=True)
        doh_n = don * ngv
        do = rn * (doh_n - oh * jnp.mean(doh_n * oh, axis=-1, keepdims=True))
        do16 = do.astype(BF16)

        qt16, kt16, q016, kc16 = (p[n].astype(BF16) for n in ("qt", "kt", "q0", "kc"))
        a = jnp.where(p["causal"], _dot_nt(qt16, kt16), 0.0)
        da = jnp.where(p["causal"], _dot_nt(do16, v16), 0.0)
        da16 = da.astype(BF16)
        dqt = _dot(da16, kt16)
        dq0 = _dot(do16, s0.astype(BF16))
        dkt = _dot_tn(da16, qt16)
        dkc = _dot(v16, ds16)
        dv_ref[:, sl] = (_dot_tn(a.astype(BF16), do16) + _dot_nt(kc16, ds16)).astype(BF16)
        lam_end = jnp.exp(p["b_end"])
        dst[h] = lam_end * ds_end + _dot_tn(do16, q016)

        qt, kt, q0, kc = (a.astype(F32) for a in (qt16, kt16, q016, kc16))
        db = dqt * qt + dq0 * q0 - dkt * kt - dkc * kc
        db_end = (jnp.sum(dkc * kc, axis=0, keepdims=True)
                  + jnp.sum(ds_end * s0, axis=0, keepdims=True) * lam_end)
        rowi = lax.broadcasted_iota(jnp.int32, (c, dh), 0)
        db = db + jnp.where(rowi == c - 1, db_end, 0.0)
        r = lax.broadcasted_iota(jnp.int32, (c, c), 0)
        s = lax.broadcasted_iota(jnp.int32, (c, c), 1)
        dlf = _tri_dot(jnp.where(s >= r, 1.0, 0.0).astype(BF16), db)

        dqs = dqt * p["e_q"] + dq0 * p["e_0"]
        dq_ref[:, sl] = (dqs * p["dqs"]).astype(BF16)
        dk = dkt * p["e_k"] + dkc * p["e_c"]
        sz, lb = p["sz"], p["lb"]
        common = dlf / p["f"] - dk
        dz_ref[:, sl] = ((1.0 - lb) * sz * (1.0 - sz) * common).astype(BF16)
        dlb_ref[:, sl] += jnp.sum((1.0 - sz) * common, axis=0, keepdims=True)

    def body(*refs):
        dng_ref, dlb_ref, dst = refs[-3:]

        @pl.when(pl.program_id(0) == 0)
        def _():
            dst[...] = jnp.zeros_like(dst)
            dng_ref[...] = jnp.zeros_like(dng_ref)
            dlb_ref[...] = jnp.zeros_like(dlb_ref)

        for h in range(HGRN_HEADS):
            head(h, *refs)

    def wide(off):
        return pl.BlockSpec((c, HGRN_WIDTH), lambda i: (last - i, off))

    vec = pl.BlockSpec((1, HGRN_WIDTH), lambda i: (0, 0))
    act = jax.ShapeDtypeStruct((t, HGRN_WIDTH), BF16)
    vsh = jax.ShapeDtypeStruct((1, HGRN_WIDTH), F32)
    return pl.pallas_call(
        body, name="hgrn_bwd", grid=(n_chunks,),
        in_specs=[wide(1), wide(2), wide(3), wide(4), wide(0),
                  pl.BlockSpec((HGRN_HEADS, None, dh, dh), lambda i: (0, last - i, 0, 0)),
                  wide(0), pl.BlockSpec((2, HGRN_WIDTH), lambda i: (0, 0)), vec],
        out_specs=[wide(0), wide(0), wide(0), wide(0), vec, vec],
        out_shape=[act, act, act, act, vsh, vsh],
        scratch_shapes=[pltpu.VMEM((HGRN_HEADS, dh, dh), F32)],
        compiler_params=_params(("arbitrary",)),
    )(proj, proj, proj, proj, o_raw, s0s, doh, logits, ng)


def _lb_bwd(logits, dlb):
    def body(lg_ref, d_ref, o_ref):
        lg = lg_ref[...]
        lb = _sigmoid(lg[0:1, :] - lg[1:2, :])
        g = d_ref[...] * lb * (1.0 - lb)
        o_ref[0:1, :] = g
        o_ref[1:2, :] = -g

    return pl.pallas_call(body, name="hgrn_lb_bwd", out_shape=jax.ShapeDtypeStruct(logits.shape, F32),
                          compiler_params=_params())(logits, dlb)


MERGE_TC = 1024
GS_BLOCK = (S5_WIDTH + 4 * HGRN_WIDTH) // MERGE_TC
GH_BLOCK = GS_BLOCK + D_MODEL // MERGE_TC


def _merge_fwd(proj, ys, yh):
    t = proj.shape[0]
    tr = _pick(t, 256, SUBLANE)

    def body(gs_ref, gh_ref, ys_ref, yh_ref, m_ref):
        m_ref[...] = (_sigmoid(gs_ref[...]) * ys_ref[...] + _sigmoid(gh_ref[...]) * yh_ref[...]).astype(BF16)

    blk = pl.BlockSpec((tr, MERGE_TC), lambda i, j: (i, j))
    return pl.pallas_call(
        body, name="merge_fwd", grid=(t // tr, D_MODEL // MERGE_TC),
        in_specs=[pl.BlockSpec((tr, MERGE_TC), lambda i, j: (i, GS_BLOCK + j)),
                  pl.BlockSpec((tr, MERGE_TC), lambda i, j: (i, GH_BLOCK + j)), blk, blk],
        out_specs=blk, out_shape=jax.ShapeDtypeStruct((t, D_MODEL), BF16),
        compiler_params=_params(("parallel", "parallel")),
    )(proj, proj, ys, yh)


def _merge_bwd(proj, ys, yh, dm, after=None):
    t = proj.shape[0]
    tr = _pick(t, 256, SUBLANE)

    def body(gs_ref, gh_ref, ys_ref, yh_ref, dm_ref, dys_ref, dyh_ref, dgs_ref, dgh_ref):
        dmv = dm_ref[...]
        ss, sh = _sigmoid(gs_ref[...]), _sigmoid(gh_ref[...])
        dys_ref[...] = (dmv * ss).astype(BF16)
        dyh_ref[...] = (dmv * sh).astype(BF16)
        dgs_ref[...] = (dmv * ys_ref[...] * ss * (1.0 - ss)).astype(BF16)
        dgh_ref[...] = (dmv * yh_ref[...] * sh * (1.0 - sh)).astype(BF16)

    blk = pl.BlockSpec((tr, MERGE_TC), lambda i, j: (i, j))
    sh16 = jax.ShapeDtypeStruct((t, D_MODEL), BF16)
    in_specs = [pl.BlockSpec((tr, MERGE_TC), lambda i, j: (i, GS_BLOCK + j)),
                pl.BlockSpec((tr, MERGE_TC), lambda i, j: (i, GH_BLOCK + j)), blk, blk, blk]
    body, in_specs, args = _ordered(body, in_specs, [proj, proj, ys, yh, dm], after)
    return pl.pallas_call(
        body, name="merge_bwd", grid=(t // tr, D_MODEL // MERGE_TC),
        in_specs=in_specs,
        out_specs=[blk, blk, blk, blk], out_shape=[sh16, sh16, sh16, sh16],
        compiler_params=_params(("parallel", "parallel")),
    )(*args)


FFN_TC = 128
FFN_ROWS = 512
HALO = SUBLANE


def _rows_with_halo(ref, r0, nrows, t, before, after):
    lo = r0 - before if r0 - before >= 0 else r0
    hi = r0 + nrows + after if r0 + nrows + after <= t else r0 + nrows
    parts = []
    if lo == r0 and before:
        parts.append(jnp.zeros((before, ref.shape[1]), F32))
    parts.append(ref[lo:hi, :])
    if hi == r0 + nrows and after:
        parts.append(jnp.zeros((after, ref.shape[1]), F32))
    return parts[0] if len(parts) == 1 else jnp.concatenate(parts, axis=0)


def _conv3(ext, w, b, nrows, off):
    n = ext.shape[0]
    x0 = ext[off:off + nrows, :]
    x1 = pltpu.roll(ext, 1, 0)[off:off + nrows, :]
    x2 = pltpu.roll(ext, 2, 0)[off:off + nrows, :]
    return b + w[0:1, :] * x2 + w[1:2, :] * x1 + w[2:3, :] * x0, (x0, x1, x2)


def _ffn_act_fwd(up, cw, cb):
    t = up.shape[0]
    rows = _pick(t, FFN_ROWS, SUBLANE)
    nvb = D_FF // FFN_TC

    def body(ug_ref, uv_ref, wg_ref, wv_ref, bg_ref, bv_ref, act_ref):
        wg, wv, bg, bv = wg_ref[...], wv_ref[...], bg_ref[...], bv_ref[...]
        for r0 in range(0, t, rows):
            cg, _ = _conv3(_rows_with_halo(ug_ref, r0, rows, t, HALO, 0), wg, bg, rows, HALO)
            cv, _ = _conv3(_rows_with_halo(uv_ref, r0, rows, t, HALO, 0), wv, bv, rows, HALO)
            act_ref[r0:r0 + rows, :] = (_silu_and_grad(cg)[0] * cv).astype(BF16)

    def colblk(nrow, off):
        return pl.BlockSpec((nrow, FFN_TC), lambda j: (0, off + j))

    return pl.pallas_call(
        body, name="ffn_act_fwd", grid=(nvb,),
        in_specs=[colblk(t, 0), colblk(t, nvb), colblk(3, 0), colblk(3, nvb), colblk(1, 0), colblk(1, nvb)],
        out_specs=colblk(t, 0), out_shape=jax.ShapeDtypeStruct((t, D_FF), BF16),
        compiler_params=_params(("parallel",)),
    )(up, up, cw, cw, cb, cb)


def _ffn_act_bwd(up, dact, cw, cb, after=None):
    t = up.shape[0]
    rows = _pick(t, FFN_ROWS, SUBLANE)
    nvb = D_FF // FFN_TC

    def body(ug_ref, uv_ref, da_ref, wg_ref, wv_ref, bg_ref, bv_ref,
             dug_ref, duv_ref, dwg_ref, dwv_ref, dbg_ref, dbv_ref):
        wg, wv, bg, bv = wg_ref[...], wv_ref[...], bg_ref[...], bv_ref[...]
        ext = rows + HALO
        acc_g = [jnp.zeros((1, FFN_TC), F32) for _ in range(4)]
        acc_v = [jnp.zeros((1, FFN_TC), F32) for _ in range(4)]
        for r0 in range(0, t, rows):
            cg, xg = _conv3(_rows_with_halo(ug_ref, r0, rows, t, HALO, HALO), wg, bg, ext, HALO)
            cv, xv = _conv3(_rows_with_halo(uv_ref, r0, rows, t, HALO, HALO), wv, bv, ext, HALO)
            dav = _rows_with_halo(da_ref, r0, rows, t, 0, HALO)
            sg, dsg = _silu_and_grad(cg)
            for dconv, xs, w, acc, out in ((dav * cv * dsg, xg, wg, acc_g, dug_ref), (dav * sg, xv, wv, acc_v, duv_ref)):
                d0 = dconv[0:rows, :]
                d1 = pltpu.roll(dconv, ext - 1, 0)[0:rows, :]
                d2 = pltpu.roll(dconv, ext - 2, 0)[0:rows, :]
                out[r0:r0 + rows, :] = (w[2:3, :] * d0 + w[1:2, :] * d1 + w[0:1, :] * d2).astype(BF16)
                x0, x1, x2 = xs
                acc[0] = acc[0] + jnp.sum(d0 * x2[0:rows, :], axis=0, keepdims=True)
                acc[1] = acc[1] + jnp.sum(d0 * x1[0:rows, :], axis=0, keepdims=True)
                acc[2] = acc[2] + jnp.sum(d0 * x0[0:rows, :], axis=0, keepdims=True)
                acc[3] = acc[3] + jnp.sum(d0, axis=0, keepdims=True)
        for acc, dw_ref, db_ref in ((acc_g, dwg_ref, dbg_ref), (acc_v, dwv_ref, dbv_ref)):
            dw_ref[0:1, :] = acc[0]
            dw_ref[1:2, :] = acc[1]
            dw_ref[2:3, :] = acc[2]
            db_ref[...] = acc[3]

    def colblk(nrow, off):
        return pl.BlockSpec((nrow, FFN_TC), lambda j: (0, off + j))

    in_specs = [colblk(t, 0), colblk(t, nvb), colblk(t, 0), colblk(3, 0), colblk(3, nvb), colblk(1, 0), colblk(1, nvb)]
    body, in_specs, args = _ordered(body, in_specs, [up, up, dact, cw, cw, cb, cb], after)
    return pl.pallas_call(
        body, name="ffn_act_bwd", grid=(nvb,),
        in_specs=in_specs,
        out_specs=[colblk(t, 0), colblk(t, 0), colblk(3, 0), colblk(3, 0), colblk(1, 0), colblk(1, 0)],
        out_shape=[jax.ShapeDtypeStruct((t, D_FF), BF16), jax.ShapeDtypeStruct((t, D_FF), BF16),
                   jax.ShapeDtypeStruct((3, D_FF), F32), jax.ShapeDtypeStruct((3, D_FF), F32),
                   jax.ShapeDtypeStruct((1, D_FF), F32), jax.ShapeDtypeStruct((1, D_FF), F32)],
        compiler_params=_params(("parallel",)),
    )(*args)


def _all_gather(shards, name):
    nw = len(shards)

    def body(*refs):
        x_refs, out_refs = refs[:nw], refs[nw:2 * nw]
        send_sems, recv_sems, local_sems = refs[2 * nw:]
        x, y, c = lax.axis_index("x"), lax.axis_index("y"), lax.axis_index("c")
        me, sibling = (x, y, c), (x, y, 1 - c)
        chips = [(1 - x, y), (x, 1 - y), (1 - x, 1 - y)]

        def copy(w, k, block, to, src=None):
            slot = out_refs[w].at[4 * block[0] + 2 * block[1] + block[2]]
            return pltpu.make_async_remote_copy(
                src_ref=slot if src is None else src, dst_ref=slot,
                send_sem=send_sems.at[w, k], recv_sem=recv_sems.at[w, k],
                device_id=to, device_id_type=MESH)

        mine, first, passed = [], [], []
        for w in range(nw):
            cp = pltpu.make_async_copy(x_refs[w], out_refs[w].at[4 * x + 2 * y + c], local_sems.at[w])
            cp.start()
            mine.append(cp)
            first.append(copy(w, 0, me, sibling, src=x_refs[w]))
            first += [copy(w, 1 + j, me, (*chip, c), src=x_refs[w]) for j, chip in enumerate(chips)]
        for cp in first:
            cp.start()
        for w in range(nw):
            for j, chip in enumerate(chips):
                copy(w, 1 + j, (*chip, c), me).wait_recv()
                fwd = copy(w, 4 + j, (*chip, c), sibling)
                fwd.start()
                passed.append(fwd)
        for w in range(nw):
            copy(w, 0, sibling, me).wait_recv()
            for j, chip in enumerate(chips):
                copy(w, 4 + j, (*chip, 1 - c), me).wait_recv()
        for cp in first + passed:
            cp.wait_send()
        for cp in mine:
            cp.wait()

    anyspec = pl.BlockSpec(memory_space=pl.ANY)
    return pl.pallas_call(
        body, name=name,
        in_specs=[anyspec] * nw, out_specs=[anyspec] * nw,
        out_shape=[jax.ShapeDtypeStruct((N_DEV,) + s.shape, s.dtype) for s in shards],
        scratch_shapes=[pltpu.SemaphoreType.DMA((nw, 7)), pltpu.SemaphoreType.DMA((nw, 7)),
                        pltpu.SemaphoreType.DMA((nw,))],
    )(*shards)


HBM_SPEC = pl.BlockSpec(memory_space=pltpu.HBM)
SEM_SPEC = pl.BlockSpec(memory_space=pltpu.SEMAPHORE)
ANY_SPEC = pl.BlockSpec(memory_space=pl.ANY)
DATAFLOW = pltpu.SideEffectType.DATAFLOW_SIDE_EFFECTING


def _my_index():
    return 4 * lax.axis_index("x") + 2 * lax.axis_index("y") + lax.axis_index("c")


def _peers():
    x, y, c = lax.axis_index("x"), lax.axis_index("y"), lax.axis_index("c")
    peers = []
    for k in range(1, N_DEV):
        px = 1 - x if k & 4 else x
        py = 1 - y if k & 2 else y
        pc = 1 - c if k & 1 else c
        peers.append((k, (px, py, pc), 4 * px + 2 * py + pc))
    return peers


def _split_copy(src_ref, land_ref, send_sems, recv_sems, w, k, peer, slot, scatter, outgoing):
    return pltpu.make_async_remote_copy(
        src_ref=src_ref.at[slot] if scatter else src_ref,
        dst_ref=land_ref.at[_my_index() if outgoing else slot],
        send_sem=send_sems.at[w * (N_DEV - 1) + k - 1], recv_sem=recv_sems.at[w * (N_DEV - 1) + k - 1],
        device_id=peer, device_id_type=MESH)


def _exchange_start(srcs, scatter, after, name):
    nw = len(srcs)
    me = _my_index()
    lands = []
    for s in srcs:
        own = lax.dynamic_index_in_dim(s, me, 0, keepdims=True) if scatter else s[None]
        shape = s.shape if scatter else (N_DEV,) + s.shape
        lands.append(lax.dynamic_update_slice_in_dim(lax.empty(shape, s.dtype), own, me, 0))

    afters = [] if after is None else [after]

    def body(*refs):
        s_refs, l_refs = refs[:nw], refs[nw:2 * nw]
        send_sems, recv_sems = refs[2 * nw + len(afters)], refs[2 * nw + len(afters) + 1]
        token = refs[-1]
        for w in range(nw):
            for k, peer, slot in _peers():
                _split_copy(s_refs[w], l_refs[w], send_sems, recv_sems, w, k, peer, slot, scatter, True).start()
        token[...] = jnp.zeros_like(token)

    sems = pltpu.SemaphoreType.DMA((nw * (N_DEV - 1),))
    outs = pl.pallas_call(
        body, name=name,
        out_shape=(sems, sems, *[pltpu.HBM(a.shape, a.dtype) for a in (*srcs, *lands)],
                   jax.ShapeDtypeStruct((SUBLANE, LANE), F32)),
        in_specs=[HBM_SPEC] * (2 * nw) + [ANY_SPEC] * len(afters),
        out_specs=(SEM_SPEC, SEM_SPEC, *[HBM_SPEC] * (2 * nw), pl.BlockSpec(memory_space=pltpu.VMEM)),
        input_output_aliases={i: 2 + i for i in range(2 * nw)},
        compiler_params=pltpu.CompilerParams(has_side_effects=DATAFLOW),
    )(*[pltpu.with_memory_space_constraint(a, pltpu.HBM) for a in (*srcs, *lands)], *afters)
    return dict(sems=outs[:2], srcs=outs[2:2 + nw], lands=outs[2 + nw:2 + 2 * nw], token=outs[-1], scatter=scatter)


def _exchange_wait(handle, afters, name):
    srcs, lands, scatter = handle["srcs"], handle["lands"], handle["scatter"]
    nw = len(srcs)

    def body(*refs):
        s_refs, l_refs = refs[:nw], refs[nw:2 * nw]
        send_sems, recv_sems = refs[2 * nw], refs[2 * nw + 1]
        for w in range(nw):
            for k, peer, slot in _peers():
                cp = _split_copy(s_refs[w], l_refs[w], send_sems, recv_sems, w, k, peer, slot, scatter, False)
                cp.wait_send()
                cp.wait_recv()

    outs = pl.pallas_call(
        body, name=name,
        out_shape=tuple(pltpu.HBM(a.shape, a.dtype) for a in (*srcs, *lands)),
        in_specs=[HBM_SPEC] * (2 * nw) + [SEM_SPEC, SEM_SPEC] + [ANY_SPEC] * len(afters),
        out_specs=tuple([HBM_SPEC] * (2 * nw)),
        input_output_aliases={i: i for i in range(2 * nw)},
        compiler_params=pltpu.CompilerParams(has_side_effects=DATAFLOW),
    )(*srcs, *lands, *handle["sems"], *afters)
    return list(outs[nw:])


def _adamw(w, g, m, v):
    m = ADAM_B1 * m + (1.0 - ADAM_B1) * g
    v = ADAM_B2 * v + (1.0 - ADAM_B2) * (g * g)
    m_hat = m / (1.0 - ADAM_B1 ** ADAM_STEP)
    v_hat = v / (1.0 - ADAM_B2 ** ADAM_STEP)
    delta = -ADAM_LR * (m_hat / (jnp.sqrt(v_hat) + ADAM_EPS) + ADAM_WD * w)
    return delta, m, v


def _sum_adam(parts, w, m, v, name):
    _, r, c = parts.shape
    tr = _pick(r, 128, 16)

    def body(p_ref, w_ref, m_ref, v_ref, g_ref, d_ref, mo_ref, vo_ref):
        g = p_ref[0].astype(F32)
        for s in range(1, N_DEV):
            g = g + p_ref[s].astype(F32)
        g_ref[...] = g
        d_ref[...], mo_ref[...], vo_ref[...] = _adamw(w_ref[...], g, m_ref[...], v_ref[...])

    row = pl.BlockSpec((tr, c), lambda i: (i, 0))
    sh = jax.ShapeDtypeStruct((r, c), F32)
    return pl.pallas_call(
        body, name=name, grid=(r // tr,),
        in_specs=[pl.BlockSpec((N_DEV, tr, c), lambda i: (0, i, 0)), row, row, row],
        out_specs=[row, row, row, row], out_shape=[sh, sh, sh, sh],
        compiler_params=_params(("parallel",)),
    )(parts, w, m, v)


def _sum_slots(parts, name):
    _, r, c = parts.shape
    tr = _pick(r, 512, SUBLANE)

    def body(p_ref, o_ref):
        g = p_ref[0]
        for s in range(1, N_DEV):
            g = g + p_ref[s]
        o_ref[...] = g

    return pl.pallas_call(
        body, name=name, grid=(r // tr,),
        in_specs=[pl.BlockSpec((N_DEV, tr, c), lambda i: (0, i, 0))],
        out_specs=pl.BlockSpec((tr, c), lambda i: (i, 0)), out_shape=jax.ShapeDtypeStruct((r, c), F32),
        compiler_params=_params(("parallel",)),
    )(parts)


def _adam_rows(g, w, m, v, name):
    r, c = g.shape
    tr = _pick(r, 512, SUBLANE)

    def body(g_ref, w_ref, m_ref, v_ref, d_ref, mo_ref, vo_ref):
        d_ref[...], mo_ref[...], vo_ref[...] = _adamw(w_ref[...], g_ref[...], m_ref[...], v_ref[...])

    row = pl.BlockSpec((tr, c), lambda i: (i, 0))
    sh = jax.ShapeDtypeStruct((r, c), F32)
    return pl.pallas_call(body, name=name, grid=(r // tr,), in_specs=[row] * 4, out_specs=[row] * 3,
                          out_shape=[sh, sh, sh], compiler_params=_params(("parallel",)))(g, w, m, v)


def _pack(arrays):
    flat = jnp.concatenate([a.reshape(-1).astype(F32) for a in arrays])
    pad = (-flat.shape[0]) % (SUBLANE * LANE)
    return jnp.pad(flat, (0, pad)).reshape(-1, LANE)


def _unpack(packed, shapes):
    flat = packed.reshape(-1)
    out, off = [], 0
    for s in shapes:
        n = math.prod(s)
        out.append(flat[off:off + n].reshape(s))
        off += n
    return out


def _block_diag(t):
    eye = jnp.eye(S5_SUPER, dtype=bool)
    bd = jnp.where(eye[None, :, None, :, None], t[:, :, :, None, :], 0.0)
    return bd.reshape(S5_SUPER, S5_SUPER * t.shape[2], S5_SUPER * t.shape[3])


def _diag_blocks(dense, a, b):
    x = dense.reshape(S5_SUPER, S5_SUPER, a, S5_SUPER, b)
    return jnp.moveaxis(jnp.diagonal(x, axis1=1, axis2=3), -1, 1)


def _s5_layouts(b_re, b_im, c_re, c_im, d):
    g2 = (S5_GROUPS // S5_SUPER, S5_SUPER)
    bt = lambda b: _block_diag(b.reshape(*g2, S5_STATE, S5_GROUP).transpose(0, 1, 3, 2))
    ct = lambda c: _block_diag(c.reshape(*g2, S5_GROUP, S5_STATE).transpose(0, 1, 3, 2))
    bsg = jnp.concatenate([bt(b_re), bt(b_im)], axis=2).astype(BF16)
    ccat = jnp.concatenate([ct(c_re), -ct(c_im)], axis=1).astype(BF16)
    return bsg, ccat, d.reshape(1, S5_WIDTH)


def _s5_param_grads(gb, gc):
    n = S5_LANES
    gb_re = _diag_blocks(gb[:, :, 0:n], S5_GROUP, S5_STATE).transpose(0, 1, 3, 2).reshape(S5_GROUPS, S5_STATE, S5_GROUP)
    gb_im = _diag_blocks(gb[:, :, n:2 * n], S5_GROUP, S5_STATE).transpose(0, 1, 3, 2).reshape(S5_GROUPS, S5_STATE, S5_GROUP)
    gc_re = _diag_blocks(gc[:, 0:n, :], S5_STATE, S5_GROUP).transpose(0, 1, 3, 2).reshape(S5_GROUPS, S5_GROUP, S5_STATE)
    gc_im = -_diag_blocks(gc[:, n:2 * n, :], S5_STATE, S5_GROUP).transpose(0, 1, 3, 2).reshape(S5_GROUPS, S5_GROUP, S5_STATE)
    return gb_re, gb_im, gc_re, gc_im


def _local_step(x, target, weight, emit, small, after=None):
    sp = small
    a_re, a_im = sp["s5_a_re"], sp["s5_a_im"]
    ldt = sp["s5_log_dt"].reshape(S5_GROUPS, 1)

    h1 = _rms_fwd(x, sp["ln_mix_g"], "rms_mix", after=after)
    w_in = weight("w_in", h1)
    proj = _mm_nn(h1, w_in, "mm_in")
    disc = _s5_param_fwd(a_re, a_im, ldt)
    bsg, ccat, d_row = _s5_layouts(sp["s5_b_re"], sp["s5_b_im"], sp["s5_c_re"], sp["s5_c_im"], sp["s5_d"])
    abar_t, coef_t = _s5_to_tile(disc[0], disc[1]), _s5_to_tile(disc[2], disc[3])
    y, sb = _s5_fwd(proj, bsg, ccat, d_row, abar_t, coef_t)
    z16 = _gelu_fwd(y)
    w_glu = weight("s5_w_glu", z16)
    gl = _mm_nn(z16, w_glu, "mm_glu")
    z2 = _glu_fwd(y, gl, sp["s5_b_glu"])
    w_ps = weight("w_proj_s5", z2)
    ys = _mm_nn(z2, w_ps, "mm_proj_s5")
    o_raw, oh, s0s = _hgrn_fwd(proj, sp["hgrn_lb_logits"], sp["hgrn_norm_g"])
    w_ph = weight("w_proj_hgrn", oh)
    yh = _mm_nn(oh, w_ph, "mm_proj_hgrn")
    merged = _merge_fwd(proj, ys, yh)
    w_out = weight("w_out", merged)
    x1 = _mm_nn(merged, w_out, "mm_out", res=x)
    h2 = _rms_fwd(x1, sp["ln_ffn_g"], "rms_ffn")
    w_up = weight("w_up", h2)
    up = _mm_nn(h2, w_up, "mm_up")
    act = _ffn_act_fwd(up, sp["conv_w"], sp["conv_b"])
    w_down = weight("w_down", act)
    x2 = _mm_nn(act, w_down, "mm_down", res=x1)
    dx2, dx2_16, g_ln_final, loss = _loss_head(x2, sp["ln_final_g"], target)

    dact = _mm_nt(dx2_16, w_down, "mm_down_dx")
    tok = emit("w_down", _mm_tn(act, dx2_16, 1, "mm_down_dw"))
    dup_g, dup_v, dcw_g, dcw_v, dcb_g, dcb_v = _ffn_act_bwd(up, dact, sp["conv_w"], sp["conv_b"], after=tok)
    dup = jnp.concatenate([dup_g, dup_v], axis=1)
    g_conv_w = jnp.concatenate([dcw_g, dcw_v], axis=1)
    g_conv_b = jnp.concatenate([dcb_g, dcb_v], axis=1)
    dh2 = _mm_nt(dup, w_up, "mm_up_dx")
    tok = emit("w_up", _mm_tn(h2, dup, N_DEV, "mm_up_dw"))
    dx1, dx1_16, g_ln_ffn = _rms_bwd(x1, sp["ln_ffn_g"], dh2, dx2, "rms_ffn_bwd", True, after=tok)

    dmerged = _mm_nt(dx1_16, w_out, "mm_out_dx")
    tok = emit("w_out", _mm_tn(merged, dx1_16, 1, "mm_out_dw"))
    dys, dyh, dgs, dgh = _merge_bwd(proj, ys, yh, dmerged, after=tok)
    doh = _mm_nt(dyh, w_ph, "mm_proj_hgrn_dx")
    tok = emit("w_proj_hgrn", _mm_tn(oh, dyh, N_DEV, "mm_proj_hgrn_dw"))
    dz2 = _mm_nt(dys, w_ps, "mm_proj_s5_dx", after=tok)
    tok = emit("w_proj_s5", _mm_tn(z2, dys, N_DEV, "mm_proj_s5_dw"))
    dgl, dza, g_b_glu = _glu_bwd(y, gl, sp["s5_b_glu"], dz2, after=tok)
    dzb = _mm_nt(dgl, w_glu, "mm_glu_dx")
    tok = emit("s5_w_glu", _mm_tn(z16, dgl, 1, "mm_glu_dw"))
    dy = _gelu_bwd(y, dza, dzb, after=tok)
    du, gb, gc, gd, g_abar_t, g_coef_t = _s5_bwd(proj, dy, sb, bsg, ccat, d_row, abar_t, coef_t)
    g_a_re, g_a_im, g_ldt = _s5_param_bwd(a_re, a_im, ldt, [*_s5_from_tile(g_abar_t), *_s5_from_tile(g_coef_t)])
    g_b_re, g_b_im, g_c_re, g_c_im = _s5_param_grads(gb, gc)
    dq, dz, dv, dg, g_norm, dlb = _hgrn_bwd(proj, o_raw, s0s, doh, sp["hgrn_lb_logits"], sp["hgrn_norm_g"])
    g_logits = _lb_bwd(sp["hgrn_lb_logits"], dlb)

    small_g = dict(s5_a_re=g_a_re, s5_a_im=g_a_im, s5_log_dt=g_ldt.reshape(1, S5_GROUPS),
                   s5_b_re=g_b_re, s5_b_im=g_b_im, s5_c_re=g_c_re, s5_c_im=g_c_im,
                   s5_d=gd.reshape(S5_GROUPS, S5_GROUP), s5_b_glu=g_b_glu, hgrn_lb_logits=g_logits,
                   hgrn_norm_g=g_norm, ln_ffn_g=g_ln_ffn, conv_w=g_conv_w, conv_b=g_conv_b, ln_final_g=g_ln_final,
                   loss=loss[0, 0:1])
    tok_small = emit("small", small_g)

    dproj = jnp.concatenate([du, dq, dz, dv, dg, dgs, dgh], axis=1)
    tok = emit("w_in", _mm_tn(h1, dproj, N_DEV, "mm_in_dw"))
    dh1 = _mm_nt(dproj, w_in, "mm_in_dx", after=tok_small)
    grad_x, g_ln_mix = _rms_bwd(x, sp["ln_mix_g"], dh1, dx1, "rms_mix_bwd", False, after=tok)
    return grad_x, g_ln_mix


BIG = ("w_in", "s5_w_glu", "w_proj_s5", "w_proj_hgrn", "w_out", "w_up", "w_down")
COL_SHARDED = ("w_in", "w_proj_s5", "w_proj_hgrn", "w_up")
SMALL = ("ln_mix_g", "s5_a_re", "s5_a_im", "s5_log_dt", "s5_b_re", "s5_b_im", "s5_c_re", "s5_c_im", "s5_d",
         "s5_b_glu", "hgrn_lb_logits", "hgrn_norm_g", "ln_ffn_g", "conv_b", "ln_final_g")
WEIGHTS = ("ln_mix_g", "w_in", "s5_a_re", "s5_a_im", "s5_log_dt", "s5_b_re", "s5_b_im", "s5_c_re", "s5_c_im", "s5_d",
           "s5_w_glu", "s5_b_glu", "w_proj_s5", "hgrn_lb_logits", "hgrn_norm_g", "w_proj_hgrn", "w_out", "ln_ffn_g",
           "w_up", "conv_w", "conv_b", "w_down", "ln_final_g")


def kernel(x, ln_mix_g, w_in, s5_a_re, s5_a_im, s5_log_dt, s5_b_re, s5_b_im, s5_c_re, s5_c_im, s5_d, s5_w_glu, s5_b_glu, w_proj_s5, hgrn_lb_logits, hgrn_norm_g, w_proj_hgrn, w_out, ln_ffn_g, w_up, conv_w, conv_b, w_down, ln_final_g, loss_target, m_ln_mix_g, m_w_in, m_s5_a_re, m_s5_a_im, m_s5_log_dt, m_s5_b_re, m_s5_b_im, m_s5_c_re, m_s5_c_im, m_s5_d, m_s5_w_glu, m_s5_b_glu, m_w_proj_s5, m_hgrn_lb_logits, m_hgrn_norm_g, m_w_proj_hgrn, m_w_out, m_ln_ffn_g, m_w_up, m_conv_w, m_conv_b, m_w_down, m_ln_final_g, v_ln_mix_g, v_w_in, v_s5_a_re, v_s5_a_im, v_s5_log_dt, v_s5_b_re, v_s5_b_im, v_s5_c_re, v_s5_c_im, v_s5_d, v_s5_w_glu, v_s5_b_glu, v_w_proj_s5, v_hgrn_lb_logits, v_hgrn_norm_g, v_w_proj_hgrn, v_w_out, v_ln_ffn_g, v_w_up, v_conv_w, v_conv_b, v_w_down, v_ln_final_g):
    given = dict(locals())
    w = {n: given[n] for n in WEIGHTS}
    mom = {n: given["m_" + n] for n in WEIGHTS}
    var = {n: given["v_" + n] for n in WEIGHTS}

    shard16 = {n: w[n][0].astype(BF16) for n in BIG}
    w_in_all, conv_w_all = _all_gather([shard16["w_in"], conv_w[0]], "gather_first")
    gather_groups = (("s5_w_glu", "w_proj_s5", "w_proj_hgrn", "w_out"), ("w_up",), ("w_down",))
    pending, token = {}, w_in_all
    for i, group in enumerate(gather_groups):
        handle = _exchange_start([shard16[n] for n in group], False, token, f"gather_start_{i}")
        token = handle["token"]
        for n in group:
            pending[n] = (group, handle, f"gather_wait_{i}")
    ready = {"w_in": w_in_all}

    def weight(name, after):
        if name not in ready:
            group, handle, wait_name = pending[name]
            for n, g in zip(group, _exchange_wait(handle, [after], wait_name)):
                ready[n] = g
        g = ready[name]
        return g if name in COL_SHARDED else g.reshape(1, N_DEV * g.shape[1], g.shape[2])

    scatter_groups = (("w_down",), ("w_up",), ("w_out", "w_proj_hgrn", "w_proj_s5", "s5_w_glu"), ("w_in",))
    emitted, scatters = {}, []
    packed_names = SMALL[1:] + ("conv_w", "loss")

    def emit(name, grad):
        if name == "small":
            emitted[name] = ([grad[n].shape for n in packed_names],
                             _exchange_start([_pack([grad[n] for n in packed_names])], False, None, "small_start"))
            return emitted[name][1]["token"]
        emitted[name] = grad if name in COL_SHARDED else grad.reshape(N_DEV, -1, grad.shape[2])
        group = scatter_groups[len(scatters)]
        if not all(n in emitted for n in group):
            return None
        handle = _exchange_start([emitted[n] for n in group], True, None, f"scatter_start_{len(scatters)}")
        scatters.append((group, handle))
        return handle["token"]

    small = dict(ln_mix_g=ln_mix_g, s5_a_re=s5_a_re[0], s5_a_im=s5_a_im[0], s5_log_dt=s5_log_dt,
                 s5_b_re=s5_b_re[0], s5_b_im=s5_b_im[0], s5_c_re=s5_c_re[0], s5_c_im=s5_c_im[0], s5_d=s5_d[0],
                 s5_b_glu=s5_b_glu, hgrn_lb_logits=hgrn_lb_logits, hgrn_norm_g=hgrn_norm_g, ln_ffn_g=ln_ffn_g,
                 conv_w=conv_w_all.transpose(1, 0, 2).reshape(3, 2 * D_FF), conv_b=conv_b,
                 ln_final_g=ln_final_g.reshape(1, D_MODEL))
    grad_x, g_ln_mix = _local_step(x[0], loss_target[0], weight, emit, small, after=token)

    shapes, handle = emitted["small"]
    total = _sum_slots(_exchange_wait(handle, [grad_x], "small_wait")[0], "sum_small")
    summed = dict(zip(packed_names, _unpack(total, shapes)))
    mix_all = _all_gather([g_ln_mix.reshape(-1, LANE)], "gather_ln_mix")[0]
    summed["ln_mix_g"] = _sum_slots(mix_all, "sum_ln_mix").reshape(1, D_MODEL)

    grads, delta, new_m, new_v = {}, {}, {}, {}
    afters = [grad_x, total]
    for i, (group, handle) in enumerate(scatters):
        for n, r in zip(group, _exchange_wait(handle, afters, f"scatter_wait_{i}")):
            g, d, m2, v2 = _sum_adam(r, w[n][0], mom[n][0], var[n][0], "adam_" + n)
            grads[n], delta[n], new_m[n], new_v[n] = g[None], d[None], m2[None], v2[None]
        if i == len(scatters) - 2:
            afters = [delta[n] for g2, _ in scatters[:-1] for n in g2]

    packed_small = SMALL[1:]
    pw = _pack([w[n] for n in packed_small])
    d_s, m_s, v_s = _adam_rows(_pack([summed[n] for n in packed_small]), pw, _pack([mom[n] for n in packed_small]),
                               _pack([var[n] for n in packed_small]), "adam_small")
    wshapes = [w[n].shape for n in packed_small]
    for n, d, m2, v2 in zip(packed_small, _unpack(d_s, wshapes), _unpack(m_s, wshapes), _unpack(v_s, wshapes)):
        grads[n], delta[n], new_m[n], new_v[n] = summed[n].reshape(w[n].shape), d, m2, v2
    grads["ln_mix_g"] = summed["ln_mix_g"]
    delta["ln_mix_g"], new_m["ln_mix_g"], new_v["ln_mix_g"] = _adam_rows(summed["ln_mix_g"], ln_mix_g, m_ln_mix_g,
                                                                         v_ln_mix_g, "adam_ln_mix")
    me = 4 * lax.axis_index("x") + 2 * lax.axis_index("y") + lax.axis_index("c")
    ncol = conv_w.shape[2]
    g_cw = lax.dynamic_slice_in_dim(summed["conv_w"], me * ncol, ncol, axis=1)
    d_cw, m_cw, v_cw = _adam_rows(g_cw, conv_w[0], m_conv_w[0], v_conv_w[0], "adam_conv_w")
    grads["conv_w"], delta["conv_w"], new_m["conv_w"], new_v["conv_w"] = g_cw[None], d_cw[None], m_cw[None], v_cw[None]

    return (summed["loss"].reshape(()), grad_x[None], *[grads[n] for n in WEIGHTS], *[delta[n] for n in WEIGHTS],
            *[new_m[n] for n in WEIGHTS], *[new_v[n] for n in WEIGHTS])
```

```python
import math

import jax
import jax.numpy as jnp
from jax import lax
from jax.experimental import pallas as pl
from jax.experimental.pallas import tpu as pltpu

F32 = jnp.float32
BF16 = jnp.bfloat16

N_DEV = 8
D_MODEL = 2048
S5_WIDTH = 1024
S5_GROUP = 16
S5_GROUPS = 64
S5_STATE = 64
S5_MAX_RE = -1e-4
S5_SUPER = 8
S5_LANES = S5_SUPER * S5_STATE
HGRN_WIDTH = 1024
HGRN_HEADS = 8
HGRN_DH = 128
HGRN_CHUNK = 64
D_FF = 5632
RMS_EPS = 1e-6
ADAM_LR = 0.001
ADAM_B1 = 0.9
ADAM_B2 = 0.999
ADAM_EPS = 1e-08
ADAM_WD = 0.01
ADAM_STEP = 10

LANE = 128
SUBLANE = 8
VMEM_LIMIT = 48 * 1024 * 1024
MESH = pl.DeviceIdType.MESH
GELU_C = math.sqrt(2.0 / math.pi)
GELU_A = 0.044715


def _params(sem=None):
    return pltpu.CompilerParams(dimension_semantics=sem, vmem_limit_bytes=VMEM_LIMIT)


def _pick(n, cap, unit=LANE):
    best = None
    for t in range(unit, min(n, cap) + 1, unit):
        if n % t == 0:
            best = t
    return best if best is not None else n


def _ordered(body, in_specs, args, after):
    if after is None:
        return body, list(in_specs), list(args)
    n_in = len(args)

    def ordered_body(*refs):
        return body(*refs[:n_in], *refs[n_in + 1:])

    return ordered_body, [*in_specs, pl.BlockSpec(memory_space=pl.ANY)], [*args, after]


def _sigmoid(x):
    return 0.5 * jnp.tanh(0.5 * x) + 0.5


def _silu_and_grad(x):
    s = _sigmoid(x)
    return x * s, s * (1.0 + x * (1.0 - s))


def _gelu_and_grad(y):
    inner = GELU_C * (y + GELU_A * y * y * y)
    th = jnp.tanh(inner)
    val = 0.5 * y * (1.0 + th)
    grad = 0.5 * (1.0 + th) + 0.5 * y * (1.0 - th * th) * GELU_C * (1.0 + 3.0 * GELU_A * y * y)
    return val, grad


def _dot(a, b):
    return jnp.dot(a, b, preferred_element_type=F32)


def _dot_nt(a, b):
    return lax.dot_general(a, b, (((1,), (1,)), ((), ())), preferred_element_type=F32)


def _dot_tn(a, b):
    return lax.dot_general(a, b, (((0,), (0,)), ((), ())), preferred_element_type=F32)


def _blocks_per_step(nb, ns, tn, cap=2048):
    if tn != ns:
        return 1
    best = 1
    for b in range(1, nb + 1):
        if nb % b == 0 and b * ns <= cap:
            best = b
    return best


def _mm_nn(a, w, name, res=None, out_dtype=F32):
    m, kdim = a.shape
    nb, _, ns = w.shape
    tm, tk, tn = _pick(m, 512), _pick(kdim, 2048), _pick(ns, 1536)
    npb, nk = ns // tn, kdim // tk
    bps = _blocks_per_step(nb, ns, tn)
    assert bps == 1 or nk == 1

    def body(*refs):
        a_ref, w_ref = refs[0], refs[1]
        r_ref = refs[2] if res is not None else None
        o_ref = refs[3] if res is not None else refs[2]

        def finish(r, cols):
            if res is not None:
                r = r + r_ref[:, cols]
            o_ref[:, cols] = r.astype(out_dtype)

        if nk == 1:
            for b in range(bps):
                finish(_dot(a_ref[...], w_ref[b]), slice(b * tn, (b + 1) * tn))
            return
        acc = refs[-1]
        k = pl.program_id(2)

        @pl.when(k == 0)
        def _():
            acc[...] = jnp.zeros_like(acc)

        acc[...] += _dot(a_ref[...], w_ref[0])

        @pl.when(k == nk - 1)
        def _():
            finish(acc[...], slice(0, tn))

    in_specs = [pl.BlockSpec((tm, tk), lambda j, i, k: (i, k)),
                pl.BlockSpec((bps, tk, tn), lambda j, i, k: (j // npb, k, j % npb))]
    args = [a, w]
    if res is not None:
        in_specs.append(pl.BlockSpec((tm, bps * tn), lambda j, i, k: (i, j)))
        args.append(res)
    return pl.pallas_call(
        body, name=name, grid=(nb * npb // bps, m // tm, nk),
        in_specs=in_specs, out_specs=pl.BlockSpec((tm, bps * tn), lambda j, i, k: (i, j)),
        out_shape=jax.ShapeDtypeStruct((m, nb * ns), out_dtype),
        scratch_shapes=[pltpu.VMEM((tm, tn), F32)] if nk > 1 else [],
        compiler_params=_params(("parallel", "parallel", "arbitrary")),
    )(*args)


def _mm_nt(a, w, name, out_dtype=F32, after=None):
    m, _ = a.shape
    nb, kdim, ns = w.shape
    tm, tko, tn = _pick(m, 1024), _pick(kdim, 1024), _pick(ns, 2048)
    npb = ns // tn
    bps = _blocks_per_step(nb, ns, tn)
    nred = nb * npb // bps

    def body(a_ref, w_ref, o_ref, *scratch):
        total = _dot_nt(a_ref[:, 0:tn], w_ref[0])
        for b in range(1, bps):
            total = total + _dot_nt(a_ref[:, b * tn:(b + 1) * tn], w_ref[b])
        if nred == 1:
            o_ref[...] = total.astype(out_dtype)
            return
        acc = scratch[0]
        n = pl.program_id(2)

        @pl.when(n == 0)
        def _():
            acc[...] = jnp.zeros_like(acc)

        acc[...] += total

        @pl.when(n == nred - 1)
        def _():
            o_ref[...] = acc[...].astype(out_dtype)

    in_specs = [pl.BlockSpec((tm, bps * tn), lambda i, j, n: (i, n)),
                pl.BlockSpec((bps, tko, tn), lambda i, j, n: (n // npb, j, n % npb))]
    body, in_specs, args = _ordered(body, in_specs, [a, w], after)
    return pl.pallas_call(
        body, name=name, grid=(m // tm, kdim // tko, nred),
        in_specs=in_specs,
        out_specs=pl.BlockSpec((tm, tko), lambda i, j, n: (i, j)),
        out_shape=jax.ShapeDtypeStruct((m, kdim), out_dtype),
        scratch_shapes=[pltpu.VMEM((tm, tko), F32)] if nred > 1 else [],
        compiler_params=_params(("parallel", "parallel", "arbitrary")),
    )(*args)


def _mm_tn(a, d, nb, name, out_dtype=BF16, after=None):
    m, kdim = a.shape
    ns = d.shape[1] // nb
    tm, tko, tn = _pick(m, 4096), _pick(kdim, 512), _pick(ns, 1536)
    npb, nm = ns // tn, m // tm

    def body(a_ref, d_ref, o_ref, *scratch):
        if nm == 1:
            o_ref[...] = _dot_tn(a_ref[...], d_ref[...]).astype(out_dtype)
            return
        acc = scratch[0]
        r = pl.program_id(2)

        @pl.when(r == 0)
        def _():
            acc[...] = jnp.zeros_like(acc)

        acc[...] += _dot_tn(a_ref[...], d_ref[...])

        @pl.when(r == nm - 1)
        def _():
            o_ref[...] = acc[...].astype(out_dtype)

    in_specs = [pl.BlockSpec((tm, tko), lambda j, i, r: (r, i)), pl.BlockSpec((tm, tn), lambda j, i, r: (r, j))]
    body, in_specs, args = _ordered(body, in_specs, [a, d], after)
    return pl.pallas_call(
        body, name=name, grid=(nb * npb, kdim // tko, nm),
        in_specs=in_specs,
        out_specs=pl.BlockSpec((None, tko, tn), lambda j, i, r: (j // npb, i, j % npb)),
        out_shape=jax.ShapeDtypeStruct((nb, kdim, ns), out_dtype),
        scratch_shapes=[pltpu.VMEM((tko, tn), F32)] if nm > 1 else [],
        compiler_params=_params(("parallel", "parallel", "arbitrary")),
    )(*args)


def _rms_fwd(x, g, name, after=None):
    t, d = x.shape
    tr = _pick(t, 256, SUBLANE)

    def body(x_ref, g_ref, h_ref):
        xv = x_ref[...]
        r = lax.rsqrt(jnp.mean(xv * xv, axis=-1, keepdims=True) + RMS_EPS)
        h_ref[...] = (xv * r * g_ref[...]).astype(BF16)

    in_specs = [pl.BlockSpec((tr, d), lambda i: (i, 0)), pl.BlockSpec((1, d), lambda i: (0, 0))]
    body, in_specs, args = _ordered(body, in_specs, [x, g], after)
    return pl.pallas_call(
        body, name=name, grid=(t // tr,),
        in_specs=in_specs,
        out_specs=pl.BlockSpec((tr, d), lambda i: (i, 0)),
        out_shape=jax.ShapeDtypeStruct((t, d), BF16),
        compiler_params=_params(("parallel",)),
    )(*args)


def _rms_bwd(x, g, dh, add, name, want_bf16, after=None):
    t, d = x.shape
    tr = _pick(t, 256, SUBLANE)

    def body(x_ref, g_ref, dh_ref, add_ref, *outs):
        if want_bf16:
            dx_ref, dxb_ref, dg_ref = outs
        else:
            dx_ref, dg_ref = outs
        i = pl.program_id(0)

        @pl.when(i == 0)
        def _():
            dg_ref[...] = jnp.zeros_like(dg_ref)

        xv, dhv = x_ref[...], dh_ref[...]
        r = lax.rsqrt(jnp.mean(xv * xv, axis=-1, keepdims=True) + RMS_EPS)
        xh = xv * r
        dg_ref[...] += jnp.sum(dhv * xh, axis=0, keepdims=True)
        dxh = dhv * g_ref[...]
        dx = add_ref[...] + r * (dxh - xh * jnp.mean(dxh * xh, axis=-1, keepdims=True))
        dx_ref[...] = dx
        if want_bf16:
            dxb_ref[...] = dx.astype(BF16)

    row = pl.BlockSpec((tr, d), lambda i: (i, 0))
    vec = pl.BlockSpec((1, d), lambda i: (0, 0))
    out_specs = [row] + ([row] if want_bf16 else []) + [vec]
    out_shape = ([jax.ShapeDtypeStruct((t, d), F32)] + ([jax.ShapeDtypeStruct((t, d), BF16)] if want_bf16 else [])
                 + [jax.ShapeDtypeStruct((1, d), F32)])
    body, in_specs, args = _ordered(body, [row, vec, row, row], [x, g, dh, add], after)
    return pl.pallas_call(
        body, name=name, grid=(t // tr,),
        in_specs=in_specs, out_specs=out_specs, out_shape=out_shape,
        compiler_params=_params(("arbitrary",)),
    )(*args)


def _loss_head(x2, g, target, name="loss_head"):
    t, d = x2.shape
    tr = _pick(t, 256, SUBLANE)

    def body(x_ref, g_ref, t_ref, dx_ref, dxb_ref, dg_ref, loss_ref):
        i = pl.program_id(0)

        @pl.when(i == 0)
        def _():
            dg_ref[...] = jnp.zeros_like(dg_ref)
            loss_ref[...] = jnp.zeros_like(loss_ref)

        xv = x_ref[...]
        gv = g_ref[...]
        r = lax.rsqrt(jnp.mean(xv * xv, axis=-1, keepdims=True) + RMS_EPS)
        xh = xv * r
        err = xh * gv - t_ref[...]
        part = 0.5 * jnp.sum(jnp.mean(err * err, axis=-1, keepdims=True), axis=0, keepdims=True)
        loss_ref[...] += jnp.broadcast_to(part, loss_ref.shape)
        dy = err * (1.0 / d)
        dg_ref[...] += jnp.sum(dy * xh, axis=0, keepdims=True)
        dxh = dy * gv
        dx = r * (dxh - xh * jnp.mean(dxh * xh, axis=-1, keepdims=True))
        dx_ref[...] = dx
        dxb_ref[...] = dx.astype(BF16)

    row = pl.BlockSpec((tr, d), lambda i: (i, 0))
    vec = pl.BlockSpec((1, d), lambda i: (0, 0))
    return pl.pallas_call(
        body, name=name, grid=(t // tr,),
        in_specs=[row, vec, row],
        out_specs=[row, row, vec, pl.BlockSpec((1, LANE), lambda i: (0, 0))],
        out_shape=[jax.ShapeDtypeStruct((t, d), F32), jax.ShapeDtypeStruct((t, d), BF16),
                   jax.ShapeDtypeStruct((1, d), F32), jax.ShapeDtypeStruct((1, LANE), F32)],
        compiler_params=_params(("arbitrary",)),
    )(x2, g, target)


def _s5_discretize(a_re, a_im, ldt):
    lam_re = jnp.minimum(a_re, S5_MAX_RE)
    lam_im = a_im
    dt = jnp.exp(ldt)
    mag = jnp.exp(lam_re * dt)
    abar_re = mag * jnp.cos(lam_im * dt)
    abar_im = mag * jnp.sin(lam_im * dt)
    den = lam_re * lam_re + lam_im * lam_im
    nr = abar_re - 1.0
    ni = abar_im
    coef_re = (nr * lam_re + ni * lam_im) / den
    coef_im = (ni * lam_re - nr * lam_im) / den
    return abar_re, abar_im, coef_re, coef_im


def _s5_param_fwd(a_re, a_im, ldt):
    def body(ar_ref, ai_ref, l_ref, o0, o1, o2, o3):
        outs = _s5_discretize(ar_ref[...], ai_ref[...], l_ref[...])
        for o, v in zip((o0, o1, o2, o3), outs):
            o[...] = v

    sh = jax.ShapeDtypeStruct(a_re.shape, F32)
    return pl.pallas_call(body, name="s5_param_fwd", out_shape=[sh, sh, sh, sh], compiler_params=_params())(a_re, a_im, ldt)


def _s5_param_bwd(a_re, a_im, ldt, cts):
    def body(ar_ref, ai_ref, l_ref, c0, c1, c2, c3, g0, g1, g2):
        _, vjp = jax.vjp(_s5_discretize, ar_ref[...], ai_ref[...], l_ref[...])
        ga, gb, gl = vjp((c0[...], c1[...], c2[...], c3[...]))
        g0[...] = ga
        g1[...] = gb
        g2[...] = gl

    sh = jax.ShapeDtypeStruct(a_re.shape, F32)
    return pl.pallas_call(body, name="s5_param_bwd", out_shape=[sh, sh, jax.ShapeDtypeStruct(ldt.shape, F32)],
                          compiler_params=_params())(a_re, a_im, ldt, *cts)


def _cmul(ar, ai, br, bi):
    return ar * br - ai * bi, ar * bi + ai * br


S5_TC = 128
S5_TILE = S5_SUPER * SUBLANE
S5_HALF = S5_TILE // 2


def _s5_to_tile(re, im):
    f = lambda a: a.reshape(S5_SUPER, S5_LANES // LANE, LANE).transpose(1, 0, 2).reshape(S5_HALF, LANE)
    return jnp.concatenate([f(re), f(im)], axis=0)


def _s5_from_tile(tile):
    f = lambda a: a.reshape(S5_LANES // LANE, S5_SUPER, LANE).transpose(1, 0, 2).reshape(S5_GROUPS, S5_STATE)
    return f(tile[0:S5_HALF]), f(tile[S5_HALF:])


RE = slice(0, S5_HALF)
IM = slice(S5_HALF, S5_TILE)


def _s5_scatter_rows(buf, rows, first_tile=0):
    tc = rows[0].shape[0]
    for j in range(SUBLANE):
        stacked = jnp.stack([r[:, j * LANE:(j + 1) * LANE] for r in rows], axis=0)
        buf[first_tile:first_tile + tc, j * SUBLANE:(j + 1) * SUBLANE, :] = jnp.swapaxes(stacked, 0, 1)


def _s5_gather_rows(buf, tc, first_tile=0):
    per_j = [jnp.swapaxes(buf[first_tile:first_tile + tc, j * SUBLANE:(j + 1) * SUBLANE, :], 0, 1)
             for j in range(SUBLANE)]
    return [jnp.concatenate([per_j[j][k] for j in range(SUBLANE)], axis=1) for k in range(S5_SUPER)]


def _s5_fwd(proj, bsg, ccat, d_row, abar_t, coef_t):
    t = proj.shape[0]
    tc = min(t, S5_TC)
    n_chunks = t // tc

    def body(u_ref, b_ref, c_ref, d_ref, a_ref, cf_ref, y_ref, sb_ref, x, car):
        @pl.when(pl.program_id(0) == 0)
        def _():
            car[...] = jnp.zeros_like(car)

        sb_ref[...] = car[...]
        u = u_ref[...]
        _s5_scatter_rows(x, [_dot(u[:, k * LANE:(k + 1) * LANE].astype(BF16), b_ref[k]) for k in range(S5_SUPER)])
        ar, ai = a_ref[RE, :], a_ref[IM, :]
        cr, ci = cf_ref[RE, :], cf_ref[IM, :]

        def step(i, carry):
            sr, si = carry
            xr, xi = _cmul(cr, ci, x[i, RE, :], x[i, IM, :])
            sr, si = ar * sr - ai * si + xr, ar * si + ai * sr + xi
            x[i, RE, :] = sr
            x[i, IM, :] = si
            return sr, si

        sr, si = lax.fori_loop(0, tc, step, (car[RE, :], car[IM, :]), unroll=4)
        car[RE, :] = sr
        car[IM, :] = si
        for k, s_k in enumerate(_s5_gather_rows(x, tc)):
            cols = slice(k * LANE, (k + 1) * LANE)
            y_ref[:, cols] = _dot(s_k.astype(BF16), c_ref[k]) + d_ref[:, cols] * u[:, cols]

    full = lambda shape: pl.BlockSpec(shape, lambda c: (0,) * len(shape))
    return pl.pallas_call(
        body, name="s5_fwd", grid=(n_chunks,),
        in_specs=[pl.BlockSpec((tc, S5_WIDTH), lambda c: (c, 0)), full(bsg.shape), full(ccat.shape), full(d_row.shape),
                  full(abar_t.shape), full(coef_t.shape)],
        out_specs=[pl.BlockSpec((tc, S5_WIDTH), lambda c: (c, 0)), pl.BlockSpec((None, S5_TILE, LANE), lambda c: (c, 0, 0))],
        out_shape=[jax.ShapeDtypeStruct((t, S5_WIDTH), F32), jax.ShapeDtypeStruct((n_chunks, S5_TILE, LANE), F32)],
        scratch_shapes=[pltpu.VMEM((tc, S5_TILE, LANE), F32), pltpu.VMEM((S5_TILE, LANE), F32)],
        compiler_params=_params(("arbitrary",)),
    )(proj, bsg, ccat, d_row, abar_t, coef_t)


def _s5_bwd(proj, dy, sb, bsg, ccat, d_row, abar_t, coef_t):
    t = proj.shape[0]
    tc = min(t, S5_TC)
    n_chunks = t // tc
    last = n_chunks - 1

    def body(u_ref, dy_ref, sb_ref, b_ref, c_ref, d_ref, a_ref, cf_ref,
             du_ref, gb_ref, gc_ref, gd_ref, ga_ref, gcf_ref, xb, xs, xg, gcar, acc):
        @pl.when(pl.program_id(0) == 0)
        def _():
            gcar[...] = jnp.zeros_like(gcar)
            acc[...] = jnp.zeros_like(acc)
            gb_ref[...] = jnp.zeros_like(gb_ref)
            gc_ref[...] = jnp.zeros_like(gc_ref)
            gd_ref[...] = jnp.zeros_like(gd_ref)

        u = u_ref[...]
        dyv = dy_ref[...]
        u16, dy16 = u.astype(BF16), dyv.astype(BF16)
        subs = [slice(k * LANE, (k + 1) * LANE) for k in range(S5_SUPER)]
        _s5_scatter_rows(xb, [_dot(u16[:, c], b_ref[k]) for k, c in enumerate(subs)])
        _s5_scatter_rows(xg, [_dot_nt(dy16[:, c], c_ref[k]) for k, c in enumerate(subs)])
        ar, ai = a_ref[RE, :], a_ref[IM, :]
        cr, ci = cf_ref[RE, :], cf_ref[IM, :]

        xs[0] = sb_ref[...]

        def fstep(i, carry):
            sr, si = carry
            xr, xi = _cmul(cr, ci, xb[i, RE, :], xb[i, IM, :])
            sr, si = ar * sr - ai * si + xr, ar * si + ai * sr + xi
            xs[i + 1, RE, :] = sr
            xs[i + 1, IM, :] = si
            return sr, si

        lax.fori_loop(0, tc, fstep, (sb_ref[RE, :], sb_ref[IM, :]), unroll=4)

        def rstep(n, carry):
            gr, gi, a0, a1, a2, a3 = carry
            i = tc - 1 - n
            xr = xg[i, RE, :] + ar * gr + ai * gi
            xi = xg[i, IM, :] + ar * gi - ai * gr
            pr, pi = xs[i, RE, :], xs[i, IM, :]
            br, bi = xb[i, RE, :], xb[i, IM, :]
            a0 = a0 + pr * xr + pi * xi
            a1 = a1 + pr * xi - pi * xr
            a2 = a2 + br * xr + bi * xi
            a3 = a3 + br * xi - bi * xr
            xg[i, RE, :] = cr * xr + ci * xi
            xg[i, IM, :] = cr * xi - ci * xr
            return xr, xi, a0, a1, a2, a3

        init = (gcar[RE, :], gcar[IM, :], acc[0], acc[1], acc[2], acc[3])
        gr, gi, a0, a1, a2, a3 = lax.fori_loop(0, tc, rstep, init, unroll=2)
        gcar[RE, :] = gr
        gcar[IM, :] = gi
        for idx, a in enumerate((a0, a1, a2, a3)):
            acc[idx] = a
        ga_ref[RE, :] = a0
        ga_ref[IM, :] = a1
        gcf_ref[RE, :] = a2
        gcf_ref[IM, :] = a3

        g_rows = _s5_gather_rows(xg, tc)
        s_rows = _s5_gather_rows(xs, tc, first_tile=1)
        for k in range(S5_SUPER):
            cols = subs[k]
            g16 = g_rows[k].astype(BF16)
            s16 = s_rows[k].astype(BF16)
            gb_ref[k] += _dot_tn(u16[:, cols], g16)
            gc_ref[k] += _dot_tn(s16, dy16[:, cols])
            du_ref[:, cols] = (_dot_nt(g16, b_ref[k]) + d_ref[:, cols] * dyv[:, cols]).astype(BF16)
        gd_ref[...] += jnp.sum(dyv * u, axis=0, keepdims=True)

    full = lambda shape: pl.BlockSpec(shape, lambda c: (0,) * len(shape))
    rows = pl.BlockSpec((tc, S5_WIDTH), lambda c: (last - c, 0))
    tile = (S5_TILE, LANE)
    return pl.pallas_call(
        body, name="s5_bwd", grid=(n_chunks,),
        in_specs=[rows, rows, pl.BlockSpec((None, S5_TILE, LANE), lambda c: (last - c, 0, 0)),
                  full(bsg.shape), full(ccat.shape), full(d_row.shape), full(abar_t.shape), full(coef_t.shape)],
        out_specs=[rows, full(bsg.shape), full(ccat.shape), full(d_row.shape), full(tile), full(tile)],
        out_shape=[jax.ShapeDtypeStruct((t, S5_WIDTH), BF16), jax.ShapeDtypeStruct(bsg.shape, F32),
                   jax.ShapeDtypeStruct(ccat.shape, F32), jax.ShapeDtypeStruct(d_row.shape, F32),
                   jax.ShapeDtypeStruct(tile, F32), jax.ShapeDtypeStruct(tile, F32)],
        scratch_shapes=[pltpu.VMEM((tc, S5_TILE, LANE), F32), pltpu.VMEM((tc + 1, S5_TILE, LANE), F32),
                        pltpu.VMEM((tc, S5_TILE, LANE), F32), pltpu.VMEM(tile, F32),
                        pltpu.VMEM((4, S5_HALF, LANE), F32)],
        compiler_params=_params(("arbitrary",)),
    )(proj, dy, sb, bsg, ccat, d_row, abar_t, coef_t)


def _gelu_fwd(y, name="s5_gelu"):
    t, w = y.shape
    tr = _pick(t, 512, SUBLANE)

    def body(y_ref, z_ref):
        z_ref[...] = _gelu_and_grad(y_ref[...])[0].astype(BF16)

    row = pl.BlockSpec((tr, w), lambda i: (i, 0))
    return pl.pallas_call(body, name=name, grid=(t // tr,), in_specs=[row], out_specs=row,
                          out_shape=jax.ShapeDtypeStruct((t, w), BF16), compiler_params=_params(("parallel",)))(y)


def _glu_fwd(y, gl, b, name="s5_glu"):
    t, w = y.shape
    tr = _pick(t, 512, SUBLANE)

    def body(y_ref, gl_ref, b_ref, z2_ref):
        z = _gelu_and_grad(y_ref[...])[0]
        z2_ref[...] = (z * _sigmoid(gl_ref[...] + b_ref[...])).astype(BF16)

    row = pl.BlockSpec((tr, w), lambda i: (i, 0))
    return pl.pallas_call(body, name=name, grid=(t // tr,),
                          in_specs=[row, row, pl.BlockSpec((1, w), lambda i: (0, 0))], out_specs=row,
                          out_shape=jax.ShapeDtypeStruct((t, w), BF16), compiler_params=_params(("parallel",)))(y, gl, b)


def _glu_bwd(y, gl, b, dz2, name="s5_glu_bwd", after=None):
    t, w = y.shape
    tr = _pick(t, 512, SUBLANE)

    def body(y_ref, gl_ref, b_ref, dz2_ref, dgl_ref, dza_ref, db_ref):
        @pl.when(pl.program_id(0) == 0)
        def _():
            db_ref[...] = jnp.zeros_like(db_ref)

        z = _gelu_and_grad(y_ref[...])[0]
        s = _sigmoid(gl_ref[...] + b_ref[...])
        dz2v = dz2_ref[...]
        dgl = dz2v * z * s * (1.0 - s)
        dgl_ref[...] = dgl.astype(BF16)
        dza_ref[...] = dz2v * s
        db_ref[...] += jnp.sum(dgl, axis=0, keepdims=True)

    row = pl.BlockSpec((tr, w), lambda i: (i, 0))
    vec = pl.BlockSpec((1, w), lambda i: (0, 0))
    body, in_specs, args = _ordered(body, [row, row, vec, row], [y, gl, b, dz2], after)
    return pl.pallas_call(body, name=name, grid=(t // tr,), in_specs=in_specs, out_specs=[row, row, vec],
                          out_shape=[jax.ShapeDtypeStruct((t, w), BF16), jax.ShapeDtypeStruct((t, w), F32),
                                     jax.ShapeDtypeStruct((1, w), F32)],
                          compiler_params=_params(("arbitrary",)))(*args)


def _gelu_bwd(y, dza, dzb, name="s5_gelu_bwd", after=None):
    t, w = y.shape
    tr = _pick(t, 512, SUBLANE)

    def body(y_ref, a_ref, b_ref, dy_ref):
        dy_ref[...] = (a_ref[...] + b_ref[...]) * _gelu_and_grad(y_ref[...])[1]

    row = pl.BlockSpec((tr, w), lambda i: (i, 0))
    body, in_specs, args = _ordered(body, [row, row, row], [y, dza, dzb], after)
    return pl.pallas_call(body, name=name, grid=(t // tr,), in_specs=in_specs, out_specs=row,
                          out_shape=jax.ShapeDtypeStruct((t, w), F32), compiler_params=_params(("parallel",)))(*args)


def _tri_dot(tri16, x):
    hi = x.astype(BF16)
    r1 = x - hi.astype(F32)
    mid = r1.astype(BF16)
    lo = (r1 - mid.astype(F32)).astype(BF16)
    return _dot(tri16, hi) + _dot(tri16, mid) + _dot(tri16, lo)


def _hgrn_pre(q_in, z, lg):
    lb = _sigmoid(lg[0:1, :] - lg[1:2, :])
    qs, dqs = _silu_and_grad(q_in)
    sz = _sigmoid(z)
    f = lb + (1.0 - lb) * sz
    k = (1.0 - lb) * (1.0 - sz)
    c = HGRN_CHUNK
    r = lax.broadcasted_iota(jnp.int32, (c, c), 0)
    s = lax.broadcasted_iota(jnp.int32, (c, c), 1)
    causal = r >= s
    b = _tri_dot(jnp.where(causal, 1.0, 0.0).astype(BF16), jnp.log(f))
    b_end = b[c - 1:c, :]
    b_mid = b[c // 2 - 1:c // 2, :]
    e_q, e_k, e_0, e_c = jnp.exp(b - b_mid), jnp.exp(b_mid - b), jnp.exp(b), jnp.exp(b_end - b)
    return dict(lb=lb, qs=qs, dqs=dqs, sz=sz, f=f, k=k, causal=causal, b_end=b_end,
                e_q=e_q, e_k=e_k, e_0=e_0, e_c=e_c,
                qt=qs * e_q, kt=k * e_k, q0=qs * e_0, kc=k * e_c)


def _hgrn_fwd(proj, logits, ng):
    t = proj.shape[0]
    c, dh = HGRN_CHUNK, HGRN_DH
    n_chunks = t // c

    def head(h, q_ref, z_ref, v_ref, g_ref, lg_ref, ng_ref, o_ref, oh_ref, s0_ref, st):
        sl = slice(h * dh, (h + 1) * dh)
        s0 = st[h]
        s0_ref[h] = s0
        p = _hgrn_pre(q_ref[:, sl], z_ref[:, sl], lg_ref[:, sl])
        v16 = v_ref[:, sl].astype(BF16)
        a = jnp.where(p["causal"], _dot_nt(p["qt"].astype(BF16), p["kt"].astype(BF16)), 0.0)
        o = _dot_nt(p["q0"].astype(BF16), s0.astype(BF16)) + _dot(a.astype(BF16), v16)
        st[h] = jnp.exp(p["b_end"]) * s0 + _dot_tn(v16, p["kc"].astype(BF16))
        o_ref[:, sl] = o
        rn = lax.rsqrt(jnp.mean(o * o, axis=-1, keepdims=True) + RMS_EPS)
        oh_ref[:, sl] = (o * rn * ng_ref[:, sl] * _silu_and_grad(g_ref[:, sl])[0]).astype(BF16)

    def body(*refs):
        st = refs[-1]

        @pl.when(pl.program_id(0) == 0)
        def _():
            st[...] = jnp.zeros_like(st)

        for h in range(HGRN_HEADS):
            head(h, *refs)

    def wide(off):
        return pl.BlockSpec((c, HGRN_WIDTH), lambda i: (i, off))

    return pl.pallas_call(
        body, name="hgrn_fwd", grid=(n_chunks,),
        in_specs=[wide(1), wide(2), wide(3), wide(4),
                  pl.BlockSpec((2, HGRN_WIDTH), lambda i: (0, 0)), pl.BlockSpec((1, HGRN_WIDTH), lambda i: (0, 0))],
        out_specs=[wide(0), wide(0), pl.BlockSpec((HGRN_HEADS, None, dh, dh), lambda i: (0, i, 0, 0))],
        out_shape=[jax.ShapeDtypeStruct((t, HGRN_WIDTH), F32), jax.ShapeDtypeStruct((t, HGRN_WIDTH), BF16),
                   jax.ShapeDtypeStruct((HGRN_HEADS, n_chunks, dh, dh), F32)],
        scratch_shapes=[pltpu.VMEM((HGRN_HEADS, dh, dh), F32)],
        compiler_params=_params(("arbitrary",)),
    )(proj, proj, proj, proj, logits, ng)


def _hgrn_bwd(proj, o_raw, s0s, doh, logits, ng):
    t = proj.shape[0]
    c, dh = HGRN_CHUNK, HGRN_DH
    n_chunks = t // c
    last = n_chunks - 1

    def head(h, q_ref, z_ref, v_ref, g_ref, o_ref, s0_ref, doh_ref, lg_ref, ng_ref,
             dq_ref, dz_ref, dv_ref, dg_ref, dng_ref, dlb_ref, dst):
        sl = slice(h * dh, (h + 1) * dh)
        p = _hgrn_pre(q_ref[:, sl], z_ref[:, sl], lg_ref[:, sl])
        v = v_ref[:, sl]
        v16 = v.astype(BF16)
        s0 = s0_ref[h]
        ds_end = dst[h]
        ds16 = ds_end.astype(BF16)
        ngv = ng_ref[:, sl]

        o = o_ref[:, sl]
        dohv = doh_ref[:, sl]
        sg, dsg = _silu_and_grad(g_ref[:, sl])
        rn = lax.rsqrt(jnp.mean(o * o, axis=-1, keepdims=True) + RMS_EPS)
        oh = o * rn
        dg_ref[:, sl] = (dohv * oh * ngv * dsg).astype(BF16)
        don = dohv * sg
        dng_ref[:, sl] += jnp.sum(don * oh, axis=0, keepdims=True)
        doh_n = don * ngv
        do = rn * (doh_n - oh * jnp.mean(doh_n * oh, axis=-1, keepdims=True))
        do16 = do.astype(BF16)

        qt16, kt16, q016, kc16 = (p[n].astype(BF16) for n in ("qt", "kt", "q0", "kc"))
        a = jnp.where(p["causal"], _dot_nt(qt16, kt16), 0.0)
        da = jnp.where(p["causal"], _dot_nt(do16, v16), 0.0)
        da16 = da.astype(BF16)
        dqt = _dot(da16, kt16)
        dq0 = _dot(do16, s0.astype(BF16))
        dkt = _dot_tn(da16, qt16)
        dkc = _dot(v16, ds16)
        dv_ref[:, sl] = (_dot_tn(a.astype(BF16), do16) + _dot_nt(kc16, ds16)).astype(BF16)
        lam_end = jnp.exp(p["b_end"])
        dst[h] = lam_end * ds_end + _dot_tn(do16, q016)

        qt, kt, q0, kc = (a.astype(F32) for a in (qt16, kt16, q016, kc16))
        db = dqt * qt + dq0 * q0 - dkt * kt - dkc * kc
        db_end = (jnp.sum(dkc * kc, axis=0, keepdims=True)
                  + jnp.sum(ds_end * s0, axis=0, keepdims=True) * lam_end)
        rowi = lax.broadcasted_iota(jnp.int32, (c, dh), 0)
        db = db + jnp.where(rowi == c - 1, db_end, 0.0)
        r = lax.broadcasted_iota(jnp.int32, (c, c), 0)
        s = lax.broadcasted_iota(jnp.int32, (c, c), 1)
        dlf = _tri_dot(jnp.where(s >= r, 1.0, 0.0).astype(BF16), db)

        dqs = dqt * p["e_q"] + dq0 * p["e_0"]
        dq_ref[:, sl] = (dqs * p["dqs"]).astype(BF16)
        dk = dkt * p["e_k"] + dkc * p["e_c"]
        sz, lb = p["sz"], p["lb"]
        common = dlf / p["f"] - dk
        dz_ref[:, sl] = ((1.0 - lb) * sz * (1.0 - sz) * common).astype(BF16)
        dlb_ref[:, sl] += jnp.sum((1.0 - sz) * common, axis=0, keepdims=True)

    def body(*refs):
        dng_ref, dlb_ref, dst = refs[-3:]

        @pl.when(pl.program_id(0) == 0)
        def _():
            dst[...] = jnp.zeros_like(dst)
            dng_ref[...] = jnp.zeros_like(dng_ref)
            dlb_ref[...] = jnp.zeros_like(dlb_ref)

        for h in range(HGRN_HEADS):
            head(h, *refs)

    def wide(off):
        return pl.BlockSpec((c, HGRN_WIDTH), lambda i: (last - i, off))

    vec = pl.BlockSpec((1, HGRN_WIDTH), lambda i: (0, 0))
    act = jax.ShapeDtypeStruct((t, HGRN_WIDTH), BF16)
    vsh = jax.ShapeDtypeStruct((1, HGRN_WIDTH), F32)
    return pl.pallas_call(
        body, name="hgrn_bwd", grid=(n_chunks,),
        in_specs=[wide(1), wide(2), wide(3), wide(4), wide(0),
                  pl.BlockSpec((HGRN_HEADS, None, dh, dh), lambda i: (0, last - i, 0, 0)),
                  wide(0), pl.BlockSpec((2, HGRN_WIDTH), lambda i: (0, 0)), vec],
        out_specs=[wide(0), wide(0), wide(0), wide(0), vec, vec],
        out_shape=[act, act, act, act, vsh, vsh],
        scratch_shapes=[pltpu.VMEM((HGRN_HEADS, dh, dh), F32)],
        compiler_params=_params(("arbitrary",)),
    )(proj, proj, proj, proj, o_raw, s0s, doh, logits, ng)


def _lb_bwd(logits, dlb):
    def body(lg_ref, d_ref, o_ref):
        lg = lg_ref[...]
        lb = _sigmoid(lg[0:1, :] - lg[1:2, :])
        g = d_ref[...] * lb * (1.0 - lb)
        o_ref[0:1, :] = g
        o_ref[1:2, :] = -g

    return pl.pallas_call(body, name="hgrn_lb_bwd", out_shape=jax.ShapeDtypeStruct(logits.shape, F32),
                          compiler_params=_params())(logits, dlb)


MERGE_TC = 1024
GS_BLOCK = (S5_WIDTH + 4 * HGRN_WIDTH) // MERGE_TC
GH_BLOCK = GS_BLOCK + D_MODEL // MERGE_TC


def _merge_fwd(proj, ys, yh):
    t = proj.shape[0]
    tr = _pick(t, 256, SUBLANE)

    def body(gs_ref, gh_ref, ys_ref, yh_ref, m_ref):
        m_ref[...] = (_sigmoid(gs_ref[...]) * ys_ref[...] + _sigmoid(gh_ref[...]) * yh_ref[...]).astype(BF16)

    blk = pl.BlockSpec((tr, MERGE_TC), lambda i, j: (i, j))
    return pl.pallas_call(
        body, name="merge_fwd", grid=(t // tr, D_MODEL // MERGE_TC),
        in_specs=[pl.BlockSpec((tr, MERGE_TC), lambda i, j: (i, GS_BLOCK + j)),
                  pl.BlockSpec((tr, MERGE_TC), lambda i, j: (i, GH_BLOCK + j)), blk, blk],
        out_specs=blk, out_shape=jax.ShapeDtypeStruct((t, D_MODEL), BF16),
        compiler_params=_params(("parallel", "parallel")),
    )(proj, proj, ys, yh)


def _merge_bwd(proj, ys, yh, dm, after=None):
    t = proj.shape[0]
    tr = _pick(t, 256, SUBLANE)

    def body(gs_ref, gh_ref, ys_ref, yh_ref, dm_ref, dys_ref, dyh_ref, dgs_ref, dgh_ref):
        dmv = dm_ref[...]
        ss, sh = _sigmoid(gs_ref[...]), _sigmoid(gh_ref[...])
        dys_ref[...] = (dmv * ss).astype(BF16)
        dyh_ref[...] = (dmv * sh).astype(BF16)
        dgs_ref[...] = (dmv * ys_ref[...] * ss * (1.0 - ss)).astype(BF16)
        dgh_ref[...] = (dmv * yh_ref[...] * sh * (1.0 - sh)).astype(BF16)

    blk = pl.BlockSpec((tr, MERGE_TC), lambda i, j: (i, j))
    sh16 = jax.ShapeDtypeStruct((t, D_MODEL), BF16)
    in_specs = [pl.BlockSpec((tr, MERGE_TC), lambda i, j: (i, GS_BLOCK + j)),
                pl.BlockSpec((tr, MERGE_TC), lambda i, j: (i, GH_BLOCK + j)), blk, blk, blk]
    body, in_specs, args = _ordered(body, in_specs, [proj, proj, ys, yh, dm], after)
    return pl.pallas_call(
        body, name="merge_bwd", grid=(t // tr, D_MODEL // MERGE_TC),
        in_specs=in_specs,
        out_specs=[blk, blk, blk, blk], out_shape=[sh16, sh16, sh16, sh16],
        compiler_params=_params(("parallel", "parallel")),
    )(*args)


FFN_TC = 128
FFN_ROWS = 512
HALO = SUBLANE


def _rows_with_halo(ref, r0, nrows, t, before, after):
    lo = r0 - before if r0 - before >= 0 else r0
    hi = r0 + nrows + after if r0 + nrows + after <= t else r0 + nrows
    parts = []
    if lo == r0 and before:
        parts.append(jnp.zeros((before, ref.shape[1]), F32))
    parts.append(ref[lo:hi, :])
    if hi == r0 + nrows and after:
        parts.append(jnp.zeros((after, ref.shape[1]), F32))
    return parts[0] if len(parts) == 1 else jnp.concatenate(parts, axis=0)


def _conv3(ext, w, b, nrows, off):
    n = ext.shape[0]
    x0 = ext[off:off + nrows, :]
    x1 = pltpu.roll(ext, 1, 0)[off:off + nrows, :]
    x2 = pltpu.roll(ext, 2, 0)[off:off + nrows, :]
    return b + w[0:1, :] * x2 + w[1:2, :] * x1 + w[2:3, :] * x0, (x0, x1, x2)


def _ffn_act_fwd(up, cw, cb):
    t = up.shape[0]
    rows = _pick(t, FFN_ROWS, SUBLANE)
    nvb = D_FF // FFN_TC

    def body(ug_ref, uv_ref, wg_ref, wv_ref, bg_ref, bv_ref, act_ref):
        wg, wv, bg, bv = wg_ref[...], wv_ref[...], bg_ref[...], bv_ref[...]
        for r0 in range(0, t, rows):
            cg, _ = _conv3(_rows_with_halo(ug_ref, r0, rows, t, HALO, 0), wg, bg, rows, HALO)
            cv, _ = _conv3(_rows_with_halo(uv_ref, r0, rows, t, HALO, 0), wv, bv, rows, HALO)
            act_ref[r0:r0 + rows, :] = (_silu_and_grad(cg)[0] * cv).astype(BF16)

    def colblk(nrow, off):
        return pl.BlockSpec((nrow, FFN_TC), lambda j: (0, off + j))

    return pl.pallas_call(
        body, name="ffn_act_fwd", grid=(nvb,),
        in_specs=[colblk(t, 0), colblk(t, nvb), colblk(3, 0), colblk(3, nvb), colblk(1, 0), colblk(1, nvb)],
        out_specs=colblk(t, 0), out_shape=jax.ShapeDtypeStruct((t, D_FF), BF16),
        compiler_params=_params(("parallel",)),
    )(up, up, cw, cw, cb, cb)


def _ffn_act_bwd(up, dact, cw, cb, after=None):
    t = up.shape[0]
    rows = _pick(t, FFN_ROWS, SUBLANE)
    nvb = D_FF // FFN_TC

    def body(ug_ref, uv_ref, da_ref, wg_ref, wv_ref, bg_ref, bv_ref,
             dug_ref, duv_ref, dwg_ref, dwv_ref, dbg_ref, dbv_ref):
        wg, wv, bg, bv = wg_ref[...], wv_ref[...], bg_ref[...], bv_ref[...]
        ext = rows + HALO
        acc_g = [jnp.zeros((1, FFN_TC), F32) for _ in range(4)]
        acc_v = [jnp.zeros((1, FFN_TC), F32) for _ in range(4)]
        for r0 in range(0, t, rows):
            cg, xg = _conv3(_rows_with_halo(ug_ref, r0, rows, t, HALO, HALO), wg, bg, ext, HALO)
            cv, xv = _conv3(_rows_with_halo(uv_ref, r0, rows, t, HALO, HALO), wv, bv, ext, HALO)
            dav = _rows_with_halo(da_ref, r0, rows, t, 0, HALO)
            sg, dsg = _silu_and_grad(cg)
            for dconv, xs, w, acc, out in ((dav * cv * dsg, xg, wg, acc_g, dug_ref), (dav * sg, xv, wv, acc_v, duv_ref)):
                d0 = dconv[0:rows, :]
                d1 = pltpu.roll(dconv, ext - 1, 0)[0:rows, :]
                d2 = pltpu.roll(dconv, ext - 2, 0)[0:rows, :]
                out[r0:r0 + rows, :] = (w[2:3, :] * d0 + w[1:2, :] * d1 + w[0:1, :] * d2).astype(BF16)
                x0, x1, x2 = xs
                acc[0] = acc[0] + jnp.sum(d0 * x2[0:rows, :], axis=0, keepdims=True)
                acc[1] = acc[1] + jnp.sum(d0 * x1[0:rows, :], axis=0, keepdims=True)
                acc[2] = acc[2] + jnp.sum(d0 * x0[0:rows, :], axis=0, keepdims=True)
                acc[3] = acc[3] + jnp.sum(d0, axis=0, keepdims=True)
        for acc, dw_ref, db_ref in ((acc_g, dwg_ref, dbg_ref), (acc_v, dwv_ref, dbv_ref)):
            dw_ref[0:1, :] = acc[0]
            dw_ref[1:2, :] = acc[1]
            dw_ref[2:3, :] = acc[2]
            db_ref[...] = acc[3]

    def colblk(nrow, off):
        return pl.BlockSpec((nrow, FFN_TC), lambda j: (0, off + j))

    in_specs = [colblk(t, 0), colblk(t, nvb), colblk(t, 0), colblk(3, 0), colblk(3, nvb), colblk(1, 0), colblk(1, nvb)]
    body, in_specs, args = _ordered(body, in_specs, [up, up, dact, cw, cw, cb, cb], after)
    return pl.pallas_call(
        body, name="ffn_act_bwd", grid=(nvb,),
        in_specs=in_specs,
        out_specs=[colblk(t, 0), colblk(t, 0), colblk(3, 0), colblk(3, 0), colblk(1, 0), colblk(1, 0)],
        out_shape=[jax.ShapeDtypeStruct((t, D_FF), BF16), jax.ShapeDtypeStruct((t, D_FF), BF16),
                   jax.ShapeDtypeStruct((3, D_FF), F32), jax.ShapeDtypeStruct((3, D_FF), F32),
                   jax.ShapeDtypeStruct((1, D_FF), F32), jax.ShapeDtypeStruct((1, D_FF), F32)],
        compiler_params=_params(("parallel",)),
    )(*args)


def _all_gather(shards, name):
    nw = len(shards)

    def body(*refs):
        x_refs, out_refs = refs[:nw], refs[nw:2 * nw]
        send_sems, recv_sems, local_sems = refs[2 * nw:]
        x, y, c = lax.axis_index("x"), lax.axis_index("y"), lax.axis_index("c")
        me, sibling = (x, y, c), (x, y, 1 - c)
        chips = [(1 - x, y), (x, 1 - y), (1 - x, 1 - y)]

        def copy(w, k, block, to, src=None):
            slot = out_refs[w].at[4 * block[0] + 2 * block[1] + block[2]]
            return pltpu.make_async_remote_copy(
                src_ref=slot if src is None else src, dst_ref=slot,
                send_sem=send_sems.at[w, k], recv_sem=recv_sems.at[w, k],
                device_id=to, device_id_type=MESH)

        mine, first, passed = [], [], []
        for w in range(nw):
            cp = pltpu.make_async_copy(x_refs[w], out_refs[w].at[4 * x + 2 * y + c], local_sems.at[w])
            cp.start()
            mine.append(cp)
            first.append(copy(w, 0, me, sibling, src=x_refs[w]))
            first += [copy(w, 1 + j, me, (*chip, c), src=x_refs[w]) for j, chip in enumerate(chips)]
        for cp in first:
            cp.start()
        for w in range(nw):
            for j, chip in enumerate(chips):
                copy(w, 1 + j, (*chip, c), me).wait_recv()
                fwd = copy(w, 4 + j, (*chip, c), sibling)
                fwd.start()
                passed.append(fwd)
        for w in range(nw):
            copy(w, 0, sibling, me).wait_recv()
            for j, chip in enumerate(chips):
                copy(w, 4 + j, (*chip, 1 - c), me).wait_recv()
        for cp in first + passed:
            cp.wait_send()
        for cp in mine:
            cp.wait()

    anyspec = pl.BlockSpec(memory_space=pl.ANY)
    return pl.pallas_call(
        body, name=name,
        in_specs=[anyspec] * nw, out_specs=[anyspec] * nw,
        out_shape=[jax.ShapeDtypeStruct((N_DEV,) + s.shape, s.dtype) for s in shards],
        scratch_shapes=[pltpu.SemaphoreType.DMA((nw, 7)), pltpu.SemaphoreType.DMA((nw, 7)),
                        pltpu.SemaphoreType.DMA((nw,))],
    )(*shards)


HBM_SPEC = pl.BlockSpec(memory_space=pltpu.HBM)
SEM_SPEC = pl.BlockSpec(memory_space=pltpu.SEMAPHORE)
ANY_SPEC = pl.BlockSpec(memory_space=pl.ANY)
DATAFLOW = pltpu.SideEffectType.DATAFLOW_SIDE_EFFECTING


def _my_index():
    return 4 * lax.axis_index("x") + 2 * lax.axis_index("y") + lax.axis_index("c")


def _peers():
    x, y, c = lax.axis_index("x"), lax.axis_index("y"), lax.axis_index("c")
    peers = []
    for k in range(1, N_DEV):
        px = 1 - x if k & 4 else x
        py = 1 - y if k & 2 else y
        pc = 1 - c if k & 1 else c
        peers.append((k, (px, py, pc), 4 * px + 2 * py + pc))
    return peers


def _split_copy(src_ref, land_ref, send_sems, recv_sems, w, k, peer, slot, scatter, outgoing):
    return pltpu.make_async_remote_copy(
        src_ref=src_ref.at[slot] if scatter else src_ref,
        dst_ref=land_ref.at[_my_index() if outgoing else slot],
        send_sem=send_sems.at[w * (N_DEV - 1) + k - 1], recv_sem=recv_sems.at[w * (N_DEV - 1) + k - 1],
        device_id=peer, device_id_type=MESH)


def _exchange_start(srcs, scatter, after, name):
    nw = len(srcs)
    me = _my_index()
    lands = []
    for s in srcs:
        own = lax.dynamic_index_in_dim(s, me, 0, keepdims=True) if scatter else s[None]
        shape = s.shape if scatter else (N_DEV,) + s.shape
        lands.append(lax.dynamic_update_slice_in_dim(lax.empty(shape, s.dtype), own, me, 0))

    afters = [] if after is None else [after]

    def body(*refs):
        s_refs, l_refs = refs[:nw], refs[nw:2 * nw]
        send_sems, recv_sems = refs[2 * nw + len(afters)], refs[2 * nw + len(afters) + 1]
        token = refs[-1]
        for w in range(nw):
            for k, peer, slot in _peers():
                _split_copy(s_refs[w], l_refs[w], send_sems, recv_sems, w, k, peer, slot, scatter, True).start()
        token[...] = jnp.zeros_like(token)

    sems = pltpu.SemaphoreType.DMA((nw * (N_DEV - 1),))
    outs = pl.pallas_call(
        body, name=name,
        out_shape=(sems, sems, *[pltpu.HBM(a.shape, a.dtype) for a in (*srcs, *lands)],
                   jax.ShapeDtypeStruct((SUBLANE, LANE), F32)),
        in_specs=[HBM_SPEC] * (2 * nw) + [ANY_SPEC] * len(afters),
        out_specs=(SEM_SPEC, SEM_SPEC, *[HBM_SPEC] * (2 * nw), pl.BlockSpec(memory_space=pltpu.VMEM)),
        input_output_aliases={i: 2 + i for i in range(2 * nw)},
        compiler_params=pltpu.CompilerParams(has_side_effects=DATAFLOW),
    )(*[pltpu.with_memory_space_constraint(a, pltpu.HBM) for a in (*srcs, *lands)], *afters)
    return dict(sems=outs[:2], srcs=outs[2:2 + nw], lands=outs[2 + nw:2 + 2 * nw], token=outs[-1], scatter=scatter)


def _exchange_wait(handle, afters, name):
    srcs, lands, scatter = handle["srcs"], handle["lands"], handle["scatter"]
    nw = len(srcs)

    def body(*refs):
        s_refs, l_refs = refs[:nw], refs[nw:2 * nw]
        send_sems, recv_sems = refs[2 * nw], refs[2 * nw + 1]
        for w in range(nw):
            for k, peer, slot in _peers():
                cp = _split_copy(s_refs[w], l_refs[w], send_sems, recv_sems, w, k, peer, slot, scatter, False)
                cp.wait_send()
                cp.wait_recv()

    outs = pl.pallas_call(
        body, name=name,
        out_shape=tuple(pltpu.HBM(a.shape, a.dtype) for a in (*srcs, *lands)),
        in_specs=[HBM_SPEC] * (2 * nw) + [SEM_SPEC, SEM_SPEC] + [ANY_SPEC] * len(afters),
        out_specs=tuple([HBM_SPEC] * (2 * nw)),
        input_output_aliases={i: i for i in range(2 * nw)},
        compiler_params=pltpu.CompilerParams(has_side_effects=DATAFLOW),
    )(*srcs, *lands, *handle["sems"], *afters)
    return list(outs[nw:])


def _adamw(w, g, m, v):
    m = ADAM_B1 * m + (1.0 - ADAM_B1) * g
    v = ADAM_B2 * v + (1.0 - ADAM_B2) * (g * g)
    m_hat = m / (1.0 - ADAM_B1 ** ADAM_STEP)
    v_hat = v / (1.0 - ADAM_B2 ** ADAM_STEP)
    delta = -ADAM_LR * (m_hat / (jnp.sqrt(v_hat) + ADAM_EPS) + ADAM_WD * w)
    return delta, m, v


def _sum_adam(parts, w, m, v, name):
    _, r, c = parts.shape
    tr = _pick(r, 128, 16)

    def body(p_ref, w_ref, m_ref, v_ref, g_ref, d_ref, mo_ref, vo_ref):
        g = p_ref[0].astype(F32)
        for s in range(1, N_DEV):
            g = g + p_ref[s].astype(F32)
        g_ref[...] = g
        d_ref[...], mo_ref[...], vo_ref[...] = _adamw(w_ref[...], g, m_ref[...], v_ref[...])

    row = pl.BlockSpec((tr, c), lambda i: (i, 0))
    sh = jax.ShapeDtypeStruct((r, c), F32)
    return pl.pallas_call(
        body, name=name, grid=(r // tr,),
        in_specs=[pl.BlockSpec((N_DEV, tr, c), lambda i: (0, i, 0)), row, row, row],
        out_specs=[row, row, row, row], out_shape=[sh, sh, sh, sh],
        compiler_params=_params(("parallel",)),
    )(parts, w, m, v)


def _sum_slots(parts, name):
    _, r, c = parts.shape
    tr = _pick(r, 512, SUBLANE)

    def body(p_ref, o_ref):
        g = p_ref[0]
        for s in range(1, N_DEV):
            g = g + p_ref[s]
        o_ref[...] = g

    return pl.pallas_call(
        body, name=name, grid=(r // tr,),
        in_specs=[pl.BlockSpec((N_DEV, tr, c), lambda i: (0, i, 0))],
        out_specs=pl.BlockSpec((tr, c), lambda i: (i, 0)), out_shape=jax.ShapeDtypeStruct((r, c), F32),
        compiler_params=_params(("parallel",)),
    )(parts)


def _adam_rows(g, w, m, v, name):
    r, c = g.shape
    tr = _pick(r, 512, SUBLANE)

    def body(g_ref, w_ref, m_ref, v_ref, d_ref, mo_ref, vo_ref):
        d_ref[...], mo_ref[...], vo_ref[...] = _adamw(w_ref[...], g_ref[...], m_ref[...], v_ref[...])

    row = pl.BlockSpec((tr, c), lambda i: (i, 0))
    sh = jax.ShapeDtypeStruct((r, c), F32)
    return pl.pallas_call(body, name=name, grid=(r // tr,), in_specs=[row] * 4, out_specs=[row] * 3,
                          out_shape=[sh, sh, sh], compiler_params=_params(("parallel",)))(g, w, m, v)


def _pack(arrays):
    flat = jnp.concatenate([a.reshape(-1).astype(F32) for a in arrays])
    pad = (-flat.shape[0]) % (SUBLANE * LANE)
    return jnp.pad(flat, (0, pad)).reshape(-1, LANE)


def _unpack(packed, shapes):
    flat = packed.reshape(-1)
    out, off = [], 0
    for s in shapes:
        n = math.prod(s)
        out.append(flat[off:off + n].reshape(s))
        off += n
    return out


def _block_diag(t):
    eye = jnp.eye(S5_SUPER, dtype=bool)
    bd = jnp.where(eye[None, :, None, :, None], t[:, :, :, None, :], 0.0)
    return bd.reshape(S5_SUPER, S5_SUPER * t.shape[2], S5_SUPER * t.shape[3])


def _diag_blocks(dense, a, b):
    x = dense.reshape(S5_SUPER, S5_SUPER, a, S5_SUPER, b)
    return jnp.moveaxis(jnp.diagonal(x, axis1=1, axis2=3), -1, 1)


def _s5_layouts(b_re, b_im, c_re, c_im, d):
    g2 = (S5_GROUPS // S5_SUPER, S5_SUPER)
    bt = lambda b: _block_diag(b.reshape(*g2, S5_STATE, S5_GROUP).transpose(0, 1, 3, 2))
    ct = lambda c: _block_diag(c.reshape(*g2, S5_GROUP, S5_STATE).transpose(0, 1, 3, 2))
    bsg = jnp.concatenate([bt(b_re), bt(b_im)], axis=2).astype(BF16)
    ccat = jnp.concatenate([ct(c_re), -ct(c_im)], axis=1).astype(BF16)
    return bsg, ccat, d.reshape(1, S5_WIDTH)


def _s5_param_grads(gb, gc):
    n = S5_LANES
    gb_re = _diag_blocks(gb[:, :, 0:n], S5_GROUP, S5_STATE).transpose(0, 1, 3, 2).reshape(S5_GROUPS, S5_STATE, S5_GROUP)
    gb_im = _diag_blocks(gb[:, :, n:2 * n], S5_GROUP, S5_STATE).transpose(0, 1, 3, 2).reshape(S5_GROUPS, S5_STATE, S5_GROUP)
    gc_re = _diag_blocks(gc[:, 0:n, :], S5_STATE, S5_GROUP).transpose(0, 1, 3, 2).reshape(S5_GROUPS, S5_GROUP, S5_STATE)
    gc_im = -_diag_blocks(gc[:, n:2 * n, :], S5_STATE, S5_GROUP).transpose(0, 1, 3, 2).reshape(S5_GROUPS, S5_GROUP, S5_STATE)
    return gb_re, gb_im, gc_re, gc_im


def _local_step(x, target, weight, emit, small, after=None):
    sp = small
    a_re, a_im = sp["s5_a_re"], sp["s5_a_im"]
    ldt = sp["s5_log_dt"].reshape(S5_GROUPS, 1)

    h1 = _rms_fwd(x, sp["ln_mix_g"], "rms_mix", after=after)
    w_in = weight("w_in", h1)
    proj = _mm_nn(h1, w_in, "mm_in")
    disc = _s5_param_fwd(a_re, a_im, ldt)
    bsg, ccat, d_row = _s5_layouts(sp["s5_b_re"], sp["s5_b_im"], sp["s5_c_re"], sp["s5_c_im"], sp["s5_d"])
    abar_t, coef_t = _s5_to_tile(disc[0], disc[1]), _s5_to_tile(disc[2], disc[3])
    y, sb = _s5_fwd(proj, bsg, ccat, d_row, abar_t, coef_t)
    z16 = _gelu_fwd(y)
    w_glu = weight("s5_w_glu", z16)
    gl = _mm_nn(z16, w_glu, "mm_glu")
    z2 = _glu_fwd(y, gl, sp["s5_b_glu"])
    w_ps = weight("w_proj_s5", z2)
    ys = _mm_nn(z2, w_ps, "mm_proj_s5")
    o_raw, oh, s0s = _hgrn_fwd(proj, sp["hgrn_lb_logits"], sp["hgrn_norm_g"])
    w_ph = weight("w_proj_hgrn", oh)
    yh = _mm_nn(oh, w_ph, "mm_proj_hgrn")
    merged = _merge_fwd(proj, ys, yh)
    w_out = weight("w_out", merged)
    x1 = _mm_nn(merged, w_out, "mm_out", res=x)
    h2 = _rms_fwd(x1, sp["ln_ffn_g"], "rms_ffn")
    w_up = weight("w_up", h2)
    up = _mm_nn(h2, w_up, "mm_up")
    act = _ffn_act_fwd(up, sp["conv_w"], sp["conv_b"])
    w_down = weight("w_down", act)
    x2 = _mm_nn(act, w_down, "mm_down", res=x1)
    dx2, dx2_16, g_ln_final, loss = _loss_head(x2, sp["ln_final_g"], target)

    dact = _mm_nt(dx2_16, w_down, "mm_down_dx")
    tok = emit("w_down", _mm_tn(act, dx2_16, 1, "mm_down_dw"))
    dup_g, dup_v, dcw_g, dcw_v, dcb_g, dcb_v = _ffn_act_bwd(up, dact, sp["conv_w"], sp["conv_b"], after=tok)
    dup = jnp.concatenate([dup_g, dup_v], axis=1)
    g_conv_w = jnp.concatenate([dcw_g, dcw_v], axis=1)
    g_conv_b = jnp.concatenate([dcb_g, dcb_v], axis=1)
    dh2 = _mm_nt(dup, w_up, "mm_up_dx")
    tok = emit("w_up", _mm_tn(h2, dup, N_DEV, "mm_up_dw"))
    dx1, dx1_16, g_ln_ffn = _rms_bwd(x1, sp["ln_ffn_g"], dh2, dx2, "rms_ffn_bwd", True, after=tok)

    dmerged = _mm_nt(dx1_16, w_out, "mm_out_dx")
    tok = emit("w_out", _mm_tn(merged, dx1_16, 1, "mm_out_dw"))
    dys, dyh, dgs, dgh = _merge_bwd(proj, ys, yh, dmerged, after=tok)
    doh = _mm_nt(dyh, w_ph, "mm_proj_hgrn_dx")
    tok = emit("w_proj_hgrn", _mm_tn(oh, dyh, N_DEV, "mm_proj_hgrn_dw"))
    dz2 = _mm_nt(dys, w_ps, "mm_proj_s5_dx", after=tok)
    tok = emit("w_proj_s5", _mm_tn(z2, dys, N_DEV, "mm_proj_s5_dw"))
    dgl, dza, g_b_glu = _glu_bwd(y, gl, sp["s5_b_glu"], dz2, after=tok)
    dzb = _mm_nt(dgl, w_glu, "mm_glu_dx")
    tok = emit("s5_w_glu", _mm_tn(z16, dgl, 1, "mm_glu_dw"))
    dy = _gelu_bwd(y, dza, dzb, after=tok)
    du, gb, gc, gd, g_abar_t, g_coef_t = _s5_bwd(proj, dy, sb, bsg, ccat, d_row, abar_t, coef_t)
    g_a_re, g_a_im, g_ldt = _s5_param_bwd(a_re, a_im, ldt, [*_s5_from_tile(g_abar_t), *_s5_from_tile(g_coef_t)])
    g_b_re, g_b_im, g_c_re, g_c_im = _s5_param_grads(gb, gc)
    dq, dz, dv, dg, g_norm, dlb = _hgrn_bwd(proj, o_raw, s0s, doh, sp["hgrn_lb_logits"], sp["hgrn_norm_g"])
    g_logits = _lb_bwd(sp["hgrn_lb_logits"], dlb)

    small_g = dict(s5_a_re=g_a_re, s5_a_im=g_a_im, s5_log_dt=g_ldt.reshape(1, S5_GROUPS),
                   s5_b_re=g_b_re, s5_b_im=g_b_im, s5_c_re=g_c_re, s5_c_im=g_c_im,
                   s5_d=gd.reshape(S5_GROUPS, S5_GROUP), s5_b_glu=g_b_glu, hgrn_lb_logits=g_logits,
                   hgrn_norm_g=g_norm, ln_ffn_g=g_ln_ffn, conv_w=g_conv_w, conv_b=g_conv_b, ln_final_g=g_ln_final,
                   loss=loss[0, 0:1])
    tok_small = emit("small", small_g)

    dproj = jnp.concatenate([du, dq, dz, dv, dg, dgs, dgh], axis=1)
    tok = emit("w_in", _mm_tn(h1, dproj, N_DEV, "mm_in_dw", after=tok_small))
    dh1 = _mm_nt(dproj, w_in, "mm_in_dx")
    grad_x, g_ln_mix = _rms_bwd(x, sp["ln_mix_g"], dh1, dx1, "rms_mix_bwd", False, after=tok)
    return grad_x, g_ln_mix


BIG = ("w_in", "s5_w_glu", "w_proj_s5", "w_proj_hgrn", "w_out", "w_up", "w_down")
COL_SHARDED = ("w_in", "w_proj_s5", "w_proj_hgrn", "w_up")
SMALL = ("ln_mix_g", "s5_a_re", "s5_a_im", "s5_log_dt", "s5_b_re", "s5_b_im", "s5_c_re", "s5_c_im", "s5_d",
         "s5_b_glu", "hgrn_lb_logits", "hgrn_norm_g", "ln_ffn_g", "conv_b", "ln_final_g")
WEIGHTS = ("ln_mix_g", "w_in", "s5_a_re", "s5_a_im", "s5_log_dt", "s5_b_re", "s5_b_im", "s5_c_re", "s5_c_im", "s5_d",
           "s5_w_glu", "s5_b_glu", "w_proj_s5", "hgrn_lb_logits", "hgrn_norm_g", "w_proj_hgrn", "w_out", "ln_ffn_g",
           "w_up", "conv_w", "conv_b", "w_down", "ln_final_g")


def kernel(x, ln_mix_g, w_in, s5_a_re, s5_a_im, s5_log_dt, s5_b_re, s5_b_im, s5_c_re, s5_c_im, s5_d, s5_w_glu, s5_b_glu, w_proj_s5, hgrn_lb_logits, hgrn_norm_g, w_proj_hgrn, w_out, ln_ffn_g, w_up, conv_w, conv_b, w_down, ln_final_g, loss_target, m_ln_mix_g, m_w_in, m_s5_a_re, m_s5_a_im, m_s5_log_dt, m_s5_b_re, m_s5_b_im, m_s5_c_re, m_s5_c_im, m_s5_d, m_s5_w_glu, m_s5_b_glu, m_w_proj_s5, m_hgrn_lb_logits, m_hgrn_norm_g, m_w_proj_hgrn, m_w_out, m_ln_ffn_g, m_w_up, m_conv_w, m_conv_b, m_w_down, m_ln_final_g, v_ln_mix_g, v_w_in, v_s5_a_re, v_s5_a_im, v_s5_log_dt, v_s5_b_re, v_s5_b_im, v_s5_c_re, v_s5_c_im, v_s5_d, v_s5_w_glu, v_s5_b_glu, v_w_proj_s5, v_hgrn_lb_logits, v_hgrn_norm_g, v_w_proj_hgrn, v_w_out, v_ln_ffn_g, v_w_up, v_conv_w, v_conv_b, v_w_down, v_ln_final_g):
    given = dict(locals())
    w = {n: given[n] for n in WEIGHTS}
    mom = {n: given["m_" + n] for n in WEIGHTS}
    var = {n: given["v_" + n] for n in WEIGHTS}

    shard16 = {n: w[n][0].astype(BF16) for n in BIG}
    w_in_all, conv_w_all = _all_gather([shard16["w_in"], conv_w[0]], "gather_first")
    gather_groups = (("s5_w_glu", "w_proj_s5", "w_proj_hgrn", "w_out"), ("w_up",), ("w_down",))
    pending, token = {}, w_in_all
    for i, group in enumerate(gather_groups):
        handle = _exchange_start([shard16[n] for n in group], False, token, f"gather_start_{i}")
        token = handle["token"]
        for n in group:
            pending[n] = (group, handle, f"gather_wait_{i}")
    ready = {"w_in": w_in_all}

    def weight(name, after):
        if name not in ready:
            group, handle, wait_name = pending[name]
            for n, g in zip(group, _exchange_wait(handle, [after], wait_name)):
                ready[n] = g
        g = ready[name]
        return g if name in COL_SHARDED else g.reshape(1, N_DEV * g.shape[1], g.shape[2])

    scatter_groups = (("w_down",), ("w_up",), ("w_out", "w_proj_hgrn", "w_proj_s5", "s5_w_glu"), ("w_in",))
    emitted, scatters = {}, []
    packed_names = SMALL[1:] + ("conv_w", "loss")

    def emit(name, grad):
        if name == "small":
            emitted[name] = ([grad[n].shape for n in packed_names],
                             _exchange_start([_pack([grad[n] for n in packed_names])], False, None, "small_start"))
            return emitted[name][1]["token"]
        emitted[name] = grad if name in COL_SHARDED else grad.reshape(N_DEV, -1, grad.shape[2])
        group = scatter_groups[len(scatters)]
        if not all(n in emitted for n in group):
            return None
        handle = _exchange_start([emitted[n] for n in group], True, None, f"scatter_start_{len(scatters)}")
        scatters.append((group, handle))
        return handle["token"]

    small = dict(ln_mix_g=ln_mix_g, s5_a_re=s5_a_re[0], s5_a_im=s5_a_im[0], s5_log_dt=s5_log_dt,
                 s5_b_re=s5_b_re[0], s5_b_im=s5_b_im[0], s5_c_re=s5_c_re[0], s5_c_im=s5_c_im[0], s5_d=s5_d[0],
                 s5_b_glu=s5_b_glu, hgrn_lb_logits=hgrn_lb_logits, hgrn_norm_g=hgrn_norm_g, ln_ffn_g=ln_ffn_g,
                 conv_w=conv_w_all.transpose(1, 0, 2).reshape(3, 2 * D_FF), conv_b=conv_b,
                 ln_final_g=ln_final_g.reshape(1, D_MODEL))
    grad_x, g_ln_mix = _local_step(x[0], loss_target[0], weight, emit, small, after=token)

    shapes, handle = emitted["small"]
    total = _sum_slots(_exchange_wait(handle, [grad_x], "small_wait")[0], "sum_small")
    summed = dict(zip(packed_names, _unpack(total, shapes)))
    mix_all = _all_gather([g_ln_mix.reshape(-1, LANE)], "gather_ln_mix")[0]
    summed["ln_mix_g"] = _sum_slots(mix_all, "sum_ln_mix").reshape(1, D_MODEL)

    grads, delta, new_m, new_v = {}, {}, {}, {}
    afters = [grad_x, total]
    for i, (group, handle) in enumerate(scatters):
        for n, r in zip(group, _exchange_wait(handle, afters, f"scatter_wait_{i}")):
            g, d, m2, v2 = _sum_adam(r, w[n][0], mom[n][0], var[n][0], "adam_" + n)
            grads[n], delta[n], new_m[n], new_v[n] = g[None], d[None], m2[None], v2[None]
        if i == len(scatters) - 2:
            afters = [delta[n] for g2, _ in scatters[:-1] for n in g2]

    packed_small = SMALL[1:]
    pw = _pack([w[n] for n in packed_small])
    d_s, m_s, v_s = _adam_rows(_pack([summed[n] for n in packed_small]), pw, _pack([mom[n] for n in packed_small]),
                               _pack([var[n] for n in packed_small]), "adam_small")
    wshapes = [w[n].shape for n in packed_small]
    for n, d, m2, v2 in zip(packed_small, _unpack(d_s, wshapes), _unpack(m_s, wshapes), _unpack(v_s, wshapes)):
        grads[n], delta[n], new_m[n], new_v[n] = summed[n].reshape(w[n].shape), d, m2, v2
    grads["ln_mix_g"] = summed["ln_mix_g"]
    delta["ln_mix_g"], new_m["ln_mix_g"], new_v["ln_mix_g"] = _adam_rows(summed["ln_mix_g"], ln_mix_g, m_ln_mix_g,
                                                                         v_ln_mix_g, "adam_ln_mix")
    me = 4 * lax.axis_index("x") + 2 * lax.axis_index("y") + lax.axis_index("c")
    ncol = conv_w.shape[2]
    g_cw = lax.dynamic_slice_in_dim(summed["conv_w"], me * ncol, ncol, axis=1)
    d_cw, m_cw, v_cw = _adam_rows(g_cw, conv_w[0], m_conv_w[0], v_conv_w[0], "adam_conv_w")
    grads["conv_w"], delta["conv_w"], new_m["conv_w"], new_v["conv_w"] = g_cw[None], d_cw[None], m_cw[None], v_cw[None]

    return (summed["loss"].reshape(()), grad_x[None], *[grads[n] for n in WEIGHTS], *[delta[n] for n in WEIGHTS],
            *[new_m[n] for n in WEIGHTS], *[new_v[n] for n in WEIGHTS])
```

```python
import math

import jax
import jax.numpy as jnp
from jax import lax
from jax.experimental import pallas as pl
from jax.experimental.pallas import tpu as pltpu

F32 = jnp.float32
BF16 = jnp.bfloat16

N_DEV = 8
D_MODEL = 2048
S5_WIDTH = 1024
S5_GROUP = 16
S5_GROUPS = 64
S5_STATE = 64
S5_MAX_RE = -1e-4
S5_SUPER = 8
S5_LANES = S5_SUPER * S5_STATE
HGRN_WIDTH = 1024
HGRN_HEADS = 8
HGRN_DH = 128
HGRN_CHUNK = 64
D_FF = 5632
RMS_EPS = 1e-6
ADAM_LR = 0.001
ADAM_B1 = 0.9
ADAM_B2 = 0.999
ADAM_EPS = 1e-08
ADAM_WD = 0.01
ADAM_STEP = 10

LANE = 128
SUBLANE = 8
VMEM_LIMIT = 48 * 1024 * 1024
MESH = pl.DeviceIdType.MESH
GELU_C = math.sqrt(2.0 / math.pi)
GELU_A = 0.044715


def _params(sem=None):
    return pltpu.CompilerParams(dimension_semantics=sem, vmem_limit_bytes=VMEM_LIMIT)


def _pick(n, cap, unit=LANE):
    best = None
    for t in range(unit, min(n, cap) + 1, unit):
        if n % t == 0:
            best = t
    return best if best is not None else n


def _ordered(body, in_specs, args, after):
    if after is None:
        return body, list(in_specs), list(args)
    n_in = len(args)

    def ordered_body(*refs):
        return body(*refs[:n_in], *refs[n_in + 1:])

    return ordered_body, [*in_specs, pl.BlockSpec(memory_space=pl.ANY)], [*args, after]


def _sigmoid(x):
    return 0.5 * jnp.tanh(0.5 * x) + 0.5


def _silu_and_grad(x):
    s = _sigmoid(x)
    return x * s, s * (1.0 + x * (1.0 - s))


def _gelu_and_grad(y):
    inner = GELU_C * (y + GELU_A * y * y * y)
    th = jnp.tanh(inner)
    val = 0.5 * y * (1.0 + th)
    grad = 0.5 * (1.0 + th) + 0.5 * y * (1.0 - th * th) * GELU_C * (1.0 + 3.0 * GELU_A * y * y)
    return val, grad


def _dot(a, b):
    return jnp.dot(a, b, preferred_element_type=F32)


def _dot_nt(a, b):
    return lax.dot_general(a, b, (((1,), (1,)), ((), ())), preferred_element_type=F32)


def _dot_tn(a, b):
    return lax.dot_general(a, b, (((0,), (0,)), ((), ())), preferred_element_type=F32)


def _blocks_per_step(nb, ns, tn, cap=2048):
    if tn != ns:
        return 1
    best = 1
    for b in range(1, nb + 1):
        if nb % b == 0 and b * ns <= cap:
            best = b
    return best


def _mm_nn(a, w, name, res=None, out_dtype=F32):
    m, kdim = a.shape
    nb, _, ns = w.shape
    tm, tk, tn = _pick(m, 512), _pick(kdim, 2048), _pick(ns, 1536)
    npb, nk = ns // tn, kdim // tk
    bps = _blocks_per_step(nb, ns, tn)
    assert bps == 1 or nk == 1

    def body(*refs):
        a_ref, w_ref = refs[0], refs[1]
        r_ref = refs[2] if res is not None else None
        o_ref = refs[3] if res is not None else refs[2]

        def finish(r, cols):
            if res is not None:
                r = r + r_ref[:, cols]
            o_ref[:, cols] = r.astype(out_dtype)

        if nk == 1:
            for b in range(bps):
                finish(_dot(a_ref[...], w_ref[b]), slice(b * tn, (b + 1) * tn))
            return
        acc = refs[-1]
        k = pl.program_id(2)

        @pl.when(k == 0)
        def _():
            acc[...] = jnp.zeros_like(acc)

        acc[...] += _dot(a_ref[...], w_ref[0])

        @pl.when(k == nk - 1)
        def _():
            finish(acc[...], slice(0, tn))

    in_specs = [pl.BlockSpec((tm, tk), lambda j, i, k: (i, k)),
                pl.BlockSpec((bps, tk, tn), lambda j, i, k: (j // npb, k, j % npb))]
    args = [a, w]
    if res is not None:
        in_specs.append(pl.BlockSpec((tm, bps * tn), lambda j, i, k: (i, j)))
        args.append(res)
    return pl.pallas_call(
        body, name=name, grid=(nb * npb // bps, m // tm, nk),
        in_specs=in_specs, out_specs=pl.BlockSpec((tm, bps * tn), lambda j, i, k: (i, j)),
        out_shape=jax.ShapeDtypeStruct((m, nb * ns), out_dtype),
        scratch_shapes=[pltpu.VMEM((tm, tn), F32)] if nk > 1 else [],
        compiler_params=_params(("parallel", "parallel", "arbitrary")),
    )(*args)


def _mm_nt(a, w, name, out_dtype=F32, after=None):
    m, _ = a.shape
    nb, kdim, ns = w.shape
    tm, tko, tn = _pick(m, 1024), _pick(kdim, 1024), _pick(ns, 2048)
    npb = ns // tn
    bps = _blocks_per_step(nb, ns, tn)
    nred = nb * npb // bps

    def body(a_ref, w_ref, o_ref, *scratch):
        total = _dot_nt(a_ref[:, 0:tn], w_ref[0])
        for b in range(1, bps):
            total = total + _dot_nt(a_ref[:, b * tn:(b + 1) * tn], w_ref[b])
        if nred == 1:
            o_ref[...] = total.astype(out_dtype)
            return
        acc = scratch[0]
        n = pl.program_id(2)

        @pl.when(n == 0)
        def _():
            acc[...] = jnp.zeros_like(acc)

        acc[...] += total

        @pl.when(n == nred - 1)
        def _():
            o_ref[...] = acc[...].astype(out_dtype)

    in_specs = [pl.BlockSpec((tm, bps * tn), lambda i, j, n: (i, n)),
                pl.BlockSpec((bps, tko, tn), lambda i, j, n: (n // npb, j, n % npb))]
    body, in_specs, args = _ordered(body, in_specs, [a, w], after)
    return pl.pallas_call(
        body, name=name, grid=(m // tm, kdim // tko, nred),
        in_specs=in_specs,
        out_specs=pl.BlockSpec((tm, tko), lambda i, j, n: (i, j)),
        out_shape=jax.ShapeDtypeStruct((m, kdim), out_dtype),
        scratch_shapes=[pltpu.VMEM((tm, tko), F32)] if nred > 1 else [],
        compiler_params=_params(("parallel", "parallel", "arbitrary")),
    )(*args)


def _mm_tn(a, d, nb, name, out_dtype=BF16, after=None):
    m, kdim = a.shape
    ns = d.shape[1] // nb
    tm, tko, tn = _pick(m, 4096), _pick(kdim, 512), _pick(ns, 1536)
    npb, nm = ns // tn, m // tm

    def body(a_ref, d_ref, o_ref, *scratch):
        if nm == 1:
            o_ref[...] = _dot_tn(a_ref[...], d_ref[...]).astype(out_dtype)
            return
        acc = scratch[0]
        r = pl.program_id(2)

        @pl.when(r == 0)
        def _():
            acc[...] = jnp.zeros_like(acc)

        acc[...] += _dot_tn(a_ref[...], d_ref[...])

        @pl.when(r == nm - 1)
        def _():
            o_ref[...] = acc[...].astype(out_dtype)

    in_specs = [pl.BlockSpec((tm, tko), lambda j, i, r: (r, i)), pl.BlockSpec((tm, tn), lambda j, i, r: (r, j))]
    body, in_specs, args = _ordered(body, in_specs, [a, d], after)
    return pl.pallas_call(
        body, name=name, grid=(nb * npb, kdim // tko, nm),
        in_specs=in_specs,
        out_specs=pl.BlockSpec((None, tko, tn), lambda j, i, r: (j // npb, i, j % npb)),
        out_shape=jax.ShapeDtypeStruct((nb, kdim, ns), out_dtype),
        scratch_shapes=[pltpu.VMEM((tko, tn), F32)] if nm > 1 else [],
        compiler_params=_params(("parallel", "parallel", "arbitrary")),
    )(*args)


def _rms_fwd(x, g, name, after=None):
    t, d = x.shape
    tr = _pick(t, 256, SUBLANE)

    def body(x_ref, g_ref, h_ref):
        xv = x_ref[...]
        r = lax.rsqrt(jnp.mean(xv * xv, axis=-1, keepdims=True) + RMS_EPS)
        h_ref[...] = (xv * r * g_ref[...]).astype(BF16)

    in_specs = [pl.BlockSpec((tr, d), lambda i: (i, 0)), pl.BlockSpec((1, d), lambda i: (0, 0))]
    body, in_specs, args = _ordered(body, in_specs, [x, g], after)
    return pl.pallas_call(
        body, name=name, grid=(t // tr,),
        in_specs=in_specs,
        out_specs=pl.BlockSpec((tr, d), lambda i: (i, 0)),
        out_shape=jax.ShapeDtypeStruct((t, d), BF16),
        compiler_params=_params(("parallel",)),
    )(*args)


def _rms_bwd(x, g, dh, add, name, want_bf16, after=None):
    t, d = x.shape
    tr = _pick(t, 256, SUBLANE)

    def body(x_ref, g_ref, dh_ref, add_ref, *outs):
        if want_bf16:
            dx_ref, dxb_ref, dg_ref = outs
        else:
            dx_ref, dg_ref = outs
        i = pl.program_id(0)

        @pl.when(i == 0)
        def _():
            dg_ref[...] = jnp.zeros_like(dg_ref)

        xv, dhv = x_ref[...], dh_ref[...]
        r = lax.rsqrt(jnp.mean(xv * xv, axis=-1, keepdims=True) + RMS_EPS)
        xh = xv * r
        dg_ref[...] += jnp.sum(dhv * xh, axis=0, keepdims=True)
        dxh = dhv * g_ref[...]
        dx = add_ref[...] + r * (dxh - xh * jnp.mean(dxh * xh, axis=-1, keepdims=True))
        dx_ref[...] = dx
        if want_bf16:
            dxb_ref[...] = dx.astype(BF16)

    row = pl.BlockSpec((tr, d), lambda i: (i, 0))
    vec = pl.BlockSpec((1, d), lambda i: (0, 0))
    out_specs = [row] + ([row] if want_bf16 else []) + [vec]
    out_shape = ([jax.ShapeDtypeStruct((t, d), F32)] + ([jax.ShapeDtypeStruct((t, d), BF16)] if want_bf16 else [])
                 + [jax.ShapeDtypeStruct((1, d), F32)])
    body, in_specs, args = _ordered(body, [row, vec, row, row], [x, g, dh, add], after)
    return pl.pallas_call(
        body, name=name, grid=(t // tr,),
        in_specs=in_specs, out_specs=out_specs, out_shape=out_shape,
        compiler_params=_params(("arbitrary",)),
    )(*args)


def _loss_head(x2, g, target, name="loss_head"):
    t, d = x2.shape
    tr = _pick(t, 256, SUBLANE)

    def body(x_ref, g_ref, t_ref, dx_ref, dxb_ref, dg_ref, loss_ref):
        i = pl.program_id(0)

        @pl.when(i == 0)
        def _():
            dg_ref[...] = jnp.zeros_like(dg_ref)
            loss_ref[...] = jnp.zeros_like(loss_ref)

        xv = x_ref[...]
        gv = g_ref[...]
        r = lax.rsqrt(jnp.mean(xv * xv, axis=-1, keepdims=True) + RMS_EPS)
        xh = xv * r
        err = xh * gv - t_ref[...]
        part = 0.5 * jnp.sum(jnp.mean(err * err, axis=-1, keepdims=True), axis=0, keepdims=True)
        loss_ref[...] += jnp.broadcast_to(part, loss_ref.shape)
        dy = err * (1.0 / d)
        dg_ref[...] += jnp.sum(dy * xh, axis=0, keepdims=True)
        dxh = dy * gv
        dx = r * (dxh - xh * jnp.mean(dxh * xh, axis=-1, keepdims=True))
        dx_ref[...] = dx
        dxb_ref[...] = dx.astype(BF16)

    row = pl.BlockSpec((tr, d), lambda i: (i, 0))
    vec = pl.BlockSpec((1, d), lambda i: (0, 0))
    return pl.pallas_call(
        body, name=name, grid=(t // tr,),
        in_specs=[row, vec, row],
        out_specs=[row, row, vec, pl.BlockSpec((1, LANE), lambda i: (0, 0))],
        out_shape=[jax.ShapeDtypeStruct((t, d), F32), jax.ShapeDtypeStruct((t, d), BF16),
                   jax.ShapeDtypeStruct((1, d), F32), jax.ShapeDtypeStruct((1, LANE), F32)],
        compiler_params=_params(("arbitrary",)),
    )(x2, g, target)


def _s5_discretize(a_re, a_im, ldt):
    lam_re = jnp.minimum(a_re, S5_MAX_RE)
    lam_im = a_im
    dt = jnp.exp(ldt)
    mag = jnp.exp(lam_re * dt)
    abar_re = mag * jnp.cos(lam_im * dt)
    abar_im = mag * jnp.sin(lam_im * dt)
    den = lam_re * lam_re + lam_im * lam_im
    nr = abar_re - 1.0
    ni = abar_im
    coef_re = (nr * lam_re + ni * lam_im) / den
    coef_im = (ni * lam_re - nr * lam_im) / den
    return abar_re, abar_im, coef_re, coef_im


def _s5_param_fwd(a_re, a_im, ldt):
    def body(ar_ref, ai_ref, l_ref, o0, o1, o2, o3):
        outs = _s5_discretize(ar_ref[...], ai_ref[...], l_ref[...])
        for o, v in zip((o0, o1, o2, o3), outs):
            o[...] = v

    sh = jax.ShapeDtypeStruct(a_re.shape, F32)
    return pl.pallas_call(body, name="s5_param_fwd", out_shape=[sh, sh, sh, sh], compiler_params=_params())(a_re, a_im, ldt)


def _s5_param_bwd(a_re, a_im, ldt, cts):
    def body(ar_ref, ai_ref, l_ref, c0, c1, c2, c3, g0, g1, g2):
        _, vjp = jax.vjp(_s5_discretize, ar_ref[...], ai_ref[...], l_ref[...])
        ga, gb, gl = vjp((c0[...], c1[...], c2[...], c3[...]))
        g0[...] = ga
        g1[...] = gb
        g2[...] = gl

    sh = jax.ShapeDtypeStruct(a_re.shape, F32)
    return pl.pallas_call(body, name="s5_param_bwd", out_shape=[sh, sh, jax.ShapeDtypeStruct(ldt.shape, F32)],
                          compiler_params=_params())(a_re, a_im, ldt, *cts)


def _cmul(ar, ai, br, bi):
    return ar * br - ai * bi, ar * bi + ai * br


S5_TC = 128
S5_TILE = S5_SUPER * SUBLANE
S5_HALF = S5_TILE // 2


def _s5_to_tile(re, im):
    f = lambda a: a.reshape(S5_SUPER, S5_LANES // LANE, LANE).transpose(1, 0, 2).reshape(S5_HALF, LANE)
    return jnp.concatenate([f(re), f(im)], axis=0)


def _s5_from_tile(tile):
    f = lambda a: a.reshape(S5_LANES // LANE, S5_SUPER, LANE).transpose(1, 0, 2).reshape(S5_GROUPS, S5_STATE)
    return f(tile[0:S5_HALF]), f(tile[S5_HALF:])


RE = slice(0, S5_HALF)
IM = slice(S5_HALF, S5_TILE)


def _s5_scatter_rows(buf, rows, first_tile=0):
    tc = rows[0].shape[0]
    for j in range(SUBLANE):
        stacked = jnp.stack([r[:, j * LANE:(j + 1) * LANE] for r in rows], axis=0)
        buf[first_tile:first_tile + tc, j * SUBLANE:(j + 1) * SUBLANE, :] = jnp.swapaxes(stacked, 0, 1)


def _s5_gather_rows(buf, tc, first_tile=0):
    per_j = [jnp.swapaxes(buf[first_tile:first_tile + tc, j * SUBLANE:(j + 1) * SUBLANE, :], 0, 1)
             for j in range(SUBLANE)]
    return [jnp.concatenate([per_j[j][k] for j in range(SUBLANE)], axis=1) for k in range(S5_SUPER)]


def _s5_fwd(proj, bsg, ccat, d_row, abar_t, coef_t):
    t = proj.shape[0]
    tc = min(t, S5_TC)
    n_chunks = t // tc

    def body(u_ref, b_ref, c_ref, d_ref, a_ref, cf_ref, y_ref, sb_ref, x, car):
        @pl.when(pl.program_id(0) == 0)
        def _():
            car[...] = jnp.zeros_like(car)

        sb_ref[...] = car[...]
        u = u_ref[...]
        _s5_scatter_rows(x, [_dot(u[:, k * LANE:(k + 1) * LANE].astype(BF16), b_ref[k]) for k in range(S5_SUPER)])
        ar, ai = a_ref[RE, :], a_ref[IM, :]
        cr, ci = cf_ref[RE, :], cf_ref[IM, :]

        def step(i, carry):
            sr, si = carry
            xr, xi = _cmul(cr, ci, x[i, RE, :], x[i, IM, :])
            sr, si = ar * sr - ai * si + xr, ar * si + ai * sr + xi
            x[i, RE, :] = sr
            x[i, IM, :] = si
            return sr, si

        sr, si = lax.fori_loop(0, tc, step, (car[RE, :], car[IM, :]), unroll=4)
        car[RE, :] = sr
        car[IM, :] = si
        for k, s_k in enumerate(_s5_gather_rows(x, tc)):
            cols = slice(k * LANE, (k + 1) * LANE)
            y_ref[:, cols] = _dot(s_k.astype(BF16), c_ref[k]) + d_ref[:, cols] * u[:, cols]

    full = lambda shape: pl.BlockSpec(shape, lambda c: (0,) * len(shape))
    return pl.pallas_call(
        body, name="s5_fwd", grid=(n_chunks,),
        in_specs=[pl.BlockSpec((tc, S5_WIDTH), lambda c: (c, 0)), full(bsg.shape), full(ccat.shape), full(d_row.shape),
                  full(abar_t.shape), full(coef_t.shape)],
        out_specs=[pl.BlockSpec((tc, S5_WIDTH), lambda c: (c, 0)), pl.BlockSpec((None, S5_TILE, LANE), lambda c: (c, 0, 0))],
        out_shape=[jax.ShapeDtypeStruct((t, S5_WIDTH), F32), jax.ShapeDtypeStruct((n_chunks, S5_TILE, LANE), F32)],
        scratch_shapes=[pltpu.VMEM((tc, S5_TILE, LANE), F32), pltpu.VMEM((S5_TILE, LANE), F32)],
        compiler_params=_params(("arbitrary",)),
    )(proj, bsg, ccat, d_row, abar_t, coef_t)


def _s5_bwd(proj, dy, sb, bsg, ccat, d_row, abar_t, coef_t):
    t = proj.shape[0]
    tc = min(t, S5_TC)
    n_chunks = t // tc
    last = n_chunks - 1

    def body(u_ref, dy_ref, sb_ref, b_ref, c_ref, d_ref, a_ref, cf_ref,
             du_ref, gb_ref, gc_ref, gd_ref, ga_ref, gcf_ref, xb, xs, xg, gcar, acc):
        @pl.when(pl.program_id(0) == 0)
        def _():
            gcar[...] = jnp.zeros_like(gcar)
            acc[...] = jnp.zeros_like(acc)
            gb_ref[...] = jnp.zeros_like(gb_ref)
            gc_ref[...] = jnp.zeros_like(gc_ref)
            gd_ref[...] = jnp.zeros_like(gd_ref)

        u = u_ref[...]
        dyv = dy_ref[...]
        u16, dy16 = u.astype(BF16), dyv.astype(BF16)
        subs = [slice(k * LANE, (k + 1) * LANE) for k in range(S5_SUPER)]
        _s5_scatter_rows(xb, [_dot(u16[:, c], b_ref[k]) for k, c in enumerate(subs)])
        _s5_scatter_rows(xg, [_dot_nt(dy16[:, c], c_ref[k]) for k, c in enumerate(subs)])
        ar, ai = a_ref[RE, :], a_ref[IM, :]
        cr, ci = cf_ref[RE, :], cf_ref[IM, :]

        xs[0] = sb_ref[...]

        def fstep(i, carry):
            sr, si = carry
            xr, xi = _cmul(cr, ci, xb[i, RE, :], xb[i, IM, :])
            sr, si = ar * sr - ai * si + xr, ar * si + ai * sr + xi
            xs[i + 1, RE, :] = sr
            xs[i + 1, IM, :] = si
            return sr, si

        lax.fori_loop(0, tc, fstep, (sb_ref[RE, :], sb_ref[IM, :]), unroll=4)

        def rstep(n, carry):
            gr, gi, a0, a1, a2, a3 = carry
            i = tc - 1 - n
            xr = xg[i, RE, :] + ar * gr + ai * gi
            xi = xg[i, IM, :] + ar * gi - ai * gr
            pr, pi = xs[i, RE, :], xs[i, IM, :]
            br, bi = xb[i, RE, :], xb[i, IM, :]
            a0 = a0 + pr * xr + pi * xi
            a1 = a1 + pr * xi - pi * xr
            a2 = a2 + br * xr + bi * xi
            a3 = a3 + br * xi - bi * xr
            xg[i, RE, :] = cr * xr + ci * xi
            xg[i, IM, :] = cr * xi - ci * xr
            return xr, xi, a0, a1, a2, a3

        init = (gcar[RE, :], gcar[IM, :], acc[0], acc[1], acc[2], acc[3])
        gr, gi, a0, a1, a2, a3 = lax.fori_loop(0, tc, rstep, init, unroll=2)
        gcar[RE, :] = gr
        gcar[IM, :] = gi
        for idx, a in enumerate((a0, a1, a2, a3)):
            acc[idx] = a
        ga_ref[RE, :] = a0
        ga_ref[IM, :] = a1
        gcf_ref[RE, :] = a2
        gcf_ref[IM, :] = a3

        g_rows = _s5_gather_rows(xg, tc)
        s_rows = _s5_gather_rows(xs, tc, first_tile=1)
        for k in range(S5_SUPER):
            cols = subs[k]
            g16 = g_rows[k].astype(BF16)
            s16 = s_rows[k].astype(BF16)
            gb_ref[k] += _dot_tn(u16[:, cols], g16)
            gc_ref[k] += _dot_tn(s16, dy16[:, cols])
            du_ref[:, cols] = (_dot_nt(g16, b_ref[k]) + d_ref[:, cols] * dyv[:, cols]).astype(BF16)
        gd_ref[...] += jnp.sum(dyv * u, axis=0, keepdims=True)

    full = lambda shape: pl.BlockSpec(shape, lambda c: (0,) * len(shape))
    rows = pl.BlockSpec((tc, S5_WIDTH), lambda c: (last - c, 0))
    tile = (S5_TILE, LANE)
    return pl.pallas_call(
        body, name="s5_bwd", grid=(n_chunks,),
        in_specs=[rows, rows, pl.BlockSpec((None, S5_TILE, LANE), lambda c: (last - c, 0, 0)),
                  full(bsg.shape), full(ccat.shape), full(d_row.shape), full(abar_t.shape), full(coef_t.shape)],
        out_specs=[rows, full(bsg.shape), full(ccat.shape), full(d_row.shape), full(tile), full(tile)],
        out_shape=[jax.ShapeDtypeStruct((t, S5_WIDTH), BF16), jax.ShapeDtypeStruct(bsg.shape, F32),
                   jax.ShapeDtypeStruct(ccat.shape, F32), jax.ShapeDtypeStruct(d_row.shape, F32),
                   jax.ShapeDtypeStruct(tile, F32), jax.ShapeDtypeStruct(tile, F32)],
        scratch_shapes=[pltpu.VMEM((tc, S5_TILE, LANE), F32), pltpu.VMEM((tc + 1, S5_TILE, LANE), F32),
                        pltpu.VMEM((tc, S5_TILE, LANE), F32), pltpu.VMEM(tile, F32),
                        pltpu.VMEM((4, S5_HALF, LANE), F32)],
        compiler_params=_params(("arbitrary",)),
    )(proj, dy, sb, bsg, ccat, d_row, abar_t, coef_t)


def _gelu_fwd(y, name="s5_gelu"):
    t, w = y.shape
    tr = _pick(t, 512, SUBLANE)

    def body(y_ref, z_ref):
        z_ref[...] = _gelu_and_grad(y_ref[...])[0].astype(BF16)

    row = pl.BlockSpec((tr, w), lambda i: (i, 0))
    return pl.pallas_call(body, name=name, grid=(t // tr,), in_specs=[row], out_specs=row,
                          out_shape=jax.ShapeDtypeStruct((t, w), BF16), compiler_params=_params(("parallel",)))(y)


def _glu_fwd(y, gl, b, name="s5_glu"):
    t, w = y.shape
    tr = _pick(t, 512, SUBLANE)

    def body(y_ref, gl_ref, b_ref, z2_ref):
        z = _gelu_and_grad(y_ref[...])[0]
        z2_ref[...] = (z * _sigmoid(gl_ref[...] + b_ref[...])).astype(BF16)

    row = pl.BlockSpec((tr, w), lambda i: (i, 0))
    return pl.pallas_call(body, name=name, grid=(t // tr,),
                          in_specs=[row, row, pl.BlockSpec((1, w), lambda i: (0, 0))], out_specs=row,
                          out_shape=jax.ShapeDtypeStruct((t, w), BF16), compiler_params=_params(("parallel",)))(y, gl, b)


def _glu_bwd(y, gl, b, dz2, name="s5_glu_bwd", after=None):
    t, w = y.shape
    tr = _pick(t, 512, SUBLANE)

    def body(y_ref, gl_ref, b_ref, dz2_ref, dgl_ref, dza_ref, db_ref):
        @pl.when(pl.program_id(0) == 0)
        def _():
            db_ref[...] = jnp.zeros_like(db_ref)

        z = _gelu_and_grad(y_ref[...])[0]
        s = _sigmoid(gl_ref[...] + b_ref[...])
        dz2v = dz2_ref[...]
        dgl = dz2v * z * s * (1.0 - s)
        dgl_ref[...] = dgl.astype(BF16)
        dza_ref[...] = dz2v * s
        db_ref[...] += jnp.sum(dgl, axis=0, keepdims=True)

    row = pl.BlockSpec((tr, w), lambda i: (i, 0))
    vec = pl.BlockSpec((1, w), lambda i: (0, 0))
    body, in_specs, args = _ordered(body, [row, row, vec, row], [y, gl, b, dz2], after)
    return pl.pallas_call(body, name=name, grid=(t // tr,), in_specs=in_specs, out_specs=[row, row, vec],
                          out_shape=[jax.ShapeDtypeStruct((t, w), BF16), jax.ShapeDtypeStruct((t, w), F32),
                                     jax.ShapeDtypeStruct((1, w), F32)],
                          compiler_params=_params(("arbitrary",)))(*args)


def _gelu_bwd(y, dza, dzb, name="s5_gelu_bwd", after=None):
    t, w = y.shape
    tr = _pick(t, 512, SUBLANE)

    def body(y_ref, a_ref, b_ref, dy_ref):
        dy_ref[...] = (a_ref[...] + b_ref[...]) * _gelu_and_grad(y_ref[...])[1]

    row = pl.BlockSpec((tr, w), lambda i: (i, 0))
    body, in_specs, args = _ordered(body, [row, row, row], [y, dza, dzb], after)
    return pl.pallas_call(body, name=name, grid=(t // tr,), in_specs=in_specs, out_specs=row,
                          out_shape=jax.ShapeDtypeStruct((t, w), F32), compiler_params=_params(("parallel",)))(*args)


def _tri_dot(tri16, x):
    hi = x.astype(BF16)
    r1 = x - hi.astype(F32)
    mid = r1.astype(BF16)
    lo = (r1 - mid.astype(F32)).astype(BF16)
    return _dot(tri16, hi) + _dot(tri16, mid) + _dot(tri16, lo)


def _hgrn_pre(q_in, z, lg):
    lb = _sigmoid(lg[0:1, :] - lg[1:2, :])
    qs, dqs = _silu_and_grad(q_in)
    sz = _sigmoid(z)
    f = lb + (1.0 - lb) * sz
    k = (1.0 - lb) * (1.0 - sz)
    c = HGRN_CHUNK
    r = lax.broadcasted_iota(jnp.int32, (c, c), 0)
    s = lax.broadcasted_iota(jnp.int32, (c, c), 1)
    causal = r >= s
    b = _tri_dot(jnp.where(causal, 1.0, 0.0).astype(BF16), jnp.log(f))
    b_end = b[c - 1:c, :]
    b_mid = b[c // 2 - 1:c // 2, :]
    e_q, e_k, e_0, e_c = jnp.exp(b - b_mid), jnp.exp(b_mid - b), jnp.exp(b), jnp.exp(b_end - b)
    return dict(lb=lb, qs=qs, dqs=dqs, sz=sz, f=f, k=k, causal=causal, b_end=b_end,
                e_q=e_q, e_k=e_k, e_0=e_0, e_c=e_c,
                qt=qs * e_q, kt=k * e_k, q0=qs * e_0, kc=k * e_c)


def _hgrn_fwd(proj, logits, ng):
    t = proj.shape[0]
    c, dh = HGRN_CHUNK, HGRN_DH
    n_chunks = t // c

    def head(h, q_ref, z_ref, v_ref, g_ref, lg_ref, ng_ref, o_ref, oh_ref, s0_ref, st):
        sl = slice(h * dh, (h + 1) * dh)
        s0 = st[h]
        s0_ref[h] = s0
        p = _hgrn_pre(q_ref[:, sl], z_ref[:, sl], lg_ref[:, sl])
        v16 = v_ref[:, sl].astype(BF16)
        a = jnp.where(p["causal"], _dot_nt(p["qt"].astype(BF16), p["kt"].astype(BF16)), 0.0)
        o = _dot_nt(p["q0"].astype(BF16), s0.astype(BF16)) + _dot(a.astype(BF16), v16)
        st[h] = jnp.exp(p["b_end"]) * s0 + _dot_tn(v16, p["kc"].astype(BF16))
        o_ref[:, sl] = o
        rn = lax.rsqrt(jnp.mean(o * o, axis=-1, keepdims=True) + RMS_EPS)
        oh_ref[:, sl] = (o * rn * ng_ref[:, sl] * _silu_and_grad(g_ref[:, sl])[0]).astype(BF16)

    def body(*refs):
        st = refs[-1]

        @pl.when(pl.program_id(0) == 0)
        def _():
            st[...] = jnp.zeros_like(st)

        for h in range(HGRN_HEADS):
            head(h, *refs)

    def wide(off):
        return pl.BlockSpec((c, HGRN_WIDTH), lambda i: (i, off))

    return pl.pallas_call(
        body, name="hgrn_fwd", grid=(n_chunks,),
        in_specs=[wide(1), wide(2), wide(3), wide(4),
                  pl.BlockSpec((2, HGRN_WIDTH), lambda i: (0, 0)), pl.BlockSpec((1, HGRN_WIDTH), lambda i: (0, 0))],
        out_specs=[wide(0), wide(0), pl.BlockSpec((HGRN_HEADS, None, dh, dh), lambda i: (0, i, 0, 0))],
        out_shape=[jax.ShapeDtypeStruct((t, HGRN_WIDTH), F32), jax.ShapeDtypeStruct((t, HGRN_WIDTH), BF16),
                   jax.ShapeDtypeStruct((HGRN_HEADS, n_chunks, dh, dh), F32)],
        scratch_shapes=[pltpu.VMEM((HGRN_HEADS, dh, dh), F32)],
        compiler_params=_params(("arbitrary",)),
    )(proj, proj, proj, proj, logits, ng)


def _hgrn_bwd(proj, o_raw, s0s, doh, logits, ng):
    t = proj.shape[0]
    c, dh = HGRN_CHUNK, HGRN_DH
    n_chunks = t // c
    last = n_chunks - 1

    def head(h, q_ref, z_ref, v_ref, g_ref, o_ref, s0_ref, doh_ref, lg_ref, ng_ref,
             dq_ref, dz_ref, dv_ref, dg_ref, dng_ref, dlb_ref, dst):
        sl = slice(h * dh, (h + 1) * dh)
        p = _hgrn_pre(q_ref[:, sl], z_ref[:, sl], lg_ref[:, sl])
        v = v_ref[:, sl]
        v16 = v.astype(BF16)
        s0 = s0_ref[h]
        ds_end = dst[h]
        ds16 = ds_end.astype(BF16)
        ngv = ng_ref[:, sl]

        o = o_ref[:, sl]
        dohv = doh_ref[:, sl]
        sg, dsg = _silu_and_grad(g_ref[:, sl])
        rn = lax.rsqrt(jnp.mean(o * o, axis=-1, keepdims=True) + RMS_EPS)
        oh = o * rn
        dg_ref[:, sl] = (dohv * oh * ngv * dsg).astype(BF16)
        don = dohv * sg
        dng_ref[:, sl] += jnp.sum(don * oh, axis=0, keepdims=True)
        doh_n = don * ngv
        do = rn * (doh_n - oh * jnp.mean(doh_n * oh, axis=-1, keepdims=True))
        do16 = do.astype(BF16)

        qt16, kt16, q016, kc16 = (p[n].astype(BF16) for n in ("qt", "kt", "q0", "kc"))
        a = jnp.where(p["causal"], _dot_nt(qt16, kt16), 0.0)
        da = jnp.where(p["causal"], _dot_nt(do16, v16), 0.0)
        da16 = da.astype(BF16)
        dqt = _dot(da16, kt16)
        dq0 = _dot(do16, s0.astype(BF16))
        dkt = _dot_tn(da16, qt16)
        dkc = _dot(v16, ds16)
        dv_ref[:, sl] = (_dot_tn(a.astype(BF16), do16) + _dot_nt(kc16, ds16)).astype(BF16)
        lam_end = jnp.exp(p["b_end"])
        dst[h] = lam_end * ds_end + _dot_tn(do16, q016)

        qt, kt, q0, kc = (a.astype(F32) for a in (qt16, kt16, q016, kc16))
        db = dqt * qt + dq0 * q0 - dkt * kt - dkc * kc
        db_end = (jnp.sum(dkc * kc, axis=0, keepdims=True)
                  + jnp.sum(ds_end * s0, axis=0, keepdims=True) * lam_end)
        rowi = lax.broadcasted_iota(jnp.int32, (c, dh), 0)
        db = db + jnp.where(rowi == c - 1, db_end, 0.0)
        r = lax.broadcasted_iota(jnp.int32, (c, c), 0)
        s = lax.broadcasted_iota(jnp.int32, (c, c), 1)
        dlf = _tri_dot(jnp.where(s >= r, 1.0, 0.0).astype(BF16), db)

        dqs = dqt * p["e_q"] + dq0 * p["e_0"]
        dq_ref[:, sl] = (dqs * p["dqs"]).astype(BF16)
        dk = dkt * p["e_k"] + dkc * p["e_c"]
        sz, lb = p["sz"], p["lb"]
        common = dlf / p["f"] - dk
        dz_ref[:, sl] = ((1.0 - lb) * sz * (1.0 - sz) * common).astype(BF16)
        dlb_ref[:, sl] += jnp.sum((1.0 - sz) * common, axis=0, keepdims=True)

    def body(*refs):
        dng_ref, dlb_ref, dst = refs[-3:]

        @pl.when(pl.program_id(0) == 0)
        def _():
            dst[...] = jnp.zeros_like(dst)
            dng_ref[...] = jnp.zeros_like(dng_ref)
            dlb_ref[...] = jnp.zeros_like(dlb_ref)

        for h in range(HGRN_HEADS):
            head(h, *refs)

    def wide(off):
        return pl.BlockSpec((c, HGRN_WIDTH), lambda i: (last - i, off))

    vec = pl.BlockSpec((1, HGRN_WIDTH), lambda i: (0, 0))
    act = jax.ShapeDtypeStruct((t, HGRN_WIDTH), BF16)
    vsh = jax.ShapeDtypeStruct((1, HGRN_WIDTH), F32)
    return pl.pallas_call(
        body, name="hgrn_bwd", grid=(n_chunks,),
        in_specs=[wide(1), wide(2), wide(3), wide(4), wide(0),
                  pl.BlockSpec((HGRN_HEADS, None, dh, dh), lambda i: (0, last - i, 0, 0)),
                  wide(0), pl.BlockSpec((2, HGRN_WIDTH), lambda i: (0, 0)), vec],
        out_specs=[wide(0), wide(0), wide(0), wide(0), vec, vec],
        out_shape=[act, act, act, act, vsh, vsh],
        scratch_shapes=[pltpu.VMEM((HGRN_HEADS, dh, dh), F32)],
        compiler_params=_params(("arbitrary",)),
    )(proj, proj, proj, proj, o_raw, s0s, doh, logits, ng)


def _lb_bwd(logits, dlb):
    def body(lg_ref, d_ref, o_ref):
        lg = lg_ref[...]
        lb = _sigmoid(lg[0:1, :] - lg[1:2, :])
        g = d_ref[...] * lb * (1.0 - lb)
        o_ref[0:1, :] = g
        o_ref[1:2, :] = -g

    return pl.pallas_call(body, name="hgrn_lb_bwd", out_shape=jax.ShapeDtypeStruct(logits.shape, F32),
                          compiler_params=_params())(logits, dlb)


MERGE_TC = 1024
GS_BLOCK = (S5_WIDTH + 4 * HGRN_WIDTH) // MERGE_TC
GH_BLOCK = GS_BLOCK + D_MODEL // MERGE_TC


def _merge_fwd(proj, ys, yh):
    t = proj.shape[0]
    tr = _pick(t, 256, SUBLANE)

    def body(gs_ref, gh_ref, ys_ref, yh_ref, m_ref):
        m_ref[...] = (_sigmoid(gs_ref[...]) * ys_ref[...] + _sigmoid(gh_ref[...]) * yh_ref[...]).astype(BF16)

    blk = pl.BlockSpec((tr, MERGE_TC), lambda i, j: (i, j))
    return pl.pallas_call(
        body, name="merge_fwd", grid=(t // tr, D_MODEL // MERGE_TC),
        in_specs=[pl.BlockSpec((tr, MERGE_TC), lambda i, j: (i, GS_BLOCK + j)),
                  pl.BlockSpec((tr, MERGE_TC), lambda i, j: (i, GH_BLOCK + j)), blk, blk],
        out_specs=blk, out_shape=jax.ShapeDtypeStruct((t, D_MODEL), BF16),
        compiler_params=_params(("parallel", "parallel")),
    )(proj, proj, ys, yh)


def _merge_bwd(proj, ys, yh, dm, after=None):
    t = proj.shape[0]
    tr = _pick(t, 256, SUBLANE)

    def body(gs_ref, gh_ref, ys_ref, yh_ref, dm_ref, dys_ref, dyh_ref, dgs_ref, dgh_ref):
        dmv = dm_ref[...]
        ss, sh = _sigmoid(gs_ref[...]), _sigmoid(gh_ref[...])
        dys_ref[...] = (dmv * ss).astype(BF16)
        dyh_ref[...] = (dmv * sh).astype(BF16)
        dgs_ref[...] = (dmv * ys_ref[...] * ss * (1.0 - ss)).astype(BF16)
        dgh_ref[...] = (dmv * yh_ref[...] * sh * (1.0 - sh)).astype(BF16)

    blk = pl.BlockSpec((tr, MERGE_TC), lambda i, j: (i, j))
    sh16 = jax.ShapeDtypeStruct((t, D_MODEL), BF16)
    in_specs = [pl.BlockSpec((tr, MERGE_TC), lambda i, j: (i, GS_BLOCK + j)),
                pl.BlockSpec((tr, MERGE_TC), lambda i, j: (i, GH_BLOCK + j)), blk, blk, blk]
    body, in_specs, args = _ordered(body, in_specs, [proj, proj, ys, yh, dm], after)
    return pl.pallas_call(
        body, name="merge_bwd", grid=(t // tr, D_MODEL // MERGE_TC),
        in_specs=in_specs,
        out_specs=[blk, blk, blk, blk], out_shape=[sh16, sh16, sh16, sh16],
        compiler_params=_params(("parallel", "parallel")),
    )(*args)


FFN_TC = 128
FFN_ROWS = 128
HALO = SUBLANE


def _pad_rows(dst, src_ref):
    t, c = src_ref.shape
    dst[0:HALO, :] = jnp.zeros((HALO, c), F32)
    dst[HALO:HALO + t, :] = src_ref[...]
    dst[HALO + t:HALO + t + HALO, :] = jnp.zeros((HALO, c), F32)


def _conv3(padded, w, b, r0, nrows):
    x0 = padded[HALO + r0:HALO + r0 + nrows, :]
    x1 = padded[HALO + r0 - 1:HALO + r0 - 1 + nrows, :]
    x2 = padded[HALO + r0 - 2:HALO + r0 - 2 + nrows, :]
    return b + w[0:1, :] * x2 + w[1:2, :] * x1 + w[2:3, :] * x0, (x0, x1, x2)


def _ffn_act_fwd(up, cw, cb):
    t = up.shape[0]
    rows = _pick(t, FFN_ROWS, SUBLANE)
    nvb = D_FF // FFN_TC

    def body(ug_ref, uv_ref, wg_ref, wv_ref, bg_ref, bv_ref, act_ref, pg, pv):
        wg, wv, bg, bv = wg_ref[...], wv_ref[...], bg_ref[...], bv_ref[...]
        _pad_rows(pg, ug_ref)
        _pad_rows(pv, uv_ref)
        for r0 in range(0, t, rows):
            cg, _ = _conv3(pg, wg, bg, r0, rows)
            cv, _ = _conv3(pv, wv, bv, r0, rows)
            act_ref[r0:r0 + rows, :] = (_silu_and_grad(cg)[0] * cv).astype(BF16)

    def colblk(nrow, off):
        return pl.BlockSpec((nrow, FFN_TC), lambda j: (0, off + j))

    return pl.pallas_call(
        body, name="ffn_act_fwd", grid=(nvb,),
        in_specs=[colblk(t, 0), colblk(t, nvb), colblk(3, 0), colblk(3, nvb), colblk(1, 0), colblk(1, nvb)],
        out_specs=colblk(t, 0), out_shape=jax.ShapeDtypeStruct((t, D_FF), BF16),
        scratch_shapes=[pltpu.VMEM((t + 2 * HALO, FFN_TC), F32), pltpu.VMEM((t + 2 * HALO, FFN_TC), F32)],
        compiler_params=_params(("parallel",)),
    )(up, up, cw, cw, cb, cb)


def _ffn_act_bwd(up, dact, cw, cb, after=None):
    t = up.shape[0]
    rows = _pick(t, FFN_ROWS, SUBLANE)
    nvb = D_FF // FFN_TC

    def body(ug_ref, uv_ref, da_ref, wg_ref, wv_ref, bg_ref, bv_ref,
             dug_ref, duv_ref, dwg_ref, dwv_ref, dbg_ref, dbv_ref, pg, pv, dcs):
        wg, wv, bg, bv = wg_ref[...], wv_ref[...], bg_ref[...], bv_ref[...]
        _pad_rows(pg, ug_ref)
        _pad_rows(pv, uv_ref)
        ext = rows + HALO
        acc_g = [jnp.zeros((1, FFN_TC), F32) for _ in range(4)]
        acc_v = [jnp.zeros((1, FFN_TC), F32) for _ in range(4)]
        for r0 in range(0, t, rows):
            cg, xg = _conv3(pg, wg, bg, r0, ext)
            cv, xv = _conv3(pv, wv, bv, r0, ext)
            if r0 + ext <= t:
                dav = da_ref[r0:r0 + ext, :]
            else:
                dav = jnp.concatenate([da_ref[r0:t, :], jnp.zeros((HALO, FFN_TC), F32)], axis=0)
            sg, dsg = _silu_and_grad(cg)
            for h, (dconv, xs, w, acc, out) in enumerate(((dav * cv * dsg, xg, wg, acc_g, dug_ref),
                                                           (dav * sg, xv, wv, acc_v, duv_ref))):
                dcs[h] = dconv
                d0 = dconv[0:rows, :]
                d1 = dcs[h, 1:rows + 1, :]
                d2 = dcs[h, 2:rows + 2, :]
                out[r0:r0 + rows, :] = (w[2:3, :] * d0 + w[1:2, :] * d1 + w[0:1, :] * d2).astype(BF16)
                x0, x1, x2 = xs
                acc[0] = acc[0] + jnp.sum(d0 * x2[0:rows, :], axis=0, keepdims=True)
                acc[1] = acc[1] + jnp.sum(d0 * x1[0:rows, :], axis=0, keepdims=True)
                acc[2] = acc[2] + jnp.sum(d0 * x0[0:rows, :], axis=0, keepdims=True)
                acc[3] = acc[3] + jnp.sum(d0, axis=0, keepdims=True)
        for acc, dw_ref, db_ref in ((acc_g, dwg_ref, dbg_ref), (acc_v, dwv_ref, dbv_ref)):
            dw_ref[0:1, :] = acc[0]
            dw_ref[1:2, :] = acc[1]
            dw_ref[2:3, :] = acc[2]
            db_ref[...] = acc[3]

    def colblk(nrow, off):
        return pl.BlockSpec((nrow, FFN_TC), lambda j: (0, off + j))

    in_specs = [colblk(t, 0), colblk(t, nvb), colblk(t, 0), colblk(3, 0), colblk(3, nvb), colblk(1, 0), colblk(1, nvb)]
    body, in_specs, args = _ordered(body, in_specs, [up, up, dact, cw, cw, cb, cb], after)
    return pl.pallas_call(
        body, name="ffn_act_bwd", grid=(nvb,),
        in_specs=in_specs,
        out_specs=[colblk(t, 0), colblk(t, 0), colblk(3, 0), colblk(3, 0), colblk(1, 0), colblk(1, 0)],
        out_shape=[jax.ShapeDtypeStruct((t, D_FF), BF16), jax.ShapeDtypeStruct((t, D_FF), BF16),
                   jax.ShapeDtypeStruct((3, D_FF), F32), jax.ShapeDtypeStruct((3, D_FF), F32),
                   jax.ShapeDtypeStruct((1, D_FF), F32), jax.ShapeDtypeStruct((1, D_FF), F32)],
        scratch_shapes=[pltpu.VMEM((t + 2 * HALO, FFN_TC), F32), pltpu.VMEM((t + 2 * HALO, FFN_TC), F32),
                        pltpu.VMEM((2, rows + HALO, FFN_TC), F32)],
        compiler_params=_params(("parallel",)),
    )(*args)


def _all_gather(shards, name):
    nw = len(shards)

    def body(*refs):
        x_refs, out_refs = refs[:nw], refs[nw:2 * nw]
        send_sems, recv_sems, local_sems = refs[2 * nw:]
        x, y, c = lax.axis_index("x"), lax.axis_index("y"), lax.axis_index("c")
        me, sibling = (x, y, c), (x, y, 1 - c)
        chips = [(1 - x, y), (x, 1 - y), (1 - x, 1 - y)]

        def copy(w, k, block, to, src=None):
            slot = out_refs[w].at[4 * block[0] + 2 * block[1] + block[2]]
            return pltpu.make_async_remote_copy(
                src_ref=slot if src is None else src, dst_ref=slot,
                send_sem=send_sems.at[w, k], recv_sem=recv_sems.at[w, k],
                device_id=to, device_id_type=MESH)

        mine, first, passed = [], [], []
        for w in range(nw):
            cp = pltpu.make_async_copy(x_refs[w], out_refs[w].at[4 * x + 2 * y + c], local_sems.at[w])
            cp.start()
            mine.append(cp)
            first.append(copy(w, 0, me, sibling, src=x_refs[w]))
            first += [copy(w, 1 + j, me, (*chip, c), src=x_refs[w]) for j, chip in enumerate(chips)]
        for cp in first:
            cp.start()
        for w in range(nw):
            for j, chip in enumerate(chips):
                copy(w, 1 + j, (*chip, c), me).wait_recv()
                fwd = copy(w, 4 + j, (*chip, c), sibling)
                fwd.start()
                passed.append(fwd)
        for w in range(nw):
            copy(w, 0, sibling, me).wait_recv()
            for j, chip in enumerate(chips):
                copy(w, 4 + j, (*chip, 1 - c), me).wait_recv()
        for cp in first + passed:
            cp.wait_send()
        for cp in mine:
            cp.wait()

    anyspec = pl.BlockSpec(memory_space=pl.ANY)
    return pl.pallas_call(
        body, name=name,
        in_specs=[anyspec] * nw, out_specs=[anyspec] * nw,
        out_shape=[jax.ShapeDtypeStruct((N_DEV,) + s.shape, s.dtype) for s in shards],
        scratch_shapes=[pltpu.SemaphoreType.DMA((nw, 7)), pltpu.SemaphoreType.DMA((nw, 7)),
                        pltpu.SemaphoreType.DMA((nw,))],
    )(*shards)


HBM_SPEC = pl.BlockSpec(memory_space=pltpu.HBM)
SEM_SPEC = pl.BlockSpec(memory_space=pltpu.SEMAPHORE)
ANY_SPEC = pl.BlockSpec(memory_space=pl.ANY)
DATAFLOW = pltpu.SideEffectType.DATAFLOW_SIDE_EFFECTING


def _my_index():
    return 4 * lax.axis_index("x") + 2 * lax.axis_index("y") + lax.axis_index("c")


def _peers():
    x, y, c = lax.axis_index("x"), lax.axis_index("y"), lax.axis_index("c")
    peers = []
    for k in range(1, N_DEV):
        px = 1 - x if k & 4 else x
        py = 1 - y if k & 2 else y
        pc = 1 - c if k & 1 else c
        peers.append((k, (px, py, pc), 4 * px + 2 * py + pc))
    return peers


def _split_copy(src_ref, land_ref, send_sems, recv_sems, w, k, peer, slot, scatter, outgoing):
    return pltpu.make_async_remote_copy(
        src_ref=src_ref.at[slot] if scatter else src_ref,
        dst_ref=land_ref.at[_my_index() if outgoing else slot],
        send_sem=send_sems.at[w * (N_DEV - 1) + k - 1], recv_sem=recv_sems.at[w * (N_DEV - 1) + k - 1],
        device_id=peer, device_id_type=MESH)


def _exchange_start(srcs, scatter, after, name):
    nw = len(srcs)
    me = _my_index()
    lands = []
    for s in srcs:
        own = lax.dynamic_index_in_dim(s, me, 0, keepdims=True) if scatter else s[None]
        shape = s.shape if scatter else (N_DEV,) + s.shape
        lands.append(lax.dynamic_update_slice_in_dim(lax.empty(shape, s.dtype), own, me, 0))

    afters = [] if after is None else [after]

    def body(*refs):
        s_refs, l_refs = refs[:nw], refs[nw:2 * nw]
        send_sems, recv_sems = refs[2 * nw + len(afters)], refs[2 * nw + len(afters) + 1]
        token = refs[-1]
        for w in range(nw):
            for k, peer, slot in _peers():
                _split_copy(s_refs[w], l_refs[w], send_sems, recv_sems, w, k, peer, slot, scatter, True).start()
        token[...] = jnp.zeros_like(token)

    sems = pltpu.SemaphoreType.DMA((nw * (N_DEV - 1),))
    outs = pl.pallas_call(
        body, name=name,
        out_shape=(sems, sems, *[pltpu.HBM(a.shape, a.dtype) for a in (*srcs, *lands)],
                   jax.ShapeDtypeStruct((SUBLANE, LANE), F32)),
        in_specs=[HBM_SPEC] * (2 * nw) + [ANY_SPEC] * len(afters),
        out_specs=(SEM_SPEC, SEM_SPEC, *[HBM_SPEC] * (2 * nw), pl.BlockSpec(memory_space=pltpu.VMEM)),
        input_output_aliases={i: 2 + i for i in range(2 * nw)},
        compiler_params=pltpu.CompilerParams(has_side_effects=DATAFLOW),
    )(*[pltpu.with_memory_space_constraint(a, pltpu.HBM) for a in (*srcs, *lands)], *afters)
    return dict(sems=outs[:2], srcs=outs[2:2 + nw], lands=outs[2 + nw:2 + 2 * nw], token=outs[-1], scatter=scatter)


def _exchange_wait(handle, afters, name):
    srcs, lands, scatter = handle["srcs"], handle["lands"], handle["scatter"]
    nw = len(srcs)

    def body(*refs):
        s_refs, l_refs = refs[:nw], refs[nw:2 * nw]
        send_sems, recv_sems = refs[2 * nw], refs[2 * nw + 1]
        for w in range(nw):
            for k, peer, slot in _peers():
                cp = _split_copy(s_refs[w], l_refs[w], send_sems, recv_sems, w, k, peer, slot, scatter, False)
                cp.wait_send()
                cp.wait_recv()

    outs = pl.pallas_call(
        body, name=name,
        out_shape=tuple(pltpu.HBM(a.shape, a.dtype) for a in (*srcs, *lands)),
        in_specs=[HBM_SPEC] * (2 * nw) + [SEM_SPEC, SEM_SPEC] + [ANY_SPEC] * len(afters),
        out_specs=tuple([HBM_SPEC] * (2 * nw)),
        input_output_aliases={i: i for i in range(2 * nw)},
        compiler_params=pltpu.CompilerParams(has_side_effects=DATAFLOW),
    )(*srcs, *lands, *handle["sems"], *afters)
    return list(outs[nw:])


def _adamw(w, g, m, v):
    m = ADAM_B1 * m + (1.0 - ADAM_B1) * g
    v = ADAM_B2 * v + (1.0 - ADAM_B2) * (g * g)
    m_hat = m / (1.0 - ADAM_B1 ** ADAM_STEP)
    v_hat = v / (1.0 - ADAM_B2 ** ADAM_STEP)
    delta = -ADAM_LR * (m_hat / (jnp.sqrt(v_hat) + ADAM_EPS) + ADAM_WD * w)
    return delta, m, v


def _sum_adam(parts, w, m, v, name):
    _, r, c = parts.shape
    tr = _pick(r, 128, 16)

    def body(p_ref, w_ref, m_ref, v_ref, g_ref, d_ref, mo_ref, vo_ref):
        g = p_ref[0].astype(F32)
        for s in range(1, N_DEV):
            g = g + p_ref[s].astype(F32)
        g_ref[...] = g
        d_ref[...], mo_ref[...], vo_ref[...] = _adamw(w_ref[...], g, m_ref[...], v_ref[...])

    row = pl.BlockSpec((tr, c), lambda i: (i, 0))
    sh = jax.ShapeDtypeStruct((r, c), F32)
    return pl.pallas_call(
        body, name=name, grid=(r // tr,),
        in_specs=[pl.BlockSpec((N_DEV, tr, c), lambda i: (0, i, 0)), row, row, row],
        out_specs=[row, row, row, row], out_shape=[sh, sh, sh, sh],
        compiler_params=_params(("parallel",)),
    )(parts, w, m, v)


def _sum_slots(parts, name):
    _, r, c = parts.shape
    tr = _pick(r, 512, SUBLANE)

    def body(p_ref, o_ref):
        g = p_ref[0]
        for s in range(1, N_DEV):
            g = g + p_ref[s]
        o_ref[...] = g

    return pl.pallas_call(
        body, name=name, grid=(r // tr,),
        in_specs=[pl.BlockSpec((N_DEV, tr, c), lambda i: (0, i, 0))],
        out_specs=pl.BlockSpec((tr, c), lambda i: (i, 0)), out_shape=jax.ShapeDtypeStruct((r, c), F32),
        compiler_params=_params(("parallel",)),
    )(parts)


def _adam_rows(g, w, m, v, name):
    r, c = g.shape
    tr = _pick(r, 512, SUBLANE)

    def body(g_ref, w_ref, m_ref, v_ref, d_ref, mo_ref, vo_ref):
        d_ref[...], mo_ref[...], vo_ref[...] = _adamw(w_ref[...], g_ref[...], m_ref[...], v_ref[...])

    row = pl.BlockSpec((tr, c), lambda i: (i, 0))
    sh = jax.ShapeDtypeStruct((r, c), F32)
    return pl.pallas_call(body, name=name, grid=(r // tr,), in_specs=[row] * 4, out_specs=[row] * 3,
                          out_shape=[sh, sh, sh], compiler_params=_params(("parallel",)))(g, w, m, v)


def _pack(arrays):
    flat = jnp.concatenate([a.reshape(-1).astype(F32) for a in arrays])
    pad = (-flat.shape[0]) % (SUBLANE * LANE)
    return jnp.pad(flat, (0, pad)).reshape(-1, LANE)


def _unpack(packed, shapes):
    flat = packed.reshape(-1)
    out, off = [], 0
    for s in shapes:
        n = math.prod(s)
        out.append(flat[off:off + n].reshape(s))
        off += n
    return out


def _block_diag(t):
    eye = jnp.eye(S5_SUPER, dtype=bool)
    bd = jnp.where(eye[None, :, None, :, None], t[:, :, :, None, :], 0.0)
    return bd.reshape(S5_SUPER, S5_SUPER * t.shape[2], S5_SUPER * t.shape[3])


def _diag_blocks(dense, a, b):
    x = dense.reshape(S5_SUPER, S5_SUPER, a, S5_SUPER, b)
    return jnp.moveaxis(jnp.diagonal(x, axis1=1, axis2=3), -1, 1)


def _s5_layouts(b_re, b_im, c_re, c_im, d):
    g2 = (S5_GROUPS // S5_SUPER, S5_SUPER)
    bt = lambda b: _block_diag(b.reshape(*g2, S5_STATE, S5_GROUP).transpose(0, 1, 3, 2))
    ct = lambda c: _block_diag(c.reshape(*g2, S5_GROUP, S5_STATE).transpose(0, 1, 3, 2))
    bsg = jnp.concatenate([bt(b_re), bt(b_im)], axis=2).astype(BF16)
    ccat = jnp.concatenate([ct(c_re), -ct(c_im)], axis=1).astype(BF16)
    return bsg, ccat, d.reshape(1, S5_WIDTH)


def _s5_param_grads(gb, gc):
    n = S5_LANES
    gb_re = _diag_blocks(gb[:, :, 0:n], S5_GROUP, S5_STATE).transpose(0, 1, 3, 2).reshape(S5_GROUPS, S5_STATE, S5_GROUP)
    gb_im = _diag_blocks(gb[:, :, n:2 * n], S5_GROUP, S5_STATE).transpose(0, 1, 3, 2).reshape(S5_GROUPS, S5_STATE, S5_GROUP)
    gc_re = _diag_blocks(gc[:, 0:n, :], S5_STATE, S5_GROUP).transpose(0, 1, 3, 2).reshape(S5_GROUPS, S5_GROUP, S5_STATE)
    gc_im = -_diag_blocks(gc[:, n:2 * n, :], S5_STATE, S5_GROUP).transpose(0, 1, 3, 2).reshape(S5_GROUPS, S5_GROUP, S5_STATE)
    return gb_re, gb_im, gc_re, gc_im


def _local_step(x, target, weight, emit, small, after=None):
    sp = small
    a_re, a_im = sp["s5_a_re"], sp["s5_a_im"]
    ldt = sp["s5_log_dt"].reshape(S5_GROUPS, 1)

    h1 = _rms_fwd(x, sp["ln_mix_g"], "rms_mix", after=after)
    w_in = weight("w_in", h1)
    proj = _mm_nn(h1, w_in, "mm_in")
    disc = _s5_param_fwd(a_re, a_im, ldt)
    bsg, ccat, d_row = _s5_layouts(sp["s5_b_re"], sp["s5_b_im"], sp["s5_c_re"], sp["s5_c_im"], sp["s5_d"])
    abar_t, coef_t = _s5_to_tile(disc[0], disc[1]), _s5_to_tile(disc[2], disc[3])
    y, sb = _s5_fwd(proj, bsg, ccat, d_row, abar_t, coef_t)
    z16 = _gelu_fwd(y)
    w_glu = weight("s5_w_glu", z16)
    gl = _mm_nn(z16, w_glu, "mm_glu")
    z2 = _glu_fwd(y, gl, sp["s5_b_glu"])
    w_ps = weight("w_proj_s5", z2)
    ys = _mm_nn(z2, w_ps, "mm_proj_s5")
    o_raw, oh, s0s = _hgrn_fwd(proj, sp["hgrn_lb_logits"], sp["hgrn_norm_g"])
    w_ph = weight("w_proj_hgrn", oh)
    yh = _mm_nn(oh, w_ph, "mm_proj_hgrn")
    merged = _merge_fwd(proj, ys, yh)
    w_out = weight("w_out", merged)
    x1 = _mm_nn(merged, w_out, "mm_out", res=x)
    h2 = _rms_fwd(x1, sp["ln_ffn_g"], "rms_ffn")
    w_up = weight("w_up", h2)
    up = _mm_nn(h2, w_up, "mm_up")
    act = _ffn_act_fwd(up, sp["conv_w"], sp["conv_b"])
    w_down = weight("w_down", act)
    x2 = _mm_nn(act, w_down, "mm_down", res=x1)
    dx2, dx2_16, g_ln_final, loss = _loss_head(x2, sp["ln_final_g"], target)

    dact = _mm_nt(dx2_16, w_down, "mm_down_dx")
    tok = emit("w_down", _mm_tn(act, dx2_16, 1, "mm_down_dw"))
    dup_g, dup_v, dcw_g, dcw_v, dcb_g, dcb_v = _ffn_act_bwd(up, dact, sp["conv_w"], sp["conv_b"], after=tok)
    dup = jnp.concatenate([dup_g, dup_v], axis=1)
    g_conv_w = jnp.concatenate([dcw_g, dcw_v], axis=1)
    g_conv_b = jnp.concatenate([dcb_g, dcb_v], axis=1)
    dh2 = _mm_nt(dup, w_up, "mm_up_dx")
    tok = emit("w_up", _mm_tn(h2, dup, N_DEV, "mm_up_dw"))
    dx1, dx1_16, g_ln_ffn = _rms_bwd(x1, sp["ln_ffn_g"], dh2, dx2, "rms_ffn_bwd", True, after=tok)

    dmerged = _mm_nt(dx1_16, w_out, "mm_out_dx")
    tok = emit("w_out", _mm_tn(merged, dx1_16, 1, "mm_out_dw"))
    dys, dyh, dgs, dgh = _merge_bwd(proj, ys, yh, dmerged, after=tok)
    doh = _mm_nt(dyh, w_ph, "mm_proj_hgrn_dx")
    tok = emit("w_proj_hgrn", _mm_tn(oh, dyh, N_DEV, "mm_proj_hgrn_dw"))
    dz2 = _mm_nt(dys, w_ps, "mm_proj_s5_dx", after=tok)
    tok = emit("w_proj_s5", _mm_tn(z2, dys, N_DEV, "mm_proj_s5_dw"))
    dgl, dza, g_b_glu = _glu_bwd(y, gl, sp["s5_b_glu"], dz2, after=tok)
    dzb = _mm_nt(dgl, w_glu, "mm_glu_dx")
    tok = emit("s5_w_glu", _mm_tn(z16, dgl, 1, "mm_glu_dw"))
    dy = _gelu_bwd(y, dza, dzb, after=tok)
    du, gb, gc, gd, g_abar_t, g_coef_t = _s5_bwd(proj, dy, sb, bsg, ccat, d_row, abar_t, coef_t)
    g_a_re, g_a_im, g_ldt = _s5_param_bwd(a_re, a_im, ldt, [*_s5_from_tile(g_abar_t), *_s5_from_tile(g_coef_t)])
    g_b_re, g_b_im, g_c_re, g_c_im = _s5_param_grads(gb, gc)
    dq, dz, dv, dg, g_norm, dlb = _hgrn_bwd(proj, o_raw, s0s, doh, sp["hgrn_lb_logits"], sp["hgrn_norm_g"])
    g_logits = _lb_bwd(sp["hgrn_lb_logits"], dlb)

    small_g = dict(s5_a_re=g_a_re, s5_a_im=g_a_im, s5_log_dt=g_ldt.reshape(1, S5_GROUPS),
                   s5_b_re=g_b_re, s5_b_im=g_b_im, s5_c_re=g_c_re, s5_c_im=g_c_im,
                   s5_d=gd.reshape(S5_GROUPS, S5_GROUP), s5_b_glu=g_b_glu, hgrn_lb_logits=g_logits,
                   hgrn_norm_g=g_norm, ln_ffn_g=g_ln_ffn, conv_w=g_conv_w, conv_b=g_conv_b, ln_final_g=g_ln_final,
                   loss=loss[0, 0:1])
    tok_small = emit("small", small_g)

    dproj = jnp.concatenate([du, dq, dz, dv, dg, dgs, dgh], axis=1)
    tok = emit("w_in", _mm_tn(h1, dproj, N_DEV, "mm_in_dw", after=tok_small))
    dh1 = _mm_nt(dproj, w_in, "mm_in_dx")
    grad_x, g_ln_mix = _rms_bwd(x, sp["ln_mix_g"], dh1, dx1, "rms_mix_bwd", False, after=tok)
    return grad_x, g_ln_mix


BIG = ("w_in", "s5_w_glu", "w_proj_s5", "w_proj_hgrn", "w_out", "w_up", "w_down")
COL_SHARDED = ("w_in", "w_proj_s5", "w_proj_hgrn", "w_up")
SMALL = ("ln_mix_g", "s5_a_re", "s5_a_im", "s5_log_dt", "s5_b_re", "s5_b_im", "s5_c_re", "s5_c_im", "s5_d",
         "s5_b_glu", "hgrn_lb_logits", "hgrn_norm_g", "ln_ffn_g", "conv_b", "ln_final_g")
WEIGHTS = ("ln_mix_g", "w_in", "s5_a_re", "s5_a_im", "s5_log_dt", "s5_b_re", "s5_b_im", "s5_c_re", "s5_c_im", "s5_d",
           "s5_w_glu", "s5_b_glu", "w_proj_s5", "hgrn_lb_logits", "hgrn_norm_g", "w_proj_hgrn", "w_out", "ln_ffn_g",
           "w_up", "conv_w", "conv_b", "w_down", "ln_final_g")


def kernel(x, ln_mix_g, w_in, s5_a_re, s5_a_im, s5_log_dt, s5_b_re, s5_b_im, s5_c_re, s5_c_im, s5_d, s5_w_glu, s5_b_glu, w_proj_s5, hgrn_lb_logits, hgrn_norm_g, w_proj_hgrn, w_out, ln_ffn_g, w_up, conv_w, conv_b, w_down, ln_final_g, loss_target, m_ln_mix_g, m_w_in, m_s5_a_re, m_s5_a_im, m_s5_log_dt, m_s5_b_re, m_s5_b_im, m_s5_c_re, m_s5_c_im, m_s5_d, m_s5_w_glu, m_s5_b_glu, m_w_proj_s5, m_hgrn_lb_logits, m_hgrn_norm_g, m_w_proj_hgrn, m_w_out, m_ln_ffn_g, m_w_up, m_conv_w, m_conv_b, m_w_down, m_ln_final_g, v_ln_mix_g, v_w_in, v_s5_a_re, v_s5_a_im, v_s5_log_dt, v_s5_b_re, v_s5_b_im, v_s5_c_re, v_s5_c_im, v_s5_d, v_s5_w_glu, v_s5_b_glu, v_w_proj_s5, v_hgrn_lb_logits, v_hgrn_norm_g, v_w_proj_hgrn, v_w_out, v_ln_ffn_g, v_w_up, v_conv_w, v_conv_b, v_w_down, v_ln_final_g):
    given = dict(locals())
    w = {n: given[n] for n in WEIGHTS}
    mom = {n: given["m_" + n] for n in WEIGHTS}
    var = {n: given["v_" + n] for n in WEIGHTS}

    shard16 = {n: w[n][0].astype(BF16) for n in BIG}
    w_in_all, conv_w_all = _all_gather([shard16["w_in"], conv_w[0]], "gather_first")
    gather_groups = (("s5_w_glu", "w_proj_s5", "w_proj_hgrn", "w_out"), ("w_up",), ("w_down",))
    pending, token = {}, w_in_all
    for i, group in enumerate(gather_groups):
        handle = _exchange_start([shard16[n] for n in group], False, token, f"gather_start_{i}")
        token = handle["token"]
        for n in group:
            pending[n] = (group, handle, f"gather_wait_{i}")
    ready = {"w_in": w_in_all}

    def weight(name, after):
        if name not in ready:
            group, handle, wait_name = pending[name]
            for n, g in zip(group, _exchange_wait(handle, [after], wait_name)):
                ready[n] = g
        g = ready[name]
        return g if name in COL_SHARDED else g.reshape(1, N_DEV * g.shape[1], g.shape[2])

    scatter_groups = (("w_down",), ("w_up",), ("w_out", "w_proj_hgrn", "w_proj_s5", "s5_w_glu"), ("w_in",))
    emitted, scatters = {}, []
    packed_names = SMALL[1:] + ("conv_w", "loss")

    def emit(name, grad):
        if name == "small":
            emitted[name] = ([grad[n].shape for n in packed_names],
                             _exchange_start([_pack([grad[n] for n in packed_names])], False, None, "small_start"))
            return emitted[name][1]["token"]
        emitted[name] = grad if name in COL_SHARDED else grad.reshape(N_DEV, -1, grad.shape[2])
        group = scatter_groups[len(scatters)]
        if not all(n in emitted for n in group):
            return None
        handle = _exchange_start([emitted[n] for n in group], True, None, f"scatter_start_{len(scatters)}")
        scatters.append((group, handle))
        return handle["token"]

    small = dict(ln_mix_g=ln_mix_g, s5_a_re=s5_a_re[0], s5_a_im=s5_a_im[0], s5_log_dt=s5_log_dt,
                 s5_b_re=s5_b_re[0], s5_b_im=s5_b_im[0], s5_c_re=s5_c_re[0], s5_c_im=s5_c_im[0], s5_d=s5_d[0],
                 s5_b_glu=s5_b_glu, hgrn_lb_logits=hgrn_lb_logits, hgrn_norm_g=hgrn_norm_g, ln_ffn_g=ln_ffn_g,
                 conv_w=conv_w_all.transpose(1, 0, 2).reshape(3, 2 * D_FF), conv_b=conv_b,
                 ln_final_g=ln_final_g.reshape(1, D_MODEL))
    grad_x, g_ln_mix = _local_step(x[0], loss_target[0], weight, emit, small, after=token)

    shapes, handle = emitted["small"]
    total = _sum_slots(_exchange_wait(handle, [grad_x], "small_wait")[0], "sum_small")
    summed = dict(zip(packed_names, _unpack(total, shapes)))
    mix_all = _all_gather([g_ln_mix.reshape(-1, LANE)], "gather_ln_mix")[0]
    summed["ln_mix_g"] = _sum_slots(mix_all, "sum_ln_mix").reshape(1, D_MODEL)

    grads, delta, new_m, new_v = {}, {}, {}, {}
    afters = [grad_x, total]
    for i, (group, handle) in enumerate(scatters):
        for n, r in zip(group, _exchange_wait(handle, afters, f"scatter_wait_{i}")):
            g, d, m2, v2 = _sum_adam(r, w[n][0], mom[n][0], var[n][0], "adam_" + n)
            grads[n], delta[n], new_m[n], new_v[n] = g[None], d[None], m2[None], v2[None]
        if i == len(scatters) - 2:
            afters = [delta[n] for g2, _ in scatters[:-1] for n in g2]

    packed_small = SMALL[1:]
    pw = _pack([w[n] for n in packed_small])
    d_s, m_s, v_s = _adam_rows(_pack([summed[n] for n in packed_small]), pw, _pack([mom[n] for n in packed_small]),
                               _pack([var[n] for n in packed_small]), "adam_small")
    wshapes = [w[n].shape for n in packed_small]
    for n, d, m2, v2 in zip(packed_small, _unpack(d_s, wshapes), _unpack(m_s, wshapes), _unpack(v_s, wshapes)):
        grads[n], delta[n], new_m[n], new_v[n] = summed[n].reshape(w[n].shape), d, m2, v2
    grads["ln_mix_g"] = summed["ln_mix_g"]
    delta["ln_mix_g"], new_m["ln_mix_g"], new_v["ln_mix_g"] = _adam_rows(summed["ln_mix_g"], ln_mix_g, m_ln_mix_g,
                                                                         v_ln_mix_g, "adam_ln_mix")
    me = 4 * lax.axis_index("x") + 2 * lax.axis_index("y") + lax.axis_index("c")
    ncol = conv_w.shape[2]
    g_cw = lax.dynamic_slice_in_dim(summed["conv_w"], me * ncol, ncol, axis=1)
    d_cw, m_cw, v_cw = _adam_rows(g_cw, conv_w[0], m_conv_w[0], v_conv_w[0], "adam_conv_w")
    grads["conv_w"], delta["conv_w"], new_m["conv_w"], new_v["conv_w"] = g_cw[None], d_cw[None], m_cw[None], v_cw[None]

    return (summed["loss"].reshape(()), grad_x[None], *[grads[n] for n in WEIGHTS], *[delta[n] for n in WEIGHTS],
            *[new_m[n] for n in WEIGHTS], *[new_v[n] for n in WEIGHTS])
```

```python
import math

import jax
import jax.numpy as jnp
from jax import lax
from jax.experimental import pallas as pl
from jax.experimental.pallas import tpu as pltpu

F32 = jnp.float32
BF16 = jnp.bfloat16

N_DEV = 8
D_MODEL = 2048
S5_WIDTH = 1024
S5_GROUP = 16
S5_GROUPS = 64
S5_STATE = 64
S5_MAX_RE = -1e-4
S5_SUPER = 8
S5_LANES = S5_SUPER * S5_STATE
HGRN_WIDTH = 1024
HGRN_HEADS = 8
HGRN_DH = 128
HGRN_CHUNK = 64
D_FF = 5632
RMS_EPS = 1e-6
ADAM_LR = 0.001
ADAM_B1 = 0.9
ADAM_B2 = 0.999
ADAM_EPS = 1e-08
ADAM_WD = 0.01
ADAM_STEP = 10

LANE = 128
SUBLANE = 8
VMEM_LIMIT = 48 * 1024 * 1024
MESH = pl.DeviceIdType.MESH
GELU_C = math.sqrt(2.0 / math.pi)
GELU_A = 0.044715


def _params(sem=None):
    return pltpu.CompilerParams(dimension_semantics=sem, vmem_limit_bytes=VMEM_LIMIT)


def _pick(n, cap, unit=LANE):
    best = None
    for t in range(unit, min(n, cap) + 1, unit):
        if n % t == 0:
            best = t
    return best if best is not None else n


def _ordered(body, in_specs, args, after):
    if after is None:
        return body, list(in_specs), list(args)
    n_in = len(args)

    def ordered_body(*refs):
        return body(*refs[:n_in], *refs[n_in + 1:])

    return ordered_body, [*in_specs, pl.BlockSpec(memory_space=pl.ANY)], [*args, after]


def _sigmoid(x):
    return 0.5 * jnp.tanh(0.5 * x) + 0.5


def _silu_and_grad(x):
    s = _sigmoid(x)
    return x * s, s * (1.0 + x * (1.0 - s))


def _gelu_and_grad(y):
    inner = GELU_C * (y + GELU_A * y * y * y)
    th = jnp.tanh(inner)
    val = 0.5 * y * (1.0 + th)
    grad = 0.5 * (1.0 + th) + 0.5 * y * (1.0 - th * th) * GELU_C * (1.0 + 3.0 * GELU_A * y * y)
    return val, grad


def _dot(a, b):
    return jnp.dot(a, b, preferred_element_type=F32)


def _dot_nt(a, b):
    return lax.dot_general(a, b, (((1,), (1,)), ((), ())), preferred_element_type=F32)


def _dot_tn(a, b):
    return lax.dot_general(a, b, (((0,), (0,)), ((), ())), preferred_element_type=F32)


def _blocks_per_step(nb, ns, tn, cap=2048):
    if tn != ns:
        return 1
    best = 1
    for b in range(1, nb + 1):
        if nb % b == 0 and b * ns <= cap:
            best = b
    return best


def _mm_nn(a, w, name, res=None, out_dtype=F32, after=None):
    m, kdim = a.shape
    nb, _, ns = w.shape
    tm, tk, tn = _pick(m, 512), _pick(kdim, 2048), _pick(ns, 1536)
    npb, nk = ns // tn, kdim // tk
    bps = _blocks_per_step(nb, ns, tn)
    assert bps == 1 or nk == 1

    def body(*refs):
        a_ref, w_ref = refs[0], refs[1]
        r_ref = refs[2] if res is not None else None
        o_ref = refs[3] if res is not None else refs[2]

        def finish(r, cols):
            if res is not None:
                r = r + r_ref[:, cols]
            o_ref[:, cols] = r.astype(out_dtype)

        if nk == 1:
            for b in range(bps):
                finish(_dot(a_ref[...], w_ref[b]), slice(b * tn, (b + 1) * tn))
            return
        acc = refs[-1]
        k = pl.program_id(2)

        @pl.when(k == 0)
        def _():
            acc[...] = jnp.zeros_like(acc)

        acc[...] += _dot(a_ref[...], w_ref[0])

        @pl.when(k == nk - 1)
        def _():
            finish(acc[...], slice(0, tn))

    in_specs = [pl.BlockSpec((tm, tk), lambda j, i, k: (i, k)),
                pl.BlockSpec((bps, tk, tn), lambda j, i, k: (j // npb, k, j % npb))]
    args = [a, w]
    if res is not None:
        in_specs.append(pl.BlockSpec((tm, bps * tn), lambda j, i, k: (i, j)))
        args.append(res)
    body, in_specs, args = _ordered(body, in_specs, args, after)
    return pl.pallas_call(
        body, name=name, grid=(nb * npb // bps, m // tm, nk),
        in_specs=in_specs, out_specs=pl.BlockSpec((tm, bps * tn), lambda j, i, k: (i, j)),
        out_shape=jax.ShapeDtypeStruct((m, nb * ns), out_dtype),
        scratch_shapes=[pltpu.VMEM((tm, tn), F32)] if nk > 1 else [],
        compiler_params=_params(("parallel", "parallel", "arbitrary")),
    )(*args)


def _mm_nt(a, w, name, out_dtype=F32, after=None):
    m, _ = a.shape
    nb, kdim, ns = w.shape
    tm, tko, tn = _pick(m, 1024), _pick(kdim, 1024), _pick(ns, 2048)
    npb = ns // tn
    bps = _blocks_per_step(nb, ns, tn)
    nred = nb * npb // bps

    def body(a_ref, w_ref, o_ref, *scratch):
        total = _dot_nt(a_ref[:, 0:tn], w_ref[0])
        for b in range(1, bps):
            total = total + _dot_nt(a_ref[:, b * tn:(b + 1) * tn], w_ref[b])
        if nred == 1:
            o_ref[...] = total.astype(out_dtype)
            return
        acc = scratch[0]
        n = pl.program_id(2)

        @pl.when(n == 0)
        def _():
            acc[...] = jnp.zeros_like(acc)

        acc[...] += total

        @pl.when(n == nred - 1)
        def _():
            o_ref[...] = acc[...].astype(out_dtype)

    in_specs = [pl.BlockSpec((tm, bps * tn), lambda i, j, n: (i, n)),
                pl.BlockSpec((bps, tko, tn), lambda i, j, n: (n // npb, j, n % npb))]
    body, in_specs, args = _ordered(body, in_specs, [a, w], after)
    return pl.pallas_call(
        body, name=name, grid=(m // tm, kdim // tko, nred),
        in_specs=in_specs,
        out_specs=pl.BlockSpec((tm, tko), lambda i, j, n: (i, j)),
        out_shape=jax.ShapeDtypeStruct((m, kdim), out_dtype),
        scratch_shapes=[pltpu.VMEM((tm, tko), F32)] if nred > 1 else [],
        compiler_params=_params(("parallel", "parallel", "arbitrary")),
    )(*args)


def _mm_tn(a, d, nb, name, out_dtype=BF16, after=None):
    m, kdim = a.shape
    ns = d.shape[1] // nb
    tm, tko, tn = _pick(m, 4096), _pick(kdim, 512), _pick(ns, 1536)
    npb, nm = ns // tn, m // tm

    def body(a_ref, d_ref, o_ref, *scratch):
        if nm == 1:
            o_ref[...] = _dot_tn(a_ref[...], d_ref[...]).astype(out_dtype)
            return
        acc = scratch[0]
        r = pl.program_id(2)

        @pl.when(r == 0)
        def _():
            acc[...] = jnp.zeros_like(acc)

        acc[...] += _dot_tn(a_ref[...], d_ref[...])

        @pl.when(r == nm - 1)
        def _():
            o_ref[...] = acc[...].astype(out_dtype)

    in_specs = [pl.BlockSpec((tm, tko), lambda j, i, r: (r, i)), pl.BlockSpec((tm, tn), lambda j, i, r: (r, j))]
    body, in_specs, args = _ordered(body, in_specs, [a, d], after)
    return pl.pallas_call(
        body, name=name, grid=(nb * npb, kdim // tko, nm),
        in_specs=in_specs,
        out_specs=pl.BlockSpec((None, tko, tn), lambda j, i, r: (j // npb, i, j % npb)),
        out_shape=jax.ShapeDtypeStruct((nb, kdim, ns), out_dtype),
        scratch_shapes=[pltpu.VMEM((tko, tn), F32)] if nm > 1 else [],
        compiler_params=_params(("parallel", "parallel", "arbitrary")),
    )(*args)


def _rms_fwd(x, g, name, after=None):
    t, d = x.shape
    tr = _pick(t, 256, SUBLANE)

    def body(x_ref, g_ref, h_ref):
        xv = x_ref[...]
        r = lax.rsqrt(jnp.mean(xv * xv, axis=-1, keepdims=True) + RMS_EPS)
        h_ref[...] = (xv * r * g_ref[...]).astype(BF16)

    in_specs = [pl.BlockSpec((tr, d), lambda i: (i, 0)), pl.BlockSpec((1, d), lambda i: (0, 0))]
    body, in_specs, args = _ordered(body, in_specs, [x, g], after)
    return pl.pallas_call(
        body, name=name, grid=(t // tr,),
        in_specs=in_specs,
        out_specs=pl.BlockSpec((tr, d), lambda i: (i, 0)),
        out_shape=jax.ShapeDtypeStruct((t, d), BF16),
        compiler_params=_params(("parallel",)),
    )(*args)


def _rms_bwd(x, g, dh, add, name, want_bf16, after=None):
    t, d = x.shape
    tr = _pick(t, 256, SUBLANE)

    def body(x_ref, g_ref, dh_ref, add_ref, *outs):
        if want_bf16:
            dx_ref, dxb_ref, dg_ref = outs
        else:
            dx_ref, dg_ref = outs
        i = pl.program_id(0)

        @pl.when(i == 0)
        def _():
            dg_ref[...] = jnp.zeros_like(dg_ref)

        xv, dhv = x_ref[...], dh_ref[...]
        r = lax.rsqrt(jnp.mean(xv * xv, axis=-1, keepdims=True) + RMS_EPS)
        xh = xv * r
        dg_ref[...] += jnp.sum(dhv * xh, axis=0, keepdims=True)
        dxh = dhv * g_ref[...]
        dx = add_ref[...] + r * (dxh - xh * jnp.mean(dxh * xh, axis=-1, keepdims=True))
        dx_ref[...] = dx
        if want_bf16:
            dxb_ref[...] = dx.astype(BF16)

    row = pl.BlockSpec((tr, d), lambda i: (i, 0))
    vec = pl.BlockSpec((1, d), lambda i: (0, 0))
    out_specs = [row] + ([row] if want_bf16 else []) + [vec]
    out_shape = ([jax.ShapeDtypeStruct((t, d), F32)] + ([jax.ShapeDtypeStruct((t, d), BF16)] if want_bf16 else [])
                 + [jax.ShapeDtypeStruct((1, d), F32)])
    body, in_specs, args = _ordered(body, [row, vec, row, row], [x, g, dh, add], after)
    return pl.pallas_call(
        body, name=name, grid=(t // tr,),
        in_specs=in_specs, out_specs=out_specs, out_shape=out_shape,
        compiler_params=_params(("arbitrary",)),
    )(*args)


def _loss_head(x2, g, target, name="loss_head"):
    t, d = x2.shape
    tr = _pick(t, 256, SUBLANE)

    def body(x_ref, g_ref, t_ref, dx_ref, dxb_ref, dg_ref, loss_ref):
        i = pl.program_id(0)

        @pl.when(i == 0)
        def _():
            dg_ref[...] = jnp.zeros_like(dg_ref)
            loss_ref[...] = jnp.zeros_like(loss_ref)

        xv = x_ref[...]
        gv = g_ref[...]
        r = lax.rsqrt(jnp.mean(xv * xv, axis=-1, keepdims=True) + RMS_EPS)
        xh = xv * r
        err = xh * gv - t_ref[...]
        part = 0.5 * jnp.sum(jnp.mean(err * err, axis=-1, keepdims=True), axis=0, keepdims=True)
        loss_ref[...] += jnp.broadcast_to(part, loss_ref.shape)
        dy = err * (1.0 / d)
        dg_ref[...] += jnp.sum(dy * xh, axis=0, keepdims=True)
        dxh = dy * gv
        dx = r * (dxh - xh * jnp.mean(dxh * xh, axis=-1, keepdims=True))
        dx_ref[...] = dx
        dxb_ref[...] = dx.astype(BF16)

    row = pl.BlockSpec((tr, d), lambda i: (i, 0))
    vec = pl.BlockSpec((1, d), lambda i: (0, 0))
    return pl.pallas_call(
        body, name=name, grid=(t // tr,),
        in_specs=[row, vec, row],
        out_specs=[row, row, vec, pl.BlockSpec((1, LANE), lambda i: (0, 0))],
        out_shape=[jax.ShapeDtypeStruct((t, d), F32), jax.ShapeDtypeStruct((t, d), BF16),
                   jax.ShapeDtypeStruct((1, d), F32), jax.ShapeDtypeStruct((1, LANE), F32)],
        compiler_params=_params(("arbitrary",)),
    )(x2, g, target)


def _s5_discretize(a_re, a_im, ldt):
    lam_re = jnp.minimum(a_re, S5_MAX_RE)
    lam_im = a_im
    dt = jnp.exp(ldt)
    mag = jnp.exp(lam_re * dt)
    abar_re = mag * jnp.cos(lam_im * dt)
    abar_im = mag * jnp.sin(lam_im * dt)
    den = lam_re * lam_re + lam_im * lam_im
    nr = abar_re - 1.0
    ni = abar_im
    coef_re = (nr * lam_re + ni * lam_im) / den
    coef_im = (ni * lam_re - nr * lam_im) / den
    return abar_re, abar_im, coef_re, coef_im


def _s5_param_fwd(a_re, a_im, ldt):
    def body(ar_ref, ai_ref, l_ref, o0, o1, o2, o3):
        outs = _s5_discretize(ar_ref[...], ai_ref[...], l_ref[...])
        for o, v in zip((o0, o1, o2, o3), outs):
            o[...] = v

    sh = jax.ShapeDtypeStruct(a_re.shape, F32)
    return pl.pallas_call(body, name="s5_param_fwd", out_shape=[sh, sh, sh, sh], compiler_params=_params())(a_re, a_im, ldt)


def _s5_param_bwd(a_re, a_im, ldt, cts):
    def body(ar_ref, ai_ref, l_ref, c0, c1, c2, c3, g0, g1, g2):
        _, vjp = jax.vjp(_s5_discretize, ar_ref[...], ai_ref[...], l_ref[...])
        ga, gb, gl = vjp((c0[...], c1[...], c2[...], c3[...]))
        g0[...] = ga
        g1[...] = gb
        g2[...] = gl

    sh = jax.ShapeDtypeStruct(a_re.shape, F32)
    return pl.pallas_call(body, name="s5_param_bwd", out_shape=[sh, sh, jax.ShapeDtypeStruct(ldt.shape, F32)],
                          compiler_params=_params())(a_re, a_im, ldt, *cts)


def _cmul(ar, ai, br, bi):
    return ar * br - ai * bi, ar * bi + ai * br


S5_TC = 128
S5_TILE = S5_SUPER * SUBLANE
S5_HALF = S5_TILE // 2


def _s5_to_tile(re, im):
    f = lambda a: a.reshape(S5_SUPER, S5_LANES // LANE, LANE).transpose(1, 0, 2).reshape(S5_HALF, LANE)
    return jnp.concatenate([f(re), f(im)], axis=0)


def _s5_from_tile(tile):
    f = lambda a: a.reshape(S5_LANES // LANE, S5_SUPER, LANE).transpose(1, 0, 2).reshape(S5_GROUPS, S5_STATE)
    return f(tile[0:S5_HALF]), f(tile[S5_HALF:])


RE = slice(0, S5_HALF)
IM = slice(S5_HALF, S5_TILE)


def _s5_scatter_rows(buf, rows, first_tile=0):
    tc = rows[0].shape[0]
    for j in range(SUBLANE):
        stacked = jnp.stack([r[:, j * LANE:(j + 1) * LANE] for r in rows], axis=0)
        buf[first_tile:first_tile + tc, j * SUBLANE:(j + 1) * SUBLANE, :] = jnp.swapaxes(stacked, 0, 1)


def _s5_gather_rows(buf, tc, first_tile=0):
    per_j = [jnp.swapaxes(buf[first_tile:first_tile + tc, j * SUBLANE:(j + 1) * SUBLANE, :], 0, 1)
             for j in range(SUBLANE)]
    return [jnp.concatenate([per_j[j][k] for j in range(SUBLANE)], axis=1) for k in range(S5_SUPER)]


def _s5_fwd(proj, bsg, ccat, d_row, abar_t, coef_t):
    t = proj.shape[0]
    tc = min(t, S5_TC)
    n_chunks = t // tc

    def body(u_ref, b_ref, c_ref, d_ref, a_ref, cf_ref, y_ref, sb_ref, x, car):
        @pl.when(pl.program_id(0) == 0)
        def _():
            car[...] = jnp.zeros_like(car)

        sb_ref[...] = car[...]
        u = u_ref[...]
        _s5_scatter_rows(x, [_dot(u[:, k * LANE:(k + 1) * LANE].astype(BF16), b_ref[k]) for k in range(S5_SUPER)])
        ar, ai = a_ref[RE, :], a_ref[IM, :]
        cr, ci = cf_ref[RE, :], cf_ref[IM, :]

        def step(i, carry):
            sr, si = carry
            xr, xi = _cmul(cr, ci, x[i, RE, :], x[i, IM, :])
            sr, si = ar * sr - ai * si + xr, ar * si + ai * sr + xi
            x[i, RE, :] = sr
            x[i, IM, :] = si
            return sr, si

        sr, si = lax.fori_loop(0, tc, step, (car[RE, :], car[IM, :]), unroll=4)
        car[RE, :] = sr
        car[IM, :] = si
        for k, s_k in enumerate(_s5_gather_rows(x, tc)):
            cols = slice(k * LANE, (k + 1) * LANE)
            y_ref[:, cols] = _dot(s_k.astype(BF16), c_ref[k]) + d_ref[:, cols] * u[:, cols]

    full = lambda shape: pl.BlockSpec(shape, lambda c: (0,) * len(shape))
    return pl.pallas_call(
        body, name="s5_fwd", grid=(n_chunks,),
        in_specs=[pl.BlockSpec((tc, S5_WIDTH), lambda c: (c, 0)), full(bsg.shape), full(ccat.shape), full(d_row.shape),
                  full(abar_t.shape), full(coef_t.shape)],
        out_specs=[pl.BlockSpec((tc, S5_WIDTH), lambda c: (c, 0)), pl.BlockSpec((None, S5_TILE, LANE), lambda c: (c, 0, 0))],
        out_shape=[jax.ShapeDtypeStruct((t, S5_WIDTH), F32), jax.ShapeDtypeStruct((n_chunks, S5_TILE, LANE), F32)],
        scratch_shapes=[pltpu.VMEM((tc, S5_TILE, LANE), F32), pltpu.VMEM((S5_TILE, LANE), F32)],
        compiler_params=_params(("arbitrary",)),
    )(proj, bsg, ccat, d_row, abar_t, coef_t)


def _s5_bwd(proj, dy, sb, bsg, ccat, d_row, abar_t, coef_t):
    t = proj.shape[0]
    tc = min(t, S5_TC)
    n_chunks = t // tc
    last = n_chunks - 1

    def body(u_ref, dy_ref, sb_ref, b_ref, c_ref, d_ref, a_ref, cf_ref,
             du_ref, gb_ref, gc_ref, gd_ref, ga_ref, gcf_ref, xb, xs, xg, gcar, acc):
        @pl.when(pl.program_id(0) == 0)
        def _():
            gcar[...] = jnp.zeros_like(gcar)
            acc[...] = jnp.zeros_like(acc)
            gb_ref[...] = jnp.zeros_like(gb_ref)
            gc_ref[...] = jnp.zeros_like(gc_ref)
            gd_ref[...] = jnp.zeros_like(gd_ref)

        u = u_ref[...]
        dyv = dy_ref[...]
        u16, dy16 = u.astype(BF16), dyv.astype(BF16)
        subs = [slice(k * LANE, (k + 1) * LANE) for k in range(S5_SUPER)]
        _s5_scatter_rows(xb, [_dot(u16[:, c], b_ref[k]) for k, c in enumerate(subs)])
        _s5_scatter_rows(xg, [_dot_nt(dy16[:, c], c_ref[k]) for k, c in enumerate(subs)])
        ar, ai = a_ref[RE, :], a_ref[IM, :]
        cr, ci = cf_ref[RE, :], cf_ref[IM, :]

        xs[0] = sb_ref[...]

        def fstep(i, carry):
            sr, si = carry
            xr, xi = _cmul(cr, ci, xb[i, RE, :], xb[i, IM, :])
            sr, si = ar * sr - ai * si + xr, ar * si + ai * sr + xi
            xs[i + 1, RE, :] = sr
            xs[i + 1, IM, :] = si
            return sr, si

        lax.fori_loop(0, tc, fstep, (sb_ref[RE, :], sb_ref[IM, :]), unroll=4)

        def rstep(n, carry):
            gr, gi, a0, a1, a2, a3 = carry
            i = tc - 1 - n
            xr = xg[i, RE, :] + ar * gr + ai * gi
            xi = xg[i, IM, :] + ar * gi - ai * gr
            pr, pi = xs[i, RE, :], xs[i, IM, :]
            br, bi = xb[i, RE, :], xb[i, IM, :]
            a0 = a0 + pr * xr + pi * xi
            a1 = a1 + pr * xi - pi * xr
            a2 = a2 + br * xr + bi * xi
            a3 = a3 + br * xi - bi * xr
            xg[i, RE, :] = cr * xr + ci * xi
            xg[i, IM, :] = cr * xi - ci * xr
            return xr, xi, a0, a1, a2, a3

        init = (gcar[RE, :], gcar[IM, :], acc[0], acc[1], acc[2], acc[3])
        gr, gi, a0, a1, a2, a3 = lax.fori_loop(0, tc, rstep, init, unroll=2)
        gcar[RE, :] = gr
        gcar[IM, :] = gi
        for idx, a in enumerate((a0, a1, a2, a3)):
            acc[idx] = a
        ga_ref[RE, :] = a0
        ga_ref[IM, :] = a1
        gcf_ref[RE, :] = a2
        gcf_ref[IM, :] = a3

        g_rows = _s5_gather_rows(xg, tc)
        s_rows = _s5_gather_rows(xs, tc, first_tile=1)
        for k in range(S5_SUPER):
            cols = subs[k]
            g16 = g_rows[k].astype(BF16)
            s16 = s_rows[k].astype(BF16)
            gb_ref[k] += _dot_tn(u16[:, cols], g16)
            gc_ref[k] += _dot_tn(s16, dy16[:, cols])
            du_ref[:, cols] = (_dot_nt(g16, b_ref[k]) + d_ref[:, cols] * dyv[:, cols]).astype(BF16)
        gd_ref[...] += jnp.sum(dyv * u, axis=0, keepdims=True)

    full = lambda shape: pl.BlockSpec(shape, lambda c: (0,) * len(shape))
    rows = pl.BlockSpec((tc, S5_WIDTH), lambda c: (last - c, 0))
    tile = (S5_TILE, LANE)
    return pl.pallas_call(
        body, name="s5_bwd", grid=(n_chunks,),
        in_specs=[rows, rows, pl.BlockSpec((None, S5_TILE, LANE), lambda c: (last - c, 0, 0)),
                  full(bsg.shape), full(ccat.shape), full(d_row.shape), full(abar_t.shape), full(coef_t.shape)],
        out_specs=[rows, full(bsg.shape), full(ccat.shape), full(d_row.shape), full(tile), full(tile)],
        out_shape=[jax.ShapeDtypeStruct((t, S5_WIDTH), BF16), jax.ShapeDtypeStruct(bsg.shape, F32),
                   jax.ShapeDtypeStruct(ccat.shape, F32), jax.ShapeDtypeStruct(d_row.shape, F32),
                   jax.ShapeDtypeStruct(tile, F32), jax.ShapeDtypeStruct(tile, F32)],
        scratch_shapes=[pltpu.VMEM((tc, S5_TILE, LANE), F32), pltpu.VMEM((tc + 1, S5_TILE, LANE), F32),
                        pltpu.VMEM((tc, S5_TILE, LANE), F32), pltpu.VMEM(tile, F32),
                        pltpu.VMEM((4, S5_HALF, LANE), F32)],
        compiler_params=_params(("arbitrary",)),
    )(proj, dy, sb, bsg, ccat, d_row, abar_t, coef_t)


def _gelu_fwd(y, name="s5_gelu"):
    t, w = y.shape
    tr = _pick(t, 512, SUBLANE)

    def body(y_ref, z_ref):
        z_ref[...] = _gelu_and_grad(y_ref[...])[0].astype(BF16)

    row = pl.BlockSpec((tr, w), lambda i: (i, 0))
    return pl.pallas_call(body, name=name, grid=(t // tr,), in_specs=[row], out_specs=row,
                          out_shape=jax.ShapeDtypeStruct((t, w), BF16), compiler_params=_params(("parallel",)))(y)


def _glu_fwd(y, gl, b, name="s5_glu"):
    t, w = y.shape
    tr = _pick(t, 512, SUBLANE)

    def body(y_ref, gl_ref, b_ref, z2_ref):
        z = _gelu_and_grad(y_ref[...])[0]
        z2_ref[...] = (z * _sigmoid(gl_ref[...] + b_ref[...])).astype(BF16)

    row = pl.BlockSpec((tr, w), lambda i: (i, 0))
    return pl.pallas_call(body, name=name, grid=(t // tr,),
                          in_specs=[row, row, pl.BlockSpec((1, w), lambda i: (0, 0))], out_specs=row,
                          out_shape=jax.ShapeDtypeStruct((t, w), BF16), compiler_params=_params(("parallel",)))(y, gl, b)


def _glu_bwd(y, gl, b, dz2, name="s5_glu_bwd", after=None):
    t, w = y.shape
    tr = _pick(t, 512, SUBLANE)

    def body(y_ref, gl_ref, b_ref, dz2_ref, dgl_ref, dza_ref, db_ref):
        @pl.when(pl.program_id(0) == 0)
        def _():
            db_ref[...] = jnp.zeros_like(db_ref)

        z = _gelu_and_grad(y_ref[...])[0]
        s = _sigmoid(gl_ref[...] + b_ref[...])
        dz2v = dz2_ref[...]
        dgl = dz2v * z * s * (1.0 - s)
        dgl_ref[...] = dgl.astype(BF16)
        dza_ref[...] = dz2v * s
        db_ref[...] += jnp.sum(dgl, axis=0, keepdims=True)

    row = pl.BlockSpec((tr, w), lambda i: (i, 0))
    vec = pl.BlockSpec((1, w), lambda i: (0, 0))
    body, in_specs, args = _ordered(body, [row, row, vec, row], [y, gl, b, dz2], after)
    return pl.pallas_call(body, name=name, grid=(t // tr,), in_specs=in_specs, out_specs=[row, row, vec],
                          out_shape=[jax.ShapeDtypeStruct((t, w), BF16), jax.ShapeDtypeStruct((t, w), F32),
                                     jax.ShapeDtypeStruct((1, w), F32)],
                          compiler_params=_params(("arbitrary",)))(*args)


def _gelu_bwd(y, dza, dzb, name="s5_gelu_bwd", after=None):
    t, w = y.shape
    tr = _pick(t, 512, SUBLANE)

    def body(y_ref, a_ref, b_ref, dy_ref):
        dy_ref[...] = (a_ref[...] + b_ref[...]) * _gelu_and_grad(y_ref[...])[1]

    row = pl.BlockSpec((tr, w), lambda i: (i, 0))
    body, in_specs, args = _ordered(body, [row, row, row], [y, dza, dzb], after)
    return pl.pallas_call(body, name=name, grid=(t // tr,), in_specs=in_specs, out_specs=row,
                          out_shape=jax.ShapeDtypeStruct((t, w), F32), compiler_params=_params(("parallel",)))(*args)


def _tri_dot(tri16, x):
    hi = x.astype(BF16)
    r1 = x - hi.astype(F32)
    mid = r1.astype(BF16)
    lo = (r1 - mid.astype(F32)).astype(BF16)
    return _dot(tri16, hi) + _dot(tri16, mid) + _dot(tri16, lo)


def _hgrn_pre(q_in, z, lg):
    lb = _sigmoid(lg[0:1, :] - lg[1:2, :])
    qs, dqs = _silu_and_grad(q_in)
    sz = _sigmoid(z)
    f = lb + (1.0 - lb) * sz
    k = (1.0 - lb) * (1.0 - sz)
    c = HGRN_CHUNK
    r = lax.broadcasted_iota(jnp.int32, (c, c), 0)
    s = lax.broadcasted_iota(jnp.int32, (c, c), 1)
    causal = r >= s
    b = _tri_dot(jnp.where(causal, 1.0, 0.0).astype(BF16), jnp.log(f))
    b_end = b[c - 1:c, :]
    b_mid = b[c // 2 - 1:c // 2, :]
    e_q, e_k, e_0, e_c = jnp.exp(b - b_mid), jnp.exp(b_mid - b), jnp.exp(b), jnp.exp(b_end - b)
    return dict(lb=lb, qs=qs, dqs=dqs, sz=sz, f=f, k=k, causal=causal, b_end=b_end,
                e_q=e_q, e_k=e_k, e_0=e_0, e_c=e_c,
                qt=qs * e_q, kt=k * e_k, q0=qs * e_0, kc=k * e_c)


def _hgrn_fwd(proj, logits, ng):
    t = proj.shape[0]
    c, dh = HGRN_CHUNK, HGRN_DH
    n_chunks = t // c

    def head(h, q_ref, z_ref, v_ref, g_ref, lg_ref, ng_ref, o_ref, oh_ref, s0_ref, st):
        sl = slice(h * dh, (h + 1) * dh)
        s0 = st[h]
        s0_ref[h] = s0
        p = _hgrn_pre(q_ref[:, sl], z_ref[:, sl], lg_ref[:, sl])
        v16 = v_ref[:, sl].astype(BF16)
        a = jnp.where(p["causal"], _dot_nt(p["qt"].astype(BF16), p["kt"].astype(BF16)), 0.0)
        o = _dot_nt(p["q0"].astype(BF16), s0.astype(BF16)) + _dot(a.astype(BF16), v16)
        st[h] = jnp.exp(p["b_end"]) * s0 + _dot_tn(v16, p["kc"].astype(BF16))
        o_ref[:, sl] = o
        rn = lax.rsqrt(jnp.mean(o * o, axis=-1, keepdims=True) + RMS_EPS)
        oh_ref[:, sl] = (o * rn * ng_ref[:, sl] * _silu_and_grad(g_ref[:, sl])[0]).astype(BF16)

    def body(*refs):
        st = refs[-1]

        @pl.when(pl.program_id(0) == 0)
        def _():
            st[...] = jnp.zeros_like(st)

        for h in range(HGRN_HEADS):
            head(h, *refs)

    def wide(off):
        return pl.BlockSpec((c, HGRN_WIDTH), lambda i: (i, off))

    return pl.pallas_call(
        body, name="hgrn_fwd", grid=(n_chunks,),
        in_specs=[wide(1), wide(2), wide(3), wide(4),
                  pl.BlockSpec((2, HGRN_WIDTH), lambda i: (0, 0)), pl.BlockSpec((1, HGRN_WIDTH), lambda i: (0, 0))],
        out_specs=[wide(0), wide(0), pl.BlockSpec((HGRN_HEADS, None, dh, dh), lambda i: (0, i, 0, 0))],
        out_shape=[jax.ShapeDtypeStruct((t, HGRN_WIDTH), F32), jax.ShapeDtypeStruct((t, HGRN_WIDTH), BF16),
                   jax.ShapeDtypeStruct((HGRN_HEADS, n_chunks, dh, dh), F32)],
        scratch_shapes=[pltpu.VMEM((HGRN_HEADS, dh, dh), F32)],
        compiler_params=_params(("arbitrary",)),
    )(proj, proj, proj, proj, logits, ng)


def _hgrn_bwd(proj, o_raw, s0s, doh, logits, ng):
    t = proj.shape[0]
    c, dh = HGRN_CHUNK, HGRN_DH
    n_chunks = t // c
    last = n_chunks - 1

    def head(h, q_ref, z_ref, v_ref, g_ref, o_ref, s0_ref, doh_ref, lg_ref, ng_ref,
             dq_ref, dz_ref, dv_ref, dg_ref, dng_ref, dlb_ref, dst):
        sl = slice(h * dh, (h + 1) * dh)
        p = _hgrn_pre(q_ref[:, sl], z_ref[:, sl], lg_ref[:, sl])
        v = v_ref[:, sl]
        v16 = v.astype(BF16)
        s0 = s0_ref[h]
        ds_end = dst[h]
        ds16 = ds_end.astype(BF16)
        ngv = ng_ref[:, sl]

        o = o_ref[:, sl]
        dohv = doh_ref[:, sl]
        sg, dsg = _silu_and_grad(g_ref[:, sl])
        rn = lax.rsqrt(jnp.mean(o * o, axis=-1, keepdims=True) + RMS_EPS)
        oh = o * rn
        dg_ref[:, sl] = (dohv * oh * ngv * dsg).astype(BF16)
        don = dohv * sg
        dng_ref[:, sl] += jnp.sum(don * oh, axis=0, keepdims=True)
        doh_n = don * ngv
        do = rn * (doh_n - oh * jnp.mean(doh_n * oh, axis=-1, keepdims=True))
        do16 = do.astype(BF16)

        qt16, kt16, q016, kc16 = (p[n].astype(BF16) for n in ("qt", "kt", "q0", "kc"))
        a = jnp.where(p["causal"], _dot_nt(qt16, kt16), 0.0)
        da = jnp.where(p["causal"], _dot_nt(do16, v16), 0.0)
        da16 = da.astype(BF16)
        dqt = _dot(da16, kt16)
        dq0 = _dot(do16, s0.astype(BF16))
        dkt = _dot_tn(da16, qt16)
        dkc = _dot(v16, ds16)
        dv_ref[:, sl] = (_dot_tn(a.astype(BF16), do16) + _dot_nt(kc16, ds16)).astype(BF16)
        lam_end = jnp.exp(p["b_end"])
        dst[h] = lam_end * ds_end + _dot_tn(do16, q016)

        qt, kt, q0, kc = (a.astype(F32) for a in (qt16, kt16, q016, kc16))
        db = dqt * qt + dq0 * q0 - dkt * kt - dkc * kc
        db_end = (jnp.sum(dkc * kc, axis=0, keepdims=True)
                  + jnp.sum(ds_end * s0, axis=0, keepdims=True) * lam_end)
        rowi = lax.broadcasted_iota(jnp.int32, (c, dh), 0)
        db = db + jnp.where(rowi == c - 1, db_end, 0.0)
        r = lax.broadcasted_iota(jnp.int32, (c, c), 0)
        s = lax.broadcasted_iota(jnp.int32, (c, c), 1)
        dlf = _tri_dot(jnp.where(s >= r, 1.0, 0.0).astype(BF16), db)

        dqs = dqt * p["e_q"] + dq0 * p["e_0"]
        dq_ref[:, sl] = (dqs * p["dqs"]).astype(BF16)
        dk = dkt * p["e_k"] + dkc * p["e_c"]
        sz, lb = p["sz"], p["lb"]
        common = dlf / p["f"] - dk
        dz_ref[:, sl] = ((1.0 - lb) * sz * (1.0 - sz) * common).astype(BF16)
        dlb_ref[:, sl] += jnp.sum((1.0 - sz) * common, axis=0, keepdims=True)

    def body(*refs):
        dng_ref, dlb_ref, dst = refs[-3:]

        @pl.when(pl.program_id(0) == 0)
        def _():
            dst[...] = jnp.zeros_like(dst)
            dng_ref[...] = jnp.zeros_like(dng_ref)
            dlb_ref[...] = jnp.zeros_like(dlb_ref)

        for h in range(HGRN_HEADS):
            head(h, *refs)

    def wide(off):
        return pl.BlockSpec((c, HGRN_WIDTH), lambda i: (last - i, off))

    vec = pl.BlockSpec((1, HGRN_WIDTH), lambda i: (0, 0))
    act = jax.ShapeDtypeStruct((t, HGRN_WIDTH), BF16)
    vsh = jax.ShapeDtypeStruct((1, HGRN_WIDTH), F32)
    return pl.pallas_call(
        body, name="hgrn_bwd", grid=(n_chunks,),
        in_specs=[wide(1), wide(2), wide(3), wide(4), wide(0),
                  pl.BlockSpec((HGRN_HEADS, None, dh, dh), lambda i: (0, last - i, 0, 0)),
                  wide(0), pl.BlockSpec((2, HGRN_WIDTH), lambda i: (0, 0)), vec],
        out_specs=[wide(0), wide(0), wide(0), wide(0), vec, vec],
        out_shape=[act, act, act, act, vsh, vsh],
        scratch_shapes=[pltpu.VMEM((HGRN_HEADS, dh, dh), F32)],
        compiler_params=_params(("arbitrary",)),
    )(proj, proj, proj, proj, o_raw, s0s, doh, logits, ng)


def _lb_bwd(logits, dlb):
    def body(lg_ref, d_ref, o_ref):
        lg = lg_ref[...]
        lb = _sigmoid(lg[0:1, :] - lg[1:2, :])
        g = d_ref[...] * lb * (1.0 - lb)
        o_ref[0:1, :] = g
        o_ref[1:2, :] = -g

    return pl.pallas_call(body, name="hgrn_lb_bwd", out_shape=jax.ShapeDtypeStruct(logits.shape, F32),
                          compiler_params=_params())(logits, dlb)


MERGE_TC = 1024
GS_BLOCK = (S5_WIDTH + 4 * HGRN_WIDTH) // MERGE_TC
GH_BLOCK = GS_BLOCK + D_MODEL // MERGE_TC


def _merge_fwd(proj, ys, yh):
    t = proj.shape[0]
    tr = _pick(t, 256, SUBLANE)

    def body(gs_ref, gh_ref, ys_ref, yh_ref, m_ref):
        m_ref[...] = (_sigmoid(gs_ref[...]) * ys_ref[...] + _sigmoid(gh_ref[...]) * yh_ref[...]).astype(BF16)

    blk = pl.BlockSpec((tr, MERGE_TC), lambda i, j: (i, j))
    return pl.pallas_call(
        body, name="merge_fwd", grid=(t // tr, D_MODEL // MERGE_TC),
        in_specs=[pl.BlockSpec((tr, MERGE_TC), lambda i, j: (i, GS_BLOCK + j)),
                  pl.BlockSpec((tr, MERGE_TC), lambda i, j: (i, GH_BLOCK + j)), blk, blk],
        out_specs=blk, out_shape=jax.ShapeDtypeStruct((t, D_MODEL), BF16),
        compiler_params=_params(("parallel", "parallel")),
    )(proj, proj, ys, yh)


def _merge_bwd(proj, ys, yh, dm, after=None):
    t = proj.shape[0]
    tr = _pick(t, 256, SUBLANE)

    def body(gs_ref, gh_ref, ys_ref, yh_ref, dm_ref, dys_ref, dyh_ref, dgs_ref, dgh_ref):
        dmv = dm_ref[...]
        ss, sh = _sigmoid(gs_ref[...]), _sigmoid(gh_ref[...])
        dys_ref[...] = (dmv * ss).astype(BF16)
        dyh_ref[...] = (dmv * sh).astype(BF16)
        dgs_ref[...] = (dmv * ys_ref[...] * ss * (1.0 - ss)).astype(BF16)
        dgh_ref[...] = (dmv * yh_ref[...] * sh * (1.0 - sh)).astype(BF16)

    blk = pl.BlockSpec((tr, MERGE_TC), lambda i, j: (i, j))
    sh16 = jax.ShapeDtypeStruct((t, D_MODEL), BF16)
    in_specs = [pl.BlockSpec((tr, MERGE_TC), lambda i, j: (i, GS_BLOCK + j)),
                pl.BlockSpec((tr, MERGE_TC), lambda i, j: (i, GH_BLOCK + j)), blk, blk, blk]
    body, in_specs, args = _ordered(body, in_specs, [proj, proj, ys, yh, dm], after)
    return pl.pallas_call(
        body, name="merge_bwd", grid=(t // tr, D_MODEL // MERGE_TC),
        in_specs=in_specs,
        out_specs=[blk, blk, blk, blk], out_shape=[sh16, sh16, sh16, sh16],
        compiler_params=_params(("parallel", "parallel")),
    )(*args)


FFN_TC = 128
FFN_ROWS = 128
HALO = SUBLANE


def _pad_rows(dst, src_ref):
    t, c = src_ref.shape
    dst[0:HALO, :] = jnp.zeros((HALO, c), F32)
    dst[HALO:HALO + t, :] = src_ref[...]
    dst[HALO + t:HALO + t + HALO, :] = jnp.zeros((HALO, c), F32)


def _conv3(padded, w, b, r0, nrows):
    x0 = padded[HALO + r0:HALO + r0 + nrows, :]
    x1 = padded[HALO + r0 - 1:HALO + r0 - 1 + nrows, :]
    x2 = padded[HALO + r0 - 2:HALO + r0 - 2 + nrows, :]
    return b + w[0:1, :] * x2 + w[1:2, :] * x1 + w[2:3, :] * x0, (x0, x1, x2)


def _ffn_act_fwd(up, cw, cb):
    t = up.shape[0]
    rows = _pick(t, FFN_ROWS, SUBLANE)
    nvb = D_FF // FFN_TC

    def body(ug_ref, uv_ref, wg_ref, wv_ref, bg_ref, bv_ref, act_ref, pg, pv):
        wg, wv, bg, bv = wg_ref[...], wv_ref[...], bg_ref[...], bv_ref[...]
        _pad_rows(pg, ug_ref)
        _pad_rows(pv, uv_ref)
        for r0 in range(0, t, rows):
            cg, _ = _conv3(pg, wg, bg, r0, rows)
            cv, _ = _conv3(pv, wv, bv, r0, rows)
            act_ref[r0:r0 + rows, :] = (_silu_and_grad(cg)[0] * cv).astype(BF16)

    def colblk(nrow, off):
        return pl.BlockSpec((nrow, FFN_TC), lambda j: (0, off + j))

    return pl.pallas_call(
        body, name="ffn_act_fwd", grid=(nvb,),
        in_specs=[colblk(t, 0), colblk(t, nvb), colblk(3, 0), colblk(3, nvb), colblk(1, 0), colblk(1, nvb)],
        out_specs=colblk(t, 0), out_shape=jax.ShapeDtypeStruct((t, D_FF), BF16),
        scratch_shapes=[pltpu.VMEM((t + 2 * HALO, FFN_TC), F32), pltpu.VMEM((t + 2 * HALO, FFN_TC), F32)],
        compiler_params=_params(("parallel",)),
    )(up, up, cw, cw, cb, cb)


def _ffn_act_bwd(up, dact, cw, cb, after=None):
    t = up.shape[0]
    rows = _pick(t, FFN_ROWS, SUBLANE)
    nvb = D_FF // FFN_TC

    def body(ug_ref, uv_ref, da_ref, wg_ref, wv_ref, bg_ref, bv_ref,
             dug_ref, duv_ref, dwg_ref, dwv_ref, dbg_ref, dbv_ref, pg, pv, dcs):
        wg, wv, bg, bv = wg_ref[...], wv_ref[...], bg_ref[...], bv_ref[...]
        _pad_rows(pg, ug_ref)
        _pad_rows(pv, uv_ref)
        ext = rows + HALO
        acc_g = [jnp.zeros((1, FFN_TC), F32) for _ in range(4)]
        acc_v = [jnp.zeros((1, FFN_TC), F32) for _ in range(4)]
        for r0 in range(0, t, rows):
            cg, xg = _conv3(pg, wg, bg, r0, ext)
            cv, xv = _conv3(pv, wv, bv, r0, ext)
            if r0 + ext <= t:
                dav = da_ref[r0:r0 + ext, :]
            else:
                dav = jnp.concatenate([da_ref[r0:t, :], jnp.zeros((HALO, FFN_TC), F32)], axis=0)
            sg, dsg = _silu_and_grad(cg)
            for h, (dconv, xs, w, acc, out) in enumerate(((dav * cv * dsg, xg, wg, acc_g, dug_ref),
                                                           (dav * sg, xv, wv, acc_v, duv_ref))):
                dcs[h] = dconv
                d0 = dconv[0:rows, :]
                d1 = dcs[h, 1:rows + 1, :]
                d2 = dcs[h, 2:rows + 2, :]
                out[r0:r0 + rows, :] = (w[2:3, :] * d0 + w[1:2, :] * d1 + w[0:1, :] * d2).astype(BF16)
                x0, x1, x2 = xs
                acc[0] = acc[0] + jnp.sum(d0 * x2[0:rows, :], axis=0, keepdims=True)
                acc[1] = acc[1] + jnp.sum(d0 * x1[0:rows, :], axis=0, keepdims=True)
                acc[2] = acc[2] + jnp.sum(d0 * x0[0:rows, :], axis=0, keepdims=True)
                acc[3] = acc[3] + jnp.sum(d0, axis=0, keepdims=True)
        for acc, dw_ref, db_ref in ((acc_g, dwg_ref, dbg_ref), (acc_v, dwv_ref, dbv_ref)):
            dw_ref[0:1, :] = acc[0]
            dw_ref[1:2, :] = acc[1]
            dw_ref[2:3, :] = acc[2]
            db_ref[...] = acc[3]

    def colblk(nrow, off):
        return pl.BlockSpec((nrow, FFN_TC), lambda j: (0, off + j))

    in_specs = [colblk(t, 0), colblk(t, nvb), colblk(t, 0), colblk(3, 0), colblk(3, nvb), colblk(1, 0), colblk(1, nvb)]
    body, in_specs, args = _ordered(body, in_specs, [up, up, dact, cw, cw, cb, cb], after)
    return pl.pallas_call(
        body, name="ffn_act_bwd", grid=(nvb,),
        in_specs=in_specs,
        out_specs=[colblk(t, 0), colblk(t, 0), colblk(3, 0), colblk(3, 0), colblk(1, 0), colblk(1, 0)],
        out_shape=[jax.ShapeDtypeStruct((t, D_FF), BF16), jax.ShapeDtypeStruct((t, D_FF), BF16),
                   jax.ShapeDtypeStruct((3, D_FF), F32), jax.ShapeDtypeStruct((3, D_FF), F32),
                   jax.ShapeDtypeStruct((1, D_FF), F32), jax.ShapeDtypeStruct((1, D_FF), F32)],
        scratch_shapes=[pltpu.VMEM((t + 2 * HALO, FFN_TC), F32), pltpu.VMEM((t + 2 * HALO, FFN_TC), F32),
                        pltpu.VMEM((2, rows + HALO, FFN_TC), F32)],
        compiler_params=_params(("parallel",)),
    )(*args)


def _all_gather(shards, name):
    nw = len(shards)

    def body(*refs):
        x_refs, out_refs = refs[:nw], refs[nw:2 * nw]
        send_sems, recv_sems, local_sems = refs[2 * nw:]
        x, y, c = lax.axis_index("x"), lax.axis_index("y"), lax.axis_index("c")
        me, sibling = (x, y, c), (x, y, 1 - c)
        chips = [(1 - x, y), (x, 1 - y), (1 - x, 1 - y)]

        def copy(w, k, block, to, src=None):
            slot = out_refs[w].at[4 * block[0] + 2 * block[1] + block[2]]
            return pltpu.make_async_remote_copy(
                src_ref=slot if src is None else src, dst_ref=slot,
                send_sem=send_sems.at[w, k], recv_sem=recv_sems.at[w, k],
                device_id=to, device_id_type=MESH)

        mine, first, passed = [], [], []
        for w in range(nw):
            cp = pltpu.make_async_copy(x_refs[w], out_refs[w].at[4 * x + 2 * y + c], local_sems.at[w])
            cp.start()
            mine.append(cp)
            first.append(copy(w, 0, me, sibling, src=x_refs[w]))
            first += [copy(w, 1 + j, me, (*chip, c), src=x_refs[w]) for j, chip in enumerate(chips)]
        for cp in first:
            cp.start()
        for w in range(nw):
            for j, chip in enumerate(chips):
                copy(w, 1 + j, (*chip, c), me).wait_recv()
                fwd = copy(w, 4 + j, (*chip, c), sibling)
                fwd.start()
                passed.append(fwd)
        for w in range(nw):
            copy(w, 0, sibling, me).wait_recv()
            for j, chip in enumerate(chips):
                copy(w, 4 + j, (*chip, 1 - c), me).wait_recv()
        for cp in first + passed:
            cp.wait_send()
        for cp in mine:
            cp.wait()

    anyspec = pl.BlockSpec(memory_space=pl.ANY)
    return pl.pallas_call(
        body, name=name,
        in_specs=[anyspec] * nw, out_specs=[anyspec] * nw,
        out_shape=[jax.ShapeDtypeStruct((N_DEV,) + s.shape, s.dtype) for s in shards],
        scratch_shapes=[pltpu.SemaphoreType.DMA((nw, 7)), pltpu.SemaphoreType.DMA((nw, 7)),
                        pltpu.SemaphoreType.DMA((nw,))],
    )(*shards)


HBM_SPEC = pl.BlockSpec(memory_space=pltpu.HBM)
SEM_SPEC = pl.BlockSpec(memory_space=pltpu.SEMAPHORE)
ANY_SPEC = pl.BlockSpec(memory_space=pl.ANY)
DATAFLOW = pltpu.SideEffectType.DATAFLOW_SIDE_EFFECTING


def _my_index():
    return 4 * lax.axis_index("x") + 2 * lax.axis_index("y") + lax.axis_index("c")


def _peers():
    x, y, c = lax.axis_index("x"), lax.axis_index("y"), lax.axis_index("c")
    peers = []
    for k in range(1, N_DEV):
        px = 1 - x if k & 4 else x
        py = 1 - y if k & 2 else y
        pc = 1 - c if k & 1 else c
        peers.append((k, (px, py, pc), 4 * px + 2 * py + pc))
    return peers


def _split_copy(src_ref, land_ref, send_sems, recv_sems, w, k, peer, slot, scatter, outgoing):
    return pltpu.make_async_remote_copy(
        src_ref=src_ref.at[slot] if scatter else src_ref,
        dst_ref=land_ref.at[_my_index() if outgoing else slot],
        send_sem=send_sems.at[w * (N_DEV - 1) + k - 1], recv_sem=recv_sems.at[w * (N_DEV - 1) + k - 1],
        device_id=peer, device_id_type=MESH)


def _exchange_start(srcs, scatter, after, name):
    nw = len(srcs)
    me = _my_index()
    lands = []
    for s in srcs:
        own = lax.dynamic_index_in_dim(s, me, 0, keepdims=True) if scatter else s[None]
        shape = s.shape if scatter else (N_DEV,) + s.shape
        lands.append(lax.dynamic_update_slice_in_dim(lax.empty(shape, s.dtype), own, me, 0))

    afters = [] if after is None else [after]

    def body(*refs):
        s_refs, l_refs = refs[:nw], refs[nw:2 * nw]
        send_sems, recv_sems = refs[2 * nw + len(afters)], refs[2 * nw + len(afters) + 1]
        token = refs[-1]
        for w in range(nw):
            for k, peer, slot in _peers():
                _split_copy(s_refs[w], l_refs[w], send_sems, recv_sems, w, k, peer, slot, scatter, True).start()
        token[...] = jnp.zeros_like(token)

    sems = pltpu.SemaphoreType.DMA((nw * (N_DEV - 1),))
    outs = pl.pallas_call(
        body, name=name,
        out_shape=(sems, sems, *[pltpu.HBM(a.shape, a.dtype) for a in (*srcs, *lands)],
                   jax.ShapeDtypeStruct((SUBLANE, LANE), F32)),
        in_specs=[HBM_SPEC] * (2 * nw) + [ANY_SPEC] * len(afters),
        out_specs=(SEM_SPEC, SEM_SPEC, *[HBM_SPEC] * (2 * nw), pl.BlockSpec(memory_space=pltpu.VMEM)),
        input_output_aliases={i: 2 + i for i in range(2 * nw)},
        compiler_params=pltpu.CompilerParams(has_side_effects=DATAFLOW),
    )(*[pltpu.with_memory_space_constraint(a, pltpu.HBM) for a in (*srcs, *lands)], *afters)
    return dict(sems=outs[:2], srcs=outs[2:2 + nw], lands=outs[2 + nw:2 + 2 * nw], token=outs[-1], scatter=scatter)


def _exchange_wait(handle, afters, name):
    srcs, lands, scatter = handle["srcs"], handle["lands"], handle["scatter"]
    nw = len(srcs)

    def body(*refs):
        s_refs, l_refs = refs[:nw], refs[nw:2 * nw]
        send_sems, recv_sems = refs[2 * nw], refs[2 * nw + 1]
        for w in range(nw):
            for k, peer, slot in _peers():
                cp = _split_copy(s_refs[w], l_refs[w], send_sems, recv_sems, w, k, peer, slot, scatter, False)
                cp.wait_send()
                cp.wait_recv()

    outs = pl.pallas_call(
        body, name=name,
        out_shape=tuple(pltpu.HBM(a.shape, a.dtype) for a in (*srcs, *lands)),
        in_specs=[HBM_SPEC] * (2 * nw) + [SEM_SPEC, SEM_SPEC] + [ANY_SPEC] * len(afters),
        out_specs=tuple([HBM_SPEC] * (2 * nw)),
        input_output_aliases={i: i for i in range(2 * nw)},
        compiler_params=pltpu.CompilerParams(has_side_effects=DATAFLOW),
    )(*srcs, *lands, *handle["sems"], *afters)
    return list(outs[nw:])


def _chips_and_sibling():
    x, y, c = lax.axis_index("x"), lax.axis_index("y"), lax.axis_index("c")
    return [(1 - x, y), (x, 1 - y), (1 - x, 1 - y)], (x, y, 1 - c), c


def _slot(px, py, pc):
    return 4 * px + 2 * py + pc


def _two_level_start(shards, name):
    nw = len(shards)
    me = _my_index()
    lands = [lax.dynamic_update_slice_in_dim(lax.empty((N_DEV,) + s.shape, s.dtype), s[None], me, 0) for s in shards]

    def body(*refs):
        s_refs, l_refs = refs[:nw], refs[nw:2 * nw]
        send_sems, recv_sems, token = refs[2 * nw], refs[2 * nw + 1], refs[-1]
        chips, sibling, c = _chips_and_sibling()
        for w in range(nw):
            for k, to in enumerate([sibling] + [(*chip, c) for chip in chips]):
                pltpu.make_async_remote_copy(
                    src_ref=s_refs[w], dst_ref=l_refs[w].at[_my_index()],
                    send_sem=send_sems.at[4 * w + k], recv_sem=recv_sems.at[4 * w + k],
                    device_id=to, device_id_type=MESH).start()
        token[...] = jnp.zeros_like(token)

    sems = pltpu.SemaphoreType.DMA((4 * nw,))
    outs = pl.pallas_call(
        body, name=name,
        out_shape=(sems, sems, *[pltpu.HBM(a.shape, a.dtype) for a in (*shards, *lands)],
                   jax.ShapeDtypeStruct((SUBLANE, LANE), F32)),
        in_specs=[HBM_SPEC] * (2 * nw),
        out_specs=(SEM_SPEC, SEM_SPEC, *[HBM_SPEC] * (2 * nw), pl.BlockSpec(memory_space=pltpu.VMEM)),
        input_output_aliases={i: 2 + i for i in range(2 * nw)},
        compiler_params=pltpu.CompilerParams(has_side_effects=DATAFLOW),
    )(*[pltpu.with_memory_space_constraint(a, pltpu.HBM) for a in (*shards, *lands)])
    return dict(sems=outs[:2], srcs=outs[2:2 + nw], lands=outs[2 + nw:2 + 2 * nw], token=outs[-1])


def _two_level_pass(handle, afters, name):
    srcs, lands = handle["srcs"], handle["lands"]
    nw = len(srcs)

    def body(*refs):
        s_refs, l_refs = refs[:nw], refs[nw:2 * nw]
        send_a, recv_a = refs[2 * nw], refs[2 * nw + 1]
        send_b, recv_b = refs[2 * nw + 2 + len(afters)], refs[2 * nw + 3 + len(afters)]
        chips, sibling, c = _chips_and_sibling()
        for w in range(nw):
            for j, chip in enumerate(chips):
                landed = l_refs[w].at[_slot(*chip, c)]
                pltpu.make_async_remote_copy(
                    src_ref=s_refs[w], dst_ref=landed, send_sem=send_a.at[4 * w + 1 + j], recv_sem=recv_a.at[4 * w + 1 + j],
                    device_id=(*chip, c), device_id_type=MESH).wait_recv()
                pltpu.make_async_remote_copy(
                    src_ref=landed, dst_ref=landed, send_sem=send_b.at[3 * w + j], recv_sem=recv_b.at[3 * w + j],
                    device_id=sibling, device_id_type=MESH).start()

    sems = pltpu.SemaphoreType.DMA((3 * nw,))
    outs = pl.pallas_call(
        body, name=name,
        out_shape=(sems, sems, *[pltpu.HBM(a.shape, a.dtype) for a in (*srcs, *lands)]),
        in_specs=[HBM_SPEC] * (2 * nw) + [SEM_SPEC, SEM_SPEC] + [ANY_SPEC] * len(afters),
        out_specs=(SEM_SPEC, SEM_SPEC, *[HBM_SPEC] * (2 * nw)),
        input_output_aliases={i: 2 + i for i in range(2 * nw)},
        compiler_params=pltpu.CompilerParams(has_side_effects=DATAFLOW),
    )(*srcs, *lands, *handle["sems"], *afters)
    return dict(sems=handle["sems"], sems_pass=outs[:2], srcs=outs[2:2 + nw], lands=outs[2 + nw:2 + 2 * nw])


def _two_level_wait(handle, name):
    srcs, lands = handle["srcs"], handle["lands"]
    nw = len(srcs)

    def body(*refs):
        s_refs, l_refs = refs[:nw], refs[nw:2 * nw]
        send_a, recv_a, send_b, recv_b = refs[2 * nw:2 * nw + 4]
        chips, sibling, c = _chips_and_sibling()
        x, y = sibling[0], sibling[1]
        for w in range(nw):
            first = pltpu.make_async_remote_copy(
                src_ref=s_refs[w], dst_ref=l_refs[w].at[_slot(x, y, 1 - c)], send_sem=send_a.at[4 * w],
                recv_sem=recv_a.at[4 * w], device_id=sibling, device_id_type=MESH)
            first.wait_send()
            first.wait_recv()
            for j, chip in enumerate(chips):
                pltpu.make_async_remote_copy(
                    src_ref=s_refs[w], dst_ref=l_refs[w].at[_slot(*chip, c)], send_sem=send_a.at[4 * w + 1 + j],
                    recv_sem=recv_a.at[4 * w + 1 + j], device_id=(*chip, c), device_id_type=MESH).wait_send()
                passed = pltpu.make_async_remote_copy(
                    src_ref=l_refs[w].at[_slot(*chip, c)], dst_ref=l_refs[w].at[_slot(*chip, 1 - c)],
                    send_sem=send_b.at[3 * w + j], recv_sem=recv_b.at[3 * w + j], device_id=sibling, device_id_type=MESH)
                passed.wait_send()
                passed.wait_recv()

    outs = pl.pallas_call(
        body, name=name,
        out_shape=tuple(pltpu.HBM(a.shape, a.dtype) for a in (*srcs, *lands)),
        in_specs=[HBM_SPEC] * (2 * nw) + [SEM_SPEC] * 4,
        out_specs=tuple([HBM_SPEC] * (2 * nw)),
        input_output_aliases={i: i for i in range(2 * nw)},
        compiler_params=pltpu.CompilerParams(has_side_effects=DATAFLOW),
    )(*srcs, *lands, *handle["sems"], *handle["sems_pass"])
    return list(outs[nw:])


def _adamw(w, g, m, v):
    m = ADAM_B1 * m + (1.0 - ADAM_B1) * g
    v = ADAM_B2 * v + (1.0 - ADAM_B2) * (g * g)
    m_hat = m / (1.0 - ADAM_B1 ** ADAM_STEP)
    v_hat = v / (1.0 - ADAM_B2 ** ADAM_STEP)
    delta = -ADAM_LR * (m_hat / (jnp.sqrt(v_hat) + ADAM_EPS) + ADAM_WD * w)
    return delta, m, v


def _sum_adam(parts, w, m, v, name):
    _, r, c = parts.shape
    tr = _pick(r, 128, 16)

    def body(p_ref, w_ref, m_ref, v_ref, g_ref, d_ref, mo_ref, vo_ref):
        g = p_ref[0].astype(F32)
        for s in range(1, N_DEV):
            g = g + p_ref[s].astype(F32)
        g_ref[...] = g
        d_ref[...], mo_ref[...], vo_ref[...] = _adamw(w_ref[...], g, m_ref[...], v_ref[...])

    row = pl.BlockSpec((tr, c), lambda i: (i, 0))
    sh = jax.ShapeDtypeStruct((r, c), F32)
    return pl.pallas_call(
        body, name=name, grid=(r // tr,),
        in_specs=[pl.BlockSpec((N_DEV, tr, c), lambda i: (0, i, 0)), row, row, row],
        out_specs=[row, row, row, row], out_shape=[sh, sh, sh, sh],
        compiler_params=_params(("parallel",)),
    )(parts, w, m, v)


def _sum_slots(parts, name):
    _, r, c = parts.shape
    tr = _pick(r, 512, SUBLANE)

    def body(p_ref, o_ref):
        g = p_ref[0]
        for s in range(1, N_DEV):
            g = g + p_ref[s]
        o_ref[...] = g

    return pl.pallas_call(
        body, name=name, grid=(r // tr,),
        in_specs=[pl.BlockSpec((N_DEV, tr, c), lambda i: (0, i, 0))],
        out_specs=pl.BlockSpec((tr, c), lambda i: (i, 0)), out_shape=jax.ShapeDtypeStruct((r, c), F32),
        compiler_params=_params(("parallel",)),
    )(parts)


def _adam_rows(g, w, m, v, name):
    r, c = g.shape
    tr = _pick(r, 512, SUBLANE)

    def body(g_ref, w_ref, m_ref, v_ref, d_ref, mo_ref, vo_ref):
        d_ref[...], mo_ref[...], vo_ref[...] = _adamw(w_ref[...], g_ref[...], m_ref[...], v_ref[...])

    row = pl.BlockSpec((tr, c), lambda i: (i, 0))
    sh = jax.ShapeDtypeStruct((r, c), F32)
    return pl.pallas_call(body, name=name, grid=(r // tr,), in_specs=[row] * 4, out_specs=[row] * 3,
                          out_shape=[sh, sh, sh], compiler_params=_params(("parallel",)))(g, w, m, v)


def _pack(arrays):
    flat = jnp.concatenate([a.reshape(-1).astype(F32) for a in arrays])
    pad = (-flat.shape[0]) % (SUBLANE * LANE)
    return jnp.pad(flat, (0, pad)).reshape(-1, LANE)


def _unpack(packed, shapes):
    flat = packed.reshape(-1)
    out, off = [], 0
    for s in shapes:
        n = math.prod(s)
        out.append(flat[off:off + n].reshape(s))
        off += n
    return out


def _block_diag(t):
    eye = jnp.eye(S5_SUPER, dtype=bool)
    bd = jnp.where(eye[None, :, None, :, None], t[:, :, :, None, :], 0.0)
    return bd.reshape(S5_SUPER, S5_SUPER * t.shape[2], S5_SUPER * t.shape[3])


def _diag_blocks(dense, a, b):
    x = dense.reshape(S5_SUPER, S5_SUPER, a, S5_SUPER, b)
    return jnp.moveaxis(jnp.diagonal(x, axis1=1, axis2=3), -1, 1)


def _s5_layouts(b_re, b_im, c_re, c_im, d):
    g2 = (S5_GROUPS // S5_SUPER, S5_SUPER)
    bt = lambda b: _block_diag(b.reshape(*g2, S5_STATE, S5_GROUP).transpose(0, 1, 3, 2))
    ct = lambda c: _block_diag(c.reshape(*g2, S5_GROUP, S5_STATE).transpose(0, 1, 3, 2))
    bsg = jnp.concatenate([bt(b_re), bt(b_im)], axis=2).astype(BF16)
    ccat = jnp.concatenate([ct(c_re), -ct(c_im)], axis=1).astype(BF16)
    return bsg, ccat, d.reshape(1, S5_WIDTH)


def _s5_param_grads(gb, gc):
    n = S5_LANES
    gb_re = _diag_blocks(gb[:, :, 0:n], S5_GROUP, S5_STATE).transpose(0, 1, 3, 2).reshape(S5_GROUPS, S5_STATE, S5_GROUP)
    gb_im = _diag_blocks(gb[:, :, n:2 * n], S5_GROUP, S5_STATE).transpose(0, 1, 3, 2).reshape(S5_GROUPS, S5_STATE, S5_GROUP)
    gc_re = _diag_blocks(gc[:, 0:n, :], S5_STATE, S5_GROUP).transpose(0, 1, 3, 2).reshape(S5_GROUPS, S5_GROUP, S5_STATE)
    gc_im = -_diag_blocks(gc[:, n:2 * n, :], S5_STATE, S5_GROUP).transpose(0, 1, 3, 2).reshape(S5_GROUPS, S5_GROUP, S5_STATE)
    return gb_re, gb_im, gc_re, gc_im


def _local_step(x, target, weight, emit, small, after=None):
    sp = small
    a_re, a_im = sp["s5_a_re"], sp["s5_a_im"]
    ldt = sp["s5_log_dt"].reshape(S5_GROUPS, 1)

    h1 = _rms_fwd(x, sp["ln_mix_g"], "rms_mix", after=after)
    w_in = weight("w_in", h1)
    proj = _mm_nn(h1, w_in, "mm_in", after=weight("after_w_in", None))
    conv_w = weight("conv_w", None)
    disc = _s5_param_fwd(a_re, a_im, ldt)
    bsg, ccat, d_row = sp["s5_layouts"]
    abar_t, coef_t = _s5_to_tile(disc[0], disc[1]), _s5_to_tile(disc[2], disc[3])
    y, sb = _s5_fwd(proj, bsg, ccat, d_row, abar_t, coef_t)
    z16 = _gelu_fwd(y)
    w_glu = weight("s5_w_glu", z16)
    gl = _mm_nn(z16, w_glu, "mm_glu")
    z2 = _glu_fwd(y, gl, sp["s5_b_glu"])
    w_ps = weight("w_proj_s5", z2)
    ys = _mm_nn(z2, w_ps, "mm_proj_s5")
    o_raw, oh, s0s = _hgrn_fwd(proj, sp["hgrn_lb_logits"], sp["hgrn_norm_g"])
    w_ph = weight("w_proj_hgrn", oh)
    yh = _mm_nn(oh, w_ph, "mm_proj_hgrn")
    merged = _merge_fwd(proj, ys, yh)
    w_out = weight("w_out", merged)
    x1 = _mm_nn(merged, w_out, "mm_out", res=x)
    h2 = _rms_fwd(x1, sp["ln_ffn_g"], "rms_ffn")
    w_up = weight("w_up", h2)
    up = _mm_nn(h2, w_up, "mm_up")
    act = _ffn_act_fwd(up, conv_w, sp["conv_b"])
    w_down = weight("w_down", act)
    x2 = _mm_nn(act, w_down, "mm_down", res=x1)
    dx2, dx2_16, g_ln_final, loss = _loss_head(x2, sp["ln_final_g"], target)

    dact = _mm_nt(dx2_16, w_down, "mm_down_dx")
    tok = emit("w_down", _mm_tn(act, dx2_16, 1, "mm_down_dw"))
    dup_g, dup_v, dcw_g, dcw_v, dcb_g, dcb_v = _ffn_act_bwd(up, dact, conv_w, sp["conv_b"], after=tok)
    dup = jnp.concatenate([dup_g, dup_v], axis=1)
    g_conv_w = jnp.concatenate([dcw_g, dcw_v], axis=1)
    g_conv_b = jnp.concatenate([dcb_g, dcb_v], axis=1)
    dh2 = _mm_nt(dup, w_up, "mm_up_dx")
    tok = emit("w_up", _mm_tn(h2, dup, N_DEV, "mm_up_dw"))
    dx1, dx1_16, g_ln_ffn = _rms_bwd(x1, sp["ln_ffn_g"], dh2, dx2, "rms_ffn_bwd", True, after=tok)

    dmerged = _mm_nt(dx1_16, w_out, "mm_out_dx")
    tok = emit("w_out", _mm_tn(merged, dx1_16, 1, "mm_out_dw"))
    dys, dyh, dgs, dgh = _merge_bwd(proj, ys, yh, dmerged, after=tok)
    doh = _mm_nt(dyh, w_ph, "mm_proj_hgrn_dx")
    tok = emit("w_proj_hgrn", _mm_tn(oh, dyh, N_DEV, "mm_proj_hgrn_dw"))
    dz2 = _mm_nt(dys, w_ps, "mm_proj_s5_dx", after=tok)
    tok = emit("w_proj_s5", _mm_tn(z2, dys, N_DEV, "mm_proj_s5_dw"))
    dgl, dza, g_b_glu = _glu_bwd(y, gl, sp["s5_b_glu"], dz2, after=tok)
    dzb = _mm_nt(dgl, w_glu, "mm_glu_dx")
    tok = emit("s5_w_glu", _mm_tn(z16, dgl, 1, "mm_glu_dw"))
    dy = _gelu_bwd(y, dza, dzb, after=tok)
    du, gb, gc, gd, g_abar_t, g_coef_t = _s5_bwd(proj, dy, sb, bsg, ccat, d_row, abar_t, coef_t)
    g_a_re, g_a_im, g_ldt = _s5_param_bwd(a_re, a_im, ldt, [*_s5_from_tile(g_abar_t), *_s5_from_tile(g_coef_t)])
    g_b_re, g_b_im, g_c_re, g_c_im = _s5_param_grads(gb, gc)
    dq, dz, dv, dg, g_norm, dlb = _hgrn_bwd(proj, o_raw, s0s, doh, sp["hgrn_lb_logits"], sp["hgrn_norm_g"])
    g_logits = _lb_bwd(sp["hgrn_lb_logits"], dlb)

    small_g = dict(s5_a_re=g_a_re, s5_a_im=g_a_im, s5_log_dt=g_ldt.reshape(1, S5_GROUPS),
                   s5_b_re=g_b_re, s5_b_im=g_b_im, s5_c_re=g_c_re, s5_c_im=g_c_im,
                   s5_d=gd.reshape(S5_GROUPS, S5_GROUP), s5_b_glu=g_b_glu, hgrn_lb_logits=g_logits,
                   hgrn_norm_g=g_norm, ln_ffn_g=g_ln_ffn, conv_w=g_conv_w, conv_b=g_conv_b, ln_final_g=g_ln_final,
                   loss=loss[0, 0:1])
    tok_small = emit("small", small_g)

    dproj = jnp.concatenate([du, dq, dz, dv, dg, dgs, dgh], axis=1)
    tok = emit("w_in", _mm_tn(h1, dproj, N_DEV, "mm_in_dw", after=tok_small))
    dh1 = _mm_nt(dproj, w_in, "mm_in_dx")
    grad_x, g_ln_mix = _rms_bwd(x, sp["ln_mix_g"], dh1, dx1, "rms_mix_bwd", False, after=tok)
    return grad_x, g_ln_mix


BIG = ("w_in", "s5_w_glu", "w_proj_s5", "w_proj_hgrn", "w_out", "w_up", "w_down")
COL_SHARDED = ("w_in", "w_proj_s5", "w_proj_hgrn", "w_up")
SMALL = ("ln_mix_g", "s5_a_re", "s5_a_im", "s5_log_dt", "s5_b_re", "s5_b_im", "s5_c_re", "s5_c_im", "s5_d",
         "s5_b_glu", "hgrn_lb_logits", "hgrn_norm_g", "ln_ffn_g", "conv_b", "ln_final_g")
WEIGHTS = ("ln_mix_g", "w_in", "s5_a_re", "s5_a_im", "s5_log_dt", "s5_b_re", "s5_b_im", "s5_c_re", "s5_c_im", "s5_d",
           "s5_w_glu", "s5_b_glu", "w_proj_s5", "hgrn_lb_logits", "hgrn_norm_g", "w_proj_hgrn", "w_out", "ln_ffn_g",
           "w_up", "conv_w", "conv_b", "w_down", "ln_final_g")


def kernel(x, ln_mix_g, w_in, s5_a_re, s5_a_im, s5_log_dt, s5_b_re, s5_b_im, s5_c_re, s5_c_im, s5_d, s5_w_glu, s5_b_glu, w_proj_s5, hgrn_lb_logits, hgrn_norm_g, w_proj_hgrn, w_out, ln_ffn_g, w_up, conv_w, conv_b, w_down, ln_final_g, loss_target, m_ln_mix_g, m_w_in, m_s5_a_re, m_s5_a_im, m_s5_log_dt, m_s5_b_re, m_s5_b_im, m_s5_c_re, m_s5_c_im, m_s5_d, m_s5_w_glu, m_s5_b_glu, m_w_proj_s5, m_hgrn_lb_logits, m_hgrn_norm_g, m_w_proj_hgrn, m_w_out, m_ln_ffn_g, m_w_up, m_conv_w, m_conv_b, m_w_down, m_ln_final_g, v_ln_mix_g, v_w_in, v_s5_a_re, v_s5_a_im, v_s5_log_dt, v_s5_b_re, v_s5_b_im, v_s5_c_re, v_s5_c_im, v_s5_d, v_s5_w_glu, v_s5_b_glu, v_w_proj_s5, v_hgrn_lb_logits, v_hgrn_norm_g, v_w_proj_hgrn, v_w_out, v_ln_ffn_g, v_w_up, v_conv_w, v_conv_b, v_w_down, v_ln_final_g):
    given = dict(locals())
    w = {n: given[n] for n in WEIGHTS}
    mom = {n: given["m_" + n] for n in WEIGHTS}
    var = {n: given["v_" + n] for n in WEIGHTS}

    shard16 = {n: w[n][0].astype(BF16) for n in BIG}
    first = _two_level_start([shard16["w_in"], conv_w[0]], "gather_first_start")
    zero = first["token"][0, 0]
    packed_small = SMALL[1:]
    pw, pm, pv = (_pack([d[n] for n in packed_small]) + zero for d in (w, mom, var))
    layouts = _s5_layouts(s5_b_re[0] + zero, s5_b_im[0], s5_c_re[0] + zero, s5_c_im[0], s5_d[0])
    gather_groups = (("s5_w_glu", "w_proj_s5", "w_proj_hgrn", "w_out"), ("w_up",), ("w_down",))
    pending, ready = {}, {}

    def weight(name, after):
        if "w_in" not in ready:
            passed = _two_level_pass(first, [after, pw, pm, pv, layouts[0], layouts[1]], "gather_first_pass")
            ready["w_in"], conv_w_all = _two_level_wait(passed, "gather_first_wait")
            ready["conv_w"] = conv_w_all.transpose(1, 0, 2).reshape(3, 2 * D_FF)
            token = ready["w_in"]
            for i, group in enumerate(gather_groups):
                handle = _exchange_start([shard16[n] for n in group], False, token, f"gather_start_{i}")
                token = handle["token"]
                for n in group:
                    pending[n] = (group, handle, f"gather_wait_{i}")
            ready["after_w_in"] = token
        if name not in ready:
            group, handle, wait_name = pending[name]
            for n, g in zip(group, _exchange_wait(handle, [after], wait_name)):
                ready[n] = g
        g = ready[name]
        return g if name not in BIG or name in COL_SHARDED else g.reshape(1, N_DEV * g.shape[1], g.shape[2])

    scatter_groups = (("w_down",), ("w_up",), ("w_out", "w_proj_hgrn", "w_proj_s5", "s5_w_glu"), ("w_in",))
    emitted, scatters = {}, []
    packed_names = SMALL[1:] + ("conv_w", "loss")

    def emit(name, grad):
        if name == "small":
            emitted[name] = ([grad[n].shape for n in packed_names],
                             _exchange_start([_pack([grad[n] for n in packed_names])], False, None, "small_start"))
            return emitted[name][1]["token"]
        emitted[name] = grad if name in COL_SHARDED else grad.reshape(N_DEV, -1, grad.shape[2])
        group = scatter_groups[len(scatters)]
        if not all(n in emitted for n in group):
            return None
        handle = _exchange_start([emitted[n] for n in group], True, None, f"scatter_start_{len(scatters)}")
        scatters.append((group, handle))
        return handle["token"]

    small = dict(ln_mix_g=ln_mix_g, s5_a_re=s5_a_re[0], s5_a_im=s5_a_im[0], s5_log_dt=s5_log_dt, s5_layouts=layouts,
                 s5_b_glu=s5_b_glu, hgrn_lb_logits=hgrn_lb_logits, hgrn_norm_g=hgrn_norm_g, ln_ffn_g=ln_ffn_g,
                 conv_b=conv_b, ln_final_g=ln_final_g.reshape(1, D_MODEL))
    grad_x, g_ln_mix = _local_step(x[0], loss_target[0], weight, emit, small, after=first["token"])

    shapes, handle = emitted["small"]
    total = _sum_slots(_exchange_wait(handle, [grad_x], "small_wait")[0], "sum_small")
    summed = dict(zip(packed_names, _unpack(total, shapes)))
    mix_all = _all_gather([g_ln_mix.reshape(-1, LANE)], "gather_ln_mix")[0]
    summed["ln_mix_g"] = _sum_slots(mix_all, "sum_ln_mix").reshape(1, D_MODEL)

    grads, delta, new_m, new_v = {}, {}, {}, {}
    afters = [grad_x, total]
    for i, (group, handle) in enumerate(scatters):
        for n, r in zip(group, _exchange_wait(handle, afters, f"scatter_wait_{i}")):
            g, d, m2, v2 = _sum_adam(r, w[n][0], mom[n][0], var[n][0], "adam_" + n)
            grads[n], delta[n], new_m[n], new_v[n] = g[None], d[None], m2[None], v2[None]
        if i == len(scatters) - 2:
            afters = [delta[n] for g2, _ in scatters[:-1] for n in g2]

    d_s, m_s, v_s = _adam_rows(_pack([summed[n] for n in packed_small]), pw, pm, pv, "adam_small")
    wshapes = [w[n].shape for n in packed_small]
    for n, d, m2, v2 in zip(packed_small, _unpack(d_s, wshapes), _unpack(m_s, wshapes), _unpack(v_s, wshapes)):
        grads[n], delta[n], new_m[n], new_v[n] = summed[n].reshape(w[n].shape), d, m2, v2
    grads["ln_mix_g"] = summed["ln_mix_g"]
    delta["ln_mix_g"], new_m["ln_mix_g"], new_v["ln_mix_g"] = _adam_rows(summed["ln_mix_g"], ln_mix_g, m_ln_mix_g,
                                                                         v_ln_mix_g, "adam_ln_mix")
    me = 4 * lax.axis_index("x") + 2 * lax.axis_index("y") + lax.axis_index("c")
    ncol = conv_w.shape[2]
    g_cw = lax.dynamic_slice_in_dim(summed["conv_w"], me * ncol, ncol, axis=1)
    d_cw, m_cw, v_cw = _adam_rows(g_cw, conv_w[0], m_conv_w[0], v_conv_w[0], "adam_conv_w")
    grads["conv_w"], delta["conv_w"], new_m["conv_w"], new_v["conv_w"] = g_cw[None], d_cw[None], m_cw[None], v_cw[None]

    return (summed["loss"].reshape(()), grad_x[None], *[grads[n] for n in WEIGHTS], *[delta[n] for n in WEIGHTS],
            *[new_m[n] for n in WEIGHTS], *[new_v[n] for n in WEIGHTS])
```

```python
import math

import jax
import jax.numpy as jnp
from jax import lax
from jax.experimental import pallas as pl
from jax.experimental.pallas import tpu as pltpu

F32 = jnp.float32
BF16 = jnp.bfloat16

N_DEV = 8
D_MODEL = 2048
S5_WIDTH = 1024
S5_GROUP = 16
S5_GROUPS = 64
S5_STATE = 64
S5_MAX_RE = -1e-4
S5_SUPER = 8
S5_LANES = S5_SUPER * S5_STATE
HGRN_WIDTH = 1024
HGRN_HEADS = 8
HGRN_DH = 128
HGRN_CHUNK = 64
HGRN_SUBS = 2
D_FF = 5632
RMS_EPS = 1e-6
ADAM_LR = 0.001
ADAM_B1 = 0.9
ADAM_B2 = 0.999
ADAM_EPS = 1e-08
ADAM_WD = 0.01
ADAM_STEP = 10

LANE = 128
SUBLANE = 8
VMEM_LIMIT = 48 * 1024 * 1024
MESH = pl.DeviceIdType.MESH
GELU_C = math.sqrt(2.0 / math.pi)
GELU_A = 0.044715


def _params(sem=None):
    return pltpu.CompilerParams(dimension_semantics=sem, vmem_limit_bytes=VMEM_LIMIT)


def _pick(n, cap, unit=LANE):
    best = None
    for t in range(unit, min(n, cap) + 1, unit):
        if n % t == 0:
            best = t
    return best if best is not None else n


def _ordered(body, in_specs, args, after):
    if after is None:
        return body, list(in_specs), list(args)
    n_in = len(args)

    def ordered_body(*refs):
        return body(*refs[:n_in], *refs[n_in + 1:])

    return ordered_body, [*in_specs, pl.BlockSpec(memory_space=pl.ANY)], [*args, after]


def _sigmoid(x):
    return 0.5 * jnp.tanh(0.5 * x) + 0.5


def _silu_and_grad(x):
    s = _sigmoid(x)
    return x * s, s * (1.0 + x * (1.0 - s))


def _gelu_and_grad(y):
    inner = GELU_C * (y + GELU_A * y * y * y)
    th = jnp.tanh(inner)
    val = 0.5 * y * (1.0 + th)
    grad = 0.5 * (1.0 + th) + 0.5 * y * (1.0 - th * th) * GELU_C * (1.0 + 3.0 * GELU_A * y * y)
    return val, grad


def _dot(a, b):
    return jnp.dot(a, b, preferred_element_type=F32)


def _dot_nt(a, b):
    return lax.dot_general(a, b, (((1,), (1,)), ((), ())), preferred_element_type=F32)


def _dot_tn(a, b):
    return lax.dot_general(a, b, (((0,), (0,)), ((), ())), preferred_element_type=F32)


def _blocks_per_step(nb, ns, tn, cap=2048):
    if tn != ns:
        return 1
    best = 1
    for b in range(1, nb + 1):
        if nb % b == 0 and b * ns <= cap:
            best = b
    return best


def _mm_nn(a, w, name, res=None, out_dtype=F32, after=None):
    m, kdim = a.shape
    nb, _, ns = w.shape
    tm, tk, tn = _pick(m, 512), _pick(kdim, 2048), _pick(ns, 1536)
    npb, nk = ns // tn, kdim // tk
    bps = _blocks_per_step(nb, ns, tn)
    assert bps == 1 or nk == 1

    def body(*refs):
        a_ref, w_ref = refs[0], refs[1]
        r_ref = refs[2] if res is not None else None
        o_ref = refs[3] if res is not None else refs[2]

        def finish(r, cols):
            if res is not None:
                r = r + r_ref[:, cols]
            o_ref[:, cols] = r.astype(out_dtype)

        if nk == 1:
            for b in range(bps):
                finish(_dot(a_ref[...], w_ref[b]), slice(b * tn, (b + 1) * tn))
            return
        acc = refs[-1]
        k = pl.program_id(2)

        @pl.when(k == 0)
        def _():
            acc[...] = jnp.zeros_like(acc)

        acc[...] += _dot(a_ref[...], w_ref[0])

        @pl.when(k == nk - 1)
        def _():
            finish(acc[...], slice(0, tn))

    in_specs = [pl.BlockSpec((tm, tk), lambda j, i, k: (i, k)),
                pl.BlockSpec((bps, tk, tn), lambda j, i, k: (j // npb, k, j % npb))]
    args = [a, w]
    if res is not None:
        in_specs.append(pl.BlockSpec((tm, bps * tn), lambda j, i, k: (i, j)))
        args.append(res)
    body, in_specs, args = _ordered(body, in_specs, args, after)
    return pl.pallas_call(
        body, name=name, grid=(nb * npb // bps, m // tm, nk),
        in_specs=in_specs, out_specs=pl.BlockSpec((tm, bps * tn), lambda j, i, k: (i, j)),
        out_shape=jax.ShapeDtypeStruct((m, nb * ns), out_dtype),
        scratch_shapes=[pltpu.VMEM((tm, tn), F32)] if nk > 1 else [],
        compiler_params=_params(("parallel", "parallel", "arbitrary")),
    )(*args)


def _mm_nt(a, w, name, out_dtype=F32, after=None):
    m, _ = a.shape
    nb, kdim, ns = w.shape
    tm, tko, tn = _pick(m, 1024), _pick(kdim, 1024), _pick(ns, 2048)
    npb = ns // tn
    bps = _blocks_per_step(nb, ns, tn)
    nred = nb * npb // bps

    def body(a_ref, w_ref, o_ref, *scratch):
        total = _dot_nt(a_ref[:, 0:tn], w_ref[0])
        for b in range(1, bps):
            total = total + _dot_nt(a_ref[:, b * tn:(b + 1) * tn], w_ref[b])
        if nred == 1:
            o_ref[...] = total.astype(out_dtype)
            return
        acc = scratch[0]
        n = pl.program_id(2)

        @pl.when(n == 0)
        def _():
            acc[...] = jnp.zeros_like(acc)

        acc[...] += total

        @pl.when(n == nred - 1)
        def _():
            o_ref[...] = acc[...].astype(out_dtype)

    in_specs = [pl.BlockSpec((tm, bps * tn), lambda i, j, n: (i, n)),
                pl.BlockSpec((bps, tko, tn), lambda i, j, n: (n // npb, j, n % npb))]
    body, in_specs, args = _ordered(body, in_specs, [a, w], after)
    return pl.pallas_call(
        body, name=name, grid=(m // tm, kdim // tko, nred),
        in_specs=in_specs,
        out_specs=pl.BlockSpec((tm, tko), lambda i, j, n: (i, j)),
        out_shape=jax.ShapeDtypeStruct((m, kdim), out_dtype),
        scratch_shapes=[pltpu.VMEM((tm, tko), F32)] if nred > 1 else [],
        compiler_params=_params(("parallel", "parallel", "arbitrary")),
    )(*args)


def _mm_tn(a, d, nb, name, out_dtype=BF16, after=None):
    m, kdim = a.shape
    ns = d.shape[1] // nb
    tm, tko, tn = _pick(m, 4096), _pick(kdim, 512), _pick(ns, 1536)
    npb, nm = ns // tn, m // tm

    def body(a_ref, d_ref, o_ref, *scratch):
        if nm == 1:
            o_ref[...] = _dot_tn(a_ref[...], d_ref[...]).astype(out_dtype)
            return
        acc = scratch[0]
        r = pl.program_id(2)

        @pl.when(r == 0)
        def _():
            acc[...] = jnp.zeros_like(acc)

        acc[...] += _dot_tn(a_ref[...], d_ref[...])

        @pl.when(r == nm - 1)
        def _():
            o_ref[...] = acc[...].astype(out_dtype)

    in_specs = [pl.BlockSpec((tm, tko), lambda j, i, r: (r, i)), pl.BlockSpec((tm, tn), lambda j, i, r: (r, j))]
    body, in_specs, args = _ordered(body, in_specs, [a, d], after)
    return pl.pallas_call(
        body, name=name, grid=(nb * npb, kdim // tko, nm),
        in_specs=in_specs,
        out_specs=pl.BlockSpec((None, tko, tn), lambda j, i, r: (j // npb, i, j % npb)),
        out_shape=jax.ShapeDtypeStruct((nb, kdim, ns), out_dtype),
        scratch_shapes=[pltpu.VMEM((tko, tn), F32)] if nm > 1 else [],
        compiler_params=_params(("parallel", "parallel", "arbitrary")),
    )(*args)


def _rms_fwd(x, g, name, after=None):
    t, d = x.shape
    tr = _pick(t, 256, SUBLANE)

    def body(x_ref, g_ref, h_ref):
        xv = x_ref[...]
        r = lax.rsqrt(jnp.mean(xv * xv, axis=-1, keepdims=True) + RMS_EPS)
        h_ref[...] = (xv * r * g_ref[...]).astype(BF16)

    in_specs = [pl.BlockSpec((tr, d), lambda i: (i, 0)), pl.BlockSpec((1, d), lambda i: (0, 0))]
    body, in_specs, args = _ordered(body, in_specs, [x, g], after)
    return pl.pallas_call(
        body, name=name, grid=(t // tr,),
        in_specs=in_specs,
        out_specs=pl.BlockSpec((tr, d), lambda i: (i, 0)),
        out_shape=jax.ShapeDtypeStruct((t, d), BF16),
        compiler_params=_params(("parallel",)),
    )(*args)


def _rms_bwd(x, g, dh, add, name, want_bf16, after=None):
    t, d = x.shape
    tr = _pick(t, 256, SUBLANE)

    def body(x_ref, g_ref, dh_ref, add_ref, *outs):
        if want_bf16:
            dx_ref, dxb_ref, dg_ref = outs
        else:
            dx_ref, dg_ref = outs
        i = pl.program_id(0)

        @pl.when(i == 0)
        def _():
            dg_ref[...] = jnp.zeros_like(dg_ref)

        xv, dhv = x_ref[...], dh_ref[...]
        r = lax.rsqrt(jnp.mean(xv * xv, axis=-1, keepdims=True) + RMS_EPS)
        xh = xv * r
        dg_ref[...] += jnp.sum(dhv * xh, axis=0, keepdims=True)
        dxh = dhv * g_ref[...]
        dx = add_ref[...] + r * (dxh - xh * jnp.mean(dxh * xh, axis=-1, keepdims=True))
        dx_ref[...] = dx
        if want_bf16:
            dxb_ref[...] = dx.astype(BF16)

    row = pl.BlockSpec((tr, d), lambda i: (i, 0))
    vec = pl.BlockSpec((1, d), lambda i: (0, 0))
    out_specs = [row] + ([row] if want_bf16 else []) + [vec]
    out_shape = ([jax.ShapeDtypeStruct((t, d), F32)] + ([jax.ShapeDtypeStruct((t, d), BF16)] if want_bf16 else [])
                 + [jax.ShapeDtypeStruct((1, d), F32)])
    body, in_specs, args = _ordered(body, [row, vec, row, row], [x, g, dh, add], after)
    return pl.pallas_call(
        body, name=name, grid=(t // tr,),
        in_specs=in_specs, out_specs=out_specs, out_shape=out_shape,
        compiler_params=_params(("arbitrary",)),
    )(*args)


def _loss_head(x2, g, target, name="loss_head"):
    t, d = x2.shape
    tr = _pick(t, 256, SUBLANE)

    def body(x_ref, g_ref, t_ref, dx_ref, dxb_ref, dg_ref, loss_ref):
        i = pl.program_id(0)

        @pl.when(i == 0)
        def _():
            dg_ref[...] = jnp.zeros_like(dg_ref)
            loss_ref[...] = jnp.zeros_like(loss_ref)

        xv = x_ref[...]
        gv = g_ref[...]
        r = lax.rsqrt(jnp.mean(xv * xv, axis=-1, keepdims=True) + RMS_EPS)
        xh = xv * r
        err = xh * gv - t_ref[...]
        part = 0.5 * jnp.sum(jnp.mean(err * err, axis=-1, keepdims=True), axis=0, keepdims=True)
        loss_ref[...] += jnp.broadcast_to(part, loss_ref.shape)
        dy = err * (1.0 / d)
        dg_ref[...] += jnp.sum(dy * xh, axis=0, keepdims=True)
        dxh = dy * gv
        dx = r * (dxh - xh * jnp.mean(dxh * xh, axis=-1, keepdims=True))
        dx_ref[...] = dx
        dxb_ref[...] = dx.astype(BF16)

    row = pl.BlockSpec((tr, d), lambda i: (i, 0))
    vec = pl.BlockSpec((1, d), lambda i: (0, 0))
    return pl.pallas_call(
        body, name=name, grid=(t // tr,),
        in_specs=[row, vec, row],
        out_specs=[row, row, vec, pl.BlockSpec((1, LANE), lambda i: (0, 0))],
        out_shape=[jax.ShapeDtypeStruct((t, d), F32), jax.ShapeDtypeStruct((t, d), BF16),
                   jax.ShapeDtypeStruct((1, d), F32), jax.ShapeDtypeStruct((1, LANE), F32)],
        compiler_params=_params(("arbitrary",)),
    )(x2, g, target)


def _s5_discretize(a_re, a_im, ldt):
    lam_re = jnp.minimum(a_re, S5_MAX_RE)
    lam_im = a_im
    dt = jnp.exp(ldt)
    mag = jnp.exp(lam_re * dt)
    abar_re = mag * jnp.cos(lam_im * dt)
    abar_im = mag * jnp.sin(lam_im * dt)
    den = lam_re * lam_re + lam_im * lam_im
    nr = abar_re - 1.0
    ni = abar_im
    coef_re = (nr * lam_re + ni * lam_im) / den
    coef_im = (ni * lam_re - nr * lam_im) / den
    return abar_re, abar_im, coef_re, coef_im


def _s5_param_fwd(a_re, a_im, ldt):
    def body(ar_ref, ai_ref, l_ref, o0, o1, o2, o3):
        outs = _s5_discretize(ar_ref[...], ai_ref[...], l_ref[...])
        for o, v in zip((o0, o1, o2, o3), outs):
            o[...] = v

    sh = jax.ShapeDtypeStruct(a_re.shape, F32)
    return pl.pallas_call(body, name="s5_param_fwd", out_shape=[sh, sh, sh, sh], compiler_params=_params())(a_re, a_im, ldt)


def _s5_param_bwd(a_re, a_im, ldt, cts):
    def body(ar_ref, ai_ref, l_ref, c0, c1, c2, c3, g0, g1, g2):
        _, vjp = jax.vjp(_s5_discretize, ar_ref[...], ai_ref[...], l_ref[...])
        ga, gb, gl = vjp((c0[...], c1[...], c2[...], c3[...]))
        g0[...] = ga
        g1[...] = gb
        g2[...] = gl

    sh = jax.ShapeDtypeStruct(a_re.shape, F32)
    return pl.pallas_call(body, name="s5_param_bwd", out_shape=[sh, sh, jax.ShapeDtypeStruct(ldt.shape, F32)],
                          compiler_params=_params())(a_re, a_im, ldt, *cts)


def _cmul(ar, ai, br, bi):
    return ar * br - ai * bi, ar * bi + ai * br


S5_TC = 128
S5_TILE = S5_SUPER * SUBLANE
S5_HALF = S5_TILE // 2


def _s5_to_tile(re, im):
    f = lambda a: a.reshape(S5_SUPER, S5_LANES // LANE, LANE).transpose(1, 0, 2).reshape(S5_HALF, LANE)
    return jnp.concatenate([f(re), f(im)], axis=0)


def _s5_from_tile(tile):
    f = lambda a: a.reshape(S5_LANES // LANE, S5_SUPER, LANE).transpose(1, 0, 2).reshape(S5_GROUPS, S5_STATE)
    return f(tile[0:S5_HALF]), f(tile[S5_HALF:])


RE = slice(0, S5_HALF)
IM = slice(S5_HALF, S5_TILE)


def _s5_scatter_rows(buf, rows, first_tile=0):
    tc = rows[0].shape[0]
    for j in range(SUBLANE):
        stacked = jnp.stack([r[:, j * LANE:(j + 1) * LANE] for r in rows], axis=0)
        buf[first_tile:first_tile + tc, j * SUBLANE:(j + 1) * SUBLANE, :] = jnp.swapaxes(stacked, 0, 1)


def _s5_gather_rows(buf, tc, first_tile=0):
    per_j = [jnp.swapaxes(buf[first_tile:first_tile + tc, j * SUBLANE:(j + 1) * SUBLANE, :], 0, 1)
             for j in range(SUBLANE)]
    return [jnp.concatenate([per_j[j][k] for j in range(SUBLANE)], axis=1) for k in range(S5_SUPER)]


def _s5_fwd(proj, bsg, ccat, d_row, abar_t, coef_t):
    t = proj.shape[0]
    tc = min(t, S5_TC)
    n_chunks = t // tc

    def body(u_ref, b_ref, c_ref, d_ref, a_ref, cf_ref, y_ref, sb_ref, x, car):
        @pl.when(pl.program_id(0) == 0)
        def _():
            car[...] = jnp.zeros_like(car)

        sb_ref[...] = car[...]
        u = u_ref[...]
        _s5_scatter_rows(x, [_dot(u[:, k * LANE:(k + 1) * LANE].astype(BF16), b_ref[k]) for k in range(S5_SUPER)])
        ar, ai = a_ref[RE, :], a_ref[IM, :]
        cr, ci = cf_ref[RE, :], cf_ref[IM, :]

        def step(i, carry):
            sr, si = carry
            xr, xi = _cmul(cr, ci, x[i, RE, :], x[i, IM, :])
            sr, si = ar * sr - ai * si + xr, ar * si + ai * sr + xi
            x[i, RE, :] = sr
            x[i, IM, :] = si
            return sr, si

        sr, si = lax.fori_loop(0, tc, step, (car[RE, :], car[IM, :]), unroll=4)
        car[RE, :] = sr
        car[IM, :] = si
        for k, s_k in enumerate(_s5_gather_rows(x, tc)):
            cols = slice(k * LANE, (k + 1) * LANE)
            y_ref[:, cols] = _dot(s_k.astype(BF16), c_ref[k]) + d_ref[:, cols] * u[:, cols]

    full = lambda shape: pl.BlockSpec(shape, lambda c: (0,) * len(shape))
    return pl.pallas_call(
        body, name="s5_fwd", grid=(n_chunks,),
        in_specs=[pl.BlockSpec((tc, S5_WIDTH), lambda c: (c, 0)), full(bsg.shape), full(ccat.shape), full(d_row.shape),
                  full(abar_t.shape), full(coef_t.shape)],
        out_specs=[pl.BlockSpec((tc, S5_WIDTH), lambda c: (c, 0)), pl.BlockSpec((None, S5_TILE, LANE), lambda c: (c, 0, 0))],
        out_shape=[jax.ShapeDtypeStruct((t, S5_WIDTH), F32), jax.ShapeDtypeStruct((n_chunks, S5_TILE, LANE), F32)],
        scratch_shapes=[pltpu.VMEM((tc, S5_TILE, LANE), F32), pltpu.VMEM((S5_TILE, LANE), F32)],
        compiler_params=_params(("arbitrary",)),
    )(proj, bsg, ccat, d_row, abar_t, coef_t)


def _s5_bwd(proj, dy, sb, bsg, ccat, d_row, abar_t, coef_t):
    t = proj.shape[0]
    tc = min(t, S5_TC)
    n_chunks = t // tc
    last = n_chunks - 1

    def body(u_ref, dy_ref, sb_ref, b_ref, c_ref, d_ref, a_ref, cf_ref,
             du_ref, gb_ref, gc_ref, gd_ref, ga_ref, gcf_ref, xb, xs, xg, gcar, acc):
        @pl.when(pl.program_id(0) == 0)
        def _():
            gcar[...] = jnp.zeros_like(gcar)
            acc[...] = jnp.zeros_like(acc)
            gb_ref[...] = jnp.zeros_like(gb_ref)
            gc_ref[...] = jnp.zeros_like(gc_ref)
            gd_ref[...] = jnp.zeros_like(gd_ref)

        u = u_ref[...]
        dyv = dy_ref[...]
        u16, dy16 = u.astype(BF16), dyv.astype(BF16)
        subs = [slice(k * LANE, (k + 1) * LANE) for k in range(S5_SUPER)]
        _s5_scatter_rows(xb, [_dot(u16[:, c], b_ref[k]) for k, c in enumerate(subs)])
        _s5_scatter_rows(xg, [_dot_nt(dy16[:, c], c_ref[k]) for k, c in enumerate(subs)])
        ar, ai = a_ref[RE, :], a_ref[IM, :]
        cr, ci = cf_ref[RE, :], cf_ref[IM, :]

        xs[0] = sb_ref[...]

        def fstep(i, carry):
            sr, si = carry
            xr, xi = _cmul(cr, ci, xb[i, RE, :], xb[i, IM, :])
            sr, si = ar * sr - ai * si + xr, ar * si + ai * sr + xi
            xs[i + 1, RE, :] = sr
            xs[i + 1, IM, :] = si
            return sr, si

        lax.fori_loop(0, tc, fstep, (sb_ref[RE, :], sb_ref[IM, :]), unroll=4)

        def rstep(n, carry):
            gr, gi, a0, a1, a2, a3 = carry
            i = tc - 1 - n
            xr = xg[i, RE, :] + ar * gr + ai * gi
            xi = xg[i, IM, :] + ar * gi - ai * gr
            pr, pi = xs[i, RE, :], xs[i, IM, :]
            br, bi = xb[i, RE, :], xb[i, IM, :]
            a0 = a0 + pr * xr + pi * xi
            a1 = a1 + pr * xi - pi * xr
            a2 = a2 + br * xr + bi * xi
            a3 = a3 + br * xi - bi * xr
            xg[i, RE, :] = cr * xr + ci * xi
            xg[i, IM, :] = cr * xi - ci * xr
            return xr, xi, a0, a1, a2, a3

        init = (gcar[RE, :], gcar[IM, :], acc[0], acc[1], acc[2], acc[3])
        gr, gi, a0, a1, a2, a3 = lax.fori_loop(0, tc, rstep, init, unroll=2)
        gcar[RE, :] = gr
        gcar[IM, :] = gi
        for idx, a in enumerate((a0, a1, a2, a3)):
            acc[idx] = a
        ga_ref[RE, :] = a0
        ga_ref[IM, :] = a1
        gcf_ref[RE, :] = a2
        gcf_ref[IM, :] = a3

        g_rows = _s5_gather_rows(xg, tc)
        s_rows = _s5_gather_rows(xs, tc, first_tile=1)
        for k in range(S5_SUPER):
            cols = subs[k]
            g16 = g_rows[k].astype(BF16)
            s16 = s_rows[k].astype(BF16)
            gb_ref[k] += _dot_tn(u16[:, cols], g16)
            gc_ref[k] += _dot_tn(s16, dy16[:, cols])
            du_ref[:, cols] = (_dot_nt(g16, b_ref[k]) + d_ref[:, cols] * dyv[:, cols]).astype(BF16)
        gd_ref[...] += jnp.sum(dyv * u, axis=0, keepdims=True)

    full = lambda shape: pl.BlockSpec(shape, lambda c: (0,) * len(shape))
    rows = pl.BlockSpec((tc, S5_WIDTH), lambda c: (last - c, 0))
    tile = (S5_TILE, LANE)
    return pl.pallas_call(
        body, name="s5_bwd", grid=(n_chunks,),
        in_specs=[rows, rows, pl.BlockSpec((None, S5_TILE, LANE), lambda c: (last - c, 0, 0)),
                  full(bsg.shape), full(ccat.shape), full(d_row.shape), full(abar_t.shape), full(coef_t.shape)],
        out_specs=[rows, full(bsg.shape), full(ccat.shape), full(d_row.shape), full(tile), full(tile)],
        out_shape=[jax.ShapeDtypeStruct((t, S5_WIDTH), BF16), jax.ShapeDtypeStruct(bsg.shape, F32),
                   jax.ShapeDtypeStruct(ccat.shape, F32), jax.ShapeDtypeStruct(d_row.shape, F32),
                   jax.ShapeDtypeStruct(tile, F32), jax.ShapeDtypeStruct(tile, F32)],
        scratch_shapes=[pltpu.VMEM((tc, S5_TILE, LANE), F32), pltpu.VMEM((tc + 1, S5_TILE, LANE), F32),
                        pltpu.VMEM((tc, S5_TILE, LANE), F32), pltpu.VMEM(tile, F32),
                        pltpu.VMEM((4, S5_HALF, LANE), F32)],
        compiler_params=_params(("arbitrary",)),
    )(proj, dy, sb, bsg, ccat, d_row, abar_t, coef_t)


def _gelu_fwd(y, name="s5_gelu"):
    t, w = y.shape
    tr = _pick(t, 512, SUBLANE)

    def body(y_ref, z_ref):
        z_ref[...] = _gelu_and_grad(y_ref[...])[0].astype(BF16)

    row = pl.BlockSpec((tr, w), lambda i: (i, 0))
    return pl.pallas_call(body, name=name, grid=(t // tr,), in_specs=[row], out_specs=row,
                          out_shape=jax.ShapeDtypeStruct((t, w), BF16), compiler_params=_params(("parallel",)))(y)


def _glu_fwd(y, gl, b, name="s5_glu"):
    t, w = y.shape
    tr = _pick(t, 512, SUBLANE)

    def body(y_ref, gl_ref, b_ref, z2_ref):
        z = _gelu_and_grad(y_ref[...])[0]
        z2_ref[...] = (z * _sigmoid(gl_ref[...] + b_ref[...])).astype(BF16)

    row = pl.BlockSpec((tr, w), lambda i: (i, 0))
    return pl.pallas_call(body, name=name, grid=(t // tr,),
                          in_specs=[row, row, pl.BlockSpec((1, w), lambda i: (0, 0))], out_specs=row,
                          out_shape=jax.ShapeDtypeStruct((t, w), BF16), compiler_params=_params(("parallel",)))(y, gl, b)


def _glu_bwd(y, gl, b, dz2, name="s5_glu_bwd", after=None):
    t, w = y.shape
    tr = _pick(t, 512, SUBLANE)

    def body(y_ref, gl_ref, b_ref, dz2_ref, dgl_ref, dza_ref, db_ref):
        @pl.when(pl.program_id(0) == 0)
        def _():
            db_ref[...] = jnp.zeros_like(db_ref)

        z = _gelu_and_grad(y_ref[...])[0]
        s = _sigmoid(gl_ref[...] + b_ref[...])
        dz2v = dz2_ref[...]
        dgl = dz2v * z * s * (1.0 - s)
        dgl_ref[...] = dgl.astype(BF16)
        dza_ref[...] = dz2v * s
        db_ref[...] += jnp.sum(dgl, axis=0, keepdims=True)

    row = pl.BlockSpec((tr, w), lambda i: (i, 0))
    vec = pl.BlockSpec((1, w), lambda i: (0, 0))
    body, in_specs, args = _ordered(body, [row, row, vec, row], [y, gl, b, dz2], after)
    return pl.pallas_call(body, name=name, grid=(t // tr,), in_specs=in_specs, out_specs=[row, row, vec],
                          out_shape=[jax.ShapeDtypeStruct((t, w), BF16), jax.ShapeDtypeStruct((t, w), F32),
                                     jax.ShapeDtypeStruct((1, w), F32)],
                          compiler_params=_params(("arbitrary",)))(*args)


def _gelu_bwd(y, dza, dzb, name="s5_gelu_bwd", after=None):
    t, w = y.shape
    tr = _pick(t, 512, SUBLANE)

    def body(y_ref, a_ref, b_ref, dy_ref):
        dy_ref[...] = (a_ref[...] + b_ref[...]) * _gelu_and_grad(y_ref[...])[1]

    row = pl.BlockSpec((tr, w), lambda i: (i, 0))
    body, in_specs, args = _ordered(body, [row, row, row], [y, dza, dzb], after)
    return pl.pallas_call(body, name=name, grid=(t // tr,), in_specs=in_specs, out_specs=row,
                          out_shape=jax.ShapeDtypeStruct((t, w), F32), compiler_params=_params(("parallel",)))(*args)


def _tri_dot(tri16, x):
    hi = x.astype(BF16)
    r1 = x - hi.astype(F32)
    mid = r1.astype(BF16)
    lo = (r1 - mid.astype(F32)).astype(BF16)
    return _dot(tri16, hi) + _dot(tri16, mid) + _dot(tri16, lo)


def _hgrn_pre(q_in, z, lg):
    lb = _sigmoid(lg[0:1, :] - lg[1:2, :])
    qs, dqs = _silu_and_grad(q_in)
    sz = _sigmoid(z)
    f = lb + (1.0 - lb) * sz
    k = (1.0 - lb) * (1.0 - sz)
    c = HGRN_CHUNK
    r = lax.broadcasted_iota(jnp.int32, (c, c), 0)
    s = lax.broadcasted_iota(jnp.int32, (c, c), 1)
    causal = r >= s
    b = _tri_dot(jnp.where(causal, 1.0, 0.0).astype(BF16), jnp.log(f))
    b_end = b[c - 1:c, :]
    b_mid = b[c // 2 - 1:c // 2, :]
    e_q, e_k, e_0, e_c = jnp.exp(b - b_mid), jnp.exp(b_mid - b), jnp.exp(b), jnp.exp(b_end - b)
    return dict(lb=lb, qs=qs, dqs=dqs, sz=sz, f=f, k=k, causal=causal, b_end=b_end,
                e_q=e_q, e_k=e_k, e_0=e_0, e_c=e_c,
                qt=qs * e_q, kt=k * e_k, q0=qs * e_0, kc=k * e_c)


def _hgrn_fwd(proj, logits, ng):
    t = proj.shape[0]
    c, dh = HGRN_CHUNK, HGRN_DH
    n_chunks = t // c
    subs = HGRN_SUBS if n_chunks % HGRN_SUBS == 0 else 1

    def head(h, sub, q_ref, z_ref, v_ref, g_ref, lg_ref, ng_ref, o_ref, oh_ref, s0_ref, st):
        sl = slice(h * dh, (h + 1) * dh)
        rs = slice(sub * c, (sub + 1) * c)
        s0 = st[h]
        s0_ref[h, sub] = s0
        p = _hgrn_pre(q_ref[rs, sl], z_ref[rs, sl], lg_ref[:, sl])
        v16 = v_ref[rs, sl].astype(BF16)
        a = jnp.where(p["causal"], _dot_nt(p["qt"].astype(BF16), p["kt"].astype(BF16)), 0.0)
        o = _dot_nt(p["q0"].astype(BF16), s0.astype(BF16)) + _dot(a.astype(BF16), v16)
        st[h] = jnp.exp(p["b_end"]) * s0 + _dot_tn(v16, p["kc"].astype(BF16))
        o_ref[rs, sl] = o
        rn = lax.rsqrt(jnp.mean(o * o, axis=-1, keepdims=True) + RMS_EPS)
        oh_ref[rs, sl] = (o * rn * ng_ref[:, sl] * _silu_and_grad(g_ref[rs, sl])[0]).astype(BF16)

    def body(*refs):
        st = refs[-1]

        @pl.when(pl.program_id(0) == 0)
        def _():
            st[...] = jnp.zeros_like(st)

        for sub in range(subs):
            for h in range(HGRN_HEADS):
                head(h, sub, *refs)

    def wide(off):
        return pl.BlockSpec((subs * c, HGRN_WIDTH), lambda i: (i, off))

    return pl.pallas_call(
        body, name="hgrn_fwd", grid=(n_chunks // subs,),
        in_specs=[wide(1), wide(2), wide(3), wide(4),
                  pl.BlockSpec((2, HGRN_WIDTH), lambda i: (0, 0)), pl.BlockSpec((1, HGRN_WIDTH), lambda i: (0, 0))],
        out_specs=[wide(0), wide(0), pl.BlockSpec((HGRN_HEADS, subs, dh, dh), lambda i: (0, i, 0, 0))],
        out_shape=[jax.ShapeDtypeStruct((t, HGRN_WIDTH), F32), jax.ShapeDtypeStruct((t, HGRN_WIDTH), BF16),
                   jax.ShapeDtypeStruct((HGRN_HEADS, n_chunks, dh, dh), F32)],
        scratch_shapes=[pltpu.VMEM((HGRN_HEADS, dh, dh), F32)],
        compiler_params=_params(("arbitrary",)),
    )(proj, proj, proj, proj, logits, ng)


def _hgrn_bwd(proj, o_raw, s0s, doh, logits, ng):
    t = proj.shape[0]
    c, dh = HGRN_CHUNK, HGRN_DH
    n_chunks = t // c
    subs = HGRN_SUBS if n_chunks % HGRN_SUBS == 0 else 1
    last = n_chunks // subs - 1

    def head(h, sub, q_ref, z_ref, v_ref, g_ref, o_ref, s0_ref, doh_ref, lg_ref, ng_ref,
             dq_ref, dz_ref, dv_ref, dg_ref, dng_ref, dlb_ref, dst):
        sl = slice(h * dh, (h + 1) * dh)
        rs = slice(sub * c, (sub + 1) * c)
        p = _hgrn_pre(q_ref[rs, sl], z_ref[rs, sl], lg_ref[:, sl])
        v = v_ref[rs, sl]
        v16 = v.astype(BF16)
        s0 = s0_ref[h, sub]
        ds_end = dst[h]
        ds16 = ds_end.astype(BF16)
        ngv = ng_ref[:, sl]

        o = o_ref[rs, sl]
        dohv = doh_ref[rs, sl]
        sg, dsg = _silu_and_grad(g_ref[rs, sl])
        rn = lax.rsqrt(jnp.mean(o * o, axis=-1, keepdims=True) + RMS_EPS)
        oh = o * rn
        dg_ref[rs, sl] = (dohv * oh * ngv * dsg).astype(BF16)
        don = dohv * sg
        dng_ref[:, sl] += jnp.sum(don * oh, axis=0, keepdims=True)
        doh_n = don * ngv
        do = rn * (doh_n - oh * jnp.mean(doh_n * oh, axis=-1, keepdims=True))
        do16 = do.astype(BF16)

        qt16, kt16, q016, kc16 = (p[n].astype(BF16) for n in ("qt", "kt", "q0", "kc"))
        a = jnp.where(p["causal"], _dot_nt(qt16, kt16), 0.0)
        da = jnp.where(p["causal"], _dot_nt(do16, v16), 0.0)
        da16 = da.astype(BF16)
        dqt = _dot(da16, kt16)
        dq0 = _dot(do16, s0.astype(BF16))
        dkt = _dot_tn(da16, qt16)
        dkc = _dot(v16, ds16)
        dv_ref[rs, sl] = (_dot_tn(a.astype(BF16), do16) + _dot_nt(kc16, ds16)).astype(BF16)
        lam_end = jnp.exp(p["b_end"])
        dst[h] = lam_end * ds_end + _dot_tn(do16, q016)

        qt, kt, q0, kc = (a.astype(F32) for a in (qt16, kt16, q016, kc16))
        db = dqt * qt + dq0 * q0 - dkt * kt - dkc * kc
        db_end = (jnp.sum(dkc * kc, axis=0, keepdims=True)
                  + jnp.sum(ds_end * s0, axis=0, keepdims=True) * lam_end)
        rowi = lax.broadcasted_iota(jnp.int32, (c, dh), 0)
        db = db + jnp.where(rowi == c - 1, db_end, 0.0)
        r = lax.broadcasted_iota(jnp.int32, (c, c), 0)
        s = lax.broadcasted_iota(jnp.int32, (c, c), 1)
        dlf = _tri_dot(jnp.where(s >= r, 1.0, 0.0).astype(BF16), db)

        dqs = dqt * p["e_q"] + dq0 * p["e_0"]
        dq_ref[rs, sl] = (dqs * p["dqs"]).astype(BF16)
        dk = dkt * p["e_k"] + dkc * p["e_c"]
        sz, lb = p["sz"], p["lb"]
        common = dlf / p["f"] - dk
        dz_ref[rs, sl] = ((1.0 - lb) * sz * (1.0 - sz) * common).astype(BF16)
        dlb_ref[:, sl] += jnp.sum((1.0 - sz) * common, axis=0, keepdims=True)

    def body(*refs):
        dng_ref, dlb_ref, dst = refs[-3:]

        @pl.when(pl.program_id(0) == 0)
        def _():
            dst[...] = jnp.zeros_like(dst)
            dng_ref[...] = jnp.zeros_like(dng_ref)
            dlb_ref[...] = jnp.zeros_like(dlb_ref)

        for sub in reversed(range(subs)):
            for h in range(HGRN_HEADS):
                head(h, sub, *refs)

    def wide(off):
        return pl.BlockSpec((subs * c, HGRN_WIDTH), lambda i: (last - i, off))

    vec = pl.BlockSpec((1, HGRN_WIDTH), lambda i: (0, 0))
    act = jax.ShapeDtypeStruct((t, HGRN_WIDTH), BF16)
    vsh = jax.ShapeDtypeStruct((1, HGRN_WIDTH), F32)
    return pl.pallas_call(
        body, name="hgrn_bwd", grid=(n_chunks // subs,),
        in_specs=[wide(1), wide(2), wide(3), wide(4), wide(0),
                  pl.BlockSpec((HGRN_HEADS, subs, dh, dh), lambda i: (0, last - i, 0, 0)),
                  wide(0), pl.BlockSpec((2, HGRN_WIDTH), lambda i: (0, 0)), vec],
        out_specs=[wide(0), wide(0), wide(0), wide(0), vec, vec],
        out_shape=[act, act, act, act, vsh, vsh],
        scratch_shapes=[pltpu.VMEM((HGRN_HEADS, dh, dh), F32)],
        compiler_params=_params(("arbitrary",)),
    )(proj, proj, proj, proj, o_raw, s0s, doh, logits, ng)


def _lb_bwd(logits, dlb):
    def body(lg_ref, d_ref, o_ref):
        lg = lg_ref[...]
        lb = _sigmoid(lg[0:1, :] - lg[1:2, :])
        g = d_ref[...] * lb * (1.0 - lb)
        o_ref[0:1, :] = g
        o_ref[1:2, :] = -g

    return pl.pallas_call(body, name="hgrn_lb_bwd", out_shape=jax.ShapeDtypeStruct(logits.shape, F32),
                          compiler_params=_params())(logits, dlb)


MERGE_TC = 1024
GS_BLOCK = (S5_WIDTH + 4 * HGRN_WIDTH) // MERGE_TC
GH_BLOCK = GS_BLOCK + D_MODEL // MERGE_TC


def _merge_fwd(proj, ys, yh):
    t = proj.shape[0]
    tr = _pick(t, 256, SUBLANE)

    def body(gs_ref, gh_ref, ys_ref, yh_ref, m_ref):
        m_ref[...] = (_sigmoid(gs_ref[...]) * ys_ref[...] + _sigmoid(gh_ref[...]) * yh_ref[...]).astype(BF16)

    blk = pl.BlockSpec((tr, MERGE_TC), lambda i, j: (i, j))
    return pl.pallas_call(
        body, name="merge_fwd", grid=(t // tr, D_MODEL // MERGE_TC),
        in_specs=[pl.BlockSpec((tr, MERGE_TC), lambda i, j: (i, GS_BLOCK + j)),
                  pl.BlockSpec((tr, MERGE_TC), lambda i, j: (i, GH_BLOCK + j)), blk, blk],
        out_specs=blk, out_shape=jax.ShapeDtypeStruct((t, D_MODEL), BF16),
        compiler_params=_params(("parallel", "parallel")),
    )(proj, proj, ys, yh)


def _merge_bwd(proj, ys, yh, dm, after=None):
    t = proj.shape[0]
    tr = _pick(t, 256, SUBLANE)

    def body(gs_ref, gh_ref, ys_ref, yh_ref, dm_ref, dys_ref, dyh_ref, dgs_ref, dgh_ref):
        dmv = dm_ref[...]
        ss, sh = _sigmoid(gs_ref[...]), _sigmoid(gh_ref[...])
        dys_ref[...] = (dmv * ss).astype(BF16)
        dyh_ref[...] = (dmv * sh).astype(BF16)
        dgs_ref[...] = (dmv * ys_ref[...] * ss * (1.0 - ss)).astype(BF16)
        dgh_ref[...] = (dmv * yh_ref[...] * sh * (1.0 - sh)).astype(BF16)

    blk = pl.BlockSpec((tr, MERGE_TC), lambda i, j: (i, j))
    sh16 = jax.ShapeDtypeStruct((t, D_MODEL), BF16)
    in_specs = [pl.BlockSpec((tr, MERGE_TC), lambda i, j: (i, GS_BLOCK + j)),
                pl.BlockSpec((tr, MERGE_TC), lambda i, j: (i, GH_BLOCK + j)), blk, blk, blk]
    body, in_specs, args = _ordered(body, in_specs, [proj, proj, ys, yh, dm], after)
    return pl.pallas_call(
        body, name="merge_bwd", grid=(t // tr, D_MODEL // MERGE_TC),
        in_specs=in_specs,
        out_specs=[blk, blk, blk, blk], out_shape=[sh16, sh16, sh16, sh16],
        compiler_params=_params(("parallel", "parallel")),
    )(*args)


FFN_TC = 128
FFN_ROWS = 128
HALO = SUBLANE


def _pad_rows(dst, src_ref):
    t, c = src_ref.shape
    dst[0:HALO, :] = jnp.zeros((HALO, c), F32)
    dst[HALO:HALO + t, :] = src_ref[...]
    dst[HALO + t:HALO + t + HALO, :] = jnp.zeros((HALO, c), F32)


def _conv3(padded, w, b, r0, nrows):
    x0 = padded[HALO + r0:HALO + r0 + nrows, :]
    x1 = padded[HALO + r0 - 1:HALO + r0 - 1 + nrows, :]
    x2 = padded[HALO + r0 - 2:HALO + r0 - 2 + nrows, :]
    return b + w[0:1, :] * x2 + w[1:2, :] * x1 + w[2:3, :] * x0, (x0, x1, x2)


def _ffn_act_fwd(up, cw, cb):
    t = up.shape[0]
    rows = _pick(t, FFN_ROWS, SUBLANE)
    nvb = D_FF // FFN_TC

    def body(ug_ref, uv_ref, wg_ref, wv_ref, bg_ref, bv_ref, act_ref, pg, pv):
        wg, wv, bg, bv = wg_ref[...], wv_ref[...], bg_ref[...], bv_ref[...]
        _pad_rows(pg, ug_ref)
        _pad_rows(pv, uv_ref)
        for r0 in range(0, t, rows):
            cg, _ = _conv3(pg, wg, bg, r0, rows)
            cv, _ = _conv3(pv, wv, bv, r0, rows)
            act_ref[r0:r0 + rows, :] = (_silu_and_grad(cg)[0] * cv).astype(BF16)

    def colblk(nrow, off):
        return pl.BlockSpec((nrow, FFN_TC), lambda j: (0, off + j))

    return pl.pallas_call(
        body, name="ffn_act_fwd", grid=(nvb,),
        in_specs=[colblk(t, 0), colblk(t, nvb), colblk(3, 0), colblk(3, nvb), colblk(1, 0), colblk(1, nvb)],
        out_specs=colblk(t, 0), out_shape=jax.ShapeDtypeStruct((t, D_FF), BF16),
        scratch_shapes=[pltpu.VMEM((t + 2 * HALO, FFN_TC), F32), pltpu.VMEM((t + 2 * HALO, FFN_TC), F32)],
        compiler_params=_params(("parallel",)),
    )(up, up, cw, cw, cb, cb)


def _ffn_act_bwd(up, dact, cw, cb, after=None):
    t = up.shape[0]
    rows = _pick(t, FFN_ROWS, SUBLANE)
    nvb = D_FF // FFN_TC

    def body(ug_ref, uv_ref, da_ref, wg_ref, wv_ref, bg_ref, bv_ref,
             dug_ref, duv_ref, dwg_ref, dwv_ref, dbg_ref, dbv_ref, pg, pv, dcs):
        wg, wv, bg, bv = wg_ref[...], wv_ref[...], bg_ref[...], bv_ref[...]
        _pad_rows(pg, ug_ref)
        _pad_rows(pv, uv_ref)
        ext = rows + HALO
        acc_g = [jnp.zeros((1, FFN_TC), F32) for _ in range(4)]
        acc_v = [jnp.zeros((1, FFN_TC), F32) for _ in range(4)]
        for r0 in range(0, t, rows):
            cg, xg = _conv3(pg, wg, bg, r0, ext)
            cv, xv = _conv3(pv, wv, bv, r0, ext)
            if r0 + ext <= t:
                dav = da_ref[r0:r0 + ext, :]
            else:
                dav = jnp.concatenate([da_ref[r0:t, :], jnp.zeros((HALO, FFN_TC), F32)], axis=0)
            sg, dsg = _silu_and_grad(cg)
            for h, (dconv, xs, w, acc, out) in enumerate(((dav * cv * dsg, xg, wg, acc_g, dug_ref),
                                                           (dav * sg, xv, wv, acc_v, duv_ref))):
                dcs[h] = dconv
                d0 = dconv[0:rows, :]
                d1 = dcs[h, 1:rows + 1, :]
                d2 = dcs[h, 2:rows + 2, :]
                out[r0:r0 + rows, :] = (w[2:3, :] * d0 + w[1:2, :] * d1 + w[0:1, :] * d2).astype(BF16)
                x0, x1, x2 = xs
                acc[0] = acc[0] + jnp.sum(d0 * x2[0:rows, :], axis=0, keepdims=True)
                acc[1] = acc[1] + jnp.sum(d0 * x1[0:rows, :], axis=0, keepdims=True)
                acc[2] = acc[2] + jnp.sum(d0 * x0[0:rows, :], axis=0, keepdims=True)
                acc[3] = acc[3] + jnp.sum(d0, axis=0, keepdims=True)
        for acc, dw_ref, db_ref in ((acc_g, dwg_ref, dbg_ref), (acc_v, dwv_ref, dbv_ref)):
            dw_ref[0:1, :] = acc[0]
            dw_ref[1:2, :] = acc[1]
            dw_ref[2:3, :] = acc[2]
            db_ref[...] = acc[3]

    def colblk(nrow, off):
        return pl.BlockSpec((nrow, FFN_TC), lambda j: (0, off + j))

    in_specs = [colblk(t, 0), colblk(t, nvb), colblk(t, 0), colblk(3, 0), colblk(3, nvb), colblk(1, 0), colblk(1, nvb)]
    body, in_specs, args = _ordered(body, in_specs, [up, up, dact, cw, cw, cb, cb], after)
    return pl.pallas_call(
        body, name="ffn_act_bwd", grid=(nvb,),
        in_specs=in_specs,
        out_specs=[colblk(t, 0), colblk(t, 0), colblk(3, 0), colblk(3, 0), colblk(1, 0), colblk(1, 0)],
        out_shape=[jax.ShapeDtypeStruct((t, D_FF), BF16), jax.ShapeDtypeStruct((t, D_FF), BF16),
                   jax.ShapeDtypeStruct((3, D_FF), F32), jax.ShapeDtypeStruct((3, D_FF), F32),
                   jax.ShapeDtypeStruct((1, D_FF), F32), jax.ShapeDtypeStruct((1, D_FF), F32)],
        scratch_shapes=[pltpu.VMEM((t + 2 * HALO, FFN_TC), F32), pltpu.VMEM((t + 2 * HALO, FFN_TC), F32),
                        pltpu.VMEM((2, rows + HALO, FFN_TC), F32)],
        compiler_params=_params(("parallel",)),
    )(*args)


def _all_gather(shards, name):
    nw = len(shards)

    def body(*refs):
        x_refs, out_refs = refs[:nw], refs[nw:2 * nw]
        send_sems, recv_sems, local_sems = refs[2 * nw:]
        x, y, c = lax.axis_index("x"), lax.axis_index("y"), lax.axis_index("c")
        me, sibling = (x, y, c), (x, y, 1 - c)
        chips = [(1 - x, y), (x, 1 - y), (1 - x, 1 - y)]

        def copy(w, k, block, to, src=None):
            slot = out_refs[w].at[4 * block[0] + 2 * block[1] + block[2]]
            return pltpu.make_async_remote_copy(
                src_ref=slot if src is None else src, dst_ref=slot,
                send_sem=send_sems.at[w, k], recv_sem=recv_sems.at[w, k],
                device_id=to, device_id_type=MESH)

        mine, first, passed = [], [], []
        for w in range(nw):
            cp = pltpu.make_async_copy(x_refs[w], out_refs[w].at[4 * x + 2 * y + c], local_sems.at[w])
            cp.start()
            mine.append(cp)
            first.append(copy(w, 0, me, sibling, src=x_refs[w]))
            first += [copy(w, 1 + j, me, (*chip, c), src=x_refs[w]) for j, chip in enumerate(chips)]
        for cp in first:
            cp.start()
        for w in range(nw):
            for j, chip in enumerate(chips):
                copy(w, 1 + j, (*chip, c), me).wait_recv()
                fwd = copy(w, 4 + j, (*chip, c), sibling)
                fwd.start()
                passed.append(fwd)
        for w in range(nw):
            copy(w, 0, sibling, me).wait_recv()
            for j, chip in enumerate(chips):
                copy(w, 4 + j, (*chip, 1 - c), me).wait_recv()
        for cp in first + passed:
            cp.wait_send()
        for cp in mine:
            cp.wait()

    anyspec = pl.BlockSpec(memory_space=pl.ANY)
    return pl.pallas_call(
        body, name=name,
        in_specs=[anyspec] * nw, out_specs=[anyspec] * nw,
        out_shape=[jax.ShapeDtypeStruct((N_DEV,) + s.shape, s.dtype) for s in shards],
        scratch_shapes=[pltpu.SemaphoreType.DMA((nw, 7)), pltpu.SemaphoreType.DMA((nw, 7)),
                        pltpu.SemaphoreType.DMA((nw,))],
    )(*shards)


HBM_SPEC = pl.BlockSpec(memory_space=pltpu.HBM)
SEM_SPEC = pl.BlockSpec(memory_space=pltpu.SEMAPHORE)
ANY_SPEC = pl.BlockSpec(memory_space=pl.ANY)
DATAFLOW = pltpu.SideEffectType.DATAFLOW_SIDE_EFFECTING


def _my_index():
    return 4 * lax.axis_index("x") + 2 * lax.axis_index("y") + lax.axis_index("c")


def _peers():
    x, y, c = lax.axis_index("x"), lax.axis_index("y"), lax.axis_index("c")
    peers = []
    for k in range(1, N_DEV):
        px = 1 - x if k & 4 else x
        py = 1 - y if k & 2 else y
        pc = 1 - c if k & 1 else c
        peers.append((k, (px, py, pc), 4 * px + 2 * py + pc))
    return peers


def _split_copy(src_ref, land_ref, send_sems, recv_sems, w, k, peer, slot, scatter, outgoing):
    return pltpu.make_async_remote_copy(
        src_ref=src_ref.at[slot] if scatter else src_ref,
        dst_ref=land_ref.at[_my_index() if outgoing else slot],
        send_sem=send_sems.at[w * (N_DEV - 1) + k - 1], recv_sem=recv_sems.at[w * (N_DEV - 1) + k - 1],
        device_id=peer, device_id_type=MESH)


def _landing_zone(src, scatter):
    me = _my_index()
    own = lax.dynamic_index_in_dim(src, me, 0, keepdims=True) if scatter else src[None]
    shape = src.shape if scatter else (N_DEV,) + src.shape
    return lax.dynamic_update_slice_in_dim(lax.empty(shape, src.dtype), own, me, 0)


def _exchange_start(srcs, scatter, after, name, lands=None):
    nw = len(srcs)
    if lands is None:
        lands = [_landing_zone(s, scatter) for s in srcs]

    afters = [] if after is None else [after]

    def body(*refs):
        s_refs, l_refs = refs[:nw], refs[nw:2 * nw]
        send_sems, recv_sems = refs[2 * nw + len(afters)], refs[2 * nw + len(afters) + 1]
        token = refs[-1]
        for w in range(nw):
            for k, peer, slot in _peers():
                _split_copy(s_refs[w], l_refs[w], send_sems, recv_sems, w, k, peer, slot, scatter, True).start()
        token[...] = jnp.zeros_like(token)

    sems = pltpu.SemaphoreType.DMA((nw * (N_DEV - 1),))
    outs = pl.pallas_call(
        body, name=name,
        out_shape=(sems, sems, *[pltpu.HBM(a.shape, a.dtype) for a in (*srcs, *lands)],
                   jax.ShapeDtypeStruct((SUBLANE, LANE), F32)),
        in_specs=[HBM_SPEC] * (2 * nw) + [ANY_SPEC] * len(afters),
        out_specs=(SEM_SPEC, SEM_SPEC, *[HBM_SPEC] * (2 * nw), pl.BlockSpec(memory_space=pltpu.VMEM)),
        input_output_aliases={i: 2 + i for i in range(2 * nw)},
        compiler_params=pltpu.CompilerParams(has_side_effects=DATAFLOW),
    )(*[pltpu.with_memory_space_constraint(a, pltpu.HBM) for a in (*srcs, *lands)], *afters)
    return dict(sems=outs[:2], srcs=outs[2:2 + nw], lands=outs[2 + nw:2 + 2 * nw], token=outs[-1], scatter=scatter)


def _exchange_wait(handle, afters, name):
    srcs, lands, scatter = handle["srcs"], handle["lands"], handle["scatter"]
    nw = len(srcs)

    def body(*refs):
        s_refs, l_refs = refs[:nw], refs[nw:2 * nw]
        send_sems, recv_sems = refs[2 * nw], refs[2 * nw + 1]
        for w in range(nw):
            for k, peer, slot in _peers():
                cp = _split_copy(s_refs[w], l_refs[w], send_sems, recv_sems, w, k, peer, slot, scatter, False)
                cp.wait_send()
                cp.wait_recv()

    outs = pl.pallas_call(
        body, name=name,
        out_shape=tuple(pltpu.HBM(a.shape, a.dtype) for a in (*srcs, *lands)),
        in_specs=[HBM_SPEC] * (2 * nw) + [SEM_SPEC, SEM_SPEC] + [ANY_SPEC] * len(afters),
        out_specs=tuple([HBM_SPEC] * (2 * nw)),
        input_output_aliases={i: i for i in range(2 * nw)},
        compiler_params=pltpu.CompilerParams(has_side_effects=DATAFLOW),
    )(*srcs, *lands, *handle["sems"], *afters)
    return list(outs[nw:])


def _chips_and_sibling():
    x, y, c = lax.axis_index("x"), lax.axis_index("y"), lax.axis_index("c")
    return [(1 - x, y), (x, 1 - y), (1 - x, 1 - y)], (x, y, 1 - c), c


def _slot(px, py, pc):
    return 4 * px + 2 * py + pc


def _two_level_start(shards, name):
    nw = len(shards)
    me = _my_index()
    lands = [lax.dynamic_update_slice_in_dim(lax.empty((N_DEV,) + s.shape, s.dtype), s[None], me, 0) for s in shards]

    def body(*refs):
        s_refs, l_refs = refs[:nw], refs[nw:2 * nw]
        send_sems, recv_sems, token = refs[2 * nw], refs[2 * nw + 1], refs[-1]
        chips, sibling, c = _chips_and_sibling()
        for w in range(nw):
            for k, to in enumerate([sibling] + [(*chip, c) for chip in chips]):
                pltpu.make_async_remote_copy(
                    src_ref=s_refs[w], dst_ref=l_refs[w].at[_my_index()],
                    send_sem=send_sems.at[4 * w + k], recv_sem=recv_sems.at[4 * w + k],
                    device_id=to, device_id_type=MESH).start()
        token[...] = jnp.zeros_like(token)

    sems = pltpu.SemaphoreType.DMA((4 * nw,))
    outs = pl.pallas_call(
        body, name=name,
        out_shape=(sems, sems, *[pltpu.HBM(a.shape, a.dtype) for a in (*shards, *lands)],
                   jax.ShapeDtypeStruct((SUBLANE, LANE), F32)),
        in_specs=[HBM_SPEC] * (2 * nw),
        out_specs=(SEM_SPEC, SEM_SPEC, *[HBM_SPEC] * (2 * nw), pl.BlockSpec(memory_space=pltpu.VMEM)),
        input_output_aliases={i: 2 + i for i in range(2 * nw)},
        compiler_params=pltpu.CompilerParams(has_side_effects=DATAFLOW),
    )(*[pltpu.with_memory_space_constraint(a, pltpu.HBM) for a in (*shards, *lands)])
    return dict(sems=outs[:2], srcs=outs[2:2 + nw], lands=outs[2 + nw:2 + 2 * nw], token=outs[-1])


def _two_level_pass(handle, afters, name):
    srcs, lands = handle["srcs"], handle["lands"]
    nw = len(srcs)

    def body(*refs):
        s_refs, l_refs = refs[:nw], refs[nw:2 * nw]
        send_a, recv_a = refs[2 * nw], refs[2 * nw + 1]
        send_b, recv_b = refs[2 * nw + 2 + len(afters)], refs[2 * nw + 3 + len(afters)]
        chips, sibling, c = _chips_and_sibling()
        for w in range(nw):
            for j, chip in enumerate(chips):
                landed = l_refs[w].at[_slot(*chip, c)]
                pltpu.make_async_remote_copy(
                    src_ref=s_refs[w], dst_ref=landed, send_sem=send_a.at[4 * w + 1 + j], recv_sem=recv_a.at[4 * w + 1 + j],
                    device_id=(*chip, c), device_id_type=MESH).wait_recv()
                pltpu.make_async_remote_copy(
                    src_ref=landed, dst_ref=landed, send_sem=send_b.at[3 * w + j], recv_sem=recv_b.at[3 * w + j],
                    device_id=sibling, device_id_type=MESH).start()

    sems = pltpu.SemaphoreType.DMA((3 * nw,))
    outs = pl.pallas_call(
        body, name=name,
        out_shape=(sems, sems, *[pltpu.HBM(a.shape, a.dtype) for a in (*srcs, *lands)]),
        in_specs=[HBM_SPEC] * (2 * nw) + [SEM_SPEC, SEM_SPEC] + [ANY_SPEC] * len(afters),
        out_specs=(SEM_SPEC, SEM_SPEC, *[HBM_SPEC] * (2 * nw)),
        input_output_aliases={i: 2 + i for i in range(2 * nw)},
        compiler_params=pltpu.CompilerParams(has_side_effects=DATAFLOW),
    )(*srcs, *lands, *handle["sems"], *afters)
    return dict(sems=handle["sems"], sems_pass=outs[:2], srcs=outs[2:2 + nw], lands=outs[2 + nw:2 + 2 * nw])


def _two_level_wait(handle, name):
    srcs, lands = handle["srcs"], handle["lands"]
    nw = len(srcs)

    def body(*refs):
        s_refs, l_refs = refs[:nw], refs[nw:2 * nw]
        send_a, recv_a, send_b, recv_b = refs[2 * nw:2 * nw + 4]
        chips, sibling, c = _chips_and_sibling()
        x, y = sibling[0], sibling[1]
        for w in range(nw):
            first = pltpu.make_async_remote_copy(
                src_ref=s_refs[w], dst_ref=l_refs[w].at[_slot(x, y, 1 - c)], send_sem=send_a.at[4 * w],
                recv_sem=recv_a.at[4 * w], device_id=sibling, device_id_type=MESH)
            first.wait_send()
            first.wait_recv()
            for j, chip in enumerate(chips):
                pltpu.make_async_remote_copy(
                    src_ref=s_refs[w], dst_ref=l_refs[w].at[_slot(*chip, c)], send_sem=send_a.at[4 * w + 1 + j],
                    recv_sem=recv_a.at[4 * w + 1 + j], device_id=(*chip, c), device_id_type=MESH).wait_send()
                passed = pltpu.make_async_remote_copy(
                    src_ref=l_refs[w].at[_slot(*chip, c)], dst_ref=l_refs[w].at[_slot(*chip, 1 - c)],
                    send_sem=send_b.at[3 * w + j], recv_sem=recv_b.at[3 * w + j], device_id=sibling, device_id_type=MESH)
                passed.wait_send()
                passed.wait_recv()

    outs = pl.pallas_call(
        body, name=name,
        out_shape=tuple(pltpu.HBM(a.shape, a.dtype) for a in (*srcs, *lands)),
        in_specs=[HBM_SPEC] * (2 * nw) + [SEM_SPEC] * 4,
        out_specs=tuple([HBM_SPEC] * (2 * nw)),
        input_output_aliases={i: i for i in range(2 * nw)},
        compiler_params=pltpu.CompilerParams(has_side_effects=DATAFLOW),
    )(*srcs, *lands, *handle["sems"], *handle["sems_pass"])
    return list(outs[nw:])


def _adamw(w, g, m, v):
    m = ADAM_B1 * m + (1.0 - ADAM_B1) * g
    v = ADAM_B2 * v + (1.0 - ADAM_B2) * (g * g)
    m_hat = m / (1.0 - ADAM_B1 ** ADAM_STEP)
    v_hat = v / (1.0 - ADAM_B2 ** ADAM_STEP)
    delta = -ADAM_LR * (m_hat / (jnp.sqrt(v_hat) + ADAM_EPS) + ADAM_WD * w)
    return delta, m, v


def _sum_adam(parts, w, m, v, name):
    _, r, c = parts.shape
    tr = _pick(r, 128, 16)

    def body(p_ref, w_ref, m_ref, v_ref, g_ref, d_ref, mo_ref, vo_ref):
        g = p_ref[0].astype(F32)
        for s in range(1, N_DEV):
            g = g + p_ref[s].astype(F32)
        g_ref[...] = g
        d_ref[...], mo_ref[...], vo_ref[...] = _adamw(w_ref[...], g, m_ref[...], v_ref[...])

    row = pl.BlockSpec((tr, c), lambda i: (i, 0))
    sh = jax.ShapeDtypeStruct((r, c), F32)
    return pl.pallas_call(
        body, name=name, grid=(r // tr,),
        in_specs=[pl.BlockSpec((N_DEV, tr, c), lambda i: (0, i, 0)), row, row, row],
        out_specs=[row, row, row, row], out_shape=[sh, sh, sh, sh],
        compiler_params=_params(("parallel",)),
    )(parts, w, m, v)


def _sum_slots(parts, name):
    _, r, c = parts.shape
    tr = _pick(r, 512, SUBLANE)

    def body(p_ref, o_ref):
        g = p_ref[0]
        for s in range(1, N_DEV):
            g = g + p_ref[s]
        o_ref[...] = g

    return pl.pallas_call(
        body, name=name, grid=(r // tr,),
        in_specs=[pl.BlockSpec((N_DEV, tr, c), lambda i: (0, i, 0))],
        out_specs=pl.BlockSpec((tr, c), lambda i: (i, 0)), out_shape=jax.ShapeDtypeStruct((r, c), F32),
        compiler_params=_params(("parallel",)),
    )(parts)


def _adam_rows(g, w, m, v, name):
    r, c = g.shape
    tr = _pick(r, 512, SUBLANE)

    def body(g_ref, w_ref, m_ref, v_ref, d_ref, mo_ref, vo_ref):
        d_ref[...], mo_ref[...], vo_ref[...] = _adamw(w_ref[...], g_ref[...], m_ref[...], v_ref[...])

    row = pl.BlockSpec((tr, c), lambda i: (i, 0))
    sh = jax.ShapeDtypeStruct((r, c), F32)
    return pl.pallas_call(body, name=name, grid=(r // tr,), in_specs=[row] * 4, out_specs=[row] * 3,
                          out_shape=[sh, sh, sh], compiler_params=_params(("parallel",)))(g, w, m, v)


def _pack(arrays):
    flat = jnp.concatenate([a.reshape(-1).astype(F32) for a in arrays])
    pad = (-flat.shape[0]) % (SUBLANE * LANE)
    return jnp.pad(flat, (0, pad)).reshape(-1, LANE)


def _unpack(packed, shapes):
    flat = packed.reshape(-1)
    out, off = [], 0
    for s in shapes:
        n = math.prod(s)
        out.append(flat[off:off + n].reshape(s))
        off += n
    return out


def _block_diag(t):
    eye = jnp.eye(S5_SUPER, dtype=bool)
    bd = jnp.where(eye[None, :, None, :, None], t[:, :, :, None, :], 0.0)
    return bd.reshape(S5_SUPER, S5_SUPER * t.shape[2], S5_SUPER * t.shape[3])


def _diag_blocks(dense, a, b):
    x = dense.reshape(S5_SUPER, S5_SUPER, a, S5_SUPER, b)
    return jnp.moveaxis(jnp.diagonal(x, axis1=1, axis2=3), -1, 1)


def _s5_layouts(b_re, b_im, c_re, c_im, d):
    g2 = (S5_GROUPS // S5_SUPER, S5_SUPER)
    bt = lambda b: _block_diag(b.reshape(*g2, S5_STATE, S5_GROUP).transpose(0, 1, 3, 2))
    ct = lambda c: _block_diag(c.reshape(*g2, S5_GROUP, S5_STATE).transpose(0, 1, 3, 2))
    bsg = jnp.concatenate([bt(b_re), bt(b_im)], axis=2).astype(BF16)
    ccat = jnp.concatenate([ct(c_re), -ct(c_im)], axis=1).astype(BF16)
    return bsg, ccat, d.reshape(1, S5_WIDTH)


def _s5_param_grads(gb, gc):
    n = S5_LANES
    gb_re = _diag_blocks(gb[:, :, 0:n], S5_GROUP, S5_STATE).transpose(0, 1, 3, 2).reshape(S5_GROUPS, S5_STATE, S5_GROUP)
    gb_im = _diag_blocks(gb[:, :, n:2 * n], S5_GROUP, S5_STATE).transpose(0, 1, 3, 2).reshape(S5_GROUPS, S5_STATE, S5_GROUP)
    gc_re = _diag_blocks(gc[:, 0:n, :], S5_STATE, S5_GROUP).transpose(0, 1, 3, 2).reshape(S5_GROUPS, S5_GROUP, S5_STATE)
    gc_im = -_diag_blocks(gc[:, n:2 * n, :], S5_STATE, S5_GROUP).transpose(0, 1, 3, 2).reshape(S5_GROUPS, S5_GROUP, S5_STATE)
    return gb_re, gb_im, gc_re, gc_im


def _local_step(x, target, weight, emit, small, after=None):
    sp = small
    a_re, a_im = sp["s5_a_re"], sp["s5_a_im"]
    ldt = sp["s5_log_dt"].reshape(S5_GROUPS, 1)

    h1 = _rms_fwd(x, sp["ln_mix_g"], "rms_mix", after=after)
    w_in = weight("w_in", h1)
    proj = _mm_nn(h1, w_in, "mm_in", after=weight("after_w_in", None))
    conv_w = weight("conv_w", None)
    disc = _s5_param_fwd(a_re, a_im, ldt)
    bsg, ccat, d_row = sp["s5_layouts"]
    abar_t, coef_t = _s5_to_tile(disc[0], disc[1]), _s5_to_tile(disc[2], disc[3])
    y, sb = _s5_fwd(proj, bsg, ccat, d_row, abar_t, coef_t)
    z16 = _gelu_fwd(y)
    w_glu = weight("s5_w_glu", z16)
    gl = _mm_nn(z16, w_glu, "mm_glu")
    z2 = _glu_fwd(y, gl, sp["s5_b_glu"])
    w_ps = weight("w_proj_s5", z2)
    ys = _mm_nn(z2, w_ps, "mm_proj_s5")
    o_raw, oh, s0s = _hgrn_fwd(proj, sp["hgrn_lb_logits"], sp["hgrn_norm_g"])
    w_ph = weight("w_proj_hgrn", oh)
    yh = _mm_nn(oh, w_ph, "mm_proj_hgrn")
    merged = _merge_fwd(proj, ys, yh)
    w_out = weight("w_out", merged)
    x1 = _mm_nn(merged, w_out, "mm_out", res=x)
    h2 = _rms_fwd(x1, sp["ln_ffn_g"], "rms_ffn")
    w_up = weight("w_up", h2)
    up = _mm_nn(h2, w_up, "mm_up")
    act = _ffn_act_fwd(up, conv_w, sp["conv_b"])
    w_down = weight("w_down", act)
    x2 = _mm_nn(act, w_down, "mm_down", res=x1)
    dx2, dx2_16, g_ln_final, loss = _loss_head(x2, sp["ln_final_g"], target)

    dact = _mm_nt(dx2_16, w_down, "mm_down_dx")
    tok = emit("w_down", _mm_tn(act, dx2_16, 1, "mm_down_dw"))
    dup_g, dup_v, dcw_g, dcw_v, dcb_g, dcb_v = _ffn_act_bwd(up, dact, conv_w, sp["conv_b"], after=tok)
    dup = jnp.concatenate([dup_g, dup_v], axis=1)
    g_conv_w = jnp.concatenate([dcw_g, dcw_v], axis=1)
    g_conv_b = jnp.concatenate([dcb_g, dcb_v], axis=1)
    dh2 = _mm_nt(dup, w_up, "mm_up_dx")
    tok = emit("w_up", _mm_tn(h2, dup, N_DEV, "mm_up_dw"))
    dx1, dx1_16, g_ln_ffn = _rms_bwd(x1, sp["ln_ffn_g"], dh2, dx2, "rms_ffn_bwd", True, after=tok)

    dmerged = _mm_nt(dx1_16, w_out, "mm_out_dx")
    tok = emit("w_out", _mm_tn(merged, dx1_16, 1, "mm_out_dw"))
    dys, dyh, dgs, dgh = _merge_bwd(proj, ys, yh, dmerged, after=tok)
    doh = _mm_nt(dyh, w_ph, "mm_proj_hgrn_dx")
    tok = emit("w_proj_hgrn", _mm_tn(oh, dyh, N_DEV, "mm_proj_hgrn_dw"))
    dz2 = _mm_nt(dys, w_ps, "mm_proj_s5_dx", after=tok)
    tok = emit("w_proj_s5", _mm_tn(z2, dys, N_DEV, "mm_proj_s5_dw"))
    dgl, dza, g_b_glu = _glu_bwd(y, gl, sp["s5_b_glu"], dz2, after=tok)
    dzb = _mm_nt(dgl, w_glu, "mm_glu_dx")
    tok = emit("s5_w_glu", _mm_tn(z16, dgl, 1, "mm_glu_dw"))
    dy = _gelu_bwd(y, dza, dzb, after=tok)
    du, gb, gc, gd, g_abar_t, g_coef_t = _s5_bwd(proj, dy, sb, bsg, ccat, d_row, abar_t, coef_t)
    g_a_re, g_a_im, g_ldt = _s5_param_bwd(a_re, a_im, ldt, [*_s5_from_tile(g_abar_t), *_s5_from_tile(g_coef_t)])
    g_b_re, g_b_im, g_c_re, g_c_im = _s5_param_grads(gb, gc)
    dq, dz, dv, dg, g_norm, dlb = _hgrn_bwd(proj, o_raw, s0s, doh, sp["hgrn_lb_logits"], sp["hgrn_norm_g"])
    g_logits = _lb_bwd(sp["hgrn_lb_logits"], dlb)

    small_g = dict(s5_a_re=g_a_re, s5_a_im=g_a_im, s5_log_dt=g_ldt.reshape(1, S5_GROUPS),
                   s5_b_re=g_b_re, s5_b_im=g_b_im, s5_c_re=g_c_re, s5_c_im=g_c_im,
                   s5_d=gd.reshape(S5_GROUPS, S5_GROUP), s5_b_glu=g_b_glu, hgrn_lb_logits=g_logits,
                   hgrn_norm_g=g_norm, ln_ffn_g=g_ln_ffn, conv_w=g_conv_w, conv_b=g_conv_b, ln_final_g=g_ln_final,
                   loss=loss[0, 0:1])
    tok_small = emit("small", small_g)

    dproj = jnp.concatenate([du, dq, dz, dv, dg, dgs, dgh], axis=1)
    tok = emit("w_in", _mm_tn(h1, dproj, N_DEV, "mm_in_dw", after=tok_small))
    dh1 = _mm_nt(dproj, w_in, "mm_in_dx")
    grad_x, g_ln_mix = _rms_bwd(x, sp["ln_mix_g"], dh1, dx1, "rms_mix_bwd", False, after=tok)
    return grad_x, g_ln_mix


BIG = ("w_in", "s5_w_glu", "w_proj_s5", "w_proj_hgrn", "w_out", "w_up", "w_down")
COL_SHARDED = ("w_in", "w_proj_s5", "w_proj_hgrn", "w_up")
SMALL = ("ln_mix_g", "s5_a_re", "s5_a_im", "s5_log_dt", "s5_b_re", "s5_b_im", "s5_c_re", "s5_c_im", "s5_d",
         "s5_b_glu", "hgrn_lb_logits", "hgrn_norm_g", "ln_ffn_g", "conv_b", "ln_final_g")
WEIGHTS = ("ln_mix_g", "w_in", "s5_a_re", "s5_a_im", "s5_log_dt", "s5_b_re", "s5_b_im", "s5_c_re", "s5_c_im", "s5_d",
           "s5_w_glu", "s5_b_glu", "w_proj_s5", "hgrn_lb_logits", "hgrn_norm_g", "w_proj_hgrn", "w_out", "ln_ffn_g",
           "w_up", "conv_w", "conv_b", "w_down", "ln_final_g")


def kernel(x, ln_mix_g, w_in, s5_a_re, s5_a_im, s5_log_dt, s5_b_re, s5_b_im, s5_c_re, s5_c_im, s5_d, s5_w_glu, s5_b_glu, w_proj_s5, hgrn_lb_logits, hgrn_norm_g, w_proj_hgrn, w_out, ln_ffn_g, w_up, conv_w, conv_b, w_down, ln_final_g, loss_target, m_ln_mix_g, m_w_in, m_s5_a_re, m_s5_a_im, m_s5_log_dt, m_s5_b_re, m_s5_b_im, m_s5_c_re, m_s5_c_im, m_s5_d, m_s5_w_glu, m_s5_b_glu, m_w_proj_s5, m_hgrn_lb_logits, m_hgrn_norm_g, m_w_proj_hgrn, m_w_out, m_ln_ffn_g, m_w_up, m_conv_w, m_conv_b, m_w_down, m_ln_final_g, v_ln_mix_g, v_w_in, v_s5_a_re, v_s5_a_im, v_s5_log_dt, v_s5_b_re, v_s5_b_im, v_s5_c_re, v_s5_c_im, v_s5_d, v_s5_w_glu, v_s5_b_glu, v_w_proj_s5, v_hgrn_lb_logits, v_hgrn_norm_g, v_w_proj_hgrn, v_w_out, v_ln_ffn_g, v_w_up, v_conv_w, v_conv_b, v_w_down, v_ln_final_g):
    given = dict(locals())
    w = {n: given[n] for n in WEIGHTS}
    mom = {n: given["m_" + n] for n in WEIGHTS}
    var = {n: given["v_" + n] for n in WEIGHTS}

    first = _two_level_start([w_in[0].astype(BF16), conv_w[0]], "gather_first_start")
    zero = first["token"][0, 0]
    packed_small = SMALL[1:]
    pw, pm, pv = (_pack([d[n] for n in packed_small]) + zero for d in (w, mom, var))
    layouts = _s5_layouts(s5_b_re[0] + zero, s5_b_im[0], s5_c_re[0] + zero, s5_c_im[0], s5_d[0])
    gather_groups = (("s5_w_glu", "w_proj_s5", "w_proj_hgrn", "w_out"), ("w_up",), ("w_down",))
    shard16 = {n: w[n][0].astype(BF16) + zero.astype(BF16) for g in gather_groups for n in g}
    zones = {n: _landing_zone(s, False) for n, s in shard16.items()}
    pending, ready = {}, {}

    def weight(name, after):
        if "w_in" not in ready:
            local_work = [after, pw, pm, pv, layouts[0], layouts[1], *zones.values()]
            passed = _two_level_pass(first, local_work, "gather_first_pass")
            ready["w_in"], conv_w_all = _two_level_wait(passed, "gather_first_wait")
            ready["conv_w"] = conv_w_all.transpose(1, 0, 2).reshape(3, 2 * D_FF)
            token = ready["w_in"]
            for i, group in enumerate(gather_groups):
                handle = _exchange_start([shard16[n] for n in group], False, token, f"gather_start_{i}",
                                         lands=[zones[n] for n in group])
                token = handle["token"]
                for n in group:
                    pending[n] = (group, handle, f"gather_wait_{i}")
            ready["after_w_in"] = token
        if name not in ready:
            group, handle, wait_name = pending[name]
            for n, g in zip(group, _exchange_wait(handle, [after], wait_name)):
                ready[n] = g
        g = ready[name]
        return g if name not in BIG or name in COL_SHARDED else g.reshape(1, N_DEV * g.shape[1], g.shape[2])

    scatter_groups = (("w_down",), ("w_up",), ("w_out", "w_proj_hgrn", "w_proj_s5", "s5_w_glu"), ("w_in",))
    emitted, scatters = {}, []
    packed_names = SMALL[1:] + ("conv_w", "loss")

    def emit(name, grad):
        if name == "small":
            emitted[name] = ([grad[n].shape for n in packed_names],
                             _exchange_start([_pack([grad[n] for n in packed_names])], False, None, "small_start"))
            return emitted[name][1]["token"]
        emitted[name] = grad if name in COL_SHARDED else grad.reshape(N_DEV, -1, grad.shape[2])
        group = scatter_groups[len(scatters)]
        if not all(n in emitted for n in group):
            return None
        handle = _exchange_start([emitted[n] for n in group], True, None, f"scatter_start_{len(scatters)}")
        scatters.append((group, handle))
        return handle["token"]

    small = dict(ln_mix_g=ln_mix_g, s5_a_re=s5_a_re[0], s5_a_im=s5_a_im[0], s5_log_dt=s5_log_dt, s5_layouts=layouts,
                 s5_b_glu=s5_b_glu, hgrn_lb_logits=hgrn_lb_logits, hgrn_norm_g=hgrn_norm_g, ln_ffn_g=ln_ffn_g,
                 conv_b=conv_b, ln_final_g=ln_final_g.reshape(1, D_MODEL))
    grad_x, g_ln_mix = _local_step(x[0], loss_target[0], weight, emit, small, after=first["token"])

    shapes, handle = emitted["small"]
    total = _sum_slots(_exchange_wait(handle, [grad_x], "small_wait")[0], "sum_small")
    summed = dict(zip(packed_names, _unpack(total, shapes)))
    mix_all = _all_gather([g_ln_mix.reshape(-1, LANE)], "gather_ln_mix")[0]
    summed["ln_mix_g"] = _sum_slots(mix_all, "sum_ln_mix").reshape(1, D_MODEL)

    grads, delta, new_m, new_v = {}, {}, {}, {}
    afters = [grad_x, total]
    for i, (group, handle) in enumerate(scatters):
        for n, r in zip(group, _exchange_wait(handle, afters, f"scatter_wait_{i}")):
            g, d, m2, v2 = _sum_adam(r, w[n][0], mom[n][0], var[n][0], "adam_" + n)
            grads[n], delta[n], new_m[n], new_v[n] = g[None], d[None], m2[None], v2[None]
        if i == len(scatters) - 2:
            afters = [delta[n] for g2, _ in scatters[:-1] for n in g2]

    d_s, m_s, v_s = _adam_rows(_pack([summed[n] for n in packed_small]), pw, pm, pv, "adam_small")
    wshapes = [w[n].shape for n in packed_small]
    for n, d, m2, v2 in zip(packed_small, _unpack(d_s, wshapes), _unpack(m_s, wshapes), _unpack(v_s, wshapes)):
        grads[n], delta[n], new_m[n], new_v[n] = summed[n].reshape(w[n].shape), d, m2, v2
    grads["ln_mix_g"] = summed["ln_mix_g"]
    delta["ln_mix_g"], new_m["ln_mix_g"], new_v["ln_mix_g"] = _adam_rows(summed["ln_mix_g"], ln_mix_g, m_ln_mix_g,
                                                                         v_ln_mix_g, "adam_ln_mix")
    me = 4 * lax.axis_index("x") + 2 * lax.axis_index("y") + lax.axis_index("c")
    ncol = conv_w.shape[2]
    g_cw = lax.dynamic_slice_in_dim(summed["conv_w"], me * ncol, ncol, axis=1)
    d_cw, m_cw, v_cw = _adam_rows(g_cw, conv_w[0], m_conv_w[0], v_conv_w[0], "adam_conv_w")
    grads["conv_w"], delta["conv_w"], new_m["conv_w"], new_v["conv_w"] = g_cw[None], d_cw[None], m_cw[None], v_cw[None]

    return (summed["loss"].reshape(()), grad_x[None], *[grads[n] for n in WEIGHTS], *[delta[n] for n in WEIGHTS],
            *[new_m[n] for n in WEIGHTS], *[new_v[n] for n in WEIGHTS])
```

```python
import math

import jax
import jax.numpy as jnp
from jax import lax
from jax.experimental import pallas as pl
from jax.experimental.pallas import tpu as pltpu

F32 = jnp.float32
BF16 = jnp.bfloat16

N_DEV = 8
D_MODEL = 2048
S5_WIDTH = 1024
S5_GROUP = 16
S5_GROUPS = 64
S5_STATE = 64
S5_MAX_RE = -1e-4
S5_SUPER = 8
S5_LANES = S5_SUPER * S5_STATE
HGRN_WIDTH = 1024
HGRN_HEADS = 8
HGRN_DH = 128
HGRN_CHUNK = 64
HGRN_SUBS = 2
D_FF = 5632
RMS_EPS = 1e-6
ADAM_LR = 0.001
ADAM_B1 = 0.9
ADAM_B2 = 0.999
ADAM_EPS = 1e-08
ADAM_WD = 0.01
ADAM_STEP = 10

LANE = 128
SUBLANE = 8
VMEM_LIMIT = 48 * 1024 * 1024
MESH = pl.DeviceIdType.MESH
GELU_C = math.sqrt(2.0 / math.pi)
GELU_A = 0.044715


def _params(sem=None):
    return pltpu.CompilerParams(dimension_semantics=sem, vmem_limit_bytes=VMEM_LIMIT)


def _pick(n, cap, unit=LANE):
    best = None
    for t in range(unit, min(n, cap) + 1, unit):
        if n % t == 0:
            best = t
    return best if best is not None else n


def _ordered(body, in_specs, args, after):
    if after is None:
        return body, list(in_specs), list(args)
    n_in = len(args)

    def ordered_body(*refs):
        return body(*refs[:n_in], *refs[n_in + 1:])

    return ordered_body, [*in_specs, pl.BlockSpec(memory_space=pl.ANY)], [*args, after]


def _sigmoid(x):
    return 0.5 * jnp.tanh(0.5 * x) + 0.5


def _silu_and_grad(x):
    s = _sigmoid(x)
    return x * s, s * (1.0 + x * (1.0 - s))


def _gelu_and_grad(y):
    inner = GELU_C * (y + GELU_A * y * y * y)
    th = jnp.tanh(inner)
    val = 0.5 * y * (1.0 + th)
    grad = 0.5 * (1.0 + th) + 0.5 * y * (1.0 - th * th) * GELU_C * (1.0 + 3.0 * GELU_A * y * y)
    return val, grad


def _dot(a, b):
    return jnp.dot(a, b, preferred_element_type=F32)


def _dot_nt(a, b):
    return lax.dot_general(a, b, (((1,), (1,)), ((), ())), preferred_element_type=F32)


def _dot_tn(a, b):
    return lax.dot_general(a, b, (((0,), (0,)), ((), ())), preferred_element_type=F32)


def _blocks_per_step(nb, ns, tn, cap=2048):
    if tn != ns:
        return 1
    best = 1
    for b in range(1, nb + 1):
        if nb % b == 0 and b * ns <= cap:
            best = b
    return best


def _mm_nn(a, w, name, res=None, out_dtype=F32, after=None):
    m, kdim = a.shape
    nb, _, ns = w.shape
    tm, tk, tn = _pick(m, 512), _pick(kdim, 2048), _pick(ns, 1536)
    npb, nk = ns // tn, kdim // tk
    bps = _blocks_per_step(nb, ns, tn)
    assert bps == 1 or nk == 1

    def body(*refs):
        a_ref, w_ref = refs[0], refs[1]
        r_ref = refs[2] if res is not None else None
        o_ref = refs[3] if res is not None else refs[2]

        def finish(r, cols):
            if res is not None:
                r = r + r_ref[:, cols]
            o_ref[:, cols] = r.astype(out_dtype)

        if nk == 1:
            for b in range(bps):
                finish(_dot(a_ref[...], w_ref[b]), slice(b * tn, (b + 1) * tn))
            return
        acc = refs[-1]
        k = pl.program_id(2)

        @pl.when(k == 0)
        def _():
            acc[...] = jnp.zeros_like(acc)

        acc[...] += _dot(a_ref[...], w_ref[0])

        @pl.when(k == nk - 1)
        def _():
            finish(acc[...], slice(0, tn))

    in_specs = [pl.BlockSpec((tm, tk), lambda j, i, k: (i, k)),
                pl.BlockSpec((bps, tk, tn), lambda j, i, k: (j // npb, k, j % npb))]
    args = [a, w]
    if res is not None:
        in_specs.append(pl.BlockSpec((tm, bps * tn), lambda j, i, k: (i, j)))
        args.append(res)
    body, in_specs, args = _ordered(body, in_specs, args, after)
    return pl.pallas_call(
        body, name=name, grid=(nb * npb // bps, m // tm, nk),
        in_specs=in_specs, out_specs=pl.BlockSpec((tm, bps * tn), lambda j, i, k: (i, j)),
        out_shape=jax.ShapeDtypeStruct((m, nb * ns), out_dtype),
        scratch_shapes=[pltpu.VMEM((tm, tn), F32)] if nk > 1 else [],
        compiler_params=_params(("parallel", "parallel", "arbitrary")),
    )(*args)


def _mm_nt(a, w, name, out_dtype=F32, after=None):
    m, _ = a.shape
    nb, kdim, ns = w.shape
    tm, tko, tn = _pick(m, 1024), _pick(kdim, 1024), _pick(ns, 2048)
    npb = ns // tn
    bps = _blocks_per_step(nb, ns, tn)
    nred = nb * npb // bps

    def body(a_ref, w_ref, o_ref, *scratch):
        total = _dot_nt(a_ref[:, 0:tn], w_ref[0])
        for b in range(1, bps):
            total = total + _dot_nt(a_ref[:, b * tn:(b + 1) * tn], w_ref[b])
        if nred == 1:
            o_ref[...] = total.astype(out_dtype)
            return
        acc = scratch[0]
        n = pl.program_id(2)

        @pl.when(n == 0)
        def _():
            acc[...] = jnp.zeros_like(acc)

        acc[...] += total

        @pl.when(n == nred - 1)
        def _():
            o_ref[...] = acc[...].astype(out_dtype)

    in_specs = [pl.BlockSpec((tm, bps * tn), lambda i, j, n: (i, n)),
                pl.BlockSpec((bps, tko, tn), lambda i, j, n: (n // npb, j, n % npb))]
    body, in_specs, args = _ordered(body, in_specs, [a, w], after)
    return pl.pallas_call(
        body, name=name, grid=(m // tm, kdim // tko, nred),
        in_specs=in_specs,
        out_specs=pl.BlockSpec((tm, tko), lambda i, j, n: (i, j)),
        out_shape=jax.ShapeDtypeStruct((m, kdim), out_dtype),
        scratch_shapes=[pltpu.VMEM((tm, tko), F32)] if nred > 1 else [],
        compiler_params=_params(("parallel", "parallel", "arbitrary")),
    )(*args)


def _mm_tn(a, d, nb, name, out_dtype=BF16, after=None):
    m, kdim = a.shape
    ns = d.shape[1] // nb
    tm, tko, tn = _pick(m, 4096), _pick(kdim, 512), _pick(ns, 1536)
    npb, nm = ns // tn, m // tm

    def body(a_ref, d_ref, o_ref, *scratch):
        if nm == 1:
            o_ref[...] = _dot_tn(a_ref[...], d_ref[...]).astype(out_dtype)
            return
        acc = scratch[0]
        r = pl.program_id(2)

        @pl.when(r == 0)
        def _():
            acc[...] = jnp.zeros_like(acc)

        acc[...] += _dot_tn(a_ref[...], d_ref[...])

        @pl.when(r == nm - 1)
        def _():
            o_ref[...] = acc[...].astype(out_dtype)

    in_specs = [pl.BlockSpec((tm, tko), lambda j, i, r: (r, i)), pl.BlockSpec((tm, tn), lambda j, i, r: (r, j))]
    body, in_specs, args = _ordered(body, in_specs, [a, d], after)
    return pl.pallas_call(
        body, name=name, grid=(nb * npb, kdim // tko, nm),
        in_specs=in_specs,
        out_specs=pl.BlockSpec((None, tko, tn), lambda j, i, r: (j // npb, i, j % npb)),
        out_shape=jax.ShapeDtypeStruct((nb, kdim, ns), out_dtype),
        scratch_shapes=[pltpu.VMEM((tko, tn), F32)] if nm > 1 else [],
        compiler_params=_params(("parallel", "parallel", "arbitrary")),
    )(*args)


def _rms_fwd(x, g, name, after=None):
    t, d = x.shape
    tr = _pick(t, 256, SUBLANE)

    def body(x_ref, g_ref, h_ref):
        xv = x_ref[...]
        r = lax.rsqrt(jnp.mean(xv * xv, axis=-1, keepdims=True) + RMS_EPS)
        h_ref[...] = (xv * r * g_ref[...]).astype(BF16)

    in_specs = [pl.BlockSpec((tr, d), lambda i: (i, 0)), pl.BlockSpec((1, d), lambda i: (0, 0))]
    body, in_specs, args = _ordered(body, in_specs, [x, g], after)
    return pl.pallas_call(
        body, name=name, grid=(t // tr,),
        in_specs=in_specs,
        out_specs=pl.BlockSpec((tr, d), lambda i: (i, 0)),
        out_shape=jax.ShapeDtypeStruct((t, d), BF16),
        compiler_params=_params(("parallel",)),
    )(*args)


def _rms_bwd(x, g, dh, add, name, want_bf16, after=None):
    t, d = x.shape
    tr = _pick(t, 256, SUBLANE)

    def body(x_ref, g_ref, dh_ref, add_ref, *outs):
        if want_bf16:
            dx_ref, dxb_ref, dg_ref = outs
        else:
            dx_ref, dg_ref = outs
        i = pl.program_id(0)

        @pl.when(i == 0)
        def _():
            dg_ref[...] = jnp.zeros_like(dg_ref)

        xv, dhv = x_ref[...], dh_ref[...]
        r = lax.rsqrt(jnp.mean(xv * xv, axis=-1, keepdims=True) + RMS_EPS)
        xh = xv * r
        dg_ref[...] += jnp.sum(dhv * xh, axis=0, keepdims=True)
        dxh = dhv * g_ref[...]
        dx = add_ref[...] + r * (dxh - xh * jnp.mean(dxh * xh, axis=-1, keepdims=True))
        dx_ref[...] = dx
        if want_bf16:
            dxb_ref[...] = dx.astype(BF16)

    row = pl.BlockSpec((tr, d), lambda i: (i, 0))
    vec = pl.BlockSpec((1, d), lambda i: (0, 0))
    out_specs = [row] + ([row] if want_bf16 else []) + [vec]
    out_shape = ([jax.ShapeDtypeStruct((t, d), F32)] + ([jax.ShapeDtypeStruct((t, d), BF16)] if want_bf16 else [])
                 + [jax.ShapeDtypeStruct((1, d), F32)])
    body, in_specs, args = _ordered(body, [row, vec, row, row], [x, g, dh, add], after)
    return pl.pallas_call(
        body, name=name, grid=(t // tr,),
        in_specs=in_specs, out_specs=out_specs, out_shape=out_shape,
        compiler_params=_params(("arbitrary",)),
    )(*args)


def _loss_head(x2, g, target, name="loss_head"):
    t, d = x2.shape
    tr = _pick(t, 256, SUBLANE)

    def body(x_ref, g_ref, t_ref, dx_ref, dxb_ref, dg_ref, loss_ref):
        i = pl.program_id(0)

        @pl.when(i == 0)
        def _():
            dg_ref[...] = jnp.zeros_like(dg_ref)
            loss_ref[...] = jnp.zeros_like(loss_ref)

        xv = x_ref[...]
        gv = g_ref[...]
        r = lax.rsqrt(jnp.mean(xv * xv, axis=-1, keepdims=True) + RMS_EPS)
        xh = xv * r
        err = xh * gv - t_ref[...]
        part = 0.5 * jnp.sum(jnp.mean(err * err, axis=-1, keepdims=True), axis=0, keepdims=True)
        loss_ref[...] += jnp.broadcast_to(part, loss_ref.shape)
        dy = err * (1.0 / d)
        dg_ref[...] += jnp.sum(dy * xh, axis=0, keepdims=True)
        dxh = dy * gv
        dx = r * (dxh - xh * jnp.mean(dxh * xh, axis=-1, keepdims=True))
        dx_ref[...] = dx
        dxb_ref[...] = dx.astype(BF16)

    row = pl.BlockSpec((tr, d), lambda i: (i, 0))
    vec = pl.BlockSpec((1, d), lambda i: (0, 0))
    return pl.pallas_call(
        body, name=name, grid=(t // tr,),
        in_specs=[row, vec, row],
        out_specs=[row, row, vec, pl.BlockSpec((1, LANE), lambda i: (0, 0))],
        out_shape=[jax.ShapeDtypeStruct((t, d), F32), jax.ShapeDtypeStruct((t, d), BF16),
                   jax.ShapeDtypeStruct((1, d), F32), jax.ShapeDtypeStruct((1, LANE), F32)],
        compiler_params=_params(("arbitrary",)),
    )(x2, g, target)


def _s5_discretize(a_re, a_im, ldt):
    lam_re = jnp.minimum(a_re, S5_MAX_RE)
    lam_im = a_im
    dt = jnp.exp(ldt)
    mag = jnp.exp(lam_re * dt)
    abar_re = mag * jnp.cos(lam_im * dt)
    abar_im = mag * jnp.sin(lam_im * dt)
    den = lam_re * lam_re + lam_im * lam_im
    nr = abar_re - 1.0
    ni = abar_im
    coef_re = (nr * lam_re + ni * lam_im) / den
    coef_im = (ni * lam_re - nr * lam_im) / den
    return abar_re, abar_im, coef_re, coef_im


def _s5_param_fwd(a_re, a_im, ldt):
    def body(ar_ref, ai_ref, l_ref, o0, o1, o2, o3):
        outs = _s5_discretize(ar_ref[...], ai_ref[...], l_ref[...])
        for o, v in zip((o0, o1, o2, o3), outs):
            o[...] = v

    sh = jax.ShapeDtypeStruct(a_re.shape, F32)
    return pl.pallas_call(body, name="s5_param_fwd", out_shape=[sh, sh, sh, sh], compiler_params=_params())(a_re, a_im, ldt)


def _s5_param_bwd(a_re, a_im, ldt, cts):
    def body(ar_ref, ai_ref, l_ref, c0, c1, c2, c3, g0, g1, g2):
        _, vjp = jax.vjp(_s5_discretize, ar_ref[...], ai_ref[...], l_ref[...])
        ga, gb, gl = vjp((c0[...], c1[...], c2[...], c3[...]))
        g0[...] = ga
        g1[...] = gb
        g2[...] = gl

    sh = jax.ShapeDtypeStruct(a_re.shape, F32)
    return pl.pallas_call(body, name="s5_param_bwd", out_shape=[sh, sh, jax.ShapeDtypeStruct(ldt.shape, F32)],
                          compiler_params=_params())(a_re, a_im, ldt, *cts)


def _cmul(ar, ai, br, bi):
    return ar * br - ai * bi, ar * bi + ai * br


S5_TC = 128
S5_TILE = S5_SUPER * SUBLANE
S5_HALF = S5_TILE // 2


def _s5_to_tile(re, im):
    f = lambda a: a.reshape(S5_SUPER, S5_LANES // LANE, LANE).transpose(1, 0, 2).reshape(S5_HALF, LANE)
    return jnp.concatenate([f(re), f(im)], axis=0)


def _s5_from_tile(tile):
    f = lambda a: a.reshape(S5_LANES // LANE, S5_SUPER, LANE).transpose(1, 0, 2).reshape(S5_GROUPS, S5_STATE)
    return f(tile[0:S5_HALF]), f(tile[S5_HALF:])


RE = slice(0, S5_HALF)
IM = slice(S5_HALF, S5_TILE)


def _s5_scatter_rows(buf, rows, first_tile=0):
    tc = rows[0].shape[0]
    for j in range(SUBLANE):
        stacked = jnp.stack([r[:, j * LANE:(j + 1) * LANE] for r in rows], axis=0)
        buf[first_tile:first_tile + tc, j * SUBLANE:(j + 1) * SUBLANE, :] = jnp.swapaxes(stacked, 0, 1)


def _s5_gather_rows(buf, tc, first_tile=0):
    per_j = [jnp.swapaxes(buf[first_tile:first_tile + tc, j * SUBLANE:(j + 1) * SUBLANE, :], 0, 1)
             for j in range(SUBLANE)]
    return [jnp.concatenate([per_j[j][k] for j in range(SUBLANE)], axis=1) for k in range(S5_SUPER)]


def _s5_fwd(proj, bsg, ccat, d_row, abar_t, coef_t):
    t = proj.shape[0]
    tc = min(t, S5_TC)
    n_chunks = t // tc

    def body(u_ref, b_ref, c_ref, d_ref, a_ref, cf_ref, y_ref, sb_ref, x, car):
        @pl.when(pl.program_id(0) == 0)
        def _():
            car[...] = jnp.zeros_like(car)

        sb_ref[...] = car[...]
        u = u_ref[...]
        _s5_scatter_rows(x, [_dot(u[:, k * LANE:(k + 1) * LANE].astype(BF16), b_ref[k]) for k in range(S5_SUPER)])
        ar, ai = a_ref[RE, :], a_ref[IM, :]
        cr, ci = cf_ref[RE, :], cf_ref[IM, :]

        def step(i, carry):
            sr, si = carry
            xr, xi = _cmul(cr, ci, x[i, RE, :], x[i, IM, :])
            sr, si = ar * sr - ai * si + xr, ar * si + ai * sr + xi
            x[i, RE, :] = sr
            x[i, IM, :] = si
            return sr, si

        sr, si = lax.fori_loop(0, tc, step, (car[RE, :], car[IM, :]), unroll=4)
        car[RE, :] = sr
        car[IM, :] = si
        for k, s_k in enumerate(_s5_gather_rows(x, tc)):
            cols = slice(k * LANE, (k + 1) * LANE)
            y_ref[:, cols] = _dot(s_k.astype(BF16), c_ref[k]) + d_ref[:, cols] * u[:, cols]

    full = lambda shape: pl.BlockSpec(shape, lambda c: (0,) * len(shape))
    return pl.pallas_call(
        body, name="s5_fwd", grid=(n_chunks,),
        in_specs=[pl.BlockSpec((tc, S5_WIDTH), lambda c: (c, 0)), full(bsg.shape), full(ccat.shape), full(d_row.shape),
                  full(abar_t.shape), full(coef_t.shape)],
        out_specs=[pl.BlockSpec((tc, S5_WIDTH), lambda c: (c, 0)), pl.BlockSpec((None, S5_TILE, LANE), lambda c: (c, 0, 0))],
        out_shape=[jax.ShapeDtypeStruct((t, S5_WIDTH), F32), jax.ShapeDtypeStruct((n_chunks, S5_TILE, LANE), F32)],
        scratch_shapes=[pltpu.VMEM((tc, S5_TILE, LANE), F32), pltpu.VMEM((S5_TILE, LANE), F32)],
        compiler_params=_params(("arbitrary",)),
    )(proj, bsg, ccat, d_row, abar_t, coef_t)


def _s5_bwd(proj, dy, sb, bsg, ccat, d_row, abar_t, coef_t):
    t = proj.shape[0]
    tc = min(t, S5_TC)
    n_chunks = t // tc
    last = n_chunks - 1

    def body(u_ref, dy_ref, sb_ref, b_ref, c_ref, d_ref, a_ref, cf_ref,
             du_ref, gb_ref, gc_ref, gd_ref, ga_ref, gcf_ref, xb, xs, xg, gcar, acc):
        @pl.when(pl.program_id(0) == 0)
        def _():
            gcar[...] = jnp.zeros_like(gcar)
            acc[...] = jnp.zeros_like(acc)
            gb_ref[...] = jnp.zeros_like(gb_ref)
            gc_ref[...] = jnp.zeros_like(gc_ref)
            gd_ref[...] = jnp.zeros_like(gd_ref)

        u = u_ref[...]
        dyv = dy_ref[...]
        u16, dy16 = u.astype(BF16), dyv.astype(BF16)
        subs = [slice(k * LANE, (k + 1) * LANE) for k in range(S5_SUPER)]
        _s5_scatter_rows(xb, [_dot(u16[:, c], b_ref[k]) for k, c in enumerate(subs)])
        _s5_scatter_rows(xg, [_dot_nt(dy16[:, c], c_ref[k]) for k, c in enumerate(subs)])
        ar, ai = a_ref[RE, :], a_ref[IM, :]
        cr, ci = cf_ref[RE, :], cf_ref[IM, :]

        xs[0] = sb_ref[...]

        def fstep(i, carry):
            sr, si = carry
            xr, xi = _cmul(cr, ci, xb[i, RE, :], xb[i, IM, :])
            sr, si = ar * sr - ai * si + xr, ar * si + ai * sr + xi
            xs[i + 1, RE, :] = sr
            xs[i + 1, IM, :] = si
            return sr, si

        lax.fori_loop(0, tc, fstep, (sb_ref[RE, :], sb_ref[IM, :]), unroll=4)

        def rstep(n, carry):
            gr, gi, a0, a1, a2, a3 = carry
            i = tc - 1 - n
            xr = xg[i, RE, :] + ar * gr + ai * gi
            xi = xg[i, IM, :] + ar * gi - ai * gr
            pr, pi = xs[i, RE, :], xs[i, IM, :]
            br, bi = xb[i, RE, :], xb[i, IM, :]
            a0 = a0 + pr * xr + pi * xi
            a1 = a1 + pr * xi - pi * xr
            a2 = a2 + br * xr + bi * xi
            a3 = a3 + br * xi - bi * xr
            xg[i, RE, :] = cr * xr + ci * xi
            xg[i, IM, :] = cr * xi - ci * xr
            return xr, xi, a0, a1, a2, a3

        init = (gcar[RE, :], gcar[IM, :], acc[0], acc[1], acc[2], acc[3])
        gr, gi, a0, a1, a2, a3 = lax.fori_loop(0, tc, rstep, init, unroll=2)
        gcar[RE, :] = gr
        gcar[IM, :] = gi
        for idx, a in enumerate((a0, a1, a2, a3)):
            acc[idx] = a
        ga_ref[RE, :] = a0
        ga_ref[IM, :] = a1
        gcf_ref[RE, :] = a2
        gcf_ref[IM, :] = a3

        g_rows = _s5_gather_rows(xg, tc)
        s_rows = _s5_gather_rows(xs, tc, first_tile=1)
        for k in range(S5_SUPER):
            cols = subs[k]
            g16 = g_rows[k].astype(BF16)
            s16 = s_rows[k].astype(BF16)
            gb_ref[k] += _dot_tn(u16[:, cols], g16)
            gc_ref[k] += _dot_tn(s16, dy16[:, cols])
            du_ref[:, cols] = (_dot_nt(g16, b_ref[k]) + d_ref[:, cols] * dyv[:, cols]).astype(BF16)
        gd_ref[...] += jnp.sum(dyv * u, axis=0, keepdims=True)

    full = lambda shape: pl.BlockSpec(shape, lambda c: (0,) * len(shape))
    rows = pl.BlockSpec((tc, S5_WIDTH), lambda c: (last - c, 0))
    tile = (S5_TILE, LANE)
    return pl.pallas_call(
        body, name="s5_bwd", grid=(n_chunks,),
        in_specs=[rows, rows, pl.BlockSpec((None, S5_TILE, LANE), lambda c: (last - c, 0, 0)),
                  full(bsg.shape), full(ccat.shape), full(d_row.shape), full(abar_t.shape), full(coef_t.shape)],
        out_specs=[rows, full(bsg.shape), full(ccat.shape), full(d_row.shape), full(tile), full(tile)],
        out_shape=[jax.ShapeDtypeStruct((t, S5_WIDTH), BF16), jax.ShapeDtypeStruct(bsg.shape, F32),
                   jax.ShapeDtypeStruct(ccat.shape, F32), jax.ShapeDtypeStruct(d_row.shape, F32),
                   jax.ShapeDtypeStruct(tile, F32), jax.ShapeDtypeStruct(tile, F32)],
        scratch_shapes=[pltpu.VMEM((tc, S5_TILE, LANE), F32), pltpu.VMEM((tc + 1, S5_TILE, LANE), F32),
                        pltpu.VMEM((tc, S5_TILE, LANE), F32), pltpu.VMEM(tile, F32),
                        pltpu.VMEM((4, S5_HALF, LANE), F32)],
        compiler_params=_params(("arbitrary",)),
    )(proj, dy, sb, bsg, ccat, d_row, abar_t, coef_t)


def _gelu_fwd(y, name="s5_gelu"):
    t, w = y.shape
    tr = _pick(t, 512, SUBLANE)

    def body(y_ref, z_ref):
        z_ref[...] = _gelu_and_grad(y_ref[...])[0].astype(BF16)

    row = pl.BlockSpec((tr, w), lambda i: (i, 0))
    return pl.pallas_call(body, name=name, grid=(t // tr,), in_specs=[row], out_specs=row,
                          out_shape=jax.ShapeDtypeStruct((t, w), BF16), compiler_params=_params(("parallel",)))(y)


def _glu_fwd(y, gl, b, name="s5_glu"):
    t, w = y.shape
    tr = _pick(t, 512, SUBLANE)

    def body(y_ref, gl_ref, b_ref, z2_ref):
        z = _gelu_and_grad(y_ref[...])[0]
        z2_ref[...] = (z * _sigmoid(gl_ref[...] + b_ref[...])).astype(BF16)

    row = pl.BlockSpec((tr, w), lambda i: (i, 0))
    return pl.pallas_call(body, name=name, grid=(t // tr,),
                          in_specs=[row, row, pl.BlockSpec((1, w), lambda i: (0, 0))], out_specs=row,
                          out_shape=jax.ShapeDtypeStruct((t, w), BF16), compiler_params=_params(("parallel",)))(y, gl, b)


def _glu_bwd(y, gl, b, dz2, name="s5_glu_bwd", after=None):
    t, w = y.shape
    tr = _pick(t, 512, SUBLANE)

    def body(y_ref, gl_ref, b_ref, dz2_ref, dgl_ref, dza_ref, db_ref):
        @pl.when(pl.program_id(0) == 0)
        def _():
            db_ref[...] = jnp.zeros_like(db_ref)

        z = _gelu_and_grad(y_ref[...])[0]
        s = _sigmoid(gl_ref[...] + b_ref[...])
        dz2v = dz2_ref[...]
        dgl = dz2v * z * s * (1.0 - s)
        dgl_ref[...] = dgl.astype(BF16)
        dza_ref[...] = dz2v * s
        db_ref[...] += jnp.sum(dgl, axis=0, keepdims=True)

    row = pl.BlockSpec((tr, w), lambda i: (i, 0))
    vec = pl.BlockSpec((1, w), lambda i: (0, 0))
    body, in_specs, args = _ordered(body, [row, row, vec, row], [y, gl, b, dz2], after)
    return pl.pallas_call(body, name=name, grid=(t // tr,), in_specs=in_specs, out_specs=[row, row, vec],
                          out_shape=[jax.ShapeDtypeStruct((t, w), BF16), jax.ShapeDtypeStruct((t, w), F32),
                                     jax.ShapeDtypeStruct((1, w), F32)],
                          compiler_params=_params(("arbitrary",)))(*args)


def _gelu_bwd(y, dza, dzb, name="s5_gelu_bwd", after=None):
    t, w = y.shape
    tr = _pick(t, 512, SUBLANE)

    def body(y_ref, a_ref, b_ref, dy_ref):
        dy_ref[...] = (a_ref[...] + b_ref[...]) * _gelu_and_grad(y_ref[...])[1]

    row = pl.BlockSpec((tr, w), lambda i: (i, 0))
    body, in_specs, args = _ordered(body, [row, row, row], [y, dza, dzb], after)
    return pl.pallas_call(body, name=name, grid=(t // tr,), in_specs=in_specs, out_specs=row,
                          out_shape=jax.ShapeDtypeStruct((t, w), F32), compiler_params=_params(("parallel",)))(*args)


def _tri_dot(tri16, x):
    hi = x.astype(BF16)
    r1 = x - hi.astype(F32)
    mid = r1.astype(BF16)
    lo = (r1 - mid.astype(F32)).astype(BF16)
    return _dot(tri16, hi) + _dot(tri16, mid) + _dot(tri16, lo)


def _hgrn_pre(q_in, z, lg):
    lb = _sigmoid(lg[0:1, :] - lg[1:2, :])
    qs, dqs = _silu_and_grad(q_in)
    sz = _sigmoid(z)
    f = lb + (1.0 - lb) * sz
    k = (1.0 - lb) * (1.0 - sz)
    c = HGRN_CHUNK
    r = lax.broadcasted_iota(jnp.int32, (c, c), 0)
    s = lax.broadcasted_iota(jnp.int32, (c, c), 1)
    causal = r >= s
    b = _tri_dot(jnp.where(causal, 1.0, 0.0).astype(BF16), jnp.log(f))
    b_end = b[c - 1:c, :]
    b_mid = b[c // 2 - 1:c // 2, :]
    e_q, e_k, e_0, e_c = jnp.exp(b - b_mid), jnp.exp(b_mid - b), jnp.exp(b), jnp.exp(b_end - b)
    return dict(lb=lb, qs=qs, dqs=dqs, sz=sz, f=f, k=k, causal=causal, b_end=b_end,
                e_q=e_q, e_k=e_k, e_0=e_0, e_c=e_c,
                qt=qs * e_q, kt=k * e_k, q0=qs * e_0, kc=k * e_c)


def _hgrn_fwd(proj, logits, ng):
    t = proj.shape[0]
    c, dh = HGRN_CHUNK, HGRN_DH
    n_chunks = t // c
    subs = HGRN_SUBS if n_chunks % HGRN_SUBS == 0 else 1

    def head(h, sub, q_ref, z_ref, v_ref, g_ref, lg_ref, ng_ref, o_ref, oh_ref, s0_ref, st):
        sl = slice(h * dh, (h + 1) * dh)
        rs = slice(sub * c, (sub + 1) * c)
        s0 = st[h]
        s0_ref[h, sub] = s0
        p = _hgrn_pre(q_ref[rs, sl], z_ref[rs, sl], lg_ref[:, sl])
        v16 = v_ref[rs, sl].astype(BF16)
        a = jnp.where(p["causal"], _dot_nt(p["qt"].astype(BF16), p["kt"].astype(BF16)), 0.0)
        o = _dot_nt(p["q0"].astype(BF16), s0.astype(BF16)) + _dot(a.astype(BF16), v16)
        st[h] = jnp.exp(p["b_end"]) * s0 + _dot_tn(v16, p["kc"].astype(BF16))
        o_ref[rs, sl] = o
        rn = lax.rsqrt(jnp.mean(o * o, axis=-1, keepdims=True) + RMS_EPS)
        oh_ref[rs, sl] = (o * rn * ng_ref[:, sl] * _silu_and_grad(g_ref[rs, sl])[0]).astype(BF16)

    def body(*refs):
        st = refs[-1]

        @pl.when(pl.program_id(0) == 0)
        def _():
            st[...] = jnp.zeros_like(st)

        for sub in range(subs):
            for h in range(HGRN_HEADS):
                head(h, sub, *refs)

    def wide(off):
        return pl.BlockSpec((subs * c, HGRN_WIDTH), lambda i: (i, off))

    return pl.pallas_call(
        body, name="hgrn_fwd", grid=(n_chunks // subs,),
        in_specs=[wide(1), wide(2), wide(3), wide(4),
                  pl.BlockSpec((2, HGRN_WIDTH), lambda i: (0, 0)), pl.BlockSpec((1, HGRN_WIDTH), lambda i: (0, 0))],
        out_specs=[wide(0), wide(0), pl.BlockSpec((HGRN_HEADS, subs, dh, dh), lambda i: (0, i, 0, 0))],
        out_shape=[jax.ShapeDtypeStruct((t, HGRN_WIDTH), F32), jax.ShapeDtypeStruct((t, HGRN_WIDTH), BF16),
                   jax.ShapeDtypeStruct((HGRN_HEADS, n_chunks, dh, dh), F32)],
        scratch_shapes=[pltpu.VMEM((HGRN_HEADS, dh, dh), F32)],
        compiler_params=_params(("arbitrary",)),
    )(proj, proj, proj, proj, logits, ng)


def _hgrn_bwd(proj, o_raw, s0s, doh, logits, ng):
    t = proj.shape[0]
    c, dh = HGRN_CHUNK, HGRN_DH
    n_chunks = t // c
    subs = HGRN_SUBS if n_chunks % HGRN_SUBS == 0 else 1
    last = n_chunks // subs - 1

    def head(h, sub, q_ref, z_ref, v_ref, g_ref, o_ref, s0_ref, doh_ref, lg_ref, ng_ref,
             dq_ref, dz_ref, dv_ref, dg_ref, dng_ref, dlb_ref, dst):
        sl = slice(h * dh, (h + 1) * dh)
        rs = slice(sub * c, (sub + 1) * c)
        p = _hgrn_pre(q_ref[rs, sl], z_ref[rs, sl], lg_ref[:, sl])
        v = v_ref[rs, sl]
        v16 = v.astype(BF16)
        s0 = s0_ref[h, sub]
        ds_end = dst[h]
        ds16 = ds_end.astype(BF16)
        ngv = ng_ref[:, sl]

        o = o_ref[rs, sl]
        dohv = doh_ref[rs, sl]
        sg, dsg = _silu_and_grad(g_ref[rs, sl])
        rn = lax.rsqrt(jnp.mean(o * o, axis=-1, keepdims=True) + RMS_EPS)
        oh = o * rn
        dg_ref[rs, sl] = (dohv * oh * ngv * dsg).astype(BF16)
        don = dohv * sg
        dng_ref[:, sl] += jnp.sum(don * oh, axis=0, keepdims=True)
        doh_n = don * ngv
        do = rn * (doh_n - oh * jnp.mean(doh_n * oh, axis=-1, keepdims=True))
        do16 = do.astype(BF16)

        qt16, kt16, q016, kc16 = (p[n].astype(BF16) for n in ("qt", "kt", "q0", "kc"))
        a = jnp.where(p["causal"], _dot_nt(qt16, kt16), 0.0)
        da = jnp.where(p["causal"], _dot_nt(do16, v16), 0.0)
        da16 = da.astype(BF16)
        dqt = _dot(da16, kt16)
        dq0 = _dot(do16, s0.astype(BF16))
        dkt = _dot_tn(da16, qt16)
        dkc = _dot(v16, ds16)
        dv_ref[rs, sl] = (_dot_tn(a.astype(BF16), do16) + _dot_nt(kc16, ds16)).astype(BF16)
        lam_end = jnp.exp(p["b_end"])
        dst[h] = lam_end * ds_end + _dot_tn(do16, q016)

        qt, kt, q0, kc = (a.astype(F32) for a in (qt16, kt16, q016, kc16))
        db = dqt * qt + dq0 * q0 - dkt * kt - dkc * kc
        db_end = (jnp.sum(dkc * kc, axis=0, keepdims=True)
                  + jnp.sum(ds_end * s0, axis=0, keepdims=True) * lam_end)
        rowi = lax.broadcasted_iota(jnp.int32, (c, dh), 0)
        db = db + jnp.where(rowi == c - 1, db_end, 0.0)
        r = lax.broadcasted_iota(jnp.int32, (c, c), 0)
        s = lax.broadcasted_iota(jnp.int32, (c, c), 1)
        dlf = _tri_dot(jnp.where(s >= r, 1.0, 0.0).astype(BF16), db)

        dqs = dqt * p["e_q"] + dq0 * p["e_0"]
        dq_ref[rs, sl] = (dqs * p["dqs"]).astype(BF16)
        dk = dkt * p["e_k"] + dkc * p["e_c"]
        sz, lb = p["sz"], p["lb"]
        common = dlf / p["f"] - dk
        dz_ref[rs, sl] = ((1.0 - lb) * sz * (1.0 - sz) * common).astype(BF16)
        dlb_ref[:, sl] += jnp.sum((1.0 - sz) * common, axis=0, keepdims=True)

    def body(*refs):
        dng_ref, dlb_ref, dst = refs[-3:]

        @pl.when(pl.program_id(0) == 0)
        def _():
            dst[...] = jnp.zeros_like(dst)
            dng_ref[...] = jnp.zeros_like(dng_ref)
            dlb_ref[...] = jnp.zeros_like(dlb_ref)

        for sub in reversed(range(subs)):
            for h in range(HGRN_HEADS):
                head(h, sub, *refs)

    def wide(off):
        return pl.BlockSpec((subs * c, HGRN_WIDTH), lambda i: (last - i, off))

    vec = pl.BlockSpec((1, HGRN_WIDTH), lambda i: (0, 0))
    act = jax.ShapeDtypeStruct((t, HGRN_WIDTH), BF16)
    vsh = jax.ShapeDtypeStruct((1, HGRN_WIDTH), F32)
    return pl.pallas_call(
        body, name="hgrn_bwd", grid=(n_chunks // subs,),
        in_specs=[wide(1), wide(2), wide(3), wide(4), wide(0),
                  pl.BlockSpec((HGRN_HEADS, subs, dh, dh), lambda i: (0, last - i, 0, 0)),
                  wide(0), pl.BlockSpec((2, HGRN_WIDTH), lambda i: (0, 0)), vec],
        out_specs=[wide(0), wide(0), wide(0), wide(0), vec, vec],
        out_shape=[act, act, act, act, vsh, vsh],
        scratch_shapes=[pltpu.VMEM((HGRN_HEADS, dh, dh), F32)],
        compiler_params=_params(("arbitrary",)),
    )(proj, proj, proj, proj, o_raw, s0s, doh, logits, ng)


def _lb_bwd(logits, dlb):
    def body(lg_ref, d_ref, o_ref):
        lg = lg_ref[...]
        lb = _sigmoid(lg[0:1, :] - lg[1:2, :])
        g = d_ref[...] * lb * (1.0 - lb)
        o_ref[0:1, :] = g
        o_ref[1:2, :] = -g

    return pl.pallas_call(body, name="hgrn_lb_bwd", out_shape=jax.ShapeDtypeStruct(logits.shape, F32),
                          compiler_params=_params())(logits, dlb)


MERGE_TC = 1024
GS_BLOCK = (S5_WIDTH + 4 * HGRN_WIDTH) // MERGE_TC
GH_BLOCK = GS_BLOCK + D_MODEL // MERGE_TC


def _merge_fwd(proj, ys, yh):
    t = proj.shape[0]
    tr = _pick(t, 256, SUBLANE)

    def body(gs_ref, gh_ref, ys_ref, yh_ref, m_ref):
        m_ref[...] = (_sigmoid(gs_ref[...]) * ys_ref[...] + _sigmoid(gh_ref[...]) * yh_ref[...]).astype(BF16)

    blk = pl.BlockSpec((tr, MERGE_TC), lambda i, j: (i, j))
    return pl.pallas_call(
        body, name="merge_fwd", grid=(t // tr, D_MODEL // MERGE_TC),
        in_specs=[pl.BlockSpec((tr, MERGE_TC), lambda i, j: (i, GS_BLOCK + j)),
                  pl.BlockSpec((tr, MERGE_TC), lambda i, j: (i, GH_BLOCK + j)), blk, blk],
        out_specs=blk, out_shape=jax.ShapeDtypeStruct((t, D_MODEL), BF16),
        compiler_params=_params(("parallel", "parallel")),
    )(proj, proj, ys, yh)


def _merge_bwd(proj, ys, yh, dm, after=None):
    t = proj.shape[0]
    tr = _pick(t, 256, SUBLANE)

    def body(gs_ref, gh_ref, ys_ref, yh_ref, dm_ref, dys_ref, dyh_ref, dgs_ref, dgh_ref):
        dmv = dm_ref[...]
        ss, sh = _sigmoid(gs_ref[...]), _sigmoid(gh_ref[...])
        dys_ref[...] = (dmv * ss).astype(BF16)
        dyh_ref[...] = (dmv * sh).astype(BF16)
        dgs_ref[...] = (dmv * ys_ref[...] * ss * (1.0 - ss)).astype(BF16)
        dgh_ref[...] = (dmv * yh_ref[...] * sh * (1.0 - sh)).astype(BF16)

    blk = pl.BlockSpec((tr, MERGE_TC), lambda i, j: (i, j))
    sh16 = jax.ShapeDtypeStruct((t, D_MODEL), BF16)
    in_specs = [pl.BlockSpec((tr, MERGE_TC), lambda i, j: (i, GS_BLOCK + j)),
                pl.BlockSpec((tr, MERGE_TC), lambda i, j: (i, GH_BLOCK + j)), blk, blk, blk]
    body, in_specs, args = _ordered(body, in_specs, [proj, proj, ys, yh, dm], after)
    return pl.pallas_call(
        body, name="merge_bwd", grid=(t // tr, D_MODEL // MERGE_TC),
        in_specs=in_specs,
        out_specs=[blk, blk, blk, blk], out_shape=[sh16, sh16, sh16, sh16],
        compiler_params=_params(("parallel", "parallel")),
    )(*args)


FFN_TC = 128
FFN_ROWS = 128
HALO = SUBLANE


def _pad_rows(dst, src_ref):
    t, c = src_ref.shape
    dst[0:HALO, :] = jnp.zeros((HALO, c), F32)
    dst[HALO:HALO + t, :] = src_ref[...]
    dst[HALO + t:HALO + t + HALO, :] = jnp.zeros((HALO, c), F32)


def _conv3(padded, w, b, r0, nrows):
    x0 = padded[HALO + r0:HALO + r0 + nrows, :]
    x1 = padded[HALO + r0 - 1:HALO + r0 - 1 + nrows, :]
    x2 = padded[HALO + r0 - 2:HALO + r0 - 2 + nrows, :]
    return b + w[0:1, :] * x2 + w[1:2, :] * x1 + w[2:3, :] * x0, (x0, x1, x2)


def _ffn_act_fwd(up, cw, cb):
    t = up.shape[0]
    rows = _pick(t, FFN_ROWS, SUBLANE)
    nvb = D_FF // FFN_TC

    def body(ug_ref, uv_ref, wg_ref, wv_ref, bg_ref, bv_ref, act_ref, pg, pv):
        wg, wv, bg, bv = wg_ref[...], wv_ref[...], bg_ref[...], bv_ref[...]
        _pad_rows(pg, ug_ref)
        _pad_rows(pv, uv_ref)
        for r0 in range(0, t, rows):
            cg, _ = _conv3(pg, wg, bg, r0, rows)
            cv, _ = _conv3(pv, wv, bv, r0, rows)
            act_ref[r0:r0 + rows, :] = (_silu_and_grad(cg)[0] * cv).astype(BF16)

    def colblk(nrow, off):
        return pl.BlockSpec((nrow, FFN_TC), lambda j: (0, off + j))

    return pl.pallas_call(
        body, name="ffn_act_fwd", grid=(nvb,),
        in_specs=[colblk(t, 0), colblk(t, nvb), colblk(3, 0), colblk(3, nvb), colblk(1, 0), colblk(1, nvb)],
        out_specs=colblk(t, 0), out_shape=jax.ShapeDtypeStruct((t, D_FF), BF16),
        scratch_shapes=[pltpu.VMEM((t + 2 * HALO, FFN_TC), F32), pltpu.VMEM((t + 2 * HALO, FFN_TC), F32)],
        compiler_params=_params(("parallel",)),
    )(up, up, cw, cw, cb, cb)


def _ffn_act_bwd(up, dact, cw, cb, after=None):
    t = up.shape[0]
    rows = _pick(t, FFN_ROWS, SUBLANE)
    nvb = D_FF // FFN_TC

    def body(ug_ref, uv_ref, da_ref, wg_ref, wv_ref, bg_ref, bv_ref,
             dug_ref, duv_ref, dwg_ref, dwv_ref, dbg_ref, dbv_ref, pg, pv, dcs):
        wg, wv, bg, bv = wg_ref[...], wv_ref[...], bg_ref[...], bv_ref[...]
        _pad_rows(pg, ug_ref)
        _pad_rows(pv, uv_ref)
        ext = rows + HALO
        acc_g = [jnp.zeros((1, FFN_TC), F32) for _ in range(4)]
        acc_v = [jnp.zeros((1, FFN_TC), F32) for _ in range(4)]
        for r0 in range(0, t, rows):
            cg, xg = _conv3(pg, wg, bg, r0, ext)
            cv, xv = _conv3(pv, wv, bv, r0, ext)
            if r0 + ext <= t:
                dav = da_ref[r0:r0 + ext, :]
            else:
                dav = jnp.concatenate([da_ref[r0:t, :], jnp.zeros((HALO, FFN_TC), F32)], axis=0)
            sg, dsg = _silu_and_grad(cg)
            for h, (dconv, xs, w, acc, out) in enumerate(((dav * cv * dsg, xg, wg, acc_g, dug_ref),
                                                           (dav * sg, xv, wv, acc_v, duv_ref))):
                dcs[h] = dconv
                d0 = dconv[0:rows, :]
                d1 = dcs[h, 1:rows + 1, :]
                d2 = dcs[h, 2:rows + 2, :]
                out[r0:r0 + rows, :] = (w[2:3, :] * d0 + w[1:2, :] * d1 + w[0:1, :] * d2).astype(BF16)
                x0, x1, x2 = xs
                acc[0] = acc[0] + jnp.sum(d0 * x2[0:rows, :], axis=0, keepdims=True)
                acc[1] = acc[1] + jnp.sum(d0 * x1[0:rows, :], axis=0, keepdims=True)
                acc[2] = acc[2] + jnp.sum(d0 * x0[0:rows, :], axis=0, keepdims=True)
                acc[3] = acc[3] + jnp.sum(d0, axis=0, keepdims=True)
        for acc, dw_ref, db_ref in ((acc_g, dwg_ref, dbg_ref), (acc_v, dwv_ref, dbv_ref)):
            dw_ref[0:1, :] = acc[0]
            dw_ref[1:2, :] = acc[1]
            dw_ref[2:3, :] = acc[2]
            db_ref[...] = acc[3]

    def colblk(nrow, off):
        return pl.BlockSpec((nrow, FFN_TC), lambda j: (0, off + j))

    in_specs = [colblk(t, 0), colblk(t, nvb), colblk(t, 0), colblk(3, 0), colblk(3, nvb), colblk(1, 0), colblk(1, nvb)]
    body, in_specs, args = _ordered(body, in_specs, [up, up, dact, cw, cw, cb, cb], after)
    return pl.pallas_call(
        body, name="ffn_act_bwd", grid=(nvb,),
        in_specs=in_specs,
        out_specs=[colblk(t, 0), colblk(t, 0), colblk(3, 0), colblk(3, 0), colblk(1, 0), colblk(1, 0)],
        out_shape=[jax.ShapeDtypeStruct((t, D_FF), BF16), jax.ShapeDtypeStruct((t, D_FF), BF16),
                   jax.ShapeDtypeStruct((3, D_FF), F32), jax.ShapeDtypeStruct((3, D_FF), F32),
                   jax.ShapeDtypeStruct((1, D_FF), F32), jax.ShapeDtypeStruct((1, D_FF), F32)],
        scratch_shapes=[pltpu.VMEM((t + 2 * HALO, FFN_TC), F32), pltpu.VMEM((t + 2 * HALO, FFN_TC), F32),
                        pltpu.VMEM((2, rows + HALO, FFN_TC), F32)],
        compiler_params=_params(("parallel",)),
    )(*args)


def _all_gather(shards, name):
    nw = len(shards)

    def body(*refs):
        x_refs, out_refs = refs[:nw], refs[nw:2 * nw]
        send_sems, recv_sems, local_sems = refs[2 * nw:]
        x, y, c = lax.axis_index("x"), lax.axis_index("y"), lax.axis_index("c")
        me, sibling = (x, y, c), (x, y, 1 - c)
        chips = [(1 - x, y), (x, 1 - y), (1 - x, 1 - y)]

        def copy(w, k, block, to, src=None):
            slot = out_refs[w].at[4 * block[0] + 2 * block[1] + block[2]]
            return pltpu.make_async_remote_copy(
                src_ref=slot if src is None else src, dst_ref=slot,
                send_sem=send_sems.at[w, k], recv_sem=recv_sems.at[w, k],
                device_id=to, device_id_type=MESH)

        mine, first, passed = [], [], []
        for w in range(nw):
            cp = pltpu.make_async_copy(x_refs[w], out_refs[w].at[4 * x + 2 * y + c], local_sems.at[w])
            cp.start()
            mine.append(cp)
            first.append(copy(w, 0, me, sibling, src=x_refs[w]))
            first += [copy(w, 1 + j, me, (*chip, c), src=x_refs[w]) for j, chip in enumerate(chips)]
        for cp in first:
            cp.start()
        for w in range(nw):
            for j, chip in enumerate(chips):
                copy(w, 1 + j, (*chip, c), me).wait_recv()
                fwd = copy(w, 4 + j, (*chip, c), sibling)
                fwd.start()
                passed.append(fwd)
        for w in range(nw):
            copy(w, 0, sibling, me).wait_recv()
            for j, chip in enumerate(chips):
                copy(w, 4 + j, (*chip, 1 - c), me).wait_recv()
        for cp in first + passed:
            cp.wait_send()
        for cp in mine:
            cp.wait()

    anyspec = pl.BlockSpec(memory_space=pl.ANY)
    return pl.pallas_call(
        body, name=name,
        in_specs=[anyspec] * nw, out_specs=[anyspec] * nw,
        out_shape=[jax.ShapeDtypeStruct((N_DEV,) + s.shape, s.dtype) for s in shards],
        scratch_shapes=[pltpu.SemaphoreType.DMA((nw, 7)), pltpu.SemaphoreType.DMA((nw, 7)),
                        pltpu.SemaphoreType.DMA((nw,))],
    )(*shards)


HBM_SPEC = pl.BlockSpec(memory_space=pltpu.HBM)
SEM_SPEC = pl.BlockSpec(memory_space=pltpu.SEMAPHORE)
ANY_SPEC = pl.BlockSpec(memory_space=pl.ANY)
DATAFLOW = pltpu.SideEffectType.DATAFLOW_SIDE_EFFECTING


def _my_index():
    return 4 * lax.axis_index("x") + 2 * lax.axis_index("y") + lax.axis_index("c")


def _peers():
    x, y, c = lax.axis_index("x"), lax.axis_index("y"), lax.axis_index("c")
    peers = []
    for k in range(1, N_DEV):
        px = 1 - x if k & 4 else x
        py = 1 - y if k & 2 else y
        pc = 1 - c if k & 1 else c
        peers.append((k, (px, py, pc), 4 * px + 2 * py + pc))
    return peers


def _split_copy(src_ref, land_ref, send_sems, recv_sems, w, k, peer, slot, scatter, outgoing):
    return pltpu.make_async_remote_copy(
        src_ref=src_ref.at[slot] if scatter else src_ref,
        dst_ref=land_ref.at[_my_index() if outgoing else slot],
        send_sem=send_sems.at[w * (N_DEV - 1) + k - 1], recv_sem=recv_sems.at[w * (N_DEV - 1) + k - 1],
        device_id=peer, device_id_type=MESH)


def _landing_zone(src, scatter, name):
    shape = src.shape if scatter else (N_DEV,) + src.shape

    def body(s_ref, o_ref, sem):
        me = _my_index()
        own = pltpu.make_async_copy(s_ref.at[me] if scatter else s_ref, o_ref.at[me], sem)
        own.start()
        own.wait()

    return pl.pallas_call(
        body, name=name, in_specs=[ANY_SPEC], out_specs=ANY_SPEC,
        out_shape=jax.ShapeDtypeStruct(shape, src.dtype), scratch_shapes=[pltpu.SemaphoreType.DMA(())],
    )(src)


def _exchange_start(srcs, scatter, after, name, lands=None):
    nw = len(srcs)
    if lands is None:
        lands = [_landing_zone(s, scatter, f"{name}_zone{i}") for i, s in enumerate(srcs)]

    afters = [] if after is None else [after]

    def body(*refs):
        s_refs, l_refs = refs[:nw], refs[nw:2 * nw]
        send_sems, recv_sems = refs[2 * nw + len(afters)], refs[2 * nw + len(afters) + 1]
        token = refs[-1]
        for w in range(nw):
            for k, peer, slot in _peers():
                _split_copy(s_refs[w], l_refs[w], send_sems, recv_sems, w, k, peer, slot, scatter, True).start()
        token[...] = jnp.zeros_like(token)

    sems = pltpu.SemaphoreType.DMA((nw * (N_DEV - 1),))
    outs = pl.pallas_call(
        body, name=name,
        out_shape=(sems, sems, *[pltpu.HBM(a.shape, a.dtype) for a in (*srcs, *lands)],
                   jax.ShapeDtypeStruct((SUBLANE, LANE), F32)),
        in_specs=[HBM_SPEC] * (2 * nw) + [ANY_SPEC] * len(afters),
        out_specs=(SEM_SPEC, SEM_SPEC, *[HBM_SPEC] * (2 * nw), pl.BlockSpec(memory_space=pltpu.VMEM)),
        input_output_aliases={i: 2 + i for i in range(2 * nw)},
        compiler_params=pltpu.CompilerParams(has_side_effects=DATAFLOW),
    )(*[pltpu.with_memory_space_constraint(a, pltpu.HBM) for a in (*srcs, *lands)], *afters)
    return dict(sems=outs[:2], srcs=outs[2:2 + nw], lands=outs[2 + nw:2 + 2 * nw], token=outs[-1], scatter=scatter)


def _exchange_wait(handle, afters, name):
    srcs, lands, scatter = handle["srcs"], handle["lands"], handle["scatter"]
    nw = len(srcs)

    def body(*refs):
        s_refs, l_refs = refs[:nw], refs[nw:2 * nw]
        send_sems, recv_sems = refs[2 * nw], refs[2 * nw + 1]
        for w in range(nw):
            for k, peer, slot in _peers():
                cp = _split_copy(s_refs[w], l_refs[w], send_sems, recv_sems, w, k, peer, slot, scatter, False)
                cp.wait_send()
                cp.wait_recv()

    outs = pl.pallas_call(
        body, name=name,
        out_shape=tuple(pltpu.HBM(a.shape, a.dtype) for a in (*srcs, *lands)),
        in_specs=[HBM_SPEC] * (2 * nw) + [SEM_SPEC, SEM_SPEC] + [ANY_SPEC] * len(afters),
        out_specs=tuple([HBM_SPEC] * (2 * nw)),
        input_output_aliases={i: i for i in range(2 * nw)},
        compiler_params=pltpu.CompilerParams(has_side_effects=DATAFLOW),
    )(*srcs, *lands, *handle["sems"], *afters)
    return list(outs[nw:])


def _chips_and_sibling():
    x, y, c = lax.axis_index("x"), lax.axis_index("y"), lax.axis_index("c")
    return [(1 - x, y), (x, 1 - y), (1 - x, 1 - y)], (x, y, 1 - c), c


def _slot(px, py, pc):
    return 4 * px + 2 * py + pc


def _two_level_start(shards, name):
    nw = len(shards)
    lands = [_landing_zone(s, False, f"{name}_zone{i}") for i, s in enumerate(shards)]

    def body(*refs):
        s_refs, l_refs = refs[:nw], refs[nw:2 * nw]
        send_sems, recv_sems, token = refs[2 * nw], refs[2 * nw + 1], refs[-1]
        chips, sibling, c = _chips_and_sibling()
        for w in range(nw):
            for k, to in enumerate([sibling] + [(*chip, c) for chip in chips]):
                pltpu.make_async_remote_copy(
                    src_ref=s_refs[w], dst_ref=l_refs[w].at[_my_index()],
                    send_sem=send_sems.at[4 * w + k], recv_sem=recv_sems.at[4 * w + k],
                    device_id=to, device_id_type=MESH).start()
        token[...] = jnp.zeros_like(token)

    sems = pltpu.SemaphoreType.DMA((4 * nw,))
    outs = pl.pallas_call(
        body, name=name,
        out_shape=(sems, sems, *[pltpu.HBM(a.shape, a.dtype) for a in (*shards, *lands)],
                   jax.ShapeDtypeStruct((SUBLANE, LANE), F32)),
        in_specs=[HBM_SPEC] * (2 * nw),
        out_specs=(SEM_SPEC, SEM_SPEC, *[HBM_SPEC] * (2 * nw), pl.BlockSpec(memory_space=pltpu.VMEM)),
        input_output_aliases={i: 2 + i for i in range(2 * nw)},
        compiler_params=pltpu.CompilerParams(has_side_effects=DATAFLOW),
    )(*[pltpu.with_memory_space_constraint(a, pltpu.HBM) for a in (*shards, *lands)])
    return dict(sems=outs[:2], srcs=outs[2:2 + nw], lands=outs[2 + nw:2 + 2 * nw], token=outs[-1])


def _two_level_pass(handle, afters, name):
    srcs, lands = handle["srcs"], handle["lands"]
    nw = len(srcs)

    def body(*refs):
        s_refs, l_refs = refs[:nw], refs[nw:2 * nw]
        send_a, recv_a = refs[2 * nw], refs[2 * nw + 1]
        send_b, recv_b = refs[2 * nw + 2 + len(afters)], refs[2 * nw + 3 + len(afters)]
        chips, sibling, c = _chips_and_sibling()
        for w in range(nw):
            for j, chip in enumerate(chips):
                landed = l_refs[w].at[_slot(*chip, c)]
                pltpu.make_async_remote_copy(
                    src_ref=s_refs[w], dst_ref=landed, send_sem=send_a.at[4 * w + 1 + j], recv_sem=recv_a.at[4 * w + 1 + j],
                    device_id=(*chip, c), device_id_type=MESH).wait_recv()
                pltpu.make_async_remote_copy(
                    src_ref=landed, dst_ref=landed, send_sem=send_b.at[3 * w + j], recv_sem=recv_b.at[3 * w + j],
                    device_id=sibling, device_id_type=MESH).start()

    sems = pltpu.SemaphoreType.DMA((3 * nw,))
    outs = pl.pallas_call(
        body, name=name,
        out_shape=(sems, sems, *[pltpu.HBM(a.shape, a.dtype) for a in (*srcs, *lands)]),
        in_specs=[HBM_SPEC] * (2 * nw) + [SEM_SPEC, SEM_SPEC] + [ANY_SPEC] * len(afters),
        out_specs=(SEM_SPEC, SEM_SPEC, *[HBM_SPEC] * (2 * nw)),
        input_output_aliases={i: 2 + i for i in range(2 * nw)},
        compiler_params=pltpu.CompilerParams(has_side_effects=DATAFLOW),
    )(*srcs, *lands, *handle["sems"], *afters)
    return dict(sems=handle["sems"], sems_pass=outs[:2], srcs=outs[2:2 + nw], lands=outs[2 + nw:2 + 2 * nw])


def _two_level_wait(handle, name):
    srcs, lands = handle["srcs"], handle["lands"]
    nw = len(srcs)

    def body(*refs):
        s_refs, l_refs = refs[:nw], refs[nw:2 * nw]
        send_a, recv_a, send_b, recv_b = refs[2 * nw:2 * nw + 4]
        chips, sibling, c = _chips_and_sibling()
        x, y = sibling[0], sibling[1]
        for w in range(nw):
            first = pltpu.make_async_remote_copy(
                src_ref=s_refs[w], dst_ref=l_refs[w].at[_slot(x, y, 1 - c)], send_sem=send_a.at[4 * w],
                recv_sem=recv_a.at[4 * w], device_id=sibling, device_id_type=MESH)
            first.wait_send()
            first.wait_recv()
            for j, chip in enumerate(chips):
                pltpu.make_async_remote_copy(
                    src_ref=s_refs[w], dst_ref=l_refs[w].at[_slot(*chip, c)], send_sem=send_a.at[4 * w + 1 + j],
                    recv_sem=recv_a.at[4 * w + 1 + j], device_id=(*chip, c), device_id_type=MESH).wait_send()
                passed = pltpu.make_async_remote_copy(
                    src_ref=l_refs[w].at[_slot(*chip, c)], dst_ref=l_refs[w].at[_slot(*chip, 1 - c)],
                    send_sem=send_b.at[3 * w + j], recv_sem=recv_b.at[3 * w + j], device_id=sibling, device_id_type=MESH)
                passed.wait_send()
                passed.wait_recv()

    outs = pl.pallas_call(
        body, name=name,
        out_shape=tuple(pltpu.HBM(a.shape, a.dtype) for a in (*srcs, *lands)),
        in_specs=[HBM_SPEC] * (2 * nw) + [SEM_SPEC] * 4,
        out_specs=tuple([HBM_SPEC] * (2 * nw)),
        input_output_aliases={i: i for i in range(2 * nw)},
        compiler_params=pltpu.CompilerParams(has_side_effects=DATAFLOW),
    )(*srcs, *lands, *handle["sems"], *handle["sems_pass"])
    return list(outs[nw:])


def _adamw(w, g, m, v):
    m = ADAM_B1 * m + (1.0 - ADAM_B1) * g
    v = ADAM_B2 * v + (1.0 - ADAM_B2) * (g * g)
    m_hat = m / (1.0 - ADAM_B1 ** ADAM_STEP)
    v_hat = v / (1.0 - ADAM_B2 ** ADAM_STEP)
    delta = -ADAM_LR * (m_hat / (jnp.sqrt(v_hat) + ADAM_EPS) + ADAM_WD * w)
    return delta, m, v


def _sum_adam(parts, w, m, v, name):
    _, r, c = parts.shape
    tr = _pick(r, 128, 16)

    def body(p_ref, w_ref, m_ref, v_ref, g_ref, d_ref, mo_ref, vo_ref):
        g = p_ref[0].astype(F32)
        for s in range(1, N_DEV):
            g = g + p_ref[s].astype(F32)
        g_ref[...] = g
        d_ref[...], mo_ref[...], vo_ref[...] = _adamw(w_ref[...], g, m_ref[...], v_ref[...])

    row = pl.BlockSpec((tr, c), lambda i: (i, 0))
    sh = jax.ShapeDtypeStruct((r, c), F32)
    return pl.pallas_call(
        body, name=name, grid=(r // tr,),
        in_specs=[pl.BlockSpec((N_DEV, tr, c), lambda i: (0, i, 0)), row, row, row],
        out_specs=[row, row, row, row], out_shape=[sh, sh, sh, sh],
        compiler_params=_params(("parallel",)),
    )(parts, w, m, v)


def _sum_slots(parts, name):
    _, r, c = parts.shape
    tr = _pick(r, 512, SUBLANE)

    def body(p_ref, o_ref):
        g = p_ref[0]
        for s in range(1, N_DEV):
            g = g + p_ref[s]
        o_ref[...] = g

    return pl.pallas_call(
        body, name=name, grid=(r // tr,),
        in_specs=[pl.BlockSpec((N_DEV, tr, c), lambda i: (0, i, 0))],
        out_specs=pl.BlockSpec((tr, c), lambda i: (i, 0)), out_shape=jax.ShapeDtypeStruct((r, c), F32),
        compiler_params=_params(("parallel",)),
    )(parts)


def _adam_rows(g, w, m, v, name):
    r, c = g.shape
    tr = _pick(r, 512, SUBLANE)

    def body(g_ref, w_ref, m_ref, v_ref, d_ref, mo_ref, vo_ref):
        d_ref[...], mo_ref[...], vo_ref[...] = _adamw(w_ref[...], g_ref[...], m_ref[...], v_ref[...])

    row = pl.BlockSpec((tr, c), lambda i: (i, 0))
    sh = jax.ShapeDtypeStruct((r, c), F32)
    return pl.pallas_call(body, name=name, grid=(r // tr,), in_specs=[row] * 4, out_specs=[row] * 3,
                          out_shape=[sh, sh, sh], compiler_params=_params(("parallel",)))(g, w, m, v)


def _pack(arrays):
    flat = jnp.concatenate([a.reshape(-1).astype(F32) for a in arrays])
    pad = (-flat.shape[0]) % (SUBLANE * LANE)
    return jnp.pad(flat, (0, pad)).reshape(-1, LANE)


def _unpack(packed, shapes):
    flat = packed.reshape(-1)
    out, off = [], 0
    for s in shapes:
        n = math.prod(s)
        out.append(flat[off:off + n].reshape(s))
        off += n
    return out


def _block_diag(t):
    eye = jnp.eye(S5_SUPER, dtype=bool)
    bd = jnp.where(eye[None, :, None, :, None], t[:, :, :, None, :], 0.0)
    return bd.reshape(S5_SUPER, S5_SUPER * t.shape[2], S5_SUPER * t.shape[3])


def _diag_blocks(dense, a, b):
    x = dense.reshape(S5_SUPER, S5_SUPER, a, S5_SUPER, b)
    return jnp.moveaxis(jnp.diagonal(x, axis1=1, axis2=3), -1, 1)


def _s5_layouts(b_re, b_im, c_re, c_im, d):
    g2 = (S5_GROUPS // S5_SUPER, S5_SUPER)
    bt = lambda b: _block_diag(b.reshape(*g2, S5_STATE, S5_GROUP).transpose(0, 1, 3, 2))
    ct = lambda c: _block_diag(c.reshape(*g2, S5_GROUP, S5_STATE).transpose(0, 1, 3, 2))
    bsg = jnp.concatenate([bt(b_re), bt(b_im)], axis=2).astype(BF16)
    ccat = jnp.concatenate([ct(c_re), -ct(c_im)], axis=1).astype(BF16)
    return bsg, ccat, d.reshape(1, S5_WIDTH)


def _s5_param_grads(gb, gc):
    n = S5_LANES
    gb_re = _diag_blocks(gb[:, :, 0:n], S5_GROUP, S5_STATE).transpose(0, 1, 3, 2).reshape(S5_GROUPS, S5_STATE, S5_GROUP)
    gb_im = _diag_blocks(gb[:, :, n:2 * n], S5_GROUP, S5_STATE).transpose(0, 1, 3, 2).reshape(S5_GROUPS, S5_STATE, S5_GROUP)
    gc_re = _diag_blocks(gc[:, 0:n, :], S5_STATE, S5_GROUP).transpose(0, 1, 3, 2).reshape(S5_GROUPS, S5_GROUP, S5_STATE)
    gc_im = -_diag_blocks(gc[:, n:2 * n, :], S5_STATE, S5_GROUP).transpose(0, 1, 3, 2).reshape(S5_GROUPS, S5_GROUP, S5_STATE)
    return gb_re, gb_im, gc_re, gc_im


def _local_step(x, target, weight, emit, small, after=None):
    sp = small
    a_re, a_im = sp["s5_a_re"], sp["s5_a_im"]
    ldt = sp["s5_log_dt"].reshape(S5_GROUPS, 1)

    h1 = _rms_fwd(x, sp["ln_mix_g"], "rms_mix", after=after)
    w_in = weight("w_in", h1)
    proj = _mm_nn(h1, w_in, "mm_in", after=weight("after_w_in", None))
    conv_w = weight("conv_w", None)
    disc = _s5_param_fwd(a_re, a_im, ldt)
    bsg, ccat, d_row = sp["s5_layouts"]
    abar_t, coef_t = _s5_to_tile(disc[0], disc[1]), _s5_to_tile(disc[2], disc[3])
    y, sb = _s5_fwd(proj, bsg, ccat, d_row, abar_t, coef_t)
    z16 = _gelu_fwd(y)
    w_glu = weight("s5_w_glu", z16)
    gl = _mm_nn(z16, w_glu, "mm_glu")
    z2 = _glu_fwd(y, gl, sp["s5_b_glu"])
    w_ps = weight("w_proj_s5", z2)
    ys = _mm_nn(z2, w_ps, "mm_proj_s5")
    o_raw, oh, s0s = _hgrn_fwd(proj, sp["hgrn_lb_logits"], sp["hgrn_norm_g"])
    w_ph = weight("w_proj_hgrn", oh)
    yh = _mm_nn(oh, w_ph, "mm_proj_hgrn")
    merged = _merge_fwd(proj, ys, yh)
    w_out = weight("w_out", merged)
    x1 = _mm_nn(merged, w_out, "mm_out", res=x)
    h2 = _rms_fwd(x1, sp["ln_ffn_g"], "rms_ffn")
    w_up = weight("w_up", h2)
    up = _mm_nn(h2, w_up, "mm_up")
    act = _ffn_act_fwd(up, conv_w, sp["conv_b"])
    w_down = weight("w_down", act)
    x2 = _mm_nn(act, w_down, "mm_down", res=x1)
    dx2, dx2_16, g_ln_final, loss = _loss_head(x2, sp["ln_final_g"], target)

    dact = _mm_nt(dx2_16, w_down, "mm_down_dx")
    tok = emit("w_down", _mm_tn(act, dx2_16, 1, "mm_down_dw"))
    dup_g, dup_v, dcw_g, dcw_v, dcb_g, dcb_v = _ffn_act_bwd(up, dact, conv_w, sp["conv_b"], after=tok)
    dup = jnp.concatenate([dup_g, dup_v], axis=1)
    g_conv_w = jnp.concatenate([dcw_g, dcw_v], axis=1)
    g_conv_b = jnp.concatenate([dcb_g, dcb_v], axis=1)
    dh2 = _mm_nt(dup, w_up, "mm_up_dx")
    tok = emit("w_up", _mm_tn(h2, dup, N_DEV, "mm_up_dw"))
    dx1, dx1_16, g_ln_ffn = _rms_bwd(x1, sp["ln_ffn_g"], dh2, dx2, "rms_ffn_bwd", True, after=tok)

    dmerged = _mm_nt(dx1_16, w_out, "mm_out_dx")
    tok = emit("w_out", _mm_tn(merged, dx1_16, 1, "mm_out_dw"))
    dys, dyh, dgs, dgh = _merge_bwd(proj, ys, yh, dmerged, after=tok)
    doh = _mm_nt(dyh, w_ph, "mm_proj_hgrn_dx")
    tok = emit("w_proj_hgrn", _mm_tn(oh, dyh, N_DEV, "mm_proj_hgrn_dw"))
    dz2 = _mm_nt(dys, w_ps, "mm_proj_s5_dx", after=tok)
    tok = emit("w_proj_s5", _mm_tn(z2, dys, N_DEV, "mm_proj_s5_dw"))
    dgl, dza, g_b_glu = _glu_bwd(y, gl, sp["s5_b_glu"], dz2, after=tok)
    dzb = _mm_nt(dgl, w_glu, "mm_glu_dx")
    tok = emit("s5_w_glu", _mm_tn(z16, dgl, 1, "mm_glu_dw"))
    dy = _gelu_bwd(y, dza, dzb, after=tok)
    du, gb, gc, gd, g_abar_t, g_coef_t = _s5_bwd(proj, dy, sb, bsg, ccat, d_row, abar_t, coef_t)
    g_a_re, g_a_im, g_ldt = _s5_param_bwd(a_re, a_im, ldt, [*_s5_from_tile(g_abar_t), *_s5_from_tile(g_coef_t)])
    g_b_re, g_b_im, g_c_re, g_c_im = _s5_param_grads(gb, gc)
    dq, dz, dv, dg, g_norm, dlb = _hgrn_bwd(proj, o_raw, s0s, doh, sp["hgrn_lb_logits"], sp["hgrn_norm_g"])
    g_logits = _lb_bwd(sp["hgrn_lb_logits"], dlb)

    small_g = dict(s5_a_re=g_a_re, s5_a_im=g_a_im, s5_log_dt=g_ldt.reshape(1, S5_GROUPS),
                   s5_b_re=g_b_re, s5_b_im=g_b_im, s5_c_re=g_c_re, s5_c_im=g_c_im,
                   s5_d=gd.reshape(S5_GROUPS, S5_GROUP), s5_b_glu=g_b_glu, hgrn_lb_logits=g_logits,
                   hgrn_norm_g=g_norm, ln_ffn_g=g_ln_ffn, conv_w=g_conv_w, conv_b=g_conv_b, ln_final_g=g_ln_final,
                   loss=loss[0, 0:1])
    tok_small = emit("small", small_g)

    dproj = jnp.concatenate([du, dq, dz, dv, dg, dgs, dgh], axis=1)
    tok = emit("w_in", _mm_tn(h1, dproj, N_DEV, "mm_in_dw", after=tok_small))
    dh1 = _mm_nt(dproj, w_in, "mm_in_dx")
    grad_x, g_ln_mix = _rms_bwd(x, sp["ln_mix_g"], dh1, dx1, "rms_mix_bwd", False, after=tok)
    return grad_x, g_ln_mix


BIG = ("w_in", "s5_w_glu", "w_proj_s5", "w_proj_hgrn", "w_out", "w_up", "w_down")
COL_SHARDED = ("w_in", "w_proj_s5", "w_proj_hgrn", "w_up")
SMALL = ("ln_mix_g", "s5_a_re", "s5_a_im", "s5_log_dt", "s5_b_re", "s5_b_im", "s5_c_re", "s5_c_im", "s5_d",
         "s5_b_glu", "hgrn_lb_logits", "hgrn_norm_g", "ln_ffn_g", "conv_b", "ln_final_g")
WEIGHTS = ("ln_mix_g", "w_in", "s5_a_re", "s5_a_im", "s5_log_dt", "s5_b_re", "s5_b_im", "s5_c_re", "s5_c_im", "s5_d",
           "s5_w_glu", "s5_b_glu", "w_proj_s5", "hgrn_lb_logits", "hgrn_norm_g", "w_proj_hgrn", "w_out", "ln_ffn_g",
           "w_up", "conv_w", "conv_b", "w_down", "ln_final_g")


def kernel(x, ln_mix_g, w_in, s5_a_re, s5_a_im, s5_log_dt, s5_b_re, s5_b_im, s5_c_re, s5_c_im, s5_d, s5_w_glu, s5_b_glu, w_proj_s5, hgrn_lb_logits, hgrn_norm_g, w_proj_hgrn, w_out, ln_ffn_g, w_up, conv_w, conv_b, w_down, ln_final_g, loss_target, m_ln_mix_g, m_w_in, m_s5_a_re, m_s5_a_im, m_s5_log_dt, m_s5_b_re, m_s5_b_im, m_s5_c_re, m_s5_c_im, m_s5_d, m_s5_w_glu, m_s5_b_glu, m_w_proj_s5, m_hgrn_lb_logits, m_hgrn_norm_g, m_w_proj_hgrn, m_w_out, m_ln_ffn_g, m_w_up, m_conv_w, m_conv_b, m_w_down, m_ln_final_g, v_ln_mix_g, v_w_in, v_s5_a_re, v_s5_a_im, v_s5_log_dt, v_s5_b_re, v_s5_b_im, v_s5_c_re, v_s5_c_im, v_s5_d, v_s5_w_glu, v_s5_b_glu, v_w_proj_s5, v_hgrn_lb_logits, v_hgrn_norm_g, v_w_proj_hgrn, v_w_out, v_ln_ffn_g, v_w_up, v_conv_w, v_conv_b, v_w_down, v_ln_final_g):
    given = dict(locals())
    w = {n: given[n] for n in WEIGHTS}
    mom = {n: given["m_" + n] for n in WEIGHTS}
    var = {n: given["v_" + n] for n in WEIGHTS}

    first = _two_level_start([w_in[0].astype(BF16), conv_w[0]], "gather_first_start")
    zero = first["token"][0, 0]
    packed_small = SMALL[1:]
    pw, pm, pv = (_pack([d[n] for n in packed_small]) + zero for d in (w, mom, var))
    layouts = _s5_layouts(s5_b_re[0] + zero, s5_b_im[0], s5_c_re[0] + zero, s5_c_im[0], s5_d[0])
    gather_groups = (("s5_w_glu", "w_proj_s5", "w_proj_hgrn", "w_out"), ("w_up",), ("w_down",))
    shard16 = {n: w[n][0].astype(BF16) + zero.astype(BF16) for g in gather_groups for n in g}
    zones = {n: _landing_zone(s, False, "zone_" + n) for n, s in shard16.items()}
    pending, ready = {}, {}

    def weight(name, after):
        if "w_in" not in ready:
            local_work = [after, pw, pm, pv, layouts[0], layouts[1], *zones.values()]
            passed = _two_level_pass(first, local_work, "gather_first_pass")
            ready["w_in"], conv_w_all = _two_level_wait(passed, "gather_first_wait")
            ready["conv_w"] = conv_w_all.transpose(1, 0, 2).reshape(3, 2 * D_FF)
            token = ready["w_in"]
            for i, group in enumerate(gather_groups):
                handle = _exchange_start([shard16[n] for n in group], False, token, f"gather_start_{i}",
                                         lands=[zones[n] for n in group])
                token = handle["token"]
                for n in group:
                    pending[n] = (group, handle, f"gather_wait_{i}")
            ready["after_w_in"] = token
        if name not in ready:
            group, handle, wait_name = pending[name]
            for n, g in zip(group, _exchange_wait(handle, [after], wait_name)):
                ready[n] = g
        g = ready[name]
        return g if name not in BIG or name in COL_SHARDED else g.reshape(1, N_DEV * g.shape[1], g.shape[2])

    scatter_groups = (("w_down",), ("w_up",), ("w_out", "w_proj_hgrn", "w_proj_s5", "s5_w_glu"), ("w_in",))
    emitted, scatters = {}, []
    packed_names = SMALL[1:] + ("conv_w", "loss")

    def emit(name, grad):
        if name == "small":
            emitted[name] = ([grad[n].shape for n in packed_names],
                             _exchange_start([_pack([grad[n] for n in packed_names])], False, None, "small_start"))
            return emitted[name][1]["token"]
        emitted[name] = grad if name in COL_SHARDED else grad.reshape(N_DEV, -1, grad.shape[2])
        group = scatter_groups[len(scatters)]
        if not all(n in emitted for n in group):
            return None
        handle = _exchange_start([emitted[n] for n in group], True, None, f"scatter_start_{len(scatters)}")
        scatters.append((group, handle))
        return handle["token"]

    small = dict(ln_mix_g=ln_mix_g, s5_a_re=s5_a_re[0], s5_a_im=s5_a_im[0], s5_log_dt=s5_log_dt, s5_layouts=layouts,
                 s5_b_glu=s5_b_glu, hgrn_lb_logits=hgrn_lb_logits, hgrn_norm_g=hgrn_norm_g, ln_ffn_g=ln_ffn_g,
                 conv_b=conv_b, ln_final_g=ln_final_g.reshape(1, D_MODEL))
    grad_x, g_ln_mix = _local_step(x[0], loss_target[0], weight, emit, small, after=first["token"])

    shapes, handle = emitted["small"]
    total = _sum_slots(_exchange_wait(handle, [grad_x], "small_wait")[0], "sum_small")
    summed = dict(zip(packed_names, _unpack(total, shapes)))
    mix_all = _all_gather([g_ln_mix.reshape(-1, LANE)], "gather_ln_mix")[0]
    summed["ln_mix_g"] = _sum_slots(mix_all, "sum_ln_mix").reshape(1, D_MODEL)

    grads, delta, new_m, new_v = {}, {}, {}, {}
    afters = [grad_x, total]
    for i, (group, handle) in enumerate(scatters):
        for n, r in zip(group, _exchange_wait(handle, afters, f"scatter_wait_{i}")):
            g, d, m2, v2 = _sum_adam(r, w[n][0], mom[n][0], var[n][0], "adam_" + n)
            grads[n], delta[n], new_m[n], new_v[n] = g[None], d[None], m2[None], v2[None]
        if i == len(scatters) - 2:
            afters = [delta[n] for g2, _ in scatters[:-1] for n in g2]

    d_s, m_s, v_s = _adam_rows(_pack([summed[n] for n in packed_small]), pw, pm, pv, "adam_small")
    wshapes = [w[n].shape for n in packed_small]
    for n, d, m2, v2 in zip(packed_small, _unpack(d_s, wshapes), _unpack(m_s, wshapes), _unpack(v_s, wshapes)):
        grads[n], delta[n], new_m[n], new_v[n] = summed[n].reshape(w[n].shape), d, m2, v2
    grads["ln_mix_g"] = summed["ln_mix_g"]
    delta["ln_mix_g"], new_m["ln_mix_g"], new_v["ln_mix_g"] = _adam_rows(summed["ln_mix_g"], ln_mix_g, m_ln_mix_g,
                                                                         v_ln_mix_g, "adam_ln_mix")
    me = 4 * lax.axis_index("x") + 2 * lax.axis_index("y") + lax.axis_index("c")
    ncol = conv_w.shape[2]
    g_cw = lax.dynamic_slice_in_dim(summed["conv_w"], me * ncol, ncol, axis=1)
    d_cw, m_cw, v_cw = _adam_rows(g_cw, conv_w[0], m_conv_w[0], v_conv_w[0], "adam_conv_w")
    grads["conv_w"], delta["conv_w"], new_m["conv_w"], new_v["conv_w"] = g_cw[None], d_cw[None], m_cw[None], v_cw[None]

    return (summed["loss"].reshape(()), grad_x[None], *[grads[n] for n in WEIGHTS], *[delta[n] for n in WEIGHTS],
            *[new_m[n] for n in WEIGHTS], *[new_v[n] for n in WEIGHTS])
```

```python
import math

import jax
import jax.numpy as jnp
from jax import lax
from jax.experimental import pallas as pl
from jax.experimental.pallas import tpu as pltpu

F32 = jnp.float32
BF16 = jnp.bfloat16

N_DEV = 8
D_MODEL = 2048
S5_WIDTH = 1024
S5_GROUP = 16
S5_GROUPS = 64
S5_STATE = 64
S5_MAX_RE = -1e-4
S5_SUPER = 8
S5_LANES = S5_SUPER * S5_STATE
HGRN_WIDTH = 1024
HGRN_HEADS = 8
HGRN_DH = 128
HGRN_CHUNK = 64
HGRN_SUBS = 4
D_FF = 5632
RMS_EPS = 1e-6
ADAM_LR = 0.001
ADAM_B1 = 0.9
ADAM_B2 = 0.999
ADAM_EPS = 1e-08
ADAM_WD = 0.01
ADAM_STEP = 10

LANE = 128
SUBLANE = 8
VMEM_LIMIT = 48 * 1024 * 1024
MESH = pl.DeviceIdType.MESH
GELU_C = math.sqrt(2.0 / math.pi)
GELU_A = 0.044715


def _params(sem=None):
    return pltpu.CompilerParams(dimension_semantics=sem, vmem_limit_bytes=VMEM_LIMIT)


def _pick(n, cap, unit=LANE):
    best = None
    for t in range(unit, min(n, cap) + 1, unit):
        if n % t == 0:
            best = t
    return best if best is not None else n


def _ordered(body, in_specs, args, after):
    if after is None:
        return body, list(in_specs), list(args)
    n_in = len(args)

    def ordered_body(*refs):
        return body(*refs[:n_in], *refs[n_in + 1:])

    return ordered_body, [*in_specs, pl.BlockSpec(memory_space=pl.ANY)], [*args, after]


def _sigmoid(x):
    return 0.5 * jnp.tanh(0.5 * x) + 0.5


def _silu_and_grad(x):
    s = _sigmoid(x)
    return x * s, s * (1.0 + x * (1.0 - s))


def _gelu_and_grad(y):
    inner = GELU_C * (y + GELU_A * y * y * y)
    th = jnp.tanh(inner)
    val = 0.5 * y * (1.0 + th)
    grad = 0.5 * (1.0 + th) + 0.5 * y * (1.0 - th * th) * GELU_C * (1.0 + 3.0 * GELU_A * y * y)
    return val, grad


def _dot(a, b):
    return jnp.dot(a, b, preferred_element_type=F32)


def _dot_nt(a, b):
    return lax.dot_general(a, b, (((1,), (1,)), ((), ())), preferred_element_type=F32)


def _dot_tn(a, b):
    return lax.dot_general(a, b, (((0,), (0,)), ((), ())), preferred_element_type=F32)


def _blocks_per_step(nb, ns, tn, cap=2048):
    if tn != ns:
        return 1
    best = 1
    for b in range(1, nb + 1):
        if nb % b == 0 and b * ns <= cap:
            best = b
    return best


def _mm_nn(a, w, name, res=None, out_dtype=F32, after=None):
    m, kdim = a.shape
    nb, _, ns = w.shape
    tm, tk, tn = _pick(m, 512), _pick(kdim, 2048), _pick(ns, 1536)
    npb, nk = ns // tn, kdim // tk
    bps = _blocks_per_step(nb, ns, tn)
    assert bps == 1 or nk == 1

    def body(*refs):
        a_ref, w_ref = refs[0], refs[1]
        r_ref = refs[2] if res is not None else None
        o_ref = refs[3] if res is not None else refs[2]

        def finish(r, cols):
            if res is not None:
                r = r + r_ref[:, cols]
            o_ref[:, cols] = r.astype(out_dtype)

        if nk == 1:
            for b in range(bps):
                finish(_dot(a_ref[...], w_ref[b]), slice(b * tn, (b + 1) * tn))
            return
        acc = refs[-1]
        k = pl.program_id(2)

        @pl.when(k == 0)
        def _():
            acc[...] = jnp.zeros_like(acc)

        acc[...] += _dot(a_ref[...], w_ref[0])

        @pl.when(k == nk - 1)
        def _():
            finish(acc[...], slice(0, tn))

    in_specs = [pl.BlockSpec((tm, tk), lambda j, i, k: (i, k)),
                pl.BlockSpec((bps, tk, tn), lambda j, i, k: (j // npb, k, j % npb))]
    args = [a, w]
    if res is not None:
        in_specs.append(pl.BlockSpec((tm, bps * tn), lambda j, i, k: (i, j)))
        args.append(res)
    body, in_specs, args = _ordered(body, in_specs, args, after)
    return pl.pallas_call(
        body, name=name, grid=(nb * npb // bps, m // tm, nk),
        in_specs=in_specs, out_specs=pl.BlockSpec((tm, bps * tn), lambda j, i, k: (i, j)),
        out_shape=jax.ShapeDtypeStruct((m, nb * ns), out_dtype),
        scratch_shapes=[pltpu.VMEM((tm, tn), F32)] if nk > 1 else [],
        compiler_params=_params(("parallel", "parallel", "arbitrary")),
    )(*args)


def _mm_nt(a, w, name, out_dtype=F32, after=None):
    m, _ = a.shape
    nb, kdim, ns = w.shape
    tm, tko, tn = _pick(m, 1024), _pick(kdim, 1024), _pick(ns, 2048)
    npb = ns // tn
    bps = _blocks_per_step(nb, ns, tn)
    nred = nb * npb // bps

    def body(a_ref, w_ref, o_ref, *scratch):
        total = _dot_nt(a_ref[:, 0:tn], w_ref[0])
        for b in range(1, bps):
            total = total + _dot_nt(a_ref[:, b * tn:(b + 1) * tn], w_ref[b])
        if nred == 1:
            o_ref[...] = total.astype(out_dtype)
            return
        acc = scratch[0]
        n = pl.program_id(2)

        @pl.when(n == 0)
        def _():
            acc[...] = jnp.zeros_like(acc)

        acc[...] += total

        @pl.when(n == nred - 1)
        def _():
            o_ref[...] = acc[...].astype(out_dtype)

    in_specs = [pl.BlockSpec((tm, bps * tn), lambda i, j, n: (i, n)),
                pl.BlockSpec((bps, tko, tn), lambda i, j, n: (n // npb, j, n % npb))]
    body, in_specs, args = _ordered(body, in_specs, [a, w], after)
    return pl.pallas_call(
        body, name=name, grid=(m // tm, kdim // tko, nred),
        in_specs=in_specs,
        out_specs=pl.BlockSpec((tm, tko), lambda i, j, n: (i, j)),
        out_shape=jax.ShapeDtypeStruct((m, kdim), out_dtype),
        scratch_shapes=[pltpu.VMEM((tm, tko), F32)] if nred > 1 else [],
        compiler_params=_params(("parallel", "parallel", "arbitrary")),
    )(*args)


def _mm_tn(a, d, nb, name, out_dtype=BF16, after=None):
    m, kdim = a.shape
    ns = d.shape[1] // nb
    tm, tko, tn = _pick(m, 4096), _pick(kdim, 512), _pick(ns, 1536)
    npb, nm = ns // tn, m // tm

    def body(a_ref, d_ref, o_ref, *scratch):
        if nm == 1:
            o_ref[...] = _dot_tn(a_ref[...], d_ref[...]).astype(out_dtype)
            return
        acc = scratch[0]
        r = pl.program_id(2)

        @pl.when(r == 0)
        def _():
            acc[...] = jnp.zeros_like(acc)

        acc[...] += _dot_tn(a_ref[...], d_ref[...])

        @pl.when(r == nm - 1)
        def _():
            o_ref[...] = acc[...].astype(out_dtype)

    in_specs = [pl.BlockSpec((tm, tko), lambda j, i, r: (r, i)), pl.BlockSpec((tm, tn), lambda j, i, r: (r, j))]
    body, in_specs, args = _ordered(body, in_specs, [a, d], after)
    return pl.pallas_call(
        body, name=name, grid=(nb * npb, kdim // tko, nm),
        in_specs=in_specs,
        out_specs=pl.BlockSpec((None, tko, tn), lambda j, i, r: (j // npb, i, j % npb)),
        out_shape=jax.ShapeDtypeStruct((nb, kdim, ns), out_dtype),
        scratch_shapes=[pltpu.VMEM((tko, tn), F32)] if nm > 1 else [],
        compiler_params=_params(("parallel", "parallel", "arbitrary")),
    )(*args)


def _rms_fwd(x, g, name, after=None):
    t, d = x.shape
    tr = _pick(t, 256, SUBLANE)

    def body(x_ref, g_ref, h_ref):
        xv = x_ref[...]
        r = lax.rsqrt(jnp.mean(xv * xv, axis=-1, keepdims=True) + RMS_EPS)
        h_ref[...] = (xv * r * g_ref[...]).astype(BF16)

    in_specs = [pl.BlockSpec((tr, d), lambda i: (i, 0)), pl.BlockSpec((1, d), lambda i: (0, 0))]
    body, in_specs, args = _ordered(body, in_specs, [x, g], after)
    return pl.pallas_call(
        body, name=name, grid=(t // tr,),
        in_specs=in_specs,
        out_specs=pl.BlockSpec((tr, d), lambda i: (i, 0)),
        out_shape=jax.ShapeDtypeStruct((t, d), BF16),
        compiler_params=_params(("parallel",)),
    )(*args)


def _rms_bwd(x, g, dh, add, name, want_bf16, after=None):
    t, d = x.shape
    tr = _pick(t, 256, SUBLANE)

    def body(x_ref, g_ref, dh_ref, add_ref, *outs):
        if want_bf16:
            dx_ref, dxb_ref, dg_ref = outs
        else:
            dx_ref, dg_ref = outs
        i = pl.program_id(0)

        @pl.when(i == 0)
        def _():
            dg_ref[...] = jnp.zeros_like(dg_ref)

        xv, dhv = x_ref[...], dh_ref[...]
        r = lax.rsqrt(jnp.mean(xv * xv, axis=-1, keepdims=True) + RMS_EPS)
        xh = xv * r
        dg_ref[...] += jnp.sum(dhv * xh, axis=0, keepdims=True)
        dxh = dhv * g_ref[...]
        dx = add_ref[...] + r * (dxh - xh * jnp.mean(dxh * xh, axis=-1, keepdims=True))
        dx_ref[...] = dx
        if want_bf16:
            dxb_ref[...] = dx.astype(BF16)

    row = pl.BlockSpec((tr, d), lambda i: (i, 0))
    vec = pl.BlockSpec((1, d), lambda i: (0, 0))
    out_specs = [row] + ([row] if want_bf16 else []) + [vec]
    out_shape = ([jax.ShapeDtypeStruct((t, d), F32)] + ([jax.ShapeDtypeStruct((t, d), BF16)] if want_bf16 else [])
                 + [jax.ShapeDtypeStruct((1, d), F32)])
    body, in_specs, args = _ordered(body, [row, vec, row, row], [x, g, dh, add], after)
    return pl.pallas_call(
        body, name=name, grid=(t // tr,),
        in_specs=in_specs, out_specs=out_specs, out_shape=out_shape,
        compiler_params=_params(("arbitrary",)),
    )(*args)


def _loss_head(x2, g, target, name="loss_head"):
    t, d = x2.shape
    tr = _pick(t, 256, SUBLANE)

    def body(x_ref, g_ref, t_ref, dx_ref, dxb_ref, dg_ref, loss_ref):
        i = pl.program_id(0)

        @pl.when(i == 0)
        def _():
            dg_ref[...] = jnp.zeros_like(dg_ref)
            loss_ref[...] = jnp.zeros_like(loss_ref)

        xv = x_ref[...]
        gv = g_ref[...]
        r = lax.rsqrt(jnp.mean(xv * xv, axis=-1, keepdims=True) + RMS_EPS)
        xh = xv * r
        err = xh * gv - t_ref[...]
        part = 0.5 * jnp.sum(jnp.mean(err * err, axis=-1, keepdims=True), axis=0, keepdims=True)
        loss_ref[...] += jnp.broadcast_to(part, loss_ref.shape)
        dy = err * (1.0 / d)
        dg_ref[...] += jnp.sum(dy * xh, axis=0, keepdims=True)
        dxh = dy * gv
        dx = r * (dxh - xh * jnp.mean(dxh * xh, axis=-1, keepdims=True))
        dx_ref[...] = dx
        dxb_ref[...] = dx.astype(BF16)

    row = pl.BlockSpec((tr, d), lambda i: (i, 0))
    vec = pl.BlockSpec((1, d), lambda i: (0, 0))
    return pl.pallas_call(
        body, name=name, grid=(t // tr,),
        in_specs=[row, vec, row],
        out_specs=[row, row, vec, pl.BlockSpec((1, LANE), lambda i: (0, 0))],
        out_shape=[jax.ShapeDtypeStruct((t, d), F32), jax.ShapeDtypeStruct((t, d), BF16),
                   jax.ShapeDtypeStruct((1, d), F32), jax.ShapeDtypeStruct((1, LANE), F32)],
        compiler_params=_params(("arbitrary",)),
    )(x2, g, target)


def _s5_discretize(a_re, a_im, ldt):
    lam_re = jnp.minimum(a_re, S5_MAX_RE)
    lam_im = a_im
    dt = jnp.exp(ldt)
    mag = jnp.exp(lam_re * dt)
    abar_re = mag * jnp.cos(lam_im * dt)
    abar_im = mag * jnp.sin(lam_im * dt)
    den = lam_re * lam_re + lam_im * lam_im
    nr = abar_re - 1.0
    ni = abar_im
    coef_re = (nr * lam_re + ni * lam_im) / den
    coef_im = (ni * lam_re - nr * lam_im) / den
    return abar_re, abar_im, coef_re, coef_im


def _s5_param_fwd(a_re, a_im, ldt):
    def body(ar_ref, ai_ref, l_ref, o0, o1, o2, o3):
        outs = _s5_discretize(ar_ref[...], ai_ref[...], l_ref[...])
        for o, v in zip((o0, o1, o2, o3), outs):
            o[...] = v

    sh = jax.ShapeDtypeStruct(a_re.shape, F32)
    return pl.pallas_call(body, name="s5_param_fwd", out_shape=[sh, sh, sh, sh], compiler_params=_params())(a_re, a_im, ldt)


def _s5_param_bwd(a_re, a_im, ldt, cts):
    def body(ar_ref, ai_ref, l_ref, c0, c1, c2, c3, g0, g1, g2):
        _, vjp = jax.vjp(_s5_discretize, ar_ref[...], ai_ref[...], l_ref[...])
        ga, gb, gl = vjp((c0[...], c1[...], c2[...], c3[...]))
        g0[...] = ga
        g1[...] = gb
        g2[...] = gl

    sh = jax.ShapeDtypeStruct(a_re.shape, F32)
    return pl.pallas_call(body, name="s5_param_bwd", out_shape=[sh, sh, jax.ShapeDtypeStruct(ldt.shape, F32)],
                          compiler_params=_params())(a_re, a_im, ldt, *cts)


def _cmul(ar, ai, br, bi):
    return ar * br - ai * bi, ar * bi + ai * br


S5_TC = 128
S5_TILE = S5_SUPER * SUBLANE
S5_HALF = S5_TILE // 2


def _s5_to_tile(re, im):
    f = lambda a: a.reshape(S5_SUPER, S5_LANES // LANE, LANE).transpose(1, 0, 2).reshape(S5_HALF, LANE)
    return jnp.concatenate([f(re), f(im)], axis=0)


def _s5_from_tile(tile):
    f = lambda a: a.reshape(S5_LANES // LANE, S5_SUPER, LANE).transpose(1, 0, 2).reshape(S5_GROUPS, S5_STATE)
    return f(tile[0:S5_HALF]), f(tile[S5_HALF:])


RE = slice(0, S5_HALF)
IM = slice(S5_HALF, S5_TILE)


def _s5_scatter_rows(buf, rows, first_tile=0):
    tc = rows[0].shape[0]
    for j in range(SUBLANE):
        stacked = jnp.stack([r[:, j * LANE:(j + 1) * LANE] for r in rows], axis=0)
        buf[first_tile:first_tile + tc, j * SUBLANE:(j + 1) * SUBLANE, :] = jnp.swapaxes(stacked, 0, 1)


def _s5_gather_rows(buf, tc, first_tile=0):
    per_j = [jnp.swapaxes(buf[first_tile:first_tile + tc, j * SUBLANE:(j + 1) * SUBLANE, :], 0, 1)
             for j in range(SUBLANE)]
    return [jnp.concatenate([per_j[j][k] for j in range(SUBLANE)], axis=1) for k in range(S5_SUPER)]


def _s5_fwd(proj, bsg, ccat, d_row, abar_t, coef_t):
    t = proj.shape[0]
    tc = min(t, S5_TC)
    n_chunks = t // tc

    def body(u_ref, b_ref, c_ref, d_ref, a_ref, cf_ref, y_ref, sb_ref, x, car):
        @pl.when(pl.program_id(0) == 0)
        def _():
            car[...] = jnp.zeros_like(car)

        sb_ref[...] = car[...]
        u = u_ref[...]
        _s5_scatter_rows(x, [_dot(u[:, k * LANE:(k + 1) * LANE].astype(BF16), b_ref[k]) for k in range(S5_SUPER)])
        ar, ai = a_ref[RE, :], a_ref[IM, :]
        cr, ci = cf_ref[RE, :], cf_ref[IM, :]

        def step(i, carry):
            sr, si = carry
            xr, xi = _cmul(cr, ci, x[i, RE, :], x[i, IM, :])
            sr, si = ar * sr - ai * si + xr, ar * si + ai * sr + xi
            x[i, RE, :] = sr
            x[i, IM, :] = si
            return sr, si

        sr, si = lax.fori_loop(0, tc, step, (car[RE, :], car[IM, :]), unroll=4)
        car[RE, :] = sr
        car[IM, :] = si
        for k, s_k in enumerate(_s5_gather_rows(x, tc)):
            cols = slice(k * LANE, (k + 1) * LANE)
            y_ref[:, cols] = _dot(s_k.astype(BF16), c_ref[k]) + d_ref[:, cols] * u[:, cols]

    full = lambda shape: pl.BlockSpec(shape, lambda c: (0,) * len(shape))
    return pl.pallas_call(
        body, name="s5_fwd", grid=(n_chunks,),
        in_specs=[pl.BlockSpec((tc, S5_WIDTH), lambda c: (c, 0)), full(bsg.shape), full(ccat.shape), full(d_row.shape),
                  full(abar_t.shape), full(coef_t.shape)],
        out_specs=[pl.BlockSpec((tc, S5_WIDTH), lambda c: (c, 0)), pl.BlockSpec((None, S5_TILE, LANE), lambda c: (c, 0, 0))],
        out_shape=[jax.ShapeDtypeStruct((t, S5_WIDTH), F32), jax.ShapeDtypeStruct((n_chunks, S5_TILE, LANE), F32)],
        scratch_shapes=[pltpu.VMEM((tc, S5_TILE, LANE), F32), pltpu.VMEM((S5_TILE, LANE), F32)],
        compiler_params=_params(("arbitrary",)),
    )(proj, bsg, ccat, d_row, abar_t, coef_t)


def _s5_bwd(proj, dy, sb, bsg, ccat, d_row, abar_t, coef_t):
    t = proj.shape[0]
    tc = min(t, S5_TC)
    n_chunks = t // tc
    last = n_chunks - 1

    def body(u_ref, dy_ref, sb_ref, b_ref, c_ref, d_ref, a_ref, cf_ref,
             du_ref, gb_ref, gc_ref, gd_ref, ga_ref, gcf_ref, xb, xs, xg, gcar, acc):
        @pl.when(pl.program_id(0) == 0)
        def _():
            gcar[...] = jnp.zeros_like(gcar)
            acc[...] = jnp.zeros_like(acc)
            gb_ref[...] = jnp.zeros_like(gb_ref)
            gc_ref[...] = jnp.zeros_like(gc_ref)
            gd_ref[...] = jnp.zeros_like(gd_ref)

        u = u_ref[...]
        dyv = dy_ref[...]
        u16, dy16 = u.astype(BF16), dyv.astype(BF16)
        subs = [slice(k * LANE, (k + 1) * LANE) for k in range(S5_SUPER)]
        _s5_scatter_rows(xb, [_dot(u16[:, c], b_ref[k]) for k, c in enumerate(subs)])
        _s5_scatter_rows(xg, [_dot_nt(dy16[:, c], c_ref[k]) for k, c in enumerate(subs)])
        ar, ai = a_ref[RE, :], a_ref[IM, :]
        cr, ci = cf_ref[RE, :], cf_ref[IM, :]

        xs[0] = sb_ref[...]

        def fstep(i, carry):
            sr, si = carry
            xr, xi = _cmul(cr, ci, xb[i, RE, :], xb[i, IM, :])
            sr, si = ar * sr - ai * si + xr, ar * si + ai * sr + xi
            xs[i + 1, RE, :] = sr
            xs[i + 1, IM, :] = si
            return sr, si

        lax.fori_loop(0, tc, fstep, (sb_ref[RE, :], sb_ref[IM, :]), unroll=4)

        def rstep(n, carry):
            gr, gi, a0, a1, a2, a3 = carry
            i = tc - 1 - n
            xr = xg[i, RE, :] + ar * gr + ai * gi
            xi = xg[i, IM, :] + ar * gi - ai * gr
            pr, pi = xs[i, RE, :], xs[i, IM, :]
            br, bi = xb[i, RE, :], xb[i, IM, :]
            a0 = a0 + pr * xr + pi * xi
            a1 = a1 + pr * xi - pi * xr
            a2 = a2 + br * xr + bi * xi
            a3 = a3 + br * xi - bi * xr
            xg[i, RE, :] = cr * xr + ci * xi
            xg[i, IM, :] = cr * xi - ci * xr
            return xr, xi, a0, a1, a2, a3

        init = (gcar[RE, :], gcar[IM, :], acc[0], acc[1], acc[2], acc[3])
        gr, gi, a0, a1, a2, a3 = lax.fori_loop(0, tc, rstep, init, unroll=2)
        gcar[RE, :] = gr
        gcar[IM, :] = gi
        for idx, a in enumerate((a0, a1, a2, a3)):
            acc[idx] = a
        ga_ref[RE, :] = a0
        ga_ref[IM, :] = a1
        gcf_ref[RE, :] = a2
        gcf_ref[IM, :] = a3

        g_rows = _s5_gather_rows(xg, tc)
        s_rows = _s5_gather_rows(xs, tc, first_tile=1)
        for k in range(S5_SUPER):
            cols = subs[k]
            g16 = g_rows[k].astype(BF16)
            s16 = s_rows[k].astype(BF16)
            gb_ref[k] += _dot_tn(u16[:, cols], g16)
            gc_ref[k] += _dot_tn(s16, dy16[:, cols])
            du_ref[:, cols] = (_dot_nt(g16, b_ref[k]) + d_ref[:, cols] * dyv[:, cols]).astype(BF16)
        gd_ref[...] += jnp.sum(dyv * u, axis=0, keepdims=True)

    full = lambda shape: pl.BlockSpec(shape, lambda c: (0,) * len(shape))
    rows = pl.BlockSpec((tc, S5_WIDTH), lambda c: (last - c, 0))
    tile = (S5_TILE, LANE)
    return pl.pallas_call(
        body, name="s5_bwd", grid=(n_chunks,),
        in_specs=[rows, rows, pl.BlockSpec((None, S5_TILE, LANE), lambda c: (last - c, 0, 0)),
                  full(bsg.shape), full(ccat.shape), full(d_row.shape), full(abar_t.shape), full(coef_t.shape)],
        out_specs=[rows, full(bsg.shape), full(ccat.shape), full(d_row.shape), full(tile), full(tile)],
        out_shape=[jax.ShapeDtypeStruct((t, S5_WIDTH), BF16), jax.ShapeDtypeStruct(bsg.shape, F32),
                   jax.ShapeDtypeStruct(ccat.shape, F32), jax.ShapeDtypeStruct(d_row.shape, F32),
                   jax.ShapeDtypeStruct(tile, F32), jax.ShapeDtypeStruct(tile, F32)],
        scratch_shapes=[pltpu.VMEM((tc, S5_TILE, LANE), F32), pltpu.VMEM((tc + 1, S5_TILE, LANE), F32),
                        pltpu.VMEM((tc, S5_TILE, LANE), F32), pltpu.VMEM(tile, F32),
                        pltpu.VMEM((4, S5_HALF, LANE), F32)],
        compiler_params=_params(("arbitrary",)),
    )(proj, dy, sb, bsg, ccat, d_row, abar_t, coef_t)


def _gelu_fwd(y, name="s5_gelu"):
    t, w = y.shape
    tr = _pick(t, 512, SUBLANE)

    def body(y_ref, z_ref):
        z_ref[...] = _gelu_and_grad(y_ref[...])[0].astype(BF16)

    row = pl.BlockSpec((tr, w), lambda i: (i, 0))
    return pl.pallas_call(body, name=name, grid=(t // tr,), in_specs=[row], out_specs=row,
                          out_shape=jax.ShapeDtypeStruct((t, w), BF16), compiler_params=_params(("parallel",)))(y)


def _glu_fwd(y, gl, b, name="s5_glu"):
    t, w = y.shape
    tr = _pick(t, 512, SUBLANE)

    def body(y_ref, gl_ref, b_ref, z2_ref):
        z = _gelu_and_grad(y_ref[...])[0]
        z2_ref[...] = (z * _sigmoid(gl_ref[...] + b_ref[...])).astype(BF16)

    row = pl.BlockSpec((tr, w), lambda i: (i, 0))
    return pl.pallas_call(body, name=name, grid=(t // tr,),
                          in_specs=[row, row, pl.BlockSpec((1, w), lambda i: (0, 0))], out_specs=row,
                          out_shape=jax.ShapeDtypeStruct((t, w), BF16), compiler_params=_params(("parallel",)))(y, gl, b)


def _glu_bwd(y, gl, b, dz2, name="s5_glu_bwd", after=None):
    t, w = y.shape
    tr = _pick(t, 512, SUBLANE)

    def body(y_ref, gl_ref, b_ref, dz2_ref, dgl_ref, dza_ref, db_ref):
        @pl.when(pl.program_id(0) == 0)
        def _():
            db_ref[...] = jnp.zeros_like(db_ref)

        z = _gelu_and_grad(y_ref[...])[0]
        s = _sigmoid(gl_ref[...] + b_ref[...])
        dz2v = dz2_ref[...]
        dgl = dz2v * z * s * (1.0 - s)
        dgl_ref[...] = dgl.astype(BF16)
        dza_ref[...] = dz2v * s
        db_ref[...] += jnp.sum(dgl, axis=0, keepdims=True)

    row = pl.BlockSpec((tr, w), lambda i: (i, 0))
    vec = pl.BlockSpec((1, w), lambda i: (0, 0))
    body, in_specs, args = _ordered(body, [row, row, vec, row], [y, gl, b, dz2], after)
    return pl.pallas_call(body, name=name, grid=(t // tr,), in_specs=in_specs, out_specs=[row, row, vec],
                          out_shape=[jax.ShapeDtypeStruct((t, w), BF16), jax.ShapeDtypeStruct((t, w), F32),
                                     jax.ShapeDtypeStruct((1, w), F32)],
                          compiler_params=_params(("arbitrary",)))(*args)


def _gelu_bwd(y, dza, dzb, name="s5_gelu_bwd", after=None):
    t, w = y.shape
    tr = _pick(t, 512, SUBLANE)

    def body(y_ref, a_ref, b_ref, dy_ref):
        dy_ref[...] = (a_ref[...] + b_ref[...]) * _gelu_and_grad(y_ref[...])[1]

    row = pl.BlockSpec((tr, w), lambda i: (i, 0))
    body, in_specs, args = _ordered(body, [row, row, row], [y, dza, dzb], after)
    return pl.pallas_call(body, name=name, grid=(t // tr,), in_specs=in_specs, out_specs=row,
                          out_shape=jax.ShapeDtypeStruct((t, w), F32), compiler_params=_params(("parallel",)))(*args)


def _tri_dot(tri16, x):
    hi = x.astype(BF16)
    lo = (x - hi.astype(F32)).astype(BF16)
    return _dot(tri16, hi) + _dot(tri16, lo)


def _hgrn_pre(q_in, z, lg):
    lb = _sigmoid(lg[0:1, :] - lg[1:2, :])
    qs, dqs = _silu_and_grad(q_in)
    sz = _sigmoid(z)
    f = lb + (1.0 - lb) * sz
    k = (1.0 - lb) * (1.0 - sz)
    c = HGRN_CHUNK
    r = lax.broadcasted_iota(jnp.int32, (c, c), 0)
    s = lax.broadcasted_iota(jnp.int32, (c, c), 1)
    causal = r >= s
    b = _tri_dot(jnp.where(causal, 1.0, 0.0).astype(BF16), jnp.log(f))
    b_end = b[c - 1:c, :]
    b_mid = b[c // 2 - 1:c // 2, :]
    e_q, e_k, e_0, e_c = jnp.exp(b - b_mid), jnp.exp(b_mid - b), jnp.exp(b), jnp.exp(b_end - b)
    return dict(lb=lb, qs=qs, dqs=dqs, sz=sz, f=f, k=k, causal=causal, b_end=b_end,
                e_q=e_q, e_k=e_k, e_0=e_0, e_c=e_c,
                qt=qs * e_q, kt=k * e_k, q0=qs * e_0, kc=k * e_c)


def _hgrn_fwd(proj, logits, ng):
    t = proj.shape[0]
    c, dh = HGRN_CHUNK, HGRN_DH
    n_chunks = t // c
    subs = HGRN_SUBS if n_chunks % HGRN_SUBS == 0 else 1

    def head(h, sub, q_ref, z_ref, v_ref, g_ref, lg_ref, ng_ref, o_ref, oh_ref, s0_ref, st):
        sl = slice(h * dh, (h + 1) * dh)
        rs = slice(sub * c, (sub + 1) * c)
        s0 = st[h]
        s0_ref[h, sub] = s0
        p = _hgrn_pre(q_ref[rs, sl], z_ref[rs, sl], lg_ref[:, sl])
        v16 = v_ref[rs, sl].astype(BF16)
        a = jnp.where(p["causal"], _dot_nt(p["qt"].astype(BF16), p["kt"].astype(BF16)), 0.0)
        o = _dot_nt(p["q0"].astype(BF16), s0.astype(BF16)) + _dot(a.astype(BF16), v16)
        st[h] = jnp.exp(p["b_end"]) * s0 + _dot_tn(v16, p["kc"].astype(BF16))
        o_ref[rs, sl] = o
        rn = lax.rsqrt(jnp.mean(o * o, axis=-1, keepdims=True) + RMS_EPS)
        oh_ref[rs, sl] = (o * rn * ng_ref[:, sl] * _silu_and_grad(g_ref[rs, sl])[0]).astype(BF16)

    def body(*refs):
        st = refs[-1]

        @pl.when(pl.program_id(0) == 0)
        def _():
            st[...] = jnp.zeros_like(st)

        for sub in range(subs):
            for h in range(HGRN_HEADS):
                head(h, sub, *refs)

    def wide(off):
        return pl.BlockSpec((subs * c, HGRN_WIDTH), lambda i: (i, off))

    return pl.pallas_call(
        body, name="hgrn_fwd", grid=(n_chunks // subs,),
        in_specs=[wide(1), wide(2), wide(3), wide(4),
                  pl.BlockSpec((2, HGRN_WIDTH), lambda i: (0, 0)), pl.BlockSpec((1, HGRN_WIDTH), lambda i: (0, 0))],
        out_specs=[wide(0), wide(0), pl.BlockSpec((HGRN_HEADS, subs, dh, dh), lambda i: (0, i, 0, 0))],
        out_shape=[jax.ShapeDtypeStruct((t, HGRN_WIDTH), F32), jax.ShapeDtypeStruct((t, HGRN_WIDTH), BF16),
                   jax.ShapeDtypeStruct((HGRN_HEADS, n_chunks, dh, dh), F32)],
        scratch_shapes=[pltpu.VMEM((HGRN_HEADS, dh, dh), F32)],
        compiler_params=_params(("arbitrary",)),
    )(proj, proj, proj, proj, logits, ng)


def _hgrn_bwd(proj, o_raw, s0s, doh, logits, ng):
    t = proj.shape[0]
    c, dh = HGRN_CHUNK, HGRN_DH
    n_chunks = t // c
    subs = HGRN_SUBS if n_chunks % HGRN_SUBS == 0 else 1
    last = n_chunks // subs - 1

    def head(h, sub, q_ref, z_ref, v_ref, g_ref, o_ref, s0_ref, doh_ref, lg_ref, ng_ref,
             dq_ref, dz_ref, dv_ref, dg_ref, dng_ref, dlb_ref, dst):
        sl = slice(h * dh, (h + 1) * dh)
        rs = slice(sub * c, (sub + 1) * c)
        p = _hgrn_pre(q_ref[rs, sl], z_ref[rs, sl], lg_ref[:, sl])
        v = v_ref[rs, sl]
        v16 = v.astype(BF16)
        s0 = s0_ref[h, sub]
        ds_end = dst[h]
        ds16 = ds_end.astype(BF16)
        ngv = ng_ref[:, sl]

        o = o_ref[rs, sl]
        dohv = doh_ref[rs, sl]
        sg, dsg = _silu_and_grad(g_ref[rs, sl])
        rn = lax.rsqrt(jnp.mean(o * o, axis=-1, keepdims=True) + RMS_EPS)
        oh = o * rn
        dg_ref[rs, sl] = (dohv * oh * ngv * dsg).astype(BF16)
        don = dohv * sg
        dng_ref[:, sl] += jnp.sum(don * oh, axis=0, keepdims=True)
        doh_n = don * ngv
        do = rn * (doh_n - oh * jnp.mean(doh_n * oh, axis=-1, keepdims=True))
        do16 = do.astype(BF16)

        qt16, kt16, q016, kc16 = (p[n].astype(BF16) for n in ("qt", "kt", "q0", "kc"))
        a = jnp.where(p["causal"], _dot_nt(qt16, kt16), 0.0)
        da = jnp.where(p["causal"], _dot_nt(do16, v16), 0.0)
        da16 = da.astype(BF16)
        dqt = _dot(da16, kt16)
        dq0 = _dot(do16, s0.astype(BF16))
        dkt = _dot_tn(da16, qt16)
        dkc = _dot(v16, ds16)
        dv_ref[rs, sl] = (_dot_tn(a.astype(BF16), do16) + _dot_nt(kc16, ds16)).astype(BF16)
        lam_end = jnp.exp(p["b_end"])
        dst[h] = lam_end * ds_end + _dot_tn(do16, q016)

        qt, kt, q0, kc = (a.astype(F32) for a in (qt16, kt16, q016, kc16))
        db = dqt * qt + dq0 * q0 - dkt * kt - dkc * kc
        db_end = (jnp.sum(dkc * kc, axis=0, keepdims=True)
                  + jnp.sum(ds_end * s0, axis=0, keepdims=True) * lam_end)
        rowi = lax.broadcasted_iota(jnp.int32, (c, dh), 0)
        db = db + jnp.where(rowi == c - 1, db_end, 0.0)
        r = lax.broadcasted_iota(jnp.int32, (c, c), 0)
        s = lax.broadcasted_iota(jnp.int32, (c, c), 1)
        dlf = _tri_dot(jnp.where(s >= r, 1.0, 0.0).astype(BF16), db)

        dqs = dqt * p["e_q"] + dq0 * p["e_0"]
        dq_ref[rs, sl] = (dqs * p["dqs"]).astype(BF16)
        dk = dkt * p["e_k"] + dkc * p["e_c"]
        sz, lb = p["sz"], p["lb"]
        common = dlf / p["f"] - dk
        dz_ref[rs, sl] = ((1.0 - lb) * sz * (1.0 - sz) * common).astype(BF16)
        dlb_ref[:, sl] += jnp.sum((1.0 - sz) * common, axis=0, keepdims=True)

    def body(*refs):
        dng_ref, dlb_ref, dst = refs[-3:]

        @pl.when(pl.program_id(0) == 0)
        def _():
            dst[...] = jnp.zeros_like(dst)
            dng_ref[...] = jnp.zeros_like(dng_ref)
            dlb_ref[...] = jnp.zeros_like(dlb_ref)

        for sub in reversed(range(subs)):
            for h in range(HGRN_HEADS):
                head(h, sub, *refs)

    def wide(off):
        return pl.BlockSpec((subs * c, HGRN_WIDTH), lambda i: (last - i, off))

    vec = pl.BlockSpec((1, HGRN_WIDTH), lambda i: (0, 0))
    act = jax.ShapeDtypeStruct((t, HGRN_WIDTH), BF16)
    vsh = jax.ShapeDtypeStruct((1, HGRN_WIDTH), F32)
    return pl.pallas_call(
        body, name="hgrn_bwd", grid=(n_chunks // subs,),
        in_specs=[wide(1), wide(2), wide(3), wide(4), wide(0),
                  pl.BlockSpec((HGRN_HEADS, subs, dh, dh), lambda i: (0, last - i, 0, 0)),
                  wide(0), pl.BlockSpec((2, HGRN_WIDTH), lambda i: (0, 0)), vec],
        out_specs=[wide(0), wide(0), wide(0), wide(0), vec, vec],
        out_shape=[act, act, act, act, vsh, vsh],
        scratch_shapes=[pltpu.VMEM((HGRN_HEADS, dh, dh), F32)],
        compiler_params=_params(("arbitrary",)),
    )(proj, proj, proj, proj, o_raw, s0s, doh, logits, ng)


def _lb_bwd(logits, dlb):
    def body(lg_ref, d_ref, o_ref):
        lg = lg_ref[...]
        lb = _sigmoid(lg[0:1, :] - lg[1:2, :])
        g = d_ref[...] * lb * (1.0 - lb)
        o_ref[0:1, :] = g
        o_ref[1:2, :] = -g

    return pl.pallas_call(body, name="hgrn_lb_bwd", out_shape=jax.ShapeDtypeStruct(logits.shape, F32),
                          compiler_params=_params())(logits, dlb)


MERGE_TC = 1024
GS_BLOCK = (S5_WIDTH + 4 * HGRN_WIDTH) // MERGE_TC
GH_BLOCK = GS_BLOCK + D_MODEL // MERGE_TC


def _merge_fwd(proj, ys, yh):
    t = proj.shape[0]
    tr = _pick(t, 256, SUBLANE)

    def body(gs_ref, gh_ref, ys_ref, yh_ref, m_ref):
        m_ref[...] = (_sigmoid(gs_ref[...]) * ys_ref[...] + _sigmoid(gh_ref[...]) * yh_ref[...]).astype(BF16)

    blk = pl.BlockSpec((tr, MERGE_TC), lambda i, j: (i, j))
    return pl.pallas_call(
        body, name="merge_fwd", grid=(t // tr, D_MODEL // MERGE_TC),
        in_specs=[pl.BlockSpec((tr, MERGE_TC), lambda i, j: (i, GS_BLOCK + j)),
                  pl.BlockSpec((tr, MERGE_TC), lambda i, j: (i, GH_BLOCK + j)), blk, blk],
        out_specs=blk, out_shape=jax.ShapeDtypeStruct((t, D_MODEL), BF16),
        compiler_params=_params(("parallel", "parallel")),
    )(proj, proj, ys, yh)


def _merge_bwd(proj, ys, yh, dm, after=None):
    t = proj.shape[0]
    tr = _pick(t, 256, SUBLANE)

    def body(gs_ref, gh_ref, ys_ref, yh_ref, dm_ref, dys_ref, dyh_ref, dgs_ref, dgh_ref):
        dmv = dm_ref[...]
        ss, sh = _sigmoid(gs_ref[...]), _sigmoid(gh_ref[...])
        dys_ref[...] = (dmv * ss).astype(BF16)
        dyh_ref[...] = (dmv * sh).astype(BF16)
        dgs_ref[...] = (dmv * ys_ref[...] * ss * (1.0 - ss)).astype(BF16)
        dgh_ref[...] = (dmv * yh_ref[...] * sh * (1.0 - sh)).astype(BF16)

    blk = pl.BlockSpec((tr, MERGE_TC), lambda i, j: (i, j))
    sh16 = jax.ShapeDtypeStruct((t, D_MODEL), BF16)
    in_specs = [pl.BlockSpec((tr, MERGE_TC), lambda i, j: (i, GS_BLOCK + j)),
                pl.BlockSpec((tr, MERGE_TC), lambda i, j: (i, GH_BLOCK + j)), blk, blk, blk]
    body, in_specs, args = _ordered(body, in_specs, [proj, proj, ys, yh, dm], after)
    return pl.pallas_call(
        body, name="merge_bwd", grid=(t // tr, D_MODEL // MERGE_TC),
        in_specs=in_specs,
        out_specs=[blk, blk, blk, blk], out_shape=[sh16, sh16, sh16, sh16],
        compiler_params=_params(("parallel", "parallel")),
    )(*args)


FFN_TC = 128
FFN_ROWS = 128
HALO = SUBLANE


def _pad_rows(dst, src_ref):
    t, c = src_ref.shape
    dst[0:HALO, :] = jnp.zeros((HALO, c), F32)
    dst[HALO:HALO + t, :] = src_ref[...]
    dst[HALO + t:HALO + t + HALO, :] = jnp.zeros((HALO, c), F32)


def _conv3(padded, w, b, r0, nrows):
    x0 = padded[HALO + r0:HALO + r0 + nrows, :]
    x1 = padded[HALO + r0 - 1:HALO + r0 - 1 + nrows, :]
    x2 = padded[HALO + r0 - 2:HALO + r0 - 2 + nrows, :]
    return b + w[0:1, :] * x2 + w[1:2, :] * x1 + w[2:3, :] * x0, (x0, x1, x2)


def _ffn_act_fwd(up, cw, cb):
    t = up.shape[0]
    rows = _pick(t, FFN_ROWS, SUBLANE)
    nvb = D_FF // FFN_TC

    def body(ug_ref, uv_ref, wg_ref, wv_ref, bg_ref, bv_ref, act_ref, pg, pv):
        wg, wv, bg, bv = wg_ref[...], wv_ref[...], bg_ref[...], bv_ref[...]
        _pad_rows(pg, ug_ref)
        _pad_rows(pv, uv_ref)
        for r0 in range(0, t, rows):
            cg, _ = _conv3(pg, wg, bg, r0, rows)
            cv, _ = _conv3(pv, wv, bv, r0, rows)
            act_ref[r0:r0 + rows, :] = (_silu_and_grad(cg)[0] * cv).astype(BF16)

    def colblk(nrow, off):
        return pl.BlockSpec((nrow, FFN_TC), lambda j: (0, off + j))

    return pl.pallas_call(
        body, name="ffn_act_fwd", grid=(nvb,),
        in_specs=[colblk(t, 0), colblk(t, nvb), colblk(3, 0), colblk(3, nvb), colblk(1, 0), colblk(1, nvb)],
        out_specs=colblk(t, 0), out_shape=jax.ShapeDtypeStruct((t, D_FF), BF16),
        scratch_shapes=[pltpu.VMEM((t + 2 * HALO, FFN_TC), F32), pltpu.VMEM((t + 2 * HALO, FFN_TC), F32)],
        compiler_params=_params(("parallel",)),
    )(up, up, cw, cw, cb, cb)


def _ffn_act_bwd(up, dact, cw, cb, after=None):
    t = up.shape[0]
    rows = _pick(t, FFN_ROWS, SUBLANE)
    nvb = D_FF // FFN_TC

    def body(ug_ref, uv_ref, da_ref, wg_ref, wv_ref, bg_ref, bv_ref,
             dug_ref, duv_ref, dwg_ref, dwv_ref, dbg_ref, dbv_ref, pg, pv, dcs):
        wg, wv, bg, bv = wg_ref[...], wv_ref[...], bg_ref[...], bv_ref[...]
        _pad_rows(pg, ug_ref)
        _pad_rows(pv, uv_ref)
        ext = rows + HALO
        acc_g = [jnp.zeros((1, FFN_TC), F32) for _ in range(4)]
        acc_v = [jnp.zeros((1, FFN_TC), F32) for _ in range(4)]
        for r0 in range(0, t, rows):
            cg, xg = _conv3(pg, wg, bg, r0, ext)
            cv, xv = _conv3(pv, wv, bv, r0, ext)
            if r0 + ext <= t:
                dav = da_ref[r0:r0 + ext, :]
            else:
                dav = jnp.concatenate([da_ref[r0:t, :], jnp.zeros((HALO, FFN_TC), F32)], axis=0)
            sg, dsg = _silu_and_grad(cg)
            for h, (dconv, xs, w, acc, out) in enumerate(((dav * cv * dsg, xg, wg, acc_g, dug_ref),
                                                           (dav * sg, xv, wv, acc_v, duv_ref))):
                dcs[h] = dconv
                d0 = dconv[0:rows, :]
                d1 = dcs[h, 1:rows + 1, :]
                d2 = dcs[h, 2:rows + 2, :]
                out[r0:r0 + rows, :] = (w[2:3, :] * d0 + w[1:2, :] * d1 + w[0:1, :] * d2).astype(BF16)
                x0, x1, x2 = xs
                acc[0] = acc[0] + jnp.sum(d0 * x2[0:rows, :], axis=0, keepdims=True)
                acc[1] = acc[1] + jnp.sum(d0 * x1[0:rows, :], axis=0, keepdims=True)
                acc[2] = acc[2] + jnp.sum(d0 * x0[0:rows, :], axis=0, keepdims=True)
                acc[3] = acc[3] + jnp.sum(d0, axis=0, keepdims=True)
        for acc, dw_ref, db_ref in ((acc_g, dwg_ref, dbg_ref), (acc_v, dwv_ref, dbv_ref)):
            dw_ref[0:1, :] = acc[0]
            dw_ref[1:2, :] = acc[1]
            dw_ref[2:3, :] = acc[2]
            db_ref[...] = acc[3]

    def colblk(nrow, off):
        return pl.BlockSpec((nrow, FFN_TC), lambda j: (0, off + j))

    in_specs = [colblk(t, 0), colblk(t, nvb), colblk(t, 0), colblk(3, 0), colblk(3, nvb), colblk(1, 0), colblk(1, nvb)]
    body, in_specs, args = _ordered(body, in_specs, [up, up, dact, cw, cw, cb, cb], after)
    return pl.pallas_call(
        body, name="ffn_act_bwd", grid=(nvb,),
        in_specs=in_specs,
        out_specs=[colblk(t, 0), colblk(t, 0), colblk(3, 0), colblk(3, 0), colblk(1, 0), colblk(1, 0)],
        out_shape=[jax.ShapeDtypeStruct((t, D_FF), BF16), jax.ShapeDtypeStruct((t, D_FF), BF16),
                   jax.ShapeDtypeStruct((3, D_FF), F32), jax.ShapeDtypeStruct((3, D_FF), F32),
                   jax.ShapeDtypeStruct((1, D_FF), F32), jax.ShapeDtypeStruct((1, D_FF), F32)],
        scratch_shapes=[pltpu.VMEM((t + 2 * HALO, FFN_TC), F32), pltpu.VMEM((t + 2 * HALO, FFN_TC), F32),
                        pltpu.VMEM((2, rows + HALO, FFN_TC), F32)],
        compiler_params=_params(("parallel",)),
    )(*args)


def _all_gather(shards, name):
    nw = len(shards)

    def body(*refs):
        x_refs, out_refs = refs[:nw], refs[nw:2 * nw]
        send_sems, recv_sems, local_sems = refs[2 * nw:]
        x, y, c = lax.axis_index("x"), lax.axis_index("y"), lax.axis_index("c")
        me, sibling = (x, y, c), (x, y, 1 - c)
        chips = [(1 - x, y), (x, 1 - y), (1 - x, 1 - y)]

        def copy(w, k, block, to, src=None):
            slot = out_refs[w].at[4 * block[0] + 2 * block[1] + block[2]]
            return pltpu.make_async_remote_copy(
                src_ref=slot if src is None else src, dst_ref=slot,
                send_sem=send_sems.at[w, k], recv_sem=recv_sems.at[w, k],
                device_id=to, device_id_type=MESH)

        mine, first, passed = [], [], []
        for w in range(nw):
            cp = pltpu.make_async_copy(x_refs[w], out_refs[w].at[4 * x + 2 * y + c], local_sems.at[w])
            cp.start()
            mine.append(cp)
            first.append(copy(w, 0, me, sibling, src=x_refs[w]))
            first += [copy(w, 1 + j, me, (*chip, c), src=x_refs[w]) for j, chip in enumerate(chips)]
        for cp in first:
            cp.start()
        for w in range(nw):
            for j, chip in enumerate(chips):
                copy(w, 1 + j, (*chip, c), me).wait_recv()
                fwd = copy(w, 4 + j, (*chip, c), sibling)
                fwd.start()
                passed.append(fwd)
        for w in range(nw):
            copy(w, 0, sibling, me).wait_recv()
            for j, chip in enumerate(chips):
                copy(w, 4 + j, (*chip, 1 - c), me).wait_recv()
        for cp in first + passed:
            cp.wait_send()
        for cp in mine:
            cp.wait()

    anyspec = pl.BlockSpec(memory_space=pl.ANY)
    return pl.pallas_call(
        body, name=name,
        in_specs=[anyspec] * nw, out_specs=[anyspec] * nw,
        out_shape=[jax.ShapeDtypeStruct((N_DEV,) + s.shape, s.dtype) for s in shards],
        scratch_shapes=[pltpu.SemaphoreType.DMA((nw, 7)), pltpu.SemaphoreType.DMA((nw, 7)),
                        pltpu.SemaphoreType.DMA((nw,))],
    )(*shards)


HBM_SPEC = pl.BlockSpec(memory_space=pltpu.HBM)
SEM_SPEC = pl.BlockSpec(memory_space=pltpu.SEMAPHORE)
ANY_SPEC = pl.BlockSpec(memory_space=pl.ANY)
DATAFLOW = pltpu.SideEffectType.DATAFLOW_SIDE_EFFECTING


def _my_index():
    return 4 * lax.axis_index("x") + 2 * lax.axis_index("y") + lax.axis_index("c")


def _peers():
    x, y, c = lax.axis_index("x"), lax.axis_index("y"), lax.axis_index("c")
    peers = []
    for k in range(1, N_DEV):
        px = 1 - x if k & 4 else x
        py = 1 - y if k & 2 else y
        pc = 1 - c if k & 1 else c
        peers.append((k, (px, py, pc), 4 * px + 2 * py + pc))
    return peers


def _split_copy(src_ref, land_ref, send_sems, recv_sems, w, k, peer, slot, scatter, outgoing):
    return pltpu.make_async_remote_copy(
        src_ref=src_ref.at[slot] if scatter else src_ref,
        dst_ref=land_ref.at[_my_index() if outgoing else slot],
        send_sem=send_sems.at[w * (N_DEV - 1) + k - 1], recv_sem=recv_sems.at[w * (N_DEV - 1) + k - 1],
        device_id=peer, device_id_type=MESH)


def _landing_zone(src, scatter):
    me = _my_index()
    own = lax.dynamic_index_in_dim(src, me, 0, keepdims=True) if scatter else src[None]
    shape = src.shape if scatter else (N_DEV,) + src.shape
    return lax.dynamic_update_slice_in_dim(lax.empty(shape, src.dtype), own, me, 0)


def _exchange_start(srcs, scatter, after, name, lands=None):
    nw = len(srcs)
    if lands is None:
        lands = [_landing_zone(s, scatter) for s in srcs]

    afters = [] if after is None else [after]

    def body(*refs):
        s_refs, l_refs = refs[:nw], refs[nw:2 * nw]
        send_sems, recv_sems = refs[2 * nw + len(afters)], refs[2 * nw + len(afters) + 1]
        token = refs[-1]
        for w in range(nw):
            for k, peer, slot in _peers():
                _split_copy(s_refs[w], l_refs[w], send_sems, recv_sems, w, k, peer, slot, scatter, True).start()
        token[...] = jnp.zeros_like(token)

    sems = pltpu.SemaphoreType.DMA((nw * (N_DEV - 1),))
    outs = pl.pallas_call(
        body, name=name,
        out_shape=(sems, sems, *[pltpu.HBM(a.shape, a.dtype) for a in (*srcs, *lands)],
                   jax.ShapeDtypeStruct((SUBLANE, LANE), F32)),
        in_specs=[HBM_SPEC] * (2 * nw) + [ANY_SPEC] * len(afters),
        out_specs=(SEM_SPEC, SEM_SPEC, *[HBM_SPEC] * (2 * nw), pl.BlockSpec(memory_space=pltpu.VMEM)),
        input_output_aliases={i: 2 + i for i in range(2 * nw)},
        compiler_params=pltpu.CompilerParams(has_side_effects=DATAFLOW),
    )(*[pltpu.with_memory_space_constraint(a, pltpu.HBM) for a in (*srcs, *lands)], *afters)
    return dict(sems=outs[:2], srcs=outs[2:2 + nw], lands=outs[2 + nw:2 + 2 * nw], token=outs[-1], scatter=scatter)


def _exchange_wait(handle, afters, name):
    srcs, lands, scatter = handle["srcs"], handle["lands"], handle["scatter"]
    nw = len(srcs)

    def body(*refs):
        s_refs, l_refs = refs[:nw], refs[nw:2 * nw]
        send_sems, recv_sems = refs[2 * nw], refs[2 * nw + 1]
        for w in range(nw):
            for k, peer, slot in _peers():
                cp = _split_copy(s_refs[w], l_refs[w], send_sems, recv_sems, w, k, peer, slot, scatter, False)
                cp.wait_send()
                cp.wait_recv()

    outs = pl.pallas_call(
        body, name=name,
        out_shape=tuple(pltpu.HBM(a.shape, a.dtype) for a in (*srcs, *lands)),
        in_specs=[HBM_SPEC] * (2 * nw) + [SEM_SPEC, SEM_SPEC] + [ANY_SPEC] * len(afters),
        out_specs=tuple([HBM_SPEC] * (2 * nw)),
        input_output_aliases={i: i for i in range(2 * nw)},
        compiler_params=pltpu.CompilerParams(has_side_effects=DATAFLOW),
    )(*srcs, *lands, *handle["sems"], *afters)
    return list(outs[nw:])


def _chips_and_sibling():
    x, y, c = lax.axis_index("x"), lax.axis_index("y"), lax.axis_index("c")
    return [(1 - x, y), (x, 1 - y), (1 - x, 1 - y)], (x, y, 1 - c), c


def _slot(px, py, pc):
    return 4 * px + 2 * py + pc


def _two_level_start(shards, name):
    nw = len(shards)
    lands = [_landing_zone(s, False) for s in shards]

    def body(*refs):
        s_refs, l_refs = refs[:nw], refs[nw:2 * nw]
        send_sems, recv_sems, token = refs[2 * nw], refs[2 * nw + 1], refs[-1]
        chips, sibling, c = _chips_and_sibling()
        for w in range(nw):
            for k, to in enumerate([sibling] + [(*chip, c) for chip in chips]):
                pltpu.make_async_remote_copy(
                    src_ref=s_refs[w], dst_ref=l_refs[w].at[_my_index()],
                    send_sem=send_sems.at[4 * w + k], recv_sem=recv_sems.at[4 * w + k],
                    device_id=to, device_id_type=MESH).start()
        token[...] = jnp.zeros_like(token)

    sems = pltpu.SemaphoreType.DMA((4 * nw,))
    outs = pl.pallas_call(
        body, name=name,
        out_shape=(sems, sems, *[pltpu.HBM(a.shape, a.dtype) for a in (*shards, *lands)],
                   jax.ShapeDtypeStruct((SUBLANE, LANE), F32)),
        in_specs=[HBM_SPEC] * (2 * nw),
        out_specs=(SEM_SPEC, SEM_SPEC, *[HBM_SPEC] * (2 * nw), pl.BlockSpec(memory_space=pltpu.VMEM)),
        input_output_aliases={i: 2 + i for i in range(2 * nw)},
        compiler_params=pltpu.CompilerParams(has_side_effects=DATAFLOW),
    )(*[pltpu.with_memory_space_constraint(a, pltpu.HBM) for a in (*shards, *lands)])
    return dict(sems=outs[:2], srcs=outs[2:2 + nw], lands=outs[2 + nw:2 + 2 * nw], token=outs[-1])


def _two_level_pass(handle, afters, name):
    srcs, lands = handle["srcs"], handle["lands"]
    nw = len(srcs)

    def body(*refs):
        s_refs, l_refs = refs[:nw], refs[nw:2 * nw]
        send_a, recv_a = refs[2 * nw], refs[2 * nw + 1]
        send_b, recv_b = refs[2 * nw + 2 + len(afters)], refs[2 * nw + 3 + len(afters)]
        chips, sibling, c = _chips_and_sibling()
        for w in range(nw):
            for j, chip in enumerate(chips):
                landed = l_refs[w].at[_slot(*chip, c)]
                pltpu.make_async_remote_copy(
                    src_ref=s_refs[w], dst_ref=landed, send_sem=send_a.at[4 * w + 1 + j], recv_sem=recv_a.at[4 * w + 1 + j],
                    device_id=(*chip, c), device_id_type=MESH).wait_recv()
                pltpu.make_async_remote_copy(
                    src_ref=landed, dst_ref=landed, send_sem=send_b.at[3 * w + j], recv_sem=recv_b.at[3 * w + j],
                    device_id=sibling, device_id_type=MESH).start()

    sems = pltpu.SemaphoreType.DMA((3 * nw,))
    outs = pl.pallas_call(
        body, name=name,
        out_shape=(sems, sems, *[pltpu.HBM(a.shape, a.dtype) for a in (*srcs, *lands)]),
        in_specs=[HBM_SPEC] * (2 * nw) + [SEM_SPEC, SEM_SPEC] + [ANY_SPEC] * len(afters),
        out_specs=(SEM_SPEC, SEM_SPEC, *[HBM_SPEC] * (2 * nw)),
        input_output_aliases={i: 2 + i for i in range(2 * nw)},
        compiler_params=pltpu.CompilerParams(has_side_effects=DATAFLOW),
    )(*srcs, *lands, *handle["sems"], *afters)
    return dict(sems=handle["sems"], sems_pass=outs[:2], srcs=outs[2:2 + nw], lands=outs[2 + nw:2 + 2 * nw])


def _two_level_wait(handle, name):
    srcs, lands = handle["srcs"], handle["lands"]
    nw = len(srcs)

    def body(*refs):
        s_refs, l_refs = refs[:nw], refs[nw:2 * nw]
        send_a, recv_a, send_b, recv_b = refs[2 * nw:2 * nw + 4]
        chips, sibling, c = _chips_and_sibling()
        x, y = sibling[0], sibling[1]
        for w in range(nw):
            first = pltpu.make_async_remote_copy(
                src_ref=s_refs[w], dst_ref=l_refs[w].at[_slot(x, y, 1 - c)], send_sem=send_a.at[4 * w],
                recv_sem=recv_a.at[4 * w], device_id=sibling, device_id_type=MESH)
            first.wait_send()
            first.wait_recv()
            for j, chip in enumerate(chips):
                pltpu.make_async_remote_copy(
                    src_ref=s_refs[w], dst_ref=l_refs[w].at[_slot(*chip, c)], send_sem=send_a.at[4 * w + 1 + j],
                    recv_sem=recv_a.at[4 * w + 1 + j], device_id=(*chip, c), device_id_type=MESH).wait_send()
                passed = pltpu.make_async_remote_copy(
                    src_ref=l_refs[w].at[_slot(*chip, c)], dst_ref=l_refs[w].at[_slot(*chip, 1 - c)],
                    send_sem=send_b.at[3 * w + j], recv_sem=recv_b.at[3 * w + j], device_id=sibling, device_id_type=MESH)
                passed.wait_send()
                passed.wait_recv()

    outs = pl.pallas_call(
        body, name=name,
        out_shape=tuple(pltpu.HBM(a.shape, a.dtype) for a in (*srcs, *lands)),
        in_specs=[HBM_SPEC] * (2 * nw) + [SEM_SPEC] * 4,
        out_specs=tuple([HBM_SPEC] * (2 * nw)),
        input_output_aliases={i: i for i in range(2 * nw)},
        compiler_params=pltpu.CompilerParams(has_side_effects=DATAFLOW),
    )(*srcs, *lands, *handle["sems"], *handle["sems_pass"])
    return list(outs[nw:])


def _adamw(w, g, m, v):
    m = ADAM_B1 * m + (1.0 - ADAM_B1) * g
    v = ADAM_B2 * v + (1.0 - ADAM_B2) * (g * g)
    m_hat = m / (1.0 - ADAM_B1 ** ADAM_STEP)
    v_hat = v / (1.0 - ADAM_B2 ** ADAM_STEP)
    delta = -ADAM_LR * (m_hat / (jnp.sqrt(v_hat) + ADAM_EPS) + ADAM_WD * w)
    return delta, m, v


def _sum_adam(parts, w, m, v, name):
    _, r, c = parts.shape
    tr = _pick(r, 128, 16)

    def body(p_ref, w_ref, m_ref, v_ref, g_ref, d_ref, mo_ref, vo_ref):
        g = p_ref[0].astype(F32)
        for s in range(1, N_DEV):
            g = g + p_ref[s].astype(F32)
        g_ref[...] = g
        d_ref[...], mo_ref[...], vo_ref[...] = _adamw(w_ref[...], g, m_ref[...], v_ref[...])

    row = pl.BlockSpec((tr, c), lambda i: (i, 0))
    sh = jax.ShapeDtypeStruct((r, c), F32)
    return pl.pallas_call(
        body, name=name, grid=(r // tr,),
        in_specs=[pl.BlockSpec((N_DEV, tr, c), lambda i: (0, i, 0)), row, row, row],
        out_specs=[row, row, row, row], out_shape=[sh, sh, sh, sh],
        compiler_params=_params(("parallel",)),
    )(parts, w, m, v)


def _sum_slots(parts, name):
    _, r, c = parts.shape
    tr = _pick(r, 512, SUBLANE)

    def body(p_ref, o_ref):
        g = p_ref[0]
        for s in range(1, N_DEV):
            g = g + p_ref[s]
        o_ref[...] = g

    return pl.pallas_call(
        body, name=name, grid=(r // tr,),
        in_specs=[pl.BlockSpec((N_DEV, tr, c), lambda i: (0, i, 0))],
        out_specs=pl.BlockSpec((tr, c), lambda i: (i, 0)), out_shape=jax.ShapeDtypeStruct((r, c), F32),
        compiler_params=_params(("parallel",)),
    )(parts)


def _adam_rows(g, w, m, v, name):
    r, c = g.shape
    tr = _pick(r, 512, SUBLANE)

    def body(g_ref, w_ref, m_ref, v_ref, d_ref, mo_ref, vo_ref):
        d_ref[...], mo_ref[...], vo_ref[...] = _adamw(w_ref[...], g_ref[...], m_ref[...], v_ref[...])

    row = pl.BlockSpec((tr, c), lambda i: (i, 0))
    sh = jax.ShapeDtypeStruct((r, c), F32)
    return pl.pallas_call(body, name=name, grid=(r // tr,), in_specs=[row] * 4, out_specs=[row] * 3,
                          out_shape=[sh, sh, sh], compiler_params=_params(("parallel",)))(g, w, m, v)


def _pack(arrays):
    flat = jnp.concatenate([a.reshape(-1).astype(F32) for a in arrays])
    pad = (-flat.shape[0]) % (SUBLANE * LANE)
    return jnp.pad(flat, (0, pad)).reshape(-1, LANE)


def _unpack(packed, shapes):
    flat = packed.reshape(-1)
    out, off = [], 0
    for s in shapes:
        n = math.prod(s)
        out.append(flat[off:off + n].reshape(s))
        off += n
    return out


def _block_diag(t):
    eye = jnp.eye(S5_SUPER, dtype=bool)
    bd = jnp.where(eye[None, :, None, :, None], t[:, :, :, None, :], 0.0)
    return bd.reshape(S5_SUPER, S5_SUPER * t.shape[2], S5_SUPER * t.shape[3])


def _diag_blocks(dense, a, b):
    x = dense.reshape(S5_SUPER, S5_SUPER, a, S5_SUPER, b)
    return jnp.moveaxis(jnp.diagonal(x, axis1=1, axis2=3), -1, 1)


def _s5_layouts(b_re, b_im, c_re, c_im, d):
    g2 = (S5_GROUPS // S5_SUPER, S5_SUPER)
    bt = lambda b: _block_diag(b.reshape(*g2, S5_STATE, S5_GROUP).transpose(0, 1, 3, 2))
    ct = lambda c: _block_diag(c.reshape(*g2, S5_GROUP, S5_STATE).transpose(0, 1, 3, 2))
    bsg = jnp.concatenate([bt(b_re), bt(b_im)], axis=2).astype(BF16)
    ccat = jnp.concatenate([ct(c_re), -ct(c_im)], axis=1).astype(BF16)
    return bsg, ccat, d.reshape(1, S5_WIDTH)


def _s5_param_grads(gb, gc):
    n = S5_LANES
    gb_re = _diag_blocks(gb[:, :, 0:n], S5_GROUP, S5_STATE).transpose(0, 1, 3, 2).reshape(S5_GROUPS, S5_STATE, S5_GROUP)
    gb_im = _diag_blocks(gb[:, :, n:2 * n], S5_GROUP, S5_STATE).transpose(0, 1, 3, 2).reshape(S5_GROUPS, S5_STATE, S5_GROUP)
    gc_re = _diag_blocks(gc[:, 0:n, :], S5_STATE, S5_GROUP).transpose(0, 1, 3, 2).reshape(S5_GROUPS, S5_GROUP, S5_STATE)
    gc_im = -_diag_blocks(gc[:, n:2 * n, :], S5_STATE, S5_GROUP).transpose(0, 1, 3, 2).reshape(S5_GROUPS, S5_GROUP, S5_STATE)
    return gb_re, gb_im, gc_re, gc_im


def _local_step(x, target, weight, emit, small, after=None):
    sp = small
    a_re, a_im = sp["s5_a_re"], sp["s5_a_im"]
    ldt = sp["s5_log_dt"].reshape(S5_GROUPS, 1)

    h1 = _rms_fwd(x, sp["ln_mix_g"], "rms_mix", after=after)
    w_in = weight("w_in", h1)
    proj = _mm_nn(h1, w_in, "mm_in", after=weight("after_w_in", None))
    conv_w = weight("conv_w", None)
    disc = _s5_param_fwd(a_re, a_im, ldt)
    bsg, ccat, d_row = sp["s5_layouts"]
    abar_t, coef_t = _s5_to_tile(disc[0], disc[1]), _s5_to_tile(disc[2], disc[3])
    y, sb = _s5_fwd(proj, bsg, ccat, d_row, abar_t, coef_t)
    z16 = _gelu_fwd(y)
    w_glu = weight("s5_w_glu", z16)
    gl = _mm_nn(z16, w_glu, "mm_glu")
    z2 = _glu_fwd(y, gl, sp["s5_b_glu"])
    w_ps = weight("w_proj_s5", z2)
    ys = _mm_nn(z2, w_ps, "mm_proj_s5")
    o_raw, oh, s0s = _hgrn_fwd(proj, sp["hgrn_lb_logits"], sp["hgrn_norm_g"])
    w_ph = weight("w_proj_hgrn", oh)
    yh = _mm_nn(oh, w_ph, "mm_proj_hgrn")
    merged = _merge_fwd(proj, ys, yh)
    w_out = weight("w_out", merged)
    x1 = _mm_nn(merged, w_out, "mm_out", res=x)
    h2 = _rms_fwd(x1, sp["ln_ffn_g"], "rms_ffn")
    w_up = weight("w_up", h2)
    up = _mm_nn(h2, w_up, "mm_up")
    act = _ffn_act_fwd(up, conv_w, sp["conv_b"])
    w_down = weight("w_down", act)
    x2 = _mm_nn(act, w_down, "mm_down", res=x1)
    dx2, dx2_16, g_ln_final, loss = _loss_head(x2, sp["ln_final_g"], target)

    dact = _mm_nt(dx2_16, w_down, "mm_down_dx")
    tok = emit("w_down", _mm_tn(act, dx2_16, 1, "mm_down_dw"))
    dup_g, dup_v, dcw_g, dcw_v, dcb_g, dcb_v = _ffn_act_bwd(up, dact, conv_w, sp["conv_b"], after=tok)
    dup = jnp.concatenate([dup_g, dup_v], axis=1)
    g_conv_w = jnp.concatenate([dcw_g, dcw_v], axis=1)
    g_conv_b = jnp.concatenate([dcb_g, dcb_v], axis=1)
    dh2 = _mm_nt(dup, w_up, "mm_up_dx")
    tok = emit("w_up", _mm_tn(h2, dup, N_DEV, "mm_up_dw"))
    dx1, dx1_16, g_ln_ffn = _rms_bwd(x1, sp["ln_ffn_g"], dh2, dx2, "rms_ffn_bwd", True, after=tok)

    dmerged = _mm_nt(dx1_16, w_out, "mm_out_dx")
    tok = emit("w_out", _mm_tn(merged, dx1_16, 1, "mm_out_dw"))
    dys, dyh, dgs, dgh = _merge_bwd(proj, ys, yh, dmerged, after=tok)
    doh = _mm_nt(dyh, w_ph, "mm_proj_hgrn_dx")
    tok = emit("w_proj_hgrn", _mm_tn(oh, dyh, N_DEV, "mm_proj_hgrn_dw"))
    dz2 = _mm_nt(dys, w_ps, "mm_proj_s5_dx", after=tok)
    tok = emit("w_proj_s5", _mm_tn(z2, dys, N_DEV, "mm_proj_s5_dw"))
    dgl, dza, g_b_glu = _glu_bwd(y, gl, sp["s5_b_glu"], dz2, after=tok)
    dzb = _mm_nt(dgl, w_glu, "mm_glu_dx")
    tok = emit("s5_w_glu", _mm_tn(z16, dgl, 1, "mm_glu_dw"))
    dy = _gelu_bwd(y, dza, dzb, after=tok)
    du, gb, gc, gd, g_abar_t, g_coef_t = _s5_bwd(proj, dy, sb, bsg, ccat, d_row, abar_t, coef_t)
    g_a_re, g_a_im, g_ldt = _s5_param_bwd(a_re, a_im, ldt, [*_s5_from_tile(g_abar_t), *_s5_from_tile(g_coef_t)])
    g_b_re, g_b_im, g_c_re, g_c_im = _s5_param_grads(gb, gc)
    dq, dz, dv, dg, g_norm, dlb = _hgrn_bwd(proj, o_raw, s0s, doh, sp["hgrn_lb_logits"], sp["hgrn_norm_g"])
    g_logits = _lb_bwd(sp["hgrn_lb_logits"], dlb)

    small_g = dict(s5_a_re=g_a_re, s5_a_im=g_a_im, s5_log_dt=g_ldt.reshape(1, S5_GROUPS),
                   s5_b_re=g_b_re, s5_b_im=g_b_im, s5_c_re=g_c_re, s5_c_im=g_c_im,
                   s5_d=gd.reshape(S5_GROUPS, S5_GROUP), s5_b_glu=g_b_glu, hgrn_lb_logits=g_logits,
                   hgrn_norm_g=g_norm, ln_ffn_g=g_ln_ffn, conv_w=g_conv_w, conv_b=g_conv_b, ln_final_g=g_ln_final,
                   loss=loss[0, 0:1])
    tok_small = emit("small", small_g)

    dproj = jnp.concatenate([du, dq, dz, dv, dg, dgs, dgh], axis=1)
    tok = emit("w_in", _mm_tn(h1, dproj, N_DEV, "mm_in_dw", after=tok_small))
    dh1 = _mm_nt(dproj, w_in, "mm_in_dx")
    grad_x, g_ln_mix = _rms_bwd(x, sp["ln_mix_g"], dh1, dx1, "rms_mix_bwd", False, after=tok)
    return grad_x, g_ln_mix


BIG = ("w_in", "s5_w_glu", "w_proj_s5", "w_proj_hgrn", "w_out", "w_up", "w_down")
COL_SHARDED = ("w_in", "w_proj_s5", "w_proj_hgrn", "w_up")
SMALL = ("ln_mix_g", "s5_a_re", "s5_a_im", "s5_log_dt", "s5_b_re", "s5_b_im", "s5_c_re", "s5_c_im", "s5_d",
         "s5_b_glu", "hgrn_lb_logits", "hgrn_norm_g", "ln_ffn_g", "conv_b", "ln_final_g")
WEIGHTS = ("ln_mix_g", "w_in", "s5_a_re", "s5_a_im", "s5_log_dt", "s5_b_re", "s5_b_im", "s5_c_re", "s5_c_im", "s5_d",
           "s5_w_glu", "s5_b_glu", "w_proj_s5", "hgrn_lb_logits", "hgrn_norm_g", "w_proj_hgrn", "w_out", "ln_ffn_g",
           "w_up", "conv_w", "conv_b", "w_down", "ln_final_g")


def kernel(x, ln_mix_g, w_in, s5_a_re, s5_a_im, s5_log_dt, s5_b_re, s5_b_im, s5_c_re, s5_c_im, s5_d, s5_w_glu, s5_b_glu, w_proj_s5, hgrn_lb_logits, hgrn_norm_g, w_proj_hgrn, w_out, ln_ffn_g, w_up, conv_w, conv_b, w_down, ln_final_g, loss_target, m_ln_mix_g, m_w_in, m_s5_a_re, m_s5_a_im, m_s5_log_dt, m_s5_b_re, m_s5_b_im, m_s5_c_re, m_s5_c_im, m_s5_d, m_s5_w_glu, m_s5_b_glu, m_w_proj_s5, m_hgrn_lb_logits, m_hgrn_norm_g, m_w_proj_hgrn, m_w_out, m_ln_ffn_g, m_w_up, m_conv_w, m_conv_b, m_w_down, m_ln_final_g, v_ln_mix_g, v_w_in, v_s5_a_re, v_s5_a_im, v_s5_log_dt, v_s5_b_re, v_s5_b_im, v_s5_c_re, v_s5_c_im, v_s5_d, v_s5_w_glu, v_s5_b_glu, v_w_proj_s5, v_hgrn_lb_logits, v_hgrn_norm_g, v_w_proj_hgrn, v_w_out, v_ln_ffn_g, v_w_up, v_conv_w, v_conv_b, v_w_down, v_ln_final_g):
    given = dict(locals())
    w = {n: given[n] for n in WEIGHTS}
    mom = {n: given["m_" + n] for n in WEIGHTS}
    var = {n: given["v_" + n] for n in WEIGHTS}

    first = _two_level_start([w_in[0].astype(BF16), conv_w[0]], "gather_first_start")
    zero = first["token"][0, 0]
    packed_small = SMALL[1:]
    pw, pm, pv = (_pack([d[n] for n in packed_small]) + zero for d in (w, mom, var))
    layouts = _s5_layouts(s5_b_re[0] + zero, s5_b_im[0], s5_c_re[0] + zero, s5_c_im[0], s5_d[0])
    gather_groups = (("s5_w_glu", "w_proj_s5", "w_proj_hgrn", "w_out"), ("w_up",), ("w_down",))
    shard16 = {n: w[n][0].astype(BF16) + zero.astype(BF16) for g in gather_groups for n in g}
    zones = {n: _landing_zone(s, False) for n, s in shard16.items()}
    pending, ready = {}, {}

    def weight(name, after):
        if "w_in" not in ready:
            local_work = [after, pw, pm, pv, layouts[0], layouts[1], *zones.values()]
            passed = _two_level_pass(first, local_work, "gather_first_pass")
            ready["w_in"], conv_w_all = _two_level_wait(passed, "gather_first_wait")
            ready["conv_w"] = conv_w_all.transpose(1, 0, 2).reshape(3, 2 * D_FF)
            token = ready["w_in"]
            for i, group in enumerate(gather_groups):
                handle = _exchange_start([shard16[n] for n in group], False, token, f"gather_start_{i}",
                                         lands=[zones[n] for n in group])
                token = handle["token"]
                for n in group:
                    pending[n] = (group, handle, f"gather_wait_{i}")
            ready["after_w_in"] = token
        if name not in ready:
            group, handle, wait_name = pending[name]
            for n, g in zip(group, _exchange_wait(handle, [after], wait_name)):
                ready[n] = g
        g = ready[name]
        return g if name not in BIG or name in COL_SHARDED else g.reshape(1, N_DEV * g.shape[1], g.shape[2])

    scatter_groups = (("w_down",), ("w_up",), ("w_out", "w_proj_hgrn", "w_proj_s5", "s5_w_glu"), ("w_in",))
    emitted, scatters = {}, []
    packed_names = SMALL[1:] + ("conv_w", "loss")

    def emit(name, grad):
        if name == "small":
            emitted[name] = ([grad[n].shape for n in packed_names],
                             _exchange_start([_pack([grad[n] for n in packed_names])], False, None, "small_start"))
            return emitted[name][1]["token"]
        emitted[name] = grad if name in COL_SHARDED else grad.reshape(N_DEV, -1, grad.shape[2])
        group = scatter_groups[len(scatters)]
        if not all(n in emitted for n in group):
            return None
        handle = _exchange_start([emitted[n] for n in group], True, None, f"scatter_start_{len(scatters)}")
        scatters.append((group, handle))
        return handle["token"]

    small = dict(ln_mix_g=ln_mix_g, s5_a_re=s5_a_re[0], s5_a_im=s5_a_im[0], s5_log_dt=s5_log_dt, s5_layouts=layouts,
                 s5_b_glu=s5_b_glu, hgrn_lb_logits=hgrn_lb_logits, hgrn_norm_g=hgrn_norm_g, ln_ffn_g=ln_ffn_g,
                 conv_b=conv_b, ln_final_g=ln_final_g.reshape(1, D_MODEL))
    grad_x, g_ln_mix = _local_step(x[0], loss_target[0], weight, emit, small, after=first["token"])

    shapes, handle = emitted["small"]
    total = _sum_slots(_exchange_wait(handle, [grad_x], "small_wait")[0], "sum_small")
    summed = dict(zip(packed_names, _unpack(total, shapes)))
    mix_all = _all_gather([g_ln_mix.reshape(-1, LANE)], "gather_ln_mix")[0]
    summed["ln_mix_g"] = _sum_slots(mix_all, "sum_ln_mix").reshape(1, D_MODEL)

    grads, delta, new_m, new_v = {}, {}, {}, {}
    afters = [grad_x, total]
    for i, (group, handle) in enumerate(scatters):
        for n, r in zip(group, _exchange_wait(handle, afters, f"scatter_wait_{i}")):
            g, d, m2, v2 = _sum_adam(r, w[n][0], mom[n][0], var[n][0], "adam_" + n)
            grads[n], delta[n], new_m[n], new_v[n] = g[None], d[None], m2[None], v2[None]
        if i == len(scatters) - 2:
            afters = [delta[n] for g2, _ in scatters[:-1] for n in g2]

    d_s, m_s, v_s = _adam_rows(_pack([summed[n] for n in packed_small]), pw, pm, pv, "adam_small")
    wshapes = [w[n].shape for n in packed_small]
    for n, d, m2, v2 in zip(packed_small, _unpack(d_s, wshapes), _unpack(m_s, wshapes), _unpack(v_s, wshapes)):
        grads[n], delta[n], new_m[n], new_v[n] = summed[n].reshape(w[n].shape), d, m2, v2
    grads["ln_mix_g"] = summed["ln_mix_g"]
    delta["ln_mix_g"], new_m["ln_mix_g"], new_v["ln_mix_g"] = _adam_rows(summed["ln_mix_g"], ln_mix_g, m_ln_mix_g,
                                                                         v_ln_mix_g, "adam_ln_mix")
    me = 4 * lax.axis_index("x") + 2 * lax.axis_index("y") + lax.axis_index("c")
    ncol = conv_w.shape[2]
    g_cw = lax.dynamic_slice_in_dim(summed["conv_w"], me * ncol, ncol, axis=1)
    d_cw, m_cw, v_cw = _adam_rows(g_cw, conv_w[0], m_conv_w[0], v_conv_w[0], "adam_conv_w")
    grads["conv_w"], delta["conv_w"], new_m["conv_w"], new_v["conv_w"] = g_cw[None], d_cw[None], m_cw[None], v_cw[None]

    return (summed["loss"].reshape(()), grad_x[None], *[grads[n] for n in WEIGHTS], *[delta[n] for n in WEIGHTS],
            *[new_m[n] for n in WEIGHTS], *[new_v[n] for n in WEIGHTS])
```

```python
import math

import jax
import jax.numpy as jnp
from jax import lax
from jax.experimental import pallas as pl
from jax.experimental.pallas import tpu as pltpu

F32 = jnp.float32
BF16 = jnp.bfloat16

N_DEV = 8
D_MODEL = 2048
S5_WIDTH = 1024
S5_GROUP = 16
S5_GROUPS = 64
S5_STATE = 64
S5_MAX_RE = -1e-4
S5_SUPER = 8
S5_LANES = S5_SUPER * S5_STATE
HGRN_WIDTH = 1024
HGRN_HEADS = 8
HGRN_DH = 128
HGRN_CHUNK = 64
HGRN_SUBS = 4
D_FF = 5632
RMS_EPS = 1e-6
ADAM_LR = 0.001
ADAM_B1 = 0.9
ADAM_B2 = 0.999
ADAM_EPS = 1e-08
ADAM_WD = 0.01
ADAM_STEP = 10

LANE = 128
SUBLANE = 8
VMEM_LIMIT = 48 * 1024 * 1024
MESH = pl.DeviceIdType.MESH
GELU_C = math.sqrt(2.0 / math.pi)
GELU_A = 0.044715


def _params(sem=None):
    return pltpu.CompilerParams(dimension_semantics=sem, vmem_limit_bytes=VMEM_LIMIT)


def _pick(n, cap, unit=LANE):
    best = None
    for t in range(unit, min(n, cap) + 1, unit):
        if n % t == 0:
            best = t
    return best if best is not None else n


def _ordered(body, in_specs, args, after):
    if after is None:
        return body, list(in_specs), list(args)
    n_in = len(args)

    def ordered_body(*refs):
        return body(*refs[:n_in], *refs[n_in + 1:])

    return ordered_body, [*in_specs, pl.BlockSpec(memory_space=pl.ANY)], [*args, after]


def _sigmoid(x):
    return 0.5 * jnp.tanh(0.5 * x) + 0.5


def _silu_and_grad(x):
    s = _sigmoid(x)
    return x * s, s * (1.0 + x * (1.0 - s))


def _gelu_and_grad(y):
    inner = GELU_C * (y + GELU_A * y * y * y)
    th = jnp.tanh(inner)
    val = 0.5 * y * (1.0 + th)
    grad = 0.5 * (1.0 + th) + 0.5 * y * (1.0 - th * th) * GELU_C * (1.0 + 3.0 * GELU_A * y * y)
    return val, grad


def _dot(a, b):
    return jnp.dot(a, b, preferred_element_type=F32)


def _dot_nt(a, b):
    return lax.dot_general(a, b, (((1,), (1,)), ((), ())), preferred_element_type=F32)


def _dot_tn(a, b):
    return lax.dot_general(a, b, (((0,), (0,)), ((), ())), preferred_element_type=F32)


def _blocks_per_step(nb, ns, tn, cap=2048):
    if tn != ns:
        return 1
    best = 1
    for b in range(1, nb + 1):
        if nb % b == 0 and b * ns <= cap:
            best = b
    return best


def _mm_nn(a, w, name, res=None, out_dtype=F32, after=None):
    m, kdim = a.shape
    nb, _, ns = w.shape
    tm, tk, tn = _pick(m, 512), _pick(kdim, D_FF), _pick(ns, 1536)
    npb, nk = ns // tn, kdim // tk
    bps = _blocks_per_step(nb, ns, tn)
    assert bps == 1 or nk == 1

    def body(*refs):
        a_ref, w_ref = refs[0], refs[1]
        r_ref = refs[2] if res is not None else None
        o_ref = refs[3] if res is not None else refs[2]

        def finish(r, cols):
            if res is not None:
                r = r + r_ref[:, cols]
            o_ref[:, cols] = r.astype(out_dtype)

        if nk == 1:
            for b in range(bps):
                finish(_dot(a_ref[...], w_ref[b]), slice(b * tn, (b + 1) * tn))
            return
        acc = refs[-1]
        k = pl.program_id(2)

        @pl.when(k == 0)
        def _():
            acc[...] = jnp.zeros_like(acc)

        acc[...] += _dot(a_ref[...], w_ref[0])

        @pl.when(k == nk - 1)
        def _():
            finish(acc[...], slice(0, tn))

    in_specs = [pl.BlockSpec((tm, tk), lambda j, i, k: (i, k)),
                pl.BlockSpec((bps, tk, tn), lambda j, i, k: (j // npb, k, j % npb))]
    args = [a, w]
    if res is not None:
        in_specs.append(pl.BlockSpec((tm, bps * tn), lambda j, i, k: (i, j)))
        args.append(res)
    body, in_specs, args = _ordered(body, in_specs, args, after)
    return pl.pallas_call(
        body, name=name, grid=(nb * npb // bps, m // tm, nk),
        in_specs=in_specs, out_specs=pl.BlockSpec((tm, bps * tn), lambda j, i, k: (i, j)),
        out_shape=jax.ShapeDtypeStruct((m, nb * ns), out_dtype),
        scratch_shapes=[pltpu.VMEM((tm, tn), F32)] if nk > 1 else [],
        compiler_params=_params(("parallel", "parallel", "arbitrary")),
    )(*args)


def _mm_nt(a, w, name, out_dtype=F32, after=None):
    m, _ = a.shape
    nb, kdim, ns = w.shape
    tm, tko, tn = _pick(m, 1024), _pick(kdim, 1024), _pick(ns, 2048)
    npb = ns // tn
    bps = _blocks_per_step(nb, ns, tn)
    nred = nb * npb // bps

    def body(a_ref, w_ref, o_ref, *scratch):
        total = _dot_nt(a_ref[:, 0:tn], w_ref[0])
        for b in range(1, bps):
            total = total + _dot_nt(a_ref[:, b * tn:(b + 1) * tn], w_ref[b])
        if nred == 1:
            o_ref[...] = total.astype(out_dtype)
            return
        acc = scratch[0]
        n = pl.program_id(2)

        @pl.when(n == 0)
        def _():
            acc[...] = jnp.zeros_like(acc)

        acc[...] += total

        @pl.when(n == nred - 1)
        def _():
            o_ref[...] = acc[...].astype(out_dtype)

    in_specs = [pl.BlockSpec((tm, bps * tn), lambda i, j, n: (i, n)),
                pl.BlockSpec((bps, tko, tn), lambda i, j, n: (n // npb, j, n % npb))]
    body, in_specs, args = _ordered(body, in_specs, [a, w], after)
    return pl.pallas_call(
        body, name=name, grid=(m // tm, kdim // tko, nred),
        in_specs=in_specs,
        out_specs=pl.BlockSpec((tm, tko), lambda i, j, n: (i, j)),
        out_shape=jax.ShapeDtypeStruct((m, kdim), out_dtype),
        scratch_shapes=[pltpu.VMEM((tm, tko), F32)] if nred > 1 else [],
        compiler_params=_params(("parallel", "parallel", "arbitrary")),
    )(*args)


def _mm_tn(a, d, nb, name, out_dtype=BF16, after=None):
    m, kdim = a.shape
    ns = d.shape[1] // nb
    tm, tko, tn = _pick(m, 4096), _pick(kdim, 512), _pick(ns, 1536)
    npb, nm = ns // tn, m // tm

    def body(a_ref, d_ref, o_ref, *scratch):
        if nm == 1:
            o_ref[...] = _dot_tn(a_ref[...], d_ref[...]).astype(out_dtype)
            return
        acc = scratch[0]
        r = pl.program_id(2)

        @pl.when(r == 0)
        def _():
            acc[...] = jnp.zeros_like(acc)

        acc[...] += _dot_tn(a_ref[...], d_ref[...])

        @pl.when(r == nm - 1)
        def _():
            o_ref[...] = acc[...].astype(out_dtype)

    in_specs = [pl.BlockSpec((tm, tko), lambda j, i, r: (r, i)), pl.BlockSpec((tm, tn), lambda j, i, r: (r, j))]
    body, in_specs, args = _ordered(body, in_specs, [a, d], after)
    return pl.pallas_call(
        body, name=name, grid=(nb * npb, kdim // tko, nm),
        in_specs=in_specs,
        out_specs=pl.BlockSpec((None, tko, tn), lambda j, i, r: (j // npb, i, j % npb)),
        out_shape=jax.ShapeDtypeStruct((nb, kdim, ns), out_dtype),
        scratch_shapes=[pltpu.VMEM((tko, tn), F32)] if nm > 1 else [],
        compiler_params=_params(("parallel", "parallel", "arbitrary")),
    )(*args)


def _rms_fwd(x, g, name, after=None):
    t, d = x.shape
    tr = _pick(t, 256, SUBLANE)

    def body(x_ref, g_ref, h_ref):
        xv = x_ref[...]
        r = lax.rsqrt(jnp.mean(xv * xv, axis=-1, keepdims=True) + RMS_EPS)
        h_ref[...] = (xv * r * g_ref[...]).astype(BF16)

    in_specs = [pl.BlockSpec((tr, d), lambda i: (i, 0)), pl.BlockSpec((1, d), lambda i: (0, 0))]
    body, in_specs, args = _ordered(body, in_specs, [x, g], after)
    return pl.pallas_call(
        body, name=name, grid=(t // tr,),
        in_specs=in_specs,
        out_specs=pl.BlockSpec((tr, d), lambda i: (i, 0)),
        out_shape=jax.ShapeDtypeStruct((t, d), BF16),
        compiler_params=_params(("parallel",)),
    )(*args)


def _rms_bwd(x, g, dh, add, name, want_bf16, after=None):
    t, d = x.shape
    tr = _pick(t, 256, SUBLANE)

    def body(x_ref, g_ref, dh_ref, add_ref, *outs):
        if want_bf16:
            dx_ref, dxb_ref, dg_ref = outs
        else:
            dx_ref, dg_ref = outs
        i = pl.program_id(0)

        @pl.when(i == 0)
        def _():
            dg_ref[...] = jnp.zeros_like(dg_ref)

        xv, dhv = x_ref[...], dh_ref[...]
        r = lax.rsqrt(jnp.mean(xv * xv, axis=-1, keepdims=True) + RMS_EPS)
        xh = xv * r
        dg_ref[...] += jnp.sum(dhv * xh, axis=0, keepdims=True)
        dxh = dhv * g_ref[...]
        dx = add_ref[...] + r * (dxh - xh * jnp.mean(dxh * xh, axis=-1, keepdims=True))
        dx_ref[...] = dx
        if want_bf16:
            dxb_ref[...] = dx.astype(BF16)

    row = pl.BlockSpec((tr, d), lambda i: (i, 0))
    vec = pl.BlockSpec((1, d), lambda i: (0, 0))
    out_specs = [row] + ([row] if want_bf16 else []) + [vec]
    out_shape = ([jax.ShapeDtypeStruct((t, d), F32)] + ([jax.ShapeDtypeStruct((t, d), BF16)] if want_bf16 else [])
                 + [jax.ShapeDtypeStruct((1, d), F32)])
    body, in_specs, args = _ordered(body, [row, vec, row, row], [x, g, dh, add], after)
    return pl.pallas_call(
        body, name=name, grid=(t // tr,),
        in_specs=in_specs, out_specs=out_specs, out_shape=out_shape,
        compiler_params=_params(("arbitrary",)),
    )(*args)


def _loss_head(x2, g, target, name="loss_head"):
    t, d = x2.shape
    tr = _pick(t, 256, SUBLANE)

    def body(x_ref, g_ref, t_ref, dx_ref, dxb_ref, dg_ref, loss_ref):
        i = pl.program_id(0)

        @pl.when(i == 0)
        def _():
            dg_ref[...] = jnp.zeros_like(dg_ref)
            loss_ref[...] = jnp.zeros_like(loss_ref)

        xv = x_ref[...]
        gv = g_ref[...]
        r = lax.rsqrt(jnp.mean(xv * xv, axis=-1, keepdims=True) + RMS_EPS)
        xh = xv * r
        err = xh * gv - t_ref[...]
        part = 0.5 * jnp.sum(jnp.mean(err * err, axis=-1, keepdims=True), axis=0, keepdims=True)
        loss_ref[...] += jnp.broadcast_to(part, loss_ref.shape)
        dy = err * (1.0 / d)
        dg_ref[...] += jnp.sum(dy * xh, axis=0, keepdims=True)
        dxh = dy * gv
        dx = r * (dxh - xh * jnp.mean(dxh * xh, axis=-1, keepdims=True))
        dx_ref[...] = dx
        dxb_ref[...] = dx.astype(BF16)

    row = pl.BlockSpec((tr, d), lambda i: (i, 0))
    vec = pl.BlockSpec((1, d), lambda i: (0, 0))
    return pl.pallas_call(
        body, name=name, grid=(t // tr,),
        in_specs=[row, vec, row],
        out_specs=[row, row, vec, pl.BlockSpec((1, LANE), lambda i: (0, 0))],
        out_shape=[jax.ShapeDtypeStruct((t, d), F32), jax.ShapeDtypeStruct((t, d), BF16),
                   jax.ShapeDtypeStruct((1, d), F32), jax.ShapeDtypeStruct((1, LANE), F32)],
        compiler_params=_params(("arbitrary",)),
    )(x2, g, target)


def _s5_discretize(a_re, a_im, ldt):
    lam_re = jnp.minimum(a_re, S5_MAX_RE)
    lam_im = a_im
    dt = jnp.exp(ldt)
    mag = jnp.exp(lam_re * dt)
    abar_re = mag * jnp.cos(lam_im * dt)
    abar_im = mag * jnp.sin(lam_im * dt)
    den = lam_re * lam_re + lam_im * lam_im
    nr = abar_re - 1.0
    ni = abar_im
    coef_re = (nr * lam_re + ni * lam_im) / den
    coef_im = (ni * lam_re - nr * lam_im) / den
    return abar_re, abar_im, coef_re, coef_im


def _s5_param_fwd(a_re, a_im, ldt):
    def body(ar_ref, ai_ref, l_ref, o0, o1, o2, o3):
        outs = _s5_discretize(ar_ref[...], ai_ref[...], l_ref[...])
        for o, v in zip((o0, o1, o2, o3), outs):
            o[...] = v

    sh = jax.ShapeDtypeStruct(a_re.shape, F32)
    return pl.pallas_call(body, name="s5_param_fwd", out_shape=[sh, sh, sh, sh], compiler_params=_params())(a_re, a_im, ldt)


def _s5_param_bwd(a_re, a_im, ldt, cts):
    def body(ar_ref, ai_ref, l_ref, c0, c1, c2, c3, g0, g1, g2):
        _, vjp = jax.vjp(_s5_discretize, ar_ref[...], ai_ref[...], l_ref[...])
        ga, gb, gl = vjp((c0[...], c1[...], c2[...], c3[...]))
        g0[...] = ga
        g1[...] = gb
        g2[...] = gl

    sh = jax.ShapeDtypeStruct(a_re.shape, F32)
    return pl.pallas_call(body, name="s5_param_bwd", out_shape=[sh, sh, jax.ShapeDtypeStruct(ldt.shape, F32)],
                          compiler_params=_params())(a_re, a_im, ldt, *cts)


def _cmul(ar, ai, br, bi):
    return ar * br - ai * bi, ar * bi + ai * br


S5_TC = 128
S5_TILE = S5_SUPER * SUBLANE
S5_HALF = S5_TILE // 2


def _s5_to_tile(re, im):
    f = lambda a: a.reshape(S5_SUPER, S5_LANES // LANE, LANE).transpose(1, 0, 2).reshape(S5_HALF, LANE)
    return jnp.concatenate([f(re), f(im)], axis=0)


def _s5_from_tile(tile):
    f = lambda a: a.reshape(S5_LANES // LANE, S5_SUPER, LANE).transpose(1, 0, 2).reshape(S5_GROUPS, S5_STATE)
    return f(tile[0:S5_HALF]), f(tile[S5_HALF:])


RE = slice(0, S5_HALF)
IM = slice(S5_HALF, S5_TILE)


def _s5_scatter_rows(buf, rows, first_tile=0):
    tc = rows[0].shape[0]
    for j in range(SUBLANE):
        stacked = jnp.stack([r[:, j * LANE:(j + 1) * LANE] for r in rows], axis=0)
        buf[first_tile:first_tile + tc, j * SUBLANE:(j + 1) * SUBLANE, :] = jnp.swapaxes(stacked, 0, 1)


def _s5_gather_rows(buf, tc, first_tile=0):
    per_j = [jnp.swapaxes(buf[first_tile:first_tile + tc, j * SUBLANE:(j + 1) * SUBLANE, :], 0, 1)
             for j in range(SUBLANE)]
    return [jnp.concatenate([per_j[j][k] for j in range(SUBLANE)], axis=1) for k in range(S5_SUPER)]


def _s5_fwd(proj, bsg, ccat, d_row, abar_t, coef_t):
    t = proj.shape[0]
    tc = min(t, S5_TC)
    n_chunks = t // tc

    def body(u_ref, b_ref, c_ref, d_ref, a_ref, cf_ref, y_ref, sb_ref, x, car):
        @pl.when(pl.program_id(0) == 0)
        def _():
            car[...] = jnp.zeros_like(car)

        sb_ref[...] = car[...]
        u = u_ref[...]
        _s5_scatter_rows(x, [_dot(u[:, k * LANE:(k + 1) * LANE].astype(BF16), b_ref[k]) for k in range(S5_SUPER)])
        ar, ai = a_ref[RE, :], a_ref[IM, :]
        cr, ci = cf_ref[RE, :], cf_ref[IM, :]

        def step(i, carry):
            sr, si = carry
            xr, xi = _cmul(cr, ci, x[i, RE, :], x[i, IM, :])
            sr, si = ar * sr - ai * si + xr, ar * si + ai * sr + xi
            x[i, RE, :] = sr
            x[i, IM, :] = si
            return sr, si

        sr, si = lax.fori_loop(0, tc, step, (car[RE, :], car[IM, :]), unroll=4)
        car[RE, :] = sr
        car[IM, :] = si
        for k, s_k in enumerate(_s5_gather_rows(x, tc)):
            cols = slice(k * LANE, (k + 1) * LANE)
            y_ref[:, cols] = _dot(s_k.astype(BF16), c_ref[k]) + d_ref[:, cols] * u[:, cols]

    full = lambda shape: pl.BlockSpec(shape, lambda c: (0,) * len(shape))
    return pl.pallas_call(
        body, name="s5_fwd", grid=(n_chunks,),
        in_specs=[pl.BlockSpec((tc, S5_WIDTH), lambda c: (c, 0)), full(bsg.shape), full(ccat.shape), full(d_row.shape),
                  full(abar_t.shape), full(coef_t.shape)],
        out_specs=[pl.BlockSpec((tc, S5_WIDTH), lambda c: (c, 0)), pl.BlockSpec((None, S5_TILE, LANE), lambda c: (c, 0, 0))],
        out_shape=[jax.ShapeDtypeStruct((t, S5_WIDTH), F32), jax.ShapeDtypeStruct((n_chunks, S5_TILE, LANE), F32)],
        scratch_shapes=[pltpu.VMEM((tc, S5_TILE, LANE), F32), pltpu.VMEM((S5_TILE, LANE), F32)],
        compiler_params=_params(("arbitrary",)),
    )(proj, bsg, ccat, d_row, abar_t, coef_t)


def _s5_bwd(proj, dy, sb, bsg, ccat, d_row, abar_t, coef_t):
    t = proj.shape[0]
    tc = min(t, S5_TC)
    n_chunks = t // tc
    last = n_chunks - 1

    def body(u_ref, dy_ref, sb_ref, b_ref, c_ref, d_ref, a_ref, cf_ref,
             du_ref, gb_ref, gc_ref, gd_ref, ga_ref, gcf_ref, xb, xs, xg, gcar, acc):
        @pl.when(pl.program_id(0) == 0)
        def _():
            gcar[...] = jnp.zeros_like(gcar)
            acc[...] = jnp.zeros_like(acc)
            gb_ref[...] = jnp.zeros_like(gb_ref)
            gc_ref[...] = jnp.zeros_like(gc_ref)
            gd_ref[...] = jnp.zeros_like(gd_ref)

        u = u_ref[...]
        dyv = dy_ref[...]
        u16, dy16 = u.astype(BF16), dyv.astype(BF16)
        subs = [slice(k * LANE, (k + 1) * LANE) for k in range(S5_SUPER)]
        _s5_scatter_rows(xb, [_dot(u16[:, c], b_ref[k]) for k, c in enumerate(subs)])
        _s5_scatter_rows(xg, [_dot_nt(dy16[:, c], c_ref[k]) for k, c in enumerate(subs)])
        ar, ai = a_ref[RE, :], a_ref[IM, :]
        cr, ci = cf_ref[RE, :], cf_ref[IM, :]

        xs[0] = sb_ref[...]

        def fstep(i, carry):
            sr, si = carry
            xr, xi = _cmul(cr, ci, xb[i, RE, :], xb[i, IM, :])
            sr, si = ar * sr - ai * si + xr, ar * si + ai * sr + xi
            xs[i + 1, RE, :] = sr
            xs[i + 1, IM, :] = si
            return sr, si

        lax.fori_loop(0, tc, fstep, (sb_ref[RE, :], sb_ref[IM, :]), unroll=4)

        def rstep(n, carry):
            gr, gi, a0, a1, a2, a3 = carry
            i = tc - 1 - n
            xr = xg[i, RE, :] + ar * gr + ai * gi
            xi = xg[i, IM, :] + ar * gi - ai * gr
            pr, pi = xs[i, RE, :], xs[i, IM, :]
            br, bi = xb[i, RE, :], xb[i, IM, :]
            a0 = a0 + pr * xr + pi * xi
            a1 = a1 + pr * xi - pi * xr
            a2 = a2 + br * xr + bi * xi
            a3 = a3 + br * xi - bi * xr
            xg[i, RE, :] = cr * xr + ci * xi
            xg[i, IM, :] = cr * xi - ci * xr
            return xr, xi, a0, a1, a2, a3

        init = (gcar[RE, :], gcar[IM, :], acc[0], acc[1], acc[2], acc[3])
        gr, gi, a0, a1, a2, a3 = lax.fori_loop(0, tc, rstep, init, unroll=2)
        gcar[RE, :] = gr
        gcar[IM, :] = gi
        for idx, a in enumerate((a0, a1, a2, a3)):
            acc[idx] = a
        ga_ref[RE, :] = a0
        ga_ref[IM, :] = a1
        gcf_ref[RE, :] = a2
        gcf_ref[IM, :] = a3

        g_rows = _s5_gather_rows(xg, tc)
        s_rows = _s5_gather_rows(xs, tc, first_tile=1)
        for k in range(S5_SUPER):
            cols = subs[k]
            g16 = g_rows[k].astype(BF16)
            s16 = s_rows[k].astype(BF16)
            gb_ref[k] += _dot_tn(u16[:, cols], g16)
            gc_ref[k] += _dot_tn(s16, dy16[:, cols])
            du_ref[:, cols] = (_dot_nt(g16, b_ref[k]) + d_ref[:, cols] * dyv[:, cols]).astype(BF16)
        gd_ref[...] += jnp.sum(dyv * u, axis=0, keepdims=True)

    full = lambda shape: pl.BlockSpec(shape, lambda c: (0,) * len(shape))
    rows = pl.BlockSpec((tc, S5_WIDTH), lambda c: (last - c, 0))
    tile = (S5_TILE, LANE)
    return pl.pallas_call(
        body, name="s5_bwd", grid=(n_chunks,),
        in_specs=[rows, rows, pl.BlockSpec((None, S5_TILE, LANE), lambda c: (last - c, 0, 0)),
                  full(bsg.shape), full(ccat.shape), full(d_row.shape), full(abar_t.shape), full(coef_t.shape)],
        out_specs=[rows, full(bsg.shape), full(ccat.shape), full(d_row.shape), full(tile), full(tile)],
        out_shape=[jax.ShapeDtypeStruct((t, S5_WIDTH), BF16), jax.ShapeDtypeStruct(bsg.shape, F32),
                   jax.ShapeDtypeStruct(ccat.shape, F32), jax.ShapeDtypeStruct(d_row.shape, F32),
                   jax.ShapeDtypeStruct(tile, F32), jax.ShapeDtypeStruct(tile, F32)],
        scratch_shapes=[pltpu.VMEM((tc, S5_TILE, LANE), F32), pltpu.VMEM((tc + 1, S5_TILE, LANE), F32),
                        pltpu.VMEM((tc, S5_TILE, LANE), F32), pltpu.VMEM(tile, F32),
                        pltpu.VMEM((4, S5_HALF, LANE), F32)],
        compiler_params=_params(("arbitrary",)),
    )(proj, dy, sb, bsg, ccat, d_row, abar_t, coef_t)


def _gelu_fwd(y, name="s5_gelu"):
    t, w = y.shape
    tr = _pick(t, 512, SUBLANE)

    def body(y_ref, z_ref):
        z_ref[...] = _gelu_and_grad(y_ref[...])[0].astype(BF16)

    row = pl.BlockSpec((tr, w), lambda i: (i, 0))
    return pl.pallas_call(body, name=name, grid=(t // tr,), in_specs=[row], out_specs=row,
                          out_shape=jax.ShapeDtypeStruct((t, w), BF16), compiler_params=_params(("parallel",)))(y)


def _glu_fwd(y, gl, b, name="s5_glu"):
    t, w = y.shape
    tr = _pick(t, 512, SUBLANE)

    def body(y_ref, gl_ref, b_ref, z2_ref):
        z = _gelu_and_grad(y_ref[...])[0]
        z2_ref[...] = (z * _sigmoid(gl_ref[...] + b_ref[...])).astype(BF16)

    row = pl.BlockSpec((tr, w), lambda i: (i, 0))
    return pl.pallas_call(body, name=name, grid=(t // tr,),
                          in_specs=[row, row, pl.BlockSpec((1, w), lambda i: (0, 0))], out_specs=row,
                          out_shape=jax.ShapeDtypeStruct((t, w), BF16), compiler_params=_params(("parallel",)))(y, gl, b)


def _glu_bwd(y, gl, b, dz2, name="s5_glu_bwd", after=None):
    t, w = y.shape
    tr = _pick(t, 512, SUBLANE)

    def body(y_ref, gl_ref, b_ref, dz2_ref, dgl_ref, dza_ref, db_ref):
        @pl.when(pl.program_id(0) == 0)
        def _():
            db_ref[...] = jnp.zeros_like(db_ref)

        z = _gelu_and_grad(y_ref[...])[0]
        s = _sigmoid(gl_ref[...] + b_ref[...])
        dz2v = dz2_ref[...]
        dgl = dz2v * z * s * (1.0 - s)
        dgl_ref[...] = dgl.astype(BF16)
        dza_ref[...] = dz2v * s
        db_ref[...] += jnp.sum(dgl, axis=0, keepdims=True)

    row = pl.BlockSpec((tr, w), lambda i: (i, 0))
    vec = pl.BlockSpec((1, w), lambda i: (0, 0))
    body, in_specs, args = _ordered(body, [row, row, vec, row], [y, gl, b, dz2], after)
    return pl.pallas_call(body, name=name, grid=(t // tr,), in_specs=in_specs, out_specs=[row, row, vec],
                          out_shape=[jax.ShapeDtypeStruct((t, w), BF16), jax.ShapeDtypeStruct((t, w), F32),
                                     jax.ShapeDtypeStruct((1, w), F32)],
                          compiler_params=_params(("arbitrary",)))(*args)


def _gelu_bwd(y, dza, dzb, name="s5_gelu_bwd", after=None):
    t, w = y.shape
    tr = _pick(t, 512, SUBLANE)

    def body(y_ref, a_ref, b_ref, dy_ref):
        dy_ref[...] = (a_ref[...] + b_ref[...]) * _gelu_and_grad(y_ref[...])[1]

    row = pl.BlockSpec((tr, w), lambda i: (i, 0))
    body, in_specs, args = _ordered(body, [row, row, row], [y, dza, dzb], after)
    return pl.pallas_call(body, name=name, grid=(t // tr,), in_specs=in_specs, out_specs=row,
                          out_shape=jax.ShapeDtypeStruct((t, w), F32), compiler_params=_params(("parallel",)))(*args)


def _tri_dot(tri16, x):
    hi = x.astype(BF16)
    lo = (x - hi.astype(F32)).astype(BF16)
    return _dot(tri16, hi) + _dot(tri16, lo)


def _hgrn_pre(q_in, z, lg):
    lb = _sigmoid(lg[0:1, :] - lg[1:2, :])
    qs, dqs = _silu_and_grad(q_in)
    sz = _sigmoid(z)
    f = lb + (1.0 - lb) * sz
    k = (1.0 - lb) * (1.0 - sz)
    c = HGRN_CHUNK
    r = lax.broadcasted_iota(jnp.int32, (c, c), 0)
    s = lax.broadcasted_iota(jnp.int32, (c, c), 1)
    causal = r >= s
    b = _tri_dot(jnp.where(causal, 1.0, 0.0).astype(BF16), jnp.log(f))
    b_end = b[c - 1:c, :]
    b_mid = b[c // 2 - 1:c // 2, :]
    e_q, e_k, e_0, e_c = jnp.exp(b - b_mid), jnp.exp(b_mid - b), jnp.exp(b), jnp.exp(b_end - b)
    return dict(lb=lb, qs=qs, dqs=dqs, sz=sz, f=f, k=k, causal=causal, b_end=b_end,
                e_q=e_q, e_k=e_k, e_0=e_0, e_c=e_c,
                qt=qs * e_q, kt=k * e_k, q0=qs * e_0, kc=k * e_c)


def _hgrn_fwd(proj, logits, ng):
    t = proj.shape[0]
    c, dh = HGRN_CHUNK, HGRN_DH
    n_chunks = t // c
    subs = HGRN_SUBS if n_chunks % HGRN_SUBS == 0 else 1

    def head(h, sub, q_ref, z_ref, v_ref, g_ref, lg_ref, ng_ref, o_ref, oh_ref, s0_ref, st):
        sl = slice(h * dh, (h + 1) * dh)
        rs = slice(sub * c, (sub + 1) * c)
        s0 = st[h]
        s0_ref[h, sub] = s0
        p = _hgrn_pre(q_ref[rs, sl], z_ref[rs, sl], lg_ref[:, sl])
        v16 = v_ref[rs, sl].astype(BF16)
        a = jnp.where(p["causal"], _dot_nt(p["qt"].astype(BF16), p["kt"].astype(BF16)), 0.0)
        o = _dot_nt(p["q0"].astype(BF16), s0.astype(BF16)) + _dot(a.astype(BF16), v16)
        st[h] = jnp.exp(p["b_end"]) * s0 + _dot_tn(v16, p["kc"].astype(BF16))
        o_ref[rs, sl] = o
        rn = lax.rsqrt(jnp.mean(o * o, axis=-1, keepdims=True) + RMS_EPS)
        oh_ref[rs, sl] = (o * rn * ng_ref[:, sl] * _silu_and_grad(g_ref[rs, sl])[0]).astype(BF16)

    def body(*refs):
        st = refs[-1]

        @pl.when(pl.program_id(0) == 0)
        def _():
            st[...] = jnp.zeros_like(st)

        for sub in range(subs):
            for h in range(HGRN_HEADS):
                head(h, sub, *refs)

    def wide(off):
        return pl.BlockSpec((subs * c, HGRN_WIDTH), lambda i: (i, off))

    return pl.pallas_call(
        body, name="hgrn_fwd", grid=(n_chunks // subs,),
        in_specs=[wide(1), wide(2), wide(3), wide(4),
                  pl.BlockSpec((2, HGRN_WIDTH), lambda i: (0, 0)), pl.BlockSpec((1, HGRN_WIDTH), lambda i: (0, 0))],
        out_specs=[wide(0), wide(0), pl.BlockSpec((HGRN_HEADS, subs, dh, dh), lambda i: (0, i, 0, 0))],
        out_shape=[jax.ShapeDtypeStruct((t, HGRN_WIDTH), F32), jax.ShapeDtypeStruct((t, HGRN_WIDTH), BF16),
                   jax.ShapeDtypeStruct((HGRN_HEADS, n_chunks, dh, dh), F32)],
        scratch_shapes=[pltpu.VMEM((HGRN_HEADS, dh, dh), F32)],
        compiler_params=_params(("arbitrary",)),
    )(proj, proj, proj, proj, logits, ng)


def _hgrn_bwd(proj, o_raw, s0s, doh, logits, ng):
    t = proj.shape[0]
    c, dh = HGRN_CHUNK, HGRN_DH
    n_chunks = t // c
    subs = HGRN_SUBS if n_chunks % HGRN_SUBS == 0 else 1
    last = n_chunks // subs - 1

    def head(h, sub, q_ref, z_ref, v_ref, g_ref, o_ref, s0_ref, doh_ref, lg_ref, ng_ref,
             dq_ref, dz_ref, dv_ref, dg_ref, dng_ref, dlb_ref, dst):
        sl = slice(h * dh, (h + 1) * dh)
        rs = slice(sub * c, (sub + 1) * c)
        p = _hgrn_pre(q_ref[rs, sl], z_ref[rs, sl], lg_ref[:, sl])
        v = v_ref[rs, sl]
        v16 = v.astype(BF16)
        s0 = s0_ref[h, sub]
        ds_end = dst[h]
        ds16 = ds_end.astype(BF16)
        ngv = ng_ref[:, sl]

        o = o_ref[rs, sl]
        dohv = doh_ref[rs, sl]
        sg, dsg = _silu_and_grad(g_ref[rs, sl])
        rn = lax.rsqrt(jnp.mean(o * o, axis=-1, keepdims=True) + RMS_EPS)
        oh = o * rn
        dg_ref[rs, sl] = (dohv * oh * ngv * dsg).astype(BF16)
        don = dohv * sg
        dng_ref[:, sl] += jnp.sum(don * oh, axis=0, keepdims=True)
        doh_n = don * ngv
        do = rn * (doh_n - oh * jnp.mean(doh_n * oh, axis=-1, keepdims=True))
        do16 = do.astype(BF16)

        qt16, kt16, q016, kc16 = (p[n].astype(BF16) for n in ("qt", "kt", "q0", "kc"))
        a = jnp.where(p["causal"], _dot_nt(qt16, kt16), 0.0)
        da = jnp.where(p["causal"], _dot_nt(do16, v16), 0.0)
        da16 = da.astype(BF16)
        dqt = _dot(da16, kt16)
        dq0 = _dot(do16, s0.astype(BF16))
        dkt = _dot_tn(da16, qt16)
        dkc = _dot(v16, ds16)
        dv_ref[rs, sl] = (_dot_tn(a.astype(BF16), do16) + _dot_nt(kc16, ds16)).astype(BF16)
        lam_end = jnp.exp(p["b_end"])
        dst[h] = lam_end * ds_end + _dot_tn(do16, q016)

        qt, kt, q0, kc = (a.astype(F32) for a in (qt16, kt16, q016, kc16))
        db = dqt * qt + dq0 * q0 - dkt * kt - dkc * kc
        db_end = (jnp.sum(dkc * kc, axis=0, keepdims=True)
                  + jnp.sum(ds_end * s0, axis=0, keepdims=True) * lam_end)
        rowi = lax.broadcasted_iota(jnp.int32, (c, dh), 0)
        db = db + jnp.where(rowi == c - 1, db_end, 0.0)
        r = lax.broadcasted_iota(jnp.int32, (c, c), 0)
        s = lax.broadcasted_iota(jnp.int32, (c, c), 1)
        dlf = _tri_dot(jnp.where(s >= r, 1.0, 0.0).astype(BF16), db)

        dqs = dqt * p["e_q"] + dq0 * p["e_0"]
        dq_ref[rs, sl] = (dqs * p["dqs"]).astype(BF16)
        dk = dkt * p["e_k"] + dkc * p["e_c"]
        sz, lb = p["sz"], p["lb"]
        common = dlf / p["f"] - dk
        dz_ref[rs, sl] = ((1.0 - lb) * sz * (1.0 - sz) * common).astype(BF16)
        dlb_ref[:, sl] += jnp.sum((1.0 - sz) * common, axis=0, keepdims=True)

    def body(*refs):
        dng_ref, dlb_ref, dst = refs[-3:]

        @pl.when(pl.program_id(0) == 0)
        def _():
            dst[...] = jnp.zeros_like(dst)
            dng_ref[...] = jnp.zeros_like(dng_ref)
            dlb_ref[...] = jnp.zeros_like(dlb_ref)

        for sub in reversed(range(subs)):
            for h in range(HGRN_HEADS):
                head(h, sub, *refs)

    def wide(off):
        return pl.BlockSpec((subs * c, HGRN_WIDTH), lambda i: (last - i, off))

    vec = pl.BlockSpec((1, HGRN_WIDTH), lambda i: (0, 0))
    act = jax.ShapeDtypeStruct((t, HGRN_WIDTH), BF16)
    vsh = jax.ShapeDtypeStruct((1, HGRN_WIDTH), F32)
    return pl.pallas_call(
        body, name="hgrn_bwd", grid=(n_chunks // subs,),
        in_specs=[wide(1), wide(2), wide(3), wide(4), wide(0),
                  pl.BlockSpec((HGRN_HEADS, subs, dh, dh), lambda i: (0, last - i, 0, 0)),
                  wide(0), pl.BlockSpec((2, HGRN_WIDTH), lambda i: (0, 0)), vec],
        out_specs=[wide(0), wide(0), wide(0), wide(0), vec, vec],
        out_shape=[act, act, act, act, vsh, vsh],
        scratch_shapes=[pltpu.VMEM((HGRN_HEADS, dh, dh), F32)],
        compiler_params=_params(("arbitrary",)),
    )(proj, proj, proj, proj, o_raw, s0s, doh, logits, ng)


def _lb_bwd(logits, dlb):
    def body(lg_ref, d_ref, o_ref):
        lg = lg_ref[...]
        lb = _sigmoid(lg[0:1, :] - lg[1:2, :])
        g = d_ref[...] * lb * (1.0 - lb)
        o_ref[0:1, :] = g
        o_ref[1:2, :] = -g

    return pl.pallas_call(body, name="hgrn_lb_bwd", out_shape=jax.ShapeDtypeStruct(logits.shape, F32),
                          compiler_params=_params())(logits, dlb)


MERGE_TC = 1024
GS_BLOCK = (S5_WIDTH + 4 * HGRN_WIDTH) // MERGE_TC
GH_BLOCK = GS_BLOCK + D_MODEL // MERGE_TC


def _merge_fwd(proj, ys, yh):
    t = proj.shape[0]
    tr = _pick(t, 256, SUBLANE)

    def body(gs_ref, gh_ref, ys_ref, yh_ref, m_ref):
        m_ref[...] = (_sigmoid(gs_ref[...]) * ys_ref[...] + _sigmoid(gh_ref[...]) * yh_ref[...]).astype(BF16)

    blk = pl.BlockSpec((tr, MERGE_TC), lambda i, j: (i, j))
    return pl.pallas_call(
        body, name="merge_fwd", grid=(t // tr, D_MODEL // MERGE_TC),
        in_specs=[pl.BlockSpec((tr, MERGE_TC), lambda i, j: (i, GS_BLOCK + j)),
                  pl.BlockSpec((tr, MERGE_TC), lambda i, j: (i, GH_BLOCK + j)), blk, blk],
        out_specs=blk, out_shape=jax.ShapeDtypeStruct((t, D_MODEL), BF16),
        compiler_params=_params(("parallel", "parallel")),
    )(proj, proj, ys, yh)


def _merge_bwd(proj, ys, yh, dm, after=None):
    t = proj.shape[0]
    tr = _pick(t, 256, SUBLANE)

    def body(gs_ref, gh_ref, ys_ref, yh_ref, dm_ref, dys_ref, dyh_ref, dgs_ref, dgh_ref):
        dmv = dm_ref[...]
        ss, sh = _sigmoid(gs_ref[...]), _sigmoid(gh_ref[...])
        dys_ref[...] = (dmv * ss).astype(BF16)
        dyh_ref[...] = (dmv * sh).astype(BF16)
        dgs_ref[...] = (dmv * ys_ref[...] * ss * (1.0 - ss)).astype(BF16)
        dgh_ref[...] = (dmv * yh_ref[...] * sh * (1.0 - sh)).astype(BF16)

    blk = pl.BlockSpec((tr, MERGE_TC), lambda i, j: (i, j))
    sh16 = jax.ShapeDtypeStruct((t, D_MODEL), BF16)
    in_specs = [pl.BlockSpec((tr, MERGE_TC), lambda i, j: (i, GS_BLOCK + j)),
                pl.BlockSpec((tr, MERGE_TC), lambda i, j: (i, GH_BLOCK + j)), blk, blk, blk]
    body, in_specs, args = _ordered(body, in_specs, [proj, proj, ys, yh, dm], after)
    return pl.pallas_call(
        body, name="merge_bwd", grid=(t // tr, D_MODEL // MERGE_TC),
        in_specs=in_specs,
        out_specs=[blk, blk, blk, blk], out_shape=[sh16, sh16, sh16, sh16],
        compiler_params=_params(("parallel", "parallel")),
    )(*args)


FFN_TC = 128
FFN_ROWS = 128
HALO = SUBLANE


def _pad_rows(dst, src_ref):
    t, c = src_ref.shape
    dst[0:HALO, :] = jnp.zeros((HALO, c), F32)
    dst[HALO:HALO + t, :] = src_ref[...]
    dst[HALO + t:HALO + t + HALO, :] = jnp.zeros((HALO, c), F32)


def _conv3(padded, w, b, r0, nrows):
    x0 = padded[HALO + r0:HALO + r0 + nrows, :]
    x1 = padded[HALO + r0 - 1:HALO + r0 - 1 + nrows, :]
    x2 = padded[HALO + r0 - 2:HALO + r0 - 2 + nrows, :]
    return b + w[0:1, :] * x2 + w[1:2, :] * x1 + w[2:3, :] * x0, (x0, x1, x2)


def _ffn_act_fwd(up, cw, cb):
    t = up.shape[0]
    rows = _pick(t, FFN_ROWS, SUBLANE)
    nvb = D_FF // FFN_TC

    def body(ug_ref, uv_ref, wg_ref, wv_ref, bg_ref, bv_ref, act_ref, pg, pv):
        wg, wv, bg, bv = wg_ref[...], wv_ref[...], bg_ref[...], bv_ref[...]
        _pad_rows(pg, ug_ref)
        _pad_rows(pv, uv_ref)
        for r0 in range(0, t, rows):
            cg, _ = _conv3(pg, wg, bg, r0, rows)
            cv, _ = _conv3(pv, wv, bv, r0, rows)
            act_ref[r0:r0 + rows, :] = (_silu_and_grad(cg)[0] * cv).astype(BF16)

    def colblk(nrow, off):
        return pl.BlockSpec((nrow, FFN_TC), lambda j: (0, off + j))

    return pl.pallas_call(
        body, name="ffn_act_fwd", grid=(nvb,),
        in_specs=[colblk(t, 0), colblk(t, nvb), colblk(3, 0), colblk(3, nvb), colblk(1, 0), colblk(1, nvb)],
        out_specs=colblk(t, 0), out_shape=jax.ShapeDtypeStruct((t, D_FF), BF16),
        scratch_shapes=[pltpu.VMEM((t + 2 * HALO, FFN_TC), F32), pltpu.VMEM((t + 2 * HALO, FFN_TC), F32)],
        compiler_params=_params(("parallel",)),
    )(up, up, cw, cw, cb, cb)


def _ffn_act_bwd(up, dact, cw, cb, after=None):
    t = up.shape[0]
    rows = _pick(t, FFN_ROWS, SUBLANE)
    nvb = D_FF // FFN_TC

    def body(ug_ref, uv_ref, da_ref, wg_ref, wv_ref, bg_ref, bv_ref,
             dug_ref, duv_ref, dwg_ref, dwv_ref, dbg_ref, dbv_ref, pg, pv, dcs):
        wg, wv, bg, bv = wg_ref[...], wv_ref[...], bg_ref[...], bv_ref[...]
        _pad_rows(pg, ug_ref)
        _pad_rows(pv, uv_ref)
        ext = rows + HALO
        acc_g = [jnp.zeros((1, FFN_TC), F32) for _ in range(4)]
        acc_v = [jnp.zeros((1, FFN_TC), F32) for _ in range(4)]
        for r0 in range(0, t, rows):
            cg, xg = _conv3(pg, wg, bg, r0, ext)
            cv, xv = _conv3(pv, wv, bv, r0, ext)
            if r0 + ext <= t:
                dav = da_ref[r0:r0 + ext, :]
            else:
                dav = jnp.concatenate([da_ref[r0:t, :], jnp.zeros((HALO, FFN_TC), F32)], axis=0)
            sg, dsg = _silu_and_grad(cg)
            for h, (dconv, xs, w, acc, out) in enumerate(((dav * cv * dsg, xg, wg, acc_g, dug_ref),
                                                           (dav * sg, xv, wv, acc_v, duv_ref))):
                dcs[h] = dconv
                d0 = dconv[0:rows, :]
                d1 = dcs[h, 1:rows + 1, :]
                d2 = dcs[h, 2:rows + 2, :]
                out[r0:r0 + rows, :] = (w[2:3, :] * d0 + w[1:2, :] * d1 + w[0:1, :] * d2).astype(BF16)
                x0, x1, x2 = xs
                acc[0] = acc[0] + jnp.sum(d0 * x2[0:rows, :], axis=0, keepdims=True)
                acc[1] = acc[1] + jnp.sum(d0 * x1[0:rows, :], axis=0, keepdims=True)
                acc[2] = acc[2] + jnp.sum(d0 * x0[0:rows, :], axis=0, keepdims=True)
                acc[3] = acc[3] + jnp.sum(d0, axis=0, keepdims=True)
        for acc, dw_ref, db_ref in ((acc_g, dwg_ref, dbg_ref), (acc_v, dwv_ref, dbv_ref)):
            dw_ref[0:1, :] = acc[0]
            dw_ref[1:2, :] = acc[1]
            dw_ref[2:3, :] = acc[2]
            db_ref[...] = acc[3]

    def colblk(nrow, off):
        return pl.BlockSpec((nrow, FFN_TC), lambda j: (0, off + j))

    in_specs = [colblk(t, 0), colblk(t, nvb), colblk(t, 0), colblk(3, 0), colblk(3, nvb), colblk(1, 0), colblk(1, nvb)]
    body, in_specs, args = _ordered(body, in_specs, [up, up, dact, cw, cw, cb, cb], after)
    return pl.pallas_call(
        body, name="ffn_act_bwd", grid=(nvb,),
        in_specs=in_specs,
        out_specs=[colblk(t, 0), colblk(t, 0), colblk(3, 0), colblk(3, 0), colblk(1, 0), colblk(1, 0)],
        out_shape=[jax.ShapeDtypeStruct((t, D_FF), BF16), jax.ShapeDtypeStruct((t, D_FF), BF16),
                   jax.ShapeDtypeStruct((3, D_FF), F32), jax.ShapeDtypeStruct((3, D_FF), F32),
                   jax.ShapeDtypeStruct((1, D_FF), F32), jax.ShapeDtypeStruct((1, D_FF), F32)],
        scratch_shapes=[pltpu.VMEM((t + 2 * HALO, FFN_TC), F32), pltpu.VMEM((t + 2 * HALO, FFN_TC), F32),
                        pltpu.VMEM((2, rows + HALO, FFN_TC), F32)],
        compiler_params=_params(("parallel",)),
    )(*args)


def _all_gather(shards, name):
    nw = len(shards)

    def body(*refs):
        x_refs, out_refs = refs[:nw], refs[nw:2 * nw]
        send_sems, recv_sems, local_sems = refs[2 * nw:]
        x, y, c = lax.axis_index("x"), lax.axis_index("y"), lax.axis_index("c")
        me, sibling = (x, y, c), (x, y, 1 - c)
        chips = [(1 - x, y), (x, 1 - y), (1 - x, 1 - y)]

        def copy(w, k, block, to, src=None):
            slot = out_refs[w].at[4 * block[0] + 2 * block[1] + block[2]]
            return pltpu.make_async_remote_copy(
                src_ref=slot if src is None else src, dst_ref=slot,
                send_sem=send_sems.at[w, k], recv_sem=recv_sems.at[w, k],
                device_id=to, device_id_type=MESH)

        mine, first, passed = [], [], []
        for w in range(nw):
            cp = pltpu.make_async_copy(x_refs[w], out_refs[w].at[4 * x + 2 * y + c], local_sems.at[w])
            cp.start()
            mine.append(cp)
            first.append(copy(w, 0, me, sibling, src=x_refs[w]))
            first += [copy(w, 1 + j, me, (*chip, c), src=x_refs[w]) for j, chip in enumerate(chips)]
        for cp in first:
            cp.start()
        for w in range(nw):
            for j, chip in enumerate(chips):
                copy(w, 1 + j, (*chip, c), me).wait_recv()
                fwd = copy(w, 4 + j, (*chip, c), sibling)
                fwd.start()
                passed.append(fwd)
        for w in range(nw):
            copy(w, 0, sibling, me).wait_recv()
            for j, chip in enumerate(chips):
                copy(w, 4 + j, (*chip, 1 - c), me).wait_recv()
        for cp in first + passed:
            cp.wait_send()
        for cp in mine:
            cp.wait()

    anyspec = pl.BlockSpec(memory_space=pl.ANY)
    return pl.pallas_call(
        body, name=name,
        in_specs=[anyspec] * nw, out_specs=[anyspec] * nw,
        out_shape=[jax.ShapeDtypeStruct((N_DEV,) + s.shape, s.dtype) for s in shards],
        scratch_shapes=[pltpu.SemaphoreType.DMA((nw, 7)), pltpu.SemaphoreType.DMA((nw, 7)),
                        pltpu.SemaphoreType.DMA((nw,))],
    )(*shards)


HBM_SPEC = pl.BlockSpec(memory_space=pltpu.HBM)
SEM_SPEC = pl.BlockSpec(memory_space=pltpu.SEMAPHORE)
ANY_SPEC = pl.BlockSpec(memory_space=pl.ANY)
DATAFLOW = pltpu.SideEffectType.DATAFLOW_SIDE_EFFECTING


def _my_index():
    return 4 * lax.axis_index("x") + 2 * lax.axis_index("y") + lax.axis_index("c")


def _peers():
    x, y, c = lax.axis_index("x"), lax.axis_index("y"), lax.axis_index("c")
    peers = []
    for k in range(1, N_DEV):
        px = 1 - x if k & 4 else x
        py = 1 - y if k & 2 else y
        pc = 1 - c if k & 1 else c
        peers.append((k, (px, py, pc), 4 * px + 2 * py + pc))
    return peers


def _split_copy(src_ref, land_ref, send_sems, recv_sems, w, k, peer, slot, scatter, outgoing):
    return pltpu.make_async_remote_copy(
        src_ref=src_ref.at[slot] if scatter else src_ref,
        dst_ref=land_ref.at[_my_index() if outgoing else slot],
        send_sem=send_sems.at[w * (N_DEV - 1) + k - 1], recv_sem=recv_sems.at[w * (N_DEV - 1) + k - 1],
        device_id=peer, device_id_type=MESH)


def _landing_zone(src, scatter):
    me = _my_index()
    own = lax.dynamic_index_in_dim(src, me, 0, keepdims=True) if scatter else src[None]
    shape = src.shape if scatter else (N_DEV,) + src.shape
    return lax.dynamic_update_slice_in_dim(lax.empty(shape, src.dtype), own, me, 0)


def _exchange_start(srcs, scatter, after, name, lands=None):
    nw = len(srcs)
    if lands is None:
        lands = [_landing_zone(s, scatter) for s in srcs]

    afters = [] if after is None else [after]

    def body(*refs):
        s_refs, l_refs = refs[:nw], refs[nw:2 * nw]
        send_sems, recv_sems = refs[2 * nw + len(afters)], refs[2 * nw + len(afters) + 1]
        token = refs[-1]
        for w in range(nw):
            for k, peer, slot in _peers():
                _split_copy(s_refs[w], l_refs[w], send_sems, recv_sems, w, k, peer, slot, scatter, True).start()
        token[...] = jnp.zeros_like(token)

    sems = pltpu.SemaphoreType.DMA((nw * (N_DEV - 1),))
    outs = pl.pallas_call(
        body, name=name,
        out_shape=(sems, sems, *[pltpu.HBM(a.shape, a.dtype) for a in (*srcs, *lands)],
                   jax.ShapeDtypeStruct((SUBLANE, LANE), F32)),
        in_specs=[HBM_SPEC] * (2 * nw) + [ANY_SPEC] * len(afters),
        out_specs=(SEM_SPEC, SEM_SPEC, *[HBM_SPEC] * (2 * nw), pl.BlockSpec(memory_space=pltpu.VMEM)),
        input_output_aliases={i: 2 + i for i in range(2 * nw)},
        compiler_params=pltpu.CompilerParams(has_side_effects=DATAFLOW),
    )(*[pltpu.with_memory_space_constraint(a, pltpu.HBM) for a in (*srcs, *lands)], *afters)
    return dict(sems=outs[:2], srcs=outs[2:2 + nw], lands=outs[2 + nw:2 + 2 * nw], token=outs[-1], scatter=scatter)


def _exchange_wait(handle, afters, name):
    srcs, lands, scatter = handle["srcs"], handle["lands"], handle["scatter"]
    nw = len(srcs)

    def body(*refs):
        s_refs, l_refs = refs[:nw], refs[nw:2 * nw]
        send_sems, recv_sems = refs[2 * nw], refs[2 * nw + 1]
        for w in range(nw):
            for k, peer, slot in _peers():
                cp = _split_copy(s_refs[w], l_refs[w], send_sems, recv_sems, w, k, peer, slot, scatter, False)
                cp.wait_send()
                cp.wait_recv()

    outs = pl.pallas_call(
        body, name=name,
        out_shape=tuple(pltpu.HBM(a.shape, a.dtype) for a in (*srcs, *lands)),
        in_specs=[HBM_SPEC] * (2 * nw) + [SEM_SPEC, SEM_SPEC] + [ANY_SPEC] * len(afters),
        out_specs=tuple([HBM_SPEC] * (2 * nw)),
        input_output_aliases={i: i for i in range(2 * nw)},
        compiler_params=pltpu.CompilerParams(has_side_effects=DATAFLOW),
    )(*srcs, *lands, *handle["sems"], *afters)
    return list(outs[nw:])


def _chips_and_sibling():
    x, y, c = lax.axis_index("x"), lax.axis_index("y"), lax.axis_index("c")
    return [(1 - x, y), (x, 1 - y), (1 - x, 1 - y)], (x, y, 1 - c), c


def _slot(px, py, pc):
    return 4 * px + 2 * py + pc


def _two_level_start(shards, name):
    nw = len(shards)
    lands = [_landing_zone(s, False) for s in shards]

    def body(*refs):
        s_refs, l_refs = refs[:nw], refs[nw:2 * nw]
        send_sems, recv_sems, token = refs[2 * nw], refs[2 * nw + 1], refs[-1]
        chips, sibling, c = _chips_and_sibling()
        for w in range(nw):
            for k, to in enumerate([sibling] + [(*chip, c) for chip in chips]):
                pltpu.make_async_remote_copy(
                    src_ref=s_refs[w], dst_ref=l_refs[w].at[_my_index()],
                    send_sem=send_sems.at[4 * w + k], recv_sem=recv_sems.at[4 * w + k],
                    device_id=to, device_id_type=MESH).start()
        token[...] = jnp.zeros_like(token)

    sems = pltpu.SemaphoreType.DMA((4 * nw,))
    outs = pl.pallas_call(
        body, name=name,
        out_shape=(sems, sems, *[pltpu.HBM(a.shape, a.dtype) for a in (*shards, *lands)],
                   jax.ShapeDtypeStruct((SUBLANE, LANE), F32)),
        in_specs=[HBM_SPEC] * (2 * nw),
        out_specs=(SEM_SPEC, SEM_SPEC, *[HBM_SPEC] * (2 * nw), pl.BlockSpec(memory_space=pltpu.VMEM)),
        input_output_aliases={i: 2 + i for i in range(2 * nw)},
        compiler_params=pltpu.CompilerParams(has_side_effects=DATAFLOW),
    )(*[pltpu.with_memory_space_constraint(a, pltpu.HBM) for a in (*shards, *lands)])
    return dict(sems=outs[:2], srcs=outs[2:2 + nw], lands=outs[2 + nw:2 + 2 * nw], token=outs[-1])


def _two_level_pass(handle, afters, name):
    srcs, lands = handle["srcs"], handle["lands"]
    nw = len(srcs)

    def body(*refs):
        s_refs, l_refs = refs[:nw], refs[nw:2 * nw]
        send_a, recv_a = refs[2 * nw], refs[2 * nw + 1]
        send_b, recv_b = refs[2 * nw + 2 + len(afters)], refs[2 * nw + 3 + len(afters)]
        chips, sibling, c = _chips_and_sibling()
        for w in range(nw):
            for j, chip in enumerate(chips):
                landed = l_refs[w].at[_slot(*chip, c)]
                pltpu.make_async_remote_copy(
                    src_ref=s_refs[w], dst_ref=landed, send_sem=send_a.at[4 * w + 1 + j], recv_sem=recv_a.at[4 * w + 1 + j],
                    device_id=(*chip, c), device_id_type=MESH).wait_recv()
                pltpu.make_async_remote_copy(
                    src_ref=landed, dst_ref=landed, send_sem=send_b.at[3 * w + j], recv_sem=recv_b.at[3 * w + j],
                    device_id=sibling, device_id_type=MESH).start()

    sems = pltpu.SemaphoreType.DMA((3 * nw,))
    outs = pl.pallas_call(
        body, name=name,
        out_shape=(sems, sems, *[pltpu.HBM(a.shape, a.dtype) for a in (*srcs, *lands)]),
        in_specs=[HBM_SPEC] * (2 * nw) + [SEM_SPEC, SEM_SPEC] + [ANY_SPEC] * len(afters),
        out_specs=(SEM_SPEC, SEM_SPEC, *[HBM_SPEC] * (2 * nw)),
        input_output_aliases={i: 2 + i for i in range(2 * nw)},
        compiler_params=pltpu.CompilerParams(has_side_effects=DATAFLOW),
    )(*srcs, *lands, *handle["sems"], *afters)
    return dict(sems=handle["sems"], sems_pass=outs[:2], srcs=outs[2:2 + nw], lands=outs[2 + nw:2 + 2 * nw])


def _two_level_wait(handle, name):
    srcs, lands = handle["srcs"], handle["lands"]
    nw = len(srcs)

    def body(*refs):
        s_refs, l_refs = refs[:nw], refs[nw:2 * nw]
        send_a, recv_a, send_b, recv_b = refs[2 * nw:2 * nw + 4]
        chips, sibling, c = _chips_and_sibling()
        x, y = sibling[0], sibling[1]
        for w in range(nw):
            first = pltpu.make_async_remote_copy(
                src_ref=s_refs[w], dst_ref=l_refs[w].at[_slot(x, y, 1 - c)], send_sem=send_a.at[4 * w],
                recv_sem=recv_a.at[4 * w], device_id=sibling, device_id_type=MESH)
            first.wait_send()
            first.wait_recv()
            for j, chip in enumerate(chips):
                pltpu.make_async_remote_copy(
                    src_ref=s_refs[w], dst_ref=l_refs[w].at[_slot(*chip, c)], send_sem=send_a.at[4 * w + 1 + j],
                    recv_sem=recv_a.at[4 * w + 1 + j], device_id=(*chip, c), device_id_type=MESH).wait_send()
                passed = pltpu.make_async_remote_copy(
                    src_ref=l_refs[w].at[_slot(*chip, c)], dst_ref=l_refs[w].at[_slot(*chip, 1 - c)],
                    send_sem=send_b.at[3 * w + j], recv_sem=recv_b.at[3 * w + j], device_id=sibling, device_id_type=MESH)
                passed.wait_send()
                passed.wait_recv()

    outs = pl.pallas_call(
        body, name=name,
        out_shape=tuple(pltpu.HBM(a.shape, a.dtype) for a in (*srcs, *lands)),
        in_specs=[HBM_SPEC] * (2 * nw) + [SEM_SPEC] * 4,
        out_specs=tuple([HBM_SPEC] * (2 * nw)),
        input_output_aliases={i: i for i in range(2 * nw)},
        compiler_params=pltpu.CompilerParams(has_side_effects=DATAFLOW),
    )(*srcs, *lands, *handle["sems"], *handle["sems_pass"])
    return list(outs[nw:])


def _adamw(w, g, m, v):
    m = ADAM_B1 * m + (1.0 - ADAM_B1) * g
    v = ADAM_B2 * v + (1.0 - ADAM_B2) * (g * g)
    m_hat = m / (1.0 - ADAM_B1 ** ADAM_STEP)
    v_hat = v / (1.0 - ADAM_B2 ** ADAM_STEP)
    delta = -ADAM_LR * (m_hat / (jnp.sqrt(v_hat) + ADAM_EPS) + ADAM_WD * w)
    return delta, m, v


def _sum_adam(parts, w, m, v, name):
    _, r, c = parts.shape
    tr = _pick(r, 128, 16)

    def body(p_ref, w_ref, m_ref, v_ref, g_ref, d_ref, mo_ref, vo_ref):
        g = p_ref[0].astype(F32)
        for s in range(1, N_DEV):
            g = g + p_ref[s].astype(F32)
        g_ref[...] = g
        d_ref[...], mo_ref[...], vo_ref[...] = _adamw(w_ref[...], g, m_ref[...], v_ref[...])

    row = pl.BlockSpec((tr, c), lambda i: (i, 0))
    sh = jax.ShapeDtypeStruct((r, c), F32)
    return pl.pallas_call(
        body, name=name, grid=(r // tr,),
        in_specs=[pl.BlockSpec((N_DEV, tr, c), lambda i: (0, i, 0)), row, row, row],
        out_specs=[row, row, row, row], out_shape=[sh, sh, sh, sh],
        compiler_params=_params(("parallel",)),
    )(parts, w, m, v)


def _sum_slots(parts, name):
    _, r, c = parts.shape
    tr = _pick(r, 512, SUBLANE)

    def body(p_ref, o_ref):
        g = p_ref[0]
        for s in range(1, N_DEV):
            g = g + p_ref[s]
        o_ref[...] = g

    return pl.pallas_call(
        body, name=name, grid=(r // tr,),
        in_specs=[pl.BlockSpec((N_DEV, tr, c), lambda i: (0, i, 0))],
        out_specs=pl.BlockSpec((tr, c), lambda i: (i, 0)), out_shape=jax.ShapeDtypeStruct((r, c), F32),
        compiler_params=_params(("parallel",)),
    )(parts)


def _adam_rows(g, w, m, v, name):
    r, c = g.shape
    tr = _pick(r, 512, SUBLANE)

    def body(g_ref, w_ref, m_ref, v_ref, d_ref, mo_ref, vo_ref):
        d_ref[...], mo_ref[...], vo_ref[...] = _adamw(w_ref[...], g_ref[...], m_ref[...], v_ref[...])

    row = pl.BlockSpec((tr, c), lambda i: (i, 0))
    sh = jax.ShapeDtypeStruct((r, c), F32)
    return pl.pallas_call(body, name=name, grid=(r // tr,), in_specs=[row] * 4, out_specs=[row] * 3,
                          out_shape=[sh, sh, sh], compiler_params=_params(("parallel",)))(g, w, m, v)


def _pack(arrays):
    flat = jnp.concatenate([a.reshape(-1).astype(F32) for a in arrays])
    pad = (-flat.shape[0]) % (SUBLANE * LANE)
    return jnp.pad(flat, (0, pad)).reshape(-1, LANE)


def _unpack(packed, shapes):
    flat = packed.reshape(-1)
    out, off = [], 0
    for s in shapes:
        n = math.prod(s)
        out.append(flat[off:off + n].reshape(s))
        off += n
    return out


def _block_diag(t):
    eye = jnp.eye(S5_SUPER, dtype=bool)
    bd = jnp.where(eye[None, :, None, :, None], t[:, :, :, None, :], 0.0)
    return bd.reshape(S5_SUPER, S5_SUPER * t.shape[2], S5_SUPER * t.shape[3])


def _diag_blocks(dense, a, b):
    x = dense.reshape(S5_SUPER, S5_SUPER, a, S5_SUPER, b)
    return jnp.moveaxis(jnp.diagonal(x, axis1=1, axis2=3), -1, 1)


def _s5_layouts(b_re, b_im, c_re, c_im, d):
    g2 = (S5_GROUPS // S5_SUPER, S5_SUPER)
    bt = lambda b: _block_diag(b.reshape(*g2, S5_STATE, S5_GROUP).transpose(0, 1, 3, 2))
    ct = lambda c: _block_diag(c.reshape(*g2, S5_GROUP, S5_STATE).transpose(0, 1, 3, 2))
    bsg = jnp.concatenate([bt(b_re), bt(b_im)], axis=2).astype(BF16)
    ccat = jnp.concatenate([ct(c_re), -ct(c_im)], axis=1).astype(BF16)
    return bsg, ccat, d.reshape(1, S5_WIDTH)


def _s5_param_grads(gb, gc):
    n = S5_LANES
    gb_re = _diag_blocks(gb[:, :, 0:n], S5_GROUP, S5_STATE).transpose(0, 1, 3, 2).reshape(S5_GROUPS, S5_STATE, S5_GROUP)
    gb_im = _diag_blocks(gb[:, :, n:2 * n], S5_GROUP, S5_STATE).transpose(0, 1, 3, 2).reshape(S5_GROUPS, S5_STATE, S5_GROUP)
    gc_re = _diag_blocks(gc[:, 0:n, :], S5_STATE, S5_GROUP).transpose(0, 1, 3, 2).reshape(S5_GROUPS, S5_GROUP, S5_STATE)
    gc_im = -_diag_blocks(gc[:, n:2 * n, :], S5_STATE, S5_GROUP).transpose(0, 1, 3, 2).reshape(S5_GROUPS, S5_GROUP, S5_STATE)
    return gb_re, gb_im, gc_re, gc_im


def _local_step(x, target, weight, emit, small, after=None):
    sp = small
    a_re, a_im = sp["s5_a_re"], sp["s5_a_im"]
    ldt = sp["s5_log_dt"].reshape(S5_GROUPS, 1)

    h1 = _rms_fwd(x, sp["ln_mix_g"], "rms_mix", after=after)
    w_in = weight("w_in", h1)
    proj = _mm_nn(h1, w_in, "mm_in", after=weight("after_w_in", None))
    conv_w = weight("conv_w", None)
    disc = _s5_param_fwd(a_re, a_im, ldt)
    bsg, ccat, d_row = sp["s5_layouts"]
    abar_t, coef_t = _s5_to_tile(disc[0], disc[1]), _s5_to_tile(disc[2], disc[3])
    y, sb = _s5_fwd(proj, bsg, ccat, d_row, abar_t, coef_t)
    z16 = _gelu_fwd(y)
    w_glu = weight("s5_w_glu", z16)
    gl = _mm_nn(z16, w_glu, "mm_glu")
    z2 = _glu_fwd(y, gl, sp["s5_b_glu"])
    w_ps = weight("w_proj_s5", z2)
    ys = _mm_nn(z2, w_ps, "mm_proj_s5")
    o_raw, oh, s0s = _hgrn_fwd(proj, sp["hgrn_lb_logits"], sp["hgrn_norm_g"])
    w_ph = weight("w_proj_hgrn", oh)
    yh = _mm_nn(oh, w_ph, "mm_proj_hgrn")
    merged = _merge_fwd(proj, ys, yh)
    w_out = weight("w_out", merged)
    x1 = _mm_nn(merged, w_out, "mm_out", res=x)
    h2 = _rms_fwd(x1, sp["ln_ffn_g"], "rms_ffn")
    w_up = weight("w_up", h2)
    up = _mm_nn(h2, w_up, "mm_up")
    act = _ffn_act_fwd(up, conv_w, sp["conv_b"])
    w_down = weight("w_down", act)
    x2 = _mm_nn(act, w_down, "mm_down", res=x1)
    dx2, dx2_16, g_ln_final, loss = _loss_head(x2, sp["ln_final_g"], target)

    dact = _mm_nt(dx2_16, w_down, "mm_down_dx")
    tok = emit("w_down", _mm_tn(act, dx2_16, 1, "mm_down_dw"))
    dup_g, dup_v, dcw_g, dcw_v, dcb_g, dcb_v = _ffn_act_bwd(up, dact, conv_w, sp["conv_b"], after=tok)
    dup = jnp.concatenate([dup_g, dup_v], axis=1)
    g_conv_w = jnp.concatenate([dcw_g, dcw_v], axis=1)
    g_conv_b = jnp.concatenate([dcb_g, dcb_v], axis=1)
    dh2 = _mm_nt(dup, w_up, "mm_up_dx")
    tok = emit("w_up", _mm_tn(h2, dup, N_DEV, "mm_up_dw"))
    dx1, dx1_16, g_ln_ffn = _rms_bwd(x1, sp["ln_ffn_g"], dh2, dx2, "rms_ffn_bwd", True, after=tok)

    dmerged = _mm_nt(dx1_16, w_out, "mm_out_dx")
    tok = emit("w_out", _mm_tn(merged, dx1_16, 1, "mm_out_dw"))
    dys, dyh, dgs, dgh = _merge_bwd(proj, ys, yh, dmerged, after=tok)
    doh = _mm_nt(dyh, w_ph, "mm_proj_hgrn_dx")
    tok = emit("w_proj_hgrn", _mm_tn(oh, dyh, N_DEV, "mm_proj_hgrn_dw"))
    dz2 = _mm_nt(dys, w_ps, "mm_proj_s5_dx", after=tok)
    tok = emit("w_proj_s5", _mm_tn(z2, dys, N_DEV, "mm_proj_s5_dw"))
    dgl, dza, g_b_glu = _glu_bwd(y, gl, sp["s5_b_glu"], dz2, after=tok)
    dzb = _mm_nt(dgl, w_glu, "mm_glu_dx")
    tok = emit("s5_w_glu", _mm_tn(z16, dgl, 1, "mm_glu_dw"))
    dy = _gelu_bwd(y, dza, dzb, after=tok)
    du, gb, gc, gd, g_abar_t, g_coef_t = _s5_bwd(proj, dy, sb, bsg, ccat, d_row, abar_t, coef_t)
    g_a_re, g_a_im, g_ldt = _s5_param_bwd(a_re, a_im, ldt, [*_s5_from_tile(g_abar_t), *_s5_from_tile(g_coef_t)])
    g_b_re, g_b_im, g_c_re, g_c_im = _s5_param_grads(gb, gc)
    dq, dz, dv, dg, g_norm, dlb = _hgrn_bwd(proj, o_raw, s0s, doh, sp["hgrn_lb_logits"], sp["hgrn_norm_g"])
    g_logits = _lb_bwd(sp["hgrn_lb_logits"], dlb)

    small_g = dict(s5_a_re=g_a_re, s5_a_im=g_a_im, s5_log_dt=g_ldt.reshape(1, S5_GROUPS),
                   s5_b_re=g_b_re, s5_b_im=g_b_im, s5_c_re=g_c_re, s5_c_im=g_c_im,
                   s5_d=gd.reshape(S5_GROUPS, S5_GROUP), s5_b_glu=g_b_glu, hgrn_lb_logits=g_logits,
                   hgrn_norm_g=g_norm, ln_ffn_g=g_ln_ffn, conv_w=g_conv_w, conv_b=g_conv_b, ln_final_g=g_ln_final,
                   loss=loss[0, 0:1])
    tok_small = emit("small", small_g)

    dproj = jnp.concatenate([du, dq, dz, dv, dg, dgs, dgh], axis=1)
    tok = emit("w_in", _mm_tn(h1, dproj, N_DEV, "mm_in_dw", after=tok_small))
    dh1 = _mm_nt(dproj, w_in, "mm_in_dx")
    grad_x, g_ln_mix = _rms_bwd(x, sp["ln_mix_g"], dh1, dx1, "rms_mix_bwd", False, after=tok)
    return grad_x, g_ln_mix


BIG = ("w_in", "s5_w_glu", "w_proj_s5", "w_proj_hgrn", "w_out", "w_up", "w_down")
COL_SHARDED = ("w_in", "w_proj_s5", "w_proj_hgrn", "w_up")
SMALL = ("ln_mix_g", "s5_a_re", "s5_a_im", "s5_log_dt", "s5_b_re", "s5_b_im", "s5_c_re", "s5_c_im", "s5_d",
         "s5_b_glu", "hgrn_lb_logits", "hgrn_norm_g", "ln_ffn_g", "conv_b", "ln_final_g")
WEIGHTS = ("ln_mix_g", "w_in", "s5_a_re", "s5_a_im", "s5_log_dt", "s5_b_re", "s5_b_im", "s5_c_re", "s5_c_im", "s5_d",
           "s5_w_glu", "s5_b_glu", "w_proj_s5", "hgrn_lb_logits", "hgrn_norm_g", "w_proj_hgrn", "w_out", "ln_ffn_g",
           "w_up", "conv_w", "conv_b", "w_down", "ln_final_g")


def kernel(x, ln_mix_g, w_in, s5_a_re, s5_a_im, s5_log_dt, s5_b_re, s5_b_im, s5_c_re, s5_c_im, s5_d, s5_w_glu, s5_b_glu, w_proj_s5, hgrn_lb_logits, hgrn_norm_g, w_proj_hgrn, w_out, ln_ffn_g, w_up, conv_w, conv_b, w_down, ln_final_g, loss_target, m_ln_mix_g, m_w_in, m_s5_a_re, m_s5_a_im, m_s5_log_dt, m_s5_b_re, m_s5_b_im, m_s5_c_re, m_s5_c_im, m_s5_d, m_s5_w_glu, m_s5_b_glu, m_w_proj_s5, m_hgrn_lb_logits, m_hgrn_norm_g, m_w_proj_hgrn, m_w_out, m_ln_ffn_g, m_w_up, m_conv_w, m_conv_b, m_w_down, m_ln_final_g, v_ln_mix_g, v_w_in, v_s5_a_re, v_s5_a_im, v_s5_log_dt, v_s5_b_re, v_s5_b_im, v_s5_c_re, v_s5_c_im, v_s5_d, v_s5_w_glu, v_s5_b_glu, v_w_proj_s5, v_hgrn_lb_logits, v_hgrn_norm_g, v_w_proj_hgrn, v_w_out, v_ln_ffn_g, v_w_up, v_conv_w, v_conv_b, v_w_down, v_ln_final_g):
    given = dict(locals())
    w = {n: given[n] for n in WEIGHTS}
    mom = {n: given["m_" + n] for n in WEIGHTS}
    var = {n: given["v_" + n] for n in WEIGHTS}

    first = _two_level_start([w_in[0].astype(BF16), conv_w[0]], "gather_first_start")
    zero = first["token"][0, 0]
    packed_small = SMALL[1:]
    pw, pm, pv = (_pack([d[n] for n in packed_small]) + zero for d in (w, mom, var))
    layouts = _s5_layouts(s5_b_re[0] + zero, s5_b_im[0], s5_c_re[0] + zero, s5_c_im[0], s5_d[0])
    gather_groups = (("s5_w_glu", "w_proj_s5", "w_proj_hgrn", "w_out"), ("w_up",), ("w_down",))
    shard16 = {n: w[n][0].astype(BF16) + zero.astype(BF16) for g in gather_groups for n in g}
    zones = {n: _landing_zone(s, False) for n, s in shard16.items()}
    pending, ready = {}, {}

    def weight(name, after):
        if "w_in" not in ready:
            local_work = [after, pw, pm, pv, layouts[0], layouts[1], *zones.values()]
            passed = _two_level_pass(first, local_work, "gather_first_pass")
            ready["w_in"], conv_w_all = _two_level_wait(passed, "gather_first_wait")
            ready["conv_w"] = conv_w_all.transpose(1, 0, 2).reshape(3, 2 * D_FF)
            token = ready["w_in"]
            for i, group in enumerate(gather_groups):
                handle = _exchange_start([shard16[n] for n in group], False, token, f"gather_start_{i}",
                                         lands=[zones[n] for n in group])
                token = handle["token"]
                for n in group:
                    pending[n] = (group, handle, f"gather_wait_{i}")
            ready["after_w_in"] = token
        if name not in ready:
            group, handle, wait_name = pending[name]
            for n, g in zip(group, _exchange_wait(handle, [after], wait_name)):
                ready[n] = g
        g = ready[name]
        return g if name not in BIG or name in COL_SHARDED else g.reshape(1, N_DEV * g.shape[1], g.shape[2])

    scatter_groups = (("w_down",), ("w_up",), ("w_out", "w_proj_hgrn", "w_proj_s5", "s5_w_glu"), ("w_in",))
    emitted, scatters = {}, []
    packed_names = SMALL[1:] + ("conv_w", "loss")

    def emit(name, grad):
        if name == "small":
            emitted[name] = ([grad[n].shape for n in packed_names],
                             _exchange_start([_pack([grad[n] for n in packed_names])], False, None, "small_start"))
            return emitted[name][1]["token"]
        emitted[name] = grad if name in COL_SHARDED else grad.reshape(N_DEV, -1, grad.shape[2])
        group = scatter_groups[len(scatters)]
        if not all(n in emitted for n in group):
            return None
        handle = _exchange_start([emitted[n] for n in group], True, None, f"scatter_start_{len(scatters)}")
        scatters.append((group, handle))
        return handle["token"]

    small = dict(ln_mix_g=ln_mix_g, s5_a_re=s5_a_re[0], s5_a_im=s5_a_im[0], s5_log_dt=s5_log_dt, s5_layouts=layouts,
                 s5_b_glu=s5_b_glu, hgrn_lb_logits=hgrn_lb_logits, hgrn_norm_g=hgrn_norm_g, ln_ffn_g=ln_ffn_g,
                 conv_b=conv_b, ln_final_g=ln_final_g.reshape(1, D_MODEL))
    grad_x, g_ln_mix = _local_step(x[0], loss_target[0], weight, emit, small, after=first["token"])

    shapes, handle = emitted["small"]
    total = _sum_slots(_exchange_wait(handle, [grad_x], "small_wait")[0], "sum_small")
    summed = dict(zip(packed_names, _unpack(total, shapes)))
    mix_all = _all_gather([g_ln_mix.reshape(-1, LANE)], "gather_ln_mix")[0]
    summed["ln_mix_g"] = _sum_slots(mix_all, "sum_ln_mix").reshape(1, D_MODEL)

    grads, delta, new_m, new_v = {}, {}, {}, {}
    afters = [grad_x, total]
    for i, (group, handle) in enumerate(scatters):
        for n, r in zip(group, _exchange_wait(handle, afters, f"scatter_wait_{i}")):
            g, d, m2, v2 = _sum_adam(r, w[n][0], mom[n][0], var[n][0], "adam_" + n)
            grads[n], delta[n], new_m[n], new_v[n] = g[None], d[None], m2[None], v2[None]
        if i == len(scatters) - 2:
            afters = [delta[n] for g2, _ in scatters[:-1] for n in g2]

    d_s, m_s, v_s = _adam_rows(_pack([summed[n] for n in packed_small]), pw, pm, pv, "adam_small")
    wshapes = [w[n].shape for n in packed_small]
    for n, d, m2, v2 in zip(packed_small, _unpack(d_s, wshapes), _unpack(m_s, wshapes), _unpack(v_s, wshapes)):
        grads[n], delta[n], new_m[n], new_v[n] = summed[n].reshape(w[n].shape), d, m2, v2
    grads["ln_mix_g"] = summed["ln_mix_g"]
    delta["ln_mix_g"], new_m["ln_mix_g"], new_v["ln_mix_g"] = _adam_rows(summed["ln_mix_g"], ln_mix_g, m_ln_mix_g,
                                                                         v_ln_mix_g, "adam_ln_mix")
    me = 4 * lax.axis_index("x") + 2 * lax.axis_index("y") + lax.axis_index("c")
    ncol = conv_w.shape[2]
    g_cw = lax.dynamic_slice_in_dim(summed["conv_w"], me * ncol, ncol, axis=1)
    d_cw, m_cw, v_cw = _adam_rows(g_cw, conv_w[0], m_conv_w[0], v_conv_w[0], "adam_conv_w")
    grads["conv_w"], delta["conv_w"], new_m["conv_w"], new_v["conv_w"] = g_cw[None], d_cw[None], m_cw[None], v_cw[None]

    return (summed["loss"].reshape(()), grad_x[None], *[grads[n] for n in WEIGHTS], *[delta[n] for n in WEIGHTS],
            *[new_m[n] for n in WEIGHTS], *[new_v[n] for n in WEIGHTS])
```

```python
import math

import jax
import jax.numpy as jnp
from jax import lax
from jax.experimental import pallas as pl
from jax.experimental.pallas import tpu as pltpu

F32 = jnp.float32
BF16 = jnp.bfloat16

N_DEV = 8
D_MODEL = 2048
S5_WIDTH = 1024
S5_GROUP = 16
S5_GROUPS = 64
S5_STATE = 64
S5_MAX_RE = -1e-4
S5_SUPER = 8
S5_LANES = S5_SUPER * S5_STATE
HGRN_WIDTH = 1024
HGRN_HEADS = 8
HGRN_DH = 128
HGRN_CHUNK = 64
HGRN_SUBS = 4
D_FF = 5632
RMS_EPS = 1e-6
ADAM_LR = 0.001
ADAM_B1 = 0.9
ADAM_B2 = 0.999
ADAM_EPS = 1e-08
ADAM_WD = 0.01
ADAM_STEP = 10

LANE = 128
SUBLANE = 8
VMEM_LIMIT = 48 * 1024 * 1024
MESH = pl.DeviceIdType.MESH
GELU_C = math.sqrt(2.0 / math.pi)
GELU_A = 0.044715


def _params(sem=None):
    return pltpu.CompilerParams(dimension_semantics=sem, vmem_limit_bytes=VMEM_LIMIT)


def _pick(n, cap, unit=LANE):
    best = None
    for t in range(unit, min(n, cap) + 1, unit):
        if n % t == 0:
            best = t
    return best if best is not None else n


def _ordered(body, in_specs, args, after):
    if after is None:
        return body, list(in_specs), list(args)
    n_in = len(args)

    def ordered_body(*refs):
        return body(*refs[:n_in], *refs[n_in + 1:])

    return ordered_body, [*in_specs, pl.BlockSpec(memory_space=pl.ANY)], [*args, after]


def _sigmoid(x):
    return 0.5 * jnp.tanh(0.5 * x) + 0.5


def _silu_and_grad(x):
    s = _sigmoid(x)
    return x * s, s * (1.0 + x * (1.0 - s))


def _gelu_and_grad(y):
    inner = GELU_C * (y + GELU_A * y * y * y)
    th = jnp.tanh(inner)
    val = 0.5 * y * (1.0 + th)
    grad = 0.5 * (1.0 + th) + 0.5 * y * (1.0 - th * th) * GELU_C * (1.0 + 3.0 * GELU_A * y * y)
    return val, grad


def _dot(a, b):
    return jnp.dot(a, b, preferred_element_type=F32)


def _dot_nt(a, b):
    return lax.dot_general(a, b, (((1,), (1,)), ((), ())), preferred_element_type=F32)


def _dot_tn(a, b):
    return lax.dot_general(a, b, (((0,), (0,)), ((), ())), preferred_element_type=F32)


def _blocks_per_step(nb, ns, tn, cap=2048):
    if tn != ns:
        return 1
    best = 1
    for b in range(1, nb + 1):
        if nb % b == 0 and b * ns <= cap:
            best = b
    return best


NN_TILE_BYTES = 42 * 1024 * 1024


def _mm_nn(a, w, name, res=None, out_dtype=F32, after=None):
    m, kdim = a.shape
    nb, _, ns = w.shape
    tk, tn = _pick(kdim, D_FF), _pick(ns, 1536)
    npb, nk = ns // tn, kdim // tk
    bps = _blocks_per_step(nb, ns, tn)
    assert bps == 1 or nk == 1

    def buffers(rows):
        return 2 * (rows * tk * 2 + bps * tk * tn * 2 + rows * bps * tn * 4 * (2 if res is not None else 1))

    tm = next((r for r in (_pick(m, 1024), _pick(m, 512)) if buffers(r) <= NN_TILE_BYTES), _pick(m, 256))

    def body(*refs):
        a_ref, w_ref = refs[0], refs[1]
        r_ref = refs[2] if res is not None else None
        o_ref = refs[3] if res is not None else refs[2]

        def finish(r, cols):
            if res is not None:
                r = r + r_ref[:, cols]
            o_ref[:, cols] = r.astype(out_dtype)

        if nk == 1:
            for b in range(bps):
                finish(_dot(a_ref[...], w_ref[b]), slice(b * tn, (b + 1) * tn))
            return
        acc = refs[-1]
        k = pl.program_id(2)

        @pl.when(k == 0)
        def _():
            acc[...] = jnp.zeros_like(acc)

        acc[...] += _dot(a_ref[...], w_ref[0])

        @pl.when(k == nk - 1)
        def _():
            finish(acc[...], slice(0, tn))

    in_specs = [pl.BlockSpec((tm, tk), lambda j, i, k: (i, k)),
                pl.BlockSpec((bps, tk, tn), lambda j, i, k: (j // npb, k, j % npb))]
    args = [a, w]
    if res is not None:
        in_specs.append(pl.BlockSpec((tm, bps * tn), lambda j, i, k: (i, j)))
        args.append(res)
    body, in_specs, args = _ordered(body, in_specs, args, after)
    return pl.pallas_call(
        body, name=name, grid=(nb * npb // bps, m // tm, nk),
        in_specs=in_specs, out_specs=pl.BlockSpec((tm, bps * tn), lambda j, i, k: (i, j)),
        out_shape=jax.ShapeDtypeStruct((m, nb * ns), out_dtype),
        scratch_shapes=[pltpu.VMEM((tm, tn), F32)] if nk > 1 else [],
        compiler_params=_params(("parallel", "parallel", "arbitrary")),
    )(*args)


NT_STEP_COLS = 2816


def _mm_nt(a, w, name, out_dtype=F32, after=None):
    m, _ = a.shape
    nb, kdim, ns = w.shape
    tm, tko, tn = _pick(m, 1024), _pick(kdim, 1024), _pick(ns, 2048)
    npb = ns // tn
    bps = _blocks_per_step(nb, ns, tn, cap=NT_STEP_COLS)
    nred = nb * npb // bps

    def body(a_ref, w_ref, o_ref, *scratch):
        total = _dot_nt(a_ref[:, 0:tn], w_ref[0])
        for b in range(1, bps):
            total = total + _dot_nt(a_ref[:, b * tn:(b + 1) * tn], w_ref[b])
        if nred == 1:
            o_ref[...] = total.astype(out_dtype)
            return
        acc = scratch[0]
        n = pl.program_id(2)

        @pl.when(n == 0)
        def _():
            acc[...] = jnp.zeros_like(acc)

        acc[...] += total

        @pl.when(n == nred - 1)
        def _():
            o_ref[...] = acc[...].astype(out_dtype)

    in_specs = [pl.BlockSpec((tm, bps * tn), lambda i, j, n: (i, n)),
                pl.BlockSpec((bps, tko, tn), lambda i, j, n: (n // npb, j, n % npb))]
    body, in_specs, args = _ordered(body, in_specs, [a, w], after)
    return pl.pallas_call(
        body, name=name, grid=(m // tm, kdim // tko, nred),
        in_specs=in_specs,
        out_specs=pl.BlockSpec((tm, tko), lambda i, j, n: (i, j)),
        out_shape=jax.ShapeDtypeStruct((m, kdim), out_dtype),
        scratch_shapes=[pltpu.VMEM((tm, tko), F32)] if nred > 1 else [],
        compiler_params=_params(("parallel", "parallel", "arbitrary")),
    )(*args)


def _mm_tn(a, d, nb, name, out_dtype=BF16, after=None):
    m, kdim = a.shape
    ns = d.shape[1] // nb
    tm, tko, tn = _pick(m, 4096), _pick(kdim, 512), _pick(ns, 1536)
    npb, nm = ns // tn, m // tm

    def body(a_ref, d_ref, o_ref, *scratch):
        if nm == 1:
            o_ref[...] = _dot_tn(a_ref[...], d_ref[...]).astype(out_dtype)
            return
        acc = scratch[0]
        r = pl.program_id(2)

        @pl.when(r == 0)
        def _():
            acc[...] = jnp.zeros_like(acc)

        acc[...] += _dot_tn(a_ref[...], d_ref[...])

        @pl.when(r == nm - 1)
        def _():
            o_ref[...] = acc[...].astype(out_dtype)

    in_specs = [pl.BlockSpec((tm, tko), lambda j, i, r: (r, i)), pl.BlockSpec((tm, tn), lambda j, i, r: (r, j))]
    body, in_specs, args = _ordered(body, in_specs, [a, d], after)
    return pl.pallas_call(
        body, name=name, grid=(nb * npb, kdim // tko, nm),
        in_specs=in_specs,
        out_specs=pl.BlockSpec((None, tko, tn), lambda j, i, r: (j // npb, i, j % npb)),
        out_shape=jax.ShapeDtypeStruct((nb, kdim, ns), out_dtype),
        scratch_shapes=[pltpu.VMEM((tko, tn), F32)] if nm > 1 else [],
        compiler_params=_params(("parallel", "parallel", "arbitrary")),
    )(*args)


def _rms_fwd(x, g, name, after=None):
    t, d = x.shape
    tr = _pick(t, 256, SUBLANE)

    def body(x_ref, g_ref, h_ref):
        xv = x_ref[...]
        r = lax.rsqrt(jnp.mean(xv * xv, axis=-1, keepdims=True) + RMS_EPS)
        h_ref[...] = (xv * r * g_ref[...]).astype(BF16)

    in_specs = [pl.BlockSpec((tr, d), lambda i: (i, 0)), pl.BlockSpec((1, d), lambda i: (0, 0))]
    body, in_specs, args = _ordered(body, in_specs, [x, g], after)
    return pl.pallas_call(
        body, name=name, grid=(t // tr,),
        in_specs=in_specs,
        out_specs=pl.BlockSpec((tr, d), lambda i: (i, 0)),
        out_shape=jax.ShapeDtypeStruct((t, d), BF16),
        compiler_params=_params(("parallel",)),
    )(*args)


def _rms_bwd(x, g, dh, add, name, want_bf16, after=None):
    t, d = x.shape
    tr = _pick(t, 256, SUBLANE)

    def body(x_ref, g_ref, dh_ref, add_ref, *outs):
        if want_bf16:
            dx_ref, dxb_ref, dg_ref = outs
        else:
            dx_ref, dg_ref = outs
        i = pl.program_id(0)

        @pl.when(i == 0)
        def _():
            dg_ref[...] = jnp.zeros_like(dg_ref)

        xv, dhv = x_ref[...], dh_ref[...]
        r = lax.rsqrt(jnp.mean(xv * xv, axis=-1, keepdims=True) + RMS_EPS)
        xh = xv * r
        dg_ref[...] += jnp.sum(dhv * xh, axis=0, keepdims=True)
        dxh = dhv * g_ref[...]
        dx = add_ref[...] + r * (dxh - xh * jnp.mean(dxh * xh, axis=-1, keepdims=True))
        dx_ref[...] = dx
        if want_bf16:
            dxb_ref[...] = dx.astype(BF16)

    row = pl.BlockSpec((tr, d), lambda i: (i, 0))
    vec = pl.BlockSpec((1, d), lambda i: (0, 0))
    out_specs = [row] + ([row] if want_bf16 else []) + [vec]
    out_shape = ([jax.ShapeDtypeStruct((t, d), F32)] + ([jax.ShapeDtypeStruct((t, d), BF16)] if want_bf16 else [])
                 + [jax.ShapeDtypeStruct((1, d), F32)])
    body, in_specs, args = _ordered(body, [row, vec, row, row], [x, g, dh, add], after)
    return pl.pallas_call(
        body, name=name, grid=(t // tr,),
        in_specs=in_specs, out_specs=out_specs, out_shape=out_shape,
        compiler_params=_params(("arbitrary",)),
    )(*args)


def _loss_head(x2, g, target, name="loss_head"):
    t, d = x2.shape
    tr = _pick(t, 256, SUBLANE)

    def body(x_ref, g_ref, t_ref, dx_ref, dxb_ref, dg_ref, loss_ref):
        i = pl.program_id(0)

        @pl.when(i == 0)
        def _():
            dg_ref[...] = jnp.zeros_like(dg_ref)
            loss_ref[...] = jnp.zeros_like(loss_ref)

        xv = x_ref[...]
        gv = g_ref[...]
        r = lax.rsqrt(jnp.mean(xv * xv, axis=-1, keepdims=True) + RMS_EPS)
        xh = xv * r
        err = xh * gv - t_ref[...]
        part = 0.5 * jnp.sum(jnp.mean(err * err, axis=-1, keepdims=True), axis=0, keepdims=True)
        loss_ref[...] += jnp.broadcast_to(part, loss_ref.shape)
        dy = err * (1.0 / d)
        dg_ref[...] += jnp.sum(dy * xh, axis=0, keepdims=True)
        dxh = dy * gv
        dx = r * (dxh - xh * jnp.mean(dxh * xh, axis=-1, keepdims=True))
        dx_ref[...] = dx
        dxb_ref[...] = dx.astype(BF16)

    row = pl.BlockSpec((tr, d), lambda i: (i, 0))
    vec = pl.BlockSpec((1, d), lambda i: (0, 0))
    return pl.pallas_call(
        body, name=name, grid=(t // tr,),
        in_specs=[row, vec, row],
        out_specs=[row, row, vec, pl.BlockSpec((1, LANE), lambda i: (0, 0))],
        out_shape=[jax.ShapeDtypeStruct((t, d), F32), jax.ShapeDtypeStruct((t, d), BF16),
                   jax.ShapeDtypeStruct((1, d), F32), jax.ShapeDtypeStruct((1, LANE), F32)],
        compiler_params=_params(("arbitrary",)),
    )(x2, g, target)


def _s5_discretize(a_re, a_im, ldt):
    lam_re = jnp.minimum(a_re, S5_MAX_RE)
    lam_im = a_im
    dt = jnp.exp(ldt)
    mag = jnp.exp(lam_re * dt)
    abar_re = mag * jnp.cos(lam_im * dt)
    abar_im = mag * jnp.sin(lam_im * dt)
    den = lam_re * lam_re + lam_im * lam_im
    nr = abar_re - 1.0
    ni = abar_im
    coef_re = (nr * lam_re + ni * lam_im) / den
    coef_im = (ni * lam_re - nr * lam_im) / den
    return abar_re, abar_im, coef_re, coef_im


def _s5_param_fwd(a_re, a_im, ldt):
    def body(ar_ref, ai_ref, l_ref, o0, o1, o2, o3):
        outs = _s5_discretize(ar_ref[...], ai_ref[...], l_ref[...])
        for o, v in zip((o0, o1, o2, o3), outs):
            o[...] = v

    sh = jax.ShapeDtypeStruct(a_re.shape, F32)
    return pl.pallas_call(body, name="s5_param_fwd", out_shape=[sh, sh, sh, sh], compiler_params=_params())(a_re, a_im, ldt)


def _s5_param_bwd(a_re, a_im, ldt, cts):
    def body(ar_ref, ai_ref, l_ref, c0, c1, c2, c3, g0, g1, g2):
        _, vjp = jax.vjp(_s5_discretize, ar_ref[...], ai_ref[...], l_ref[...])
        ga, gb, gl = vjp((c0[...], c1[...], c2[...], c3[...]))
        g0[...] = ga
        g1[...] = gb
        g2[...] = gl

    sh = jax.ShapeDtypeStruct(a_re.shape, F32)
    return pl.pallas_call(body, name="s5_param_bwd", out_shape=[sh, sh, jax.ShapeDtypeStruct(ldt.shape, F32)],
                          compiler_params=_params())(a_re, a_im, ldt, *cts)


def _cmul(ar, ai, br, bi):
    return ar * br - ai * bi, ar * bi + ai * br


S5_TC = 128
S5_TILE = S5_SUPER * SUBLANE
S5_HALF = S5_TILE // 2


def _s5_to_tile(re, im):
    f = lambda a: a.reshape(S5_SUPER, S5_LANES // LANE, LANE).transpose(1, 0, 2).reshape(S5_HALF, LANE)
    return jnp.concatenate([f(re), f(im)], axis=0)


def _s5_from_tile(tile):
    f = lambda a: a.reshape(S5_LANES // LANE, S5_SUPER, LANE).transpose(1, 0, 2).reshape(S5_GROUPS, S5_STATE)
    return f(tile[0:S5_HALF]), f(tile[S5_HALF:])


RE = slice(0, S5_HALF)
IM = slice(S5_HALF, S5_TILE)


def _s5_scatter_rows(buf, rows, first_tile=0):
    tc = rows[0].shape[0]
    for j in range(SUBLANE):
        stacked = jnp.stack([r[:, j * LANE:(j + 1) * LANE] for r in rows], axis=0)
        buf[first_tile:first_tile + tc, j * SUBLANE:(j + 1) * SUBLANE, :] = jnp.swapaxes(stacked, 0, 1)


def _s5_gather_rows(buf, tc, first_tile=0):
    per_j = [jnp.swapaxes(buf[first_tile:first_tile + tc, j * SUBLANE:(j + 1) * SUBLANE, :], 0, 1)
             for j in range(SUBLANE)]
    return [jnp.concatenate([per_j[j][k] for j in range(SUBLANE)], axis=1) for k in range(S5_SUPER)]


def _s5_fwd(proj, bsg, ccat, d_row, abar_t, coef_t):
    t = proj.shape[0]
    tc = min(t, S5_TC)
    n_chunks = t // tc

    def body(u_ref, b_ref, c_ref, d_ref, a_ref, cf_ref, y_ref, sb_ref, x, car):
        @pl.when(pl.program_id(0) == 0)
        def _():
            car[...] = jnp.zeros_like(car)

        sb_ref[...] = car[...]
        u = u_ref[...]
        _s5_scatter_rows(x, [_dot(u[:, k * LANE:(k + 1) * LANE].astype(BF16), b_ref[k]) for k in range(S5_SUPER)])
        ar, ai = a_ref[RE, :], a_ref[IM, :]
        cr, ci = cf_ref[RE, :], cf_ref[IM, :]

        def step(i, carry):
            sr, si = carry
            xr, xi = _cmul(cr, ci, x[i, RE, :], x[i, IM, :])
            sr, si = ar * sr - ai * si + xr, ar * si + ai * sr + xi
            x[i, RE, :] = sr
            x[i, IM, :] = si
            return sr, si

        sr, si = lax.fori_loop(0, tc, step, (car[RE, :], car[IM, :]), unroll=4)
        car[RE, :] = sr
        car[IM, :] = si
        for k, s_k in enumerate(_s5_gather_rows(x, tc)):
            cols = slice(k * LANE, (k + 1) * LANE)
            y_ref[:, cols] = _dot(s_k.astype(BF16), c_ref[k]) + d_ref[:, cols] * u[:, cols]

    full = lambda shape: pl.BlockSpec(shape, lambda c: (0,) * len(shape))
    return pl.pallas_call(
        body, name="s5_fwd", grid=(n_chunks,),
        in_specs=[pl.BlockSpec((tc, S5_WIDTH), lambda c: (c, 0)), full(bsg.shape), full(ccat.shape), full(d_row.shape),
                  full(abar_t.shape), full(coef_t.shape)],
        out_specs=[pl.BlockSpec((tc, S5_WIDTH), lambda c: (c, 0)), pl.BlockSpec((None, S5_TILE, LANE), lambda c: (c, 0, 0))],
        out_shape=[jax.ShapeDtypeStruct((t, S5_WIDTH), F32), jax.ShapeDtypeStruct((n_chunks, S5_TILE, LANE), F32)],
        scratch_shapes=[pltpu.VMEM((tc, S5_TILE, LANE), F32), pltpu.VMEM((S5_TILE, LANE), F32)],
        compiler_params=_params(("arbitrary",)),
    )(proj, bsg, ccat, d_row, abar_t, coef_t)


def _s5_bwd(proj, dy, sb, bsg, ccat, d_row, abar_t, coef_t):
    t = proj.shape[0]
    tc = min(t, S5_TC)
    n_chunks = t // tc
    last = n_chunks - 1

    def body(u_ref, dy_ref, sb_ref, b_ref, c_ref, d_ref, a_ref, cf_ref,
             du_ref, gb_ref, gc_ref, gd_ref, ga_ref, gcf_ref, xb, xs, xg, gcar, acc):
        @pl.when(pl.program_id(0) == 0)
        def _():
            gcar[...] = jnp.zeros_like(gcar)
            acc[...] = jnp.zeros_like(acc)
            gb_ref[...] = jnp.zeros_like(gb_ref)
            gc_ref[...] = jnp.zeros_like(gc_ref)
            gd_ref[...] = jnp.zeros_like(gd_ref)

        u = u_ref[...]
        dyv = dy_ref[...]
        u16, dy16 = u.astype(BF16), dyv.astype(BF16)
        subs = [slice(k * LANE, (k + 1) * LANE) for k in range(S5_SUPER)]
        _s5_scatter_rows(xb, [_dot(u16[:, c], b_ref[k]) for k, c in enumerate(subs)])
        _s5_scatter_rows(xg, [_dot_nt(dy16[:, c], c_ref[k]) for k, c in enumerate(subs)])
        ar, ai = a_ref[RE, :], a_ref[IM, :]
        cr, ci = cf_ref[RE, :], cf_ref[IM, :]

        xs[0] = sb_ref[...]

        def fstep(i, carry):
            sr, si = carry
            xr, xi = _cmul(cr, ci, xb[i, RE, :], xb[i, IM, :])
            sr, si = ar * sr - ai * si + xr, ar * si + ai * sr + xi
            xs[i + 1, RE, :] = sr
            xs[i + 1, IM, :] = si
            return sr, si

        lax.fori_loop(0, tc, fstep, (sb_ref[RE, :], sb_ref[IM, :]), unroll=4)

        def rstep(n, carry):
            gr, gi, a0, a1, a2, a3 = carry
            i = tc - 1 - n
            xr = xg[i, RE, :] + ar * gr + ai * gi
            xi = xg[i, IM, :] + ar * gi - ai * gr
            pr, pi = xs[i, RE, :], xs[i, IM, :]
            br, bi = xb[i, RE, :], xb[i, IM, :]
            a0 = a0 + pr * xr + pi * xi
            a1 = a1 + pr * xi - pi * xr
            a2 = a2 + br * xr + bi * xi
            a3 = a3 + br * xi - bi * xr
            xg[i, RE, :] = cr * xr + ci * xi
            xg[i, IM, :] = cr * xi - ci * xr
            return xr, xi, a0, a1, a2, a3

        init = (gcar[RE, :], gcar[IM, :], acc[0], acc[1], acc[2], acc[3])
        gr, gi, a0, a1, a2, a3 = lax.fori_loop(0, tc, rstep, init, unroll=2)
        gcar[RE, :] = gr
        gcar[IM, :] = gi
        for idx, a in enumerate((a0, a1, a2, a3)):
            acc[idx] = a
        ga_ref[RE, :] = a0
        ga_ref[IM, :] = a1
        gcf_ref[RE, :] = a2
        gcf_ref[IM, :] = a3

        g_rows = _s5_gather_rows(xg, tc)
        s_rows = _s5_gather_rows(xs, tc, first_tile=1)
        for k in range(S5_SUPER):
            cols = subs[k]
            g16 = g_rows[k].astype(BF16)
            s16 = s_rows[k].astype(BF16)
            gb_ref[k] += _dot_tn(u16[:, cols], g16)
            gc_ref[k] += _dot_tn(s16, dy16[:, cols])
            du_ref[:, cols] = (_dot_nt(g16, b_ref[k]) + d_ref[:, cols] * dyv[:, cols]).astype(BF16)
        gd_ref[...] += jnp.sum(dyv * u, axis=0, keepdims=True)

    full = lambda shape: pl.BlockSpec(shape, lambda c: (0,) * len(shape))
    rows = pl.BlockSpec((tc, S5_WIDTH), lambda c: (last - c, 0))
    tile = (S5_TILE, LANE)
    return pl.pallas_call(
        body, name="s5_bwd", grid=(n_chunks,),
        in_specs=[rows, rows, pl.BlockSpec((None, S5_TILE, LANE), lambda c: (last - c, 0, 0)),
                  full(bsg.shape), full(ccat.shape), full(d_row.shape), full(abar_t.shape), full(coef_t.shape)],
        out_specs=[rows, full(bsg.shape), full(ccat.shape), full(d_row.shape), full(tile), full(tile)],
        out_shape=[jax.ShapeDtypeStruct((t, S5_WIDTH), BF16), jax.ShapeDtypeStruct(bsg.shape, F32),
                   jax.ShapeDtypeStruct(ccat.shape, F32), jax.ShapeDtypeStruct(d_row.shape, F32),
                   jax.ShapeDtypeStruct(tile, F32), jax.ShapeDtypeStruct(tile, F32)],
        scratch_shapes=[pltpu.VMEM((tc, S5_TILE, LANE), F32), pltpu.VMEM((tc + 1, S5_TILE, LANE), F32),
                        pltpu.VMEM((tc, S5_TILE, LANE), F32), pltpu.VMEM(tile, F32),
                        pltpu.VMEM((4, S5_HALF, LANE), F32)],
        compiler_params=_params(("arbitrary",)),
    )(proj, dy, sb, bsg, ccat, d_row, abar_t, coef_t)


def _gelu_fwd(y, name="s5_gelu"):
    t, w = y.shape
    tr = _pick(t, 512, SUBLANE)

    def body(y_ref, z_ref):
        z_ref[...] = _gelu_and_grad(y_ref[...])[0].astype(BF16)

    row = pl.BlockSpec((tr, w), lambda i: (i, 0))
    return pl.pallas_call(body, name=name, grid=(t // tr,), in_specs=[row], out_specs=row,
                          out_shape=jax.ShapeDtypeStruct((t, w), BF16), compiler_params=_params(("parallel",)))(y)


def _glu_fwd(y, gl, b, name="s5_glu"):
    t, w = y.shape
    tr = _pick(t, 512, SUBLANE)

    def body(y_ref, gl_ref, b_ref, z2_ref):
        z = _gelu_and_grad(y_ref[...])[0]
        z2_ref[...] = (z * _sigmoid(gl_ref[...] + b_ref[...])).astype(BF16)

    row = pl.BlockSpec((tr, w), lambda i: (i, 0))
    return pl.pallas_call(body, name=name, grid=(t // tr,),
                          in_specs=[row, row, pl.BlockSpec((1, w), lambda i: (0, 0))], out_specs=row,
                          out_shape=jax.ShapeDtypeStruct((t, w), BF16), compiler_params=_params(("parallel",)))(y, gl, b)


def _glu_bwd(y, gl, b, dz2, name="s5_glu_bwd", after=None):
    t, w = y.shape
    tr = _pick(t, 512, SUBLANE)

    def body(y_ref, gl_ref, b_ref, dz2_ref, dgl_ref, dza_ref, db_ref):
        @pl.when(pl.program_id(0) == 0)
        def _():
            db_ref[...] = jnp.zeros_like(db_ref)

        z = _gelu_and_grad(y_ref[...])[0]
        s = _sigmoid(gl_ref[...] + b_ref[...])
        dz2v = dz2_ref[...]
        dgl = dz2v * z * s * (1.0 - s)
        dgl_ref[...] = dgl.astype(BF16)
        dza_ref[...] = dz2v * s
        db_ref[...] += jnp.sum(dgl, axis=0, keepdims=True)

    row = pl.BlockSpec((tr, w), lambda i: (i, 0))
    vec = pl.BlockSpec((1, w), lambda i: (0, 0))
    body, in_specs, args = _ordered(body, [row, row, vec, row], [y, gl, b, dz2], after)
    return pl.pallas_call(body, name=name, grid=(t // tr,), in_specs=in_specs, out_specs=[row, row, vec],
                          out_shape=[jax.ShapeDtypeStruct((t, w), BF16), jax.ShapeDtypeStruct((t, w), F32),
                                     jax.ShapeDtypeStruct((1, w), F32)],
                          compiler_params=_params(("arbitrary",)))(*args)


def _gelu_bwd(y, dza, dzb, name="s5_gelu_bwd", after=None):
    t, w = y.shape
    tr = _pick(t, 512, SUBLANE)

    def body(y_ref, a_ref, b_ref, dy_ref):
        dy_ref[...] = (a_ref[...] + b_ref[...]) * _gelu_and_grad(y_ref[...])[1]

    row = pl.BlockSpec((tr, w), lambda i: (i, 0))
    body, in_specs, args = _ordered(body, [row, row, row], [y, dza, dzb], after)
    return pl.pallas_call(body, name=name, grid=(t // tr,), in_specs=in_specs, out_specs=row,
                          out_shape=jax.ShapeDtypeStruct((t, w), F32), compiler_params=_params(("parallel",)))(*args)


def _tri_dot(tri16, x):
    hi = x.astype(BF16)
    lo = (x - hi.astype(F32)).astype(BF16)
    return _dot(tri16, hi) + _dot(tri16, lo)


def _hgrn_pre(q_in, z, lg):
    lb = _sigmoid(lg[0:1, :] - lg[1:2, :])
    qs, dqs = _silu_and_grad(q_in)
    sz = _sigmoid(z)
    f = lb + (1.0 - lb) * sz
    k = (1.0 - lb) * (1.0 - sz)
    c = HGRN_CHUNK
    r = lax.broadcasted_iota(jnp.int32, (c, c), 0)
    s = lax.broadcasted_iota(jnp.int32, (c, c), 1)
    causal = r >= s
    b = _tri_dot(jnp.where(causal, 1.0, 0.0).astype(BF16), jnp.log(f))
    b_end = b[c - 1:c, :]
    b_mid = b[c // 2 - 1:c // 2, :]
    e_q, e_k, e_0, e_c = jnp.exp(b - b_mid), jnp.exp(b_mid - b), jnp.exp(b), jnp.exp(b_end - b)
    return dict(lb=lb, qs=qs, dqs=dqs, sz=sz, f=f, k=k, causal=causal, b_end=b_end,
                e_q=e_q, e_k=e_k, e_0=e_0, e_c=e_c,
                qt=qs * e_q, kt=k * e_k, q0=qs * e_0, kc=k * e_c)


def _hgrn_fwd(proj, logits, ng):
    t = proj.shape[0]
    c, dh = HGRN_CHUNK, HGRN_DH
    n_chunks = t // c
    subs = HGRN_SUBS if n_chunks % HGRN_SUBS == 0 else 1

    def head(h, sub, q_ref, z_ref, v_ref, g_ref, lg_ref, ng_ref, o_ref, oh_ref, s0_ref, st):
        sl = slice(h * dh, (h + 1) * dh)
        rs = slice(sub * c, (sub + 1) * c)
        s0 = st[h]
        s0_ref[h, sub] = s0
        p = _hgrn_pre(q_ref[rs, sl], z_ref[rs, sl], lg_ref[:, sl])
        v16 = v_ref[rs, sl].astype(BF16)
        a = jnp.where(p["causal"], _dot_nt(p["qt"].astype(BF16), p["kt"].astype(BF16)), 0.0)
        o = _dot_nt(p["q0"].astype(BF16), s0.astype(BF16)) + _dot(a.astype(BF16), v16)
        st[h] = jnp.exp(p["b_end"]) * s0 + _dot_tn(v16, p["kc"].astype(BF16))
        o_ref[rs, sl] = o
        rn = lax.rsqrt(jnp.mean(o * o, axis=-1, keepdims=True) + RMS_EPS)
        oh_ref[rs, sl] = (o * rn * ng_ref[:, sl] * _silu_and_grad(g_ref[rs, sl])[0]).astype(BF16)

    def body(*refs):
        st = refs[-1]

        @pl.when(pl.program_id(0) == 0)
        def _():
            st[...] = jnp.zeros_like(st)

        for sub in range(subs):
            for h in range(HGRN_HEADS):
                head(h, sub, *refs)

    def wide(off):
        return pl.BlockSpec((subs * c, HGRN_WIDTH), lambda i: (i, off))

    return pl.pallas_call(
        body, name="hgrn_fwd", grid=(n_chunks // subs,),
        in_specs=[wide(1), wide(2), wide(3), wide(4),
                  pl.BlockSpec((2, HGRN_WIDTH), lambda i: (0, 0)), pl.BlockSpec((1, HGRN_WIDTH), lambda i: (0, 0))],
        out_specs=[wide(0), wide(0), pl.BlockSpec((HGRN_HEADS, subs, dh, dh), lambda i: (0, i, 0, 0))],
        out_shape=[jax.ShapeDtypeStruct((t, HGRN_WIDTH), F32), jax.ShapeDtypeStruct((t, HGRN_WIDTH), BF16),
                   jax.ShapeDtypeStruct((HGRN_HEADS, n_chunks, dh, dh), F32)],
        scratch_shapes=[pltpu.VMEM((HGRN_HEADS, dh, dh), F32)],
        compiler_params=_params(("arbitrary",)),
    )(proj, proj, proj, proj, logits, ng)


def _hgrn_bwd(proj, o_raw, s0s, doh, logits, ng):
    t = proj.shape[0]
    c, dh = HGRN_CHUNK, HGRN_DH
    n_chunks = t // c
    subs = HGRN_SUBS if n_chunks % HGRN_SUBS == 0 else 1
    last = n_chunks // subs - 1

    def head(h, sub, q_ref, z_ref, v_ref, g_ref, o_ref, s0_ref, doh_ref, lg_ref, ng_ref,
             dq_ref, dz_ref, dv_ref, dg_ref, dng_ref, dlb_ref, dst):
        sl = slice(h * dh, (h + 1) * dh)
        rs = slice(sub * c, (sub + 1) * c)
        p = _hgrn_pre(q_ref[rs, sl], z_ref[rs, sl], lg_ref[:, sl])
        v = v_ref[rs, sl]
        v16 = v.astype(BF16)
        s0 = s0_ref[h, sub]
        ds_end = dst[h]
        ds16 = ds_end.astype(BF16)
        ngv = ng_ref[:, sl]

        o = o_ref[rs, sl]
        dohv = doh_ref[rs, sl]
        sg, dsg = _silu_and_grad(g_ref[rs, sl])
        rn = lax.rsqrt(jnp.mean(o * o, axis=-1, keepdims=True) + RMS_EPS)
        oh = o * rn
        dg_ref[rs, sl] = (dohv * oh * ngv * dsg).astype(BF16)
        don = dohv * sg
        dng_ref[:, sl] += jnp.sum(don * oh, axis=0, keepdims=True)
        doh_n = don * ngv
        do = rn * (doh_n - oh * jnp.mean(doh_n * oh, axis=-1, keepdims=True))
        do16 = do.astype(BF16)

        qt16, kt16, q016, kc16 = (p[n].astype(BF16) for n in ("qt", "kt", "q0", "kc"))
        a = jnp.where(p["causal"], _dot_nt(qt16, kt16), 0.0)
        da = jnp.where(p["causal"], _dot_nt(do16, v16), 0.0)
        da16 = da.astype(BF16)
        dqt = _dot(da16, kt16)
        dq0 = _dot(do16, s0.astype(BF16))
        dkt = _dot_tn(da16, qt16)
        dkc = _dot(v16, ds16)
        dv_ref[rs, sl] = (_dot_tn(a.astype(BF16), do16) + _dot_nt(kc16, ds16)).astype(BF16)
        lam_end = jnp.exp(p["b_end"])
        dst[h] = lam_end * ds_end + _dot_tn(do16, q016)

        qt, kt, q0, kc = (a.astype(F32) for a in (qt16, kt16, q016, kc16))
        db = dqt * qt + dq0 * q0 - dkt * kt - dkc * kc
        db_end = (jnp.sum(dkc * kc, axis=0, keepdims=True)
                  + jnp.sum(ds_end * s0, axis=0, keepdims=True) * lam_end)
        rowi = lax.broadcasted_iota(jnp.int32, (c, dh), 0)
        db = db + jnp.where(rowi == c - 1, db_end, 0.0)
        r = lax.broadcasted_iota(jnp.int32, (c, c), 0)
        s = lax.broadcasted_iota(jnp.int32, (c, c), 1)
        dlf = _tri_dot(jnp.where(s >= r, 1.0, 0.0).astype(BF16), db)

        dqs = dqt * p["e_q"] + dq0 * p["e_0"]
        dq_ref[rs, sl] = (dqs * p["dqs"]).astype(BF16)
        dk = dkt * p["e_k"] + dkc * p["e_c"]
        sz, lb = p["sz"], p["lb"]
        common = dlf / p["f"] - dk
        dz_ref[rs, sl] = ((1.0 - lb) * sz * (1.0 - sz) * common).astype(BF16)
        dlb_ref[:, sl] += jnp.sum((1.0 - sz) * common, axis=0, keepdims=True)

    def body(*refs):
        dng_ref, dlb_ref, dst = refs[-3:]

        @pl.when(pl.program_id(0) == 0)
        def _():
            dst[...] = jnp.zeros_like(dst)
            dng_ref[...] = jnp.zeros_like(dng_ref)
            dlb_ref[...] = jnp.zeros_like(dlb_ref)

        for sub in reversed(range(subs)):
            for h in range(HGRN_HEADS):
                head(h, sub, *refs)

    def wide(off):
        return pl.BlockSpec((subs * c, HGRN_WIDTH), lambda i: (last - i, off))

    vec = pl.BlockSpec((1, HGRN_WIDTH), lambda i: (0, 0))
    act = jax.ShapeDtypeStruct((t, HGRN_WIDTH), BF16)
    vsh = jax.ShapeDtypeStruct((1, HGRN_WIDTH), F32)
    return pl.pallas_call(
        body, name="hgrn_bwd", grid=(n_chunks // subs,),
        in_specs=[wide(1), wide(2), wide(3), wide(4), wide(0),
                  pl.BlockSpec((HGRN_HEADS, subs, dh, dh), lambda i: (0, last - i, 0, 0)),
                  wide(0), pl.BlockSpec((2, HGRN_WIDTH), lambda i: (0, 0)), vec],
        out_specs=[wide(0), wide(0), wide(0), wide(0), vec, vec],
        out_shape=[act, act, act, act, vsh, vsh],
        scratch_shapes=[pltpu.VMEM((HGRN_HEADS, dh, dh), F32)],
        compiler_params=_params(("arbitrary",)),
    )(proj, proj, proj, proj, o_raw, s0s, doh, logits, ng)


def _lb_bwd(logits, dlb):
    def body(lg_ref, d_ref, o_ref):
        lg = lg_ref[...]
        lb = _sigmoid(lg[0:1, :] - lg[1:2, :])
        g = d_ref[...] * lb * (1.0 - lb)
        o_ref[0:1, :] = g
        o_ref[1:2, :] = -g

    return pl.pallas_call(body, name="hgrn_lb_bwd", out_shape=jax.ShapeDtypeStruct(logits.shape, F32),
                          compiler_params=_params())(logits, dlb)


MERGE_TC = 1024
GS_BLOCK = (S5_WIDTH + 4 * HGRN_WIDTH) // MERGE_TC
GH_BLOCK = GS_BLOCK + D_MODEL // MERGE_TC


def _merge_fwd(proj, ys, yh):
    t = proj.shape[0]
    tr = _pick(t, 256, SUBLANE)

    def body(gs_ref, gh_ref, ys_ref, yh_ref, m_ref):
        m_ref[...] = (_sigmoid(gs_ref[...]) * ys_ref[...] + _sigmoid(gh_ref[...]) * yh_ref[...]).astype(BF16)

    blk = pl.BlockSpec((tr, MERGE_TC), lambda i, j: (i, j))
    return pl.pallas_call(
        body, name="merge_fwd", grid=(t // tr, D_MODEL // MERGE_TC),
        in_specs=[pl.BlockSpec((tr, MERGE_TC), lambda i, j: (i, GS_BLOCK + j)),
                  pl.BlockSpec((tr, MERGE_TC), lambda i, j: (i, GH_BLOCK + j)), blk, blk],
        out_specs=blk, out_shape=jax.ShapeDtypeStruct((t, D_MODEL), BF16),
        compiler_params=_params(("parallel", "parallel")),
    )(proj, proj, ys, yh)


def _merge_bwd(proj, ys, yh, dm, after=None):
    t = proj.shape[0]
    tr = _pick(t, 256, SUBLANE)

    def body(gs_ref, gh_ref, ys_ref, yh_ref, dm_ref, dys_ref, dyh_ref, dgs_ref, dgh_ref):
        dmv = dm_ref[...]
        ss, sh = _sigmoid(gs_ref[...]), _sigmoid(gh_ref[...])
        dys_ref[...] = (dmv * ss).astype(BF16)
        dyh_ref[...] = (dmv * sh).astype(BF16)
        dgs_ref[...] = (dmv * ys_ref[...] * ss * (1.0 - ss)).astype(BF16)
        dgh_ref[...] = (dmv * yh_ref[...] * sh * (1.0 - sh)).astype(BF16)

    blk = pl.BlockSpec((tr, MERGE_TC), lambda i, j: (i, j))
    sh16 = jax.ShapeDtypeStruct((t, D_MODEL), BF16)
    in_specs = [pl.BlockSpec((tr, MERGE_TC), lambda i, j: (i, GS_BLOCK + j)),
                pl.BlockSpec((tr, MERGE_TC), lambda i, j: (i, GH_BLOCK + j)), blk, blk, blk]
    body, in_specs, args = _ordered(body, in_specs, [proj, proj, ys, yh, dm], after)
    return pl.pallas_call(
        body, name="merge_bwd", grid=(t // tr, D_MODEL // MERGE_TC),
        in_specs=in_specs,
        out_specs=[blk, blk, blk, blk], out_shape=[sh16, sh16, sh16, sh16],
        compiler_params=_params(("parallel", "parallel")),
    )(*args)


FFN_TC = 128
FFN_ROWS = 128
HALO = SUBLANE


def _pad_rows(dst, src_ref):
    t, c = src_ref.shape
    dst[0:HALO, :] = jnp.zeros((HALO, c), F32)
    dst[HALO:HALO + t, :] = src_ref[...]
    dst[HALO + t:HALO + t + HALO, :] = jnp.zeros((HALO, c), F32)


def _conv3(padded, w, b, r0, nrows):
    x0 = padded[HALO + r0:HALO + r0 + nrows, :]
    x1 = padded[HALO + r0 - 1:HALO + r0 - 1 + nrows, :]
    x2 = padded[HALO + r0 - 2:HALO + r0 - 2 + nrows, :]
    return b + w[0:1, :] * x2 + w[1:2, :] * x1 + w[2:3, :] * x0, (x0, x1, x2)


def _ffn_act_fwd(up, cw, cb):
    t = up.shape[0]
    rows = _pick(t, FFN_ROWS, SUBLANE)
    nvb = D_FF // FFN_TC

    def body(ug_ref, uv_ref, wg_ref, wv_ref, bg_ref, bv_ref, act_ref, pg, pv):
        wg, wv, bg, bv = wg_ref[...], wv_ref[...], bg_ref[...], bv_ref[...]
        _pad_rows(pg, ug_ref)
        _pad_rows(pv, uv_ref)
        for r0 in range(0, t, rows):
            cg, _ = _conv3(pg, wg, bg, r0, rows)
            cv, _ = _conv3(pv, wv, bv, r0, rows)
            act_ref[r0:r0 + rows, :] = (_silu_and_grad(cg)[0] * cv).astype(BF16)

    def colblk(nrow, off):
        return pl.BlockSpec((nrow, FFN_TC), lambda j: (0, off + j))

    return pl.pallas_call(
        body, name="ffn_act_fwd", grid=(nvb,),
        in_specs=[colblk(t, 0), colblk(t, nvb), colblk(3, 0), colblk(3, nvb), colblk(1, 0), colblk(1, nvb)],
        out_specs=colblk(t, 0), out_shape=jax.ShapeDtypeStruct((t, D_FF), BF16),
        scratch_shapes=[pltpu.VMEM((t + 2 * HALO, FFN_TC), F32), pltpu.VMEM((t + 2 * HALO, FFN_TC), F32)],
        compiler_params=_params(("parallel",)),
    )(up, up, cw, cw, cb, cb)


def _ffn_act_bwd(up, dact, cw, cb, after=None):
    t = up.shape[0]
    rows = _pick(t, FFN_ROWS, SUBLANE)
    nvb = D_FF // FFN_TC

    def body(ug_ref, uv_ref, da_ref, wg_ref, wv_ref, bg_ref, bv_ref,
             dug_ref, duv_ref, dwg_ref, dwv_ref, dbg_ref, dbv_ref, pg, pv, dcs):
        wg, wv, bg, bv = wg_ref[...], wv_ref[...], bg_ref[...], bv_ref[...]
        _pad_rows(pg, ug_ref)
        _pad_rows(pv, uv_ref)
        ext = rows + HALO
        acc_g = [jnp.zeros((1, FFN_TC), F32) for _ in range(4)]
        acc_v = [jnp.zeros((1, FFN_TC), F32) for _ in range(4)]
        for r0 in range(0, t, rows):
            cg, xg = _conv3(pg, wg, bg, r0, ext)
            cv, xv = _conv3(pv, wv, bv, r0, ext)
            if r0 + ext <= t:
                dav = da_ref[r0:r0 + ext, :]
            else:
                dav = jnp.concatenate([da_ref[r0:t, :], jnp.zeros((HALO, FFN_TC), F32)], axis=0)
            sg, dsg = _silu_and_grad(cg)
            for h, (dconv, xs, w, acc, out) in enumerate(((dav * cv * dsg, xg, wg, acc_g, dug_ref),
                                                           (dav * sg, xv, wv, acc_v, duv_ref))):
                dcs[h] = dconv
                d0 = dconv[0:rows, :]
                d1 = dcs[h, 1:rows + 1, :]
                d2 = dcs[h, 2:rows + 2, :]
                out[r0:r0 + rows, :] = (w[2:3, :] * d0 + w[1:2, :] * d1 + w[0:1, :] * d2).astype(BF16)
                x0, x1, x2 = xs
                acc[0] = acc[0] + jnp.sum(d0 * x2[0:rows, :], axis=0, keepdims=True)
                acc[1] = acc[1] + jnp.sum(d0 * x1[0:rows, :], axis=0, keepdims=True)
                acc[2] = acc[2] + jnp.sum(d0 * x0[0:rows, :], axis=0, keepdims=True)
                acc[3] = acc[3] + jnp.sum(d0, axis=0, keepdims=True)
        for acc, dw_ref, db_ref in ((acc_g, dwg_ref, dbg_ref), (acc_v, dwv_ref, dbv_ref)):
            dw_ref[0:1, :] = acc[0]
            dw_ref[1:2, :] = acc[1]
            dw_ref[2:3, :] = acc[2]
            db_ref[...] = acc[3]

    def colblk(nrow, off):
        return pl.BlockSpec((nrow, FFN_TC), lambda j: (0, off + j))

    in_specs = [colblk(t, 0), colblk(t, nvb), colblk(t, 0), colblk(3, 0), colblk(3, nvb), colblk(1, 0), colblk(1, nvb)]
    body, in_specs, args = _ordered(body, in_specs, [up, up, dact, cw, cw, cb, cb], after)
    return pl.pallas_call(
        body, name="ffn_act_bwd", grid=(nvb,),
        in_specs=in_specs,
        out_specs=[colblk(t, 0), colblk(t, 0), colblk(3, 0), colblk(3, 0), colblk(1, 0), colblk(1, 0)],
        out_shape=[jax.ShapeDtypeStruct((t, D_FF), BF16), jax.ShapeDtypeStruct((t, D_FF), BF16),
                   jax.ShapeDtypeStruct((3, D_FF), F32), jax.ShapeDtypeStruct((3, D_FF), F32),
                   jax.ShapeDtypeStruct((1, D_FF), F32), jax.ShapeDtypeStruct((1, D_FF), F32)],
        scratch_shapes=[pltpu.VMEM((t + 2 * HALO, FFN_TC), F32), pltpu.VMEM((t + 2 * HALO, FFN_TC), F32),
                        pltpu.VMEM((2, rows + HALO, FFN_TC), F32)],
        compiler_params=_params(("parallel",)),
    )(*args)


def _all_gather(shards, name):
    nw = len(shards)

    def body(*refs):
        x_refs, out_refs = refs[:nw], refs[nw:2 * nw]
        send_sems, recv_sems, local_sems = refs[2 * nw:]
        x, y, c = lax.axis_index("x"), lax.axis_index("y"), lax.axis_index("c")
        me, sibling = (x, y, c), (x, y, 1 - c)
        chips = [(1 - x, y), (x, 1 - y), (1 - x, 1 - y)]

        def copy(w, k, block, to, src=None):
            slot = out_refs[w].at[4 * block[0] + 2 * block[1] + block[2]]
            return pltpu.make_async_remote_copy(
                src_ref=slot if src is None else src, dst_ref=slot,
                send_sem=send_sems.at[w, k], recv_sem=recv_sems.at[w, k],
                device_id=to, device_id_type=MESH)

        mine, first, passed = [], [], []
        for w in range(nw):
            cp = pltpu.make_async_copy(x_refs[w], out_refs[w].at[4 * x + 2 * y + c], local_sems.at[w])
            cp.start()
            mine.append(cp)
            first.append(copy(w, 0, me, sibling, src=x_refs[w]))
            first += [copy(w, 1 + j, me, (*chip, c), src=x_refs[w]) for j, chip in enumerate(chips)]
        for cp in first:
            cp.start()
        for w in range(nw):
            for j, chip in enumerate(chips):
                copy(w, 1 + j, (*chip, c), me).wait_recv()
                fwd = copy(w, 4 + j, (*chip, c), sibling)
                fwd.start()
                passed.append(fwd)
        for w in range(nw):
            copy(w, 0, sibling, me).wait_recv()
            for j, chip in enumerate(chips):
                copy(w, 4 + j, (*chip, 1 - c), me).wait_recv()
        for cp in first + passed:
            cp.wait_send()
        for cp in mine:
            cp.wait()

    anyspec = pl.BlockSpec(memory_space=pl.ANY)
    return pl.pallas_call(
        body, name=name,
        in_specs=[anyspec] * nw, out_specs=[anyspec] * nw,
        out_shape=[jax.ShapeDtypeStruct((N_DEV,) + s.shape, s.dtype) for s in shards],
        scratch_shapes=[pltpu.SemaphoreType.DMA((nw, 7)), pltpu.SemaphoreType.DMA((nw, 7)),
                        pltpu.SemaphoreType.DMA((nw,))],
    )(*shards)


HBM_SPEC = pl.BlockSpec(memory_space=pltpu.HBM)
SEM_SPEC = pl.BlockSpec(memory_space=pltpu.SEMAPHORE)
ANY_SPEC = pl.BlockSpec(memory_space=pl.ANY)
DATAFLOW = pltpu.SideEffectType.DATAFLOW_SIDE_EFFECTING


def _my_index():
    return 4 * lax.axis_index("x") + 2 * lax.axis_index("y") + lax.axis_index("c")


def _peers():
    x, y, c = lax.axis_index("x"), lax.axis_index("y"), lax.axis_index("c")
    peers = []
    for k in range(1, N_DEV):
        px = 1 - x if k & 4 else x
        py = 1 - y if k & 2 else y
        pc = 1 - c if k & 1 else c
        peers.append((k, (px, py, pc), 4 * px + 2 * py + pc))
    return peers


def _split_copy(src_ref, land_ref, send_sems, recv_sems, w, k, peer, slot, scatter, outgoing):
    return pltpu.make_async_remote_copy(
        src_ref=src_ref.at[slot] if scatter else src_ref,
        dst_ref=land_ref.at[_my_index() if outgoing else slot],
        send_sem=send_sems.at[w * (N_DEV - 1) + k - 1], recv_sem=recv_sems.at[w * (N_DEV - 1) + k - 1],
        device_id=peer, device_id_type=MESH)


def _landing_zone(src, scatter):
    me = _my_index()
    own = lax.dynamic_index_in_dim(src, me, 0, keepdims=True) if scatter else src[None]
    shape = src.shape if scatter else (N_DEV,) + src.shape
    return lax.dynamic_update_slice_in_dim(lax.empty(shape, src.dtype), own, me, 0)


def _exchange_start(srcs, scatter, after, name, lands=None):
    nw = len(srcs)
    if lands is None:
        lands = [_landing_zone(s, scatter) for s in srcs]

    afters = [] if after is None else [after]

    def body(*refs):
        s_refs, l_refs = refs[:nw], refs[nw:2 * nw]
        send_sems, recv_sems = refs[2 * nw + len(afters)], refs[2 * nw + len(afters) + 1]
        token = refs[-1]
        for w in range(nw):
            for k, peer, slot in _peers():
                _split_copy(s_refs[w], l_refs[w], send_sems, recv_sems, w, k, peer, slot, scatter, True).start()
        token[...] = jnp.zeros_like(token)

    sems = pltpu.SemaphoreType.DMA((nw * (N_DEV - 1),))
    outs = pl.pallas_call(
        body, name=name,
        out_shape=(sems, sems, *[pltpu.HBM(a.shape, a.dtype) for a in (*srcs, *lands)],
                   jax.ShapeDtypeStruct((SUBLANE, LANE), F32)),
        in_specs=[HBM_SPEC] * (2 * nw) + [ANY_SPEC] * len(afters),
        out_specs=(SEM_SPEC, SEM_SPEC, *[HBM_SPEC] * (2 * nw), pl.BlockSpec(memory_space=pltpu.VMEM)),
        input_output_aliases={i: 2 + i for i in range(2 * nw)},
        compiler_params=pltpu.CompilerParams(has_side_effects=DATAFLOW),
    )(*[pltpu.with_memory_space_constraint(a, pltpu.HBM) for a in (*srcs, *lands)], *afters)
    return dict(sems=outs[:2], srcs=outs[2:2 + nw], lands=outs[2 + nw:2 + 2 * nw], token=outs[-1], scatter=scatter)


def _exchange_wait(handle, afters, name):
    srcs, lands, scatter = handle["srcs"], handle["lands"], handle["scatter"]
    nw = len(srcs)

    def body(*refs):
        s_refs, l_refs = refs[:nw], refs[nw:2 * nw]
        send_sems, recv_sems = refs[2 * nw], refs[2 * nw + 1]
        for w in range(nw):
            for k, peer, slot in _peers():
                cp = _split_copy(s_refs[w], l_refs[w], send_sems, recv_sems, w, k, peer, slot, scatter, False)
                cp.wait_send()
                cp.wait_recv()

    outs = pl.pallas_call(
        body, name=name,
        out_shape=tuple(pltpu.HBM(a.shape, a.dtype) for a in (*srcs, *lands)),
        in_specs=[HBM_SPEC] * (2 * nw) + [SEM_SPEC, SEM_SPEC] + [ANY_SPEC] * len(afters),
        out_specs=tuple([HBM_SPEC] * (2 * nw)),
        input_output_aliases={i: i for i in range(2 * nw)},
        compiler_params=pltpu.CompilerParams(has_side_effects=DATAFLOW),
    )(*srcs, *lands, *handle["sems"], *afters)
    return list(outs[nw:])


def _chips_and_sibling():
    x, y, c = lax.axis_index("x"), lax.axis_index("y"), lax.axis_index("c")
    return [(1 - x, y), (x, 1 - y), (1 - x, 1 - y)], (x, y, 1 - c), c


def _slot(px, py, pc):
    return 4 * px + 2 * py + pc


def _two_level_start(shards, name):
    nw = len(shards)
    lands = [_landing_zone(s, False) for s in shards]

    def body(*refs):
        s_refs, l_refs = refs[:nw], refs[nw:2 * nw]
        send_sems, recv_sems, token = refs[2 * nw], refs[2 * nw + 1], refs[-1]
        chips, sibling, c = _chips_and_sibling()
        for w in range(nw):
            for k, to in enumerate([sibling] + [(*chip, c) for chip in chips]):
                pltpu.make_async_remote_copy(
                    src_ref=s_refs[w], dst_ref=l_refs[w].at[_my_index()],
                    send_sem=send_sems.at[4 * w + k], recv_sem=recv_sems.at[4 * w + k],
                    device_id=to, device_id_type=MESH).start()
        token[...] = jnp.zeros_like(token)

    sems = pltpu.SemaphoreType.DMA((4 * nw,))
    outs = pl.pallas_call(
        body, name=name,
        out_shape=(sems, sems, *[pltpu.HBM(a.shape, a.dtype) for a in (*shards, *lands)],
                   jax.ShapeDtypeStruct((SUBLANE, LANE), F32)),
        in_specs=[HBM_SPEC] * (2 * nw),
        out_specs=(SEM_SPEC, SEM_SPEC, *[HBM_SPEC] * (2 * nw), pl.BlockSpec(memory_space=pltpu.VMEM)),
        input_output_aliases={i: 2 + i for i in range(2 * nw)},
        compiler_params=pltpu.CompilerParams(has_side_effects=DATAFLOW),
    )(*[pltpu.with_memory_space_constraint(a, pltpu.HBM) for a in (*shards, *lands)])
    return dict(sems=outs[:2], srcs=outs[2:2 + nw], lands=outs[2 + nw:2 + 2 * nw], token=outs[-1])


def _two_level_pass(handle, afters, name):
    srcs, lands = handle["srcs"], handle["lands"]
    nw = len(srcs)

    def body(*refs):
        s_refs, l_refs = refs[:nw], refs[nw:2 * nw]
        send_a, recv_a = refs[2 * nw], refs[2 * nw + 1]
        send_b, recv_b = refs[2 * nw + 2 + len(afters)], refs[2 * nw + 3 + len(afters)]
        chips, sibling, c = _chips_and_sibling()
        for w in range(nw):
            for j, chip in enumerate(chips):
                landed = l_refs[w].at[_slot(*chip, c)]
                pltpu.make_async_remote_copy(
                    src_ref=s_refs[w], dst_ref=landed, send_sem=send_a.at[4 * w + 1 + j], recv_sem=recv_a.at[4 * w + 1 + j],
                    device_id=(*chip, c), device_id_type=MESH).wait_recv()
                pltpu.make_async_remote_copy(
                    src_ref=landed, dst_ref=landed, send_sem=send_b.at[3 * w + j], recv_sem=recv_b.at[3 * w + j],
                    device_id=sibling, device_id_type=MESH).start()

    sems = pltpu.SemaphoreType.DMA((3 * nw,))
    outs = pl.pallas_call(
        body, name=name,
        out_shape=(sems, sems, *[pltpu.HBM(a.shape, a.dtype) for a in (*srcs, *lands)]),
        in_specs=[HBM_SPEC] * (2 * nw) + [SEM_SPEC, SEM_SPEC] + [ANY_SPEC] * len(afters),
        out_specs=(SEM_SPEC, SEM_SPEC, *[HBM_SPEC] * (2 * nw)),
        input_output_aliases={i: 2 + i for i in range(2 * nw)},
        compiler_params=pltpu.CompilerParams(has_side_effects=DATAFLOW),
    )(*srcs, *lands, *handle["sems"], *afters)
    return dict(sems=handle["sems"], sems_pass=outs[:2], srcs=outs[2:2 + nw], lands=outs[2 + nw:2 + 2 * nw])


def _two_level_wait(handle, name):
    srcs, lands = handle["srcs"], handle["lands"]
    nw = len(srcs)

    def body(*refs):
        s_refs, l_refs = refs[:nw], refs[nw:2 * nw]
        send_a, recv_a, send_b, recv_b = refs[2 * nw:2 * nw + 4]
        chips, sibling, c = _chips_and_sibling()
        x, y = sibling[0], sibling[1]
        for w in range(nw):
            first = pltpu.make_async_remote_copy(
                src_ref=s_refs[w], dst_ref=l_refs[w].at[_slot(x, y, 1 - c)], send_sem=send_a.at[4 * w],
                recv_sem=recv_a.at[4 * w], device_id=sibling, device_id_type=MESH)
            first.wait_send()
            first.wait_recv()
            for j, chip in enumerate(chips):
                pltpu.make_async_remote_copy(
                    src_ref=s_refs[w], dst_ref=l_refs[w].at[_slot(*chip, c)], send_sem=send_a.at[4 * w + 1 + j],
                    recv_sem=recv_a.at[4 * w + 1 + j], device_id=(*chip, c), device_id_type=MESH).wait_send()
                passed = pltpu.make_async_remote_copy(
                    src_ref=l_refs[w].at[_slot(*chip, c)], dst_ref=l_refs[w].at[_slot(*chip, 1 - c)],
                    send_sem=send_b.at[3 * w + j], recv_sem=recv_b.at[3 * w + j], device_id=sibling, device_id_type=MESH)
                passed.wait_send()
                passed.wait_recv()

    outs = pl.pallas_call(
        body, name=name,
        out_shape=tuple(pltpu.HBM(a.shape, a.dtype) for a in (*srcs, *lands)),
        in_specs=[HBM_SPEC] * (2 * nw) + [SEM_SPEC] * 4,
        out_specs=tuple([HBM_SPEC] * (2 * nw)),
        input_output_aliases={i: i for i in range(2 * nw)},
        compiler_params=pltpu.CompilerParams(has_side_effects=DATAFLOW),
    )(*srcs, *lands, *handle["sems"], *handle["sems_pass"])
    return list(outs[nw:])


def _adamw(w, g, m, v):
    m = ADAM_B1 * m + (1.0 - ADAM_B1) * g
    v = ADAM_B2 * v + (1.0 - ADAM_B2) * (g * g)
    m_hat = m / (1.0 - ADAM_B1 ** ADAM_STEP)
    v_hat = v / (1.0 - ADAM_B2 ** ADAM_STEP)
    delta = -ADAM_LR * (m_hat / (jnp.sqrt(v_hat) + ADAM_EPS) + ADAM_WD * w)
    return delta, m, v


def _sum_adam(parts, w, m, v, name):
    _, r, c = parts.shape
    tr = _pick(r, 128, 16)

    def body(p_ref, w_ref, m_ref, v_ref, g_ref, d_ref, mo_ref, vo_ref):
        g = p_ref[0].astype(F32)
        for s in range(1, N_DEV):
            g = g + p_ref[s].astype(F32)
        g_ref[...] = g
        d_ref[...], mo_ref[...], vo_ref[...] = _adamw(w_ref[...], g, m_ref[...], v_ref[...])

    row = pl.BlockSpec((tr, c), lambda i: (i, 0))
    sh = jax.ShapeDtypeStruct((r, c), F32)
    return pl.pallas_call(
        body, name=name, grid=(r // tr,),
        in_specs=[pl.BlockSpec((N_DEV, tr, c), lambda i: (0, i, 0)), row, row, row],
        out_specs=[row, row, row, row], out_shape=[sh, sh, sh, sh],
        compiler_params=_params(("parallel",)),
    )(parts, w, m, v)


def _sum_slots(parts, name):
    _, r, c = parts.shape
    tr = _pick(r, 512, SUBLANE)

    def body(p_ref, o_ref):
        g = p_ref[0]
        for s in range(1, N_DEV):
            g = g + p_ref[s]
        o_ref[...] = g

    return pl.pallas_call(
        body, name=name, grid=(r // tr,),
        in_specs=[pl.BlockSpec((N_DEV, tr, c), lambda i: (0, i, 0))],
        out_specs=pl.BlockSpec((tr, c), lambda i: (i, 0)), out_shape=jax.ShapeDtypeStruct((r, c), F32),
        compiler_params=_params(("parallel",)),
    )(parts)


def _adam_rows(g, w, m, v, name):
    r, c = g.shape
    tr = _pick(r, 512, SUBLANE)

    def body(g_ref, w_ref, m_ref, v_ref, d_ref, mo_ref, vo_ref):
        d_ref[...], mo_ref[...], vo_ref[...] = _adamw(w_ref[...], g_ref[...], m_ref[...], v_ref[...])

    row = pl.BlockSpec((tr, c), lambda i: (i, 0))
    sh = jax.ShapeDtypeStruct((r, c), F32)
    return pl.pallas_call(body, name=name, grid=(r // tr,), in_specs=[row] * 4, out_specs=[row] * 3,
                          out_shape=[sh, sh, sh], compiler_params=_params(("parallel",)))(g, w, m, v)


def _pack(arrays):
    flat = jnp.concatenate([a.reshape(-1).astype(F32) for a in arrays])
    pad = (-flat.shape[0]) % (SUBLANE * LANE)
    return jnp.pad(flat, (0, pad)).reshape(-1, LANE)


def _unpack(packed, shapes):
    flat = packed.reshape(-1)
    out, off = [], 0
    for s in shapes:
        n = math.prod(s)
        out.append(flat[off:off + n].reshape(s))
        off += n
    return out


def _block_diag(t):
    eye = jnp.eye(S5_SUPER, dtype=bool)
    bd = jnp.where(eye[None, :, None, :, None], t[:, :, :, None, :], 0.0)
    return bd.reshape(S5_SUPER, S5_SUPER * t.shape[2], S5_SUPER * t.shape[3])


def _diag_blocks(dense, a, b):
    x = dense.reshape(S5_SUPER, S5_SUPER, a, S5_SUPER, b)
    return jnp.moveaxis(jnp.diagonal(x, axis1=1, axis2=3), -1, 1)


def _s5_layouts(b_re, b_im, c_re, c_im, d):
    g2 = (S5_GROUPS // S5_SUPER, S5_SUPER)
    bt = lambda b: _block_diag(b.reshape(*g2, S5_STATE, S5_GROUP).transpose(0, 1, 3, 2))
    ct = lambda c: _block_diag(c.reshape(*g2, S5_GROUP, S5_STATE).transpose(0, 1, 3, 2))
    bsg = jnp.concatenate([bt(b_re), bt(b_im)], axis=2).astype(BF16)
    ccat = jnp.concatenate([ct(c_re), -ct(c_im)], axis=1).astype(BF16)
    return bsg, ccat, d.reshape(1, S5_WIDTH)


def _s5_param_grads(gb, gc):
    n = S5_LANES
    gb_re = _diag_blocks(gb[:, :, 0:n], S5_GROUP, S5_STATE).transpose(0, 1, 3, 2).reshape(S5_GROUPS, S5_STATE, S5_GROUP)
    gb_im = _diag_blocks(gb[:, :, n:2 * n], S5_GROUP, S5_STATE).transpose(0, 1, 3, 2).reshape(S5_GROUPS, S5_STATE, S5_GROUP)
    gc_re = _diag_blocks(gc[:, 0:n, :], S5_STATE, S5_GROUP).transpose(0, 1, 3, 2).reshape(S5_GROUPS, S5_GROUP, S5_STATE)
    gc_im = -_diag_blocks(gc[:, n:2 * n, :], S5_STATE, S5_GROUP).transpose(0, 1, 3, 2).reshape(S5_GROUPS, S5_GROUP, S5_STATE)
    return gb_re, gb_im, gc_re, gc_im


def _local_step(x, target, weight, emit, small, after=None):
    sp = small
    a_re, a_im = sp["s5_a_re"], sp["s5_a_im"]
    ldt = sp["s5_log_dt"].reshape(S5_GROUPS, 1)

    h1 = _rms_fwd(x, sp["ln_mix_g"], "rms_mix", after=after)
    w_in = weight("w_in", h1)
    proj = _mm_nn(h1, w_in, "mm_in", after=weight("after_w_in", None))
    conv_w = weight("conv_w", None)
    disc = _s5_param_fwd(a_re, a_im, ldt)
    bsg, ccat, d_row = sp["s5_layouts"]
    abar_t, coef_t = _s5_to_tile(disc[0], disc[1]), _s5_to_tile(disc[2], disc[3])
    y, sb = _s5_fwd(proj, bsg, ccat, d_row, abar_t, coef_t)
    z16 = _gelu_fwd(y)
    w_glu = weight("s5_w_glu", z16)
    gl = _mm_nn(z16, w_glu, "mm_glu")
    z2 = _glu_fwd(y, gl, sp["s5_b_glu"])
    w_ps = weight("w_proj_s5", z2)
    ys = _mm_nn(z2, w_ps, "mm_proj_s5")
    o_raw, oh, s0s = _hgrn_fwd(proj, sp["hgrn_lb_logits"], sp["hgrn_norm_g"])
    w_ph = weight("w_proj_hgrn", oh)
    yh = _mm_nn(oh, w_ph, "mm_proj_hgrn")
    merged = _merge_fwd(proj, ys, yh)
    w_out = weight("w_out", merged)
    x1 = _mm_nn(merged, w_out, "mm_out", res=x)
    h2 = _rms_fwd(x1, sp["ln_ffn_g"], "rms_ffn")
    w_up = weight("w_up", h2)
    up = _mm_nn(h2, w_up, "mm_up")
    act = _ffn_act_fwd(up, conv_w, sp["conv_b"])
    w_down = weight("w_down", act)
    x2 = _mm_nn(act, w_down, "mm_down", res=x1)
    dx2, dx2_16, g_ln_final, loss = _loss_head(x2, sp["ln_final_g"], target)

    dact = _mm_nt(dx2_16, w_down, "mm_down_dx")
    tok = emit("w_down", _mm_tn(act, dx2_16, 1, "mm_down_dw"))
    dup_g, dup_v, dcw_g, dcw_v, dcb_g, dcb_v = _ffn_act_bwd(up, dact, conv_w, sp["conv_b"], after=tok)
    dup = jnp.concatenate([dup_g, dup_v], axis=1)
    g_conv_w = jnp.concatenate([dcw_g, dcw_v], axis=1)
    g_conv_b = jnp.concatenate([dcb_g, dcb_v], axis=1)
    dh2 = _mm_nt(dup, w_up, "mm_up_dx")
    tok = emit("w_up", _mm_tn(h2, dup, N_DEV, "mm_up_dw"))
    dx1, dx1_16, g_ln_ffn = _rms_bwd(x1, sp["ln_ffn_g"], dh2, dx2, "rms_ffn_bwd", True, after=tok)

    dmerged = _mm_nt(dx1_16, w_out, "mm_out_dx")
    tok = emit("w_out", _mm_tn(merged, dx1_16, 1, "mm_out_dw"))
    dys, dyh, dgs, dgh = _merge_bwd(proj, ys, yh, dmerged, after=tok)
    doh = _mm_nt(dyh, w_ph, "mm_proj_hgrn_dx")
    tok = emit("w_proj_hgrn", _mm_tn(oh, dyh, N_DEV, "mm_proj_hgrn_dw"))
    dz2 = _mm_nt(dys, w_ps, "mm_proj_s5_dx", after=tok)
    tok = emit("w_proj_s5", _mm_tn(z2, dys, N_DEV, "mm_proj_s5_dw"))
    dgl, dza, g_b_glu = _glu_bwd(y, gl, sp["s5_b_glu"], dz2, after=tok)
    dzb = _mm_nt(dgl, w_glu, "mm_glu_dx")
    tok = emit("s5_w_glu", _mm_tn(z16, dgl, 1, "mm_glu_dw"))
    dy = _gelu_bwd(y, dza, dzb, after=tok)
    du, gb, gc, gd, g_abar_t, g_coef_t = _s5_bwd(proj, dy, sb, bsg, ccat, d_row, abar_t, coef_t)
    g_a_re, g_a_im, g_ldt = _s5_param_bwd(a_re, a_im, ldt, [*_s5_from_tile(g_abar_t), *_s5_from_tile(g_coef_t)])
    g_b_re, g_b_im, g_c_re, g_c_im = _s5_param_grads(gb, gc)
    dq, dz, dv, dg, g_norm, dlb = _hgrn_bwd(proj, o_raw, s0s, doh, sp["hgrn_lb_logits"], sp["hgrn_norm_g"])
    g_logits = _lb_bwd(sp["hgrn_lb_logits"], dlb)

    small_g = dict(s5_a_re=g_a_re, s5_a_im=g_a_im, s5_log_dt=g_ldt.reshape(1, S5_GROUPS),
                   s5_b_re=g_b_re, s5_b_im=g_b_im, s5_c_re=g_c_re, s5_c_im=g_c_im,
                   s5_d=gd.reshape(S5_GROUPS, S5_GROUP), s5_b_glu=g_b_glu, hgrn_lb_logits=g_logits,
                   hgrn_norm_g=g_norm, ln_ffn_g=g_ln_ffn, conv_w=g_conv_w, conv_b=g_conv_b, ln_final_g=g_ln_final,
                   loss=loss[0, 0:1])
    tok_small = emit("small", small_g)

    dproj = jnp.concatenate([du, dq, dz, dv, dg, dgs, dgh], axis=1)
    tok = emit("w_in", _mm_tn(h1, dproj, N_DEV, "mm_in_dw", after=tok_small))
    dh1 = _mm_nt(dproj, w_in, "mm_in_dx")
    grad_x, g_ln_mix = _rms_bwd(x, sp["ln_mix_g"], dh1, dx1, "rms_mix_bwd", False, after=tok)
    return grad_x, g_ln_mix


BIG = ("w_in", "s5_w_glu", "w_proj_s5", "w_proj_hgrn", "w_out", "w_up", "w_down")
COL_SHARDED = ("w_in", "w_proj_s5", "w_proj_hgrn", "w_up")
SMALL = ("ln_mix_g", "s5_a_re", "s5_a_im", "s5_log_dt", "s5_b_re", "s5_b_im", "s5_c_re", "s5_c_im", "s5_d",
         "s5_b_glu", "hgrn_lb_logits", "hgrn_norm_g", "ln_ffn_g", "conv_b", "ln_final_g")
WEIGHTS = ("ln_mix_g", "w_in", "s5_a_re", "s5_a_im", "s5_log_dt", "s5_b_re", "s5_b_im", "s5_c_re", "s5_c_im", "s5_d",
           "s5_w_glu", "s5_b_glu", "w_proj_s5", "hgrn_lb_logits", "hgrn_norm_g", "w_proj_hgrn", "w_out", "ln_ffn_g",
           "w_up", "conv_w", "conv_b", "w_down", "ln_final_g")


def kernel(x, ln_mix_g, w_in, s5_a_re, s5_a_im, s5_log_dt, s5_b_re, s5_b_im, s5_c_re, s5_c_im, s5_d, s5_w_glu, s5_b_glu, w_proj_s5, hgrn_lb_logits, hgrn_norm_g, w_proj_hgrn, w_out, ln_ffn_g, w_up, conv_w, conv_b, w_down, ln_final_g, loss_target, m_ln_mix_g, m_w_in, m_s5_a_re, m_s5_a_im, m_s5_log_dt, m_s5_b_re, m_s5_b_im, m_s5_c_re, m_s5_c_im, m_s5_d, m_s5_w_glu, m_s5_b_glu, m_w_proj_s5, m_hgrn_lb_logits, m_hgrn_norm_g, m_w_proj_hgrn, m_w_out, m_ln_ffn_g, m_w_up, m_conv_w, m_conv_b, m_w_down, m_ln_final_g, v_ln_mix_g, v_w_in, v_s5_a_re, v_s5_a_im, v_s5_log_dt, v_s5_b_re, v_s5_b_im, v_s5_c_re, v_s5_c_im, v_s5_d, v_s5_w_glu, v_s5_b_glu, v_w_proj_s5, v_hgrn_lb_logits, v_hgrn_norm_g, v_w_proj_hgrn, v_w_out, v_ln_ffn_g, v_w_up, v_conv_w, v_conv_b, v_w_down, v_ln_final_g):
    given = dict(locals())
    w = {n: given[n] for n in WEIGHTS}
    mom = {n: given["m_" + n] for n in WEIGHTS}
    var = {n: given["v_" + n] for n in WEIGHTS}

    first = _two_level_start([w_in[0].astype(BF16), conv_w[0]], "gather_first_start")
    zero = first["token"][0, 0]
    packed_small = SMALL[1:]
    pw, pm, pv = (_pack([d[n] for n in packed_small]) + zero for d in (w, mom, var))
    layouts = _s5_layouts(s5_b_re[0] + zero, s5_b_im[0], s5_c_re[0] + zero, s5_c_im[0], s5_d[0])
    gather_groups = (("s5_w_glu", "w_proj_s5", "w_proj_hgrn", "w_out"), ("w_up",), ("w_down",))
    shard16 = {n: w[n][0].astype(BF16) + zero.astype(BF16) for g in gather_groups for n in g}
    zones = {n: _landing_zone(s, False) for n, s in shard16.items()}
    pending, ready = {}, {}

    def weight(name, after):
        if "w_in" not in ready:
            local_work = [after, pw, pm, pv, layouts[0], layouts[1], *zones.values()]
            passed = _two_level_pass(first, local_work, "gather_first_pass")
            ready["w_in"], conv_w_all = _two_level_wait(passed, "gather_first_wait")
            ready["conv_w"] = conv_w_all.transpose(1, 0, 2).reshape(3, 2 * D_FF)
            token = ready["w_in"]
            for i, group in enumerate(gather_groups):
                handle = _exchange_start([shard16[n] for n in group], False, token, f"gather_start_{i}",
                                         lands=[zones[n] for n in group])
                token = handle["token"]
                for n in group:
                    pending[n] = (group, handle, f"gather_wait_{i}")
            ready["after_w_in"] = token
        if name not in ready:
            group, handle, wait_name = pending[name]
            for n, g in zip(group, _exchange_wait(handle, [after], wait_name)):
                ready[n] = g
        g = ready[name]
        return g if name not in BIG or name in COL_SHARDED else g.reshape(1, N_DEV * g.shape[1], g.shape[2])

    scatter_groups = (("w_down",), ("w_up",), ("w_out", "w_proj_hgrn", "w_proj_s5", "s5_w_glu"), ("w_in",))
    emitted, scatters = {}, []
    packed_names = SMALL[1:] + ("conv_w", "loss")

    def emit(name, grad):
        if name == "small":
            emitted[name] = ([grad[n].shape for n in packed_names],
                             _exchange_start([_pack([grad[n] for n in packed_names])], False, None, "small_start"))
            return emitted[name][1]["token"]
        emitted[name] = grad if name in COL_SHARDED else grad.reshape(N_DEV, -1, grad.shape[2])
        group = scatter_groups[len(scatters)]
        if not all(n in emitted for n in group):
            return None
        handle = _exchange_start([emitted[n] for n in group], True, None, f"scatter_start_{len(scatters)}")
        scatters.append((group, handle))
        return handle["token"]

    small = dict(ln_mix_g=ln_mix_g, s5_a_re=s5_a_re[0], s5_a_im=s5_a_im[0], s5_log_dt=s5_log_dt, s5_layouts=layouts,
                 s5_b_glu=s5_b_glu, hgrn_lb_logits=hgrn_lb_logits, hgrn_norm_g=hgrn_norm_g, ln_ffn_g=ln_ffn_g,
                 conv_b=conv_b, ln_final_g=ln_final_g.reshape(1, D_MODEL))
    grad_x, g_ln_mix = _local_step(x[0], loss_target[0], weight, emit, small, after=first["token"])

    shapes, handle = emitted["small"]
    total = _sum_slots(_exchange_wait(handle, [grad_x], "small_wait")[0], "sum_small")
    summed = dict(zip(packed_names, _unpack(total, shapes)))
    mix_all = _all_gather([g_ln_mix.reshape(-1, LANE)], "gather_ln_mix")[0]
    summed["ln_mix_g"] = _sum_slots(mix_all, "sum_ln_mix").reshape(1, D_MODEL)

    grads, delta, new_m, new_v = {}, {}, {}, {}
    afters = [grad_x, total]
    for i, (group, handle) in enumerate(scatters):
        for n, r in zip(group, _exchange_wait(handle, afters, f"scatter_wait_{i}")):
            g, d, m2, v2 = _sum_adam(r, w[n][0], mom[n][0], var[n][0], "adam_" + n)
            grads[n], delta[n], new_m[n], new_v[n] = g[None], d[None], m2[None], v2[None]
        if i == len(scatters) - 2:
            afters = [delta[n] for g2, _ in scatters[:-1] for n in g2]

    d_s, m_s, v_s = _adam_rows(_pack([summed[n] for n in packed_small]), pw, pm, pv, "adam_small")
    wshapes = [w[n].shape for n in packed_small]
    for n, d, m2, v2 in zip(packed_small, _unpack(d_s, wshapes), _unpack(m_s, wshapes), _unpack(v_s, wshapes)):
        grads[n], delta[n], new_m[n], new_v[n] = summed[n].reshape(w[n].shape), d, m2, v2
    grads["ln_mix_g"] = summed["ln_mix_g"]
    delta["ln_mix_g"], new_m["ln_mix_g"], new_v["ln_mix_g"] = _adam_rows(summed["ln_mix_g"], ln_mix_g, m_ln_mix_g,
                                                                         v_ln_mix_g, "adam_ln_mix")
    me = 4 * lax.axis_index("x") + 2 * lax.axis_index("y") + lax.axis_index("c")
    ncol = conv_w.shape[2]
    g_cw = lax.dynamic_slice_in_dim(summed["conv_w"], me * ncol, ncol, axis=1)
    d_cw, m_cw, v_cw = _adam_rows(g_cw, conv_w[0], m_conv_w[0], v_conv_w[0], "adam_conv_w")
    grads["conv_w"], delta["conv_w"], new_m["conv_w"], new_v["conv_w"] = g_cw[None], d_cw[None], m_cw[None], v_cw[None]

    return (summed["loss"].reshape(()), grad_x[None], *[grads[n] for n in WEIGHTS], *[delta[n] for n in WEIGHTS],
            *[new_m[n] for n in WEIGHTS], *[new_v[n] for n in WEIGHTS])
```

```python
import math

import jax
import jax.numpy as jnp
from jax import lax
from jax.experimental import pallas as pl
from jax.experimental.pallas import tpu as pltpu

F32 = jnp.float32
BF16 = jnp.bfloat16

N_DEV = 8
D_MODEL = 2048
S5_WIDTH = 1024
S5_GROUP = 16
S5_GROUPS = 64
S5_STATE = 64
S5_MAX_RE = -1e-4
S5_SUPER = 8
S5_LANES = S5_SUPER * S5_STATE
HGRN_WIDTH = 1024
HGRN_HEADS = 8
HGRN_DH = 128
HGRN_CHUNK = 64
HGRN_SUBS = 4
D_FF = 5632
RMS_EPS = 1e-6
ADAM_LR = 0.001
ADAM_B1 = 0.9
ADAM_B2 = 0.999
ADAM_EPS = 1e-08
ADAM_WD = 0.01
ADAM_STEP = 10

LANE = 128
SUBLANE = 8
VMEM_LIMIT = 48 * 1024 * 1024
MESH = pl.DeviceIdType.MESH
GELU_C = math.sqrt(2.0 / math.pi)
GELU_A = 0.044715


def _params(sem=None):
    return pltpu.CompilerParams(dimension_semantics=sem, vmem_limit_bytes=VMEM_LIMIT)


def _pick(n, cap, unit=LANE):
    best = None
    for t in range(unit, min(n, cap) + 1, unit):
        if n % t == 0:
            best = t
    return best if best is not None else n


def _ordered(body, in_specs, args, after):
    if after is None:
        return body, list(in_specs), list(args)
    n_in = len(args)

    def ordered_body(*refs):
        return body(*refs[:n_in], *refs[n_in + 1:])

    return ordered_body, [*in_specs, pl.BlockSpec(memory_space=pl.ANY)], [*args, after]


def _sigmoid(x):
    return 0.5 * jnp.tanh(0.5 * x) + 0.5


def _silu_and_grad(x):
    s = _sigmoid(x)
    return x * s, s * (1.0 + x * (1.0 - s))


def _gelu_and_grad(y):
    inner = GELU_C * (y + GELU_A * y * y * y)
    th = jnp.tanh(inner)
    val = 0.5 * y * (1.0 + th)
    grad = 0.5 * (1.0 + th) + 0.5 * y * (1.0 - th * th) * GELU_C * (1.0 + 3.0 * GELU_A * y * y)
    return val, grad


def _dot(a, b):
    return jnp.dot(a, b, preferred_element_type=F32)


def _dot_nt(a, b):
    return lax.dot_general(a, b, (((1,), (1,)), ((), ())), preferred_element_type=F32)


def _dot_tn(a, b):
    return lax.dot_general(a, b, (((0,), (0,)), ((), ())), preferred_element_type=F32)


def _blocks_per_step(nb, ns, tn, cap=2048):
    if tn != ns:
        return 1
    best = 1
    for b in range(1, nb + 1):
        if nb % b == 0 and b * ns <= cap:
            best = b
    return best


NN_TILE_BYTES = 42 * 1024 * 1024


def _mm_nn(a, w, name, res=None, out_dtype=F32, after=None):
    m, kdim = a.shape
    nb, _, ns = w.shape
    tk, tn = _pick(kdim, D_FF), _pick(ns, 1536)
    npb, nk = ns // tn, kdim // tk
    bps = _blocks_per_step(nb, ns, tn)
    assert bps == 1 or nk == 1

    def buffers(rows):
        return 2 * (rows * tk * 2 + bps * tk * tn * 2 + rows * bps * tn * 4 * (2 if res is not None else 1))

    tm = next((r for r in (_pick(m, 1024), _pick(m, 512)) if buffers(r) <= NN_TILE_BYTES), _pick(m, 256))

    def body(*refs):
        a_ref, w_ref = refs[0], refs[1]
        r_ref = refs[2] if res is not None else None
        o_ref = refs[3] if res is not None else refs[2]

        def finish(r, cols):
            if res is not None:
                r = r + r_ref[:, cols]
            o_ref[:, cols] = r.astype(out_dtype)

        if nk == 1:
            for b in range(bps):
                finish(_dot(a_ref[...], w_ref[b]), slice(b * tn, (b + 1) * tn))
            return
        acc = refs[-1]
        k = pl.program_id(2)

        @pl.when(k == 0)
        def _():
            acc[...] = jnp.zeros_like(acc)

        acc[...] += _dot(a_ref[...], w_ref[0])

        @pl.when(k == nk - 1)
        def _():
            finish(acc[...], slice(0, tn))

    in_specs = [pl.BlockSpec((tm, tk), lambda j, i, k: (i, k)),
                pl.BlockSpec((bps, tk, tn), lambda j, i, k: (j // npb, k, j % npb))]
    args = [a, w]
    if res is not None:
        in_specs.append(pl.BlockSpec((tm, bps * tn), lambda j, i, k: (i, j)))
        args.append(res)
    body, in_specs, args = _ordered(body, in_specs, args, after)
    return pl.pallas_call(
        body, name=name, grid=(nb * npb // bps, m // tm, nk),
        in_specs=in_specs, out_specs=pl.BlockSpec((tm, bps * tn), lambda j, i, k: (i, j)),
        out_shape=jax.ShapeDtypeStruct((m, nb * ns), out_dtype),
        scratch_shapes=[pltpu.VMEM((tm, tn), F32)] if nk > 1 else [],
        compiler_params=_params(("parallel", "parallel", "arbitrary")),
    )(*args)


NT_STEP_COLS = 2816


def _mm_nt(a, w, name, out_dtype=F32, after=None):
    m, _ = a.shape
    nb, kdim, ns = w.shape
    tm, tko, tn = _pick(m, 1024), _pick(kdim, 1024), _pick(ns, 2048)
    npb = ns // tn
    bps = _blocks_per_step(nb, ns, tn, cap=NT_STEP_COLS)
    nred = nb * npb // bps

    def body(a_ref, w_ref, o_ref, *scratch):
        total = _dot_nt(a_ref[:, 0:tn], w_ref[0])
        for b in range(1, bps):
            total = total + _dot_nt(a_ref[:, b * tn:(b + 1) * tn], w_ref[b])
        if nred == 1:
            o_ref[...] = total.astype(out_dtype)
            return
        acc = scratch[0]
        n = pl.program_id(2)

        @pl.when(n == 0)
        def _():
            acc[...] = jnp.zeros_like(acc)

        acc[...] += total

        @pl.when(n == nred - 1)
        def _():
            o_ref[...] = acc[...].astype(out_dtype)

    in_specs = [pl.BlockSpec((tm, bps * tn), lambda i, j, n: (i, n)),
                pl.BlockSpec((bps, tko, tn), lambda i, j, n: (n // npb, j, n % npb))]
    body, in_specs, args = _ordered(body, in_specs, [a, w], after)
    return pl.pallas_call(
        body, name=name, grid=(m // tm, kdim // tko, nred),
        in_specs=in_specs,
        out_specs=pl.BlockSpec((tm, tko), lambda i, j, n: (i, j)),
        out_shape=jax.ShapeDtypeStruct((m, kdim), out_dtype),
        scratch_shapes=[pltpu.VMEM((tm, tko), F32)] if nred > 1 else [],
        compiler_params=_params(("parallel", "parallel", "arbitrary")),
    )(*args)


def _mm_tn(a, d, nb, name, out_dtype=BF16, after=None):
    m, kdim = a.shape
    ns = d.shape[1] // nb
    tm, tko, tn = _pick(m, 4096), _pick(kdim, 512), _pick(ns, 1536)
    npb, nm = ns // tn, m // tm
    bps = _blocks_per_step(nb, ns, tn, cap=1536)
    assert bps == 1 or nm == 1

    def body(a_ref, d_ref, o_ref, *scratch):
        if nm == 1:
            for b in range(bps):
                o_ref[b] = _dot_tn(a_ref[...], d_ref[:, b * tn:(b + 1) * tn]).astype(out_dtype)
            return
        acc = scratch[0]
        r = pl.program_id(2)

        @pl.when(r == 0)
        def _():
            acc[...] = jnp.zeros_like(acc)

        acc[...] += _dot_tn(a_ref[...], d_ref[...])

        @pl.when(r == nm - 1)
        def _():
            o_ref[0] = acc[...].astype(out_dtype)

    in_specs = [pl.BlockSpec((tm, tko), lambda j, i, r: (r, i)), pl.BlockSpec((tm, bps * tn), lambda j, i, r: (r, j))]
    body, in_specs, args = _ordered(body, in_specs, [a, d], after)
    return pl.pallas_call(
        body, name=name, grid=(nb * npb // bps, kdim // tko, nm),
        in_specs=in_specs,
        out_specs=pl.BlockSpec((bps, tko, tn), lambda j, i, r: (j // npb, i, j % npb)),
        out_shape=jax.ShapeDtypeStruct((nb, kdim, ns), out_dtype),
        scratch_shapes=[pltpu.VMEM((tko, tn), F32)] if nm > 1 else [],
        compiler_params=_params(("parallel", "parallel", "arbitrary")),
    )(*args)


def _rms_fwd(x, g, name, after=None):
    t, d = x.shape
    tr = _pick(t, 256, SUBLANE)

    def body(x_ref, g_ref, h_ref):
        xv = x_ref[...]
        r = lax.rsqrt(jnp.mean(xv * xv, axis=-1, keepdims=True) + RMS_EPS)
        h_ref[...] = (xv * r * g_ref[...]).astype(BF16)

    in_specs = [pl.BlockSpec((tr, d), lambda i: (i, 0)), pl.BlockSpec((1, d), lambda i: (0, 0))]
    body, in_specs, args = _ordered(body, in_specs, [x, g], after)
    return pl.pallas_call(
        body, name=name, grid=(t // tr,),
        in_specs=in_specs,
        out_specs=pl.BlockSpec((tr, d), lambda i: (i, 0)),
        out_shape=jax.ShapeDtypeStruct((t, d), BF16),
        compiler_params=_params(("parallel",)),
    )(*args)


def _rms_bwd(x, g, dh, add, name, want_bf16, after=None):
    t, d = x.shape
    tr = _pick(t, 256, SUBLANE)

    def body(x_ref, g_ref, dh_ref, add_ref, *outs):
        if want_bf16:
            dx_ref, dxb_ref, dg_ref = outs
        else:
            dx_ref, dg_ref = outs
        i = pl.program_id(0)

        @pl.when(i == 0)
        def _():
            dg_ref[...] = jnp.zeros_like(dg_ref)

        xv, dhv = x_ref[...], dh_ref[...]
        r = lax.rsqrt(jnp.mean(xv * xv, axis=-1, keepdims=True) + RMS_EPS)
        xh = xv * r
        dg_ref[...] += jnp.sum(dhv * xh, axis=0, keepdims=True)
        dxh = dhv * g_ref[...]
        dx = add_ref[...] + r * (dxh - xh * jnp.mean(dxh * xh, axis=-1, keepdims=True))
        dx_ref[...] = dx
        if want_bf16:
            dxb_ref[...] = dx.astype(BF16)

    row = pl.BlockSpec((tr, d), lambda i: (i, 0))
    vec = pl.BlockSpec((1, d), lambda i: (0, 0))
    out_specs = [row] + ([row] if want_bf16 else []) + [vec]
    out_shape = ([jax.ShapeDtypeStruct((t, d), F32)] + ([jax.ShapeDtypeStruct((t, d), BF16)] if want_bf16 else [])
                 + [jax.ShapeDtypeStruct((1, d), F32)])
    body, in_specs, args = _ordered(body, [row, vec, row, row], [x, g, dh, add], after)
    return pl.pallas_call(
        body, name=name, grid=(t // tr,),
        in_specs=in_specs, out_specs=out_specs, out_shape=out_shape,
        compiler_params=_params(("arbitrary",)),
    )(*args)


def _loss_head(x2, g, target, name="loss_head"):
    t, d = x2.shape
    tr = _pick(t, 256, SUBLANE)

    def body(x_ref, g_ref, t_ref, dx_ref, dxb_ref, dg_ref, loss_ref):
        i = pl.program_id(0)

        @pl.when(i == 0)
        def _():
            dg_ref[...] = jnp.zeros_like(dg_ref)
            loss_ref[...] = jnp.zeros_like(loss_ref)

        xv = x_ref[...]
        gv = g_ref[...]
        r = lax.rsqrt(jnp.mean(xv * xv, axis=-1, keepdims=True) + RMS_EPS)
        xh = xv * r
        err = xh * gv - t_ref[...]
        part = 0.5 * jnp.sum(jnp.mean(err * err, axis=-1, keepdims=True), axis=0, keepdims=True)
        loss_ref[...] += jnp.broadcast_to(part, loss_ref.shape)
        dy = err * (1.0 / d)
        dg_ref[...] += jnp.sum(dy * xh, axis=0, keepdims=True)
        dxh = dy * gv
        dx = r * (dxh - xh * jnp.mean(dxh * xh, axis=-1, keepdims=True))
        dx_ref[...] = dx
        dxb_ref[...] = dx.astype(BF16)

    row = pl.BlockSpec((tr, d), lambda i: (i, 0))
    vec = pl.BlockSpec((1, d), lambda i: (0, 0))
    return pl.pallas_call(
        body, name=name, grid=(t // tr,),
        in_specs=[row, vec, row],
        out_specs=[row, row, vec, pl.BlockSpec((1, LANE), lambda i: (0, 0))],
        out_shape=[jax.ShapeDtypeStruct((t, d), F32), jax.ShapeDtypeStruct((t, d), BF16),
                   jax.ShapeDtypeStruct((1, d), F32), jax.ShapeDtypeStruct((1, LANE), F32)],
        compiler_params=_params(("arbitrary",)),
    )(x2, g, target)


def _s5_discretize(a_re, a_im, ldt):
    lam_re = jnp.minimum(a_re, S5_MAX_RE)
    lam_im = a_im
    dt = jnp.exp(ldt)
    mag = jnp.exp(lam_re * dt)
    abar_re = mag * jnp.cos(lam_im * dt)
    abar_im = mag * jnp.sin(lam_im * dt)
    den = lam_re * lam_re + lam_im * lam_im
    nr = abar_re - 1.0
    ni = abar_im
    coef_re = (nr * lam_re + ni * lam_im) / den
    coef_im = (ni * lam_re - nr * lam_im) / den
    return abar_re, abar_im, coef_re, coef_im


def _s5_param_fwd(a_re, a_im, ldt):
    def body(ar_ref, ai_ref, l_ref, o0, o1, o2, o3):
        outs = _s5_discretize(ar_ref[...], ai_ref[...], l_ref[...])
        for o, v in zip((o0, o1, o2, o3), outs):
            o[...] = v

    sh = jax.ShapeDtypeStruct(a_re.shape, F32)
    return pl.pallas_call(body, name="s5_param_fwd", out_shape=[sh, sh, sh, sh], compiler_params=_params())(a_re, a_im, ldt)


def _s5_param_bwd(a_re, a_im, ldt, cts):
    def body(ar_ref, ai_ref, l_ref, c0, c1, c2, c3, g0, g1, g2):
        _, vjp = jax.vjp(_s5_discretize, ar_ref[...], ai_ref[...], l_ref[...])
        ga, gb, gl = vjp((c0[...], c1[...], c2[...], c3[...]))
        g0[...] = ga
        g1[...] = gb
        g2[...] = gl

    sh = jax.ShapeDtypeStruct(a_re.shape, F32)
    return pl.pallas_call(body, name="s5_param_bwd", out_shape=[sh, sh, jax.ShapeDtypeStruct(ldt.shape, F32)],
                          compiler_params=_params())(a_re, a_im, ldt, *cts)


def _cmul(ar, ai, br, bi):
    return ar * br - ai * bi, ar * bi + ai * br


S5_TC = 128
S5_TILE = S5_SUPER * SUBLANE
S5_HALF = S5_TILE // 2


def _s5_to_tile(re, im):
    f = lambda a: a.reshape(S5_SUPER, S5_LANES // LANE, LANE).transpose(1, 0, 2).reshape(S5_HALF, LANE)
    return jnp.concatenate([f(re), f(im)], axis=0)


def _s5_from_tile(tile):
    f = lambda a: a.reshape(S5_LANES // LANE, S5_SUPER, LANE).transpose(1, 0, 2).reshape(S5_GROUPS, S5_STATE)
    return f(tile[0:S5_HALF]), f(tile[S5_HALF:])


RE = slice(0, S5_HALF)
IM = slice(S5_HALF, S5_TILE)


def _s5_scatter_rows(buf, rows, first_tile=0):
    tc = rows[0].shape[0]
    for j in range(SUBLANE):
        stacked = jnp.stack([r[:, j * LANE:(j + 1) * LANE] for r in rows], axis=0)
        buf[first_tile:first_tile + tc, j * SUBLANE:(j + 1) * SUBLANE, :] = jnp.swapaxes(stacked, 0, 1)


def _s5_gather_rows(buf, tc, first_tile=0):
    per_j = [jnp.swapaxes(buf[first_tile:first_tile + tc, j * SUBLANE:(j + 1) * SUBLANE, :], 0, 1)
             for j in range(SUBLANE)]
    return [jnp.concatenate([per_j[j][k] for j in range(SUBLANE)], axis=1) for k in range(S5_SUPER)]


def _s5_fwd(proj, bsg, ccat, d_row, abar_t, coef_t):
    t = proj.shape[0]
    tc = min(t, S5_TC)
    n_chunks = t // tc

    def body(u_ref, b_ref, c_ref, d_ref, a_ref, cf_ref, y_ref, sb_ref, x, car):
        @pl.when(pl.program_id(0) == 0)
        def _():
            car[...] = jnp.zeros_like(car)

        sb_ref[...] = car[...]
        u = u_ref[...]
        _s5_scatter_rows(x, [_dot(u[:, k * LANE:(k + 1) * LANE].astype(BF16), b_ref[k]) for k in range(S5_SUPER)])
        ar, ai = a_ref[RE, :], a_ref[IM, :]
        cr, ci = cf_ref[RE, :], cf_ref[IM, :]

        def step(i, carry):
            sr, si = carry
            xr, xi = _cmul(cr, ci, x[i, RE, :], x[i, IM, :])
            sr, si = ar * sr - ai * si + xr, ar * si + ai * sr + xi
            x[i, RE, :] = sr
            x[i, IM, :] = si
            return sr, si

        sr, si = lax.fori_loop(0, tc, step, (car[RE, :], car[IM, :]), unroll=4)
        car[RE, :] = sr
        car[IM, :] = si
        for k, s_k in enumerate(_s5_gather_rows(x, tc)):
            cols = slice(k * LANE, (k + 1) * LANE)
            y_ref[:, cols] = _dot(s_k.astype(BF16), c_ref[k]) + d_ref[:, cols] * u[:, cols]

    full = lambda shape: pl.BlockSpec(shape, lambda c: (0,) * len(shape))
    return pl.pallas_call(
        body, name="s5_fwd", grid=(n_chunks,),
        in_specs=[pl.BlockSpec((tc, S5_WIDTH), lambda c: (c, 0)), full(bsg.shape), full(ccat.shape), full(d_row.shape),
                  full(abar_t.shape), full(coef_t.shape)],
        out_specs=[pl.BlockSpec((tc, S5_WIDTH), lambda c: (c, 0)), pl.BlockSpec((None, S5_TILE, LANE), lambda c: (c, 0, 0))],
        out_shape=[jax.ShapeDtypeStruct((t, S5_WIDTH), F32), jax.ShapeDtypeStruct((n_chunks, S5_TILE, LANE), F32)],
        scratch_shapes=[pltpu.VMEM((tc, S5_TILE, LANE), F32), pltpu.VMEM((S5_TILE, LANE), F32)],
        compiler_params=_params(("arbitrary",)),
    )(proj, bsg, ccat, d_row, abar_t, coef_t)


def _s5_bwd(proj, dy, sb, bsg, ccat, d_row, abar_t, coef_t):
    t = proj.shape[0]
    tc = min(t, S5_TC)
    n_chunks = t // tc
    last = n_chunks - 1

    def body(u_ref, dy_ref, sb_ref, b_ref, c_ref, d_ref, a_ref, cf_ref,
             du_ref, gb_ref, gc_ref, gd_ref, ga_ref, gcf_ref, xb, xs, xg, gcar, acc):
        @pl.when(pl.program_id(0) == 0)
        def _():
            gcar[...] = jnp.zeros_like(gcar)
            acc[...] = jnp.zeros_like(acc)
            gb_ref[...] = jnp.zeros_like(gb_ref)
            gc_ref[...] = jnp.zeros_like(gc_ref)
            gd_ref[...] = jnp.zeros_like(gd_ref)

        u = u_ref[...]
        dyv = dy_ref[...]
        u16, dy16 = u.astype(BF16), dyv.astype(BF16)
        subs = [slice(k * LANE, (k + 1) * LANE) for k in range(S5_SUPER)]
        _s5_scatter_rows(xb, [_dot(u16[:, c], b_ref[k]) for k, c in enumerate(subs)])
        _s5_scatter_rows(xg, [_dot_nt(dy16[:, c], c_ref[k]) for k, c in enumerate(subs)])
        ar, ai = a_ref[RE, :], a_ref[IM, :]
        cr, ci = cf_ref[RE, :], cf_ref[IM, :]

        xs[0] = sb_ref[...]

        def fstep(i, carry):
            sr, si = carry
            xr, xi = _cmul(cr, ci, xb[i, RE, :], xb[i, IM, :])
            sr, si = ar * sr - ai * si + xr, ar * si + ai * sr + xi
            xs[i + 1, RE, :] = sr
            xs[i + 1, IM, :] = si
            return sr, si

        lax.fori_loop(0, tc, fstep, (sb_ref[RE, :], sb_ref[IM, :]), unroll=4)

        def rstep(n, carry):
            gr, gi, a0, a1, a2, a3 = carry
            i = tc - 1 - n
            xr = xg[i, RE, :] + ar * gr + ai * gi
            xi = xg[i, IM, :] + ar * gi - ai * gr
            pr, pi = xs[i, RE, :], xs[i, IM, :]
            br, bi = xb[i, RE, :], xb[i, IM, :]
            a0 = a0 + pr * xr + pi * xi
            a1 = a1 + pr * xi - pi * xr
            a2 = a2 + br * xr + bi * xi
            a3 = a3 + br * xi - bi * xr
            xg[i, RE, :] = cr * xr + ci * xi
            xg[i, IM, :] = cr * xi - ci * xr
            return xr, xi, a0, a1, a2, a3

        init = (gcar[RE, :], gcar[IM, :], acc[0], acc[1], acc[2], acc[3])
        gr, gi, a0, a1, a2, a3 = lax.fori_loop(0, tc, rstep, init, unroll=2)
        gcar[RE, :] = gr
        gcar[IM, :] = gi
        for idx, a in enumerate((a0, a1, a2, a3)):
            acc[idx] = a
        ga_ref[RE, :] = a0
        ga_ref[IM, :] = a1
        gcf_ref[RE, :] = a2
        gcf_ref[IM, :] = a3

        g_rows = _s5_gather_rows(xg, tc)
        s_rows = _s5_gather_rows(xs, tc, first_tile=1)
        for k in range(S5_SUPER):
            cols = subs[k]
            g16 = g_rows[k].astype(BF16)
            s16 = s_rows[k].astype(BF16)
            gb_ref[k] += _dot_tn(u16[:, cols], g16)
            gc_ref[k] += _dot_tn(s16, dy16[:, cols])
            du_ref[:, cols] = (_dot_nt(g16, b_ref[k]) + d_ref[:, cols] * dyv[:, cols]).astype(BF16)
        gd_ref[...] += jnp.sum(dyv * u, axis=0, keepdims=True)

    full = lambda shape: pl.BlockSpec(shape, lambda c: (0,) * len(shape))
    rows = pl.BlockSpec((tc, S5_WIDTH), lambda c: (last - c, 0))
    tile = (S5_TILE, LANE)
    return pl.pallas_call(
        body, name="s5_bwd", grid=(n_chunks,),
        in_specs=[rows, rows, pl.BlockSpec((None, S5_TILE, LANE), lambda c: (last - c, 0, 0)),
                  full(bsg.shape), full(ccat.shape), full(d_row.shape), full(abar_t.shape), full(coef_t.shape)],
        out_specs=[rows, full(bsg.shape), full(ccat.shape), full(d_row.shape), full(tile), full(tile)],
        out_shape=[jax.ShapeDtypeStruct((t, S5_WIDTH), BF16), jax.ShapeDtypeStruct(bsg.shape, F32),
                   jax.ShapeDtypeStruct(ccat.shape, F32), jax.ShapeDtypeStruct(d_row.shape, F32),
                   jax.ShapeDtypeStruct(tile, F32), jax.ShapeDtypeStruct(tile, F32)],
        scratch_shapes=[pltpu.VMEM((tc, S5_TILE, LANE), F32), pltpu.VMEM((tc + 1, S5_TILE, LANE), F32),
                        pltpu.VMEM((tc, S5_TILE, LANE), F32), pltpu.VMEM(tile, F32),
                        pltpu.VMEM((4, S5_HALF, LANE), F32)],
        compiler_params=_params(("arbitrary",)),
    )(proj, dy, sb, bsg, ccat, d_row, abar_t, coef_t)


def _gelu_fwd(y, name="s5_gelu"):
    t, w = y.shape
    tr = _pick(t, 512, SUBLANE)

    def body(y_ref, z_ref):
        z_ref[...] = _gelu_and_grad(y_ref[...])[0].astype(BF16)

    row = pl.BlockSpec((tr, w), lambda i: (i, 0))
    return pl.pallas_call(body, name=name, grid=(t // tr,), in_specs=[row], out_specs=row,
                          out_shape=jax.ShapeDtypeStruct((t, w), BF16), compiler_params=_params(("parallel",)))(y)


def _glu_fwd(y, gl, b, name="s5_glu"):
    t, w = y.shape
    tr = _pick(t, 512, SUBLANE)

    def body(y_ref, gl_ref, b_ref, z2_ref):
        z = _gelu_and_grad(y_ref[...])[0]
        z2_ref[...] = (z * _sigmoid(gl_ref[...] + b_ref[...])).astype(BF16)

    row = pl.BlockSpec((tr, w), lambda i: (i, 0))
    return pl.pallas_call(body, name=name, grid=(t // tr,),
                          in_specs=[row, row, pl.BlockSpec((1, w), lambda i: (0, 0))], out_specs=row,
                          out_shape=jax.ShapeDtypeStruct((t, w), BF16), compiler_params=_params(("parallel",)))(y, gl, b)


def _glu_bwd(y, gl, b, dz2, name="s5_glu_bwd", after=None):
    t, w = y.shape
    tr = _pick(t, 512, SUBLANE)

    def body(y_ref, gl_ref, b_ref, dz2_ref, dgl_ref, dza_ref, db_ref):
        @pl.when(pl.program_id(0) == 0)
        def _():
            db_ref[...] = jnp.zeros_like(db_ref)

        z = _gelu_and_grad(y_ref[...])[0]
        s = _sigmoid(gl_ref[...] + b_ref[...])
        dz2v = dz2_ref[...]
        dgl = dz2v * z * s * (1.0 - s)
        dgl_ref[...] = dgl.astype(BF16)
        dza_ref[...] = dz2v * s
        db_ref[...] += jnp.sum(dgl, axis=0, keepdims=True)

    row = pl.BlockSpec((tr, w), lambda i: (i, 0))
    vec = pl.BlockSpec((1, w), lambda i: (0, 0))
    body, in_specs, args = _ordered(body, [row, row, vec, row], [y, gl, b, dz2], after)
    return pl.pallas_call(body, name=name, grid=(t // tr,), in_specs=in_specs, out_specs=[row, row, vec],
                          out_shape=[jax.ShapeDtypeStruct((t, w), BF16), jax.ShapeDtypeStruct((t, w), F32),
                                     jax.ShapeDtypeStruct((1, w), F32)],
                          compiler_params=_params(("arbitrary",)))(*args)


def _gelu_bwd(y, dza, dzb, name="s5_gelu_bwd", after=None):
    t, w = y.shape
    tr = _pick(t, 512, SUBLANE)

    def body(y_ref, a_ref, b_ref, dy_ref):
        dy_ref[...] = (a_ref[...] + b_ref[...]) * _gelu_and_grad(y_ref[...])[1]

    row = pl.BlockSpec((tr, w), lambda i: (i, 0))
    body, in_specs, args = _ordered(body, [row, row, row], [y, dza, dzb], after)
    return pl.pallas_call(body, name=name, grid=(t // tr,), in_specs=in_specs, out_specs=row,
                          out_shape=jax.ShapeDtypeStruct((t, w), F32), compiler_params=_params(("parallel",)))(*args)


def _tri_dot(tri16, x):
    hi = x.astype(BF16)
    lo = (x - hi.astype(F32)).astype(BF16)
    return _dot(tri16, hi) + _dot(tri16, lo)


def _hgrn_pre(q_in, z, lg):
    lb = _sigmoid(lg[0:1, :] - lg[1:2, :])
    qs, dqs = _silu_and_grad(q_in)
    sz = _sigmoid(z)
    f = lb + (1.0 - lb) * sz
    k = (1.0 - lb) * (1.0 - sz)
    c = HGRN_CHUNK
    r = lax.broadcasted_iota(jnp.int32, (c, c), 0)
    s = lax.broadcasted_iota(jnp.int32, (c, c), 1)
    causal = r >= s
    b = _tri_dot(jnp.where(causal, 1.0, 0.0).astype(BF16), jnp.log(f))
    b_end = b[c - 1:c, :]
    b_mid = b[c // 2 - 1:c // 2, :]
    e_q, e_k, e_0, e_c = jnp.exp(b - b_mid), jnp.exp(b_mid - b), jnp.exp(b), jnp.exp(b_end - b)
    return dict(lb=lb, qs=qs, dqs=dqs, sz=sz, f=f, k=k, causal=causal, b_end=b_end,
                e_q=e_q, e_k=e_k, e_0=e_0, e_c=e_c,
                qt=qs * e_q, kt=k * e_k, q0=qs * e_0, kc=k * e_c)


def _hgrn_fwd(proj, logits, ng):
    t = proj.shape[0]
    c, dh = HGRN_CHUNK, HGRN_DH
    n_chunks = t // c
    subs = HGRN_SUBS if n_chunks % HGRN_SUBS == 0 else 1

    def head(h, sub, q_ref, z_ref, v_ref, g_ref, lg_ref, ng_ref, o_ref, oh_ref, s0_ref, st):
        sl = slice(h * dh, (h + 1) * dh)
        rs = slice(sub * c, (sub + 1) * c)
        s0 = st[h]
        s0_ref[h, sub] = s0
        p = _hgrn_pre(q_ref[rs, sl], z_ref[rs, sl], lg_ref[:, sl])
        v16 = v_ref[rs, sl].astype(BF16)
        a = jnp.where(p["causal"], _dot_nt(p["qt"].astype(BF16), p["kt"].astype(BF16)), 0.0)
        o = _dot_nt(p["q0"].astype(BF16), s0.astype(BF16)) + _dot(a.astype(BF16), v16)
        st[h] = jnp.exp(p["b_end"]) * s0 + _dot_tn(v16, p["kc"].astype(BF16))
        o_ref[rs, sl] = o
        rn = lax.rsqrt(jnp.mean(o * o, axis=-1, keepdims=True) + RMS_EPS)
        oh_ref[rs, sl] = (o * rn * ng_ref[:, sl] * _silu_and_grad(g_ref[rs, sl])[0]).astype(BF16)

    def body(*refs):
        st = refs[-1]

        @pl.when(pl.program_id(0) == 0)
        def _():
            st[...] = jnp.zeros_like(st)

        for sub in range(subs):
            for h in range(HGRN_HEADS):
                head(h, sub, *refs)

    def wide(off):
        return pl.BlockSpec((subs * c, HGRN_WIDTH), lambda i: (i, off))

    return pl.pallas_call(
        body, name="hgrn_fwd", grid=(n_chunks // subs,),
        in_specs=[wide(1), wide(2), wide(3), wide(4),
                  pl.BlockSpec((2, HGRN_WIDTH), lambda i: (0, 0)), pl.BlockSpec((1, HGRN_WIDTH), lambda i: (0, 0))],
        out_specs=[wide(0), wide(0), pl.BlockSpec((HGRN_HEADS, subs, dh, dh), lambda i: (0, i, 0, 0))],
        out_shape=[jax.ShapeDtypeStruct((t, HGRN_WIDTH), F32), jax.ShapeDtypeStruct((t, HGRN_WIDTH), BF16),
                   jax.ShapeDtypeStruct((HGRN_HEADS, n_chunks, dh, dh), F32)],
        scratch_shapes=[pltpu.VMEM((HGRN_HEADS, dh, dh), F32)],
        compiler_params=_params(("arbitrary",)),
    )(proj, proj, proj, proj, logits, ng)


def _hgrn_bwd(proj, o_raw, s0s, doh, logits, ng):
    t = proj.shape[0]
    c, dh = HGRN_CHUNK, HGRN_DH
    n_chunks = t // c
    subs = HGRN_SUBS if n_chunks % HGRN_SUBS == 0 else 1
    last = n_chunks // subs - 1

    def head(h, sub, q_ref, z_ref, v_ref, g_ref, o_ref, s0_ref, doh_ref, lg_ref, ng_ref,
             dq_ref, dz_ref, dv_ref, dg_ref, dng_ref, dlb_ref, dst):
        sl = slice(h * dh, (h + 1) * dh)
        rs = slice(sub * c, (sub + 1) * c)
        p = _hgrn_pre(q_ref[rs, sl], z_ref[rs, sl], lg_ref[:, sl])
        v = v_ref[rs, sl]
        v16 = v.astype(BF16)
        s0 = s0_ref[h, sub]
        ds_end = dst[h]
        ds16 = ds_end.astype(BF16)
        ngv = ng_ref[:, sl]

        o = o_ref[rs, sl]
        dohv = doh_ref[rs, sl]
        sg, dsg = _silu_and_grad(g_ref[rs, sl])
        rn = lax.rsqrt(jnp.mean(o * o, axis=-1, keepdims=True) + RMS_EPS)
        oh = o * rn
        dg_ref[rs, sl] = (dohv * oh * ngv * dsg).astype(BF16)
        don = dohv * sg
        dng_ref[:, sl] += jnp.sum(don * oh, axis=0, keepdims=True)
        doh_n = don * ngv
        do = rn * (doh_n - oh * jnp.mean(doh_n * oh, axis=-1, keepdims=True))
        do16 = do.astype(BF16)

        qt16, kt16, q016, kc16 = (p[n].astype(BF16) for n in ("qt", "kt", "q0", "kc"))
        a = jnp.where(p["causal"], _dot_nt(qt16, kt16), 0.0)
        da = jnp.where(p["causal"], _dot_nt(do16, v16), 0.0)
        da16 = da.astype(BF16)
        dqt = _dot(da16, kt16)
        dq0 = _dot(do16, s0.astype(BF16))
        dkt = _dot_tn(da16, qt16)
        dkc = _dot(v16, ds16)
        dv_ref[rs, sl] = (_dot_tn(a.astype(BF16), do16) + _dot_nt(kc16, ds16)).astype(BF16)
        lam_end = jnp.exp(p["b_end"])
        dst[h] = lam_end * ds_end + _dot_tn(do16, q016)

        qt, kt, q0, kc = (a.astype(F32) for a in (qt16, kt16, q016, kc16))
        db = dqt * qt + dq0 * q0 - dkt * kt - dkc * kc
        db_end = (jnp.sum(dkc * kc, axis=0, keepdims=True)
                  + jnp.sum(ds_end * s0, axis=0, keepdims=True) * lam_end)
        rowi = lax.broadcasted_iota(jnp.int32, (c, dh), 0)
        db = db + jnp.where(rowi == c - 1, db_end, 0.0)
        r = lax.broadcasted_iota(jnp.int32, (c, c), 0)
        s = lax.broadcasted_iota(jnp.int32, (c, c), 1)
        dlf = _tri_dot(jnp.where(s >= r, 1.0, 0.0).astype(BF16), db)

        dqs = dqt * p["e_q"] + dq0 * p["e_0"]
        dq_ref[rs, sl] = (dqs * p["dqs"]).astype(BF16)
        dk = dkt * p["e_k"] + dkc * p["e_c"]
        sz, lb = p["sz"], p["lb"]
        common = dlf / p["f"] - dk
        dz_ref[rs, sl] = ((1.0 - lb) * sz * (1.0 - sz) * common).astype(BF16)
        dlb_ref[:, sl] += jnp.sum((1.0 - sz) * common, axis=0, keepdims=True)

    def body(*refs):
        dng_ref, dlb_ref, dst = refs[-3:]

        @pl.when(pl.program_id(0) == 0)
        def _():
            dst[...] = jnp.zeros_like(dst)
            dng_ref[...] = jnp.zeros_like(dng_ref)
            dlb_ref[...] = jnp.zeros_like(dlb_ref)

        for sub in reversed(range(subs)):
            for h in range(HGRN_HEADS):
                head(h, sub, *refs)

    def wide(off):
        return pl.BlockSpec((subs * c, HGRN_WIDTH), lambda i: (last - i, off))

    vec = pl.BlockSpec((1, HGRN_WIDTH), lambda i: (0, 0))
    act = jax.ShapeDtypeStruct((t, HGRN_WIDTH), BF16)
    vsh = jax.ShapeDtypeStruct((1, HGRN_WIDTH), F32)
    return pl.pallas_call(
        body, name="hgrn_bwd", grid=(n_chunks // subs,),
        in_specs=[wide(1), wide(2), wide(3), wide(4), wide(0),
                  pl.BlockSpec((HGRN_HEADS, subs, dh, dh), lambda i: (0, last - i, 0, 0)),
                  wide(0), pl.BlockSpec((2, HGRN_WIDTH), lambda i: (0, 0)), vec],
        out_specs=[wide(0), wide(0), wide(0), wide(0), vec, vec],
        out_shape=[act, act, act, act, vsh, vsh],
        scratch_shapes=[pltpu.VMEM((HGRN_HEADS, dh, dh), F32)],
        compiler_params=_params(("arbitrary",)),
    )(proj, proj, proj, proj, o_raw, s0s, doh, logits, ng)


def _lb_bwd(logits, dlb):
    def body(lg_ref, d_ref, o_ref):
        lg = lg_ref[...]
        lb = _sigmoid(lg[0:1, :] - lg[1:2, :])
        g = d_ref[...] * lb * (1.0 - lb)
        o_ref[0:1, :] = g
        o_ref[1:2, :] = -g

    return pl.pallas_call(body, name="hgrn_lb_bwd", out_shape=jax.ShapeDtypeStruct(logits.shape, F32),
                          compiler_params=_params())(logits, dlb)


MERGE_TC = 1024
GS_BLOCK = (S5_WIDTH + 4 * HGRN_WIDTH) // MERGE_TC
GH_BLOCK = GS_BLOCK + D_MODEL // MERGE_TC


def _merge_fwd(proj, ys, yh):
    t = proj.shape[0]
    tr = _pick(t, 256, SUBLANE)

    def body(gs_ref, gh_ref, ys_ref, yh_ref, m_ref):
        m_ref[...] = (_sigmoid(gs_ref[...]) * ys_ref[...] + _sigmoid(gh_ref[...]) * yh_ref[...]).astype(BF16)

    blk = pl.BlockSpec((tr, MERGE_TC), lambda i, j: (i, j))
    return pl.pallas_call(
        body, name="merge_fwd", grid=(t // tr, D_MODEL // MERGE_TC),
        in_specs=[pl.BlockSpec((tr, MERGE_TC), lambda i, j: (i, GS_BLOCK + j)),
                  pl.BlockSpec((tr, MERGE_TC), lambda i, j: (i, GH_BLOCK + j)), blk, blk],
        out_specs=blk, out_shape=jax.ShapeDtypeStruct((t, D_MODEL), BF16),
        compiler_params=_params(("parallel", "parallel")),
    )(proj, proj, ys, yh)


def _merge_bwd(proj, ys, yh, dm, after=None):
    t = proj.shape[0]
    tr = _pick(t, 256, SUBLANE)

    def body(gs_ref, gh_ref, ys_ref, yh_ref, dm_ref, dys_ref, dyh_ref, dgs_ref, dgh_ref):
        dmv = dm_ref[...]
        ss, sh = _sigmoid(gs_ref[...]), _sigmoid(gh_ref[...])
        dys_ref[...] = (dmv * ss).astype(BF16)
        dyh_ref[...] = (dmv * sh).astype(BF16)
        dgs_ref[...] = (dmv * ys_ref[...] * ss * (1.0 - ss)).astype(BF16)
        dgh_ref[...] = (dmv * yh_ref[...] * sh * (1.0 - sh)).astype(BF16)

    blk = pl.BlockSpec((tr, MERGE_TC), lambda i, j: (i, j))
    sh16 = jax.ShapeDtypeStruct((t, D_MODEL), BF16)
    in_specs = [pl.BlockSpec((tr, MERGE_TC), lambda i, j: (i, GS_BLOCK + j)),
                pl.BlockSpec((tr, MERGE_TC), lambda i, j: (i, GH_BLOCK + j)), blk, blk, blk]
    body, in_specs, args = _ordered(body, in_specs, [proj, proj, ys, yh, dm], after)
    return pl.pallas_call(
        body, name="merge_bwd", grid=(t // tr, D_MODEL // MERGE_TC),
        in_specs=in_specs,
        out_specs=[blk, blk, blk, blk], out_shape=[sh16, sh16, sh16, sh16],
        compiler_params=_params(("parallel", "parallel")),
    )(*args)


FFN_TC = 128
FFN_ROWS = 128
HALO = SUBLANE


def _pad_rows(dst, src_ref):
    t, c = src_ref.shape
    dst[0:HALO, :] = jnp.zeros((HALO, c), F32)
    dst[HALO:HALO + t, :] = src_ref[...]
    dst[HALO + t:HALO + t + HALO, :] = jnp.zeros((HALO, c), F32)


def _conv3(padded, w, b, r0, nrows):
    x0 = padded[HALO + r0:HALO + r0 + nrows, :]
    x1 = padded[HALO + r0 - 1:HALO + r0 - 1 + nrows, :]
    x2 = padded[HALO + r0 - 2:HALO + r0 - 2 + nrows, :]
    return b + w[0:1, :] * x2 + w[1:2, :] * x1 + w[2:3, :] * x0, (x0, x1, x2)


def _ffn_act_fwd(up, cw, cb):
    t = up.shape[0]
    rows = _pick(t, FFN_ROWS, SUBLANE)
    nvb = D_FF // FFN_TC

    def body(ug_ref, uv_ref, wg_ref, wv_ref, bg_ref, bv_ref, act_ref, pg, pv):
        wg, wv, bg, bv = wg_ref[...], wv_ref[...], bg_ref[...], bv_ref[...]
        _pad_rows(pg, ug_ref)
        _pad_rows(pv, uv_ref)
        for r0 in range(0, t, rows):
            cg, _ = _conv3(pg, wg, bg, r0, rows)
            cv, _ = _conv3(pv, wv, bv, r0, rows)
            act_ref[r0:r0 + rows, :] = (_silu_and_grad(cg)[0] * cv).astype(BF16)

    def colblk(nrow, off):
        return pl.BlockSpec((nrow, FFN_TC), lambda j: (0, off + j))

    return pl.pallas_call(
        body, name="ffn_act_fwd", grid=(nvb,),
        in_specs=[colblk(t, 0), colblk(t, nvb), colblk(3, 0), colblk(3, nvb), colblk(1, 0), colblk(1, nvb)],
        out_specs=colblk(t, 0), out_shape=jax.ShapeDtypeStruct((t, D_FF), BF16),
        scratch_shapes=[pltpu.VMEM((t + 2 * HALO, FFN_TC), F32), pltpu.VMEM((t + 2 * HALO, FFN_TC), F32)],
        compiler_params=_params(("parallel",)),
    )(up, up, cw, cw, cb, cb)


def _ffn_act_bwd(up, dact, cw, cb, after=None):
    t = up.shape[0]
    rows = _pick(t, FFN_ROWS, SUBLANE)
    nvb = D_FF // FFN_TC

    def body(ug_ref, uv_ref, da_ref, wg_ref, wv_ref, bg_ref, bv_ref,
             dug_ref, duv_ref, dwg_ref, dwv_ref, dbg_ref, dbv_ref, pg, pv, dcs):
        wg, wv, bg, bv = wg_ref[...], wv_ref[...], bg_ref[...], bv_ref[...]
        _pad_rows(pg, ug_ref)
        _pad_rows(pv, uv_ref)
        ext = rows + HALO
        acc_g = [jnp.zeros((1, FFN_TC), F32) for _ in range(4)]
        acc_v = [jnp.zeros((1, FFN_TC), F32) for _ in range(4)]
        for r0 in range(0, t, rows):
            cg, xg = _conv3(pg, wg, bg, r0, ext)
            cv, xv = _conv3(pv, wv, bv, r0, ext)
            if r0 + ext <= t:
                dav = da_ref[r0:r0 + ext, :]
            else:
                dav = jnp.concatenate([da_ref[r0:t, :], jnp.zeros((HALO, FFN_TC), F32)], axis=0)
            sg, dsg = _silu_and_grad(cg)
            for h, (dconv, xs, w, acc, out) in enumerate(((dav * cv * dsg, xg, wg, acc_g, dug_ref),
                                                           (dav * sg, xv, wv, acc_v, duv_ref))):
                dcs[h] = dconv
                d0 = dconv[0:rows, :]
                d1 = dcs[h, 1:rows + 1, :]
                d2 = dcs[h, 2:rows + 2, :]
                out[r0:r0 + rows, :] = (w[2:3, :] * d0 + w[1:2, :] * d1 + w[0:1, :] * d2).astype(BF16)
                x0, x1, x2 = xs
                acc[0] = acc[0] + jnp.sum(d0 * x2[0:rows, :], axis=0, keepdims=True)
                acc[1] = acc[1] + jnp.sum(d0 * x1[0:rows, :], axis=0, keepdims=True)
                acc[2] = acc[2] + jnp.sum(d0 * x0[0:rows, :], axis=0, keepdims=True)
                acc[3] = acc[3] + jnp.sum(d0, axis=0, keepdims=True)
        for acc, dw_ref, db_ref in ((acc_g, dwg_ref, dbg_ref), (acc_v, dwv_ref, dbv_ref)):
            dw_ref[0:1, :] = acc[0]
            dw_ref[1:2, :] = acc[1]
            dw_ref[2:3, :] = acc[2]
            db_ref[...] = acc[3]

    def colblk(nrow, off):
        return pl.BlockSpec((nrow, FFN_TC), lambda j: (0, off + j))

    in_specs = [colblk(t, 0), colblk(t, nvb), colblk(t, 0), colblk(3, 0), colblk(3, nvb), colblk(1, 0), colblk(1, nvb)]
    body, in_specs, args = _ordered(body, in_specs, [up, up, dact, cw, cw, cb, cb], after)
    return pl.pallas_call(
        body, name="ffn_act_bwd", grid=(nvb,),
        in_specs=in_specs,
        out_specs=[colblk(t, 0), colblk(t, 0), colblk(3, 0), colblk(3, 0), colblk(1, 0), colblk(1, 0)],
        out_shape=[jax.ShapeDtypeStruct((t, D_FF), BF16), jax.ShapeDtypeStruct((t, D_FF), BF16),
                   jax.ShapeDtypeStruct((3, D_FF), F32), jax.ShapeDtypeStruct((3, D_FF), F32),
                   jax.ShapeDtypeStruct((1, D_FF), F32), jax.ShapeDtypeStruct((1, D_FF), F32)],
        scratch_shapes=[pltpu.VMEM((t + 2 * HALO, FFN_TC), F32), pltpu.VMEM((t + 2 * HALO, FFN_TC), F32),
                        pltpu.VMEM((2, rows + HALO, FFN_TC), F32)],
        compiler_params=_params(("parallel",)),
    )(*args)


def _all_gather(shards, name):
    nw = len(shards)

    def body(*refs):
        x_refs, out_refs = refs[:nw], refs[nw:2 * nw]
        send_sems, recv_sems, local_sems = refs[2 * nw:]
        x, y, c = lax.axis_index("x"), lax.axis_index("y"), lax.axis_index("c")
        me, sibling = (x, y, c), (x, y, 1 - c)
        chips = [(1 - x, y), (x, 1 - y), (1 - x, 1 - y)]

        def copy(w, k, block, to, src=None):
            slot = out_refs[w].at[4 * block[0] + 2 * block[1] + block[2]]
            return pltpu.make_async_remote_copy(
                src_ref=slot if src is None else src, dst_ref=slot,
                send_sem=send_sems.at[w, k], recv_sem=recv_sems.at[w, k],
                device_id=to, device_id_type=MESH)

        mine, first, passed = [], [], []
        for w in range(nw):
            cp = pltpu.make_async_copy(x_refs[w], out_refs[w].at[4 * x + 2 * y + c], local_sems.at[w])
            cp.start()
            mine.append(cp)
            first.append(copy(w, 0, me, sibling, src=x_refs[w]))
            first += [copy(w, 1 + j, me, (*chip, c), src=x_refs[w]) for j, chip in enumerate(chips)]
        for cp in first:
            cp.start()
        for w in range(nw):
            for j, chip in enumerate(chips):
                copy(w, 1 + j, (*chip, c), me).wait_recv()
                fwd = copy(w, 4 + j, (*chip, c), sibling)
                fwd.start()
                passed.append(fwd)
        for w in range(nw):
            copy(w, 0, sibling, me).wait_recv()
            for j, chip in enumerate(chips):
                copy(w, 4 + j, (*chip, 1 - c), me).wait_recv()
        for cp in first + passed:
            cp.wait_send()
        for cp in mine:
            cp.wait()

    anyspec = pl.BlockSpec(memory_space=pl.ANY)
    return pl.pallas_call(
        body, name=name,
        in_specs=[anyspec] * nw, out_specs=[anyspec] * nw,
        out_shape=[jax.ShapeDtypeStruct((N_DEV,) + s.shape, s.dtype) for s in shards],
        scratch_shapes=[pltpu.SemaphoreType.DMA((nw, 7)), pltpu.SemaphoreType.DMA((nw, 7)),
                        pltpu.SemaphoreType.DMA((nw,))],
    )(*shards)


HBM_SPEC = pl.BlockSpec(memory_space=pltpu.HBM)
SEM_SPEC = pl.BlockSpec(memory_space=pltpu.SEMAPHORE)
ANY_SPEC = pl.BlockSpec(memory_space=pl.ANY)
DATAFLOW = pltpu.SideEffectType.DATAFLOW_SIDE_EFFECTING


def _my_index():
    return 4 * lax.axis_index("x") + 2 * lax.axis_index("y") + lax.axis_index("c")


def _peers():
    x, y, c = lax.axis_index("x"), lax.axis_index("y"), lax.axis_index("c")
    peers = []
    for k in range(1, N_DEV):
        px = 1 - x if k & 4 else x
        py = 1 - y if k & 2 else y
        pc = 1 - c if k & 1 else c
        peers.append((k, (px, py, pc), 4 * px + 2 * py + pc))
    return peers


def _split_copy(src_ref, land_ref, send_sems, recv_sems, w, k, peer, slot, scatter, outgoing):
    return pltpu.make_async_remote_copy(
        src_ref=src_ref.at[slot] if scatter else src_ref,
        dst_ref=land_ref.at[_my_index() if outgoing else slot],
        send_sem=send_sems.at[w * (N_DEV - 1) + k - 1], recv_sem=recv_sems.at[w * (N_DEV - 1) + k - 1],
        device_id=peer, device_id_type=MESH)


def _landing_zone(src, scatter):
    me = _my_index()
    own = lax.dynamic_index_in_dim(src, me, 0, keepdims=True) if scatter else src[None]
    shape = src.shape if scatter else (N_DEV,) + src.shape
    return lax.dynamic_update_slice_in_dim(lax.empty(shape, src.dtype), own, me, 0)


def _exchange_start(srcs, scatter, after, name, lands=None):
    nw = len(srcs)
    if lands is None:
        lands = [_landing_zone(s, scatter) for s in srcs]

    afters = [] if after is None else [after]

    def body(*refs):
        s_refs, l_refs = refs[:nw], refs[nw:2 * nw]
        send_sems, recv_sems = refs[2 * nw + len(afters)], refs[2 * nw + len(afters) + 1]
        token = refs[-1]
        for w in range(nw):
            for k, peer, slot in _peers():
                _split_copy(s_refs[w], l_refs[w], send_sems, recv_sems, w, k, peer, slot, scatter, True).start()
        token[...] = jnp.zeros_like(token)

    sems = pltpu.SemaphoreType.DMA((nw * (N_DEV - 1),))
    outs = pl.pallas_call(
        body, name=name,
        out_shape=(sems, sems, *[pltpu.HBM(a.shape, a.dtype) for a in (*srcs, *lands)],
                   jax.ShapeDtypeStruct((SUBLANE, LANE), F32)),
        in_specs=[HBM_SPEC] * (2 * nw) + [ANY_SPEC] * len(afters),
        out_specs=(SEM_SPEC, SEM_SPEC, *[HBM_SPEC] * (2 * nw), pl.BlockSpec(memory_space=pltpu.VMEM)),
        input_output_aliases={i: 2 + i for i in range(2 * nw)},
        compiler_params=pltpu.CompilerParams(has_side_effects=DATAFLOW),
    )(*[pltpu.with_memory_space_constraint(a, pltpu.HBM) for a in (*srcs, *lands)], *afters)
    return dict(sems=outs[:2], srcs=outs[2:2 + nw], lands=outs[2 + nw:2 + 2 * nw], token=outs[-1], scatter=scatter)


def _exchange_wait(handle, afters, name):
    srcs, lands, scatter = handle["srcs"], handle["lands"], handle["scatter"]
    nw = len(srcs)

    def body(*refs):
        s_refs, l_refs = refs[:nw], refs[nw:2 * nw]
        send_sems, recv_sems = refs[2 * nw], refs[2 * nw + 1]
        for w in range(nw):
            for k, peer, slot in _peers():
                cp = _split_copy(s_refs[w], l_refs[w], send_sems, recv_sems, w, k, peer, slot, scatter, False)
                cp.wait_send()
                cp.wait_recv()

    outs = pl.pallas_call(
        body, name=name,
        out_shape=tuple(pltpu.HBM(a.shape, a.dtype) for a in (*srcs, *lands)),
        in_specs=[HBM_SPEC] * (2 * nw) + [SEM_SPEC, SEM_SPEC] + [ANY_SPEC] * len(afters),
        out_specs=tuple([HBM_SPEC] * (2 * nw)),
        input_output_aliases={i: i for i in range(2 * nw)},
        compiler_params=pltpu.CompilerParams(has_side_effects=DATAFLOW),
    )(*srcs, *lands, *handle["sems"], *afters)
    return list(outs[nw:])


def _chips_and_sibling():
    x, y, c = lax.axis_index("x"), lax.axis_index("y"), lax.axis_index("c")
    return [(1 - x, y), (x, 1 - y), (1 - x, 1 - y)], (x, y, 1 - c), c


def _slot(px, py, pc):
    return 4 * px + 2 * py + pc


def _two_level_start(shards, name):
    nw = len(shards)
    lands = [_landing_zone(s, False) for s in shards]

    def body(*refs):
        s_refs, l_refs = refs[:nw], refs[nw:2 * nw]
        send_sems, recv_sems, token = refs[2 * nw], refs[2 * nw + 1], refs[-1]
        chips, sibling, c = _chips_and_sibling()
        for w in range(nw):
            for k, to in enumerate([sibling] + [(*chip, c) for chip in chips]):
                pltpu.make_async_remote_copy(
                    src_ref=s_refs[w], dst_ref=l_refs[w].at[_my_index()],
                    send_sem=send_sems.at[4 * w + k], recv_sem=recv_sems.at[4 * w + k],
                    device_id=to, device_id_type=MESH).start()
        token[...] = jnp.zeros_like(token)

    sems = pltpu.SemaphoreType.DMA((4 * nw,))
    outs = pl.pallas_call(
        body, name=name,
        out_shape=(sems, sems, *[pltpu.HBM(a.shape, a.dtype) for a in (*shards, *lands)],
                   jax.ShapeDtypeStruct((SUBLANE, LANE), F32)),
        in_specs=[HBM_SPEC] * (2 * nw),
        out_specs=(SEM_SPEC, SEM_SPEC, *[HBM_SPEC] * (2 * nw), pl.BlockSpec(memory_space=pltpu.VMEM)),
        input_output_aliases={i: 2 + i for i in range(2 * nw)},
        compiler_params=pltpu.CompilerParams(has_side_effects=DATAFLOW),
    )(*[pltpu.with_memory_space_constraint(a, pltpu.HBM) for a in (*shards, *lands)])
    return dict(sems=outs[:2], srcs=outs[2:2 + nw], lands=outs[2 + nw:2 + 2 * nw], token=outs[-1])


def _two_level_pass(handle, afters, name):
    srcs, lands = handle["srcs"], handle["lands"]
    nw = len(srcs)

    def body(*refs):
        s_refs, l_refs = refs[:nw], refs[nw:2 * nw]
        send_a, recv_a = refs[2 * nw], refs[2 * nw + 1]
        send_b, recv_b = refs[2 * nw + 2 + len(afters)], refs[2 * nw + 3 + len(afters)]
        chips, sibling, c = _chips_and_sibling()
        for w in range(nw):
            for j, chip in enumerate(chips):
                landed = l_refs[w].at[_slot(*chip, c)]
                pltpu.make_async_remote_copy(
                    src_ref=s_refs[w], dst_ref=landed, send_sem=send_a.at[4 * w + 1 + j], recv_sem=recv_a.at[4 * w + 1 + j],
                    device_id=(*chip, c), device_id_type=MESH).wait_recv()
                pltpu.make_async_remote_copy(
                    src_ref=landed, dst_ref=landed, send_sem=send_b.at[3 * w + j], recv_sem=recv_b.at[3 * w + j],
                    device_id=sibling, device_id_type=MESH).start()

    sems = pltpu.SemaphoreType.DMA((3 * nw,))
    outs = pl.pallas_call(
        body, name=name,
        out_shape=(sems, sems, *[pltpu.HBM(a.shape, a.dtype) for a in (*srcs, *lands)]),
        in_specs=[HBM_SPEC] * (2 * nw) + [SEM_SPEC, SEM_SPEC] + [ANY_SPEC] * len(afters),
        out_specs=(SEM_SPEC, SEM_SPEC, *[HBM_SPEC] * (2 * nw)),
        input_output_aliases={i: 2 + i for i in range(2 * nw)},
        compiler_params=pltpu.CompilerParams(has_side_effects=DATAFLOW),
    )(*srcs, *lands, *handle["sems"], *afters)
    return dict(sems=handle["sems"], sems_pass=outs[:2], srcs=outs[2:2 + nw], lands=outs[2 + nw:2 + 2 * nw])


def _two_level_wait(handle, name):
    srcs, lands = handle["srcs"], handle["lands"]
    nw = len(srcs)

    def body(*refs):
        s_refs, l_refs = refs[:nw], refs[nw:2 * nw]
        send_a, recv_a, send_b, recv_b = refs[2 * nw:2 * nw + 4]
        chips, sibling, c = _chips_and_sibling()
        x, y = sibling[0], sibling[1]
        for w in range(nw):
            first = pltpu.make_async_remote_copy(
                src_ref=s_refs[w], dst_ref=l_refs[w].at[_slot(x, y, 1 - c)], send_sem=send_a.at[4 * w],
                recv_sem=recv_a.at[4 * w], device_id=sibling, device_id_type=MESH)
            first.wait_send()
            first.wait_recv()
            for j, chip in enumerate(chips):
                pltpu.make_async_remote_copy(
                    src_ref=s_refs[w], dst_ref=l_refs[w].at[_slot(*chip, c)], send_sem=send_a.at[4 * w + 1 + j],
                    recv_sem=recv_a.at[4 * w + 1 + j], device_id=(*chip, c), device_id_type=MESH).wait_send()
                passed = pltpu.make_async_remote_copy(
                    src_ref=l_refs[w].at[_slot(*chip, c)], dst_ref=l_refs[w].at[_slot(*chip, 1 - c)],
                    send_sem=send_b.at[3 * w + j], recv_sem=recv_b.at[3 * w + j], device_id=sibling, device_id_type=MESH)
                passed.wait_send()
                passed.wait_recv()

    outs = pl.pallas_call(
        body, name=name,
        out_shape=tuple(pltpu.HBM(a.shape, a.dtype) for a in (*srcs, *lands)),
        in_specs=[HBM_SPEC] * (2 * nw) + [SEM_SPEC] * 4,
        out_specs=tuple([HBM_SPEC] * (2 * nw)),
        input_output_aliases={i: i for i in range(2 * nw)},
        compiler_params=pltpu.CompilerParams(has_side_effects=DATAFLOW),
    )(*srcs, *lands, *handle["sems"], *handle["sems_pass"])
    return list(outs[nw:])


def _adamw(w, g, m, v):
    m = ADAM_B1 * m + (1.0 - ADAM_B1) * g
    v = ADAM_B2 * v + (1.0 - ADAM_B2) * (g * g)
    m_hat = m / (1.0 - ADAM_B1 ** ADAM_STEP)
    v_hat = v / (1.0 - ADAM_B2 ** ADAM_STEP)
    delta = -ADAM_LR * (m_hat / (jnp.sqrt(v_hat) + ADAM_EPS) + ADAM_WD * w)
    return delta, m, v


def _sum_adam(parts, w, m, v, name):
    _, r, c = parts.shape
    tr = _pick(r, 128, 16)

    def body(p_ref, w_ref, m_ref, v_ref, g_ref, d_ref, mo_ref, vo_ref):
        g = p_ref[0].astype(F32)
        for s in range(1, N_DEV):
            g = g + p_ref[s].astype(F32)
        g_ref[...] = g
        d_ref[...], mo_ref[...], vo_ref[...] = _adamw(w_ref[...], g, m_ref[...], v_ref[...])

    row = pl.BlockSpec((tr, c), lambda i: (i, 0))
    sh = jax.ShapeDtypeStruct((r, c), F32)
    return pl.pallas_call(
        body, name=name, grid=(r // tr,),
        in_specs=[pl.BlockSpec((N_DEV, tr, c), lambda i: (0, i, 0)), row, row, row],
        out_specs=[row, row, row, row], out_shape=[sh, sh, sh, sh],
        compiler_params=_params(("parallel",)),
    )(parts, w, m, v)


def _sum_slots(parts, name):
    _, r, c = parts.shape
    tr = _pick(r, 512, SUBLANE)

    def body(p_ref, o_ref):
        g = p_ref[0]
        for s in range(1, N_DEV):
            g = g + p_ref[s]
        o_ref[...] = g

    return pl.pallas_call(
        body, name=name, grid=(r // tr,),
        in_specs=[pl.BlockSpec((N_DEV, tr, c), lambda i: (0, i, 0))],
        out_specs=pl.BlockSpec((tr, c), lambda i: (i, 0)), out_shape=jax.ShapeDtypeStruct((r, c), F32),
        compiler_params=_params(("parallel",)),
    )(parts)


def _adam_rows(g, w, m, v, name):
    r, c = g.shape
    tr = _pick(r, 512, SUBLANE)

    def body(g_ref, w_ref, m_ref, v_ref, d_ref, mo_ref, vo_ref):
        d_ref[...], mo_ref[...], vo_ref[...] = _adamw(w_ref[...], g_ref[...], m_ref[...], v_ref[...])

    row = pl.BlockSpec((tr, c), lambda i: (i, 0))
    sh = jax.ShapeDtypeStruct((r, c), F32)
    return pl.pallas_call(body, name=name, grid=(r // tr,), in_specs=[row] * 4, out_specs=[row] * 3,
                          out_shape=[sh, sh, sh], compiler_params=_params(("parallel",)))(g, w, m, v)


def _pack(arrays):
    flat = jnp.concatenate([a.reshape(-1).astype(F32) for a in arrays])
    pad = (-flat.shape[0]) % (SUBLANE * LANE)
    return jnp.pad(flat, (0, pad)).reshape(-1, LANE)


def _unpack(packed, shapes):
    flat = packed.reshape(-1)
    out, off = [], 0
    for s in shapes:
        n = math.prod(s)
        out.append(flat[off:off + n].reshape(s))
        off += n
    return out


def _block_diag(t):
    eye = jnp.eye(S5_SUPER, dtype=bool)
    bd = jnp.where(eye[None, :, None, :, None], t[:, :, :, None, :], 0.0)
    return bd.reshape(S5_SUPER, S5_SUPER * t.shape[2], S5_SUPER * t.shape[3])


def _diag_blocks(dense, a, b):
    x = dense.reshape(S5_SUPER, S5_SUPER, a, S5_SUPER, b)
    return jnp.moveaxis(jnp.diagonal(x, axis1=1, axis2=3), -1, 1)


def _s5_layouts(b_re, b_im, c_re, c_im, d):
    g2 = (S5_GROUPS // S5_SUPER, S5_SUPER)
    bt = lambda b: _block_diag(b.reshape(*g2, S5_STATE, S5_GROUP).transpose(0, 1, 3, 2))
    ct = lambda c: _block_diag(c.reshape(*g2, S5_GROUP, S5_STATE).transpose(0, 1, 3, 2))
    bsg = jnp.concatenate([bt(b_re), bt(b_im)], axis=2).astype(BF16)
    ccat = jnp.concatenate([ct(c_re), -ct(c_im)], axis=1).astype(BF16)
    return bsg, ccat, d.reshape(1, S5_WIDTH)


def _s5_param_grads(gb, gc):
    n = S5_LANES
    gb_re = _diag_blocks(gb[:, :, 0:n], S5_GROUP, S5_STATE).transpose(0, 1, 3, 2).reshape(S5_GROUPS, S5_STATE, S5_GROUP)
    gb_im = _diag_blocks(gb[:, :, n:2 * n], S5_GROUP, S5_STATE).transpose(0, 1, 3, 2).reshape(S5_GROUPS, S5_STATE, S5_GROUP)
    gc_re = _diag_blocks(gc[:, 0:n, :], S5_STATE, S5_GROUP).transpose(0, 1, 3, 2).reshape(S5_GROUPS, S5_GROUP, S5_STATE)
    gc_im = -_diag_blocks(gc[:, n:2 * n, :], S5_STATE, S5_GROUP).transpose(0, 1, 3, 2).reshape(S5_GROUPS, S5_GROUP, S5_STATE)
    return gb_re, gb_im, gc_re, gc_im


def _local_step(x, target, weight, emit, small, after=None):
    sp = small
    a_re, a_im = sp["s5_a_re"], sp["s5_a_im"]
    ldt = sp["s5_log_dt"].reshape(S5_GROUPS, 1)

    h1 = _rms_fwd(x, sp["ln_mix_g"], "rms_mix", after=after)
    w_in = weight("w_in", h1)
    proj = _mm_nn(h1, w_in, "mm_in", after=weight("after_w_in", None))
    conv_w = weight("conv_w", None)
    disc = _s5_param_fwd(a_re, a_im, ldt)
    bsg, ccat, d_row = sp["s5_layouts"]
    abar_t, coef_t = _s5_to_tile(disc[0], disc[1]), _s5_to_tile(disc[2], disc[3])
    y, sb = _s5_fwd(proj, bsg, ccat, d_row, abar_t, coef_t)
    z16 = _gelu_fwd(y)
    w_glu = weight("s5_w_glu", z16)
    gl = _mm_nn(z16, w_glu, "mm_glu")
    z2 = _glu_fwd(y, gl, sp["s5_b_glu"])
    w_ps = weight("w_proj_s5", z2)
    ys = _mm_nn(z2, w_ps, "mm_proj_s5")
    o_raw, oh, s0s = _hgrn_fwd(proj, sp["hgrn_lb_logits"], sp["hgrn_norm_g"])
    w_ph = weight("w_proj_hgrn", oh)
    yh = _mm_nn(oh, w_ph, "mm_proj_hgrn")
    merged = _merge_fwd(proj, ys, yh)
    w_out = weight("w_out", merged)
    x1 = _mm_nn(merged, w_out, "mm_out", res=x)
    h2 = _rms_fwd(x1, sp["ln_ffn_g"], "rms_ffn")
    w_up = weight("w_up", h2)
    up = _mm_nn(h2, w_up, "mm_up")
    act = _ffn_act_fwd(up, conv_w, sp["conv_b"])
    w_down = weight("w_down", act)
    x2 = _mm_nn(act, w_down, "mm_down", res=x1)
    dx2, dx2_16, g_ln_final, loss = _loss_head(x2, sp["ln_final_g"], target)

    dact = _mm_nt(dx2_16, w_down, "mm_down_dx")
    tok = emit("w_down", _mm_tn(act, dx2_16, 1, "mm_down_dw"))
    dup_g, dup_v, dcw_g, dcw_v, dcb_g, dcb_v = _ffn_act_bwd(up, dact, conv_w, sp["conv_b"], after=tok)
    dup = jnp.concatenate([dup_g, dup_v], axis=1)
    g_conv_w = jnp.concatenate([dcw_g, dcw_v], axis=1)
    g_conv_b = jnp.concatenate([dcb_g, dcb_v], axis=1)
    dh2 = _mm_nt(dup, w_up, "mm_up_dx")
    tok = emit("w_up", _mm_tn(h2, dup, N_DEV, "mm_up_dw"))
    dx1, dx1_16, g_ln_ffn = _rms_bwd(x1, sp["ln_ffn_g"], dh2, dx2, "rms_ffn_bwd", True, after=tok)

    dmerged = _mm_nt(dx1_16, w_out, "mm_out_dx")
    tok = emit("w_out", _mm_tn(merged, dx1_16, 1, "mm_out_dw"))
    dys, dyh, dgs, dgh = _merge_bwd(proj, ys, yh, dmerged, after=tok)
    doh = _mm_nt(dyh, w_ph, "mm_proj_hgrn_dx")
    tok = emit("w_proj_hgrn", _mm_tn(oh, dyh, N_DEV, "mm_proj_hgrn_dw"))
    dz2 = _mm_nt(dys, w_ps, "mm_proj_s5_dx", after=tok)
    tok = emit("w_proj_s5", _mm_tn(z2, dys, N_DEV, "mm_proj_s5_dw"))
    dgl, dza, g_b_glu = _glu_bwd(y, gl, sp["s5_b_glu"], dz2, after=tok)
    dzb = _mm_nt(dgl, w_glu, "mm_glu_dx")
    tok = emit("s5_w_glu", _mm_tn(z16, dgl, 1, "mm_glu_dw"))
    dy = _gelu_bwd(y, dza, dzb, after=tok)
    du, gb, gc, gd, g_abar_t, g_coef_t = _s5_bwd(proj, dy, sb, bsg, ccat, d_row, abar_t, coef_t)
    g_a_re, g_a_im, g_ldt = _s5_param_bwd(a_re, a_im, ldt, [*_s5_from_tile(g_abar_t), *_s5_from_tile(g_coef_t)])
    g_b_re, g_b_im, g_c_re, g_c_im = _s5_param_grads(gb, gc)
    dq, dz, dv, dg, g_norm, dlb = _hgrn_bwd(proj, o_raw, s0s, doh, sp["hgrn_lb_logits"], sp["hgrn_norm_g"])
    g_logits = _lb_bwd(sp["hgrn_lb_logits"], dlb)

    small_g = dict(s5_a_re=g_a_re, s5_a_im=g_a_im, s5_log_dt=g_ldt.reshape(1, S5_GROUPS),
                   s5_b_re=g_b_re, s5_b_im=g_b_im, s5_c_re=g_c_re, s5_c_im=g_c_im,
                   s5_d=gd.reshape(S5_GROUPS, S5_GROUP), s5_b_glu=g_b_glu, hgrn_lb_logits=g_logits,
                   hgrn_norm_g=g_norm, ln_ffn_g=g_ln_ffn, conv_w=g_conv_w, conv_b=g_conv_b, ln_final_g=g_ln_final,
                   loss=loss[0, 0:1])
    tok_small = emit("small", small_g)

    dproj = jnp.concatenate([du, dq, dz, dv, dg, dgs, dgh], axis=1)
    tok = emit("w_in", _mm_tn(h1, dproj, N_DEV, "mm_in_dw", after=tok_small))
    dh1 = _mm_nt(dproj, w_in, "mm_in_dx")
    grad_x, g_ln_mix = _rms_bwd(x, sp["ln_mix_g"], dh1, dx1, "rms_mix_bwd", False, after=tok)
    return grad_x, g_ln_mix


BIG = ("w_in", "s5_w_glu", "w_proj_s5", "w_proj_hgrn", "w_out", "w_up", "w_down")
COL_SHARDED = ("w_in", "w_proj_s5", "w_proj_hgrn", "w_up")
SMALL = ("ln_mix_g", "s5_a_re", "s5_a_im", "s5_log_dt", "s5_b_re", "s5_b_im", "s5_c_re", "s5_c_im", "s5_d",
         "s5_b_glu", "hgrn_lb_logits", "hgrn_norm_g", "ln_ffn_g", "conv_b", "ln_final_g")
WEIGHTS = ("ln_mix_g", "w_in", "s5_a_re", "s5_a_im", "s5_log_dt", "s5_b_re", "s5_b_im", "s5_c_re", "s5_c_im", "s5_d",
           "s5_w_glu", "s5_b_glu", "w_proj_s5", "hgrn_lb_logits", "hgrn_norm_g", "w_proj_hgrn", "w_out", "ln_ffn_g",
           "w_up", "conv_w", "conv_b", "w_down", "ln_final_g")


def kernel(x, ln_mix_g, w_in, s5_a_re, s5_a_im, s5_log_dt, s5_b_re, s5_b_im, s5_c_re, s5_c_im, s5_d, s5_w_glu, s5_b_glu, w_proj_s5, hgrn_lb_logits, hgrn_norm_g, w_proj_hgrn, w_out, ln_ffn_g, w_up, conv_w, conv_b, w_down, ln_final_g, loss_target, m_ln_mix_g, m_w_in, m_s5_a_re, m_s5_a_im, m_s5_log_dt, m_s5_b_re, m_s5_b_im, m_s5_c_re, m_s5_c_im, m_s5_d, m_s5_w_glu, m_s5_b_glu, m_w_proj_s5, m_hgrn_lb_logits, m_hgrn_norm_g, m_w_proj_hgrn, m_w_out, m_ln_ffn_g, m_w_up, m_conv_w, m_conv_b, m_w_down, m_ln_final_g, v_ln_mix_g, v_w_in, v_s5_a_re, v_s5_a_im, v_s5_log_dt, v_s5_b_re, v_s5_b_im, v_s5_c_re, v_s5_c_im, v_s5_d, v_s5_w_glu, v_s5_b_glu, v_w_proj_s5, v_hgrn_lb_logits, v_hgrn_norm_g, v_w_proj_hgrn, v_w_out, v_ln_ffn_g, v_w_up, v_conv_w, v_conv_b, v_w_down, v_ln_final_g):
    given = dict(locals())
    w = {n: given[n] for n in WEIGHTS}
    mom = {n: given["m_" + n] for n in WEIGHTS}
    var = {n: given["v_" + n] for n in WEIGHTS}

    first = _two_level_start([w_in[0].astype(BF16), conv_w[0]], "gather_first_start")
    zero = first["token"][0, 0]
    packed_small = SMALL[1:]
    pw, pm, pv = (_pack([d[n] for n in packed_small]) + zero for d in (w, mom, var))
    layouts = _s5_layouts(s5_b_re[0] + zero, s5_b_im[0], s5_c_re[0] + zero, s5_c_im[0], s5_d[0])
    gather_groups = (("s5_w_glu", "w_proj_s5", "w_proj_hgrn", "w_out"), ("w_up",), ("w_down",))
    shard16 = {n: w[n][0].astype(BF16) + zero.astype(BF16) for g in gather_groups for n in g}
    zones = {n: _landing_zone(s, False) for n, s in shard16.items()}
    pending, ready = {}, {}

    def weight(name, after):
        if "w_in" not in ready:
            local_work = [after, pw, pm, pv, layouts[0], layouts[1], *zones.values()]
            passed = _two_level_pass(first, local_work, "gather_first_pass")
            ready["w_in"], conv_w_all = _two_level_wait(passed, "gather_first_wait")
            ready["conv_w"] = conv_w_all.transpose(1, 0, 2).reshape(3, 2 * D_FF)
            token = ready["w_in"]
            for i, group in enumerate(gather_groups):
                handle = _exchange_start([shard16[n] for n in group], False, token, f"gather_start_{i}",
                                         lands=[zones[n] for n in group])
                token = handle["token"]
                for n in group:
                    pending[n] = (group, handle, f"gather_wait_{i}")
            ready["after_w_in"] = token
        if name not in ready:
            group, handle, wait_name = pending[name]
            for n, g in zip(group, _exchange_wait(handle, [after], wait_name)):
                ready[n] = g
        g = ready[name]
        return g if name not in BIG or name in COL_SHARDED else g.reshape(1, N_DEV * g.shape[1], g.shape[2])

    scatter_groups = (("w_down",), ("w_up",), ("w_out", "w_proj_hgrn", "w_proj_s5", "s5_w_glu"), ("w_in",))
    emitted, scatters = {}, []
    packed_names = SMALL[1:] + ("conv_w", "loss")

    def emit(name, grad):
        if name == "small":
            emitted[name] = ([grad[n].shape for n in packed_names],
                             _exchange_start([_pack([grad[n] for n in packed_names])], False, None, "small_start"))
            return emitted[name][1]["token"]
        emitted[name] = grad if name in COL_SHARDED else grad.reshape(N_DEV, -1, grad.shape[2])
        group = scatter_groups[len(scatters)]
        if not all(n in emitted for n in group):
            return None
        handle = _exchange_start([emitted[n] for n in group], True, None, f"scatter_start_{len(scatters)}")
        scatters.append((group, handle))
        return handle["token"]

    small = dict(ln_mix_g=ln_mix_g, s5_a_re=s5_a_re[0], s5_a_im=s5_a_im[0], s5_log_dt=s5_log_dt, s5_layouts=layouts,
                 s5_b_glu=s5_b_glu, hgrn_lb_logits=hgrn_lb_logits, hgrn_norm_g=hgrn_norm_g, ln_ffn_g=ln_ffn_g,
                 conv_b=conv_b, ln_final_g=ln_final_g.reshape(1, D_MODEL))
    grad_x, g_ln_mix = _local_step(x[0], loss_target[0], weight, emit, small, after=first["token"])

    shapes, handle = emitted["small"]
    total = _sum_slots(_exchange_wait(handle, [grad_x], "small_wait")[0], "sum_small")
    summed = dict(zip(packed_names, _unpack(total, shapes)))
    mix_all = _all_gather([g_ln_mix.reshape(-1, LANE)], "gather_ln_mix")[0]
    summed["ln_mix_g"] = _sum_slots(mix_all, "sum_ln_mix").reshape(1, D_MODEL)

    grads, delta, new_m, new_v = {}, {}, {}, {}
    afters = [grad_x, total]
    for i, (group, handle) in enumerate(scatters):
        for n, r in zip(group, _exchange_wait(handle, afters, f"scatter_wait_{i}")):
            g, d, m2, v2 = _sum_adam(r, w[n][0], mom[n][0], var[n][0], "adam_" + n)
            grads[n], delta[n], new_m[n], new_v[n] = g[None], d[None], m2[None], v2[None]
        if i == len(scatters) - 2:
            afters = [delta[n] for g2, _ in scatters[:-1] for n in g2]

    d_s, m_s, v_s = _adam_rows(_pack([summed[n] for n in packed_small]), pw, pm, pv, "adam_small")
    wshapes = [w[n].shape for n in packed_small]
    for n, d, m2, v2 in zip(packed_small, _unpack(d_s, wshapes), _unpack(m_s, wshapes), _unpack(v_s, wshapes)):
        grads[n], delta[n], new_m[n], new_v[n] = summed[n].reshape(w[n].shape), d, m2, v2
    grads["ln_mix_g"] = summed["ln_mix_g"]
    delta["ln_mix_g"], new_m["ln_mix_g"], new_v["ln_mix_g"] = _adam_rows(summed["ln_mix_g"], ln_mix_g, m_ln_mix_g,
                                                                         v_ln_mix_g, "adam_ln_mix")
    me = 4 * lax.axis_index("x") + 2 * lax.axis_index("y") + lax.axis_index("c")
    ncol = conv_w.shape[2]
    g_cw = lax.dynamic_slice_in_dim(summed["conv_w"], me * ncol, ncol, axis=1)
    d_cw, m_cw, v_cw = _adam_rows(g_cw, conv_w[0], m_conv_w[0], v_conv_w[0], "adam_conv_w")
    grads["conv_w"], delta["conv_w"], new_m["conv_w"], new_v["conv_w"] = g_cw[None], d_cw[None], m_cw[None], v_cw[None]

    return (summed["loss"].reshape(()), grad_x[None], *[grads[n] for n in WEIGHTS], *[delta[n] for n in WEIGHTS],
            *[new_m[n] for n in WEIGHTS], *[new_v[n] for n in WEIGHTS])
```

```python
import math

import jax
import jax.numpy as jnp
from jax import lax
from jax.experimental import pallas as pl
from jax.experimental.pallas import tpu as pltpu

F32 = jnp.float32
BF16 = jnp.bfloat16

N_DEV = 8
D_MODEL = 2048
S5_WIDTH = 1024
S5_GROUP = 16
S5_GROUPS = 64
S5_STATE = 64
S5_MAX_RE = -1e-4
S5_SUPER = 8
S5_LANES = S5_SUPER * S5_STATE
HGRN_WIDTH = 1024
HGRN_HEADS = 8
HGRN_DH = 128
HGRN_CHUNK = 64
HGRN_SUBS = 4
D_FF = 5632
RMS_EPS = 1e-6
ADAM_LR = 0.001
ADAM_B1 = 0.9
ADAM_B2 = 0.999
ADAM_EPS = 1e-08
ADAM_WD = 0.01
ADAM_STEP = 10

LANE = 128
SUBLANE = 8
VMEM_LIMIT = 48 * 1024 * 1024
MESH = pl.DeviceIdType.MESH
GELU_C = math.sqrt(2.0 / math.pi)
GELU_A = 0.044715


def _params(sem=None):
    return pltpu.CompilerParams(dimension_semantics=sem, vmem_limit_bytes=VMEM_LIMIT)


def _pick(n, cap, unit=LANE):
    best = None
    for t in range(unit, min(n, cap) + 1, unit):
        if n % t == 0:
            best = t
    return best if best is not None else n


def _ordered(body, in_specs, args, after):
    if after is None:
        return body, list(in_specs), list(args)
    n_in = len(args)

    def ordered_body(*refs):
        return body(*refs[:n_in], *refs[n_in + 1:])

    return ordered_body, [*in_specs, pl.BlockSpec(memory_space=pl.ANY)], [*args, after]


def _sigmoid(x):
    return 0.5 * jnp.tanh(0.5 * x) + 0.5


def _silu_and_grad(x):
    s = _sigmoid(x)
    return x * s, s * (1.0 + x * (1.0 - s))


def _gelu_and_grad(y):
    inner = GELU_C * (y + GELU_A * y * y * y)
    th = jnp.tanh(inner)
    val = 0.5 * y * (1.0 + th)
    grad = 0.5 * (1.0 + th) + 0.5 * y * (1.0 - th * th) * GELU_C * (1.0 + 3.0 * GELU_A * y * y)
    return val, grad


def _dot(a, b):
    return jnp.dot(a, b, preferred_element_type=F32)


def _dot_nt(a, b):
    return lax.dot_general(a, b, (((1,), (1,)), ((), ())), preferred_element_type=F32)


def _dot_tn(a, b):
    return lax.dot_general(a, b, (((0,), (0,)), ((), ())), preferred_element_type=F32)


def _blocks_per_step(nb, ns, tn, cap=2048):
    if tn != ns:
        return 1
    best = 1
    for b in range(1, nb + 1):
        if nb % b == 0 and b * ns <= cap:
            best = b
    return best


NN_TILE_BYTES = 42 * 1024 * 1024


def _mm_nn(a, w, name, res=None, out_dtype=F32, after=None):
    m, kdim = a.shape
    nb, _, ns = w.shape
    tk, tn = _pick(kdim, D_FF), _pick(ns, 1536)
    npb, nk = ns // tn, kdim // tk
    bps = _blocks_per_step(nb, ns, tn)
    assert bps == 1 or nk == 1

    def buffers(rows):
        return 2 * (rows * tk * 2 + bps * tk * tn * 2 + rows * bps * tn * 4 * (2 if res is not None else 1))

    tm = next((r for r in (_pick(m, 1024), _pick(m, 512)) if buffers(r) <= NN_TILE_BYTES), _pick(m, 256))

    def body(*refs):
        a_ref, w_ref = refs[0], refs[1]
        r_ref = refs[2] if res is not None else None
        o_ref = refs[3] if res is not None else refs[2]

        def finish(r, cols):
            if res is not None:
                r = r + r_ref[:, cols]
            o_ref[:, cols] = r.astype(out_dtype)

        if nk == 1:
            for b in range(bps):
                finish(_dot(a_ref[...], w_ref[b]), slice(b * tn, (b + 1) * tn))
            return
        acc = refs[-1]
        k = pl.program_id(2)

        @pl.when(k == 0)
        def _():
            acc[...] = jnp.zeros_like(acc)

        acc[...] += _dot(a_ref[...], w_ref[0])

        @pl.when(k == nk - 1)
        def _():
            finish(acc[...], slice(0, tn))

    in_specs = [pl.BlockSpec((tm, tk), lambda j, i, k: (i, k)),
                pl.BlockSpec((bps, tk, tn), lambda j, i, k: (j // npb, k, j % npb))]
    args = [a, w]
    if res is not None:
        in_specs.append(pl.BlockSpec((tm, bps * tn), lambda j, i, k: (i, j)))
        args.append(res)
    body, in_specs, args = _ordered(body, in_specs, args, after)
    return pl.pallas_call(
        body, name=name, grid=(nb * npb // bps, m // tm, nk),
        in_specs=in_specs, out_specs=pl.BlockSpec((tm, bps * tn), lambda j, i, k: (i, j)),
        out_shape=jax.ShapeDtypeStruct((m, nb * ns), out_dtype),
        scratch_shapes=[pltpu.VMEM((tm, tn), F32)] if nk > 1 else [],
        compiler_params=_params(("parallel", "parallel", "arbitrary")),
    )(*args)


NT_STEP_COLS = 2816


def _mm_nt(a, w, name, out_dtype=F32, after=None):
    m, _ = a.shape
    nb, kdim, ns = w.shape
    tm, tko, tn = _pick(m, 1024), _pick(kdim, 1024), _pick(ns, 2048)
    npb = ns // tn
    bps = _blocks_per_step(nb, ns, tn, cap=NT_STEP_COLS)
    nred = nb * npb // bps

    def body(a_ref, w_ref, o_ref, *scratch):
        total = _dot_nt(a_ref[:, 0:tn], w_ref[0])
        for b in range(1, bps):
            total = total + _dot_nt(a_ref[:, b * tn:(b + 1) * tn], w_ref[b])
        if nred == 1:
            o_ref[...] = total.astype(out_dtype)
            return
        acc = scratch[0]
        n = pl.program_id(2)

        @pl.when(n == 0)
        def _():
            acc[...] = jnp.zeros_like(acc)

        acc[...] += total

        @pl.when(n == nred - 1)
        def _():
            o_ref[...] = acc[...].astype(out_dtype)

    in_specs = [pl.BlockSpec((tm, bps * tn), lambda i, j, n: (i, n)),
                pl.BlockSpec((bps, tko, tn), lambda i, j, n: (n // npb, j, n % npb))]
    body, in_specs, args = _ordered(body, in_specs, [a, w], after)
    return pl.pallas_call(
        body, name=name, grid=(m // tm, kdim // tko, nred),
        in_specs=in_specs,
        out_specs=pl.BlockSpec((tm, tko), lambda i, j, n: (i, j)),
        out_shape=jax.ShapeDtypeStruct((m, kdim), out_dtype),
        scratch_shapes=[pltpu.VMEM((tm, tko), F32)] if nred > 1 else [],
        compiler_params=_params(("parallel", "parallel", "arbitrary")),
    )(*args)


def _mm_tn(a, d, nb, name, out_dtype=BF16, after=None):
    m, kdim = a.shape
    ns = d.shape[1] // nb
    tm, tko, tn = _pick(m, 4096), _pick(kdim, 512), _pick(ns, 1536)
    npb, nm = ns // tn, m // tm
    bps = _blocks_per_step(nb, ns, tn, cap=1536)
    assert bps == 1 or nm == 1

    def body(a_ref, d_ref, o_ref, *scratch):
        if nm == 1:
            for b in range(bps):
                o_ref[b] = _dot_tn(a_ref[...], d_ref[:, b * tn:(b + 1) * tn]).astype(out_dtype)
            return
        acc = scratch[0]
        r = pl.program_id(2)

        @pl.when(r == 0)
        def _():
            acc[...] = jnp.zeros_like(acc)

        acc[...] += _dot_tn(a_ref[...], d_ref[...])

        @pl.when(r == nm - 1)
        def _():
            o_ref[0] = acc[...].astype(out_dtype)

    in_specs = [pl.BlockSpec((tm, tko), lambda j, i, r: (r, i)), pl.BlockSpec((tm, bps * tn), lambda j, i, r: (r, j))]
    body, in_specs, args = _ordered(body, in_specs, [a, d], after)
    return pl.pallas_call(
        body, name=name, grid=(nb * npb // bps, kdim // tko, nm),
        in_specs=in_specs,
        out_specs=pl.BlockSpec((bps, tko, tn), lambda j, i, r: (j // npb, i, j % npb)),
        out_shape=jax.ShapeDtypeStruct((nb, kdim, ns), out_dtype),
        scratch_shapes=[pltpu.VMEM((tko, tn), F32)] if nm > 1 else [],
        compiler_params=_params(("parallel", "parallel", "arbitrary")),
    )(*args)


def _rms_fwd(x, g, name, after=None):
    t, d = x.shape
    tr = _pick(t, 256, SUBLANE)

    def body(x_ref, g_ref, h_ref):
        xv = x_ref[...]
        r = lax.rsqrt(jnp.mean(xv * xv, axis=-1, keepdims=True) + RMS_EPS)
        h_ref[...] = (xv * r * g_ref[...]).astype(BF16)

    in_specs = [pl.BlockSpec((tr, d), lambda i: (i, 0)), pl.BlockSpec((1, d), lambda i: (0, 0))]
    body, in_specs, args = _ordered(body, in_specs, [x, g], after)
    return pl.pallas_call(
        body, name=name, grid=(t // tr,),
        in_specs=in_specs,
        out_specs=pl.BlockSpec((tr, d), lambda i: (i, 0)),
        out_shape=jax.ShapeDtypeStruct((t, d), BF16),
        compiler_params=_params(("parallel",)),
    )(*args)


def _rms_bwd(x, g, dh, add, name, want_bf16, after=None):
    t, d = x.shape
    tr = _pick(t, 256, SUBLANE)

    def body(x_ref, g_ref, dh_ref, add_ref, *outs):
        if want_bf16:
            dx_ref, dxb_ref, dg_ref = outs
        else:
            dx_ref, dg_ref = outs
        i = pl.program_id(0)

        @pl.when(i == 0)
        def _():
            dg_ref[...] = jnp.zeros_like(dg_ref)

        xv, dhv = x_ref[...], dh_ref[...]
        r = lax.rsqrt(jnp.mean(xv * xv, axis=-1, keepdims=True) + RMS_EPS)
        xh = xv * r
        dg_ref[...] += jnp.sum(dhv * xh, axis=0, keepdims=True)
        dxh = dhv * g_ref[...]
        dx = add_ref[...] + r * (dxh - xh * jnp.mean(dxh * xh, axis=-1, keepdims=True))
        dx_ref[...] = dx
        if want_bf16:
            dxb_ref[...] = dx.astype(BF16)

    row = pl.BlockSpec((tr, d), lambda i: (i, 0))
    vec = pl.BlockSpec((1, d), lambda i: (0, 0))
    out_specs = [row] + ([row] if want_bf16 else []) + [vec]
    out_shape = ([jax.ShapeDtypeStruct((t, d), F32)] + ([jax.ShapeDtypeStruct((t, d), BF16)] if want_bf16 else [])
                 + [jax.ShapeDtypeStruct((1, d), F32)])
    body, in_specs, args = _ordered(body, [row, vec, row, row], [x, g, dh, add], after)
    return pl.pallas_call(
        body, name=name, grid=(t // tr,),
        in_specs=in_specs, out_specs=out_specs, out_shape=out_shape,
        compiler_params=_params(("arbitrary",)),
    )(*args)


def _loss_head(x2, g, target, name="loss_head"):
    t, d = x2.shape
    tr = _pick(t, 256, SUBLANE)

    def body(x_ref, g_ref, t_ref, dx_ref, dxb_ref, dg_ref, loss_ref):
        i = pl.program_id(0)

        @pl.when(i == 0)
        def _():
            dg_ref[...] = jnp.zeros_like(dg_ref)
            loss_ref[...] = jnp.zeros_like(loss_ref)

        xv = x_ref[...]
        gv = g_ref[...]
        r = lax.rsqrt(jnp.mean(xv * xv, axis=-1, keepdims=True) + RMS_EPS)
        xh = xv * r
        err = xh * gv - t_ref[...]
        part = 0.5 * jnp.sum(jnp.mean(err * err, axis=-1, keepdims=True), axis=0, keepdims=True)
        loss_ref[...] += jnp.broadcast_to(part, loss_ref.shape)
        dy = err * (1.0 / d)
        dg_ref[...] += jnp.sum(dy * xh, axis=0, keepdims=True)
        dxh = dy * gv
        dx = r * (dxh - xh * jnp.mean(dxh * xh, axis=-1, keepdims=True))
        dx_ref[...] = dx
        dxb_ref[...] = dx.astype(BF16)

    row = pl.BlockSpec((tr, d), lambda i: (i, 0))
    vec = pl.BlockSpec((1, d), lambda i: (0, 0))
    return pl.pallas_call(
        body, name=name, grid=(t // tr,),
        in_specs=[row, vec, row],
        out_specs=[row, row, vec, pl.BlockSpec((1, LANE), lambda i: (0, 0))],
        out_shape=[jax.ShapeDtypeStruct((t, d), F32), jax.ShapeDtypeStruct((t, d), BF16),
                   jax.ShapeDtypeStruct((1, d), F32), jax.ShapeDtypeStruct((1, LANE), F32)],
        compiler_params=_params(("arbitrary",)),
    )(x2, g, target)


def _s5_discretize(a_re, a_im, ldt):
    lam_re = jnp.minimum(a_re, S5_MAX_RE)
    lam_im = a_im
    dt = jnp.exp(ldt)
    mag = jnp.exp(lam_re * dt)
    abar_re = mag * jnp.cos(lam_im * dt)
    abar_im = mag * jnp.sin(lam_im * dt)
    den = lam_re * lam_re + lam_im * lam_im
    nr = abar_re - 1.0
    ni = abar_im
    coef_re = (nr * lam_re + ni * lam_im) / den
    coef_im = (ni * lam_re - nr * lam_im) / den
    return abar_re, abar_im, coef_re, coef_im


def _s5_param_fwd(a_re, a_im, ldt):
    def body(ar_ref, ai_ref, l_ref, o0, o1, o2, o3):
        outs = _s5_discretize(ar_ref[...], ai_ref[...], l_ref[...])
        for o, v in zip((o0, o1, o2, o3), outs):
            o[...] = v

    sh = jax.ShapeDtypeStruct(a_re.shape, F32)
    return pl.pallas_call(body, name="s5_param_fwd", out_shape=[sh, sh, sh, sh], compiler_params=_params())(a_re, a_im, ldt)


def _s5_param_bwd(a_re, a_im, ldt, cts):
    def body(ar_ref, ai_ref, l_ref, c0, c1, c2, c3, g0, g1, g2):
        _, vjp = jax.vjp(_s5_discretize, ar_ref[...], ai_ref[...], l_ref[...])
        ga, gb, gl = vjp((c0[...], c1[...], c2[...], c3[...]))
        g0[...] = ga
        g1[...] = gb
        g2[...] = gl

    sh = jax.ShapeDtypeStruct(a_re.shape, F32)
    return pl.pallas_call(body, name="s5_param_bwd", out_shape=[sh, sh, jax.ShapeDtypeStruct(ldt.shape, F32)],
                          compiler_params=_params())(a_re, a_im, ldt, *cts)


def _cmul(ar, ai, br, bi):
    return ar * br - ai * bi, ar * bi + ai * br


S5_TC = 128
S5_TILE = S5_SUPER * SUBLANE
S5_HALF = S5_TILE // 2


def _s5_to_tile(re, im):
    f = lambda a: a.reshape(S5_SUPER, S5_LANES // LANE, LANE).transpose(1, 0, 2).reshape(S5_HALF, LANE)
    return jnp.concatenate([f(re), f(im)], axis=0)


def _s5_from_tile(tile):
    f = lambda a: a.reshape(S5_LANES // LANE, S5_SUPER, LANE).transpose(1, 0, 2).reshape(S5_GROUPS, S5_STATE)
    return f(tile[0:S5_HALF]), f(tile[S5_HALF:])


RE = slice(0, S5_HALF)
IM = slice(S5_HALF, S5_TILE)


def _s5_scatter_rows(buf, rows, first_tile=0):
    tc = rows[0].shape[0]
    for j in range(SUBLANE):
        stacked = jnp.stack([r[:, j * LANE:(j + 1) * LANE] for r in rows], axis=0)
        buf[first_tile:first_tile + tc, j * SUBLANE:(j + 1) * SUBLANE, :] = jnp.swapaxes(stacked, 0, 1)


def _s5_gather_rows(buf, tc, first_tile=0):
    per_j = [jnp.swapaxes(buf[first_tile:first_tile + tc, j * SUBLANE:(j + 1) * SUBLANE, :], 0, 1)
             for j in range(SUBLANE)]
    return [jnp.concatenate([per_j[j][k] for j in range(SUBLANE)], axis=1) for k in range(S5_SUPER)]


def _s5_fwd(proj, bsg, ccat, d_row, abar_t, coef_t):
    t = proj.shape[0]
    tc = min(t, S5_TC)
    n_chunks = t // tc

    def body(u_ref, b_ref, c_ref, d_ref, a_ref, cf_ref, y_ref, sb_ref, x, car):
        @pl.when(pl.program_id(0) == 0)
        def _():
            car[...] = jnp.zeros_like(car)

        sb_ref[...] = car[...]
        u = u_ref[...]
        _s5_scatter_rows(x, [_dot(u[:, k * LANE:(k + 1) * LANE].astype(BF16), b_ref[k]) for k in range(S5_SUPER)])
        ar, ai = a_ref[RE, :], a_ref[IM, :]
        cr, ci = cf_ref[RE, :], cf_ref[IM, :]

        def step(i, carry):
            sr, si = carry
            xr, xi = _cmul(cr, ci, x[i, RE, :], x[i, IM, :])
            sr, si = ar * sr - ai * si + xr, ar * si + ai * sr + xi
            x[i, RE, :] = sr
            x[i, IM, :] = si
            return sr, si

        sr, si = lax.fori_loop(0, tc, step, (car[RE, :], car[IM, :]), unroll=4)
        car[RE, :] = sr
        car[IM, :] = si
        for k, s_k in enumerate(_s5_gather_rows(x, tc)):
            cols = slice(k * LANE, (k + 1) * LANE)
            y_ref[:, cols] = _dot(s_k.astype(BF16), c_ref[k]) + d_ref[:, cols] * u[:, cols]

    full = lambda shape: pl.BlockSpec(shape, lambda c: (0,) * len(shape))
    return pl.pallas_call(
        body, name="s5_fwd", grid=(n_chunks,),
        in_specs=[pl.BlockSpec((tc, S5_WIDTH), lambda c: (c, 0)), full(bsg.shape), full(ccat.shape), full(d_row.shape),
                  full(abar_t.shape), full(coef_t.shape)],
        out_specs=[pl.BlockSpec((tc, S5_WIDTH), lambda c: (c, 0)), pl.BlockSpec((None, S5_TILE, LANE), lambda c: (c, 0, 0))],
        out_shape=[jax.ShapeDtypeStruct((t, S5_WIDTH), F32), jax.ShapeDtypeStruct((n_chunks, S5_TILE, LANE), F32)],
        scratch_shapes=[pltpu.VMEM((tc, S5_TILE, LANE), F32), pltpu.VMEM((S5_TILE, LANE), F32)],
        compiler_params=_params(("arbitrary",)),
    )(proj, bsg, ccat, d_row, abar_t, coef_t)


def _s5_bwd(proj, dy, sb, bsg, ccat, d_row, abar_t, coef_t):
    t = proj.shape[0]
    tc = min(t, S5_TC)
    n_chunks = t // tc
    last = n_chunks - 1

    def body(u_ref, dy_ref, sb_ref, b_ref, c_ref, d_ref, a_ref, cf_ref,
             du_ref, gb_ref, gc_ref, gd_ref, ga_ref, gcf_ref, xb, xs, xg, gcar, acc):
        @pl.when(pl.program_id(0) == 0)
        def _():
            gcar[...] = jnp.zeros_like(gcar)
            acc[...] = jnp.zeros_like(acc)
            gb_ref[...] = jnp.zeros_like(gb_ref)
            gc_ref[...] = jnp.zeros_like(gc_ref)
            gd_ref[...] = jnp.zeros_like(gd_ref)

        u = u_ref[...]
        dyv = dy_ref[...]
        u16, dy16 = u.astype(BF16), dyv.astype(BF16)
        subs = [slice(k * LANE, (k + 1) * LANE) for k in range(S5_SUPER)]
        _s5_scatter_rows(xb, [_dot(u16[:, c], b_ref[k]) for k, c in enumerate(subs)])
        _s5_scatter_rows(xg, [_dot_nt(dy16[:, c], c_ref[k]) for k, c in enumerate(subs)])
        ar, ai = a_ref[RE, :], a_ref[IM, :]
        cr, ci = cf_ref[RE, :], cf_ref[IM, :]

        xs[0] = sb_ref[...]

        def fstep(i, carry):
            sr, si = carry
            xr, xi = _cmul(cr, ci, xb[i, RE, :], xb[i, IM, :])
            sr, si = ar * sr - ai * si + xr, ar * si + ai * sr + xi
            xs[i + 1, RE, :] = sr
            xs[i + 1, IM, :] = si
            return sr, si

        lax.fori_loop(0, tc, fstep, (sb_ref[RE, :], sb_ref[IM, :]), unroll=4)

        def rstep(n, carry):
            gr, gi, a0, a1, a2, a3 = carry
            i = tc - 1 - n
            xr = xg[i, RE, :] + ar * gr + ai * gi
            xi = xg[i, IM, :] + ar * gi - ai * gr
            pr, pi = xs[i, RE, :], xs[i, IM, :]
            br, bi = xb[i, RE, :], xb[i, IM, :]
            a0 = a0 + pr * xr + pi * xi
            a1 = a1 + pr * xi - pi * xr
            a2 = a2 + br * xr + bi * xi
            a3 = a3 + br * xi - bi * xr
            xg[i, RE, :] = cr * xr + ci * xi
            xg[i, IM, :] = cr * xi - ci * xr
            return xr, xi, a0, a1, a2, a3

        init = (gcar[RE, :], gcar[IM, :], acc[0], acc[1], acc[2], acc[3])
        gr, gi, a0, a1, a2, a3 = lax.fori_loop(0, tc, rstep, init, unroll=2)
        gcar[RE, :] = gr
        gcar[IM, :] = gi
        for idx, a in enumerate((a0, a1, a2, a3)):
            acc[idx] = a
        ga_ref[RE, :] = a0
        ga_ref[IM, :] = a1
        gcf_ref[RE, :] = a2
        gcf_ref[IM, :] = a3

        g_rows = _s5_gather_rows(xg, tc)
        s_rows = _s5_gather_rows(xs, tc, first_tile=1)
        for k in range(S5_SUPER):
            cols = subs[k]
            g16 = g_rows[k].astype(BF16)
            s16 = s_rows[k].astype(BF16)
            gb_ref[k] += _dot_tn(u16[:, cols], g16)
            gc_ref[k] += _dot_tn(s16, dy16[:, cols])
            du_ref[:, cols] = (_dot_nt(g16, b_ref[k]) + d_ref[:, cols] * dyv[:, cols]).astype(BF16)
        gd_ref[...] += jnp.sum(dyv * u, axis=0, keepdims=True)

    full = lambda shape: pl.BlockSpec(shape, lambda c: (0,) * len(shape))
    rows = pl.BlockSpec((tc, S5_WIDTH), lambda c: (last - c, 0))
    tile = (S5_TILE, LANE)
    return pl.pallas_call(
        body, name="s5_bwd", grid=(n_chunks,),
        in_specs=[rows, rows, pl.BlockSpec((None, S5_TILE, LANE), lambda c: (last - c, 0, 0)),
                  full(bsg.shape), full(ccat.shape), full(d_row.shape), full(abar_t.shape), full(coef_t.shape)],
        out_specs=[rows, full(bsg.shape), full(ccat.shape), full(d_row.shape), full(tile), full(tile)],
        out_shape=[jax.ShapeDtypeStruct((t, S5_WIDTH), BF16), jax.ShapeDtypeStruct(bsg.shape, F32),
                   jax.ShapeDtypeStruct(ccat.shape, F32), jax.ShapeDtypeStruct(d_row.shape, F32),
                   jax.ShapeDtypeStruct(tile, F32), jax.ShapeDtypeStruct(tile, F32)],
        scratch_shapes=[pltpu.VMEM((tc, S5_TILE, LANE), F32), pltpu.VMEM((tc + 1, S5_TILE, LANE), F32),
                        pltpu.VMEM((tc, S5_TILE, LANE), F32), pltpu.VMEM(tile, F32),
                        pltpu.VMEM((4, S5_HALF, LANE), F32)],
        compiler_params=_params(("arbitrary",)),
    )(proj, dy, sb, bsg, ccat, d_row, abar_t, coef_t)


def _gelu_fwd(y, name="s5_gelu"):
    t, w = y.shape
    tr = _pick(t, 512, SUBLANE)

    def body(y_ref, z_ref):
        z_ref[...] = _gelu_and_grad(y_ref[...])[0].astype(BF16)

    row = pl.BlockSpec((tr, w), lambda i: (i, 0))
    return pl.pallas_call(body, name=name, grid=(t // tr,), in_specs=[row], out_specs=row,
                          out_shape=jax.ShapeDtypeStruct((t, w), BF16), compiler_params=_params(("parallel",)))(y)


def _glu_fwd(y, gl, b, name="s5_glu"):
    t, w = y.shape
    tr = _pick(t, 512, SUBLANE)

    def body(y_ref, gl_ref, b_ref, z2_ref):
        z = _gelu_and_grad(y_ref[...])[0]
        z2_ref[...] = (z * _sigmoid(gl_ref[...] + b_ref[...])).astype(BF16)

    row = pl.BlockSpec((tr, w), lambda i: (i, 0))
    return pl.pallas_call(body, name=name, grid=(t // tr,),
                          in_specs=[row, row, pl.BlockSpec((1, w), lambda i: (0, 0))], out_specs=row,
                          out_shape=jax.ShapeDtypeStruct((t, w), BF16), compiler_params=_params(("parallel",)))(y, gl, b)


def _glu_bwd(y, gl, b, dz2, name="s5_glu_bwd", after=None):
    t, w = y.shape
    tr = _pick(t, 512, SUBLANE)

    def body(y_ref, gl_ref, b_ref, dz2_ref, dgl_ref, dza_ref, db_ref):
        @pl.when(pl.program_id(0) == 0)
        def _():
            db_ref[...] = jnp.zeros_like(db_ref)

        z = _gelu_and_grad(y_ref[...])[0]
        s = _sigmoid(gl_ref[...] + b_ref[...])
        dz2v = dz2_ref[...]
        dgl = dz2v * z * s * (1.0 - s)
        dgl_ref[...] = dgl.astype(BF16)
        dza_ref[...] = dz2v * s
        db_ref[...] += jnp.sum(dgl, axis=0, keepdims=True)

    row = pl.BlockSpec((tr, w), lambda i: (i, 0))
    vec = pl.BlockSpec((1, w), lambda i: (0, 0))
    body, in_specs, args = _ordered(body, [row, row, vec, row], [y, gl, b, dz2], after)
    return pl.pallas_call(body, name=name, grid=(t // tr,), in_specs=in_specs, out_specs=[row, row, vec],
                          out_shape=[jax.ShapeDtypeStruct((t, w), BF16), jax.ShapeDtypeStruct((t, w), F32),
                                     jax.ShapeDtypeStruct((1, w), F32)],
                          compiler_params=_params(("arbitrary",)))(*args)


def _gelu_bwd(y, dza, dzb, name="s5_gelu_bwd", after=None):
    t, w = y.shape
    tr = _pick(t, 512, SUBLANE)

    def body(y_ref, a_ref, b_ref, dy_ref):
        dy_ref[...] = (a_ref[...] + b_ref[...]) * _gelu_and_grad(y_ref[...])[1]

    row = pl.BlockSpec((tr, w), lambda i: (i, 0))
    body, in_specs, args = _ordered(body, [row, row, row], [y, dza, dzb], after)
    return pl.pallas_call(body, name=name, grid=(t // tr,), in_specs=in_specs, out_specs=row,
                          out_shape=jax.ShapeDtypeStruct((t, w), F32), compiler_params=_params(("parallel",)))(*args)


def _tri_dot(tri16, x):
    hi = x.astype(BF16)
    lo = (x - hi.astype(F32)).astype(BF16)
    return _dot(tri16, hi) + _dot(tri16, lo)


def _hgrn_pre(q_in, z, lg):
    lb = _sigmoid(lg[0:1, :] - lg[1:2, :])
    qs, dqs = _silu_and_grad(q_in)
    sz = _sigmoid(z)
    f = lb + (1.0 - lb) * sz
    k = (1.0 - lb) * (1.0 - sz)
    c = HGRN_CHUNK
    r = lax.broadcasted_iota(jnp.int32, (c, c), 0)
    s = lax.broadcasted_iota(jnp.int32, (c, c), 1)
    causal = r >= s
    b = _tri_dot(jnp.where(causal, 1.0, 0.0).astype(BF16), jnp.log(f))
    b_end = b[c - 1:c, :]
    b_mid = b[c // 2 - 1:c // 2, :]
    e_q, e_k, e_0, e_c = jnp.exp(b - b_mid), jnp.exp(b_mid - b), jnp.exp(b), jnp.exp(b_end - b)
    return dict(lb=lb, qs=qs, dqs=dqs, sz=sz, f=f, k=k, causal=causal, b_end=b_end,
                e_q=e_q, e_k=e_k, e_0=e_0, e_c=e_c,
                qt=qs * e_q, kt=k * e_k, q0=qs * e_0, kc=k * e_c)


def _hgrn_fwd(proj, logits, ng):
    t = proj.shape[0]
    c, dh = HGRN_CHUNK, HGRN_DH
    n_chunks = t // c
    subs = HGRN_SUBS if n_chunks % HGRN_SUBS == 0 else 1

    def head(h, sub, q_ref, z_ref, v_ref, g_ref, lg_ref, ng_ref, o_ref, oh_ref, s0_ref, st):
        sl = slice(h * dh, (h + 1) * dh)
        rs = slice(sub * c, (sub + 1) * c)
        s0 = st[h]
        s0_ref[h, sub] = s0
        p = _hgrn_pre(q_ref[rs, sl], z_ref[rs, sl], lg_ref[:, sl])
        v16 = v_ref[rs, sl].astype(BF16)
        a = jnp.where(p["causal"], _dot_nt(p["qt"].astype(BF16), p["kt"].astype(BF16)), 0.0)
        o = _dot_nt(p["q0"].astype(BF16), s0.astype(BF16)) + _dot(a.astype(BF16), v16)
        st[h] = jnp.exp(p["b_end"]) * s0 + _dot_tn(v16, p["kc"].astype(BF16))
        o_ref[rs, sl] = o
        rn = lax.rsqrt(jnp.mean(o * o, axis=-1, keepdims=True) + RMS_EPS)
        oh_ref[rs, sl] = (o * rn * ng_ref[:, sl] * _silu_and_grad(g_ref[rs, sl])[0]).astype(BF16)

    def body(*refs):
        st = refs[-1]

        @pl.when(pl.program_id(0) == 0)
        def _():
            st[...] = jnp.zeros_like(st)

        for sub in range(subs):
            for h in range(HGRN_HEADS):
                head(h, sub, *refs)

    def wide(off):
        return pl.BlockSpec((subs * c, HGRN_WIDTH), lambda i: (i, off))

    return pl.pallas_call(
        body, name="hgrn_fwd", grid=(n_chunks // subs,),
        in_specs=[wide(1), wide(2), wide(3), wide(4),
                  pl.BlockSpec((2, HGRN_WIDTH), lambda i: (0, 0)), pl.BlockSpec((1, HGRN_WIDTH), lambda i: (0, 0))],
        out_specs=[wide(0), wide(0), pl.BlockSpec((HGRN_HEADS, subs, dh, dh), lambda i: (0, i, 0, 0))],
        out_shape=[jax.ShapeDtypeStruct((t, HGRN_WIDTH), F32), jax.ShapeDtypeStruct((t, HGRN_WIDTH), BF16),
                   jax.ShapeDtypeStruct((HGRN_HEADS, n_chunks, dh, dh), F32)],
        scratch_shapes=[pltpu.VMEM((HGRN_HEADS, dh, dh), F32)],
        compiler_params=_params(("arbitrary",)),
    )(proj, proj, proj, proj, logits, ng)


def _hgrn_bwd(proj, o_raw, s0s, doh, logits, ng):
    t = proj.shape[0]
    c, dh = HGRN_CHUNK, HGRN_DH
    n_chunks = t // c
    subs = HGRN_SUBS if n_chunks % HGRN_SUBS == 0 else 1
    last = n_chunks // subs - 1

    def head(h, sub, q_ref, z_ref, v_ref, g_ref, o_ref, s0_ref, doh_ref, lg_ref, ng_ref,
             dq_ref, dz_ref, dv_ref, dg_ref, dng_ref, dlb_ref, dst):
        sl = slice(h * dh, (h + 1) * dh)
        rs = slice(sub * c, (sub + 1) * c)
        p = _hgrn_pre(q_ref[rs, sl], z_ref[rs, sl], lg_ref[:, sl])
        v = v_ref[rs, sl]
        v16 = v.astype(BF16)
        s0 = s0_ref[h, sub]
        ds_end = dst[h]
        ds16 = ds_end.astype(BF16)
        ngv = ng_ref[:, sl]

        o = o_ref[rs, sl]
        dohv = doh_ref[rs, sl]
        sg, dsg = _silu_and_grad(g_ref[rs, sl])
        rn = lax.rsqrt(jnp.mean(o * o, axis=-1, keepdims=True) + RMS_EPS)
        oh = o * rn
        dg_ref[rs, sl] = (dohv * oh * ngv * dsg).astype(BF16)
        don = dohv * sg
        dng_ref[:, sl] += jnp.sum(don * oh, axis=0, keepdims=True)
        doh_n = don * ngv
        do = rn * (doh_n - oh * jnp.mean(doh_n * oh, axis=-1, keepdims=True))
        do16 = do.astype(BF16)

        qt16, kt16, q016, kc16 = (p[n].astype(BF16) for n in ("qt", "kt", "q0", "kc"))
        a = jnp.where(p["causal"], _dot_nt(qt16, kt16), 0.0)
        da = jnp.where(p["causal"], _dot_nt(do16, v16), 0.0)
        da16 = da.astype(BF16)
        dqt = _dot(da16, kt16)
        dq0 = _dot(do16, s0.astype(BF16))
        dkt = _dot_tn(da16, qt16)
        dkc = _dot(v16, ds16)
        dv_ref[rs, sl] = (_dot_tn(a.astype(BF16), do16) + _dot_nt(kc16, ds16)).astype(BF16)
        lam_end = jnp.exp(p["b_end"])
        dst[h] = lam_end * ds_end + _dot_tn(do16, q016)

        qt, kt, q0, kc = (a.astype(F32) for a in (qt16, kt16, q016, kc16))
        db = dqt * qt + dq0 * q0 - dkt * kt - dkc * kc
        db_end = (jnp.sum(dkc * kc, axis=0, keepdims=True)
                  + jnp.sum(ds_end * s0, axis=0, keepdims=True) * lam_end)
        rowi = lax.broadcasted_iota(jnp.int32, (c, dh), 0)
        db = db + jnp.where(rowi == c - 1, db_end, 0.0)
        r = lax.broadcasted_iota(jnp.int32, (c, c), 0)
        s = lax.broadcasted_iota(jnp.int32, (c, c), 1)
        dlf = _tri_dot(jnp.where(s >= r, 1.0, 0.0).astype(BF16), db)

        dqs = dqt * p["e_q"] + dq0 * p["e_0"]
        dq_ref[rs, sl] = (dqs * p["dqs"]).astype(BF16)
        dk = dkt * p["e_k"] + dkc * p["e_c"]
        sz, lb = p["sz"], p["lb"]
        common = dlf / p["f"] - dk
        dz_ref[rs, sl] = ((1.0 - lb) * sz * (1.0 - sz) * common).astype(BF16)
        dlb_ref[:, sl] += jnp.sum((1.0 - sz) * common, axis=0, keepdims=True)

    def body(*refs):
        dng_ref, dlb_ref, dst = refs[-3:]

        @pl.when(pl.program_id(0) == 0)
        def _():
            dst[...] = jnp.zeros_like(dst)
            dng_ref[...] = jnp.zeros_like(dng_ref)
            dlb_ref[...] = jnp.zeros_like(dlb_ref)

        for sub in reversed(range(subs)):
            for h in range(HGRN_HEADS):
                head(h, sub, *refs)

    def wide(off):
        return pl.BlockSpec((subs * c, HGRN_WIDTH), lambda i: (last - i, off))

    vec = pl.BlockSpec((1, HGRN_WIDTH), lambda i: (0, 0))
    act = jax.ShapeDtypeStruct((t, HGRN_WIDTH), BF16)
    vsh = jax.ShapeDtypeStruct((1, HGRN_WIDTH), F32)
    return pl.pallas_call(
        body, name="hgrn_bwd", grid=(n_chunks // subs,),
        in_specs=[wide(1), wide(2), wide(3), wide(4), wide(0),
                  pl.BlockSpec((HGRN_HEADS, subs, dh, dh), lambda i: (0, last - i, 0, 0)),
                  wide(0), pl.BlockSpec((2, HGRN_WIDTH), lambda i: (0, 0)), vec],
        out_specs=[wide(0), wide(0), wide(0), wide(0), vec, vec],
        out_shape=[act, act, act, act, vsh, vsh],
        scratch_shapes=[pltpu.VMEM((HGRN_HEADS, dh, dh), F32)],
        compiler_params=_params(("arbitrary",)),
    )(proj, proj, proj, proj, o_raw, s0s, doh, logits, ng)


def _lb_bwd(logits, dlb):
    def body(lg_ref, d_ref, o_ref):
        lg = lg_ref[...]
        lb = _sigmoid(lg[0:1, :] - lg[1:2, :])
        g = d_ref[...] * lb * (1.0 - lb)
        o_ref[0:1, :] = g
        o_ref[1:2, :] = -g

    return pl.pallas_call(body, name="hgrn_lb_bwd", out_shape=jax.ShapeDtypeStruct(logits.shape, F32),
                          compiler_params=_params())(logits, dlb)


MERGE_TC = 1024
GS_BLOCK = (S5_WIDTH + 4 * HGRN_WIDTH) // MERGE_TC
GH_BLOCK = GS_BLOCK + D_MODEL // MERGE_TC


def _merge_fwd(proj, ys, yh):
    t = proj.shape[0]
    tr = _pick(t, 512, SUBLANE)

    def body(gs_ref, gh_ref, ys_ref, yh_ref, m_ref):
        m_ref[...] = (_sigmoid(gs_ref[...]) * ys_ref[...] + _sigmoid(gh_ref[...]) * yh_ref[...]).astype(BF16)

    blk = pl.BlockSpec((tr, MERGE_TC), lambda i, j: (i, j))
    return pl.pallas_call(
        body, name="merge_fwd", grid=(t // tr, D_MODEL // MERGE_TC),
        in_specs=[pl.BlockSpec((tr, MERGE_TC), lambda i, j: (i, GS_BLOCK + j)),
                  pl.BlockSpec((tr, MERGE_TC), lambda i, j: (i, GH_BLOCK + j)), blk, blk],
        out_specs=blk, out_shape=jax.ShapeDtypeStruct((t, D_MODEL), BF16),
        compiler_params=_params(("parallel", "parallel")),
    )(proj, proj, ys, yh)


def _merge_bwd(proj, ys, yh, dm, after=None):
    t = proj.shape[0]
    tr = _pick(t, 512, SUBLANE)

    def body(gs_ref, gh_ref, ys_ref, yh_ref, dm_ref, dys_ref, dyh_ref, dgs_ref, dgh_ref):
        dmv = dm_ref[...]
        ss, sh = _sigmoid(gs_ref[...]), _sigmoid(gh_ref[...])
        dys_ref[...] = (dmv * ss).astype(BF16)
        dyh_ref[...] = (dmv * sh).astype(BF16)
        dgs_ref[...] = (dmv * ys_ref[...] * ss * (1.0 - ss)).astype(BF16)
        dgh_ref[...] = (dmv * yh_ref[...] * sh * (1.0 - sh)).astype(BF16)

    blk = pl.BlockSpec((tr, MERGE_TC), lambda i, j: (i, j))
    sh16 = jax.ShapeDtypeStruct((t, D_MODEL), BF16)
    in_specs = [pl.BlockSpec((tr, MERGE_TC), lambda i, j: (i, GS_BLOCK + j)),
                pl.BlockSpec((tr, MERGE_TC), lambda i, j: (i, GH_BLOCK + j)), blk, blk, blk]
    body, in_specs, args = _ordered(body, in_specs, [proj, proj, ys, yh, dm], after)
    return pl.pallas_call(
        body, name="merge_bwd", grid=(t // tr, D_MODEL // MERGE_TC),
        in_specs=in_specs,
        out_specs=[blk, blk, blk, blk], out_shape=[sh16, sh16, sh16, sh16],
        compiler_params=_params(("parallel", "parallel")),
    )(*args)


FFN_TC = 128
FFN_ROWS = 128
HALO = SUBLANE


def _pad_rows(dst, src_ref):
    t, c = src_ref.shape
    dst[0:HALO, :] = jnp.zeros((HALO, c), F32)
    dst[HALO:HALO + t, :] = src_ref[...]
    dst[HALO + t:HALO + t + HALO, :] = jnp.zeros((HALO, c), F32)


def _conv3(padded, w, b, r0, nrows):
    x0 = padded[HALO + r0:HALO + r0 + nrows, :]
    x1 = padded[HALO + r0 - 1:HALO + r0 - 1 + nrows, :]
    x2 = padded[HALO + r0 - 2:HALO + r0 - 2 + nrows, :]
    return b + w[0:1, :] * x2 + w[1:2, :] * x1 + w[2:3, :] * x0, (x0, x1, x2)


def _ffn_act_fwd(up, cw, cb):
    t = up.shape[0]
    rows = _pick(t, FFN_ROWS, SUBLANE)
    nvb = D_FF // FFN_TC

    def body(ug_ref, uv_ref, wg_ref, wv_ref, bg_ref, bv_ref, act_ref, pg, pv):
        wg, wv, bg, bv = wg_ref[...], wv_ref[...], bg_ref[...], bv_ref[...]
        _pad_rows(pg, ug_ref)
        _pad_rows(pv, uv_ref)
        for r0 in range(0, t, rows):
            cg, _ = _conv3(pg, wg, bg, r0, rows)
            cv, _ = _conv3(pv, wv, bv, r0, rows)
            act_ref[r0:r0 + rows, :] = (_silu_and_grad(cg)[0] * cv).astype(BF16)

    def colblk(nrow, off):
        return pl.BlockSpec((nrow, FFN_TC), lambda j: (0, off + j))

    return pl.pallas_call(
        body, name="ffn_act_fwd", grid=(nvb,),
        in_specs=[colblk(t, 0), colblk(t, nvb), colblk(3, 0), colblk(3, nvb), colblk(1, 0), colblk(1, nvb)],
        out_specs=colblk(t, 0), out_shape=jax.ShapeDtypeStruct((t, D_FF), BF16),
        scratch_shapes=[pltpu.VMEM((t + 2 * HALO, FFN_TC), F32), pltpu.VMEM((t + 2 * HALO, FFN_TC), F32)],
        compiler_params=_params(("parallel",)),
    )(up, up, cw, cw, cb, cb)


def _ffn_act_bwd(up, dact, cw, cb, after=None):
    t = up.shape[0]
    rows = _pick(t, FFN_ROWS, SUBLANE)
    nvb = D_FF // FFN_TC

    def body(ug_ref, uv_ref, da_ref, wg_ref, wv_ref, bg_ref, bv_ref,
             dug_ref, duv_ref, dwg_ref, dwv_ref, dbg_ref, dbv_ref, pg, pv, dcs):
        wg, wv, bg, bv = wg_ref[...], wv_ref[...], bg_ref[...], bv_ref[...]
        _pad_rows(pg, ug_ref)
        _pad_rows(pv, uv_ref)
        ext = rows + HALO
        acc_g = [jnp.zeros((1, FFN_TC), F32) for _ in range(4)]
        acc_v = [jnp.zeros((1, FFN_TC), F32) for _ in range(4)]
        for r0 in range(0, t, rows):
            cg, xg = _conv3(pg, wg, bg, r0, ext)
            cv, xv = _conv3(pv, wv, bv, r0, ext)
            if r0 + ext <= t:
                dav = da_ref[r0:r0 + ext, :]
            else:
                dav = jnp.concatenate([da_ref[r0:t, :], jnp.zeros((HALO, FFN_TC), F32)], axis=0)
            sg, dsg = _silu_and_grad(cg)
            for h, (dconv, xs, w, acc, out) in enumerate(((dav * cv * dsg, xg, wg, acc_g, dug_ref),
                                                           (dav * sg, xv, wv, acc_v, duv_ref))):
                dcs[h] = dconv
                d0 = dconv[0:rows, :]
                d1 = dcs[h, 1:rows + 1, :]
                d2 = dcs[h, 2:rows + 2, :]
                out[r0:r0 + rows, :] = (w[2:3, :] * d0 + w[1:2, :] * d1 + w[0:1, :] * d2).astype(BF16)
                x0, x1, x2 = xs
                acc[0] = acc[0] + jnp.sum(d0 * x2[0:rows, :], axis=0, keepdims=True)
                acc[1] = acc[1] + jnp.sum(d0 * x1[0:rows, :], axis=0, keepdims=True)
                acc[2] = acc[2] + jnp.sum(d0 * x0[0:rows, :], axis=0, keepdims=True)
                acc[3] = acc[3] + jnp.sum(d0, axis=0, keepdims=True)
        for acc, dw_ref, db_ref in ((acc_g, dwg_ref, dbg_ref), (acc_v, dwv_ref, dbv_ref)):
            dw_ref[0:1, :] = acc[0]
            dw_ref[1:2, :] = acc[1]
            dw_ref[2:3, :] = acc[2]
            db_ref[...] = acc[3]

    def colblk(nrow, off):
        return pl.BlockSpec((nrow, FFN_TC), lambda j: (0, off + j))

    in_specs = [colblk(t, 0), colblk(t, nvb), colblk(t, 0), colblk(3, 0), colblk(3, nvb), colblk(1, 0), colblk(1, nvb)]
    body, in_specs, args = _ordered(body, in_specs, [up, up, dact, cw, cw, cb, cb], after)
    return pl.pallas_call(
        body, name="ffn_act_bwd", grid=(nvb,),
        in_specs=in_specs,
        out_specs=[colblk(t, 0), colblk(t, 0), colblk(3, 0), colblk(3, 0), colblk(1, 0), colblk(1, 0)],
        out_shape=[jax.ShapeDtypeStruct((t, D_FF), BF16), jax.ShapeDtypeStruct((t, D_FF), BF16),
                   jax.ShapeDtypeStruct((3, D_FF), F32), jax.ShapeDtypeStruct((3, D_FF), F32),
                   jax.ShapeDtypeStruct((1, D_FF), F32), jax.ShapeDtypeStruct((1, D_FF), F32)],
        scratch_shapes=[pltpu.VMEM((t + 2 * HALO, FFN_TC), F32), pltpu.VMEM((t + 2 * HALO, FFN_TC), F32),
                        pltpu.VMEM((2, rows + HALO, FFN_TC), F32)],
        compiler_params=_params(("parallel",)),
    )(*args)


def _all_gather(shards, name):
    nw = len(shards)

    def body(*refs):
        x_refs, out_refs = refs[:nw], refs[nw:2 * nw]
        send_sems, recv_sems, local_sems = refs[2 * nw:]
        x, y, c = lax.axis_index("x"), lax.axis_index("y"), lax.axis_index("c")
        me, sibling = (x, y, c), (x, y, 1 - c)
        chips = [(1 - x, y), (x, 1 - y), (1 - x, 1 - y)]

        def copy(w, k, block, to, src=None):
            slot = out_refs[w].at[4 * block[0] + 2 * block[1] + block[2]]
            return pltpu.make_async_remote_copy(
                src_ref=slot if src is None else src, dst_ref=slot,
                send_sem=send_sems.at[w, k], recv_sem=recv_sems.at[w, k],
                device_id=to, device_id_type=MESH)

        mine, first, passed = [], [], []
        for w in range(nw):
            cp = pltpu.make_async_copy(x_refs[w], out_refs[w].at[4 * x + 2 * y + c], local_sems.at[w])
            cp.start()
            mine.append(cp)
            first.append(copy(w, 0, me, sibling, src=x_refs[w]))
            first += [copy(w, 1 + j, me, (*chip, c), src=x_refs[w]) for j, chip in enumerate(chips)]
        for cp in first:
            cp.start()
        for w in range(nw):
            for j, chip in enumerate(chips):
                copy(w, 1 + j, (*chip, c), me).wait_recv()
                fwd = copy(w, 4 + j, (*chip, c), sibling)
                fwd.start()
                passed.append(fwd)
        for w in range(nw):
            copy(w, 0, sibling, me).wait_recv()
            for j, chip in enumerate(chips):
                copy(w, 4 + j, (*chip, 1 - c), me).wait_recv()
        for cp in first + passed:
            cp.wait_send()
        for cp in mine:
            cp.wait()

    anyspec = pl.BlockSpec(memory_space=pl.ANY)
    return pl.pallas_call(
        body, name=name,
        in_specs=[anyspec] * nw, out_specs=[anyspec] * nw,
        out_shape=[jax.ShapeDtypeStruct((N_DEV,) + s.shape, s.dtype) for s in shards],
        scratch_shapes=[pltpu.SemaphoreType.DMA((nw, 7)), pltpu.SemaphoreType.DMA((nw, 7)),
                        pltpu.SemaphoreType.DMA((nw,))],
    )(*shards)


HBM_SPEC = pl.BlockSpec(memory_space=pltpu.HBM)
SEM_SPEC = pl.BlockSpec(memory_space=pltpu.SEMAPHORE)
ANY_SPEC = pl.BlockSpec(memory_space=pl.ANY)
DATAFLOW = pltpu.SideEffectType.DATAFLOW_SIDE_EFFECTING


def _my_index():
    return 4 * lax.axis_index("x") + 2 * lax.axis_index("y") + lax.axis_index("c")


def _peers():
    x, y, c = lax.axis_index("x"), lax.axis_index("y"), lax.axis_index("c")
    peers = []
    for k in range(1, N_DEV):
        px = 1 - x if k & 4 else x
        py = 1 - y if k & 2 else y
        pc = 1 - c if k & 1 else c
        peers.append((k, (px, py, pc), 4 * px + 2 * py + pc))
    return peers


def _split_copy(src_ref, land_ref, send_sems, recv_sems, w, k, peer, slot, scatter, outgoing):
    return pltpu.make_async_remote_copy(
        src_ref=src_ref.at[slot] if scatter else src_ref,
        dst_ref=land_ref.at[_my_index() if outgoing else slot],
        send_sem=send_sems.at[w * (N_DEV - 1) + k - 1], recv_sem=recv_sems.at[w * (N_DEV - 1) + k - 1],
        device_id=peer, device_id_type=MESH)


def _landing_zone(src, scatter):
    me = _my_index()
    own = lax.dynamic_index_in_dim(src, me, 0, keepdims=True) if scatter else src[None]
    shape = src.shape if scatter else (N_DEV,) + src.shape
    return lax.dynamic_update_slice_in_dim(lax.empty(shape, src.dtype), own, me, 0)


def _exchange_start(srcs, scatter, after, name, lands=None):
    nw = len(srcs)
    if lands is None:
        lands = [_landing_zone(s, scatter) for s in srcs]

    afters = [] if after is None else [after]

    def body(*refs):
        s_refs, l_refs = refs[:nw], refs[nw:2 * nw]
        send_sems, recv_sems = refs[2 * nw + len(afters)], refs[2 * nw + len(afters) + 1]
        token = refs[-1]
        for w in range(nw):
            for k, peer, slot in _peers():
                _split_copy(s_refs[w], l_refs[w], send_sems, recv_sems, w, k, peer, slot, scatter, True).start()
        token[...] = jnp.zeros_like(token)

    sems = pltpu.SemaphoreType.DMA((nw * (N_DEV - 1),))
    outs = pl.pallas_call(
        body, name=name,
        out_shape=(sems, sems, *[pltpu.HBM(a.shape, a.dtype) for a in (*srcs, *lands)],
                   jax.ShapeDtypeStruct((SUBLANE, LANE), F32)),
        in_specs=[HBM_SPEC] * (2 * nw) + [ANY_SPEC] * len(afters),
        out_specs=(SEM_SPEC, SEM_SPEC, *[HBM_SPEC] * (2 * nw), pl.BlockSpec(memory_space=pltpu.VMEM)),
        input_output_aliases={i: 2 + i for i in range(2 * nw)},
        compiler_params=pltpu.CompilerParams(has_side_effects=DATAFLOW),
    )(*[pltpu.with_memory_space_constraint(a, pltpu.HBM) for a in (*srcs, *lands)], *afters)
    return dict(sems=outs[:2], srcs=outs[2:2 + nw], lands=outs[2 + nw:2 + 2 * nw], token=outs[-1], scatter=scatter)


def _exchange_wait(handle, afters, name):
    srcs, lands, scatter = handle["srcs"], handle["lands"], handle["scatter"]
    nw = len(srcs)

    def body(*refs):
        s_refs, l_refs = refs[:nw], refs[nw:2 * nw]
        send_sems, recv_sems = refs[2 * nw], refs[2 * nw + 1]
        for w in range(nw):
            for k, peer, slot in _peers():
                cp = _split_copy(s_refs[w], l_refs[w], send_sems, recv_sems, w, k, peer, slot, scatter, False)
                cp.wait_send()
                cp.wait_recv()

    outs = pl.pallas_call(
        body, name=name,
        out_shape=tuple(pltpu.HBM(a.shape, a.dtype) for a in (*srcs, *lands)),
        in_specs=[HBM_SPEC] * (2 * nw) + [SEM_SPEC, SEM_SPEC] + [ANY_SPEC] * len(afters),
        out_specs=tuple([HBM_SPEC] * (2 * nw)),
        input_output_aliases={i: i for i in range(2 * nw)},
        compiler_params=pltpu.CompilerParams(has_side_effects=DATAFLOW),
    )(*srcs, *lands, *handle["sems"], *afters)
    return list(outs[nw:])


def _chips_and_sibling():
    x, y, c = lax.axis_index("x"), lax.axis_index("y"), lax.axis_index("c")
    return [(1 - x, y), (x, 1 - y), (1 - x, 1 - y)], (x, y, 1 - c), c


def _slot(px, py, pc):
    return 4 * px + 2 * py + pc


def _two_level_start(shards, name):
    nw = len(shards)
    lands = [_landing_zone(s, False) for s in shards]

    def body(*refs):
        s_refs, l_refs = refs[:nw], refs[nw:2 * nw]
        send_sems, recv_sems, token = refs[2 * nw], refs[2 * nw + 1], refs[-1]
        chips, sibling, c = _chips_and_sibling()
        for w in range(nw):
            for k, to in enumerate([sibling] + [(*chip, c) for chip in chips]):
                pltpu.make_async_remote_copy(
                    src_ref=s_refs[w], dst_ref=l_refs[w].at[_my_index()],
                    send_sem=send_sems.at[4 * w + k], recv_sem=recv_sems.at[4 * w + k],
                    device_id=to, device_id_type=MESH).start()
        token[...] = jnp.zeros_like(token)

    sems = pltpu.SemaphoreType.DMA((4 * nw,))
    outs = pl.pallas_call(
        body, name=name,
        out_shape=(sems, sems, *[pltpu.HBM(a.shape, a.dtype) for a in (*shards, *lands)],
                   jax.ShapeDtypeStruct((SUBLANE, LANE), F32)),
        in_specs=[HBM_SPEC] * (2 * nw),
        out_specs=(SEM_SPEC, SEM_SPEC, *[HBM_SPEC] * (2 * nw), pl.BlockSpec(memory_space=pltpu.VMEM)),
        input_output_aliases={i: 2 + i for i in range(2 * nw)},
        compiler_params=pltpu.CompilerParams(has_side_effects=DATAFLOW),
    )(*[pltpu.with_memory_space_constraint(a, pltpu.HBM) for a in (*shards, *lands)])
    return dict(sems=outs[:2], srcs=outs[2:2 + nw], lands=outs[2 + nw:2 + 2 * nw], token=outs[-1])


def _two_level_pass(handle, afters, name):
    srcs, lands = handle["srcs"], handle["lands"]
    nw = len(srcs)

    def body(*refs):
        s_refs, l_refs = refs[:nw], refs[nw:2 * nw]
        send_a, recv_a = refs[2 * nw], refs[2 * nw + 1]
        send_b, recv_b = refs[2 * nw + 2 + len(afters)], refs[2 * nw + 3 + len(afters)]
        chips, sibling, c = _chips_and_sibling()
        for w in range(nw):
            for j, chip in enumerate(chips):
                landed = l_refs[w].at[_slot(*chip, c)]
                pltpu.make_async_remote_copy(
                    src_ref=s_refs[w], dst_ref=landed, send_sem=send_a.at[4 * w + 1 + j], recv_sem=recv_a.at[4 * w + 1 + j],
                    device_id=(*chip, c), device_id_type=MESH).wait_recv()
                pltpu.make_async_remote_copy(
                    src_ref=landed, dst_ref=landed, send_sem=send_b.at[3 * w + j], recv_sem=recv_b.at[3 * w + j],
                    device_id=sibling, device_id_type=MESH).start()

    sems = pltpu.SemaphoreType.DMA((3 * nw,))
    outs = pl.pallas_call(
        body, name=name,
        out_shape=(sems, sems, *[pltpu.HBM(a.shape, a.dtype) for a in (*srcs, *lands)]),
        in_specs=[HBM_SPEC] * (2 * nw) + [SEM_SPEC, SEM_SPEC] + [ANY_SPEC] * len(afters),
        out_specs=(SEM_SPEC, SEM_SPEC, *[HBM_SPEC] * (2 * nw)),
        input_output_aliases={i: 2 + i for i in range(2 * nw)},
        compiler_params=pltpu.CompilerParams(has_side_effects=DATAFLOW),
    )(*srcs, *lands, *handle["sems"], *afters)
    return dict(sems=handle["sems"], sems_pass=outs[:2], srcs=outs[2:2 + nw], lands=outs[2 + nw:2 + 2 * nw])


def _two_level_wait(handle, name):
    srcs, lands = handle["srcs"], handle["lands"]
    nw = len(srcs)

    def body(*refs):
        s_refs, l_refs = refs[:nw], refs[nw:2 * nw]
        send_a, recv_a, send_b, recv_b = refs[2 * nw:2 * nw + 4]
        chips, sibling, c = _chips_and_sibling()
        x, y = sibling[0], sibling[1]
        for w in range(nw):
            first = pltpu.make_async_remote_copy(
                src_ref=s_refs[w], dst_ref=l_refs[w].at[_slot(x, y, 1 - c)], send_sem=send_a.at[4 * w],
                recv_sem=recv_a.at[4 * w], device_id=sibling, device_id_type=MESH)
            first.wait_send()
            first.wait_recv()
            for j, chip in enumerate(chips):
                pltpu.make_async_remote_copy(
                    src_ref=s_refs[w], dst_ref=l_refs[w].at[_slot(*chip, c)], send_sem=send_a.at[4 * w + 1 + j],
                    recv_sem=recv_a.at[4 * w + 1 + j], device_id=(*chip, c), device_id_type=MESH).wait_send()
                passed = pltpu.make_async_remote_copy(
                    src_ref=l_refs[w].at[_slot(*chip, c)], dst_ref=l_refs[w].at[_slot(*chip, 1 - c)],
                    send_sem=send_b.at[3 * w + j], recv_sem=recv_b.at[3 * w + j], device_id=sibling, device_id_type=MESH)
                passed.wait_send()
                passed.wait_recv()

    outs = pl.pallas_call(
        body, name=name,
        out_shape=tuple(pltpu.HBM(a.shape, a.dtype) for a in (*srcs, *lands)),
        in_specs=[HBM_SPEC] * (2 * nw) + [SEM_SPEC] * 4,
        out_specs=tuple([HBM_SPEC] * (2 * nw)),
        input_output_aliases={i: i for i in range(2 * nw)},
        compiler_params=pltpu.CompilerParams(has_side_effects=DATAFLOW),
    )(*srcs, *lands, *handle["sems"], *handle["sems_pass"])
    return list(outs[nw:])


def _adamw(w, g, m, v):
    m = ADAM_B1 * m + (1.0 - ADAM_B1) * g
    v = ADAM_B2 * v + (1.0 - ADAM_B2) * (g * g)
    m_hat = m / (1.0 - ADAM_B1 ** ADAM_STEP)
    v_hat = v / (1.0 - ADAM_B2 ** ADAM_STEP)
    delta = -ADAM_LR * (m_hat / (jnp.sqrt(v_hat) + ADAM_EPS) + ADAM_WD * w)
    return delta, m, v


def _sum_adam(parts, w, m, v, name):
    _, r, c = parts.shape
    tr = _pick(r, 256, 16)

    def body(p_ref, w_ref, m_ref, v_ref, g_ref, d_ref, mo_ref, vo_ref):
        g = p_ref[0].astype(F32)
        for s in range(1, N_DEV):
            g = g + p_ref[s].astype(F32)
        g_ref[...] = g
        d_ref[...], mo_ref[...], vo_ref[...] = _adamw(w_ref[...], g, m_ref[...], v_ref[...])

    row = pl.BlockSpec((tr, c), lambda i: (i, 0))
    sh = jax.ShapeDtypeStruct((r, c), F32)
    return pl.pallas_call(
        body, name=name, grid=(r // tr,),
        in_specs=[pl.BlockSpec((N_DEV, tr, c), lambda i: (0, i, 0)), row, row, row],
        out_specs=[row, row, row, row], out_shape=[sh, sh, sh, sh],
        compiler_params=_params(("parallel",)),
    )(parts, w, m, v)


def _sum_slots(parts, name):
    _, r, c = parts.shape
    tr = _pick(r, 512, SUBLANE)

    def body(p_ref, o_ref):
        g = p_ref[0]
        for s in range(1, N_DEV):
            g = g + p_ref[s]
        o_ref[...] = g

    return pl.pallas_call(
        body, name=name, grid=(r // tr,),
        in_specs=[pl.BlockSpec((N_DEV, tr, c), lambda i: (0, i, 0))],
        out_specs=pl.BlockSpec((tr, c), lambda i: (i, 0)), out_shape=jax.ShapeDtypeStruct((r, c), F32),
        compiler_params=_params(("parallel",)),
    )(parts)


def _adam_rows(g, w, m, v, name):
    r, c = g.shape
    tr = _pick(r, 512, SUBLANE)

    def body(g_ref, w_ref, m_ref, v_ref, d_ref, mo_ref, vo_ref):
        d_ref[...], mo_ref[...], vo_ref[...] = _adamw(w_ref[...], g_ref[...], m_ref[...], v_ref[...])

    row = pl.BlockSpec((tr, c), lambda i: (i, 0))
    sh = jax.ShapeDtypeStruct((r, c), F32)
    return pl.pallas_call(body, name=name, grid=(r // tr,), in_specs=[row] * 4, out_specs=[row] * 3,
                          out_shape=[sh, sh, sh], compiler_params=_params(("parallel",)))(g, w, m, v)


def _pack(arrays):
    flat = jnp.concatenate([a.reshape(-1).astype(F32) for a in arrays])
    pad = (-flat.shape[0]) % (SUBLANE * LANE)
    return jnp.pad(flat, (0, pad)).reshape(-1, LANE)


def _unpack(packed, shapes):
    flat = packed.reshape(-1)
    out, off = [], 0
    for s in shapes:
        n = math.prod(s)
        out.append(flat[off:off + n].reshape(s))
        off += n
    return out


def _block_diag(t):
    eye = jnp.eye(S5_SUPER, dtype=bool)
    bd = jnp.where(eye[None, :, None, :, None], t[:, :, :, None, :], 0.0)
    return bd.reshape(S5_SUPER, S5_SUPER * t.shape[2], S5_SUPER * t.shape[3])


def _diag_blocks(dense, a, b):
    x = dense.reshape(S5_SUPER, S5_SUPER, a, S5_SUPER, b)
    return jnp.moveaxis(jnp.diagonal(x, axis1=1, axis2=3), -1, 1)


def _s5_layouts(b_re, b_im, c_re, c_im, d):
    g2 = (S5_GROUPS // S5_SUPER, S5_SUPER)
    bt = lambda b: _block_diag(b.reshape(*g2, S5_STATE, S5_GROUP).transpose(0, 1, 3, 2))
    ct = lambda c: _block_diag(c.reshape(*g2, S5_GROUP, S5_STATE).transpose(0, 1, 3, 2))
    bsg = jnp.concatenate([bt(b_re), bt(b_im)], axis=2).astype(BF16)
    ccat = jnp.concatenate([ct(c_re), -ct(c_im)], axis=1).astype(BF16)
    return bsg, ccat, d.reshape(1, S5_WIDTH)


def _s5_param_grads(gb, gc):
    n = S5_LANES
    gb_re = _diag_blocks(gb[:, :, 0:n], S5_GROUP, S5_STATE).transpose(0, 1, 3, 2).reshape(S5_GROUPS, S5_STATE, S5_GROUP)
    gb_im = _diag_blocks(gb[:, :, n:2 * n], S5_GROUP, S5_STATE).transpose(0, 1, 3, 2).reshape(S5_GROUPS, S5_STATE, S5_GROUP)
    gc_re = _diag_blocks(gc[:, 0:n, :], S5_STATE, S5_GROUP).transpose(0, 1, 3, 2).reshape(S5_GROUPS, S5_GROUP, S5_STATE)
    gc_im = -_diag_blocks(gc[:, n:2 * n, :], S5_STATE, S5_GROUP).transpose(0, 1, 3, 2).reshape(S5_GROUPS, S5_GROUP, S5_STATE)
    return gb_re, gb_im, gc_re, gc_im


def _local_step(x, target, weight, emit, small, after=None):
    sp = small
    a_re, a_im = sp["s5_a_re"], sp["s5_a_im"]
    ldt = sp["s5_log_dt"].reshape(S5_GROUPS, 1)

    h1 = _rms_fwd(x, sp["ln_mix_g"], "rms_mix", after=after)
    w_in = weight("w_in", h1)
    proj = _mm_nn(h1, w_in, "mm_in", after=weight("after_w_in", None))
    conv_w = weight("conv_w", None)
    disc = _s5_param_fwd(a_re, a_im, ldt)
    bsg, ccat, d_row = sp["s5_layouts"]
    abar_t, coef_t = _s5_to_tile(disc[0], disc[1]), _s5_to_tile(disc[2], disc[3])
    y, sb = _s5_fwd(proj, bsg, ccat, d_row, abar_t, coef_t)
    z16 = _gelu_fwd(y)
    w_glu = weight("s5_w_glu", z16)
    gl = _mm_nn(z16, w_glu, "mm_glu")
    z2 = _glu_fwd(y, gl, sp["s5_b_glu"])
    w_ps = weight("w_proj_s5", z2)
    ys = _mm_nn(z2, w_ps, "mm_proj_s5")
    o_raw, oh, s0s = _hgrn_fwd(proj, sp["hgrn_lb_logits"], sp["hgrn_norm_g"])
    w_ph = weight("w_proj_hgrn", oh)
    yh = _mm_nn(oh, w_ph, "mm_proj_hgrn")
    merged = _merge_fwd(proj, ys, yh)
    w_out = weight("w_out", merged)
    x1 = _mm_nn(merged, w_out, "mm_out", res=x)
    h2 = _rms_fwd(x1, sp["ln_ffn_g"], "rms_ffn")
    w_up = weight("w_up", h2)
    up = _mm_nn(h2, w_up, "mm_up")
    act = _ffn_act_fwd(up, conv_w, sp["conv_b"])
    w_down = weight("w_down", act)
    x2 = _mm_nn(act, w_down, "mm_down", res=x1)
    dx2, dx2_16, g_ln_final, loss = _loss_head(x2, sp["ln_final_g"], target)

    dact = _mm_nt(dx2_16, w_down, "mm_down_dx")
    tok = emit("w_down", _mm_tn(act, dx2_16, 1, "mm_down_dw"))
    dup_g, dup_v, dcw_g, dcw_v, dcb_g, dcb_v = _ffn_act_bwd(up, dact, conv_w, sp["conv_b"], after=tok)
    dup = jnp.concatenate([dup_g, dup_v], axis=1)
    g_conv_w = jnp.concatenate([dcw_g, dcw_v], axis=1)
    g_conv_b = jnp.concatenate([dcb_g, dcb_v], axis=1)
    dh2 = _mm_nt(dup, w_up, "mm_up_dx")
    tok = emit("w_up", _mm_tn(h2, dup, N_DEV, "mm_up_dw"))
    dx1, dx1_16, g_ln_ffn = _rms_bwd(x1, sp["ln_ffn_g"], dh2, dx2, "rms_ffn_bwd", True, after=tok)

    dmerged = _mm_nt(dx1_16, w_out, "mm_out_dx")
    tok = emit("w_out", _mm_tn(merged, dx1_16, 1, "mm_out_dw"))
    dys, dyh, dgs, dgh = _merge_bwd(proj, ys, yh, dmerged, after=tok)
    doh = _mm_nt(dyh, w_ph, "mm_proj_hgrn_dx")
    tok = emit("w_proj_hgrn", _mm_tn(oh, dyh, N_DEV, "mm_proj_hgrn_dw"))
    dz2 = _mm_nt(dys, w_ps, "mm_proj_s5_dx", after=tok)
    tok = emit("w_proj_s5", _mm_tn(z2, dys, N_DEV, "mm_proj_s5_dw"))
    dgl, dza, g_b_glu = _glu_bwd(y, gl, sp["s5_b_glu"], dz2, after=tok)
    dzb = _mm_nt(dgl, w_glu, "mm_glu_dx")
    tok = emit("s5_w_glu", _mm_tn(z16, dgl, 1, "mm_glu_dw"))
    dy = _gelu_bwd(y, dza, dzb, after=tok)
    du, gb, gc, gd, g_abar_t, g_coef_t = _s5_bwd(proj, dy, sb, bsg, ccat, d_row, abar_t, coef_t)
    g_a_re, g_a_im, g_ldt = _s5_param_bwd(a_re, a_im, ldt, [*_s5_from_tile(g_abar_t), *_s5_from_tile(g_coef_t)])
    g_b_re, g_b_im, g_c_re, g_c_im = _s5_param_grads(gb, gc)
    dq, dz, dv, dg, g_norm, dlb = _hgrn_bwd(proj, o_raw, s0s, doh, sp["hgrn_lb_logits"], sp["hgrn_norm_g"])
    g_logits = _lb_bwd(sp["hgrn_lb_logits"], dlb)

    small_g = dict(s5_a_re=g_a_re, s5_a_im=g_a_im, s5_log_dt=g_ldt.reshape(1, S5_GROUPS),
                   s5_b_re=g_b_re, s5_b_im=g_b_im, s5_c_re=g_c_re, s5_c_im=g_c_im,
                   s5_d=gd.reshape(S5_GROUPS, S5_GROUP), s5_b_glu=g_b_glu, hgrn_lb_logits=g_logits,
                   hgrn_norm_g=g_norm, ln_ffn_g=g_ln_ffn, conv_w=g_conv_w, conv_b=g_conv_b, ln_final_g=g_ln_final,
                   loss=loss[0, 0:1])
    tok_small = emit("small", small_g)

    dproj = jnp.concatenate([du, dq, dz, dv, dg, dgs, dgh], axis=1)
    tok = emit("w_in", _mm_tn(h1, dproj, N_DEV, "mm_in_dw", after=tok_small))
    dh1 = _mm_nt(dproj, w_in, "mm_in_dx")
    grad_x, g_ln_mix = _rms_bwd(x, sp["ln_mix_g"], dh1, dx1, "rms_mix_bwd", False, after=tok)
    return grad_x, g_ln_mix


BIG = ("w_in", "s5_w_glu", "w_proj_s5", "w_proj_hgrn", "w_out", "w_up", "w_down")
COL_SHARDED = ("w_in", "w_proj_s5", "w_proj_hgrn", "w_up")
SMALL = ("ln_mix_g", "s5_a_re", "s5_a_im", "s5_log_dt", "s5_b_re", "s5_b_im", "s5_c_re", "s5_c_im", "s5_d",
         "s5_b_glu", "hgrn_lb_logits", "hgrn_norm_g", "ln_ffn_g", "conv_b", "ln_final_g")
WEIGHTS = ("ln_mix_g", "w_in", "s5_a_re", "s5_a_im", "s5_log_dt", "s5_b_re", "s5_b_im", "s5_c_re", "s5_c_im", "s5_d",
           "s5_w_glu", "s5_b_glu", "w_proj_s5", "hgrn_lb_logits", "hgrn_norm_g", "w_proj_hgrn", "w_out", "ln_ffn_g",
           "w_up", "conv_w", "conv_b", "w_down", "ln_final_g")


def kernel(x, ln_mix_g, w_in, s5_a_re, s5_a_im, s5_log_dt, s5_b_re, s5_b_im, s5_c_re, s5_c_im, s5_d, s5_w_glu, s5_b_glu, w_proj_s5, hgrn_lb_logits, hgrn_norm_g, w_proj_hgrn, w_out, ln_ffn_g, w_up, conv_w, conv_b, w_down, ln_final_g, loss_target, m_ln_mix_g, m_w_in, m_s5_a_re, m_s5_a_im, m_s5_log_dt, m_s5_b_re, m_s5_b_im, m_s5_c_re, m_s5_c_im, m_s5_d, m_s5_w_glu, m_s5_b_glu, m_w_proj_s5, m_hgrn_lb_logits, m_hgrn_norm_g, m_w_proj_hgrn, m_w_out, m_ln_ffn_g, m_w_up, m_conv_w, m_conv_b, m_w_down, m_ln_final_g, v_ln_mix_g, v_w_in, v_s5_a_re, v_s5_a_im, v_s5_log_dt, v_s5_b_re, v_s5_b_im, v_s5_c_re, v_s5_c_im, v_s5_d, v_s5_w_glu, v_s5_b_glu, v_w_proj_s5, v_hgrn_lb_logits, v_hgrn_norm_g, v_w_proj_hgrn, v_w_out, v_ln_ffn_g, v_w_up, v_conv_w, v_conv_b, v_w_down, v_ln_final_g):
    given = dict(locals())
    w = {n: given[n] for n in WEIGHTS}
    mom = {n: given["m_" + n] for n in WEIGHTS}
    var = {n: given["v_" + n] for n in WEIGHTS}

    first = _two_level_start([w_in[0].astype(BF16), conv_w[0]], "gather_first_start")
    zero = first["token"][0, 0]
    packed_small = SMALL[1:]
    pw, pm, pv = (_pack([d[n] for n in packed_small]) + zero for d in (w, mom, var))
    layouts = _s5_layouts(s5_b_re[0] + zero, s5_b_im[0], s5_c_re[0] + zero, s5_c_im[0], s5_d[0])
    gather_groups = (("s5_w_glu", "w_proj_s5", "w_proj_hgrn", "w_out"), ("w_up",), ("w_down",))
    shard16 = {n: w[n][0].astype(BF16) + zero.astype(BF16) for g in gather_groups for n in g}
    zones = {n: _landing_zone(s, False) for n, s in shard16.items()}
    pending, ready = {}, {}

    def weight(name, after):
        if "w_in" not in ready:
            local_work = [after, pw, pm, pv, layouts[0], layouts[1], *zones.values()]
            passed = _two_level_pass(first, local_work, "gather_first_pass")
            ready["w_in"], conv_w_all = _two_level_wait(passed, "gather_first_wait")
            ready["conv_w"] = conv_w_all.transpose(1, 0, 2).reshape(3, 2 * D_FF)
            token = ready["w_in"]
            for i, group in enumerate(gather_groups):
                handle = _exchange_start([shard16[n] for n in group], False, token, f"gather_start_{i}",
                                         lands=[zones[n] for n in group])
                token = handle["token"]
                for n in group:
                    pending[n] = (group, handle, f"gather_wait_{i}")
            ready["after_w_in"] = token
        if name not in ready:
            group, handle, wait_name = pending[name]
            for n, g in zip(group, _exchange_wait(handle, [after], wait_name)):
                ready[n] = g
        g = ready[name]
        return g if name not in BIG or name in COL_SHARDED else g.reshape(1, N_DEV * g.shape[1], g.shape[2])

    scatter_groups = (("w_down",), ("w_up",), ("w_out", "w_proj_hgrn", "w_proj_s5", "s5_w_glu"), ("w_in",))
    emitted, scatters = {}, []
    packed_names = SMALL[1:] + ("conv_w", "loss")

    def emit(name, grad):
        if name == "small":
            emitted[name] = ([grad[n].shape for n in packed_names],
                             _exchange_start([_pack([grad[n] for n in packed_names])], False, None, "small_start"))
            return emitted[name][1]["token"]
        emitted[name] = grad if name in COL_SHARDED else grad.reshape(N_DEV, -1, grad.shape[2])
        group = scatter_groups[len(scatters)]
        if not all(n in emitted for n in group):
            return None
        handle = _exchange_start([emitted[n] for n in group], True, None, f"scatter_start_{len(scatters)}")
        scatters.append((group, handle))
        return handle["token"]

    small = dict(ln_mix_g=ln_mix_g, s5_a_re=s5_a_re[0], s5_a_im=s5_a_im[0], s5_log_dt=s5_log_dt, s5_layouts=layouts,
                 s5_b_glu=s5_b_glu, hgrn_lb_logits=hgrn_lb_logits, hgrn_norm_g=hgrn_norm_g, ln_ffn_g=ln_ffn_g,
                 conv_b=conv_b, ln_final_g=ln_final_g.reshape(1, D_MODEL))
    grad_x, g_ln_mix = _local_step(x[0], loss_target[0], weight, emit, small, after=first["token"])

    shapes, handle = emitted["small"]
    total = _sum_slots(_exchange_wait(handle, [grad_x], "small_wait")[0], "sum_small")
    summed = dict(zip(packed_names, _unpack(total, shapes)))
    mix_all = _all_gather([g_ln_mix.reshape(-1, LANE)], "gather_ln_mix")[0]
    summed["ln_mix_g"] = _sum_slots(mix_all, "sum_ln_mix").reshape(1, D_MODEL)

    grads, delta, new_m, new_v = {}, {}, {}, {}
    afters = [grad_x, total]
    for i, (group, handle) in enumerate(scatters):
        for n, r in zip(group, _exchange_wait(handle, afters, f"scatter_wait_{i}")):
            g, d, m2, v2 = _sum_adam(r, w[n][0], mom[n][0], var[n][0], "adam_" + n)
            grads[n], delta[n], new_m[n], new_v[n] = g[None], d[None], m2[None], v2[None]
        if i == len(scatters) - 2:
            afters = [delta[n] for g2, _ in scatters[:-1] for n in g2]

    d_s, m_s, v_s = _adam_rows(_pack([summed[n] for n in packed_small]), pw, pm, pv, "adam_small")
    wshapes = [w[n].shape for n in packed_small]
    for n, d, m2, v2 in zip(packed_small, _unpack(d_s, wshapes), _unpack(m_s, wshapes), _unpack(v_s, wshapes)):
        grads[n], delta[n], new_m[n], new_v[n] = summed[n].reshape(w[n].shape), d, m2, v2
    grads["ln_mix_g"] = summed["ln_mix_g"]
    delta["ln_mix_g"], new_m["ln_mix_g"], new_v["ln_mix_g"] = _adam_rows(summed["ln_mix_g"], ln_mix_g, m_ln_mix_g,
                                                                         v_ln_mix_g, "adam_ln_mix")
    me = 4 * lax.axis_index("x") + 2 * lax.axis_index("y") + lax.axis_index("c")
    ncol = conv_w.shape[2]
    g_cw = lax.dynamic_slice_in_dim(summed["conv_w"], me * ncol, ncol, axis=1)
    d_cw, m_cw, v_cw = _adam_rows(g_cw, conv_w[0], m_conv_w[0], v_conv_w[0], "adam_conv_w")
    grads["conv_w"], delta["conv_w"], new_m["conv_w"], new_v["conv_w"] = g_cw[None], d_cw[None], m_cw[None], v_cw[None]

    return (summed["loss"].reshape(()), grad_x[None], *[grads[n] for n in WEIGHTS], *[delta[n] for n in WEIGHTS],
            *[new_m[n] for n in WEIGHTS], *[new_v[n] for n in WEIGHTS])
```

```python
import math

import jax
import jax.numpy as jnp
from jax import lax
from jax.experimental import pallas as pl
from jax.experimental.pallas import tpu as pltpu

F32 = jnp.float32
BF16 = jnp.bfloat16

N_DEV = 8
D_MODEL = 2048
S5_WIDTH = 1024
S5_GROUP = 16
S5_GROUPS = 64
S5_STATE = 64
S5_MAX_RE = -1e-4
S5_SUPER = 8
S5_LANES = S5_SUPER * S5_STATE
HGRN_WIDTH = 1024
HGRN_HEADS = 8
HGRN_DH = 128
HGRN_CHUNK = 64
HGRN_SUBS = 4
D_FF = 5632
RMS_EPS = 1e-6
ADAM_LR = 0.001
ADAM_B1 = 0.9
ADAM_B2 = 0.999
ADAM_EPS = 1e-08
ADAM_WD = 0.01
ADAM_STEP = 10

LANE = 128
SUBLANE = 8
VMEM_LIMIT = 48 * 1024 * 1024
MESH = pl.DeviceIdType.MESH
GELU_C = math.sqrt(2.0 / math.pi)
GELU_A = 0.044715


def _params(sem=None):
    return pltpu.CompilerParams(dimension_semantics=sem, vmem_limit_bytes=VMEM_LIMIT)


def _pick(n, cap, unit=LANE):
    best = None
    for t in range(unit, min(n, cap) + 1, unit):
        if n % t == 0:
            best = t
    return best if best is not None else n


def _ordered(body, in_specs, args, after):
    if after is None:
        return body, list(in_specs), list(args)
    n_in = len(args)

    def ordered_body(*refs):
        return body(*refs[:n_in], *refs[n_in + 1:])

    return ordered_body, [*in_specs, pl.BlockSpec(memory_space=pl.ANY)], [*args, after]


def _sigmoid(x):
    return 0.5 * jnp.tanh(0.5 * x) + 0.5


def _silu_and_grad(x):
    s = _sigmoid(x)
    return x * s, s * (1.0 + x * (1.0 - s))


def _gelu_and_grad(y):
    inner = GELU_C * (y + GELU_A * y * y * y)
    th = jnp.tanh(inner)
    val = 0.5 * y * (1.0 + th)
    grad = 0.5 * (1.0 + th) + 0.5 * y * (1.0 - th * th) * GELU_C * (1.0 + 3.0 * GELU_A * y * y)
    return val, grad


def _dot(a, b):
    return jnp.dot(a, b, preferred_element_type=F32)


def _dot_nt(a, b):
    return lax.dot_general(a, b, (((1,), (1,)), ((), ())), preferred_element_type=F32)


def _dot_tn(a, b):
    return lax.dot_general(a, b, (((0,), (0,)), ((), ())), preferred_element_type=F32)


def _blocks_per_step(nb, ns, tn, cap=2048):
    if tn != ns:
        return 1
    best = 1
    for b in range(1, nb + 1):
        if nb % b == 0 and b * ns <= cap:
            best = b
    return best


NN_TILE_BYTES = 42 * 1024 * 1024


def _mm_nn(a, w, name, res=None, out_dtype=F32, after=None):
    m, kdim = a.shape
    nb, _, ns = w.shape
    tk, tn = _pick(kdim, D_FF), _pick(ns, 1536)
    npb, nk = ns // tn, kdim // tk
    bps = _blocks_per_step(nb, ns, tn)
    assert bps == 1 or nk == 1

    def buffers(rows):
        return 2 * (rows * tk * 2 + bps * tk * tn * 2 + rows * bps * tn * 4 * (2 if res is not None else 1))

    tm = next((r for r in (_pick(m, 1024), _pick(m, 512)) if buffers(r) <= NN_TILE_BYTES), _pick(m, 256))

    def body(*refs):
        a_ref, w_ref = refs[0], refs[1]
        r_ref = refs[2] if res is not None else None
        o_ref = refs[3] if res is not None else refs[2]

        def finish(r, cols):
            if res is not None:
                r = r + r_ref[:, cols]
            o_ref[:, cols] = r.astype(out_dtype)

        if nk == 1:
            for b in range(bps):
                finish(_dot(a_ref[...], w_ref[b]), slice(b * tn, (b + 1) * tn))
            return
        acc = refs[-1]
        k = pl.program_id(2)

        @pl.when(k == 0)
        def _():
            acc[...] = jnp.zeros_like(acc)

        acc[...] += _dot(a_ref[...], w_ref[0])

        @pl.when(k == nk - 1)
        def _():
            finish(acc[...], slice(0, tn))

    in_specs = [pl.BlockSpec((tm, tk), lambda j, i, k: (i, k)),
                pl.BlockSpec((bps, tk, tn), lambda j, i, k: (j // npb, k, j % npb))]
    args = [a, w]
    if res is not None:
        in_specs.append(pl.BlockSpec((tm, bps * tn), lambda j, i, k: (i, j)))
        args.append(res)
    body, in_specs, args = _ordered(body, in_specs, args, after)
    return pl.pallas_call(
        body, name=name, grid=(nb * npb // bps, m // tm, nk),
        in_specs=in_specs, out_specs=pl.BlockSpec((tm, bps * tn), lambda j, i, k: (i, j)),
        out_shape=jax.ShapeDtypeStruct((m, nb * ns), out_dtype),
        scratch_shapes=[pltpu.VMEM((tm, tn), F32)] if nk > 1 else [],
        compiler_params=_params(("parallel", "parallel", "arbitrary")),
    )(*args)


NT_STEP_COLS = 2816


def _mm_nt(a, w, name, out_dtype=F32, after=None):
    m, _ = a.shape
    nb, kdim, ns = w.shape
    tm, tko, tn = _pick(m, 1024), _pick(kdim, 1024), _pick(ns, 2048)
    npb = ns // tn
    bps = _blocks_per_step(nb, ns, tn, cap=NT_STEP_COLS)
    nred = nb * npb // bps

    def body(a_ref, w_ref, o_ref, *scratch):
        total = _dot_nt(a_ref[:, 0:tn], w_ref[0])
        for b in range(1, bps):
            total = total + _dot_nt(a_ref[:, b * tn:(b + 1) * tn], w_ref[b])
        if nred == 1:
            o_ref[...] = total.astype(out_dtype)
            return
        acc = scratch[0]
        n = pl.program_id(2)

        @pl.when(n == 0)
        def _():
            acc[...] = jnp.zeros_like(acc)

        acc[...] += total

        @pl.when(n == nred - 1)
        def _():
            o_ref[...] = acc[...].astype(out_dtype)

    in_specs = [pl.BlockSpec((tm, bps * tn), lambda i, j, n: (i, n)),
                pl.BlockSpec((bps, tko, tn), lambda i, j, n: (n // npb, j, n % npb))]
    body, in_specs, args = _ordered(body, in_specs, [a, w], after)
    return pl.pallas_call(
        body, name=name, grid=(m // tm, kdim // tko, nred),
        in_specs=in_specs,
        out_specs=pl.BlockSpec((tm, tko), lambda i, j, n: (i, j)),
        out_shape=jax.ShapeDtypeStruct((m, kdim), out_dtype),
        scratch_shapes=[pltpu.VMEM((tm, tko), F32)] if nred > 1 else [],
        compiler_params=_params(("parallel", "parallel", "arbitrary")),
    )(*args)


def _mm_tn(a, d, nb, name, out_dtype=BF16, after=None):
    m, kdim = a.shape
    ns = d.shape[1] // nb
    tm, tko, tn = _pick(m, 4096), _pick(kdim, 512), _pick(ns, 1536)
    npb, nm = ns // tn, m // tm
    bps = _blocks_per_step(nb, ns, tn, cap=1536)
    assert bps == 1 or nm == 1

    def body(a_ref, d_ref, o_ref, *scratch):
        if nm == 1:
            for b in range(bps):
                o_ref[b] = _dot_tn(a_ref[...], d_ref[:, b * tn:(b + 1) * tn]).astype(out_dtype)
            return
        acc = scratch[0]
        r = pl.program_id(2)

        @pl.when(r == 0)
        def _():
            acc[...] = jnp.zeros_like(acc)

        acc[...] += _dot_tn(a_ref[...], d_ref[...])

        @pl.when(r == nm - 1)
        def _():
            o_ref[0] = acc[...].astype(out_dtype)

    in_specs = [pl.BlockSpec((tm, tko), lambda j, i, r: (r, i)), pl.BlockSpec((tm, bps * tn), lambda j, i, r: (r, j))]
    body, in_specs, args = _ordered(body, in_specs, [a, d], after)
    return pl.pallas_call(
        body, name=name, grid=(nb * npb // bps, kdim // tko, nm),
        in_specs=in_specs,
        out_specs=pl.BlockSpec((bps, tko, tn), lambda j, i, r: (j // npb, i, j % npb)),
        out_shape=jax.ShapeDtypeStruct((nb, kdim, ns), out_dtype),
        scratch_shapes=[pltpu.VMEM((tko, tn), F32)] if nm > 1 else [],
        compiler_params=_params(("parallel", "parallel", "arbitrary")),
    )(*args)


def _rms_fwd(x, g, name, after=None):
    t, d = x.shape
    tr = _pick(t, 512, SUBLANE)

    def body(x_ref, g_ref, h_ref):
        xv = x_ref[...]
        r = lax.rsqrt(jnp.mean(xv * xv, axis=-1, keepdims=True) + RMS_EPS)
        h_ref[...] = (xv * r * g_ref[...]).astype(BF16)

    in_specs = [pl.BlockSpec((tr, d), lambda i: (i, 0)), pl.BlockSpec((1, d), lambda i: (0, 0))]
    body, in_specs, args = _ordered(body, in_specs, [x, g], after)
    return pl.pallas_call(
        body, name=name, grid=(t // tr,),
        in_specs=in_specs,
        out_specs=pl.BlockSpec((tr, d), lambda i: (i, 0)),
        out_shape=jax.ShapeDtypeStruct((t, d), BF16),
        compiler_params=_params(("parallel",)),
    )(*args)


def _rms_bwd(x, g, dh, add, name, want_bf16, after=None):
    t, d = x.shape
    tr = _pick(t, 256, SUBLANE)

    def body(x_ref, g_ref, dh_ref, add_ref, *outs):
        if want_bf16:
            dx_ref, dxb_ref, dg_ref = outs
        else:
            dx_ref, dg_ref = outs
        i = pl.program_id(0)

        @pl.when(i == 0)
        def _():
            dg_ref[...] = jnp.zeros_like(dg_ref)

        xv, dhv = x_ref[...], dh_ref[...]
        r = lax.rsqrt(jnp.mean(xv * xv, axis=-1, keepdims=True) + RMS_EPS)
        xh = xv * r
        dg_ref[...] += jnp.sum(dhv * xh, axis=0, keepdims=True)
        dxh = dhv * g_ref[...]
        dx = add_ref[...] + r * (dxh - xh * jnp.mean(dxh * xh, axis=-1, keepdims=True))
        dx_ref[...] = dx
        if want_bf16:
            dxb_ref[...] = dx.astype(BF16)

    row = pl.BlockSpec((tr, d), lambda i: (i, 0))
    vec = pl.BlockSpec((1, d), lambda i: (0, 0))
    out_specs = [row] + ([row] if want_bf16 else []) + [vec]
    out_shape = ([jax.ShapeDtypeStruct((t, d), F32)] + ([jax.ShapeDtypeStruct((t, d), BF16)] if want_bf16 else [])
                 + [jax.ShapeDtypeStruct((1, d), F32)])
    body, in_specs, args = _ordered(body, [row, vec, row, row], [x, g, dh, add], after)
    return pl.pallas_call(
        body, name=name, grid=(t // tr,),
        in_specs=in_specs, out_specs=out_specs, out_shape=out_shape,
        compiler_params=_params(("arbitrary",)),
    )(*args)


def _loss_head(x2, g, target, name="loss_head"):
    t, d = x2.shape
    tr = _pick(t, 512, SUBLANE)

    def body(x_ref, g_ref, t_ref, dx_ref, dxb_ref, dg_ref, loss_ref):
        i = pl.program_id(0)

        @pl.when(i == 0)
        def _():
            dg_ref[...] = jnp.zeros_like(dg_ref)
            loss_ref[...] = jnp.zeros_like(loss_ref)

        xv = x_ref[...]
        gv = g_ref[...]
        r = lax.rsqrt(jnp.mean(xv * xv, axis=-1, keepdims=True) + RMS_EPS)
        xh = xv * r
        err = xh * gv - t_ref[...]
        part = 0.5 * jnp.sum(jnp.mean(err * err, axis=-1, keepdims=True), axis=0, keepdims=True)
        loss_ref[...] += jnp.broadcast_to(part, loss_ref.shape)
        dy = err * (1.0 / d)
        dg_ref[...] += jnp.sum(dy * xh, axis=0, keepdims=True)
        dxh = dy * gv
        dx = r * (dxh - xh * jnp.mean(dxh * xh, axis=-1, keepdims=True))
        dx_ref[...] = dx
        dxb_ref[...] = dx.astype(BF16)

    row = pl.BlockSpec((tr, d), lambda i: (i, 0))
    vec = pl.BlockSpec((1, d), lambda i: (0, 0))
    return pl.pallas_call(
        body, name=name, grid=(t // tr,),
        in_specs=[row, vec, row],
        out_specs=[row, row, vec, pl.BlockSpec((1, LANE), lambda i: (0, 0))],
        out_shape=[jax.ShapeDtypeStruct((t, d), F32), jax.ShapeDtypeStruct((t, d), BF16),
                   jax.ShapeDtypeStruct((1, d), F32), jax.ShapeDtypeStruct((1, LANE), F32)],
        compiler_params=_params(("arbitrary",)),
    )(x2, g, target)


def _s5_discretize(a_re, a_im, ldt):
    lam_re = jnp.minimum(a_re, S5_MAX_RE)
    lam_im = a_im
    dt = jnp.exp(ldt)
    mag = jnp.exp(lam_re * dt)
    abar_re = mag * jnp.cos(lam_im * dt)
    abar_im = mag * jnp.sin(lam_im * dt)
    den = lam_re * lam_re + lam_im * lam_im
    nr = abar_re - 1.0
    ni = abar_im
    coef_re = (nr * lam_re + ni * lam_im) / den
    coef_im = (ni * lam_re - nr * lam_im) / den
    return abar_re, abar_im, coef_re, coef_im


def _s5_param_fwd(a_re, a_im, ldt):
    def body(ar_ref, ai_ref, l_ref, o0, o1, o2, o3):
        outs = _s5_discretize(ar_ref[...], ai_ref[...], l_ref[...])
        for o, v in zip((o0, o1, o2, o3), outs):
            o[...] = v

    sh = jax.ShapeDtypeStruct(a_re.shape, F32)
    return pl.pallas_call(body, name="s5_param_fwd", out_shape=[sh, sh, sh, sh], compiler_params=_params())(a_re, a_im, ldt)


def _s5_param_bwd(a_re, a_im, ldt, cts):
    def body(ar_ref, ai_ref, l_ref, c0, c1, c2, c3, g0, g1, g2):
        _, vjp = jax.vjp(_s5_discretize, ar_ref[...], ai_ref[...], l_ref[...])
        ga, gb, gl = vjp((c0[...], c1[...], c2[...], c3[...]))
        g0[...] = ga
        g1[...] = gb
        g2[...] = gl

    sh = jax.ShapeDtypeStruct(a_re.shape, F32)
    return pl.pallas_call(body, name="s5_param_bwd", out_shape=[sh, sh, jax.ShapeDtypeStruct(ldt.shape, F32)],
                          compiler_params=_params())(a_re, a_im, ldt, *cts)


def _cmul(ar, ai, br, bi):
    return ar * br - ai * bi, ar * bi + ai * br


S5_TC = 128
S5_TILE = S5_SUPER * SUBLANE
S5_HALF = S5_TILE // 2


def _s5_to_tile(re, im):
    f = lambda a: a.reshape(S5_SUPER, S5_LANES // LANE, LANE).transpose(1, 0, 2).reshape(S5_HALF, LANE)
    return jnp.concatenate([f(re), f(im)], axis=0)


def _s5_from_tile(tile):
    f = lambda a: a.reshape(S5_LANES // LANE, S5_SUPER, LANE).transpose(1, 0, 2).reshape(S5_GROUPS, S5_STATE)
    return f(tile[0:S5_HALF]), f(tile[S5_HALF:])


RE = slice(0, S5_HALF)
IM = slice(S5_HALF, S5_TILE)


def _s5_scatter_rows(buf, rows, first_tile=0):
    tc = rows[0].shape[0]
    for j in range(SUBLANE):
        stacked = jnp.stack([r[:, j * LANE:(j + 1) * LANE] for r in rows], axis=0)
        buf[first_tile:first_tile + tc, j * SUBLANE:(j + 1) * SUBLANE, :] = jnp.swapaxes(stacked, 0, 1)


def _s5_gather_rows(buf, tc, first_tile=0):
    per_j = [jnp.swapaxes(buf[first_tile:first_tile + tc, j * SUBLANE:(j + 1) * SUBLANE, :], 0, 1)
             for j in range(SUBLANE)]
    return [jnp.concatenate([per_j[j][k] for j in range(SUBLANE)], axis=1) for k in range(S5_SUPER)]


def _s5_fwd(proj, bsg, ccat, d_row, abar_t, coef_t):
    t = proj.shape[0]
    tc = min(t, S5_TC)
    n_chunks = t // tc

    def body(u_ref, b_ref, c_ref, d_ref, a_ref, cf_ref, y_ref, sb_ref, x, car):
        @pl.when(pl.program_id(0) == 0)
        def _():
            car[...] = jnp.zeros_like(car)

        sb_ref[...] = car[...]
        u = u_ref[...]
        _s5_scatter_rows(x, [_dot(u[:, k * LANE:(k + 1) * LANE].astype(BF16), b_ref[k]) for k in range(S5_SUPER)])
        ar, ai = a_ref[RE, :], a_ref[IM, :]
        cr, ci = cf_ref[RE, :], cf_ref[IM, :]

        def step(i, carry):
            sr, si = carry
            xr, xi = _cmul(cr, ci, x[i, RE, :], x[i, IM, :])
            sr, si = ar * sr - ai * si + xr, ar * si + ai * sr + xi
            x[i, RE, :] = sr
            x[i, IM, :] = si
            return sr, si

        sr, si = lax.fori_loop(0, tc, step, (car[RE, :], car[IM, :]), unroll=8)
        car[RE, :] = sr
        car[IM, :] = si
        for k, s_k in enumerate(_s5_gather_rows(x, tc)):
            cols = slice(k * LANE, (k + 1) * LANE)
            y_ref[:, cols] = _dot(s_k.astype(BF16), c_ref[k]) + d_ref[:, cols] * u[:, cols]

    full = lambda shape: pl.BlockSpec(shape, lambda c: (0,) * len(shape))
    return pl.pallas_call(
        body, name="s5_fwd", grid=(n_chunks,),
        in_specs=[pl.BlockSpec((tc, S5_WIDTH), lambda c: (c, 0)), full(bsg.shape), full(ccat.shape), full(d_row.shape),
                  full(abar_t.shape), full(coef_t.shape)],
        out_specs=[pl.BlockSpec((tc, S5_WIDTH), lambda c: (c, 0)), pl.BlockSpec((None, S5_TILE, LANE), lambda c: (c, 0, 0))],
        out_shape=[jax.ShapeDtypeStruct((t, S5_WIDTH), F32), jax.ShapeDtypeStruct((n_chunks, S5_TILE, LANE), F32)],
        scratch_shapes=[pltpu.VMEM((tc, S5_TILE, LANE), F32), pltpu.VMEM((S5_TILE, LANE), F32)],
        compiler_params=_params(("arbitrary",)),
    )(proj, bsg, ccat, d_row, abar_t, coef_t)


def _s5_bwd(proj, dy, sb, bsg, ccat, d_row, abar_t, coef_t):
    t = proj.shape[0]
    tc = min(t, S5_TC)
    n_chunks = t // tc
    last = n_chunks - 1

    def body(u_ref, dy_ref, sb_ref, b_ref, c_ref, d_ref, a_ref, cf_ref,
             du_ref, gb_ref, gc_ref, gd_ref, ga_ref, gcf_ref, xb, xs, xg, gcar, acc):
        @pl.when(pl.program_id(0) == 0)
        def _():
            gcar[...] = jnp.zeros_like(gcar)
            acc[...] = jnp.zeros_like(acc)
            gb_ref[...] = jnp.zeros_like(gb_ref)
            gc_ref[...] = jnp.zeros_like(gc_ref)
            gd_ref[...] = jnp.zeros_like(gd_ref)

        u = u_ref[...]
        dyv = dy_ref[...]
        u16, dy16 = u.astype(BF16), dyv.astype(BF16)
        subs = [slice(k * LANE, (k + 1) * LANE) for k in range(S5_SUPER)]
        _s5_scatter_rows(xb, [_dot(u16[:, c], b_ref[k]) for k, c in enumerate(subs)])
        _s5_scatter_rows(xg, [_dot_nt(dy16[:, c], c_ref[k]) for k, c in enumerate(subs)])
        ar, ai = a_ref[RE, :], a_ref[IM, :]
        cr, ci = cf_ref[RE, :], cf_ref[IM, :]

        xs[0] = sb_ref[...]

        def fstep(i, carry):
            sr, si = carry
            xr, xi = _cmul(cr, ci, xb[i, RE, :], xb[i, IM, :])
            sr, si = ar * sr - ai * si + xr, ar * si + ai * sr + xi
            xs[i + 1, RE, :] = sr
            xs[i + 1, IM, :] = si
            return sr, si

        lax.fori_loop(0, tc, fstep, (sb_ref[RE, :], sb_ref[IM, :]), unroll=8)

        def rstep(n, carry):
            gr, gi, a0, a1, a2, a3 = carry
            i = tc - 1 - n
            xr = xg[i, RE, :] + ar * gr + ai * gi
            xi = xg[i, IM, :] + ar * gi - ai * gr
            pr, pi = xs[i, RE, :], xs[i, IM, :]
            br, bi = xb[i, RE, :], xb[i, IM, :]
            a0 = a0 + pr * xr + pi * xi
            a1 = a1 + pr * xi - pi * xr
            a2 = a2 + br * xr + bi * xi
            a3 = a3 + br * xi - bi * xr
            xg[i, RE, :] = cr * xr + ci * xi
            xg[i, IM, :] = cr * xi - ci * xr
            return xr, xi, a0, a1, a2, a3

        init = (gcar[RE, :], gcar[IM, :], acc[0], acc[1], acc[2], acc[3])
        gr, gi, a0, a1, a2, a3 = lax.fori_loop(0, tc, rstep, init, unroll=4)
        gcar[RE, :] = gr
        gcar[IM, :] = gi
        for idx, a in enumerate((a0, a1, a2, a3)):
            acc[idx] = a
        ga_ref[RE, :] = a0
        ga_ref[IM, :] = a1
        gcf_ref[RE, :] = a2
        gcf_ref[IM, :] = a3

        g_rows = _s5_gather_rows(xg, tc)
        s_rows = _s5_gather_rows(xs, tc, first_tile=1)
        for k in range(S5_SUPER):
            cols = subs[k]
            g16 = g_rows[k].astype(BF16)
            s16 = s_rows[k].astype(BF16)
            gb_ref[k] += _dot_tn(u16[:, cols], g16)
            gc_ref[k] += _dot_tn(s16, dy16[:, cols])
            du_ref[:, cols] = (_dot_nt(g16, b_ref[k]) + d_ref[:, cols] * dyv[:, cols]).astype(BF16)
        gd_ref[...] += jnp.sum(dyv * u, axis=0, keepdims=True)

    full = lambda shape: pl.BlockSpec(shape, lambda c: (0,) * len(shape))
    rows = pl.BlockSpec((tc, S5_WIDTH), lambda c: (last - c, 0))
    tile = (S5_TILE, LANE)
    return pl.pallas_call(
        body, name="s5_bwd", grid=(n_chunks,),
        in_specs=[rows, rows, pl.BlockSpec((None, S5_TILE, LANE), lambda c: (last - c, 0, 0)),
                  full(bsg.shape), full(ccat.shape), full(d_row.shape), full(abar_t.shape), full(coef_t.shape)],
        out_specs=[rows, full(bsg.shape), full(ccat.shape), full(d_row.shape), full(tile), full(tile)],
        out_shape=[jax.ShapeDtypeStruct((t, S5_WIDTH), BF16), jax.ShapeDtypeStruct(bsg.shape, F32),
                   jax.ShapeDtypeStruct(ccat.shape, F32), jax.ShapeDtypeStruct(d_row.shape, F32),
                   jax.ShapeDtypeStruct(tile, F32), jax.ShapeDtypeStruct(tile, F32)],
        scratch_shapes=[pltpu.VMEM((tc, S5_TILE, LANE), F32), pltpu.VMEM((tc + 1, S5_TILE, LANE), F32),
                        pltpu.VMEM((tc, S5_TILE, LANE), F32), pltpu.VMEM(tile, F32),
                        pltpu.VMEM((4, S5_HALF, LANE), F32)],
        compiler_params=_params(("arbitrary",)),
    )(proj, dy, sb, bsg, ccat, d_row, abar_t, coef_t)


def _gelu_fwd(y, name="s5_gelu"):
    t, w = y.shape
    tr = _pick(t, 512, SUBLANE)

    def body(y_ref, z_ref):
        z_ref[...] = _gelu_and_grad(y_ref[...])[0].astype(BF16)

    row = pl.BlockSpec((tr, w), lambda i: (i, 0))
    return pl.pallas_call(body, name=name, grid=(t // tr,), in_specs=[row], out_specs=row,
                          out_shape=jax.ShapeDtypeStruct((t, w), BF16), compiler_params=_params(("parallel",)))(y)


def _glu_fwd(y, gl, b, name="s5_glu"):
    t, w = y.shape
    tr = _pick(t, 512, SUBLANE)

    def body(y_ref, gl_ref, b_ref, z2_ref):
        z = _gelu_and_grad(y_ref[...])[0]
        z2_ref[...] = (z * _sigmoid(gl_ref[...] + b_ref[...])).astype(BF16)

    row = pl.BlockSpec((tr, w), lambda i: (i, 0))
    return pl.pallas_call(body, name=name, grid=(t // tr,),
                          in_specs=[row, row, pl.BlockSpec((1, w), lambda i: (0, 0))], out_specs=row,
                          out_shape=jax.ShapeDtypeStruct((t, w), BF16), compiler_params=_params(("parallel",)))(y, gl, b)


def _glu_bwd(y, gl, b, dz2, name="s5_glu_bwd", after=None):
    t, w = y.shape
    tr = _pick(t, 512, SUBLANE)

    def body(y_ref, gl_ref, b_ref, dz2_ref, dgl_ref, dza_ref, db_ref):
        @pl.when(pl.program_id(0) == 0)
        def _():
            db_ref[...] = jnp.zeros_like(db_ref)

        z = _gelu_and_grad(y_ref[...])[0]
        s = _sigmoid(gl_ref[...] + b_ref[...])
        dz2v = dz2_ref[...]
        dgl = dz2v * z * s * (1.0 - s)
        dgl_ref[...] = dgl.astype(BF16)
        dza_ref[...] = dz2v * s
        db_ref[...] += jnp.sum(dgl, axis=0, keepdims=True)

    row = pl.BlockSpec((tr, w), lambda i: (i, 0))
    vec = pl.BlockSpec((1, w), lambda i: (0, 0))
    body, in_specs, args = _ordered(body, [row, row, vec, row], [y, gl, b, dz2], after)
    return pl.pallas_call(body, name=name, grid=(t // tr,), in_specs=in_specs, out_specs=[row, row, vec],
                          out_shape=[jax.ShapeDtypeStruct((t, w), BF16), jax.ShapeDtypeStruct((t, w), F32),
                                     jax.ShapeDtypeStruct((1, w), F32)],
                          compiler_params=_params(("arbitrary",)))(*args)


def _gelu_bwd(y, dza, dzb, name="s5_gelu_bwd", after=None):
    t, w = y.shape
    tr = _pick(t, 512, SUBLANE)

    def body(y_ref, a_ref, b_ref, dy_ref):
        dy_ref[...] = (a_ref[...] + b_ref[...]) * _gelu_and_grad(y_ref[...])[1]

    row = pl.BlockSpec((tr, w), lambda i: (i, 0))
    body, in_specs, args = _ordered(body, [row, row, row], [y, dza, dzb], after)
    return pl.pallas_call(body, name=name, grid=(t // tr,), in_specs=in_specs, out_specs=row,
                          out_shape=jax.ShapeDtypeStruct((t, w), F32), compiler_params=_params(("parallel",)))(*args)


def _tri_dot(tri16, x):
    hi = x.astype(BF16)
    lo = (x - hi.astype(F32)).astype(BF16)
    return _dot(tri16, hi) + _dot(tri16, lo)


def _hgrn_pre(q_in, z, lg):
    lb = _sigmoid(lg[0:1, :] - lg[1:2, :])
    qs, dqs = _silu_and_grad(q_in)
    sz = _sigmoid(z)
    f = lb + (1.0 - lb) * sz
    k = (1.0 - lb) * (1.0 - sz)
    c = HGRN_CHUNK
    r = lax.broadcasted_iota(jnp.int32, (c, c), 0)
    s = lax.broadcasted_iota(jnp.int32, (c, c), 1)
    causal = r >= s
    b = _tri_dot(jnp.where(causal, 1.0, 0.0).astype(BF16), jnp.log(f))
    b_end = b[c - 1:c, :]
    b_mid = b[c // 2 - 1:c // 2, :]
    e_q, e_k, e_0, e_c = jnp.exp(b - b_mid), jnp.exp(b_mid - b), jnp.exp(b), jnp.exp(b_end - b)
    return dict(lb=lb, qs=qs, dqs=dqs, sz=sz, f=f, k=k, causal=causal, b_end=b_end,
                e_q=e_q, e_k=e_k, e_0=e_0, e_c=e_c,
                qt=qs * e_q, kt=k * e_k, q0=qs * e_0, kc=k * e_c)


def _hgrn_fwd(proj, logits, ng):
    t = proj.shape[0]
    c, dh = HGRN_CHUNK, HGRN_DH
    n_chunks = t // c
    subs = HGRN_SUBS if n_chunks % HGRN_SUBS == 0 else 1

    def head(h, sub, q_ref, z_ref, v_ref, g_ref, lg_ref, ng_ref, o_ref, oh_ref, s0_ref, st):
        sl = slice(h * dh, (h + 1) * dh)
        rs = slice(sub * c, (sub + 1) * c)
        s0 = st[h]
        s0_ref[h, sub] = s0
        p = _hgrn_pre(q_ref[rs, sl], z_ref[rs, sl], lg_ref[:, sl])
        v16 = v_ref[rs, sl].astype(BF16)
        a = jnp.where(p["causal"], _dot_nt(p["qt"].astype(BF16), p["kt"].astype(BF16)), 0.0)
        o = _dot_nt(p["q0"].astype(BF16), s0.astype(BF16)) + _dot(a.astype(BF16), v16)
        st[h] = jnp.exp(p["b_end"]) * s0 + _dot_tn(v16, p["kc"].astype(BF16))
        o_ref[rs, sl] = o
        rn = lax.rsqrt(jnp.mean(o * o, axis=-1, keepdims=True) + RMS_EPS)
        oh_ref[rs, sl] = (o * rn * ng_ref[:, sl] * _silu_and_grad(g_ref[rs, sl])[0]).astype(BF16)

    def body(*refs):
        st = refs[-1]

        @pl.when(pl.program_id(0) == 0)
        def _():
            st[...] = jnp.zeros_like(st)

        for sub in range(subs):
            for h in range(HGRN_HEADS):
                head(h, sub, *refs)

    def wide(off):
        return pl.BlockSpec((subs * c, HGRN_WIDTH), lambda i: (i, off))

    return pl.pallas_call(
        body, name="hgrn_fwd", grid=(n_chunks // subs,),
        in_specs=[wide(1), wide(2), wide(3), wide(4),
                  pl.BlockSpec((2, HGRN_WIDTH), lambda i: (0, 0)), pl.BlockSpec((1, HGRN_WIDTH), lambda i: (0, 0))],
        out_specs=[wide(0), wide(0), pl.BlockSpec((HGRN_HEADS, subs, dh, dh), lambda i: (0, i, 0, 0))],
        out_shape=[jax.ShapeDtypeStruct((t, HGRN_WIDTH), F32), jax.ShapeDtypeStruct((t, HGRN_WIDTH), BF16),
                   jax.ShapeDtypeStruct((HGRN_HEADS, n_chunks, dh, dh), F32)],
        scratch_shapes=[pltpu.VMEM((HGRN_HEADS, dh, dh), F32)],
        compiler_params=_params(("arbitrary",)),
    )(proj, proj, proj, proj, logits, ng)


def _hgrn_bwd(proj, o_raw, s0s, doh, logits, ng):
    t = proj.shape[0]
    c, dh = HGRN_CHUNK, HGRN_DH
    n_chunks = t // c
    subs = HGRN_SUBS if n_chunks % HGRN_SUBS == 0 else 1
    last = n_chunks // subs - 1

    def head(h, sub, q_ref, z_ref, v_ref, g_ref, o_ref, s0_ref, doh_ref, lg_ref, ng_ref,
             dq_ref, dz_ref, dv_ref, dg_ref, dng_ref, dlb_ref, dst):
        sl = slice(h * dh, (h + 1) * dh)
        rs = slice(sub * c, (sub + 1) * c)
        p = _hgrn_pre(q_ref[rs, sl], z_ref[rs, sl], lg_ref[:, sl])
        v = v_ref[rs, sl]
        v16 = v.astype(BF16)
        s0 = s0_ref[h, sub]
        ds_end = dst[h]
        ds16 = ds_end.astype(BF16)
        ngv = ng_ref[:, sl]

        o = o_ref[rs, sl]
        dohv = doh_ref[rs, sl]
        sg, dsg = _silu_and_grad(g_ref[rs, sl])
        rn = lax.rsqrt(jnp.mean(o * o, axis=-1, keepdims=True) + RMS_EPS)
        oh = o * rn
        dg_ref[rs, sl] = (dohv * oh * ngv * dsg).astype(BF16)
        don = dohv * sg
        dng_ref[:, sl] += jnp.sum(don * oh, axis=0, keepdims=True)
        doh_n = don * ngv
        do = rn * (doh_n - oh * jnp.mean(doh_n * oh, axis=-1, keepdims=True))
        do16 = do.astype(BF16)

        qt16, kt16, q016, kc16 = (p[n].astype(BF16) for n in ("qt", "kt", "q0", "kc"))
        a = jnp.where(p["causal"], _dot_nt(qt16, kt16), 0.0)
        da = jnp.where(p["causal"], _dot_nt(do16, v16), 0.0)
        da16 = da.astype(BF16)
        dqt = _dot(da16, kt16)
        dq0 = _dot(do16, s0.astype(BF16))
        dkt = _dot_tn(da16, qt16)
        dkc = _dot(v16, ds16)
        dv_ref[rs, sl] = (_dot_tn(a.astype(BF16), do16) + _dot_nt(kc16, ds16)).astype(BF16)
        lam_end = jnp.exp(p["b_end"])
        dst[h] = lam_end * ds_end + _dot_tn(do16, q016)

        qt, kt, q0, kc = (a.astype(F32) for a in (qt16, kt16, q016, kc16))
        db = dqt * qt + dq0 * q0 - dkt * kt - dkc * kc
        db_end = (jnp.sum(dkc * kc, axis=0, keepdims=True)
                  + jnp.sum(ds_end * s0, axis=0, keepdims=True) * lam_end)
        rowi = lax.broadcasted_iota(jnp.int32, (c, dh), 0)
        db = db + jnp.where(rowi == c - 1, db_end, 0.0)
        r = lax.broadcasted_iota(jnp.int32, (c, c), 0)
        s = lax.broadcasted_iota(jnp.int32, (c, c), 1)
        dlf = _tri_dot(jnp.where(s >= r, 1.0, 0.0).astype(BF16), db)

        dqs = dqt * p["e_q"] + dq0 * p["e_0"]
        dq_ref[rs, sl] = (dqs * p["dqs"]).astype(BF16)
        dk = dkt * p["e_k"] + dkc * p["e_c"]
        sz, lb = p["sz"], p["lb"]
        common = dlf / p["f"] - dk
        dz_ref[rs, sl] = ((1.0 - lb) * sz * (1.0 - sz) * common).astype(BF16)
        dlb_ref[:, sl] += jnp.sum((1.0 - sz) * common, axis=0, keepdims=True)

    def body(*refs):
        dng_ref, dlb_ref, dst = refs[-3:]

        @pl.when(pl.program_id(0) == 0)
        def _():
            dst[...] = jnp.zeros_like(dst)
            dng_ref[...] = jnp.zeros_like(dng_ref)
            dlb_ref[...] = jnp.zeros_like(dlb_ref)

        for sub in reversed(range(subs)):
            for h in range(HGRN_HEADS):
                head(h, sub, *refs)

    def wide(off):
        return pl.BlockSpec((subs * c, HGRN_WIDTH), lambda i: (last - i, off))

    vec = pl.BlockSpec((1, HGRN_WIDTH), lambda i: (0, 0))
    act = jax.ShapeDtypeStruct((t, HGRN_WIDTH), BF16)
    vsh = jax.ShapeDtypeStruct((1, HGRN_WIDTH), F32)
    return pl.pallas_call(
        body, name="hgrn_bwd", grid=(n_chunks // subs,),
        in_specs=[wide(1), wide(2), wide(3), wide(4), wide(0),
                  pl.BlockSpec((HGRN_HEADS, subs, dh, dh), lambda i: (0, last - i, 0, 0)),
                  wide(0), pl.BlockSpec((2, HGRN_WIDTH), lambda i: (0, 0)), vec],
        out_specs=[wide(0), wide(0), wide(0), wide(0), vec, vec],
        out_shape=[act, act, act, act, vsh, vsh],
        scratch_shapes=[pltpu.VMEM((HGRN_HEADS, dh, dh), F32)],
        compiler_params=_params(("arbitrary",)),
    )(proj, proj, proj, proj, o_raw, s0s, doh, logits, ng)


def _lb_bwd(logits, dlb):
    def body(lg_ref, d_ref, o_ref):
        lg = lg_ref[...]
        lb = _sigmoid(lg[0:1, :] - lg[1:2, :])
        g = d_ref[...] * lb * (1.0 - lb)
        o_ref[0:1, :] = g
        o_ref[1:2, :] = -g

    return pl.pallas_call(body, name="hgrn_lb_bwd", out_shape=jax.ShapeDtypeStruct(logits.shape, F32),
                          compiler_params=_params())(logits, dlb)


MERGE_TC = 1024
GS_BLOCK = (S5_WIDTH + 4 * HGRN_WIDTH) // MERGE_TC
GH_BLOCK = GS_BLOCK + D_MODEL // MERGE_TC


def _merge_fwd(proj, ys, yh):
    t = proj.shape[0]
    tr = _pick(t, 512, SUBLANE)

    def body(gs_ref, gh_ref, ys_ref, yh_ref, m_ref):
        m_ref[...] = (_sigmoid(gs_ref[...]) * ys_ref[...] + _sigmoid(gh_ref[...]) * yh_ref[...]).astype(BF16)

    blk = pl.BlockSpec((tr, MERGE_TC), lambda i, j: (i, j))
    return pl.pallas_call(
        body, name="merge_fwd", grid=(t // tr, D_MODEL // MERGE_TC),
        in_specs=[pl.BlockSpec((tr, MERGE_TC), lambda i, j: (i, GS_BLOCK + j)),
                  pl.BlockSpec((tr, MERGE_TC), lambda i, j: (i, GH_BLOCK + j)), blk, blk],
        out_specs=blk, out_shape=jax.ShapeDtypeStruct((t, D_MODEL), BF16),
        compiler_params=_params(("parallel", "parallel")),
    )(proj, proj, ys, yh)


def _merge_bwd(proj, ys, yh, dm, after=None):
    t = proj.shape[0]
    tr = _pick(t, 512, SUBLANE)

    def body(gs_ref, gh_ref, ys_ref, yh_ref, dm_ref, dys_ref, dyh_ref, dgs_ref, dgh_ref):
        dmv = dm_ref[...]
        ss, sh = _sigmoid(gs_ref[...]), _sigmoid(gh_ref[...])
        dys_ref[...] = (dmv * ss).astype(BF16)
        dyh_ref[...] = (dmv * sh).astype(BF16)
        dgs_ref[...] = (dmv * ys_ref[...] * ss * (1.0 - ss)).astype(BF16)
        dgh_ref[...] = (dmv * yh_ref[...] * sh * (1.0 - sh)).astype(BF16)

    blk = pl.BlockSpec((tr, MERGE_TC), lambda i, j: (i, j))
    sh16 = jax.ShapeDtypeStruct((t, D_MODEL), BF16)
    in_specs = [pl.BlockSpec((tr, MERGE_TC), lambda i, j: (i, GS_BLOCK + j)),
                pl.BlockSpec((tr, MERGE_TC), lambda i, j: (i, GH_BLOCK + j)), blk, blk, blk]
    body, in_specs, args = _ordered(body, in_specs, [proj, proj, ys, yh, dm], after)
    return pl.pallas_call(
        body, name="merge_bwd", grid=(t // tr, D_MODEL // MERGE_TC),
        in_specs=in_specs,
        out_specs=[blk, blk, blk, blk], out_shape=[sh16, sh16, sh16, sh16],
        compiler_params=_params(("parallel", "parallel")),
    )(*args)


FFN_TC = 128
FFN_ROWS = 128
HALO = SUBLANE


def _pad_rows(dst, src_ref):
    t, c = src_ref.shape
    dst[0:HALO, :] = jnp.zeros((HALO, c), F32)
    dst[HALO:HALO + t, :] = src_ref[...]
    dst[HALO + t:HALO + t + HALO, :] = jnp.zeros((HALO, c), F32)


def _conv3(padded, w, b, r0, nrows):
    x0 = padded[HALO + r0:HALO + r0 + nrows, :]
    x1 = padded[HALO + r0 - 1:HALO + r0 - 1 + nrows, :]
    x2 = padded[HALO + r0 - 2:HALO + r0 - 2 + nrows, :]
    return b + w[0:1, :] * x2 + w[1:2, :] * x1 + w[2:3, :] * x0, (x0, x1, x2)


def _ffn_act_fwd(up, cw, cb):
    t = up.shape[0]
    rows = _pick(t, FFN_ROWS, SUBLANE)
    nvb = D_FF // FFN_TC

    def body(ug_ref, uv_ref, wg_ref, wv_ref, bg_ref, bv_ref, act_ref, pg, pv):
        wg, wv, bg, bv = wg_ref[...], wv_ref[...], bg_ref[...], bv_ref[...]
        _pad_rows(pg, ug_ref)
        _pad_rows(pv, uv_ref)
        for r0 in range(0, t, rows):
            cg, _ = _conv3(pg, wg, bg, r0, rows)
            cv, _ = _conv3(pv, wv, bv, r0, rows)
            act_ref[r0:r0 + rows, :] = (_silu_and_grad(cg)[0] * cv).astype(BF16)

    def colblk(nrow, off):
        return pl.BlockSpec((nrow, FFN_TC), lambda j: (0, off + j))

    return pl.pallas_call(
        body, name="ffn_act_fwd", grid=(nvb,),
        in_specs=[colblk(t, 0), colblk(t, nvb), colblk(3, 0), colblk(3, nvb), colblk(1, 0), colblk(1, nvb)],
        out_specs=colblk(t, 0), out_shape=jax.ShapeDtypeStruct((t, D_FF), BF16),
        scratch_shapes=[pltpu.VMEM((t + 2 * HALO, FFN_TC), F32), pltpu.VMEM((t + 2 * HALO, FFN_TC), F32)],
        compiler_params=_params(("parallel",)),
    )(up, up, cw, cw, cb, cb)


def _ffn_act_bwd(up, dact, cw, cb, after=None):
    t = up.shape[0]
    rows = _pick(t, FFN_ROWS, SUBLANE)
    nvb = D_FF // FFN_TC

    def body(ug_ref, uv_ref, da_ref, wg_ref, wv_ref, bg_ref, bv_ref,
             dug_ref, duv_ref, dwg_ref, dwv_ref, dbg_ref, dbv_ref, pg, pv, dcs):
        wg, wv, bg, bv = wg_ref[...], wv_ref[...], bg_ref[...], bv_ref[...]
        _pad_rows(pg, ug_ref)
        _pad_rows(pv, uv_ref)
        ext = rows + HALO
        acc_g = [jnp.zeros((1, FFN_TC), F32) for _ in range(4)]
        acc_v = [jnp.zeros((1, FFN_TC), F32) for _ in range(4)]
        for r0 in range(0, t, rows):
            cg, xg = _conv3(pg, wg, bg, r0, ext)
            cv, xv = _conv3(pv, wv, bv, r0, ext)
            if r0 + ext <= t:
                dav = da_ref[r0:r0 + ext, :]
            else:
                dav = jnp.concatenate([da_ref[r0:t, :], jnp.zeros((HALO, FFN_TC), F32)], axis=0)
            sg, dsg = _silu_and_grad(cg)
            for h, (dconv, xs, w, acc, out) in enumerate(((dav * cv * dsg, xg, wg, acc_g, dug_ref),
                                                           (dav * sg, xv, wv, acc_v, duv_ref))):
                dcs[h] = dconv
                d0 = dconv[0:rows, :]
                d1 = dcs[h, 1:rows + 1, :]
                d2 = dcs[h, 2:rows + 2, :]
                out[r0:r0 + rows, :] = (w[2:3, :] * d0 + w[1:2, :] * d1 + w[0:1, :] * d2).astype(BF16)
                x0, x1, x2 = xs
                acc[0] = acc[0] + jnp.sum(d0 * x2[0:rows, :], axis=0, keepdims=True)
                acc[1] = acc[1] + jnp.sum(d0 * x1[0:rows, :], axis=0, keepdims=True)
                acc[2] = acc[2] + jnp.sum(d0 * x0[0:rows, :], axis=0, keepdims=True)
                acc[3] = acc[3] + jnp.sum(d0, axis=0, keepdims=True)
        for acc, dw_ref, db_ref in ((acc_g, dwg_ref, dbg_ref), (acc_v, dwv_ref, dbv_ref)):
            dw_ref[0:1, :] = acc[0]
            dw_ref[1:2, :] = acc[1]
            dw_ref[2:3, :] = acc[2]
            db_ref[...] = acc[3]

    def colblk(nrow, off):
        return pl.BlockSpec((nrow, FFN_TC), lambda j: (0, off + j))

    in_specs = [colblk(t, 0), colblk(t, nvb), colblk(t, 0), colblk(3, 0), colblk(3, nvb), colblk(1, 0), colblk(1, nvb)]
    body, in_specs, args = _ordered(body, in_specs, [up, up, dact, cw, cw, cb, cb], after)
    return pl.pallas_call(
        body, name="ffn_act_bwd", grid=(nvb,),
        in_specs=in_specs,
        out_specs=[colblk(t, 0), colblk(t, 0), colblk(3, 0), colblk(3, 0), colblk(1, 0), colblk(1, 0)],
        out_shape=[jax.ShapeDtypeStruct((t, D_FF), BF16), jax.ShapeDtypeStruct((t, D_FF), BF16),
                   jax.ShapeDtypeStruct((3, D_FF), F32), jax.ShapeDtypeStruct((3, D_FF), F32),
                   jax.ShapeDtypeStruct((1, D_FF), F32), jax.ShapeDtypeStruct((1, D_FF), F32)],
        scratch_shapes=[pltpu.VMEM((t + 2 * HALO, FFN_TC), F32), pltpu.VMEM((t + 2 * HALO, FFN_TC), F32),
                        pltpu.VMEM((2, rows + HALO, FFN_TC), F32)],
        compiler_params=_params(("parallel",)),
    )(*args)


def _all_gather(shards, name):
    nw = len(shards)

    def body(*refs):
        x_refs, out_refs = refs[:nw], refs[nw:2 * nw]
        send_sems, recv_sems, local_sems = refs[2 * nw:]
        x, y, c = lax.axis_index("x"), lax.axis_index("y"), lax.axis_index("c")
        me, sibling = (x, y, c), (x, y, 1 - c)
        chips = [(1 - x, y), (x, 1 - y), (1 - x, 1 - y)]

        def copy(w, k, block, to, src=None):
            slot = out_refs[w].at[4 * block[0] + 2 * block[1] + block[2]]
            return pltpu.make_async_remote_copy(
                src_ref=slot if src is None else src, dst_ref=slot,
                send_sem=send_sems.at[w, k], recv_sem=recv_sems.at[w, k],
                device_id=to, device_id_type=MESH)

        mine, first, passed = [], [], []
        for w in range(nw):
            cp = pltpu.make_async_copy(x_refs[w], out_refs[w].at[4 * x + 2 * y + c], local_sems.at[w])
            cp.start()
            mine.append(cp)
            first.append(copy(w, 0, me, sibling, src=x_refs[w]))
            first += [copy(w, 1 + j, me, (*chip, c), src=x_refs[w]) for j, chip in enumerate(chips)]
        for cp in first:
            cp.start()
        for w in range(nw):
            for j, chip in enumerate(chips):
                copy(w, 1 + j, (*chip, c), me).wait_recv()
                fwd = copy(w, 4 + j, (*chip, c), sibling)
                fwd.start()
                passed.append(fwd)
        for w in range(nw):
            copy(w, 0, sibling, me).wait_recv()
            for j, chip in enumerate(chips):
                copy(w, 4 + j, (*chip, 1 - c), me).wait_recv()
        for cp in first + passed:
            cp.wait_send()
        for cp in mine:
            cp.wait()

    anyspec = pl.BlockSpec(memory_space=pl.ANY)
    return pl.pallas_call(
        body, name=name,
        in_specs=[anyspec] * nw, out_specs=[anyspec] * nw,
        out_shape=[jax.ShapeDtypeStruct((N_DEV,) + s.shape, s.dtype) for s in shards],
        scratch_shapes=[pltpu.SemaphoreType.DMA((nw, 7)), pltpu.SemaphoreType.DMA((nw, 7)),
                        pltpu.SemaphoreType.DMA((nw,))],
    )(*shards)


HBM_SPEC = pl.BlockSpec(memory_space=pltpu.HBM)
SEM_SPEC = pl.BlockSpec(memory_space=pltpu.SEMAPHORE)
ANY_SPEC = pl.BlockSpec(memory_space=pl.ANY)
DATAFLOW = pltpu.SideEffectType.DATAFLOW_SIDE_EFFECTING


def _my_index():
    return 4 * lax.axis_index("x") + 2 * lax.axis_index("y") + lax.axis_index("c")


def _peers():
    x, y, c = lax.axis_index("x"), lax.axis_index("y"), lax.axis_index("c")
    peers = []
    for k in range(1, N_DEV):
        px = 1 - x if k & 4 else x
        py = 1 - y if k & 2 else y
        pc = 1 - c if k & 1 else c
        peers.append((k, (px, py, pc), 4 * px + 2 * py + pc))
    return peers


def _split_copy(src_ref, land_ref, send_sems, recv_sems, w, k, peer, slot, scatter, outgoing):
    return pltpu.make_async_remote_copy(
        src_ref=src_ref.at[slot] if scatter else src_ref,
        dst_ref=land_ref.at[_my_index() if outgoing else slot],
        send_sem=send_sems.at[w * (N_DEV - 1) + k - 1], recv_sem=recv_sems.at[w * (N_DEV - 1) + k - 1],
        device_id=peer, device_id_type=MESH)


def _landing_zone(src, scatter):
    me = _my_index()
    own = lax.dynamic_index_in_dim(src, me, 0, keepdims=True) if scatter else src[None]
    shape = src.shape if scatter else (N_DEV,) + src.shape
    return lax.dynamic_update_slice_in_dim(lax.empty(shape, src.dtype), own, me, 0)


def _exchange_start(srcs, scatter, after, name, lands=None):
    nw = len(srcs)
    if lands is None:
        lands = [_landing_zone(s, scatter) for s in srcs]

    afters = [] if after is None else [after]

    def body(*refs):
        s_refs, l_refs = refs[:nw], refs[nw:2 * nw]
        send_sems, recv_sems = refs[2 * nw + len(afters)], refs[2 * nw + len(afters) + 1]
        token = refs[-1]
        for w in range(nw):
            for k, peer, slot in _peers():
                _split_copy(s_refs[w], l_refs[w], send_sems, recv_sems, w, k, peer, slot, scatter, True).start()
        token[...] = jnp.zeros_like(token)

    sems = pltpu.SemaphoreType.DMA((nw * (N_DEV - 1),))
    outs = pl.pallas_call(
        body, name=name,
        out_shape=(sems, sems, *[pltpu.HBM(a.shape, a.dtype) for a in (*srcs, *lands)],
                   jax.ShapeDtypeStruct((SUBLANE, LANE), F32)),
        in_specs=[HBM_SPEC] * (2 * nw) + [ANY_SPEC] * len(afters),
        out_specs=(SEM_SPEC, SEM_SPEC, *[HBM_SPEC] * (2 * nw), pl.BlockSpec(memory_space=pltpu.VMEM)),
        input_output_aliases={i: 2 + i for i in range(2 * nw)},
        compiler_params=pltpu.CompilerParams(has_side_effects=DATAFLOW),
    )(*[pltpu.with_memory_space_constraint(a, pltpu.HBM) for a in (*srcs, *lands)], *afters)
    return dict(sems=outs[:2], srcs=outs[2:2 + nw], lands=outs[2 + nw:2 + 2 * nw], token=outs[-1], scatter=scatter)


def _exchange_wait(handle, afters, name):
    srcs, lands, scatter = handle["srcs"], handle["lands"], handle["scatter"]
    nw = len(srcs)

    def body(*refs):
        s_refs, l_refs = refs[:nw], refs[nw:2 * nw]
        send_sems, recv_sems = refs[2 * nw], refs[2 * nw + 1]
        for w in range(nw):
            for k, peer, slot in _peers():
                cp = _split_copy(s_refs[w], l_refs[w], send_sems, recv_sems, w, k, peer, slot, scatter, False)
                cp.wait_send()
                cp.wait_recv()

    outs = pl.pallas_call(
        body, name=name,
        out_shape=tuple(pltpu.HBM(a.shape, a.dtype) for a in (*srcs, *lands)),
        in_specs=[HBM_SPEC] * (2 * nw) + [SEM_SPEC, SEM_SPEC] + [ANY_SPEC] * len(afters),
        out_specs=tuple([HBM_SPEC] * (2 * nw)),
        input_output_aliases={i: i for i in range(2 * nw)},
        compiler_params=pltpu.CompilerParams(has_side_effects=DATAFLOW),
    )(*srcs, *lands, *handle["sems"], *afters)
    return list(outs[nw:])


def _chips_and_sibling():
    x, y, c = lax.axis_index("x"), lax.axis_index("y"), lax.axis_index("c")
    return [(1 - x, y), (x, 1 - y), (1 - x, 1 - y)], (x, y, 1 - c), c


def _slot(px, py, pc):
    return 4 * px + 2 * py + pc


def _two_level_start(shards, name):
    nw = len(shards)
    lands = [_landing_zone(s, False) for s in shards]

    def body(*refs):
        s_refs, l_refs = refs[:nw], refs[nw:2 * nw]
        send_sems, recv_sems, token = refs[2 * nw], refs[2 * nw + 1], refs[-1]
        chips, sibling, c = _chips_and_sibling()
        for w in range(nw):
            for k, to in enumerate([sibling] + [(*chip, c) for chip in chips]):
                pltpu.make_async_remote_copy(
                    src_ref=s_refs[w], dst_ref=l_refs[w].at[_my_index()],
                    send_sem=send_sems.at[4 * w + k], recv_sem=recv_sems.at[4 * w + k],
                    device_id=to, device_id_type=MESH).start()
        token[...] = jnp.zeros_like(token)

    sems = pltpu.SemaphoreType.DMA((4 * nw,))
    outs = pl.pallas_call(
        body, name=name,
        out_shape=(sems, sems, *[pltpu.HBM(a.shape, a.dtype) for a in (*shards, *lands)],
                   jax.ShapeDtypeStruct((SUBLANE, LANE), F32)),
        in_specs=[HBM_SPEC] * (2 * nw),
        out_specs=(SEM_SPEC, SEM_SPEC, *[HBM_SPEC] * (2 * nw), pl.BlockSpec(memory_space=pltpu.VMEM)),
        input_output_aliases={i: 2 + i for i in range(2 * nw)},
        compiler_params=pltpu.CompilerParams(has_side_effects=DATAFLOW),
    )(*[pltpu.with_memory_space_constraint(a, pltpu.HBM) for a in (*shards, *lands)])
    return dict(sems=outs[:2], srcs=outs[2:2 + nw], lands=outs[2 + nw:2 + 2 * nw], token=outs[-1])


def _two_level_pass(handle, afters, name):
    srcs, lands = handle["srcs"], handle["lands"]
    nw = len(srcs)

    def body(*refs):
        s_refs, l_refs = refs[:nw], refs[nw:2 * nw]
        send_a, recv_a = refs[2 * nw], refs[2 * nw + 1]
        send_b, recv_b = refs[2 * nw + 2 + len(afters)], refs[2 * nw + 3 + len(afters)]
        chips, sibling, c = _chips_and_sibling()
        for w in range(nw):
            for j, chip in enumerate(chips):
                landed = l_refs[w].at[_slot(*chip, c)]
                pltpu.make_async_remote_copy(
                    src_ref=s_refs[w], dst_ref=landed, send_sem=send_a.at[4 * w + 1 + j], recv_sem=recv_a.at[4 * w + 1 + j],
                    device_id=(*chip, c), device_id_type=MESH).wait_recv()
                pltpu.make_async_remote_copy(
                    src_ref=landed, dst_ref=landed, send_sem=send_b.at[3 * w + j], recv_sem=recv_b.at[3 * w + j],
                    device_id=sibling, device_id_type=MESH).start()

    sems = pltpu.SemaphoreType.DMA((3 * nw,))
    outs = pl.pallas_call(
        body, name=name,
        out_shape=(sems, sems, *[pltpu.HBM(a.shape, a.dtype) for a in (*srcs, *lands)]),
        in_specs=[HBM_SPEC] * (2 * nw) + [SEM_SPEC, SEM_SPEC] + [ANY_SPEC] * len(afters),
        out_specs=(SEM_SPEC, SEM_SPEC, *[HBM_SPEC] * (2 * nw)),
        input_output_aliases={i: 2 + i for i in range(2 * nw)},
        compiler_params=pltpu.CompilerParams(has_side_effects=DATAFLOW),
    )(*srcs, *lands, *handle["sems"], *afters)
    return dict(sems=handle["sems"], sems_pass=outs[:2], srcs=outs[2:2 + nw], lands=outs[2 + nw:2 + 2 * nw])


def _two_level_wait(handle, name):
    srcs, lands = handle["srcs"], handle["lands"]
    nw = len(srcs)

    def body(*refs):
        s_refs, l_refs = refs[:nw], refs[nw:2 * nw]
        send_a, recv_a, send_b, recv_b = refs[2 * nw:2 * nw + 4]
        chips, sibling, c = _chips_and_sibling()
        x, y = sibling[0], sibling[1]
        for w in range(nw):
            first = pltpu.make_async_remote_copy(
                src_ref=s_refs[w], dst_ref=l_refs[w].at[_slot(x, y, 1 - c)], send_sem=send_a.at[4 * w],
                recv_sem=recv_a.at[4 * w], device_id=sibling, device_id_type=MESH)
            first.wait_send()
            first.wait_recv()
            for j, chip in enumerate(chips):
                pltpu.make_async_remote_copy(
                    src_ref=s_refs[w], dst_ref=l_refs[w].at[_slot(*chip, c)], send_sem=send_a.at[4 * w + 1 + j],
                    recv_sem=recv_a.at[4 * w + 1 + j], device_id=(*chip, c), device_id_type=MESH).wait_send()
                passed = pltpu.make_async_remote_copy(
                    src_ref=l_refs[w].at[_slot(*chip, c)], dst_ref=l_refs[w].at[_slot(*chip, 1 - c)],
                    send_sem=send_b.at[3 * w + j], recv_sem=recv_b.at[3 * w + j], device_id=sibling, device_id_type=MESH)
                passed.wait_send()
                passed.wait_recv()

    outs = pl.pallas_call(
        body, name=name,
        out_shape=tuple(pltpu.HBM(a.shape, a.dtype) for a in (*srcs, *lands)),
        in_specs=[HBM_SPEC] * (2 * nw) + [SEM_SPEC] * 4,
        out_specs=tuple([HBM_SPEC] * (2 * nw)),
        input_output_aliases={i: i for i in range(2 * nw)},
        compiler_params=pltpu.CompilerParams(has_side_effects=DATAFLOW),
    )(*srcs, *lands, *handle["sems"], *handle["sems_pass"])
    return list(outs[nw:])


def _adamw(w, g, m, v):
    m = ADAM_B1 * m + (1.0 - ADAM_B1) * g
    v = ADAM_B2 * v + (1.0 - ADAM_B2) * (g * g)
    m_hat = m / (1.0 - ADAM_B1 ** ADAM_STEP)
    v_hat = v / (1.0 - ADAM_B2 ** ADAM_STEP)
    delta = -ADAM_LR * (m_hat / (jnp.sqrt(v_hat) + ADAM_EPS) + ADAM_WD * w)
    return delta, m, v


def _sum_adam(parts, w, m, v, name):
    _, r, c = parts.shape
    tr = _pick(r, 256, 16)

    def body(p_ref, w_ref, m_ref, v_ref, g_ref, d_ref, mo_ref, vo_ref):
        g = p_ref[0].astype(F32)
        for s in range(1, N_DEV):
            g = g + p_ref[s].astype(F32)
        g_ref[...] = g
        d_ref[...], mo_ref[...], vo_ref[...] = _adamw(w_ref[...], g, m_ref[...], v_ref[...])

    row = pl.BlockSpec((tr, c), lambda i: (i, 0))
    sh = jax.ShapeDtypeStruct((r, c), F32)
    return pl.pallas_call(
        body, name=name, grid=(r // tr,),
        in_specs=[pl.BlockSpec((N_DEV, tr, c), lambda i: (0, i, 0)), row, row, row],
        out_specs=[row, row, row, row], out_shape=[sh, sh, sh, sh],
        compiler_params=_params(("parallel",)),
    )(parts, w, m, v)


def _sum_slots(parts, name):
    _, r, c = parts.shape
    tr = _pick(r, 512, SUBLANE)

    def body(p_ref, o_ref):
        g = p_ref[0]
        for s in range(1, N_DEV):
            g = g + p_ref[s]
        o_ref[...] = g

    return pl.pallas_call(
        body, name=name, grid=(r // tr,),
        in_specs=[pl.BlockSpec((N_DEV, tr, c), lambda i: (0, i, 0))],
        out_specs=pl.BlockSpec((tr, c), lambda i: (i, 0)), out_shape=jax.ShapeDtypeStruct((r, c), F32),
        compiler_params=_params(("parallel",)),
    )(parts)


def _adam_rows(g, w, m, v, name):
    r, c = g.shape
    tr = _pick(r, 512, SUBLANE)

    def body(g_ref, w_ref, m_ref, v_ref, d_ref, mo_ref, vo_ref):
        d_ref[...], mo_ref[...], vo_ref[...] = _adamw(w_ref[...], g_ref[...], m_ref[...], v_ref[...])

    row = pl.BlockSpec((tr, c), lambda i: (i, 0))
    sh = jax.ShapeDtypeStruct((r, c), F32)
    return pl.pallas_call(body, name=name, grid=(r // tr,), in_specs=[row] * 4, out_specs=[row] * 3,
                          out_shape=[sh, sh, sh], compiler_params=_params(("parallel",)))(g, w, m, v)


def _pack(arrays):
    flat = jnp.concatenate([a.reshape(-1).astype(F32) for a in arrays])
    pad = (-flat.shape[0]) % (SUBLANE * LANE)
    return jnp.pad(flat, (0, pad)).reshape(-1, LANE)


def _unpack(packed, shapes):
    flat = packed.reshape(-1)
    out, off = [], 0
    for s in shapes:
        n = math.prod(s)
        out.append(flat[off:off + n].reshape(s))
        off += n
    return out


def _block_diag(t):
    eye = jnp.eye(S5_SUPER, dtype=bool)
    bd = jnp.where(eye[None, :, None, :, None], t[:, :, :, None, :], 0.0)
    return bd.reshape(S5_SUPER, S5_SUPER * t.shape[2], S5_SUPER * t.shape[3])


def _diag_blocks(dense, a, b):
    x = dense.reshape(S5_SUPER, S5_SUPER, a, S5_SUPER, b)
    return jnp.moveaxis(jnp.diagonal(x, axis1=1, axis2=3), -1, 1)


def _s5_layouts(b_re, b_im, c_re, c_im, d):
    g2 = (S5_GROUPS // S5_SUPER, S5_SUPER)
    bt = lambda b: _block_diag(b.reshape(*g2, S5_STATE, S5_GROUP).transpose(0, 1, 3, 2))
    ct = lambda c: _block_diag(c.reshape(*g2, S5_GROUP, S5_STATE).transpose(0, 1, 3, 2))
    bsg = jnp.concatenate([bt(b_re), bt(b_im)], axis=2).astype(BF16)
    ccat = jnp.concatenate([ct(c_re), -ct(c_im)], axis=1).astype(BF16)
    return bsg, ccat, d.reshape(1, S5_WIDTH)


def _s5_param_grads(gb, gc):
    n = S5_LANES
    gb_re = _diag_blocks(gb[:, :, 0:n], S5_GROUP, S5_STATE).transpose(0, 1, 3, 2).reshape(S5_GROUPS, S5_STATE, S5_GROUP)
    gb_im = _diag_blocks(gb[:, :, n:2 * n], S5_GROUP, S5_STATE).transpose(0, 1, 3, 2).reshape(S5_GROUPS, S5_STATE, S5_GROUP)
    gc_re = _diag_blocks(gc[:, 0:n, :], S5_STATE, S5_GROUP).transpose(0, 1, 3, 2).reshape(S5_GROUPS, S5_GROUP, S5_STATE)
    gc_im = -_diag_blocks(gc[:, n:2 * n, :], S5_STATE, S5_GROUP).transpose(0, 1, 3, 2).reshape(S5_GROUPS, S5_GROUP, S5_STATE)
    return gb_re, gb_im, gc_re, gc_im


def _local_step(x, target, weight, emit, small, after=None):
    sp = small
    a_re, a_im = sp["s5_a_re"], sp["s5_a_im"]
    ldt = sp["s5_log_dt"].reshape(S5_GROUPS, 1)

    h1 = _rms_fwd(x, sp["ln_mix_g"], "rms_mix", after=after)
    w_in = weight("w_in", h1)
    proj = _mm_nn(h1, w_in, "mm_in", after=weight("after_w_in", None))
    conv_w = weight("conv_w", None)
    disc = _s5_param_fwd(a_re, a_im, ldt)
    bsg, ccat, d_row = sp["s5_layouts"]
    abar_t, coef_t = _s5_to_tile(disc[0], disc[1]), _s5_to_tile(disc[2], disc[3])
    y, sb = _s5_fwd(proj, bsg, ccat, d_row, abar_t, coef_t)
    z16 = _gelu_fwd(y)
    w_glu = weight("s5_w_glu", z16)
    gl = _mm_nn(z16, w_glu, "mm_glu")
    z2 = _glu_fwd(y, gl, sp["s5_b_glu"])
    w_ps = weight("w_proj_s5", z2)
    ys = _mm_nn(z2, w_ps, "mm_proj_s5")
    o_raw, oh, s0s = _hgrn_fwd(proj, sp["hgrn_lb_logits"], sp["hgrn_norm_g"])
    w_ph = weight("w_proj_hgrn", oh)
    yh = _mm_nn(oh, w_ph, "mm_proj_hgrn")
    merged = _merge_fwd(proj, ys, yh)
    w_out = weight("w_out", merged)
    x1 = _mm_nn(merged, w_out, "mm_out", res=x)
    h2 = _rms_fwd(x1, sp["ln_ffn_g"], "rms_ffn")
    w_up = weight("w_up", h2)
    up = _mm_nn(h2, w_up, "mm_up")
    act = _ffn_act_fwd(up, conv_w, sp["conv_b"])
    w_down = weight("w_down", act)
    x2 = _mm_nn(act, w_down, "mm_down", res=x1)
    dx2, dx2_16, g_ln_final, loss = _loss_head(x2, sp["ln_final_g"], target)

    dact = _mm_nt(dx2_16, w_down, "mm_down_dx")
    tok = emit("w_down", _mm_tn(act, dx2_16, 1, "mm_down_dw"))
    dup_g, dup_v, dcw_g, dcw_v, dcb_g, dcb_v = _ffn_act_bwd(up, dact, conv_w, sp["conv_b"], after=tok)
    dup = jnp.concatenate([dup_g, dup_v], axis=1)
    g_conv_w = jnp.concatenate([dcw_g, dcw_v], axis=1)
    g_conv_b = jnp.concatenate([dcb_g, dcb_v], axis=1)
    dh2 = _mm_nt(dup, w_up, "mm_up_dx")
    tok = emit("w_up", _mm_tn(h2, dup, N_DEV, "mm_up_dw"))
    dx1, dx1_16, g_ln_ffn = _rms_bwd(x1, sp["ln_ffn_g"], dh2, dx2, "rms_ffn_bwd", True, after=tok)

    dmerged = _mm_nt(dx1_16, w_out, "mm_out_dx")
    tok = emit("w_out", _mm_tn(merged, dx1_16, 1, "mm_out_dw"))
    dys, dyh, dgs, dgh = _merge_bwd(proj, ys, yh, dmerged, after=tok)
    doh = _mm_nt(dyh, w_ph, "mm_proj_hgrn_dx")
    tok = emit("w_proj_hgrn", _mm_tn(oh, dyh, N_DEV, "mm_proj_hgrn_dw"))
    dz2 = _mm_nt(dys, w_ps, "mm_proj_s5_dx", after=tok)
    tok = emit("w_proj_s5", _mm_tn(z2, dys, N_DEV, "mm_proj_s5_dw"))
    dgl, dza, g_b_glu = _glu_bwd(y, gl, sp["s5_b_glu"], dz2, after=tok)
    dzb = _mm_nt(dgl, w_glu, "mm_glu_dx")
    tok = emit("s5_w_glu", _mm_tn(z16, dgl, 1, "mm_glu_dw"))
    dy = _gelu_bwd(y, dza, dzb, after=tok)
    du, gb, gc, gd, g_abar_t, g_coef_t = _s5_bwd(proj, dy, sb, bsg, ccat, d_row, abar_t, coef_t)
    g_a_re, g_a_im, g_ldt = _s5_param_bwd(a_re, a_im, ldt, [*_s5_from_tile(g_abar_t), *_s5_from_tile(g_coef_t)])
    g_b_re, g_b_im, g_c_re, g_c_im = _s5_param_grads(gb, gc)
    dq, dz, dv, dg, g_norm, dlb = _hgrn_bwd(proj, o_raw, s0s, doh, sp["hgrn_lb_logits"], sp["hgrn_norm_g"])
    g_logits = _lb_bwd(sp["hgrn_lb_logits"], dlb)

    small_g = dict(s5_a_re=g_a_re, s5_a_im=g_a_im, s5_log_dt=g_ldt.reshape(1, S5_GROUPS),
                   s5_b_re=g_b_re, s5_b_im=g_b_im, s5_c_re=g_c_re, s5_c_im=g_c_im,
                   s5_d=gd.reshape(S5_GROUPS, S5_GROUP), s5_b_glu=g_b_glu, hgrn_lb_logits=g_logits,
                   hgrn_norm_g=g_norm, ln_ffn_g=g_ln_ffn, conv_w=g_conv_w, conv_b=g_conv_b, ln_final_g=g_ln_final,
                   loss=loss[0, 0:1])
    tok_small = emit("small", small_g)

    dproj = jnp.concatenate([du, dq, dz, dv, dg, dgs, dgh], axis=1)
    tok = emit("w_in", _mm_tn(h1, dproj, N_DEV, "mm_in_dw", after=tok_small))
    dh1 = _mm_nt(dproj, w_in, "mm_in_dx")
    grad_x, g_ln_mix = _rms_bwd(x, sp["ln_mix_g"], dh1, dx1, "rms_mix_bwd", False, after=tok)
    return grad_x, g_ln_mix


BIG = ("w_in", "s5_w_glu", "w_proj_s5", "w_proj_hgrn", "w_out", "w_up", "w_down")
COL_SHARDED = ("w_in", "w_proj_s5", "w_proj_hgrn", "w_up")
SMALL = ("ln_mix_g", "s5_a_re", "s5_a_im", "s5_log_dt", "s5_b_re", "s5_b_im", "s5_c_re", "s5_c_im", "s5_d",
         "s5_b_glu", "hgrn_lb_logits", "hgrn_norm_g", "ln_ffn_g", "conv_b", "ln_final_g")
WEIGHTS = ("ln_mix_g", "w_in", "s5_a_re", "s5_a_im", "s5_log_dt", "s5_b_re", "s5_b_im", "s5_c_re", "s5_c_im", "s5_d",
           "s5_w_glu", "s5_b_glu", "w_proj_s5", "hgrn_lb_logits", "hgrn_norm_g", "w_proj_hgrn", "w_out", "ln_ffn_g",
           "w_up", "conv_w", "conv_b", "w_down", "ln_final_g")


def kernel(x, ln_mix_g, w_in, s5_a_re, s5_a_im, s5_log_dt, s5_b_re, s5_b_im, s5_c_re, s5_c_im, s5_d, s5_w_glu, s5_b_glu, w_proj_s5, hgrn_lb_logits, hgrn_norm_g, w_proj_hgrn, w_out, ln_ffn_g, w_up, conv_w, conv_b, w_down, ln_final_g, loss_target, m_ln_mix_g, m_w_in, m_s5_a_re, m_s5_a_im, m_s5_log_dt, m_s5_b_re, m_s5_b_im, m_s5_c_re, m_s5_c_im, m_s5_d, m_s5_w_glu, m_s5_b_glu, m_w_proj_s5, m_hgrn_lb_logits, m_hgrn_norm_g, m_w_proj_hgrn, m_w_out, m_ln_ffn_g, m_w_up, m_conv_w, m_conv_b, m_w_down, m_ln_final_g, v_ln_mix_g, v_w_in, v_s5_a_re, v_s5_a_im, v_s5_log_dt, v_s5_b_re, v_s5_b_im, v_s5_c_re, v_s5_c_im, v_s5_d, v_s5_w_glu, v_s5_b_glu, v_w_proj_s5, v_hgrn_lb_logits, v_hgrn_norm_g, v_w_proj_hgrn, v_w_out, v_ln_ffn_g, v_w_up, v_conv_w, v_conv_b, v_w_down, v_ln_final_g):
    given = dict(locals())
    w = {n: given[n] for n in WEIGHTS}
    mom = {n: given["m_" + n] for n in WEIGHTS}
    var = {n: given["v_" + n] for n in WEIGHTS}

    first = _two_level_start([w_in[0].astype(BF16), conv_w[0]], "gather_first_start")
    zero = first["token"][0, 0]
    packed_small = SMALL[1:]
    pw, pm, pv = (_pack([d[n] for n in packed_small]) + zero for d in (w, mom, var))
    layouts = _s5_layouts(s5_b_re[0] + zero, s5_b_im[0], s5_c_re[0] + zero, s5_c_im[0], s5_d[0])
    gather_groups = (("s5_w_glu", "w_proj_s5", "w_proj_hgrn", "w_out"), ("w_up",), ("w_down",))
    shard16 = {n: w[n][0].astype(BF16) + zero.astype(BF16) for g in gather_groups for n in g}
    zones = {n: _landing_zone(s, False) for n, s in shard16.items()}
    pending, ready = {}, {}

    def weight(name, after):
        if "w_in" not in ready:
            local_work = [after, pw, pm, pv, layouts[0], layouts[1], *zones.values()]
            passed = _two_level_pass(first, local_work, "gather_first_pass")
            ready["w_in"], conv_w_all = _two_level_wait(passed, "gather_first_wait")
            ready["conv_w"] = conv_w_all.transpose(1, 0, 2).reshape(3, 2 * D_FF)
            token = ready["w_in"]
            for i, group in enumerate(gather_groups):
                handle = _exchange_start([shard16[n] for n in group], False, token, f"gather_start_{i}",
                                         lands=[zones[n] for n in group])
                token = handle["token"]
                for n in group:
                    pending[n] = (group, handle, f"gather_wait_{i}")
            ready["after_w_in"] = token
        if name not in ready:
            group, handle, wait_name = pending[name]
            for n, g in zip(group, _exchange_wait(handle, [after], wait_name)):
                ready[n] = g
        g = ready[name]
        return g if name not in BIG or name in COL_SHARDED else g.reshape(1, N_DEV * g.shape[1], g.shape[2])

    scatter_groups = (("w_down",), ("w_up",), ("w_out", "w_proj_hgrn", "w_proj_s5", "s5_w_glu"), ("w_in",))
    emitted, scatters = {}, []
    packed_names = SMALL[1:] + ("conv_w", "loss")

    def emit(name, grad):
        if name == "small":
            emitted[name] = ([grad[n].shape for n in packed_names],
                             _exchange_start([_pack([grad[n] for n in packed_names])], False, None, "small_start"))
            return emitted[name][1]["token"]
        emitted[name] = grad if name in COL_SHARDED else grad.reshape(N_DEV, -1, grad.shape[2])
        group = scatter_groups[len(scatters)]
        if not all(n in emitted for n in group):
            return None
        handle = _exchange_start([emitted[n] for n in group], True, None, f"scatter_start_{len(scatters)}")
        scatters.append((group, handle))
        return handle["token"]

    small = dict(ln_mix_g=ln_mix_g, s5_a_re=s5_a_re[0], s5_a_im=s5_a_im[0], s5_log_dt=s5_log_dt, s5_layouts=layouts,
                 s5_b_glu=s5_b_glu, hgrn_lb_logits=hgrn_lb_logits, hgrn_norm_g=hgrn_norm_g, ln_ffn_g=ln_ffn_g,
                 conv_b=conv_b, ln_final_g=ln_final_g.reshape(1, D_MODEL))
    grad_x, g_ln_mix = _local_step(x[0], loss_target[0], weight, emit, small, after=first["token"])

    shapes, handle = emitted["small"]
    total = _sum_slots(_exchange_wait(handle, [grad_x], "small_wait")[0], "sum_small")
    summed = dict(zip(packed_names, _unpack(total, shapes)))
    mix_all = _all_gather([g_ln_mix.reshape(-1, LANE)], "gather_ln_mix")[0]
    summed["ln_mix_g"] = _sum_slots(mix_all, "sum_ln_mix").reshape(1, D_MODEL)

    grads, delta, new_m, new_v = {}, {}, {}, {}
    afters = [grad_x, total]
    for i, (group, handle) in enumerate(scatters):
        for n, r in zip(group, _exchange_wait(handle, afters, f"scatter_wait_{i}")):
            g, d, m2, v2 = _sum_adam(r, w[n][0], mom[n][0], var[n][0], "adam_" + n)
            grads[n], delta[n], new_m[n], new_v[n] = g[None], d[None], m2[None], v2[None]
        if i == len(scatters) - 2:
            afters = [delta[n] for g2, _ in scatters[:-1] for n in g2]

    d_s, m_s, v_s = _adam_rows(_pack([summed[n] for n in packed_small]), pw, pm, pv, "adam_small")
    wshapes = [w[n].shape for n in packed_small]
    for n, d, m2, v2 in zip(packed_small, _unpack(d_s, wshapes), _unpack(m_s, wshapes), _unpack(v_s, wshapes)):
        grads[n], delta[n], new_m[n], new_v[n] = summed[n].reshape(w[n].shape), d, m2, v2
    grads["ln_mix_g"] = summed["ln_mix_g"]
    delta["ln_mix_g"], new_m["ln_mix_g"], new_v["ln_mix_g"] = _adam_rows(summed["ln_mix_g"], ln_mix_g, m_ln_mix_g,
                                                                         v_ln_mix_g, "adam_ln_mix")
    me = 4 * lax.axis_index("x") + 2 * lax.axis_index("y") + lax.axis_index("c")
    ncol = conv_w.shape[2]
    g_cw = lax.dynamic_slice_in_dim(summed["conv_w"], me * ncol, ncol, axis=1)
    d_cw, m_cw, v_cw = _adam_rows(g_cw, conv_w[0], m_conv_w[0], v_conv_w[0], "adam_conv_w")
    grads["conv_w"], delta["conv_w"], new_m["conv_w"], new_v["conv_w"] = g_cw[None], d_cw[None], m_cw[None], v_cw[None]

    return (summed["loss"].reshape(()), grad_x[None], *[grads[n] for n in WEIGHTS], *[delta[n] for n in WEIGHTS],
            *[new_m[n] for n in WEIGHTS], *[new_v[n] for n in WEIGHTS])
```

```python
import math

import jax
import jax.numpy as jnp
from jax import lax
from jax.experimental import pallas as pl
from jax.experimental.pallas import tpu as pltpu

F32 = jnp.float32
BF16 = jnp.bfloat16

N_DEV = 8
D_MODEL = 2048
S5_WIDTH = 1024
S5_GROUP = 16
S5_GROUPS = 64
S5_STATE = 64
S5_MAX_RE = -1e-4
S5_SUPER = 8
S5_LANES = S5_SUPER * S5_STATE
HGRN_WIDTH = 1024
HGRN_HEADS = 8
HGRN_DH = 128
HGRN_CHUNK = 64
HGRN_SUBS = 8
D_FF = 5632
RMS_EPS = 1e-6
ADAM_LR = 0.001
ADAM_B1 = 0.9
ADAM_B2 = 0.999
ADAM_EPS = 1e-08
ADAM_WD = 0.01
ADAM_STEP = 10

LANE = 128
SUBLANE = 8
VMEM_LIMIT = 48 * 1024 * 1024
MESH = pl.DeviceIdType.MESH
GELU_C = math.sqrt(2.0 / math.pi)
GELU_A = 0.044715


def _params(sem=None):
    return pltpu.CompilerParams(dimension_semantics=sem, vmem_limit_bytes=VMEM_LIMIT)


def _pick(n, cap, unit=LANE):
    best = None
    for t in range(unit, min(n, cap) + 1, unit):
        if n % t == 0:
            best = t
    return best if best is not None else n


def _ordered(body, in_specs, args, after):
    if after is None:
        return body, list(in_specs), list(args)
    n_in = len(args)

    def ordered_body(*refs):
        return body(*refs[:n_in], *refs[n_in + 1:])

    return ordered_body, [*in_specs, pl.BlockSpec(memory_space=pl.ANY)], [*args, after]


def _sigmoid(x):
    return 0.5 * jnp.tanh(0.5 * x) + 0.5


def _silu_and_grad(x):
    s = _sigmoid(x)
    return x * s, s * (1.0 + x * (1.0 - s))


def _gelu_and_grad(y):
    inner = GELU_C * (y + GELU_A * y * y * y)
    th = jnp.tanh(inner)
    val = 0.5 * y * (1.0 + th)
    grad = 0.5 * (1.0 + th) + 0.5 * y * (1.0 - th * th) * GELU_C * (1.0 + 3.0 * GELU_A * y * y)
    return val, grad


def _dot(a, b):
    return jnp.dot(a, b, preferred_element_type=F32)


def _dot_nt(a, b):
    return lax.dot_general(a, b, (((1,), (1,)), ((), ())), preferred_element_type=F32)


def _dot_tn(a, b):
    return lax.dot_general(a, b, (((0,), (0,)), ((), ())), preferred_element_type=F32)


def _blocks_per_step(nb, ns, tn, cap=2048):
    if tn != ns:
        return 1
    best = 1
    for b in range(1, nb + 1):
        if nb % b == 0 and b * ns <= cap:
            best = b
    return best


NN_TILE_BYTES = 42 * 1024 * 1024


def _mm_nn(a, w, name, res=None, out_dtype=F32, after=None):
    m, kdim = a.shape
    nb, _, ns = w.shape
    tk, tn = _pick(kdim, D_FF), _pick(ns, 1536)
    npb, nk = ns // tn, kdim // tk
    bps = _blocks_per_step(nb, ns, tn)
    assert bps == 1 or nk == 1

    def buffers(rows):
        return 2 * (rows * tk * 2 + bps * tk * tn * 2 + rows * bps * tn * 4 * (2 if res is not None else 1))

    tm = next((r for r in (_pick(m, 1024), _pick(m, 512)) if buffers(r) <= NN_TILE_BYTES), _pick(m, 256))

    def body(*refs):
        a_ref, w_ref = refs[0], refs[1]
        r_ref = refs[2] if res is not None else None
        o_ref = refs[3] if res is not None else refs[2]

        def finish(r, cols):
            if res is not None:
                r = r + r_ref[:, cols]
            o_ref[:, cols] = r.astype(out_dtype)

        if nk == 1:
            for b in range(bps):
                finish(_dot(a_ref[...], w_ref[b]), slice(b * tn, (b + 1) * tn))
            return
        acc = refs[-1]
        k = pl.program_id(2)

        @pl.when(k == 0)
        def _():
            acc[...] = jnp.zeros_like(acc)

        acc[...] += _dot(a_ref[...], w_ref[0])

        @pl.when(k == nk - 1)
        def _():
            finish(acc[...], slice(0, tn))

    in_specs = [pl.BlockSpec((tm, tk), lambda j, i, k: (i, k)),
                pl.BlockSpec((bps, tk, tn), lambda j, i, k: (j // npb, k, j % npb))]
    args = [a, w]
    if res is not None:
        in_specs.append(pl.BlockSpec((tm, bps * tn), lambda j, i, k: (i, j)))
        args.append(res)
    body, in_specs, args = _ordered(body, in_specs, args, after)
    return pl.pallas_call(
        body, name=name, grid=(nb * npb // bps, m // tm, nk),
        in_specs=in_specs, out_specs=pl.BlockSpec((tm, bps * tn), lambda j, i, k: (i, j)),
        out_shape=jax.ShapeDtypeStruct((m, nb * ns), out_dtype),
        scratch_shapes=[pltpu.VMEM((tm, tn), F32)] if nk > 1 else [],
        compiler_params=_params(("parallel", "parallel", "arbitrary")),
    )(*args)


NT_STEP_COLS = 2816


def _mm_nt(a, w, name, out_dtype=F32, after=None):
    m, _ = a.shape
    nb, kdim, ns = w.shape
    tm, tko, tn = _pick(m, 1024), _pick(kdim, 1024), _pick(ns, 2048)
    npb = ns // tn
    bps = _blocks_per_step(nb, ns, tn, cap=NT_STEP_COLS)
    nred = nb * npb // bps

    def body(a_ref, w_ref, o_ref, *scratch):
        total = _dot_nt(a_ref[:, 0:tn], w_ref[0])
        for b in range(1, bps):
            total = total + _dot_nt(a_ref[:, b * tn:(b + 1) * tn], w_ref[b])
        if nred == 1:
            o_ref[...] = total.astype(out_dtype)
            return
        acc = scratch[0]
        n = pl.program_id(2)

        @pl.when(n == 0)
        def _():
            acc[...] = jnp.zeros_like(acc)

        acc[...] += total

        @pl.when(n == nred - 1)
        def _():
            o_ref[...] = acc[...].astype(out_dtype)

    in_specs = [pl.BlockSpec((tm, bps * tn), lambda i, j, n: (i, n)),
                pl.BlockSpec((bps, tko, tn), lambda i, j, n: (n // npb, j, n % npb))]
    body, in_specs, args = _ordered(body, in_specs, [a, w], after)
    return pl.pallas_call(
        body, name=name, grid=(m // tm, kdim // tko, nred),
        in_specs=in_specs,
        out_specs=pl.BlockSpec((tm, tko), lambda i, j, n: (i, j)),
        out_shape=jax.ShapeDtypeStruct((m, kdim), out_dtype),
        scratch_shapes=[pltpu.VMEM((tm, tko), F32)] if nred > 1 else [],
        compiler_params=_params(("parallel", "parallel", "arbitrary")),
    )(*args)


def _mm_tn(a, d, nb, name, out_dtype=BF16, after=None):
    m, kdim = a.shape
    ns = d.shape[1] // nb
    tm, tko, tn = _pick(m, 4096), _pick(kdim, 512), _pick(ns, 1536)
    npb, nm = ns // tn, m // tm
    bps = _blocks_per_step(nb, ns, tn, cap=1536)
    assert bps == 1 or nm == 1

    def body(a_ref, d_ref, o_ref, *scratch):
        if nm == 1:
            for b in range(bps):
                o_ref[b] = _dot_tn(a_ref[...], d_ref[:, b * tn:(b + 1) * tn]).astype(out_dtype)
            return
        acc = scratch[0]
        r = pl.program_id(2)

        @pl.when(r == 0)
        def _():
            acc[...] = jnp.zeros_like(acc)

        acc[...] += _dot_tn(a_ref[...], d_ref[...])

        @pl.when(r == nm - 1)
        def _():
            o_ref[0] = acc[...].astype(out_dtype)

    in_specs = [pl.BlockSpec((tm, tko), lambda j, i, r: (r, i)), pl.BlockSpec((tm, bps * tn), lambda j, i, r: (r, j))]
    body, in_specs, args = _ordered(body, in_specs, [a, d], after)
    return pl.pallas_call(
        body, name=name, grid=(nb * npb // bps, kdim // tko, nm),
        in_specs=in_specs,
        out_specs=pl.BlockSpec((bps, tko, tn), lambda j, i, r: (j // npb, i, j % npb)),
        out_shape=jax.ShapeDtypeStruct((nb, kdim, ns), out_dtype),
        scratch_shapes=[pltpu.VMEM((tko, tn), F32)] if nm > 1 else [],
        compiler_params=_params(("parallel", "parallel", "arbitrary")),
    )(*args)


def _rms_fwd(x, g, name, after=None):
    t, d = x.shape
    tr = _pick(t, 512, SUBLANE)

    def body(x_ref, g_ref, h_ref):
        xv = x_ref[...]
        r = lax.rsqrt(jnp.mean(xv * xv, axis=-1, keepdims=True) + RMS_EPS)
        h_ref[...] = (xv * r * g_ref[...]).astype(BF16)

    in_specs = [pl.BlockSpec((tr, d), lambda i: (i, 0)), pl.BlockSpec((1, d), lambda i: (0, 0))]
    body, in_specs, args = _ordered(body, in_specs, [x, g], after)
    return pl.pallas_call(
        body, name=name, grid=(t // tr,),
        in_specs=in_specs,
        out_specs=pl.BlockSpec((tr, d), lambda i: (i, 0)),
        out_shape=jax.ShapeDtypeStruct((t, d), BF16),
        compiler_params=_params(("parallel",)),
    )(*args)


def _rms_bwd(x, g, dh, add, name, want_bf16, after=None):
    t, d = x.shape
    tr = _pick(t, 256, SUBLANE)

    def body(x_ref, g_ref, dh_ref, add_ref, *outs):
        if want_bf16:
            dx_ref, dxb_ref, dg_ref = outs
        else:
            dx_ref, dg_ref = outs
        i = pl.program_id(0)

        @pl.when(i == 0)
        def _():
            dg_ref[...] = jnp.zeros_like(dg_ref)

        xv, dhv = x_ref[...], dh_ref[...]
        r = lax.rsqrt(jnp.mean(xv * xv, axis=-1, keepdims=True) + RMS_EPS)
        xh = xv * r
        dg_ref[...] += jnp.sum(dhv * xh, axis=0, keepdims=True)
        dxh = dhv * g_ref[...]
        dx = add_ref[...] + r * (dxh - xh * jnp.mean(dxh * xh, axis=-1, keepdims=True))
        dx_ref[...] = dx
        if want_bf16:
            dxb_ref[...] = dx.astype(BF16)

    row = pl.BlockSpec((tr, d), lambda i: (i, 0))
    vec = pl.BlockSpec((1, d), lambda i: (0, 0))
    out_specs = [row] + ([row] if want_bf16 else []) + [vec]
    out_shape = ([jax.ShapeDtypeStruct((t, d), F32)] + ([jax.ShapeDtypeStruct((t, d), BF16)] if want_bf16 else [])
                 + [jax.ShapeDtypeStruct((1, d), F32)])
    body, in_specs, args = _ordered(body, [row, vec, row, row], [x, g, dh, add], after)
    return pl.pallas_call(
        body, name=name, grid=(t // tr,),
        in_specs=in_specs, out_specs=out_specs, out_shape=out_shape,
        compiler_params=_params(("arbitrary",)),
    )(*args)


def _loss_head(x2, g, target, name="loss_head"):
    t, d = x2.shape
    tr = _pick(t, 512, SUBLANE)

    def body(x_ref, g_ref, t_ref, dx_ref, dxb_ref, dg_ref, loss_ref):
        i = pl.program_id(0)

        @pl.when(i == 0)
        def _():
            dg_ref[...] = jnp.zeros_like(dg_ref)
            loss_ref[...] = jnp.zeros_like(loss_ref)

        xv = x_ref[...]
        gv = g_ref[...]
        r = lax.rsqrt(jnp.mean(xv * xv, axis=-1, keepdims=True) + RMS_EPS)
        xh = xv * r
        err = xh * gv - t_ref[...]
        part = 0.5 * jnp.sum(jnp.mean(err * err, axis=-1, keepdims=True), axis=0, keepdims=True)
        loss_ref[...] += jnp.broadcast_to(part, loss_ref.shape)
        dy = err * (1.0 / d)
        dg_ref[...] += jnp.sum(dy * xh, axis=0, keepdims=True)
        dxh = dy * gv
        dx = r * (dxh - xh * jnp.mean(dxh * xh, axis=-1, keepdims=True))
        dx_ref[...] = dx
        dxb_ref[...] = dx.astype(BF16)

    row = pl.BlockSpec((tr, d), lambda i: (i, 0))
    vec = pl.BlockSpec((1, d), lambda i: (0, 0))
    return pl.pallas_call(
        body, name=name, grid=(t // tr,),
        in_specs=[row, vec, row],
        out_specs=[row, row, vec, pl.BlockSpec((1, LANE), lambda i: (0, 0))],
        out_shape=[jax.ShapeDtypeStruct((t, d), F32), jax.ShapeDtypeStruct((t, d), BF16),
                   jax.ShapeDtypeStruct((1, d), F32), jax.ShapeDtypeStruct((1, LANE), F32)],
        compiler_params=_params(("arbitrary",)),
    )(x2, g, target)


def _s5_discretize(a_re, a_im, ldt):
    lam_re = jnp.minimum(a_re, S5_MAX_RE)
    lam_im = a_im
    dt = jnp.exp(ldt)
    mag = jnp.exp(lam_re * dt)
    abar_re = mag * jnp.cos(lam_im * dt)
    abar_im = mag * jnp.sin(lam_im * dt)
    den = lam_re * lam_re + lam_im * lam_im
    nr = abar_re - 1.0
    ni = abar_im
    coef_re = (nr * lam_re + ni * lam_im) / den
    coef_im = (ni * lam_re - nr * lam_im) / den
    return abar_re, abar_im, coef_re, coef_im


def _s5_param_fwd(a_re, a_im, ldt):
    def body(ar_ref, ai_ref, l_ref, o0, o1, o2, o3):
        outs = _s5_discretize(ar_ref[...], ai_ref[...], l_ref[...])
        for o, v in zip((o0, o1, o2, o3), outs):
            o[...] = v

    sh = jax.ShapeDtypeStruct(a_re.shape, F32)
    return pl.pallas_call(body, name="s5_param_fwd", out_shape=[sh, sh, sh, sh], compiler_params=_params())(a_re, a_im, ldt)


def _s5_param_bwd(a_re, a_im, ldt, cts):
    def body(ar_ref, ai_ref, l_ref, c0, c1, c2, c3, g0, g1, g2):
        _, vjp = jax.vjp(_s5_discretize, ar_ref[...], ai_ref[...], l_ref[...])
        ga, gb, gl = vjp((c0[...], c1[...], c2[...], c3[...]))
        g0[...] = ga
        g1[...] = gb
        g2[...] = gl

    sh = jax.ShapeDtypeStruct(a_re.shape, F32)
    return pl.pallas_call(body, name="s5_param_bwd", out_shape=[sh, sh, jax.ShapeDtypeStruct(ldt.shape, F32)],
                          compiler_params=_params())(a_re, a_im, ldt, *cts)


def _cmul(ar, ai, br, bi):
    return ar * br - ai * bi, ar * bi + ai * br


S5_TC = 128
S5_TILE = S5_SUPER * SUBLANE
S5_HALF = S5_TILE // 2


def _s5_to_tile(re, im):
    f = lambda a: a.reshape(S5_SUPER, S5_LANES // LANE, LANE).transpose(1, 0, 2).reshape(S5_HALF, LANE)
    return jnp.concatenate([f(re), f(im)], axis=0)


def _s5_from_tile(tile):
    f = lambda a: a.reshape(S5_LANES // LANE, S5_SUPER, LANE).transpose(1, 0, 2).reshape(S5_GROUPS, S5_STATE)
    return f(tile[0:S5_HALF]), f(tile[S5_HALF:])


RE = slice(0, S5_HALF)
IM = slice(S5_HALF, S5_TILE)


def _s5_scatter_rows(buf, rows, first_tile=0):
    tc = rows[0].shape[0]
    for j in range(SUBLANE):
        stacked = jnp.stack([r[:, j * LANE:(j + 1) * LANE] for r in rows], axis=0)
        buf[first_tile:first_tile + tc, j * SUBLANE:(j + 1) * SUBLANE, :] = jnp.swapaxes(stacked, 0, 1)


def _s5_gather_rows(buf, tc, first_tile=0):
    per_j = [jnp.swapaxes(buf[first_tile:first_tile + tc, j * SUBLANE:(j + 1) * SUBLANE, :], 0, 1)
             for j in range(SUBLANE)]
    return [jnp.concatenate([per_j[j][k] for j in range(SUBLANE)], axis=1) for k in range(S5_SUPER)]


def _s5_fwd(proj, bsg, ccat, d_row, abar_t, coef_t):
    t = proj.shape[0]
    tc = min(t, S5_TC)
    n_chunks = t // tc

    def body(u_ref, b_ref, c_ref, d_ref, a_ref, cf_ref, y_ref, sb_ref, x, car):
        @pl.when(pl.program_id(0) == 0)
        def _():
            car[...] = jnp.zeros_like(car)

        sb_ref[...] = car[...]
        u = u_ref[...]
        _s5_scatter_rows(x, [_dot(u[:, k * LANE:(k + 1) * LANE].astype(BF16), b_ref[k]) for k in range(S5_SUPER)])
        ar, ai = a_ref[RE, :], a_ref[IM, :]
        cr, ci = cf_ref[RE, :], cf_ref[IM, :]

        def step(i, carry):
            sr, si = carry
            xr, xi = _cmul(cr, ci, x[i, RE, :], x[i, IM, :])
            sr, si = ar * sr - ai * si + xr, ar * si + ai * sr + xi
            x[i, RE, :] = sr
            x[i, IM, :] = si
            return sr, si

        sr, si = lax.fori_loop(0, tc, step, (car[RE, :], car[IM, :]), unroll=8)
        car[RE, :] = sr
        car[IM, :] = si
        for k, s_k in enumerate(_s5_gather_rows(x, tc)):
            cols = slice(k * LANE, (k + 1) * LANE)
            y_ref[:, cols] = _dot(s_k.astype(BF16), c_ref[k]) + d_ref[:, cols] * u[:, cols]

    full = lambda shape: pl.BlockSpec(shape, lambda c: (0,) * len(shape))
    return pl.pallas_call(
        body, name="s5_fwd", grid=(n_chunks,),
        in_specs=[pl.BlockSpec((tc, S5_WIDTH), lambda c: (c, 0)), full(bsg.shape), full(ccat.shape), full(d_row.shape),
                  full(abar_t.shape), full(coef_t.shape)],
        out_specs=[pl.BlockSpec((tc, S5_WIDTH), lambda c: (c, 0)), pl.BlockSpec((None, S5_TILE, LANE), lambda c: (c, 0, 0))],
        out_shape=[jax.ShapeDtypeStruct((t, S5_WIDTH), F32), jax.ShapeDtypeStruct((n_chunks, S5_TILE, LANE), F32)],
        scratch_shapes=[pltpu.VMEM((tc, S5_TILE, LANE), F32), pltpu.VMEM((S5_TILE, LANE), F32)],
        compiler_params=_params(("arbitrary",)),
    )(proj, bsg, ccat, d_row, abar_t, coef_t)


def _s5_bwd(proj, dy, sb, bsg, ccat, d_row, abar_t, coef_t):
    t = proj.shape[0]
    tc = min(t, S5_TC)
    n_chunks = t // tc
    last = n_chunks - 1

    def body(u_ref, dy_ref, sb_ref, b_ref, c_ref, d_ref, a_ref, cf_ref,
             du_ref, gb_ref, gc_ref, gd_ref, ga_ref, gcf_ref, xb, xs, xg, gcar, acc):
        @pl.when(pl.program_id(0) == 0)
        def _():
            gcar[...] = jnp.zeros_like(gcar)
            acc[...] = jnp.zeros_like(acc)
            gb_ref[...] = jnp.zeros_like(gb_ref)
            gc_ref[...] = jnp.zeros_like(gc_ref)
            gd_ref[...] = jnp.zeros_like(gd_ref)

        u = u_ref[...]
        dyv = dy_ref[...]
        u16, dy16 = u.astype(BF16), dyv.astype(BF16)
        subs = [slice(k * LANE, (k + 1) * LANE) for k in range(S5_SUPER)]
        _s5_scatter_rows(xb, [_dot(u16[:, c], b_ref[k]) for k, c in enumerate(subs)])
        _s5_scatter_rows(xg, [_dot_nt(dy16[:, c], c_ref[k]) for k, c in enumerate(subs)])
        ar, ai = a_ref[RE, :], a_ref[IM, :]
        cr, ci = cf_ref[RE, :], cf_ref[IM, :]

        xs[0] = sb_ref[...]

        def fstep(i, carry):
            sr, si = carry
            xr, xi = _cmul(cr, ci, xb[i, RE, :], xb[i, IM, :])
            sr, si = ar * sr - ai * si + xr, ar * si + ai * sr + xi
            xs[i + 1, RE, :] = sr
            xs[i + 1, IM, :] = si
            return sr, si

        lax.fori_loop(0, tc, fstep, (sb_ref[RE, :], sb_ref[IM, :]), unroll=8)

        def rstep(n, carry):
            gr, gi, a0, a1, a2, a3 = carry
            i = tc - 1 - n
            xr = xg[i, RE, :] + ar * gr + ai * gi
            xi = xg[i, IM, :] + ar * gi - ai * gr
            pr, pi = xs[i, RE, :], xs[i, IM, :]
            br, bi = xb[i, RE, :], xb[i, IM, :]
            a0 = a0 + pr * xr + pi * xi
            a1 = a1 + pr * xi - pi * xr
            a2 = a2 + br * xr + bi * xi
            a3 = a3 + br * xi - bi * xr
            xg[i, RE, :] = cr * xr + ci * xi
            xg[i, IM, :] = cr * xi - ci * xr
            return xr, xi, a0, a1, a2, a3

        init = (gcar[RE, :], gcar[IM, :], acc[0], acc[1], acc[2], acc[3])
        gr, gi, a0, a1, a2, a3 = lax.fori_loop(0, tc, rstep, init, unroll=4)
        gcar[RE, :] = gr
        gcar[IM, :] = gi
        for idx, a in enumerate((a0, a1, a2, a3)):
            acc[idx] = a
        ga_ref[RE, :] = a0
        ga_ref[IM, :] = a1
        gcf_ref[RE, :] = a2
        gcf_ref[IM, :] = a3

        g_rows = _s5_gather_rows(xg, tc)
        s_rows = _s5_gather_rows(xs, tc, first_tile=1)
        for k in range(S5_SUPER):
            cols = subs[k]
            g16 = g_rows[k].astype(BF16)
            s16 = s_rows[k].astype(BF16)
            gb_ref[k] += _dot_tn(u16[:, cols], g16)
            gc_ref[k] += _dot_tn(s16, dy16[:, cols])
            du_ref[:, cols] = (_dot_nt(g16, b_ref[k]) + d_ref[:, cols] * dyv[:, cols]).astype(BF16)
        gd_ref[...] += jnp.sum(dyv * u, axis=0, keepdims=True)

    full = lambda shape: pl.BlockSpec(shape, lambda c: (0,) * len(shape))
    rows = pl.BlockSpec((tc, S5_WIDTH), lambda c: (last - c, 0))
    tile = (S5_TILE, LANE)
    return pl.pallas_call(
        body, name="s5_bwd", grid=(n_chunks,),
        in_specs=[rows, rows, pl.BlockSpec((None, S5_TILE, LANE), lambda c: (last - c, 0, 0)),
                  full(bsg.shape), full(ccat.shape), full(d_row.shape), full(abar_t.shape), full(coef_t.shape)],
        out_specs=[rows, full(bsg.shape), full(ccat.shape), full(d_row.shape), full(tile), full(tile)],
        out_shape=[jax.ShapeDtypeStruct((t, S5_WIDTH), BF16), jax.ShapeDtypeStruct(bsg.shape, F32),
                   jax.ShapeDtypeStruct(ccat.shape, F32), jax.ShapeDtypeStruct(d_row.shape, F32),
                   jax.ShapeDtypeStruct(tile, F32), jax.ShapeDtypeStruct(tile, F32)],
        scratch_shapes=[pltpu.VMEM((tc, S5_TILE, LANE), F32), pltpu.VMEM((tc + 1, S5_TILE, LANE), F32),
                        pltpu.VMEM((tc, S5_TILE, LANE), F32), pltpu.VMEM(tile, F32),
                        pltpu.VMEM((4, S5_HALF, LANE), F32)],
        compiler_params=_params(("arbitrary",)),
    )(proj, dy, sb, bsg, ccat, d_row, abar_t, coef_t)


def _gelu_fwd(y, name="s5_gelu"):
    t, w = y.shape
    tr = _pick(t, 512, SUBLANE)

    def body(y_ref, z_ref):
        z_ref[...] = _gelu_and_grad(y_ref[...])[0].astype(BF16)

    row = pl.BlockSpec((tr, w), lambda i: (i, 0))
    return pl.pallas_call(body, name=name, grid=(t // tr,), in_specs=[row], out_specs=row,
                          out_shape=jax.ShapeDtypeStruct((t, w), BF16), compiler_params=_params(("parallel",)))(y)


def _glu_fwd(y, gl, b, name="s5_glu"):
    t, w = y.shape
    tr = _pick(t, 512, SUBLANE)

    def body(y_ref, gl_ref, b_ref, z2_ref):
        z = _gelu_and_grad(y_ref[...])[0]
        z2_ref[...] = (z * _sigmoid(gl_ref[...] + b_ref[...])).astype(BF16)

    row = pl.BlockSpec((tr, w), lambda i: (i, 0))
    return pl.pallas_call(body, name=name, grid=(t // tr,),
                          in_specs=[row, row, pl.BlockSpec((1, w), lambda i: (0, 0))], out_specs=row,
                          out_shape=jax.ShapeDtypeStruct((t, w), BF16), compiler_params=_params(("parallel",)))(y, gl, b)


def _glu_bwd(y, gl, b, dz2, name="s5_glu_bwd", after=None):
    t, w = y.shape
    tr = _pick(t, 512, SUBLANE)

    def body(y_ref, gl_ref, b_ref, dz2_ref, dgl_ref, dza_ref, db_ref):
        @pl.when(pl.program_id(0) == 0)
        def _():
            db_ref[...] = jnp.zeros_like(db_ref)

        z = _gelu_and_grad(y_ref[...])[0]
        s = _sigmoid(gl_ref[...] + b_ref[...])
        dz2v = dz2_ref[...]
        dgl = dz2v * z * s * (1.0 - s)
        dgl_ref[...] = dgl.astype(BF16)
        dza_ref[...] = dz2v * s
        db_ref[...] += jnp.sum(dgl, axis=0, keepdims=True)

    row = pl.BlockSpec((tr, w), lambda i: (i, 0))
    vec = pl.BlockSpec((1, w), lambda i: (0, 0))
    body, in_specs, args = _ordered(body, [row, row, vec, row], [y, gl, b, dz2], after)
    return pl.pallas_call(body, name=name, grid=(t // tr,), in_specs=in_specs, out_specs=[row, row, vec],
                          out_shape=[jax.ShapeDtypeStruct((t, w), BF16), jax.ShapeDtypeStruct((t, w), F32),
                                     jax.ShapeDtypeStruct((1, w), F32)],
                          compiler_params=_params(("arbitrary",)))(*args)


def _gelu_bwd(y, dza, dzb, name="s5_gelu_bwd", after=None):
    t, w = y.shape
    tr = _pick(t, 512, SUBLANE)

    def body(y_ref, a_ref, b_ref, dy_ref):
        dy_ref[...] = (a_ref[...] + b_ref[...]) * _gelu_and_grad(y_ref[...])[1]

    row = pl.BlockSpec((tr, w), lambda i: (i, 0))
    body, in_specs, args = _ordered(body, [row, row, row], [y, dza, dzb], after)
    return pl.pallas_call(body, name=name, grid=(t // tr,), in_specs=in_specs, out_specs=row,
                          out_shape=jax.ShapeDtypeStruct((t, w), F32), compiler_params=_params(("parallel",)))(*args)


def _tri_dot(tri16, x):
    hi = x.astype(BF16)
    lo = (x - hi.astype(F32)).astype(BF16)
    return _dot(tri16, hi) + _dot(tri16, lo)


def _hgrn_pre(q_in, z, lg):
    lb = _sigmoid(lg[0:1, :] - lg[1:2, :])
    qs, dqs = _silu_and_grad(q_in)
    sz = _sigmoid(z)
    f = lb + (1.0 - lb) * sz
    k = (1.0 - lb) * (1.0 - sz)
    c = HGRN_CHUNK
    r = lax.broadcasted_iota(jnp.int32, (c, c), 0)
    s = lax.broadcasted_iota(jnp.int32, (c, c), 1)
    causal = r >= s
    b = _tri_dot(jnp.where(causal, 1.0, 0.0).astype(BF16), jnp.log(f))
    b_end = b[c - 1:c, :]
    b_mid = b[c // 2 - 1:c // 2, :]
    e_q, e_k, e_0, e_c = jnp.exp(b - b_mid), jnp.exp(b_mid - b), jnp.exp(b), jnp.exp(b_end - b)
    return dict(lb=lb, qs=qs, dqs=dqs, sz=sz, f=f, k=k, causal=causal, b_end=b_end,
                e_q=e_q, e_k=e_k, e_0=e_0, e_c=e_c,
                qt=qs * e_q, kt=k * e_k, q0=qs * e_0, kc=k * e_c)


def _hgrn_fwd(proj, logits, ng):
    t = proj.shape[0]
    c, dh = HGRN_CHUNK, HGRN_DH
    n_chunks = t // c
    subs = HGRN_SUBS if n_chunks % HGRN_SUBS == 0 else 1

    def head(h, sub, q_ref, z_ref, v_ref, g_ref, lg_ref, ng_ref, o_ref, oh_ref, s0_ref, st):
        sl = slice(h * dh, (h + 1) * dh)
        rs = slice(sub * c, (sub + 1) * c)
        s0 = st[h]
        s0_ref[h, sub] = s0
        p = _hgrn_pre(q_ref[rs, sl], z_ref[rs, sl], lg_ref[:, sl])
        v16 = v_ref[rs, sl].astype(BF16)
        a = jnp.where(p["causal"], _dot_nt(p["qt"].astype(BF16), p["kt"].astype(BF16)), 0.0)
        o = _dot_nt(p["q0"].astype(BF16), s0.astype(BF16)) + _dot(a.astype(BF16), v16)
        st[h] = jnp.exp(p["b_end"]) * s0 + _dot_tn(v16, p["kc"].astype(BF16))
        o_ref[rs, sl] = o
        rn = lax.rsqrt(jnp.mean(o * o, axis=-1, keepdims=True) + RMS_EPS)
        oh_ref[rs, sl] = (o * rn * ng_ref[:, sl] * _silu_and_grad(g_ref[rs, sl])[0]).astype(BF16)

    def body(*refs):
        st = refs[-1]

        @pl.when(pl.program_id(0) == 0)
        def _():
            st[...] = jnp.zeros_like(st)

        for sub in range(subs):
            for h in range(HGRN_HEADS):
                head(h, sub, *refs)

    def wide(off):
        return pl.BlockSpec((subs * c, HGRN_WIDTH), lambda i: (i, off))

    return pl.pallas_call(
        body, name="hgrn_fwd", grid=(n_chunks // subs,),
        in_specs=[wide(1), wide(2), wide(3), wide(4),
                  pl.BlockSpec((2, HGRN_WIDTH), lambda i: (0, 0)), pl.BlockSpec((1, HGRN_WIDTH), lambda i: (0, 0))],
        out_specs=[wide(0), wide(0), pl.BlockSpec((HGRN_HEADS, subs, dh, dh), lambda i: (0, i, 0, 0))],
        out_shape=[jax.ShapeDtypeStruct((t, HGRN_WIDTH), F32), jax.ShapeDtypeStruct((t, HGRN_WIDTH), BF16),
                   jax.ShapeDtypeStruct((HGRN_HEADS, n_chunks, dh, dh), F32)],
        scratch_shapes=[pltpu.VMEM((HGRN_HEADS, dh, dh), F32)],
        compiler_params=_params(("arbitrary",)),
    )(proj, proj, proj, proj, logits, ng)


def _hgrn_bwd(proj, o_raw, s0s, doh, logits, ng):
    t = proj.shape[0]
    c, dh = HGRN_CHUNK, HGRN_DH
    n_chunks = t // c
    subs = HGRN_SUBS if n_chunks % HGRN_SUBS == 0 else 1
    last = n_chunks // subs - 1

    def head(h, sub, q_ref, z_ref, v_ref, g_ref, o_ref, s0_ref, doh_ref, lg_ref, ng_ref,
             dq_ref, dz_ref, dv_ref, dg_ref, dng_ref, dlb_ref, dst):
        sl = slice(h * dh, (h + 1) * dh)
        rs = slice(sub * c, (sub + 1) * c)
        p = _hgrn_pre(q_ref[rs, sl], z_ref[rs, sl], lg_ref[:, sl])
        v = v_ref[rs, sl]
        v16 = v.astype(BF16)
        s0 = s0_ref[h, sub]
        ds_end = dst[h]
        ds16 = ds_end.astype(BF16)
        ngv = ng_ref[:, sl]

        o = o_ref[rs, sl]
        dohv = doh_ref[rs, sl]
        sg, dsg = _silu_and_grad(g_ref[rs, sl])
        rn = lax.rsqrt(jnp.mean(o * o, axis=-1, keepdims=True) + RMS_EPS)
        oh = o * rn
        dg_ref[rs, sl] = (dohv * oh * ngv * dsg).astype(BF16)
        don = dohv * sg
        dng_ref[:, sl] += jnp.sum(don * oh, axis=0, keepdims=True)
        doh_n = don * ngv
        do = rn * (doh_n - oh * jnp.mean(doh_n * oh, axis=-1, keepdims=True))
        do16 = do.astype(BF16)

        qt16, kt16, q016, kc16 = (p[n].astype(BF16) for n in ("qt", "kt", "q0", "kc"))
        a = jnp.where(p["causal"], _dot_nt(qt16, kt16), 0.0)
        da = jnp.where(p["causal"], _dot_nt(do16, v16), 0.0)
        da16 = da.astype(BF16)
        dqt = _dot(da16, kt16)
        dq0 = _dot(do16, s0.astype(BF16))
        dkt = _dot_tn(da16, qt16)
        dkc = _dot(v16, ds16)
        dv_ref[rs, sl] = (_dot_tn(a.astype(BF16), do16) + _dot_nt(kc16, ds16)).astype(BF16)
        lam_end = jnp.exp(p["b_end"])
        dst[h] = lam_end * ds_end + _dot_tn(do16, q016)

        qt, kt, q0, kc = (a.astype(F32) for a in (qt16, kt16, q016, kc16))
        db = dqt * qt + dq0 * q0 - dkt * kt - dkc * kc
        db_end = (jnp.sum(dkc * kc, axis=0, keepdims=True)
                  + jnp.sum(ds_end * s0, axis=0, keepdims=True) * lam_end)
        rowi = lax.broadcasted_iota(jnp.int32, (c, dh), 0)
        db = db + jnp.where(rowi == c - 1, db_end, 0.0)
        r = lax.broadcasted_iota(jnp.int32, (c, c), 0)
        s = lax.broadcasted_iota(jnp.int32, (c, c), 1)
        dlf = _tri_dot(jnp.where(s >= r, 1.0, 0.0).astype(BF16), db)

        dqs = dqt * p["e_q"] + dq0 * p["e_0"]
        dq_ref[rs, sl] = (dqs * p["dqs"]).astype(BF16)
        dk = dkt * p["e_k"] + dkc * p["e_c"]
        sz, lb = p["sz"], p["lb"]
        common = dlf / p["f"] - dk
        dz_ref[rs, sl] = ((1.0 - lb) * sz * (1.0 - sz) * common).astype(BF16)
        dlb_ref[:, sl] += jnp.sum((1.0 - sz) * common, axis=0, keepdims=True)

    def body(*refs):
        dng_ref, dlb_ref, dst = refs[-3:]

        @pl.when(pl.program_id(0) == 0)
        def _():
            dst[...] = jnp.zeros_like(dst)
            dng_ref[...] = jnp.zeros_like(dng_ref)
            dlb_ref[...] = jnp.zeros_like(dlb_ref)

        for sub in reversed(range(subs)):
            for h in range(HGRN_HEADS):
                head(h, sub, *refs)

    def wide(off):
        return pl.BlockSpec((subs * c, HGRN_WIDTH), lambda i: (last - i, off))

    vec = pl.BlockSpec((1, HGRN_WIDTH), lambda i: (0, 0))
    act = jax.ShapeDtypeStruct((t, HGRN_WIDTH), BF16)
    vsh = jax.ShapeDtypeStruct((1, HGRN_WIDTH), F32)
    return pl.pallas_call(
        body, name="hgrn_bwd", grid=(n_chunks // subs,),
        in_specs=[wide(1), wide(2), wide(3), wide(4), wide(0),
                  pl.BlockSpec((HGRN_HEADS, subs, dh, dh), lambda i: (0, last - i, 0, 0)),
                  wide(0), pl.BlockSpec((2, HGRN_WIDTH), lambda i: (0, 0)), vec],
        out_specs=[wide(0), wide(0), wide(0), wide(0), vec, vec],
        out_shape=[act, act, act, act, vsh, vsh],
        scratch_shapes=[pltpu.VMEM((HGRN_HEADS, dh, dh), F32)],
        compiler_params=_params(("arbitrary",)),
    )(proj, proj, proj, proj, o_raw, s0s, doh, logits, ng)


def _lb_bwd(logits, dlb):
    def body(lg_ref, d_ref, o_ref):
        lg = lg_ref[...]
        lb = _sigmoid(lg[0:1, :] - lg[1:2, :])
        g = d_ref[...] * lb * (1.0 - lb)
        o_ref[0:1, :] = g
        o_ref[1:2, :] = -g

    return pl.pallas_call(body, name="hgrn_lb_bwd", out_shape=jax.ShapeDtypeStruct(logits.shape, F32),
                          compiler_params=_params())(logits, dlb)


MERGE_TC = 1024
GS_BLOCK = (S5_WIDTH + 4 * HGRN_WIDTH) // MERGE_TC
GH_BLOCK = GS_BLOCK + D_MODEL // MERGE_TC


def _merge_fwd(proj, ys, yh):
    t = proj.shape[0]
    tr = _pick(t, 512, SUBLANE)

    def body(gs_ref, gh_ref, ys_ref, yh_ref, m_ref):
        m_ref[...] = (_sigmoid(gs_ref[...]) * ys_ref[...] + _sigmoid(gh_ref[...]) * yh_ref[...]).astype(BF16)

    blk = pl.BlockSpec((tr, MERGE_TC), lambda i, j: (i, j))
    return pl.pallas_call(
        body, name="merge_fwd", grid=(t // tr, D_MODEL // MERGE_TC),
        in_specs=[pl.BlockSpec((tr, MERGE_TC), lambda i, j: (i, GS_BLOCK + j)),
                  pl.BlockSpec((tr, MERGE_TC), lambda i, j: (i, GH_BLOCK + j)), blk, blk],
        out_specs=blk, out_shape=jax.ShapeDtypeStruct((t, D_MODEL), BF16),
        compiler_params=_params(("parallel", "parallel")),
    )(proj, proj, ys, yh)


def _merge_bwd(proj, ys, yh, dm, after=None):
    t = proj.shape[0]
    tr = _pick(t, 512, SUBLANE)

    def body(gs_ref, gh_ref, ys_ref, yh_ref, dm_ref, dys_ref, dyh_ref, dgs_ref, dgh_ref):
        dmv = dm_ref[...]
        ss, sh = _sigmoid(gs_ref[...]), _sigmoid(gh_ref[...])
        dys_ref[...] = (dmv * ss).astype(BF16)
        dyh_ref[...] = (dmv * sh).astype(BF16)
        dgs_ref[...] = (dmv * ys_ref[...] * ss * (1.0 - ss)).astype(BF16)
        dgh_ref[...] = (dmv * yh_ref[...] * sh * (1.0 - sh)).astype(BF16)

    blk = pl.BlockSpec((tr, MERGE_TC), lambda i, j: (i, j))
    sh16 = jax.ShapeDtypeStruct((t, D_MODEL), BF16)
    in_specs = [pl.BlockSpec((tr, MERGE_TC), lambda i, j: (i, GS_BLOCK + j)),
                pl.BlockSpec((tr, MERGE_TC), lambda i, j: (i, GH_BLOCK + j)), blk, blk, blk]
    body, in_specs, args = _ordered(body, in_specs, [proj, proj, ys, yh, dm], after)
    return pl.pallas_call(
        body, name="merge_bwd", grid=(t // tr, D_MODEL // MERGE_TC),
        in_specs=in_specs,
        out_specs=[blk, blk, blk, blk], out_shape=[sh16, sh16, sh16, sh16],
        compiler_params=_params(("parallel", "parallel")),
    )(*args)


FFN_TC = 128
FFN_ROWS = 128
HALO = SUBLANE


def _pad_rows(dst, src_ref):
    t, c = src_ref.shape
    dst[0:HALO, :] = jnp.zeros((HALO, c), F32)
    dst[HALO:HALO + t, :] = src_ref[...]
    dst[HALO + t:HALO + t + HALO, :] = jnp.zeros((HALO, c), F32)


def _conv3(padded, w, b, r0, nrows):
    x0 = padded[HALO + r0:HALO + r0 + nrows, :]
    x1 = padded[HALO + r0 - 1:HALO + r0 - 1 + nrows, :]
    x2 = padded[HALO + r0 - 2:HALO + r0 - 2 + nrows, :]
    return b + w[0:1, :] * x2 + w[1:2, :] * x1 + w[2:3, :] * x0, (x0, x1, x2)


def _ffn_act_fwd(up, cw, cb):
    t = up.shape[0]
    rows = _pick(t, FFN_ROWS, SUBLANE)
    nvb = D_FF // FFN_TC

    def body(ug_ref, uv_ref, wg_ref, wv_ref, bg_ref, bv_ref, act_ref, pg, pv):
        wg, wv, bg, bv = wg_ref[...], wv_ref[...], bg_ref[...], bv_ref[...]
        _pad_rows(pg, ug_ref)
        _pad_rows(pv, uv_ref)
        for r0 in range(0, t, rows):
            cg, _ = _conv3(pg, wg, bg, r0, rows)
            cv, _ = _conv3(pv, wv, bv, r0, rows)
            act_ref[r0:r0 + rows, :] = (_silu_and_grad(cg)[0] * cv).astype(BF16)

    def colblk(nrow, off):
        return pl.BlockSpec((nrow, FFN_TC), lambda j: (0, off + j))

    return pl.pallas_call(
        body, name="ffn_act_fwd", grid=(nvb,),
        in_specs=[colblk(t, 0), colblk(t, nvb), colblk(3, 0), colblk(3, nvb), colblk(1, 0), colblk(1, nvb)],
        out_specs=colblk(t, 0), out_shape=jax.ShapeDtypeStruct((t, D_FF), BF16),
        scratch_shapes=[pltpu.VMEM((t + 2 * HALO, FFN_TC), F32), pltpu.VMEM((t + 2 * HALO, FFN_TC), F32)],
        compiler_params=_params(("parallel",)),
    )(up, up, cw, cw, cb, cb)


def _ffn_act_bwd(up, dact, cw, cb, after=None):
    t = up.shape[0]
    rows = _pick(t, FFN_ROWS, SUBLANE)
    nvb = D_FF // FFN_TC

    def body(ug_ref, uv_ref, da_ref, wg_ref, wv_ref, bg_ref, bv_ref,
             dug_ref, duv_ref, dwg_ref, dwv_ref, dbg_ref, dbv_ref, pg, pv, dcs):
        wg, wv, bg, bv = wg_ref[...], wv_ref[...], bg_ref[...], bv_ref[...]
        _pad_rows(pg, ug_ref)
        _pad_rows(pv, uv_ref)
        ext = rows + HALO
        acc_g = [jnp.zeros((1, FFN_TC), F32) for _ in range(4)]
        acc_v = [jnp.zeros((1, FFN_TC), F32) for _ in range(4)]
        for r0 in range(0, t, rows):
            cg, xg = _conv3(pg, wg, bg, r0, ext)
            cv, xv = _conv3(pv, wv, bv, r0, ext)
            if r0 + ext <= t:
                dav = da_ref[r0:r0 + ext, :]
            else:
                dav = jnp.concatenate([da_ref[r0:t, :], jnp.zeros((HALO, FFN_TC), F32)], axis=0)
            sg, dsg = _silu_and_grad(cg)
            for h, (dconv, xs, w, acc, out) in enumerate(((dav * cv * dsg, xg, wg, acc_g, dug_ref),
                                                           (dav * sg, xv, wv, acc_v, duv_ref))):
                dcs[h] = dconv
                d0 = dconv[0:rows, :]
                d1 = dcs[h, 1:rows + 1, :]
                d2 = dcs[h, 2:rows + 2, :]
                out[r0:r0 + rows, :] = (w[2:3, :] * d0 + w[1:2, :] * d1 + w[0:1, :] * d2).astype(BF16)
                x0, x1, x2 = xs
                acc[0] = acc[0] + jnp.sum(d0 * x2[0:rows, :], axis=0, keepdims=True)
                acc[1] = acc[1] + jnp.sum(d0 * x1[0:rows, :], axis=0, keepdims=True)
                acc[2] = acc[2] + jnp.sum(d0 * x0[0:rows, :], axis=0, keepdims=True)
                acc[3] = acc[3] + jnp.sum(d0, axis=0, keepdims=True)
        for acc, dw_ref, db_ref in ((acc_g, dwg_ref, dbg_ref), (acc_v, dwv_ref, dbv_ref)):
            dw_ref[0:1, :] = acc[0]
            dw_ref[1:2, :] = acc[1]
            dw_ref[2:3, :] = acc[2]
            db_ref[...] = acc[3]

    def colblk(nrow, off):
        return pl.BlockSpec((nrow, FFN_TC), lambda j: (0, off + j))

    in_specs = [colblk(t, 0), colblk(t, nvb), colblk(t, 0), colblk(3, 0), colblk(3, nvb), colblk(1, 0), colblk(1, nvb)]
    body, in_specs, args = _ordered(body, in_specs, [up, up, dact, cw, cw, cb, cb], after)
    return pl.pallas_call(
        body, name="ffn_act_bwd", grid=(nvb,),
        in_specs=in_specs,
        out_specs=[colblk(t, 0), colblk(t, 0), colblk(3, 0), colblk(3, 0), colblk(1, 0), colblk(1, 0)],
        out_shape=[jax.ShapeDtypeStruct((t, D_FF), BF16), jax.ShapeDtypeStruct((t, D_FF), BF16),
                   jax.ShapeDtypeStruct((3, D_FF), F32), jax.ShapeDtypeStruct((3, D_FF), F32),
                   jax.ShapeDtypeStruct((1, D_FF), F32), jax.ShapeDtypeStruct((1, D_FF), F32)],
        scratch_shapes=[pltpu.VMEM((t + 2 * HALO, FFN_TC), F32), pltpu.VMEM((t + 2 * HALO, FFN_TC), F32),
                        pltpu.VMEM((2, rows + HALO, FFN_TC), F32)],
        compiler_params=_params(("parallel",)),
    )(*args)


def _all_gather(shards, name):
    nw = len(shards)

    def body(*refs):
        x_refs, out_refs = refs[:nw], refs[nw:2 * nw]
        send_sems, recv_sems, local_sems = refs[2 * nw:]
        x, y, c = lax.axis_index("x"), lax.axis_index("y"), lax.axis_index("c")
        me, sibling = (x, y, c), (x, y, 1 - c)
        chips = [(1 - x, y), (x, 1 - y), (1 - x, 1 - y)]

        def copy(w, k, block, to, src=None):
            slot = out_refs[w].at[4 * block[0] + 2 * block[1] + block[2]]
            return pltpu.make_async_remote_copy(
                src_ref=slot if src is None else src, dst_ref=slot,
                send_sem=send_sems.at[w, k], recv_sem=recv_sems.at[w, k],
                device_id=to, device_id_type=MESH)

        mine, first, passed = [], [], []
        for w in range(nw):
            cp = pltpu.make_async_copy(x_refs[w], out_refs[w].at[4 * x + 2 * y + c], local_sems.at[w])
            cp.start()
            mine.append(cp)
            first.append(copy(w, 0, me, sibling, src=x_refs[w]))
            first += [copy(w, 1 + j, me, (*chip, c), src=x_refs[w]) for j, chip in enumerate(chips)]
        for cp in first:
            cp.start()
        for w in range(nw):
            for j, chip in enumerate(chips):
                copy(w, 1 + j, (*chip, c), me).wait_recv()
                fwd = copy(w, 4 + j, (*chip, c), sibling)
                fwd.start()
                passed.append(fwd)
        for w in range(nw):
            copy(w, 0, sibling, me).wait_recv()
            for j, chip in enumerate(chips):
                copy(w, 4 + j, (*chip, 1 - c), me).wait_recv()
        for cp in first + passed:
            cp.wait_send()
        for cp in mine:
            cp.wait()

    anyspec = pl.BlockSpec(memory_space=pl.ANY)
    return pl.pallas_call(
        body, name=name,
        in_specs=[anyspec] * nw, out_specs=[anyspec] * nw,
        out_shape=[jax.ShapeDtypeStruct((N_DEV,) + s.shape, s.dtype) for s in shards],
        scratch_shapes=[pltpu.SemaphoreType.DMA((nw, 7)), pltpu.SemaphoreType.DMA((nw, 7)),
                        pltpu.SemaphoreType.DMA((nw,))],
    )(*shards)


HBM_SPEC = pl.BlockSpec(memory_space=pltpu.HBM)
SEM_SPEC = pl.BlockSpec(memory_space=pltpu.SEMAPHORE)
ANY_SPEC = pl.BlockSpec(memory_space=pl.ANY)
DATAFLOW = pltpu.SideEffectType.DATAFLOW_SIDE_EFFECTING


def _my_index():
    return 4 * lax.axis_index("x") + 2 * lax.axis_index("y") + lax.axis_index("c")


def _peers():
    x, y, c = lax.axis_index("x"), lax.axis_index("y"), lax.axis_index("c")
    peers = []
    for k in range(1, N_DEV):
        px = 1 - x if k & 4 else x
        py = 1 - y if k & 2 else y
        pc = 1 - c if k & 1 else c
        peers.append((k, (px, py, pc), 4 * px + 2 * py + pc))
    return peers


def _split_copy(src_ref, land_ref, send_sems, recv_sems, w, k, peer, slot, scatter, outgoing):
    return pltpu.make_async_remote_copy(
        src_ref=src_ref.at[slot] if scatter else src_ref,
        dst_ref=land_ref.at[_my_index() if outgoing else slot],
        send_sem=send_sems.at[w * (N_DEV - 1) + k - 1], recv_sem=recv_sems.at[w * (N_DEV - 1) + k - 1],
        device_id=peer, device_id_type=MESH)


def _landing_zone(src, scatter):
    me = _my_index()
    own = lax.dynamic_index_in_dim(src, me, 0, keepdims=True) if scatter else src[None]
    shape = src.shape if scatter else (N_DEV,) + src.shape
    return lax.dynamic_update_slice_in_dim(lax.empty(shape, src.dtype), own, me, 0)


def _exchange_start(srcs, scatter, after, name, lands=None):
    nw = len(srcs)
    if lands is None:
        lands = [_landing_zone(s, scatter) for s in srcs]

    afters = [] if after is None else [after]

    def body(*refs):
        s_refs, l_refs = refs[:nw], refs[nw:2 * nw]
        send_sems, recv_sems = refs[2 * nw + len(afters)], refs[2 * nw + len(afters) + 1]
        token = refs[-1]
        for w in range(nw):
            for k, peer, slot in _peers():
                _split_copy(s_refs[w], l_refs[w], send_sems, recv_sems, w, k, peer, slot, scatter, True).start()
        token[...] = jnp.zeros_like(token)

    sems = pltpu.SemaphoreType.DMA((nw * (N_DEV - 1),))
    outs = pl.pallas_call(
        body, name=name,
        out_shape=(sems, sems, *[pltpu.HBM(a.shape, a.dtype) for a in (*srcs, *lands)],
                   jax.ShapeDtypeStruct((SUBLANE, LANE), F32)),
        in_specs=[HBM_SPEC] * (2 * nw) + [ANY_SPEC] * len(afters),
        out_specs=(SEM_SPEC, SEM_SPEC, *[HBM_SPEC] * (2 * nw), pl.BlockSpec(memory_space=pltpu.VMEM)),
        input_output_aliases={i: 2 + i for i in range(2 * nw)},
        compiler_params=pltpu.CompilerParams(has_side_effects=DATAFLOW),
    )(*[pltpu.with_memory_space_constraint(a, pltpu.HBM) for a in (*srcs, *lands)], *afters)
    return dict(sems=outs[:2], srcs=outs[2:2 + nw], lands=outs[2 + nw:2 + 2 * nw], token=outs[-1], scatter=scatter)


def _exchange_wait(handle, afters, name):
    srcs, lands, scatter = handle["srcs"], handle["lands"], handle["scatter"]
    nw = len(srcs)

    def body(*refs):
        s_refs, l_refs = refs[:nw], refs[nw:2 * nw]
        send_sems, recv_sems = refs[2 * nw], refs[2 * nw + 1]
        for w in range(nw):
            for k, peer, slot in _peers():
                cp = _split_copy(s_refs[w], l_refs[w], send_sems, recv_sems, w, k, peer, slot, scatter, False)
                cp.wait_send()
                cp.wait_recv()

    outs = pl.pallas_call(
        body, name=name,
        out_shape=tuple(pltpu.HBM(a.shape, a.dtype) for a in (*srcs, *lands)),
        in_specs=[HBM_SPEC] * (2 * nw) + [SEM_SPEC, SEM_SPEC] + [ANY_SPEC] * len(afters),
        out_specs=tuple([HBM_SPEC] * (2 * nw)),
        input_output_aliases={i: i for i in range(2 * nw)},
        compiler_params=pltpu.CompilerParams(has_side_effects=DATAFLOW),
    )(*srcs, *lands, *handle["sems"], *afters)
    return list(outs[nw:])


def _chips_and_sibling():
    x, y, c = lax.axis_index("x"), lax.axis_index("y"), lax.axis_index("c")
    return [(1 - x, y), (x, 1 - y), (1 - x, 1 - y)], (x, y, 1 - c), c


def _slot(px, py, pc):
    return 4 * px + 2 * py + pc


def _two_level_start(shards, name):
    nw = len(shards)
    lands = [_landing_zone(s, False) for s in shards]

    def body(*refs):
        s_refs, l_refs = refs[:nw], refs[nw:2 * nw]
        send_sems, recv_sems, token = refs[2 * nw], refs[2 * nw + 1], refs[-1]
        chips, sibling, c = _chips_and_sibling()
        for w in range(nw):
            for k, to in enumerate([sibling] + [(*chip, c) for chip in chips]):
                pltpu.make_async_remote_copy(
                    src_ref=s_refs[w], dst_ref=l_refs[w].at[_my_index()],
                    send_sem=send_sems.at[4 * w + k], recv_sem=recv_sems.at[4 * w + k],
                    device_id=to, device_id_type=MESH).start()
        token[...] = jnp.zeros_like(token)

    sems = pltpu.SemaphoreType.DMA((4 * nw,))
    outs = pl.pallas_call(
        body, name=name,
        out_shape=(sems, sems, *[pltpu.HBM(a.shape, a.dtype) for a in (*shards, *lands)],
                   jax.ShapeDtypeStruct((SUBLANE, LANE), F32)),
        in_specs=[HBM_SPEC] * (2 * nw),
        out_specs=(SEM_SPEC, SEM_SPEC, *[HBM_SPEC] * (2 * nw), pl.BlockSpec(memory_space=pltpu.VMEM)),
        input_output_aliases={i: 2 + i for i in range(2 * nw)},
        compiler_params=pltpu.CompilerParams(has_side_effects=DATAFLOW),
    )(*[pltpu.with_memory_space_constraint(a, pltpu.HBM) for a in (*shards, *lands)])
    return dict(sems=outs[:2], srcs=outs[2:2 + nw], lands=outs[2 + nw:2 + 2 * nw], token=outs[-1])


def _two_level_pass(handle, afters, name):
    srcs, lands = handle["srcs"], handle["lands"]
    nw = len(srcs)

    def body(*refs):
        s_refs, l_refs = refs[:nw], refs[nw:2 * nw]
        send_a, recv_a = refs[2 * nw], refs[2 * nw + 1]
        send_b, recv_b = refs[2 * nw + 2 + len(afters)], refs[2 * nw + 3 + len(afters)]
        chips, sibling, c = _chips_and_sibling()
        for w in range(nw):
            for j, chip in enumerate(chips):
                landed = l_refs[w].at[_slot(*chip, c)]
                pltpu.make_async_remote_copy(
                    src_ref=s_refs[w], dst_ref=landed, send_sem=send_a.at[4 * w + 1 + j], recv_sem=recv_a.at[4 * w + 1 + j],
                    device_id=(*chip, c), device_id_type=MESH).wait_recv()
                pltpu.make_async_remote_copy(
                    src_ref=landed, dst_ref=landed, send_sem=send_b.at[3 * w + j], recv_sem=recv_b.at[3 * w + j],
                    device_id=sibling, device_id_type=MESH).start()

    sems = pltpu.SemaphoreType.DMA((3 * nw,))
    outs = pl.pallas_call(
        body, name=name,
        out_shape=(sems, sems, *[pltpu.HBM(a.shape, a.dtype) for a in (*srcs, *lands)]),
        in_specs=[HBM_SPEC] * (2 * nw) + [SEM_SPEC, SEM_SPEC] + [ANY_SPEC] * len(afters),
        out_specs=(SEM_SPEC, SEM_SPEC, *[HBM_SPEC] * (2 * nw)),
        input_output_aliases={i: 2 + i for i in range(2 * nw)},
        compiler_params=pltpu.CompilerParams(has_side_effects=DATAFLOW),
    )(*srcs, *lands, *handle["sems"], *afters)
    return dict(sems=handle["sems"], sems_pass=outs[:2], srcs=outs[2:2 + nw], lands=outs[2 + nw:2 + 2 * nw])


def _two_level_wait(handle, name):
    srcs, lands = handle["srcs"], handle["lands"]
    nw = len(srcs)

    def body(*refs):
        s_refs, l_refs = refs[:nw], refs[nw:2 * nw]
        send_a, recv_a, send_b, recv_b = refs[2 * nw:2 * nw + 4]
        chips, sibling, c = _chips_and_sibling()
        x, y = sibling[0], sibling[1]
        for w in range(nw):
            first = pltpu.make_async_remote_copy(
                src_ref=s_refs[w], dst_ref=l_refs[w].at[_slot(x, y, 1 - c)], send_sem=send_a.at[4 * w],
                recv_sem=recv_a.at[4 * w], device_id=sibling, device_id_type=MESH)
            first.wait_send()
            first.wait_recv()
            for j, chip in enumerate(chips):
                pltpu.make_async_remote_copy(
                    src_ref=s_refs[w], dst_ref=l_refs[w].at[_slot(*chip, c)], send_sem=send_a.at[4 * w + 1 + j],
                    recv_sem=recv_a.at[4 * w + 1 + j], device_id=(*chip, c), device_id_type=MESH).wait_send()
                passed = pltpu.make_async_remote_copy(
                    src_ref=l_refs[w].at[_slot(*chip, c)], dst_ref=l_refs[w].at[_slot(*chip, 1 - c)],
                    send_sem=send_b.at[3 * w + j], recv_sem=recv_b.at[3 * w + j], device_id=sibling, device_id_type=MESH)
                passed.wait_send()
                passed.wait_recv()

    outs = pl.pallas_call(
        body, name=name,
        out_shape=tuple(pltpu.HBM(a.shape, a.dtype) for a in (*srcs, *lands)),
        in_specs=[HBM_SPEC] * (2 * nw) + [SEM_SPEC] * 4,
        out_specs=tuple([HBM_SPEC] * (2 * nw)),
        input_output_aliases={i: i for i in range(2 * nw)},
        compiler_params=pltpu.CompilerParams(has_side_effects=DATAFLOW),
    )(*srcs, *lands, *handle["sems"], *handle["sems_pass"])
    return list(outs[nw:])


def _adamw(w, g, m, v):
    m = ADAM_B1 * m + (1.0 - ADAM_B1) * g
    v = ADAM_B2 * v + (1.0 - ADAM_B2) * (g * g)
    m_hat = m / (1.0 - ADAM_B1 ** ADAM_STEP)
    v_hat = v / (1.0 - ADAM_B2 ** ADAM_STEP)
    delta = -ADAM_LR * (m_hat / (jnp.sqrt(v_hat) + ADAM_EPS) + ADAM_WD * w)
    return delta, m, v


def _sum_adam(parts, w, m, v, name):
    _, r, c = parts.shape
    tr = _pick(r, 256, 16)

    def body(p_ref, w_ref, m_ref, v_ref, g_ref, d_ref, mo_ref, vo_ref):
        g = p_ref[0].astype(F32)
        for s in range(1, N_DEV):
            g = g + p_ref[s].astype(F32)
        g_ref[...] = g
        d_ref[...], mo_ref[...], vo_ref[...] = _adamw(w_ref[...], g, m_ref[...], v_ref[...])

    row = pl.BlockSpec((tr, c), lambda i: (i, 0))
    sh = jax.ShapeDtypeStruct((r, c), F32)
    return pl.pallas_call(
        body, name=name, grid=(r // tr,),
        in_specs=[pl.BlockSpec((N_DEV, tr, c), lambda i: (0, i, 0)), row, row, row],
        out_specs=[row, row, row, row], out_shape=[sh, sh, sh, sh],
        compiler_params=_params(("parallel",)),
    )(parts, w, m, v)


def _sum_slots(parts, name):
    _, r, c = parts.shape
    tr = _pick(r, 512, SUBLANE)

    def body(p_ref, o_ref):
        g = p_ref[0]
        for s in range(1, N_DEV):
            g = g + p_ref[s]
        o_ref[...] = g

    return pl.pallas_call(
        body, name=name, grid=(r // tr,),
        in_specs=[pl.BlockSpec((N_DEV, tr, c), lambda i: (0, i, 0))],
        out_specs=pl.BlockSpec((tr, c), lambda i: (i, 0)), out_shape=jax.ShapeDtypeStruct((r, c), F32),
        compiler_params=_params(("parallel",)),
    )(parts)


def _adam_rows(g, w, m, v, name):
    r, c = g.shape
    tr = _pick(r, 512, SUBLANE)

    def body(g_ref, w_ref, m_ref, v_ref, d_ref, mo_ref, vo_ref):
        d_ref[...], mo_ref[...], vo_ref[...] = _adamw(w_ref[...], g_ref[...], m_ref[...], v_ref[...])

    row = pl.BlockSpec((tr, c), lambda i: (i, 0))
    sh = jax.ShapeDtypeStruct((r, c), F32)
    return pl.pallas_call(body, name=name, grid=(r // tr,), in_specs=[row] * 4, out_specs=[row] * 3,
                          out_shape=[sh, sh, sh], compiler_params=_params(("parallel",)))(g, w, m, v)


def _pack(arrays):
    flat = jnp.concatenate([a.reshape(-1).astype(F32) for a in arrays])
    pad = (-flat.shape[0]) % (SUBLANE * LANE)
    return jnp.pad(flat, (0, pad)).reshape(-1, LANE)


def _unpack(packed, shapes):
    flat = packed.reshape(-1)
    out, off = [], 0
    for s in shapes:
        n = math.prod(s)
        out.append(flat[off:off + n].reshape(s))
        off += n
    return out


def _block_diag(t):
    eye = jnp.eye(S5_SUPER, dtype=bool)
    bd = jnp.where(eye[None, :, None, :, None], t[:, :, :, None, :], 0.0)
    return bd.reshape(S5_SUPER, S5_SUPER * t.shape[2], S5_SUPER * t.shape[3])


def _diag_blocks(dense, a, b):
    x = dense.reshape(S5_SUPER, S5_SUPER, a, S5_SUPER, b)
    return jnp.moveaxis(jnp.diagonal(x, axis1=1, axis2=3), -1, 1)


def _s5_layouts(b_re, b_im, c_re, c_im, d):
    g2 = (S5_GROUPS // S5_SUPER, S5_SUPER)
    bt = lambda b: _block_diag(b.reshape(*g2, S5_STATE, S5_GROUP).transpose(0, 1, 3, 2))
    ct = lambda c: _block_diag(c.reshape(*g2, S5_GROUP, S5_STATE).transpose(0, 1, 3, 2))
    bsg = jnp.concatenate([bt(b_re), bt(b_im)], axis=2).astype(BF16)
    ccat = jnp.concatenate([ct(c_re), -ct(c_im)], axis=1).astype(BF16)
    return bsg, ccat, d.reshape(1, S5_WIDTH)


def _s5_param_grads(gb, gc):
    n = S5_LANES
    gb_re = _diag_blocks(gb[:, :, 0:n], S5_GROUP, S5_STATE).transpose(0, 1, 3, 2).reshape(S5_GROUPS, S5_STATE, S5_GROUP)
    gb_im = _diag_blocks(gb[:, :, n:2 * n], S5_GROUP, S5_STATE).transpose(0, 1, 3, 2).reshape(S5_GROUPS, S5_STATE, S5_GROUP)
    gc_re = _diag_blocks(gc[:, 0:n, :], S5_STATE, S5_GROUP).transpose(0, 1, 3, 2).reshape(S5_GROUPS, S5_GROUP, S5_STATE)
    gc_im = -_diag_blocks(gc[:, n:2 * n, :], S5_STATE, S5_GROUP).transpose(0, 1, 3, 2).reshape(S5_GROUPS, S5_GROUP, S5_STATE)
    return gb_re, gb_im, gc_re, gc_im


def _local_step(x, target, weight, emit, small, after=None):
    sp = small
    a_re, a_im = sp["s5_a_re"], sp["s5_a_im"]
    ldt = sp["s5_log_dt"].reshape(S5_GROUPS, 1)

    h1 = _rms_fwd(x, sp["ln_mix_g"], "rms_mix", after=after)
    w_in = weight("w_in", h1)
    proj = _mm_nn(h1, w_in, "mm_in", after=weight("after_w_in", None))
    conv_w = weight("conv_w", None)
    disc = _s5_param_fwd(a_re, a_im, ldt)
    bsg, ccat, d_row = sp["s5_layouts"]
    abar_t, coef_t = _s5_to_tile(disc[0], disc[1]), _s5_to_tile(disc[2], disc[3])
    y, sb = _s5_fwd(proj, bsg, ccat, d_row, abar_t, coef_t)
    z16 = _gelu_fwd(y)
    w_glu = weight("s5_w_glu", z16)
    gl = _mm_nn(z16, w_glu, "mm_glu")
    z2 = _glu_fwd(y, gl, sp["s5_b_glu"])
    w_ps = weight("w_proj_s5", z2)
    ys = _mm_nn(z2, w_ps, "mm_proj_s5")
    o_raw, oh, s0s = _hgrn_fwd(proj, sp["hgrn_lb_logits"], sp["hgrn_norm_g"])
    w_ph = weight("w_proj_hgrn", oh)
    yh = _mm_nn(oh, w_ph, "mm_proj_hgrn")
    merged = _merge_fwd(proj, ys, yh)
    w_out = weight("w_out", merged)
    x1 = _mm_nn(merged, w_out, "mm_out", res=x)
    h2 = _rms_fwd(x1, sp["ln_ffn_g"], "rms_ffn")
    w_up = weight("w_up", h2)
    up = _mm_nn(h2, w_up, "mm_up")
    act = _ffn_act_fwd(up, conv_w, sp["conv_b"])
    w_down = weight("w_down", act)
    x2 = _mm_nn(act, w_down, "mm_down", res=x1)
    dx2, dx2_16, g_ln_final, loss = _loss_head(x2, sp["ln_final_g"], target)

    dact = _mm_nt(dx2_16, w_down, "mm_down_dx")
    tok = emit("w_down", _mm_tn(act, dx2_16, 1, "mm_down_dw"))
    dup_g, dup_v, dcw_g, dcw_v, dcb_g, dcb_v = _ffn_act_bwd(up, dact, conv_w, sp["conv_b"], after=tok)
    dup = jnp.concatenate([dup_g, dup_v], axis=1)
    g_conv_w = jnp.concatenate([dcw_g, dcw_v], axis=1)
    g_conv_b = jnp.concatenate([dcb_g, dcb_v], axis=1)
    dh2 = _mm_nt(dup, w_up, "mm_up_dx")
    tok = emit("w_up", _mm_tn(h2, dup, N_DEV, "mm_up_dw"))
    dx1, dx1_16, g_ln_ffn = _rms_bwd(x1, sp["ln_ffn_g"], dh2, dx2, "rms_ffn_bwd", True, after=tok)

    dmerged = _mm_nt(dx1_16, w_out, "mm_out_dx")
    tok = emit("w_out", _mm_tn(merged, dx1_16, 1, "mm_out_dw"))
    dys, dyh, dgs, dgh = _merge_bwd(proj, ys, yh, dmerged, after=tok)
    doh = _mm_nt(dyh, w_ph, "mm_proj_hgrn_dx")
    tok = emit("w_proj_hgrn", _mm_tn(oh, dyh, N_DEV, "mm_proj_hgrn_dw"))
    dz2 = _mm_nt(dys, w_ps, "mm_proj_s5_dx", after=tok)
    tok = emit("w_proj_s5", _mm_tn(z2, dys, N_DEV, "mm_proj_s5_dw"))
    dgl, dza, g_b_glu = _glu_bwd(y, gl, sp["s5_b_glu"], dz2, after=tok)
    dzb = _mm_nt(dgl, w_glu, "mm_glu_dx")
    tok = emit("s5_w_glu", _mm_tn(z16, dgl, 1, "mm_glu_dw"))
    dy = _gelu_bwd(y, dza, dzb, after=tok)
    du, gb, gc, gd, g_abar_t, g_coef_t = _s5_bwd(proj, dy, sb, bsg, ccat, d_row, abar_t, coef_t)
    g_a_re, g_a_im, g_ldt = _s5_param_bwd(a_re, a_im, ldt, [*_s5_from_tile(g_abar_t), *_s5_from_tile(g_coef_t)])
    g_b_re, g_b_im, g_c_re, g_c_im = _s5_param_grads(gb, gc)
    dq, dz, dv, dg, g_norm, dlb = _hgrn_bwd(proj, o_raw, s0s, doh, sp["hgrn_lb_logits"], sp["hgrn_norm_g"])
    g_logits = _lb_bwd(sp["hgrn_lb_logits"], dlb)

    small_g = dict(s5_a_re=g_a_re, s5_a_im=g_a_im, s5_log_dt=g_ldt.reshape(1, S5_GROUPS),
                   s5_b_re=g_b_re, s5_b_im=g_b_im, s5_c_re=g_c_re, s5_c_im=g_c_im,
                   s5_d=gd.reshape(S5_GROUPS, S5_GROUP), s5_b_glu=g_b_glu, hgrn_lb_logits=g_logits,
                   hgrn_norm_g=g_norm, ln_ffn_g=g_ln_ffn, conv_w=g_conv_w, conv_b=g_conv_b, ln_final_g=g_ln_final,
                   loss=loss[0, 0:1])
    tok_small = emit("small", small_g)

    dproj = jnp.concatenate([du, dq, dz, dv, dg, dgs, dgh], axis=1)
    tok = emit("w_in", _mm_tn(h1, dproj, N_DEV, "mm_in_dw", after=tok_small))
    dh1 = _mm_nt(dproj, w_in, "mm_in_dx")
    grad_x, g_ln_mix = _rms_bwd(x, sp["ln_mix_g"], dh1, dx1, "rms_mix_bwd", False, after=tok)
    return grad_x, g_ln_mix


BIG = ("w_in", "s5_w_glu", "w_proj_s5", "w_proj_hgrn", "w_out", "w_up", "w_down")
COL_SHARDED = ("w_in", "w_proj_s5", "w_proj_hgrn", "w_up")
SMALL = ("ln_mix_g", "s5_a_re", "s5_a_im", "s5_log_dt", "s5_b_re", "s5_b_im", "s5_c_re", "s5_c_im", "s5_d",
         "s5_b_glu", "hgrn_lb_logits", "hgrn_norm_g", "ln_ffn_g", "conv_b", "ln_final_g")
WEIGHTS = ("ln_mix_g", "w_in", "s5_a_re", "s5_a_im", "s5_log_dt", "s5_b_re", "s5_b_im", "s5_c_re", "s5_c_im", "s5_d",
           "s5_w_glu", "s5_b_glu", "w_proj_s5", "hgrn_lb_logits", "hgrn_norm_g", "w_proj_hgrn", "w_out", "ln_ffn_g",
           "w_up", "conv_w", "conv_b", "w_down", "ln_final_g")


def kernel(x, ln_mix_g, w_in, s5_a_re, s5_a_im, s5_log_dt, s5_b_re, s5_b_im, s5_c_re, s5_c_im, s5_d, s5_w_glu, s5_b_glu, w_proj_s5, hgrn_lb_logits, hgrn_norm_g, w_proj_hgrn, w_out, ln_ffn_g, w_up, conv_w, conv_b, w_down, ln_final_g, loss_target, m_ln_mix_g, m_w_in, m_s5_a_re, m_s5_a_im, m_s5_log_dt, m_s5_b_re, m_s5_b_im, m_s5_c_re, m_s5_c_im, m_s5_d, m_s5_w_glu, m_s5_b_glu, m_w_proj_s5, m_hgrn_lb_logits, m_hgrn_norm_g, m_w_proj_hgrn, m_w_out, m_ln_ffn_g, m_w_up, m_conv_w, m_conv_b, m_w_down, m_ln_final_g, v_ln_mix_g, v_w_in, v_s5_a_re, v_s5_a_im, v_s5_log_dt, v_s5_b_re, v_s5_b_im, v_s5_c_re, v_s5_c_im, v_s5_d, v_s5_w_glu, v_s5_b_glu, v_w_proj_s5, v_hgrn_lb_logits, v_hgrn_norm_g, v_w_proj_hgrn, v_w_out, v_ln_ffn_g, v_w_up, v_conv_w, v_conv_b, v_w_down, v_ln_final_g):
    given = dict(locals())
    w = {n: given[n] for n in WEIGHTS}
    mom = {n: given["m_" + n] for n in WEIGHTS}
    var = {n: given["v_" + n] for n in WEIGHTS}

    first = _two_level_start([w_in[0].astype(BF16), conv_w[0]], "gather_first_start")
    zero = first["token"][0, 0]
    packed_small = SMALL[1:]
    pw, pm, pv = (_pack([d[n] for n in packed_small]) + zero for d in (w, mom, var))
    layouts = _s5_layouts(s5_b_re[0] + zero, s5_b_im[0], s5_c_re[0] + zero, s5_c_im[0], s5_d[0])
    gather_groups = (("s5_w_glu", "w_proj_s5", "w_proj_hgrn", "w_out"), ("w_up",), ("w_down",))
    shard16 = {n: w[n][0].astype(BF16) + zero.astype(BF16) for g in gather_groups for n in g}
    zones = {n: _landing_zone(s, False) for n, s in shard16.items()}
    pending, ready = {}, {}

    def weight(name, after):
        if "w_in" not in ready:
            local_work = [after, pw, pm, pv, layouts[0], layouts[1], *zones.values()]
            passed = _two_level_pass(first, local_work, "gather_first_pass")
            ready["w_in"], conv_w_all = _two_level_wait(passed, "gather_first_wait")
            ready["conv_w"] = conv_w_all.transpose(1, 0, 2).reshape(3, 2 * D_FF)
            token = ready["w_in"]
            for i, group in enumerate(gather_groups):
                handle = _exchange_start([shard16[n] for n in group], False, token, f"gather_start_{i}",
                                         lands=[zones[n] for n in group])
                token = handle["token"]
                for n in group:
                    pending[n] = (group, handle, f"gather_wait_{i}")
            ready["after_w_in"] = token
        if name not in ready:
            group, handle, wait_name = pending[name]
            for n, g in zip(group, _exchange_wait(handle, [after], wait_name)):
                ready[n] = g
        g = ready[name]
        return g if name not in BIG or name in COL_SHARDED else g.reshape(1, N_DEV * g.shape[1], g.shape[2])

    scatter_groups = (("w_down",), ("w_up",), ("w_out", "w_proj_hgrn", "w_proj_s5", "s5_w_glu"), ("w_in",))
    emitted, scatters = {}, []
    packed_names = SMALL[1:] + ("conv_w", "loss")

    def emit(name, grad):
        if name == "small":
            emitted[name] = ([grad[n].shape for n in packed_names],
                             _exchange_start([_pack([grad[n] for n in packed_names])], False, None, "small_start"))
            return emitted[name][1]["token"]
        emitted[name] = grad if name in COL_SHARDED else grad.reshape(N_DEV, -1, grad.shape[2])
        group = scatter_groups[len(scatters)]
        if not all(n in emitted for n in group):
            return None
        handle = _exchange_start([emitted[n] for n in group], True, None, f"scatter_start_{len(scatters)}")
        scatters.append((group, handle))
        return handle["token"]

    small = dict(ln_mix_g=ln_mix_g, s5_a_re=s5_a_re[0], s5_a_im=s5_a_im[0], s5_log_dt=s5_log_dt, s5_layouts=layouts,
                 s5_b_glu=s5_b_glu, hgrn_lb_logits=hgrn_lb_logits, hgrn_norm_g=hgrn_norm_g, ln_ffn_g=ln_ffn_g,
                 conv_b=conv_b, ln_final_g=ln_final_g.reshape(1, D_MODEL))
    grad_x, g_ln_mix = _local_step(x[0], loss_target[0], weight, emit, small, after=first["token"])

    shapes, handle = emitted["small"]
    total = _sum_slots(_exchange_wait(handle, [grad_x], "small_wait")[0], "sum_small")
    summed = dict(zip(packed_names, _unpack(total, shapes)))
    mix_all = _all_gather([g_ln_mix.reshape(-1, LANE)], "gather_ln_mix")[0]
    summed["ln_mix_g"] = _sum_slots(mix_all, "sum_ln_mix").reshape(1, D_MODEL)

    grads, delta, new_m, new_v = {}, {}, {}, {}
    afters = [grad_x, total]
    for i, (group, handle) in enumerate(scatters):
        for n, r in zip(group, _exchange_wait(handle, afters, f"scatter_wait_{i}")):
            g, d, m2, v2 = _sum_adam(r, w[n][0], mom[n][0], var[n][0], "adam_" + n)
            grads[n], delta[n], new_m[n], new_v[n] = g[None], d[None], m2[None], v2[None]
        if i == len(scatters) - 2:
            afters = [delta[n] for g2, _ in scatters[:-1] for n in g2]

    d_s, m_s, v_s = _adam_rows(_pack([summed[n] for n in packed_small]), pw, pm, pv, "adam_small")
    wshapes = [w[n].shape for n in packed_small]
    for n, d, m2, v2 in zip(packed_small, _unpack(d_s, wshapes), _unpack(m_s, wshapes), _unpack(v_s, wshapes)):
        grads[n], delta[n], new_m[n], new_v[n] = summed[n].reshape(w[n].shape), d, m2, v2
    grads["ln_mix_g"] = summed["ln_mix_g"]
    delta["ln_mix_g"], new_m["ln_mix_g"], new_v["ln_mix_g"] = _adam_rows(summed["ln_mix_g"], ln_mix_g, m_ln_mix_g,
                                                                         v_ln_mix_g, "adam_ln_mix")
    me = 4 * lax.axis_index("x") + 2 * lax.axis_index("y") + lax.axis_index("c")
    ncol = conv_w.shape[2]
    g_cw = lax.dynamic_slice_in_dim(summed["conv_w"], me * ncol, ncol, axis=1)
    d_cw, m_cw, v_cw = _adam_rows(g_cw, conv_w[0], m_conv_w[0], v_conv_w[0], "adam_conv_w")
    grads["conv_w"], delta["conv_w"], new_m["conv_w"], new_v["conv_w"] = g_cw[None], d_cw[None], m_cw[None], v_cw[None]

    return (summed["loss"].reshape(()), grad_x[None], *[grads[n] for n in WEIGHTS], *[delta[n] for n in WEIGHTS],
            *[new_m[n] for n in WEIGHTS], *[new_v[n] for n in WEIGHTS])
```

```python
import math

import jax
import jax.numpy as jnp
from jax import lax
from jax.experimental import pallas as pl
from jax.experimental.pallas import tpu as pltpu

F32 = jnp.float32
BF16 = jnp.bfloat16

N_DEV = 8
D_MODEL = 2048
S5_WIDTH = 1024
S5_GROUP = 16
S5_GROUPS = 64
S5_STATE = 64
S5_MAX_RE = -1e-4
S5_SUPER = 8
S5_LANES = S5_SUPER * S5_STATE
HGRN_WIDTH = 1024
HGRN_HEADS = 8
HGRN_DH = 128
HGRN_CHUNK = 64
HGRN_SUBS = 8
D_FF = 5632
RMS_EPS = 1e-6
ADAM_LR = 0.001
ADAM_B1 = 0.9
ADAM_B2 = 0.999
ADAM_EPS = 1e-08
ADAM_WD = 0.01
ADAM_STEP = 10

LANE = 128
SUBLANE = 8
VMEM_LIMIT = 48 * 1024 * 1024
MESH = pl.DeviceIdType.MESH
GELU_C = math.sqrt(2.0 / math.pi)
GELU_A = 0.044715


def _params(sem=None):
    return pltpu.CompilerParams(dimension_semantics=sem, vmem_limit_bytes=VMEM_LIMIT)


def _pick(n, cap, unit=LANE):
    best = None
    for t in range(unit, min(n, cap) + 1, unit):
        if n % t == 0:
            best = t
    return best if best is not None else n


def _ordered(body, in_specs, args, after):
    if after is None:
        return body, list(in_specs), list(args)
    n_in = len(args)

    def ordered_body(*refs):
        return body(*refs[:n_in], *refs[n_in + 1:])

    return ordered_body, [*in_specs, pl.BlockSpec(memory_space=pl.ANY)], [*args, after]


def _sigmoid(x):
    return 0.5 * jnp.tanh(0.5 * x) + 0.5


def _silu_and_grad(x):
    s = _sigmoid(x)
    return x * s, s * (1.0 + x * (1.0 - s))


def _gelu_and_grad(y):
    inner = GELU_C * (y + GELU_A * y * y * y)
    th = jnp.tanh(inner)
    val = 0.5 * y * (1.0 + th)
    grad = 0.5 * (1.0 + th) + 0.5 * y * (1.0 - th * th) * GELU_C * (1.0 + 3.0 * GELU_A * y * y)
    return val, grad


def _dot(a, b):
    return jnp.dot(a, b, preferred_element_type=F32)


def _dot_nt(a, b):
    return lax.dot_general(a, b, (((1,), (1,)), ((), ())), preferred_element_type=F32)


def _dot_tn(a, b):
    return lax.dot_general(a, b, (((0,), (0,)), ((), ())), preferred_element_type=F32)


def _blocks_per_step(nb, ns, tn, cap=2048):
    if tn != ns:
        return 1
    best = 1
    for b in range(1, nb + 1):
        if nb % b == 0 and b * ns <= cap:
            best = b
    return best


NN_TILE_BYTES = 42 * 1024 * 1024


def _mm_nn(a, w, name, res=None, out_dtype=F32, after=None):
    m, kdim = a.shape
    nb, _, ns = w.shape
    tk, tn = _pick(kdim, D_FF), _pick(ns, 1536)
    npb, nk = ns // tn, kdim // tk
    bps = _blocks_per_step(nb, ns, tn)
    assert bps == 1 or nk == 1

    def buffers(rows):
        return 2 * (rows * tk * 2 + bps * tk * tn * 2 + rows * bps * tn * 4 * (2 if res is not None else 1))

    tm = next((r for r in (_pick(m, 1024), _pick(m, 512)) if buffers(r) <= NN_TILE_BYTES), _pick(m, 256))

    def body(*refs):
        a_ref, w_ref = refs[0], refs[1]
        r_ref = refs[2] if res is not None else None
        o_ref = refs[3] if res is not None else refs[2]

        def finish(r, cols):
            if res is not None:
                r = r + r_ref[:, cols]
            o_ref[:, cols] = r.astype(out_dtype)

        if nk == 1:
            for b in range(bps):
                finish(_dot(a_ref[...], w_ref[b]), slice(b * tn, (b + 1) * tn))
            return
        acc = refs[-1]
        k = pl.program_id(2)

        @pl.when(k == 0)
        def _():
            acc[...] = jnp.zeros_like(acc)

        acc[...] += _dot(a_ref[...], w_ref[0])

        @pl.when(k == nk - 1)
        def _():
            finish(acc[...], slice(0, tn))

    in_specs = [pl.BlockSpec((tm, tk), lambda j, i, k: (i, k)),
                pl.BlockSpec((bps, tk, tn), lambda j, i, k: (j // npb, k, j % npb))]
    args = [a, w]
    if res is not None:
        in_specs.append(pl.BlockSpec((tm, bps * tn), lambda j, i, k: (i, j)))
        args.append(res)
    body, in_specs, args = _ordered(body, in_specs, args, after)
    return pl.pallas_call(
        body, name=name, grid=(nb * npb // bps, m // tm, nk),
        in_specs=in_specs, out_specs=pl.BlockSpec((tm, bps * tn), lambda j, i, k: (i, j)),
        out_shape=jax.ShapeDtypeStruct((m, nb * ns), out_dtype),
        scratch_shapes=[pltpu.VMEM((tm, tn), F32)] if nk > 1 else [],
        compiler_params=_params(("parallel", "parallel", "arbitrary")),
    )(*args)


NT_STEP_COLS = 2816


def _mm_nt(a, w, name, out_dtype=F32, after=None):
    m, _ = a.shape
    nb, kdim, ns = w.shape
    tm, tko, tn = _pick(m, 1024), _pick(kdim, 1024), _pick(ns, 2048)
    npb = ns // tn
    bps = _blocks_per_step(nb, ns, tn, cap=NT_STEP_COLS)
    nred = nb * npb // bps

    def body(a_ref, w_ref, o_ref, *scratch):
        total = _dot_nt(a_ref[:, 0:tn], w_ref[0])
        for b in range(1, bps):
            total = total + _dot_nt(a_ref[:, b * tn:(b + 1) * tn], w_ref[b])
        if nred == 1:
            o_ref[...] = total.astype(out_dtype)
            return
        acc = scratch[0]
        n = pl.program_id(2)

        @pl.when(n == 0)
        def _():
            acc[...] = jnp.zeros_like(acc)

        acc[...] += total

        @pl.when(n == nred - 1)
        def _():
            o_ref[...] = acc[...].astype(out_dtype)

    in_specs = [pl.BlockSpec((tm, bps * tn), lambda i, j, n: (i, n)),
                pl.BlockSpec((bps, tko, tn), lambda i, j, n: (n // npb, j, n % npb))]
    body, in_specs, args = _ordered(body, in_specs, [a, w], after)
    return pl.pallas_call(
        body, name=name, grid=(m // tm, kdim // tko, nred),
        in_specs=in_specs,
        out_specs=pl.BlockSpec((tm, tko), lambda i, j, n: (i, j)),
        out_shape=jax.ShapeDtypeStruct((m, kdim), out_dtype),
        scratch_shapes=[pltpu.VMEM((tm, tko), F32)] if nred > 1 else [],
        compiler_params=_params(("parallel", "parallel", "arbitrary")),
    )(*args)


def _mm_tn(a, d, nb, name, out_dtype=BF16, after=None):
    m, kdim = a.shape
    ns = d.shape[1] // nb
    tm, tko, tn = _pick(m, 4096), _pick(kdim, 512), _pick(ns, 1536)
    npb, nm = ns // tn, m // tm
    bps = _blocks_per_step(nb, ns, tn, cap=1536)
    assert bps == 1 or nm == 1

    def body(a_ref, d_ref, o_ref, *scratch):
        if nm == 1:
            for b in range(bps):
                o_ref[b] = _dot_tn(a_ref[...], d_ref[:, b * tn:(b + 1) * tn]).astype(out_dtype)
            return
        acc = scratch[0]
        r = pl.program_id(2)

        @pl.when(r == 0)
        def _():
            acc[...] = jnp.zeros_like(acc)

        acc[...] += _dot_tn(a_ref[...], d_ref[...])

        @pl.when(r == nm - 1)
        def _():
            o_ref[0] = acc[...].astype(out_dtype)

    in_specs = [pl.BlockSpec((tm, tko), lambda j, i, r: (r, i)), pl.BlockSpec((tm, bps * tn), lambda j, i, r: (r, j))]
    body, in_specs, args = _ordered(body, in_specs, [a, d], after)
    return pl.pallas_call(
        body, name=name, grid=(nb * npb // bps, kdim // tko, nm),
        in_specs=in_specs,
        out_specs=pl.BlockSpec((bps, tko, tn), lambda j, i, r: (j // npb, i, j % npb)),
        out_shape=jax.ShapeDtypeStruct((nb, kdim, ns), out_dtype),
        scratch_shapes=[pltpu.VMEM((tko, tn), F32)] if nm > 1 else [],
        compiler_params=_params(("parallel", "parallel", "arbitrary")),
    )(*args)


def _rms_fwd(x, g, name, after=None):
    t, d = x.shape
    tr = _pick(t, 512, SUBLANE)

    def body(x_ref, g_ref, h_ref):
        xv = x_ref[...]
        r = lax.rsqrt(jnp.mean(xv * xv, axis=-1, keepdims=True) + RMS_EPS)
        h_ref[...] = (xv * r * g_ref[...]).astype(BF16)

    in_specs = [pl.BlockSpec((tr, d), lambda i: (i, 0)), pl.BlockSpec((1, d), lambda i: (0, 0))]
    body, in_specs, args = _ordered(body, in_specs, [x, g], after)
    return pl.pallas_call(
        body, name=name, grid=(t // tr,),
        in_specs=in_specs,
        out_specs=pl.BlockSpec((tr, d), lambda i: (i, 0)),
        out_shape=jax.ShapeDtypeStruct((t, d), BF16),
        compiler_params=_params(("parallel",)),
    )(*args)


def _rms_bwd(x, g, dh, add, name, want_bf16, after=None):
    t, d = x.shape
    tr = _pick(t, 256, SUBLANE)

    def body(x_ref, g_ref, dh_ref, add_ref, *outs):
        if want_bf16:
            dx_ref, dxb_ref, dg_ref = outs
        else:
            dx_ref, dg_ref = outs
        i = pl.program_id(0)

        @pl.when(i == 0)
        def _():
            dg_ref[...] = jnp.zeros_like(dg_ref)

        xv, dhv = x_ref[...], dh_ref[...]
        r = lax.rsqrt(jnp.mean(xv * xv, axis=-1, keepdims=True) + RMS_EPS)
        xh = xv * r
        dg_ref[...] += jnp.sum(dhv * xh, axis=0, keepdims=True)
        dxh = dhv * g_ref[...]
        dx = add_ref[...] + r * (dxh - xh * jnp.mean(dxh * xh, axis=-1, keepdims=True))
        dx_ref[...] = dx
        if want_bf16:
            dxb_ref[...] = dx.astype(BF16)

    row = pl.BlockSpec((tr, d), lambda i: (i, 0))
    vec = pl.BlockSpec((1, d), lambda i: (0, 0))
    out_specs = [row] + ([row] if want_bf16 else []) + [vec]
    out_shape = ([jax.ShapeDtypeStruct((t, d), F32)] + ([jax.ShapeDtypeStruct((t, d), BF16)] if want_bf16 else [])
                 + [jax.ShapeDtypeStruct((1, d), F32)])
    body, in_specs, args = _ordered(body, [row, vec, row, row], [x, g, dh, add], after)
    return pl.pallas_call(
        body, name=name, grid=(t // tr,),
        in_specs=in_specs, out_specs=out_specs, out_shape=out_shape,
        compiler_params=_params(("arbitrary",)),
    )(*args)


def _loss_head(x2, g, target, name="loss_head"):
    t, d = x2.shape
    tr = _pick(t, 512, SUBLANE)

    def body(x_ref, g_ref, t_ref, dx_ref, dxb_ref, dg_ref, loss_ref):
        i = pl.program_id(0)

        @pl.when(i == 0)
        def _():
            dg_ref[...] = jnp.zeros_like(dg_ref)
            loss_ref[...] = jnp.zeros_like(loss_ref)

        xv = x_ref[...]
        gv = g_ref[...]
        r = lax.rsqrt(jnp.mean(xv * xv, axis=-1, keepdims=True) + RMS_EPS)
        xh = xv * r
        err = xh * gv - t_ref[...]
        part = 0.5 * jnp.sum(jnp.mean(err * err, axis=-1, keepdims=True), axis=0, keepdims=True)
        loss_ref[...] += jnp.broadcast_to(part, loss_ref.shape)
        dy = err * (1.0 / d)
        dg_ref[...] += jnp.sum(dy * xh, axis=0, keepdims=True)
        dxh = dy * gv
        dx = r * (dxh - xh * jnp.mean(dxh * xh, axis=-1, keepdims=True))
        dx_ref[...] = dx
        dxb_ref[...] = dx.astype(BF16)

    row = pl.BlockSpec((tr, d), lambda i: (i, 0))
    vec = pl.BlockSpec((1, d), lambda i: (0, 0))
    return pl.pallas_call(
        body, name=name, grid=(t // tr,),
        in_specs=[row, vec, row],
        out_specs=[row, row, vec, pl.BlockSpec((1, LANE), lambda i: (0, 0))],
        out_shape=[jax.ShapeDtypeStruct((t, d), F32), jax.ShapeDtypeStruct((t, d), BF16),
                   jax.ShapeDtypeStruct((1, d), F32), jax.ShapeDtypeStruct((1, LANE), F32)],
        compiler_params=_params(("arbitrary",)),
    )(x2, g, target)


def _s5_discretize(a_re, a_im, ldt):
    lam_re = jnp.minimum(a_re, S5_MAX_RE)
    lam_im = a_im
    dt = jnp.exp(ldt)
    mag = jnp.exp(lam_re * dt)
    abar_re = mag * jnp.cos(lam_im * dt)
    abar_im = mag * jnp.sin(lam_im * dt)
    den = lam_re * lam_re + lam_im * lam_im
    nr = abar_re - 1.0
    ni = abar_im
    coef_re = (nr * lam_re + ni * lam_im) / den
    coef_im = (ni * lam_re - nr * lam_im) / den
    return abar_re, abar_im, coef_re, coef_im


def _s5_param_fwd(a_re, a_im, ldt):
    def body(ar_ref, ai_ref, l_ref, o0, o1, o2, o3):
        outs = _s5_discretize(ar_ref[...], ai_ref[...], l_ref[...])
        for o, v in zip((o0, o1, o2, o3), outs):
            o[...] = v

    sh = jax.ShapeDtypeStruct(a_re.shape, F32)
    return pl.pallas_call(body, name="s5_param_fwd", out_shape=[sh, sh, sh, sh], compiler_params=_params())(a_re, a_im, ldt)


def _s5_param_bwd(a_re, a_im, ldt, cts):
    def body(ar_ref, ai_ref, l_ref, c0, c1, c2, c3, g0, g1, g2):
        _, vjp = jax.vjp(_s5_discretize, ar_ref[...], ai_ref[...], l_ref[...])
        ga, gb, gl = vjp((c0[...], c1[...], c2[...], c3[...]))
        g0[...] = ga
        g1[...] = gb
        g2[...] = gl

    sh = jax.ShapeDtypeStruct(a_re.shape, F32)
    return pl.pallas_call(body, name="s5_param_bwd", out_shape=[sh, sh, jax.ShapeDtypeStruct(ldt.shape, F32)],
                          compiler_params=_params())(a_re, a_im, ldt, *cts)


def _cmul(ar, ai, br, bi):
    return ar * br - ai * bi, ar * bi + ai * br


S5_TC = 128
S5_TILE = S5_SUPER * SUBLANE
S5_HALF = S5_TILE // 2


def _s5_to_tile(re, im):
    f = lambda a: a.reshape(S5_SUPER, S5_LANES // LANE, LANE).transpose(1, 0, 2).reshape(S5_HALF, LANE)
    return jnp.concatenate([f(re), f(im)], axis=0)


def _s5_from_tile(tile):
    f = lambda a: a.reshape(S5_LANES // LANE, S5_SUPER, LANE).transpose(1, 0, 2).reshape(S5_GROUPS, S5_STATE)
    return f(tile[0:S5_HALF]), f(tile[S5_HALF:])


RE = slice(0, S5_HALF)
IM = slice(S5_HALF, S5_TILE)


def _s5_scatter_rows(buf, rows, first_tile=0):
    tc = rows[0].shape[0]
    for j in range(SUBLANE):
        stacked = jnp.stack([r[:, j * LANE:(j + 1) * LANE] for r in rows], axis=0)
        buf[first_tile:first_tile + tc, j * SUBLANE:(j + 1) * SUBLANE, :] = jnp.swapaxes(stacked, 0, 1)


def _s5_gather_rows(buf, tc, first_tile=0):
    per_j = [jnp.swapaxes(buf[first_tile:first_tile + tc, j * SUBLANE:(j + 1) * SUBLANE, :], 0, 1)
             for j in range(SUBLANE)]
    return [jnp.concatenate([per_j[j][k] for j in range(SUBLANE)], axis=1) for k in range(S5_SUPER)]


def _s5_fwd(proj, bsg, ccat, d_row, abar_t, coef_t):
    t = proj.shape[0]
    tc = min(t, S5_TC)
    n_chunks = t // tc

    def body(u_ref, b_ref, c_ref, d_ref, a_ref, cf_ref, y_ref, sb_ref, x, car):
        @pl.when(pl.program_id(0) == 0)
        def _():
            car[...] = jnp.zeros_like(car)

        sb_ref[...] = car[...]
        u = u_ref[...]
        _s5_scatter_rows(x, [_dot(u[:, k * LANE:(k + 1) * LANE].astype(BF16), b_ref[k]) for k in range(S5_SUPER)])
        ar, ai = a_ref[RE, :], a_ref[IM, :]
        cr, ci = cf_ref[RE, :], cf_ref[IM, :]

        def step(i, carry):
            sr, si = carry
            xr, xi = _cmul(cr, ci, x[i, RE, :], x[i, IM, :])
            sr, si = ar * sr - ai * si + xr, ar * si + ai * sr + xi
            x[i, RE, :] = sr
            x[i, IM, :] = si
            return sr, si

        sr, si = lax.fori_loop(0, tc, step, (car[RE, :], car[IM, :]), unroll=8)
        car[RE, :] = sr
        car[IM, :] = si
        for k, s_k in enumerate(_s5_gather_rows(x, tc)):
            cols = slice(k * LANE, (k + 1) * LANE)
            y_ref[:, cols] = _dot(s_k.astype(BF16), c_ref[k]) + d_ref[:, cols] * u[:, cols]

    full = lambda shape: pl.BlockSpec(shape, lambda c: (0,) * len(shape))
    return pl.pallas_call(
        body, name="s5_fwd", grid=(n_chunks,),
        in_specs=[pl.BlockSpec((tc, S5_WIDTH), lambda c: (c, 0)), full(bsg.shape), full(ccat.shape), full(d_row.shape),
                  full(abar_t.shape), full(coef_t.shape)],
        out_specs=[pl.BlockSpec((tc, S5_WIDTH), lambda c: (c, 0)), pl.BlockSpec((None, S5_TILE, LANE), lambda c: (c, 0, 0))],
        out_shape=[jax.ShapeDtypeStruct((t, S5_WIDTH), F32), jax.ShapeDtypeStruct((n_chunks, S5_TILE, LANE), F32)],
        scratch_shapes=[pltpu.VMEM((tc, S5_TILE, LANE), F32), pltpu.VMEM((S5_TILE, LANE), F32)],
        compiler_params=_params(("arbitrary",)),
    )(proj, bsg, ccat, d_row, abar_t, coef_t)


def _s5_bwd(proj, dy, sb, bsg, ccat, d_row, abar_t, coef_t):
    t = proj.shape[0]
    tc = min(t, S5_TC)
    n_chunks = t // tc
    last = n_chunks - 1

    def body(u_ref, dy_ref, sb_ref, b_ref, c_ref, d_ref, a_ref, cf_ref,
             du_ref, gb_ref, gc_ref, gd_ref, ga_ref, gcf_ref, xb, xs, xg, gcar, acc):
        @pl.when(pl.program_id(0) == 0)
        def _():
            gcar[...] = jnp.zeros_like(gcar)
            acc[...] = jnp.zeros_like(acc)
            gb_ref[...] = jnp.zeros_like(gb_ref)
            gc_ref[...] = jnp.zeros_like(gc_ref)
            gd_ref[...] = jnp.zeros_like(gd_ref)

        u = u_ref[...]
        dyv = dy_ref[...]
        u16, dy16 = u.astype(BF16), dyv.astype(BF16)
        subs = [slice(k * LANE, (k + 1) * LANE) for k in range(S5_SUPER)]
        _s5_scatter_rows(xb, [_dot(u16[:, c], b_ref[k]) for k, c in enumerate(subs)])
        _s5_scatter_rows(xg, [_dot_nt(dy16[:, c], c_ref[k]) for k, c in enumerate(subs)])
        ar, ai = a_ref[RE, :], a_ref[IM, :]
        cr, ci = cf_ref[RE, :], cf_ref[IM, :]

        xs[0] = sb_ref[...]

        def fstep(i, carry):
            sr, si = carry
            xr, xi = _cmul(cr, ci, xb[i, RE, :], xb[i, IM, :])
            sr, si = ar * sr - ai * si + xr, ar * si + ai * sr + xi
            xs[i + 1, RE, :] = sr
            xs[i + 1, IM, :] = si
            return sr, si

        lax.fori_loop(0, tc, fstep, (sb_ref[RE, :], sb_ref[IM, :]), unroll=8)

        def rstep(n, carry):
            gr, gi, a0, a1, a2, a3 = carry
            i = tc - 1 - n
            xr = xg[i, RE, :] + ar * gr + ai * gi
            xi = xg[i, IM, :] + ar * gi - ai * gr
            pr, pi = xs[i, RE, :], xs[i, IM, :]
            br, bi = xb[i, RE, :], xb[i, IM, :]
            a0 = a0 + pr * xr + pi * xi
            a1 = a1 + pr * xi - pi * xr
            a2 = a2 + br * xr + bi * xi
            a3 = a3 + br * xi - bi * xr
            xg[i, RE, :] = cr * xr + ci * xi
            xg[i, IM, :] = cr * xi - ci * xr
            return xr, xi, a0, a1, a2, a3

        init = (gcar[RE, :], gcar[IM, :], acc[0], acc[1], acc[2], acc[3])
        gr, gi, a0, a1, a2, a3 = lax.fori_loop(0, tc, rstep, init, unroll=4)
        gcar[RE, :] = gr
        gcar[IM, :] = gi
        for idx, a in enumerate((a0, a1, a2, a3)):
            acc[idx] = a
        ga_ref[RE, :] = a0
        ga_ref[IM, :] = a1
        gcf_ref[RE, :] = a2
        gcf_ref[IM, :] = a3

        g_rows = _s5_gather_rows(xg, tc)
        s_rows = _s5_gather_rows(xs, tc, first_tile=1)
        for k in range(S5_SUPER):
            cols = subs[k]
            g16 = g_rows[k].astype(BF16)
            s16 = s_rows[k].astype(BF16)
            gb_ref[k] += _dot_tn(u16[:, cols], g16)
            gc_ref[k] += _dot_tn(s16, dy16[:, cols])
            du_ref[:, cols] = (_dot_nt(g16, b_ref[k]) + d_ref[:, cols] * dyv[:, cols]).astype(BF16)
        gd_ref[...] += jnp.sum(dyv * u, axis=0, keepdims=True)

    full = lambda shape: pl.BlockSpec(shape, lambda c: (0,) * len(shape))
    rows = pl.BlockSpec((tc, S5_WIDTH), lambda c: (last - c, 0))
    tile = (S5_TILE, LANE)
    return pl.pallas_call(
        body, name="s5_bwd", grid=(n_chunks,),
        in_specs=[rows, rows, pl.BlockSpec((None, S5_TILE, LANE), lambda c: (last - c, 0, 0)),
                  full(bsg.shape), full(ccat.shape), full(d_row.shape), full(abar_t.shape), full(coef_t.shape)],
        out_specs=[rows, full(bsg.shape), full(ccat.shape), full(d_row.shape), full(tile), full(tile)],
        out_shape=[jax.ShapeDtypeStruct((t, S5_WIDTH), BF16), jax.ShapeDtypeStruct(bsg.shape, F32),
                   jax.ShapeDtypeStruct(ccat.shape, F32), jax.ShapeDtypeStruct(d_row.shape, F32),
                   jax.ShapeDtypeStruct(tile, F32), jax.ShapeDtypeStruct(tile, F32)],
        scratch_shapes=[pltpu.VMEM((tc, S5_TILE, LANE), F32), pltpu.VMEM((tc + 1, S5_TILE, LANE), F32),
                        pltpu.VMEM((tc, S5_TILE, LANE), F32), pltpu.VMEM(tile, F32),
                        pltpu.VMEM((4, S5_HALF, LANE), F32)],
        compiler_params=_params(("arbitrary",)),
    )(proj, dy, sb, bsg, ccat, d_row, abar_t, coef_t)


def _gelu_fwd(y, name="s5_gelu"):
    t, w = y.shape
    tr = _pick(t, 512, SUBLANE)

    def body(y_ref, z_ref):
        z_ref[...] = _gelu_and_grad(y_ref[...])[0].astype(BF16)

    row = pl.BlockSpec((tr, w), lambda i: (i, 0))
    return pl.pallas_call(body, name=name, grid=(t // tr,), in_specs=[row], out_specs=row,
                          out_shape=jax.ShapeDtypeStruct((t, w), BF16), compiler_params=_params(("parallel",)))(y)


def _glu_fwd(y, gl, b, name="s5_glu"):
    t, w = y.shape
    tr = _pick(t, 512, SUBLANE)

    def body(y_ref, gl_ref, b_ref, z2_ref):
        z = _gelu_and_grad(y_ref[...])[0]
        z2_ref[...] = (z * _sigmoid(gl_ref[...] + b_ref[...])).astype(BF16)

    row = pl.BlockSpec((tr, w), lambda i: (i, 0))
    return pl.pallas_call(body, name=name, grid=(t // tr,),
                          in_specs=[row, row, pl.BlockSpec((1, w), lambda i: (0, 0))], out_specs=row,
                          out_shape=jax.ShapeDtypeStruct((t, w), BF16), compiler_params=_params(("parallel",)))(y, gl, b)


def _glu_bwd(y, gl, b, dz2, name="s5_glu_bwd", after=None):
    t, w = y.shape
    tr = _pick(t, 512, SUBLANE)

    def body(y_ref, gl_ref, b_ref, dz2_ref, dgl_ref, dza_ref, db_ref):
        @pl.when(pl.program_id(0) == 0)
        def _():
            db_ref[...] = jnp.zeros_like(db_ref)

        z = _gelu_and_grad(y_ref[...])[0]
        s = _sigmoid(gl_ref[...] + b_ref[...])
        dz2v = dz2_ref[...]
        dgl = dz2v * z * s * (1.0 - s)
        dgl_ref[...] = dgl.astype(BF16)
        dza_ref[...] = dz2v * s
        db_ref[...] += jnp.sum(dgl, axis=0, keepdims=True)

    row = pl.BlockSpec((tr, w), lambda i: (i, 0))
    vec = pl.BlockSpec((1, w), lambda i: (0, 0))
    body, in_specs, args = _ordered(body, [row, row, vec, row], [y, gl, b, dz2], after)
    return pl.pallas_call(body, name=name, grid=(t // tr,), in_specs=in_specs, out_specs=[row, row, vec],
                          out_shape=[jax.ShapeDtypeStruct((t, w), BF16), jax.ShapeDtypeStruct((t, w), F32),
                                     jax.ShapeDtypeStruct((1, w), F32)],
                          compiler_params=_params(("arbitrary",)))(*args)


def _gelu_bwd(y, dza, dzb, name="s5_gelu_bwd", after=None):
    t, w = y.shape
    tr = _pick(t, 512, SUBLANE)

    def body(y_ref, a_ref, b_ref, dy_ref):
        dy_ref[...] = (a_ref[...] + b_ref[...]) * _gelu_and_grad(y_ref[...])[1]

    row = pl.BlockSpec((tr, w), lambda i: (i, 0))
    body, in_specs, args = _ordered(body, [row, row, row], [y, dza, dzb], after)
    return pl.pallas_call(body, name=name, grid=(t // tr,), in_specs=in_specs, out_specs=row,
                          out_shape=jax.ShapeDtypeStruct((t, w), F32), compiler_params=_params(("parallel",)))(*args)


def _tri_dot(tri16, x):
    hi = x.astype(BF16)
    lo = (x - hi.astype(F32)).astype(BF16)
    return _dot(tri16, hi) + _dot(tri16, lo)


def _hgrn_pre(q_in, z, lg):
    lb = _sigmoid(lg[0:1, :] - lg[1:2, :])
    qs, dqs = _silu_and_grad(q_in)
    sz = _sigmoid(z)
    f = lb + (1.0 - lb) * sz
    k = (1.0 - lb) * (1.0 - sz)
    c = HGRN_CHUNK
    r = lax.broadcasted_iota(jnp.int32, (c, c), 0)
    s = lax.broadcasted_iota(jnp.int32, (c, c), 1)
    causal = r >= s
    b = _tri_dot(jnp.where(causal, 1.0, 0.0).astype(BF16), jnp.log(f))
    b_end = b[c - 1:c, :]
    b_mid = b[c // 2 - 1:c // 2, :]
    e_q, e_k, e_0, e_c = jnp.exp(b - b_mid), jnp.exp(b_mid - b), jnp.exp(b), jnp.exp(b_end - b)
    return dict(lb=lb, qs=qs, dqs=dqs, sz=sz, f=f, k=k, causal=causal, b_end=b_end,
                e_q=e_q, e_k=e_k, e_0=e_0, e_c=e_c,
                qt=qs * e_q, kt=k * e_k, q0=qs * e_0, kc=k * e_c)


def _hgrn_fwd(proj, logits, ng):
    t = proj.shape[0]
    c, dh = HGRN_CHUNK, HGRN_DH
    n_chunks = t // c
    subs = HGRN_SUBS if n_chunks % HGRN_SUBS == 0 else 1

    def head(h, sub, q_ref, z_ref, v_ref, g_ref, lg_ref, ng_ref, o_ref, oh_ref, s0_ref, st):
        sl = slice(h * dh, (h + 1) * dh)
        rs = slice(sub * c, (sub + 1) * c)
        s0 = st[h]
        s0_ref[h, sub] = s0
        p = _hgrn_pre(q_ref[rs, sl], z_ref[rs, sl], lg_ref[:, sl])
        v16 = v_ref[rs, sl].astype(BF16)
        a = jnp.where(p["causal"], _dot_nt(p["qt"].astype(BF16), p["kt"].astype(BF16)), 0.0)
        o = _dot_nt(p["q0"].astype(BF16), s0.astype(BF16)) + _dot(a.astype(BF16), v16)
        st[h] = jnp.exp(p["b_end"]) * s0 + _dot_tn(v16, p["kc"].astype(BF16))
        o_ref[rs, sl] = o
        rn = lax.rsqrt(jnp.mean(o * o, axis=-1, keepdims=True) + RMS_EPS)
        oh_ref[rs, sl] = (o * rn * ng_ref[:, sl] * _silu_and_grad(g_ref[rs, sl])[0]).astype(BF16)

    def body(*refs):
        st = refs[-1]

        @pl.when(pl.program_id(0) == 0)
        def _():
            st[...] = jnp.zeros_like(st)

        for sub in range(subs):
            for h in range(HGRN_HEADS):
                head(h, sub, *refs)

    def wide(off):
        return pl.BlockSpec((subs * c, HGRN_WIDTH), lambda i: (i, off))

    return pl.pallas_call(
        body, name="hgrn_fwd", grid=(n_chunks // subs,),
        in_specs=[wide(1), wide(2), wide(3), wide(4),
                  pl.BlockSpec((2, HGRN_WIDTH), lambda i: (0, 0)), pl.BlockSpec((1, HGRN_WIDTH), lambda i: (0, 0))],
        out_specs=[wide(0), wide(0), pl.BlockSpec((HGRN_HEADS, subs, dh, dh), lambda i: (0, i, 0, 0))],
        out_shape=[jax.ShapeDtypeStruct((t, HGRN_WIDTH), F32), jax.ShapeDtypeStruct((t, HGRN_WIDTH), BF16),
                   jax.ShapeDtypeStruct((HGRN_HEADS, n_chunks, dh, dh), F32)],
        scratch_shapes=[pltpu.VMEM((HGRN_HEADS, dh, dh), F32)],
        compiler_params=_params(("arbitrary",)),
    )(proj, proj, proj, proj, logits, ng)


def _hgrn_bwd(proj, o_raw, s0s, doh, logits, ng):
    t = proj.shape[0]
    c, dh = HGRN_CHUNK, HGRN_DH
    n_chunks = t // c
    subs = HGRN_SUBS if n_chunks % HGRN_SUBS == 0 else 1
    last = n_chunks // subs - 1

    def head(h, sub, q_ref, z_ref, v_ref, g_ref, o_ref, s0_ref, doh_ref, lg_ref, ng_ref,
             dq_ref, dz_ref, dv_ref, dg_ref, dng_ref, dlb_ref, dst):
        sl = slice(h * dh, (h + 1) * dh)
        rs = slice(sub * c, (sub + 1) * c)
        p = _hgrn_pre(q_ref[rs, sl], z_ref[rs, sl], lg_ref[:, sl])
        v = v_ref[rs, sl]
        v16 = v.astype(BF16)
        s0 = s0_ref[h, sub]
        ds_end = dst[h]
        ds16 = ds_end.astype(BF16)
        ngv = ng_ref[:, sl]

        o = o_ref[rs, sl]
        dohv = doh_ref[rs, sl]
        sg, dsg = _silu_and_grad(g_ref[rs, sl])
        rn = lax.rsqrt(jnp.mean(o * o, axis=-1, keepdims=True) + RMS_EPS)
        oh = o * rn
        dg_ref[rs, sl] = (dohv * oh * ngv * dsg).astype(BF16)
        don = dohv * sg
        dng_ref[:, sl] += jnp.sum(don * oh, axis=0, keepdims=True)
        doh_n = don * ngv
        do = rn * (doh_n - oh * jnp.mean(doh_n * oh, axis=-1, keepdims=True))
        do16 = do.astype(BF16)

        qt16, kt16, q016, kc16 = (p[n].astype(BF16) for n in ("qt", "kt", "q0", "kc"))
        a = jnp.where(p["causal"], _dot_nt(qt16, kt16), 0.0)
        da = jnp.where(p["causal"], _dot_nt(do16, v16), 0.0)
        da16 = da.astype(BF16)
        dqt = _dot(da16, kt16)
        dq0 = _dot(do16, s0.astype(BF16))
        dkt = _dot_tn(da16, qt16)
        dkc = _dot(v16, ds16)
        dv_ref[rs, sl] = (_dot_tn(a.astype(BF16), do16) + _dot_nt(kc16, ds16)).astype(BF16)
        lam_end = jnp.exp(p["b_end"])
        dst[h] = lam_end * ds_end + _dot_tn(do16, q016)

        qt, kt, q0, kc = (a.astype(F32) for a in (qt16, kt16, q016, kc16))
        db = dqt * qt + dq0 * q0 - dkt * kt - dkc * kc
        db_end = (jnp.sum(dkc * kc, axis=0, keepdims=True)
                  + jnp.sum(ds_end * s0, axis=0, keepdims=True) * lam_end)
        rowi = lax.broadcasted_iota(jnp.int32, (c, dh), 0)
        db = db + jnp.where(rowi == c - 1, db_end, 0.0)
        r = lax.broadcasted_iota(jnp.int32, (c, c), 0)
        s = lax.broadcasted_iota(jnp.int32, (c, c), 1)
        dlf = _tri_dot(jnp.where(s >= r, 1.0, 0.0).astype(BF16), db)

        dqs = dqt * p["e_q"] + dq0 * p["e_0"]
        dq_ref[rs, sl] = (dqs * p["dqs"]).astype(BF16)
        dk = dkt * p["e_k"] + dkc * p["e_c"]
        sz, lb = p["sz"], p["lb"]
        common = dlf / p["f"] - dk
        dz_ref[rs, sl] = ((1.0 - lb) * sz * (1.0 - sz) * common).astype(BF16)
        dlb_ref[:, sl] += jnp.sum((1.0 - sz) * common, axis=0, keepdims=True)

    def body(*refs):
        dng_ref, dlb_ref, dst = refs[-3:]

        @pl.when(pl.program_id(0) == 0)
        def _():
            dst[...] = jnp.zeros_like(dst)
            dng_ref[...] = jnp.zeros_like(dng_ref)
            dlb_ref[...] = jnp.zeros_like(dlb_ref)

        for sub in reversed(range(subs)):
            for h in range(HGRN_HEADS):
                head(h, sub, *refs)

    def wide(off):
        return pl.BlockSpec((subs * c, HGRN_WIDTH), lambda i: (last - i, off))

    vec = pl.BlockSpec((1, HGRN_WIDTH), lambda i: (0, 0))
    act = jax.ShapeDtypeStruct((t, HGRN_WIDTH), BF16)
    vsh = jax.ShapeDtypeStruct((1, HGRN_WIDTH), F32)
    return pl.pallas_call(
        body, name="hgrn_bwd", grid=(n_chunks // subs,),
        in_specs=[wide(1), wide(2), wide(3), wide(4), wide(0),
                  pl.BlockSpec((HGRN_HEADS, subs, dh, dh), lambda i: (0, last - i, 0, 0)),
                  wide(0), pl.BlockSpec((2, HGRN_WIDTH), lambda i: (0, 0)), vec],
        out_specs=[wide(0), wide(0), wide(0), wide(0), vec, vec],
        out_shape=[act, act, act, act, vsh, vsh],
        scratch_shapes=[pltpu.VMEM((HGRN_HEADS, dh, dh), F32)],
        compiler_params=_params(("arbitrary",)),
    )(proj, proj, proj, proj, o_raw, s0s, doh, logits, ng)


def _lb_bwd(logits, dlb):
    def body(lg_ref, d_ref, o_ref):
        lg = lg_ref[...]
        lb = _sigmoid(lg[0:1, :] - lg[1:2, :])
        g = d_ref[...] * lb * (1.0 - lb)
        o_ref[0:1, :] = g
        o_ref[1:2, :] = -g

    return pl.pallas_call(body, name="hgrn_lb_bwd", out_shape=jax.ShapeDtypeStruct(logits.shape, F32),
                          compiler_params=_params())(logits, dlb)


MERGE_TC = 1024
GS_BLOCK = (S5_WIDTH + 4 * HGRN_WIDTH) // MERGE_TC
GH_BLOCK = GS_BLOCK + D_MODEL // MERGE_TC


def _merge_fwd(proj, ys, yh):
    t = proj.shape[0]
    tr = _pick(t, 512, SUBLANE)

    def body(gs_ref, gh_ref, ys_ref, yh_ref, m_ref):
        m_ref[...] = (_sigmoid(gs_ref[...]) * ys_ref[...] + _sigmoid(gh_ref[...]) * yh_ref[...]).astype(BF16)

    blk = pl.BlockSpec((tr, MERGE_TC), lambda i, j: (i, j))
    return pl.pallas_call(
        body, name="merge_fwd", grid=(t // tr, D_MODEL // MERGE_TC),
        in_specs=[pl.BlockSpec((tr, MERGE_TC), lambda i, j: (i, GS_BLOCK + j)),
                  pl.BlockSpec((tr, MERGE_TC), lambda i, j: (i, GH_BLOCK + j)), blk, blk],
        out_specs=blk, out_shape=jax.ShapeDtypeStruct((t, D_MODEL), BF16),
        compiler_params=_params(("parallel", "parallel")),
    )(proj, proj, ys, yh)


def _merge_bwd(proj, ys, yh, dm, after=None):
    t = proj.shape[0]
    tr = _pick(t, 512, SUBLANE)

    def body(gs_ref, gh_ref, ys_ref, yh_ref, dm_ref, dys_ref, dyh_ref, dgs_ref, dgh_ref):
        dmv = dm_ref[...]
        ss, sh = _sigmoid(gs_ref[...]), _sigmoid(gh_ref[...])
        dys_ref[...] = (dmv * ss).astype(BF16)
        dyh_ref[...] = (dmv * sh).astype(BF16)
        dgs_ref[...] = (dmv * ys_ref[...] * ss * (1.0 - ss)).astype(BF16)
        dgh_ref[...] = (dmv * yh_ref[...] * sh * (1.0 - sh)).astype(BF16)

    blk = pl.BlockSpec((tr, MERGE_TC), lambda i, j: (i, j))
    sh16 = jax.ShapeDtypeStruct((t, D_MODEL), BF16)
    in_specs = [pl.BlockSpec((tr, MERGE_TC), lambda i, j: (i, GS_BLOCK + j)),
                pl.BlockSpec((tr, MERGE_TC), lambda i, j: (i, GH_BLOCK + j)), blk, blk, blk]
    body, in_specs, args = _ordered(body, in_specs, [proj, proj, ys, yh, dm], after)
    return pl.pallas_call(
        body, name="merge_bwd", grid=(t // tr, D_MODEL // MERGE_TC),
        in_specs=in_specs,
        out_specs=[blk, blk, blk, blk], out_shape=[sh16, sh16, sh16, sh16],
        compiler_params=_params(("parallel", "parallel")),
    )(*args)


FFN_TC = 128
FFN_ROWS = 128
HALO = SUBLANE


def _pad_rows(dst, src_ref):
    t, c = src_ref.shape
    dst[0:HALO, :] = jnp.zeros((HALO, c), F32)
    dst[HALO:HALO + t, :] = src_ref[...]
    dst[HALO + t:HALO + t + HALO, :] = jnp.zeros((HALO, c), F32)


def _conv3(padded, w, b, r0, nrows):
    x0 = padded[HALO + r0:HALO + r0 + nrows, :]
    x1 = padded[HALO + r0 - 1:HALO + r0 - 1 + nrows, :]
    x2 = padded[HALO + r0 - 2:HALO + r0 - 2 + nrows, :]
    return b + w[0:1, :] * x2 + w[1:2, :] * x1 + w[2:3, :] * x0, (x0, x1, x2)


def _ffn_act_fwd(up, cw, cb):
    t = up.shape[0]
    rows = _pick(t, FFN_ROWS, SUBLANE)
    nvb = D_FF // FFN_TC

    def body(ug_ref, uv_ref, wg_ref, wv_ref, bg_ref, bv_ref, act_ref, pg, pv):
        wg, wv, bg, bv = wg_ref[...], wv_ref[...], bg_ref[...], bv_ref[...]
        _pad_rows(pg, ug_ref)
        _pad_rows(pv, uv_ref)
        for r0 in range(0, t, rows):
            cg, _ = _conv3(pg, wg, bg, r0, rows)
            cv, _ = _conv3(pv, wv, bv, r0, rows)
            act_ref[r0:r0 + rows, :] = (_silu_and_grad(cg)[0] * cv).astype(BF16)

    def colblk(nrow, off):
        return pl.BlockSpec((nrow, FFN_TC), lambda j: (0, off + j))

    return pl.pallas_call(
        body, name="ffn_act_fwd", grid=(nvb,),
        in_specs=[colblk(t, 0), colblk(t, nvb), colblk(3, 0), colblk(3, nvb), colblk(1, 0), colblk(1, nvb)],
        out_specs=colblk(t, 0), out_shape=jax.ShapeDtypeStruct((t, D_FF), BF16),
        scratch_shapes=[pltpu.VMEM((t + 2 * HALO, FFN_TC), F32), pltpu.VMEM((t + 2 * HALO, FFN_TC), F32)],
        compiler_params=_params(("parallel",)),
    )(up, up, cw, cw, cb, cb)


def _ffn_act_bwd(up, dact, cw, cb, after=None):
    t = up.shape[0]
    rows = _pick(t, FFN_ROWS, SUBLANE)
    nvb = D_FF // FFN_TC

    def body(ug_ref, uv_ref, da_ref, wg_ref, wv_ref, bg_ref, bv_ref,
             dug_ref, duv_ref, dwg_ref, dwv_ref, dbg_ref, dbv_ref, pg, pv, dcs):
        wg, wv, bg, bv = wg_ref[...], wv_ref[...], bg_ref[...], bv_ref[...]
        _pad_rows(pg, ug_ref)
        _pad_rows(pv, uv_ref)
        ext = rows + HALO
        acc_g = [jnp.zeros((1, FFN_TC), F32) for _ in range(4)]
        acc_v = [jnp.zeros((1, FFN_TC), F32) for _ in range(4)]
        for r0 in range(0, t, rows):
            cg, xg = _conv3(pg, wg, bg, r0, ext)
            cv, xv = _conv3(pv, wv, bv, r0, ext)
            if r0 + ext <= t:
                dav = da_ref[r0:r0 + ext, :]
            else:
                dav = jnp.concatenate([da_ref[r0:t, :], jnp.zeros((HALO, FFN_TC), F32)], axis=0)
            sg, dsg = _silu_and_grad(cg)
            for h, (dconv, xs, w, acc, out) in enumerate(((dav * cv * dsg, xg, wg, acc_g, dug_ref),
                                                           (dav * sg, xv, wv, acc_v, duv_ref))):
                dcs[h] = dconv
                d0 = dconv[0:rows, :]
                d1 = dcs[h, 1:rows + 1, :]
                d2 = dcs[h, 2:rows + 2, :]
                out[r0:r0 + rows, :] = (w[2:3, :] * d0 + w[1:2, :] * d1 + w[0:1, :] * d2).astype(BF16)
                x0, x1, x2 = xs
                acc[0] = acc[0] + jnp.sum(d0 * x2[0:rows, :], axis=0, keepdims=True)
                acc[1] = acc[1] + jnp.sum(d0 * x1[0:rows, :], axis=0, keepdims=True)
                acc[2] = acc[2] + jnp.sum(d0 * x0[0:rows, :], axis=0, keepdims=True)
                acc[3] = acc[3] + jnp.sum(d0, axis=0, keepdims=True)
        for acc, dw_ref, db_ref in ((acc_g, dwg_ref, dbg_ref), (acc_v, dwv_ref, dbv_ref)):
            dw_ref[0:1, :] = acc[0]
            dw_ref[1:2, :] = acc[1]
            dw_ref[2:3, :] = acc[2]
            db_ref[...] = acc[3]

    def colblk(nrow, off):
        return pl.BlockSpec((nrow, FFN_TC), lambda j: (0, off + j))

    in_specs = [colblk(t, 0), colblk(t, nvb), colblk(t, 0), colblk(3, 0), colblk(3, nvb), colblk(1, 0), colblk(1, nvb)]
    body, in_specs, args = _ordered(body, in_specs, [up, up, dact, cw, cw, cb, cb], after)
    return pl.pallas_call(
        body, name="ffn_act_bwd", grid=(nvb,),
        in_specs=in_specs,
        out_specs=[colblk(t, 0), colblk(t, 0), colblk(3, 0), colblk(3, 0), colblk(1, 0), colblk(1, 0)],
        out_shape=[jax.ShapeDtypeStruct((t, D_FF), BF16), jax.ShapeDtypeStruct((t, D_FF), BF16),
                   jax.ShapeDtypeStruct((3, D_FF), F32), jax.ShapeDtypeStruct((3, D_FF), F32),
                   jax.ShapeDtypeStruct((1, D_FF), F32), jax.ShapeDtypeStruct((1, D_FF), F32)],
        scratch_shapes=[pltpu.VMEM((t + 2 * HALO, FFN_TC), F32), pltpu.VMEM((t + 2 * HALO, FFN_TC), F32),
                        pltpu.VMEM((2, rows + HALO, FFN_TC), F32)],
        compiler_params=_params(("parallel",)),
    )(*args)


def _all_gather(shards, name):
    nw = len(shards)

    def body(*refs):
        x_refs, out_refs = refs[:nw], refs[nw:2 * nw]
        send_sems, recv_sems, local_sems = refs[2 * nw:]
        x, y, c = lax.axis_index("x"), lax.axis_index("y"), lax.axis_index("c")
        me, sibling = (x, y, c), (x, y, 1 - c)
        chips = [(1 - x, y), (x, 1 - y), (1 - x, 1 - y)]

        def copy(w, k, block, to, src=None):
            slot = out_refs[w].at[4 * block[0] + 2 * block[1] + block[2]]
            return pltpu.make_async_remote_copy(
                src_ref=slot if src is None else src, dst_ref=slot,
                send_sem=send_sems.at[w, k], recv_sem=recv_sems.at[w, k],
                device_id=to, device_id_type=MESH)

        mine, first, passed = [], [], []
        for w in range(nw):
            cp = pltpu.make_async_copy(x_refs[w], out_refs[w].at[4 * x + 2 * y + c], local_sems.at[w])
            cp.start()
            mine.append(cp)
            first.append(copy(w, 0, me, sibling, src=x_refs[w]))
            first += [copy(w, 1 + j, me, (*chip, c), src=x_refs[w]) for j, chip in enumerate(chips)]
        for cp in first:
            cp.start()
        for w in range(nw):
            for j, chip in enumerate(chips):
                copy(w, 1 + j, (*chip, c), me).wait_recv()
                fwd = copy(w, 4 + j, (*chip, c), sibling)
                fwd.start()
                passed.append(fwd)
        for w in range(nw):
            copy(w, 0, sibling, me).wait_recv()
            for j, chip in enumerate(chips):
                copy(w, 4 + j, (*chip, 1 - c), me).wait_recv()
        for cp in first + passed:
            cp.wait_send()
        for cp in mine:
            cp.wait()

    anyspec = pl.BlockSpec(memory_space=pl.ANY)
    return pl.pallas_call(
        body, name=name,
        in_specs=[anyspec] * nw, out_specs=[anyspec] * nw,
        out_shape=[jax.ShapeDtypeStruct((N_DEV,) + s.shape, s.dtype) for s in shards],
        scratch_shapes=[pltpu.SemaphoreType.DMA((nw, 7)), pltpu.SemaphoreType.DMA((nw, 7)),
                        pltpu.SemaphoreType.DMA((nw,))],
    )(*shards)


HBM_SPEC = pl.BlockSpec(memory_space=pltpu.HBM)
SEM_SPEC = pl.BlockSpec(memory_space=pltpu.SEMAPHORE)
ANY_SPEC = pl.BlockSpec(memory_space=pl.ANY)
DATAFLOW = pltpu.SideEffectType.DATAFLOW_SIDE_EFFECTING


def _my_index():
    return 4 * lax.axis_index("x") + 2 * lax.axis_index("y") + lax.axis_index("c")


def _peers():
    x, y, c = lax.axis_index("x"), lax.axis_index("y"), lax.axis_index("c")
    peers = []
    for k in range(1, N_DEV):
        px = 1 - x if k & 4 else x
        py = 1 - y if k & 2 else y
        pc = 1 - c if k & 1 else c
        peers.append((k, (px, py, pc), 4 * px + 2 * py + pc))
    return peers


def _split_copy(src_ref, land_ref, send_sems, recv_sems, w, k, peer, slot, scatter, outgoing):
    return pltpu.make_async_remote_copy(
        src_ref=src_ref.at[slot] if scatter else src_ref,
        dst_ref=land_ref.at[_my_index() if outgoing else slot],
        send_sem=send_sems.at[w * (N_DEV - 1) + k - 1], recv_sem=recv_sems.at[w * (N_DEV - 1) + k - 1],
        device_id=peer, device_id_type=MESH)


def _landing_zone(src, scatter):
    me = _my_index()
    own = lax.dynamic_index_in_dim(src, me, 0, keepdims=True) if scatter else src[None]
    shape = src.shape if scatter else (N_DEV,) + src.shape
    return lax.dynamic_update_slice_in_dim(lax.empty(shape, src.dtype), own, me, 0)


def _exchange_start(srcs, scatter, after, name, lands=None):
    nw = len(srcs)
    if lands is None:
        lands = [_landing_zone(s, scatter) for s in srcs]

    afters = [] if after is None else [after]

    def body(*refs):
        s_refs, l_refs = refs[:nw], refs[nw:2 * nw]
        send_sems, recv_sems = refs[2 * nw + len(afters)], refs[2 * nw + len(afters) + 1]
        token = refs[-1]
        for w in range(nw):
            for k, peer, slot in _peers():
                _split_copy(s_refs[w], l_refs[w], send_sems, recv_sems, w, k, peer, slot, scatter, True).start()
        token[...] = jnp.zeros_like(token)

    sems = pltpu.SemaphoreType.DMA((nw * (N_DEV - 1),))
    outs = pl.pallas_call(
        body, name=name,
        out_shape=(sems, sems, *[pltpu.HBM(a.shape, a.dtype) for a in (*srcs, *lands)],
                   jax.ShapeDtypeStruct((SUBLANE, LANE), F32)),
        in_specs=[HBM_SPEC] * (2 * nw) + [ANY_SPEC] * len(afters),
        out_specs=(SEM_SPEC, SEM_SPEC, *[HBM_SPEC] * (2 * nw), pl.BlockSpec(memory_space=pltpu.VMEM)),
        input_output_aliases={i: 2 + i for i in range(2 * nw)},
        compiler_params=pltpu.CompilerParams(has_side_effects=DATAFLOW),
    )(*[pltpu.with_memory_space_constraint(a, pltpu.HBM) for a in (*srcs, *lands)], *afters)
    return dict(sems=outs[:2], srcs=outs[2:2 + nw], lands=outs[2 + nw:2 + 2 * nw], token=outs[-1], scatter=scatter)


def _exchange_wait(handle, afters, name, which=None):
    srcs, lands, scatter = handle["srcs"], handle["lands"], handle["scatter"]
    nw = len(srcs)
    which = list(range(nw)) if which is None else list(which)

    def body(*refs):
        s_refs, l_refs = refs[:nw], refs[nw:2 * nw]
        send_sems, recv_sems = refs[2 * nw], refs[2 * nw + 1]
        for w in which:
            for k, peer, slot in _peers():
                cp = _split_copy(s_refs[w], l_refs[w], send_sems, recv_sems, w, k, peer, slot, scatter, False)
                cp.wait_send()
                cp.wait_recv()

    outs = pl.pallas_call(
        body, name=name,
        out_shape=tuple(pltpu.HBM(a.shape, a.dtype) for a in (*srcs, *lands)),
        in_specs=[HBM_SPEC] * (2 * nw) + [SEM_SPEC, SEM_SPEC] + [ANY_SPEC] * len(afters),
        out_specs=tuple([HBM_SPEC] * (2 * nw)),
        input_output_aliases={i: i for i in range(2 * nw)},
        compiler_params=pltpu.CompilerParams(has_side_effects=DATAFLOW),
    )(*srcs, *lands, *handle["sems"], *afters)
    handle["srcs"], handle["lands"] = list(outs[:nw]), list(outs[nw:])
    return [outs[nw + w] for w in which]


def _chips_and_sibling():
    x, y, c = lax.axis_index("x"), lax.axis_index("y"), lax.axis_index("c")
    return [(1 - x, y), (x, 1 - y), (1 - x, 1 - y)], (x, y, 1 - c), c


def _slot(px, py, pc):
    return 4 * px + 2 * py + pc


def _two_level_start(shards, name):
    nw = len(shards)
    lands = [_landing_zone(s, False) for s in shards]

    def body(*refs):
        s_refs, l_refs = refs[:nw], refs[nw:2 * nw]
        send_sems, recv_sems, token = refs[2 * nw], refs[2 * nw + 1], refs[-1]
        chips, sibling, c = _chips_and_sibling()
        for w in range(nw):
            for k, to in enumerate([sibling] + [(*chip, c) for chip in chips]):
                pltpu.make_async_remote_copy(
                    src_ref=s_refs[w], dst_ref=l_refs[w].at[_my_index()],
                    send_sem=send_sems.at[4 * w + k], recv_sem=recv_sems.at[4 * w + k],
                    device_id=to, device_id_type=MESH).start()
        token[...] = jnp.zeros_like(token)

    sems = pltpu.SemaphoreType.DMA((4 * nw,))
    outs = pl.pallas_call(
        body, name=name,
        out_shape=(sems, sems, *[pltpu.HBM(a.shape, a.dtype) for a in (*shards, *lands)],
                   jax.ShapeDtypeStruct((SUBLANE, LANE), F32)),
        in_specs=[HBM_SPEC] * (2 * nw),
        out_specs=(SEM_SPEC, SEM_SPEC, *[HBM_SPEC] * (2 * nw), pl.BlockSpec(memory_space=pltpu.VMEM)),
        input_output_aliases={i: 2 + i for i in range(2 * nw)},
        compiler_params=pltpu.CompilerParams(has_side_effects=DATAFLOW),
    )(*[pltpu.with_memory_space_constraint(a, pltpu.HBM) for a in (*shards, *lands)])
    return dict(sems=outs[:2], srcs=outs[2:2 + nw], lands=outs[2 + nw:2 + 2 * nw], token=outs[-1])


def _two_level_pass(handle, afters, name):
    srcs, lands = handle["srcs"], handle["lands"]
    nw = len(srcs)

    def body(*refs):
        s_refs, l_refs = refs[:nw], refs[nw:2 * nw]
        send_a, recv_a = refs[2 * nw], refs[2 * nw + 1]
        send_b, recv_b = refs[2 * nw + 2 + len(afters)], refs[2 * nw + 3 + len(afters)]
        chips, sibling, c = _chips_and_sibling()
        for w in range(nw):
            for j, chip in enumerate(chips):
                landed = l_refs[w].at[_slot(*chip, c)]
                pltpu.make_async_remote_copy(
                    src_ref=s_refs[w], dst_ref=landed, send_sem=send_a.at[4 * w + 1 + j], recv_sem=recv_a.at[4 * w + 1 + j],
                    device_id=(*chip, c), device_id_type=MESH).wait_recv()
                pltpu.make_async_remote_copy(
                    src_ref=landed, dst_ref=landed, send_sem=send_b.at[3 * w + j], recv_sem=recv_b.at[3 * w + j],
                    device_id=sibling, device_id_type=MESH).start()

    sems = pltpu.SemaphoreType.DMA((3 * nw,))
    outs = pl.pallas_call(
        body, name=name,
        out_shape=(sems, sems, *[pltpu.HBM(a.shape, a.dtype) for a in (*srcs, *lands)]),
        in_specs=[HBM_SPEC] * (2 * nw) + [SEM_SPEC, SEM_SPEC] + [ANY_SPEC] * len(afters),
        out_specs=(SEM_SPEC, SEM_SPEC, *[HBM_SPEC] * (2 * nw)),
        input_output_aliases={i: 2 + i for i in range(2 * nw)},
        compiler_params=pltpu.CompilerParams(has_side_effects=DATAFLOW),
    )(*srcs, *lands, *handle["sems"], *afters)
    return dict(sems=handle["sems"], sems_pass=outs[:2], srcs=outs[2:2 + nw], lands=outs[2 + nw:2 + 2 * nw])


def _two_level_wait(handle, name):
    srcs, lands = handle["srcs"], handle["lands"]
    nw = len(srcs)

    def body(*refs):
        s_refs, l_refs = refs[:nw], refs[nw:2 * nw]
        send_a, recv_a, send_b, recv_b = refs[2 * nw:2 * nw + 4]
        chips, sibling, c = _chips_and_sibling()
        x, y = sibling[0], sibling[1]
        for w in range(nw):
            first = pltpu.make_async_remote_copy(
                src_ref=s_refs[w], dst_ref=l_refs[w].at[_slot(x, y, 1 - c)], send_sem=send_a.at[4 * w],
                recv_sem=recv_a.at[4 * w], device_id=sibling, device_id_type=MESH)
            first.wait_send()
            first.wait_recv()
            for j, chip in enumerate(chips):
                pltpu.make_async_remote_copy(
                    src_ref=s_refs[w], dst_ref=l_refs[w].at[_slot(*chip, c)], send_sem=send_a.at[4 * w + 1 + j],
                    recv_sem=recv_a.at[4 * w + 1 + j], device_id=(*chip, c), device_id_type=MESH).wait_send()
                passed = pltpu.make_async_remote_copy(
                    src_ref=l_refs[w].at[_slot(*chip, c)], dst_ref=l_refs[w].at[_slot(*chip, 1 - c)],
                    send_sem=send_b.at[3 * w + j], recv_sem=recv_b.at[3 * w + j], device_id=sibling, device_id_type=MESH)
                passed.wait_send()
                passed.wait_recv()

    outs = pl.pallas_call(
        body, name=name,
        out_shape=tuple(pltpu.HBM(a.shape, a.dtype) for a in (*srcs, *lands)),
        in_specs=[HBM_SPEC] * (2 * nw) + [SEM_SPEC] * 4,
        out_specs=tuple([HBM_SPEC] * (2 * nw)),
        input_output_aliases={i: i for i in range(2 * nw)},
        compiler_params=pltpu.CompilerParams(has_side_effects=DATAFLOW),
    )(*srcs, *lands, *handle["sems"], *handle["sems_pass"])
    return list(outs[nw:])


def _adamw(w, g, m, v):
    m = ADAM_B1 * m + (1.0 - ADAM_B1) * g
    v = ADAM_B2 * v + (1.0 - ADAM_B2) * (g * g)
    m_hat = m / (1.0 - ADAM_B1 ** ADAM_STEP)
    v_hat = v / (1.0 - ADAM_B2 ** ADAM_STEP)
    delta = -ADAM_LR * (m_hat / (jnp.sqrt(v_hat) + ADAM_EPS) + ADAM_WD * w)
    return delta, m, v


def _sum_adam(parts, w, m, v, name):
    _, r, c = parts.shape
    tr = _pick(r, 256, 16)

    def body(p_ref, w_ref, m_ref, v_ref, g_ref, d_ref, mo_ref, vo_ref):
        g = p_ref[0].astype(F32)
        for s in range(1, N_DEV):
            g = g + p_ref[s].astype(F32)
        g_ref[...] = g
        d_ref[...], mo_ref[...], vo_ref[...] = _adamw(w_ref[...], g, m_ref[...], v_ref[...])

    row = pl.BlockSpec((tr, c), lambda i: (i, 0))
    sh = jax.ShapeDtypeStruct((r, c), F32)
    return pl.pallas_call(
        body, name=name, grid=(r // tr,),
        in_specs=[pl.BlockSpec((N_DEV, tr, c), lambda i: (0, i, 0)), row, row, row],
        out_specs=[row, row, row, row], out_shape=[sh, sh, sh, sh],
        compiler_params=_params(("parallel",)),
    )(parts, w, m, v)


def _sum_slots(parts, name):
    _, r, c = parts.shape
    tr = _pick(r, 512, SUBLANE)

    def body(p_ref, o_ref):
        g = p_ref[0]
        for s in range(1, N_DEV):
            g = g + p_ref[s]
        o_ref[...] = g

    return pl.pallas_call(
        body, name=name, grid=(r // tr,),
        in_specs=[pl.BlockSpec((N_DEV, tr, c), lambda i: (0, i, 0))],
        out_specs=pl.BlockSpec((tr, c), lambda i: (i, 0)), out_shape=jax.ShapeDtypeStruct((r, c), F32),
        compiler_params=_params(("parallel",)),
    )(parts)


def _adam_rows(g, w, m, v, name):
    r, c = g.shape
    tr = _pick(r, 512, SUBLANE)

    def body(g_ref, w_ref, m_ref, v_ref, d_ref, mo_ref, vo_ref):
        d_ref[...], mo_ref[...], vo_ref[...] = _adamw(w_ref[...], g_ref[...], m_ref[...], v_ref[...])

    row = pl.BlockSpec((tr, c), lambda i: (i, 0))
    sh = jax.ShapeDtypeStruct((r, c), F32)
    return pl.pallas_call(body, name=name, grid=(r // tr,), in_specs=[row] * 4, out_specs=[row] * 3,
                          out_shape=[sh, sh, sh], compiler_params=_params(("parallel",)))(g, w, m, v)


def _pack(arrays):
    flat = jnp.concatenate([a.reshape(-1).astype(F32) for a in arrays])
    pad = (-flat.shape[0]) % (SUBLANE * LANE)
    return jnp.pad(flat, (0, pad)).reshape(-1, LANE)


def _unpack(packed, shapes):
    flat = packed.reshape(-1)
    out, off = [], 0
    for s in shapes:
        n = math.prod(s)
        out.append(flat[off:off + n].reshape(s))
        off += n
    return out


def _block_diag(t):
    eye = jnp.eye(S5_SUPER, dtype=bool)
    bd = jnp.where(eye[None, :, None, :, None], t[:, :, :, None, :], 0.0)
    return bd.reshape(S5_SUPER, S5_SUPER * t.shape[2], S5_SUPER * t.shape[3])


def _diag_blocks(dense, a, b):
    x = dense.reshape(S5_SUPER, S5_SUPER, a, S5_SUPER, b)
    return jnp.moveaxis(jnp.diagonal(x, axis1=1, axis2=3), -1, 1)


def _s5_layouts(b_re, b_im, c_re, c_im, d):
    g2 = (S5_GROUPS // S5_SUPER, S5_SUPER)
    bt = lambda b: _block_diag(b.reshape(*g2, S5_STATE, S5_GROUP).transpose(0, 1, 3, 2))
    ct = lambda c: _block_diag(c.reshape(*g2, S5_GROUP, S5_STATE).transpose(0, 1, 3, 2))
    bsg = jnp.concatenate([bt(b_re), bt(b_im)], axis=2).astype(BF16)
    ccat = jnp.concatenate([ct(c_re), -ct(c_im)], axis=1).astype(BF16)
    return bsg, ccat, d.reshape(1, S5_WIDTH)


def _s5_param_grads(gb, gc):
    n = S5_LANES
    gb_re = _diag_blocks(gb[:, :, 0:n], S5_GROUP, S5_STATE).transpose(0, 1, 3, 2).reshape(S5_GROUPS, S5_STATE, S5_GROUP)
    gb_im = _diag_blocks(gb[:, :, n:2 * n], S5_GROUP, S5_STATE).transpose(0, 1, 3, 2).reshape(S5_GROUPS, S5_STATE, S5_GROUP)
    gc_re = _diag_blocks(gc[:, 0:n, :], S5_STATE, S5_GROUP).transpose(0, 1, 3, 2).reshape(S5_GROUPS, S5_GROUP, S5_STATE)
    gc_im = -_diag_blocks(gc[:, n:2 * n, :], S5_STATE, S5_GROUP).transpose(0, 1, 3, 2).reshape(S5_GROUPS, S5_GROUP, S5_STATE)
    return gb_re, gb_im, gc_re, gc_im


def _local_step(x, target, weight, emit, small, after=None):
    sp = small
    a_re, a_im = sp["s5_a_re"], sp["s5_a_im"]
    ldt = sp["s5_log_dt"].reshape(S5_GROUPS, 1)

    h1 = _rms_fwd(x, sp["ln_mix_g"], "rms_mix", after=after)
    w_in = weight("w_in", h1)
    proj = _mm_nn(h1, w_in, "mm_in", after=weight("after_w_in", None))
    conv_w = weight("conv_w", None)
    disc = _s5_param_fwd(a_re, a_im, ldt)
    bsg, ccat, d_row = sp["s5_layouts"]
    abar_t, coef_t = _s5_to_tile(disc[0], disc[1]), _s5_to_tile(disc[2], disc[3])
    y, sb = _s5_fwd(proj, bsg, ccat, d_row, abar_t, coef_t)
    z16 = _gelu_fwd(y)
    w_glu = weight("s5_w_glu", z16)
    gl = _mm_nn(z16, w_glu, "mm_glu")
    z2 = _glu_fwd(y, gl, sp["s5_b_glu"])
    w_ps = weight("w_proj_s5", z2)
    ys = _mm_nn(z2, w_ps, "mm_proj_s5")
    o_raw, oh, s0s = _hgrn_fwd(proj, sp["hgrn_lb_logits"], sp["hgrn_norm_g"])
    w_ph = weight("w_proj_hgrn", oh)
    yh = _mm_nn(oh, w_ph, "mm_proj_hgrn")
    merged = _merge_fwd(proj, ys, yh)
    w_out = weight("w_out", merged)
    x1 = _mm_nn(merged, w_out, "mm_out", res=x)
    h2 = _rms_fwd(x1, sp["ln_ffn_g"], "rms_ffn")
    w_up = weight("w_up", h2)
    up = _mm_nn(h2, w_up, "mm_up")
    act = _ffn_act_fwd(up, conv_w, sp["conv_b"])
    w_down = weight("w_down", act)
    x2 = _mm_nn(act, w_down, "mm_down", res=x1)
    dx2, dx2_16, g_ln_final, loss = _loss_head(x2, sp["ln_final_g"], target)

    dact = _mm_nt(dx2_16, w_down, "mm_down_dx")
    tok = emit("w_down", _mm_tn(act, dx2_16, 1, "mm_down_dw"))
    dup_g, dup_v, dcw_g, dcw_v, dcb_g, dcb_v = _ffn_act_bwd(up, dact, conv_w, sp["conv_b"], after=tok)
    dup = jnp.concatenate([dup_g, dup_v], axis=1)
    g_conv_w = jnp.concatenate([dcw_g, dcw_v], axis=1)
    g_conv_b = jnp.concatenate([dcb_g, dcb_v], axis=1)
    dh2 = _mm_nt(dup, w_up, "mm_up_dx")
    tok = emit("w_up", _mm_tn(h2, dup, N_DEV, "mm_up_dw"))
    dx1, dx1_16, g_ln_ffn = _rms_bwd(x1, sp["ln_ffn_g"], dh2, dx2, "rms_ffn_bwd", True, after=tok)

    dmerged = _mm_nt(dx1_16, w_out, "mm_out_dx")
    tok = emit("w_out", _mm_tn(merged, dx1_16, 1, "mm_out_dw"))
    dys, dyh, dgs, dgh = _merge_bwd(proj, ys, yh, dmerged, after=tok)
    doh = _mm_nt(dyh, w_ph, "mm_proj_hgrn_dx")
    tok = emit("w_proj_hgrn", _mm_tn(oh, dyh, N_DEV, "mm_proj_hgrn_dw"))
    dz2 = _mm_nt(dys, w_ps, "mm_proj_s5_dx", after=tok)
    tok = emit("w_proj_s5", _mm_tn(z2, dys, N_DEV, "mm_proj_s5_dw"))
    dgl, dza, g_b_glu = _glu_bwd(y, gl, sp["s5_b_glu"], dz2, after=tok)
    dzb = _mm_nt(dgl, w_glu, "mm_glu_dx")
    tok = emit("s5_w_glu", _mm_tn(z16, dgl, 1, "mm_glu_dw"))
    dy = _gelu_bwd(y, dza, dzb, after=tok)
    du, gb, gc, gd, g_abar_t, g_coef_t = _s5_bwd(proj, dy, sb, bsg, ccat, d_row, abar_t, coef_t)
    g_a_re, g_a_im, g_ldt = _s5_param_bwd(a_re, a_im, ldt, [*_s5_from_tile(g_abar_t), *_s5_from_tile(g_coef_t)])
    g_b_re, g_b_im, g_c_re, g_c_im = _s5_param_grads(gb, gc)
    dq, dz, dv, dg, g_norm, dlb = _hgrn_bwd(proj, o_raw, s0s, doh, sp["hgrn_lb_logits"], sp["hgrn_norm_g"])
    g_logits = _lb_bwd(sp["hgrn_lb_logits"], dlb)

    small_g = dict(s5_a_re=g_a_re, s5_a_im=g_a_im, s5_log_dt=g_ldt.reshape(1, S5_GROUPS),
                   s5_b_re=g_b_re, s5_b_im=g_b_im, s5_c_re=g_c_re, s5_c_im=g_c_im,
                   s5_d=gd.reshape(S5_GROUPS, S5_GROUP), s5_b_glu=g_b_glu, hgrn_lb_logits=g_logits,
                   hgrn_norm_g=g_norm, ln_ffn_g=g_ln_ffn, conv_w=g_conv_w, conv_b=g_conv_b, ln_final_g=g_ln_final,
                   loss=loss[0, 0:1])
    tok_small = emit("small", small_g)

    dproj = jnp.concatenate([du, dq, dz, dv, dg, dgs, dgh], axis=1)
    tok = emit("w_in", _mm_tn(h1, dproj, N_DEV, "mm_in_dw", after=tok_small))
    dh1 = _mm_nt(dproj, w_in, "mm_in_dx")
    grad_x, g_ln_mix = _rms_bwd(x, sp["ln_mix_g"], dh1, dx1, "rms_mix_bwd", False, after=tok)
    return grad_x, g_ln_mix


BIG = ("w_in", "s5_w_glu", "w_proj_s5", "w_proj_hgrn", "w_out", "w_up", "w_down")
COL_SHARDED = ("w_in", "w_proj_s5", "w_proj_hgrn", "w_up")
SMALL = ("ln_mix_g", "s5_a_re", "s5_a_im", "s5_log_dt", "s5_b_re", "s5_b_im", "s5_c_re", "s5_c_im", "s5_d",
         "s5_b_glu", "hgrn_lb_logits", "hgrn_norm_g", "ln_ffn_g", "conv_b", "ln_final_g")
WEIGHTS = ("ln_mix_g", "w_in", "s5_a_re", "s5_a_im", "s5_log_dt", "s5_b_re", "s5_b_im", "s5_c_re", "s5_c_im", "s5_d",
           "s5_w_glu", "s5_b_glu", "w_proj_s5", "hgrn_lb_logits", "hgrn_norm_g", "w_proj_hgrn", "w_out", "ln_ffn_g",
           "w_up", "conv_w", "conv_b", "w_down", "ln_final_g")


def kernel(x, ln_mix_g, w_in, s5_a_re, s5_a_im, s5_log_dt, s5_b_re, s5_b_im, s5_c_re, s5_c_im, s5_d, s5_w_glu, s5_b_glu, w_proj_s5, hgrn_lb_logits, hgrn_norm_g, w_proj_hgrn, w_out, ln_ffn_g, w_up, conv_w, conv_b, w_down, ln_final_g, loss_target, m_ln_mix_g, m_w_in, m_s5_a_re, m_s5_a_im, m_s5_log_dt, m_s5_b_re, m_s5_b_im, m_s5_c_re, m_s5_c_im, m_s5_d, m_s5_w_glu, m_s5_b_glu, m_w_proj_s5, m_hgrn_lb_logits, m_hgrn_norm_g, m_w_proj_hgrn, m_w_out, m_ln_ffn_g, m_w_up, m_conv_w, m_conv_b, m_w_down, m_ln_final_g, v_ln_mix_g, v_w_in, v_s5_a_re, v_s5_a_im, v_s5_log_dt, v_s5_b_re, v_s5_b_im, v_s5_c_re, v_s5_c_im, v_s5_d, v_s5_w_glu, v_s5_b_glu, v_w_proj_s5, v_hgrn_lb_logits, v_hgrn_norm_g, v_w_proj_hgrn, v_w_out, v_ln_ffn_g, v_w_up, v_conv_w, v_conv_b, v_w_down, v_ln_final_g):
    given = dict(locals())
    w = {n: given[n] for n in WEIGHTS}
    mom = {n: given["m_" + n] for n in WEIGHTS}
    var = {n: given["v_" + n] for n in WEIGHTS}

    first = _two_level_start([w_in[0].astype(BF16), conv_w[0]], "gather_first_start")
    zero = first["token"][0, 0]
    packed_small = SMALL[1:]
    pw, pm, pv = (_pack([d[n] for n in packed_small]) + zero for d in (w, mom, var))
    layouts = _s5_layouts(s5_b_re[0] + zero, s5_b_im[0], s5_c_re[0] + zero, s5_c_im[0], s5_d[0])
    gather_groups = (("s5_w_glu", "w_proj_s5", "w_proj_hgrn", "w_out"), ("w_up",), ("w_down",))
    shard16 = {n: w[n][0].astype(BF16) + zero.astype(BF16) for g in gather_groups for n in g}
    zones = {n: _landing_zone(s, False) for n, s in shard16.items()}
    pending, ready = {}, {}

    def weight(name, after):
        if "w_in" not in ready:
            local_work = [after, pw, pm, pv, layouts[0], layouts[1], *zones.values()]
            passed = _two_level_pass(first, local_work, "gather_first_pass")
            ready["w_in"], conv_w_all = _two_level_wait(passed, "gather_first_wait")
            ready["conv_w"] = conv_w_all.transpose(1, 0, 2).reshape(3, 2 * D_FF)
            order = [n for group in gather_groups for n in group]
            handle = _exchange_start([shard16[n] for n in order], False, ready["w_in"], "gather_start",
                                     lands=[zones[n] for n in order])
            for i, group in enumerate(gather_groups):
                for n in group:
                    pending[n] = (group, [order.index(m) for m in group], handle, f"gather_wait_{i}")
            ready["after_w_in"] = handle["token"]
        if name not in ready:
            group, which, handle, wait_name = pending[name]
            for n, g in zip(group, _exchange_wait(handle, [after], wait_name, which)):
                ready[n] = g
        g = ready[name]
        return g if name not in BIG or name in COL_SHARDED else g.reshape(1, N_DEV * g.shape[1], g.shape[2])

    scatter_groups = (("w_down",), ("w_up",), ("w_out", "w_proj_hgrn", "w_proj_s5", "s5_w_glu"), ("w_in",))
    emitted, scatters = {}, []
    packed_names = SMALL[1:] + ("conv_w", "loss")

    def emit(name, grad):
        if name == "small":
            emitted[name] = ([grad[n].shape for n in packed_names],
                             _exchange_start([_pack([grad[n] for n in packed_names])], False, None, "small_start"))
            return emitted[name][1]["token"]
        emitted[name] = grad if name in COL_SHARDED else grad.reshape(N_DEV, -1, grad.shape[2])
        group = scatter_groups[len(scatters)]
        if not all(n in emitted for n in group):
            return None
        handle = _exchange_start([emitted[n] for n in group], True, None, f"scatter_start_{len(scatters)}")
        scatters.append((group, handle))
        return handle["token"]

    small = dict(ln_mix_g=ln_mix_g, s5_a_re=s5_a_re[0], s5_a_im=s5_a_im[0], s5_log_dt=s5_log_dt, s5_layouts=layouts,
                 s5_b_glu=s5_b_glu, hgrn_lb_logits=hgrn_lb_logits, hgrn_norm_g=hgrn_norm_g, ln_ffn_g=ln_ffn_g,
                 conv_b=conv_b, ln_final_g=ln_final_g.reshape(1, D_MODEL))
    grad_x, g_ln_mix = _local_step(x[0], loss_target[0], weight, emit, small, after=first["token"])

    shapes, handle = emitted["small"]
    total = _sum_slots(_exchange_wait(handle, [grad_x], "small_wait")[0], "sum_small")
    summed = dict(zip(packed_names, _unpack(total, shapes)))
    mix_all = _all_gather([g_ln_mix.reshape(-1, LANE)], "gather_ln_mix")[0]
    summed["ln_mix_g"] = _sum_slots(mix_all, "sum_ln_mix").reshape(1, D_MODEL)

    grads, delta, new_m, new_v = {}, {}, {}, {}
    afters = [grad_x, total]
    for i, (group, handle) in enumerate(scatters):
        for n, r in zip(group, _exchange_wait(handle, afters, f"scatter_wait_{i}")):
            g, d, m2, v2 = _sum_adam(r, w[n][0], mom[n][0], var[n][0], "adam_" + n)
            grads[n], delta[n], new_m[n], new_v[n] = g[None], d[None], m2[None], v2[None]
        if i == len(scatters) - 2:
            afters = [delta[n] for g2, _ in scatters[:-1] for n in g2]

    d_s, m_s, v_s = _adam_rows(_pack([summed[n] for n in packed_small]), pw, pm, pv, "adam_small")
    wshapes = [w[n].shape for n in packed_small]
    for n, d, m2, v2 in zip(packed_small, _unpack(d_s, wshapes), _unpack(m_s, wshapes), _unpack(v_s, wshapes)):
        grads[n], delta[n], new_m[n], new_v[n] = summed[n].reshape(w[n].shape), d, m2, v2
    grads["ln_mix_g"] = summed["ln_mix_g"]
    delta["ln_mix_g"], new_m["ln_mix_g"], new_v["ln_mix_g"] = _adam_rows(summed["ln_mix_g"], ln_mix_g, m_ln_mix_g,
                                                                         v_ln_mix_g, "adam_ln_mix")
    me = 4 * lax.axis_index("x") + 2 * lax.axis_index("y") + lax.axis_index("c")
    ncol = conv_w.shape[2]
    g_cw = lax.dynamic_slice_in_dim(summed["conv_w"], me * ncol, ncol, axis=1)
    d_cw, m_cw, v_cw = _adam_rows(g_cw, conv_w[0], m_conv_w[0], v_conv_w[0], "adam_conv_w")
    grads["conv_w"], delta["conv_w"], new_m["conv_w"], new_v["conv_w"] = g_cw[None], d_cw[None], m_cw[None], v_cw[None]

    return (summed["loss"].reshape(()), grad_x[None], *[grads[n] for n in WEIGHTS], *[delta[n] for n in WEIGHTS],
            *[new_m[n] for n in WEIGHTS], *[new_v[n] for n in WEIGHTS])
```
